```python
import jax, jax.numpy as jnp
from jax import lax
import numpy as np

D_MODEL = 1024
BATCH = 8
SEQ = 4096
DEPTH = 2

CHUNK = 64
BRANCH_WIDTH = D_MODEL // 2
N_BRANCHES = 3
SB_HEADS = 8
SB_HEAD_DIM = BRANCH_WIDTH // SB_HEADS
SB_BLOCK = 128
SGU_LEN = 128
SGU_GROUPS = 4
SGU_GROUP_DIM = BRANCH_WIDTH // SGU_GROUPS
CONV_WIDTH = 3
MEM_TOKENS = 256
XA_HEADS = 4
XA_HEAD_DIM = D_MODEL // XA_HEADS
FFN_HIDDEN = ((8 * D_MODEL // 3 + 255) // 256) * 256

W_QKV = 3 * BRANCH_WIDTH
W_SGU = 2 * BRANCH_WIDTH
W_CONV = 3 * BRANCH_WIDTH
W_GATES = N_BRANCHES * D_MODEL
IN_COLS = W_QKV + W_SGU + W_CONV + W_GATES
SPLIT_IDX = [BRANCH_WIDTH, 2 * BRANCH_WIDTH, W_QKV,
             W_QKV + W_SGU,
             W_QKV + W_SGU + BRANCH_WIDTH, W_QKV + W_SGU + 2 * BRANCH_WIDTH,
             W_QKV + W_SGU + W_CONV]

kernel_name = "hybrid_stickbreak_gmlp_shortconv_block"


def rms_norm(x, g, eps=1e-6):
    xf = x.astype(jnp.float32)
    y = xf * lax.rsqrt(jnp.mean(xf * xf, axis=-1, keepdims=True) + eps)
    return (y * g.astype(jnp.float32)).astype(x.dtype)


def layer_norm(x, g, b, eps=1e-5):
    xf = x.astype(jnp.float32)
    mu = jnp.mean(xf, axis=-1, keepdims=True)
    xc = xf - mu
    y = xc * lax.rsqrt(jnp.mean(xc * xc, axis=-1, keepdims=True) + eps)
    return (y * g.astype(jnp.float32) + b.astype(jnp.float32)).astype(x.dtype)


def stick_breaking_attention(q, k, v):
    seq = q.shape[2]
    scale = SB_HEAD_DIM ** -0.5
    outs = []
    for i in range(seq // SB_BLOCK):
        q0 = i * SB_BLOCK
        kend = q0 + SB_BLOCK
        qb = q[:, :, q0:kend].astype(jnp.float32)
        kb = k[:, :, :kend].astype(jnp.float32)
        z = jnp.einsum("bhqd,bhkd->bhqk", qb, kb) * scale
        t_pos = q0 + jnp.arange(SB_BLOCK)[:, None]
        s_pos = jnp.arange(kend)[None, :]
        valid = s_pos < t_pos
        log_1m = jnp.where(valid, jax.nn.log_sigmoid(-z), 0.0)
        log_a = jax.nn.log_sigmoid(z) + lax.cumsum(log_1m, axis=3, reverse=True) - log_1m
        a = jnp.where(valid, jnp.exp(log_a), 0.0)
        outs.append(jnp.einsum("bhqk,bhkd->bhqd", a.astype(v.dtype), v[:, :, :kend]))
    return jnp.concatenate(outs, axis=2)


def spatial_gating(z, ln_g, ln_b, w_s, b_s):
    bsz, seq, _ = z.shape
    u, v = jnp.split(z, 2, axis=-1)
    v = layer_norm(v, ln_g, ln_b)
    v = v.reshape(bsz, seq // SGU_LEN, SGU_LEN, SGU_GROUPS, SGU_GROUP_DIM)
    pos = jnp.arange(SGU_LEN)
    allowed = (pos[:, None] // CHUNK) >= (pos[None, :] // CHUNK)
    w = jnp.where(allowed[None], w_s, 0.0).astype(v.dtype)
    vm = jnp.einsum("gts,bnsgc->bntgc", w, v) + b_s.T[None, None, :, :, None].astype(v.dtype)
    return u * vm.reshape(bsz, seq, BRANCH_WIDTH)


def gated_short_conv(gate_b, gate_c, xin, conv_w):
    y = gate_c * xin
    ch = y.shape[-1]
    conv = lax.conv_general_dilated(
        y, conv_w[:, None, :].astype(y.dtype), window_strides=(1,),
        padding=((CONV_WIDTH - 1, 0),), dimension_numbers=("NWC", "WIO", "NWC"),
        feature_group_count=ch)
    return gate_b * conv


def hybrid_mixer(h, w_in, sgu_ln_g, sgu_ln_b, w_spatial, b_spatial, conv_w, w_branch, w_out):
    bsz, seq, _ = h.shape
    p = h @ w_in
    q, k, v, z, cb, cc, cx, gates = jnp.split(p, SPLIT_IDX, axis=-1)

    def heads(t):
        return t.reshape(bsz, seq, SB_HEADS, SB_HEAD_DIM).transpose(0, 2, 1, 3)

    ya = stick_breaking_attention(heads(q), heads(k), heads(v))
    ya = ya.transpose(0, 2, 1, 3).reshape(bsz, seq, BRANCH_WIDTH)
    yb = spatial_gating(jax.nn.gelu(z, approximate=False), sgu_ln_g, sgu_ln_b, w_spatial, b_spatial)
    yc = gated_short_conv(cb, cc, cx, conv_w)

    br = jnp.stack([ya, yb, yc], axis=2)
    br_d = jnp.einsum("bsnc,ncd->bsnd", br, w_branch)
    g = jax.nn.sigmoid(gates.reshape(bsz, seq, N_BRANCHES, D_MODEL))
    merged = jnp.sum(g * br_d, axis=2)
    return merged @ w_out


def memory_cross_attention(h, mem, mem_g, wq, wk, wv, wo):
    bsz, seq, _ = h.shape
    m = rms_norm(mem, mem_g)
    q = (h @ wq).reshape(bsz, seq, XA_HEADS, XA_HEAD_DIM)
    k = (m @ wk).reshape(bsz, MEM_TOKENS, XA_HEADS, XA_HEAD_DIM)
    v = (m @ wv).reshape(bsz, MEM_TOKENS, XA_HEADS, XA_HEAD_DIM)
    s = jnp.einsum("bqhd,bkhd->bhqk", q.astype(jnp.float32), k.astype(jnp.float32)) * (XA_HEAD_DIM ** -0.5)
    pr = jax.nn.softmax(s, axis=-1)
    o = jnp.einsum("bhqk,bkhd->bqhd", pr.astype(v.dtype), v).reshape(bsz, seq, D_MODEL)
    return o @ wo


def swiglu(h, w_gate, w_up, w_down):
    return (jax.nn.silu(h @ w_gate) * (h @ w_up)) @ w_down


def _fwd_setup_inputs(seed: int = 0) -> dict:
    key = jax.random.key(seed)
    ks = jax.random.split(key, 24)
    f32 = jnp.float32

    def nrm(k, shape, fan_in):
        return jax.random.normal(k, shape, f32) * (fan_in ** -0.5)

    def gain(k, shape):
        return 1.0 + 0.02 * jax.random.normal(k, shape, f32)

    L, D, W = DEPTH, D_MODEL, BRANCH_WIDTH
    return {
        "x": jax.random.normal(ks[0], (BATCH, SEQ, D), f32),
        "mem": jax.random.normal(ks[1], (BATCH, MEM_TOKENS, D), f32),
        "norm_mix_g": gain(ks[2], (L, D)),
        "w_in": nrm(ks[3], (L, D, IN_COLS), D),
        "sgu_ln_g": gain(ks[4], (L, W)),
        "sgu_ln_b": 0.02 * jax.random.normal(ks[5], (L, W), f32),
        "w_spatial": nrm(ks[6], (L, SGU_GROUPS, SGU_LEN, SGU_LEN), SGU_LEN),
        "b_spatial": gain(ks[7], (L, SGU_GROUPS, SGU_LEN)),
        "conv_w": nrm(ks[8], (L, CONV_WIDTH, W), CONV_WIDTH),
        "w_branch": nrm(ks[9], (L, N_BRANCHES, W, D), W),
        "w_out": nrm(ks[10], (L, D, D), D),
        "norm_xa_g": gain(ks[11], (L, D)),
        "mem_norm_g": gain(ks[12], (L, D)),
        "w_q_xa": nrm(ks[13], (L, D, D), D),
        "w_k_xa": nrm(ks[14], (L, D, D), D),
        "w_v_xa": nrm(ks[15], (L, D, D), D),
        "w_o_xa": nrm(ks[16], (L, D, D), D),
        "norm_ffn_g": gain(ks[17], (L, D)),
        "w_gate_ffn": nrm(ks[18], (L, D, FFN_HIDDEN), D),
        "w_up_ffn": nrm(ks[19], (L, D, FFN_HIDDEN), D),
        "w_down_ffn": nrm(ks[20], (L, FFN_HIDDEN, D), FFN_HIDDEN),
        "final_g": gain(ks[21], (D,)),
    }


def _fwd_reference(x, mem, norm_mix_g, w_in, sgu_ln_g, sgu_ln_b, w_spatial, b_spatial, conv_w,
              w_branch, w_out, norm_xa_g, mem_norm_g, w_q_xa, w_k_xa, w_v_xa, w_o_xa,
              norm_ffn_g, w_gate_ffn, w_up_ffn, w_down_ffn, final_g):
    for l in range(DEPTH):
        x = x + hybrid_mixer(rms_norm(x, norm_mix_g[l]), w_in[l], sgu_ln_g[l], sgu_ln_b[l],
                             w_spatial[l], b_spatial[l], conv_w[l], w_branch[l], w_out[l])
        x = x + memory_cross_attention(rms_norm(x, norm_xa_g[l]), mem, mem_norm_g[l],
                                       w_q_xa[l], w_k_xa[l], w_v_xa[l], w_o_xa[l])
        x = x + swiglu(rms_norm(x, norm_ffn_g[l]), w_gate_ffn[l], w_up_ffn[l], w_down_ffn[l])
    return rms_norm(x, final_g)


import jax as _jax
import jax.numpy as _jnp

TWIN_FORMAT = 'train_step'
FWD_PARAMS = ['x', 'mem', 'norm_mix_g', 'w_in', 'sgu_ln_g', 'sgu_ln_b', 'w_spatial', 'b_spatial', 'conv_w', 'w_branch', 'w_out', 'norm_xa_g', 'mem_norm_g', 'w_q_xa', 'w_k_xa', 'w_v_xa', 'w_o_xa', 'norm_ffn_g', 'w_gate_ffn', 'w_up_ffn', 'w_down_ffn', 'final_g']
TWIN_WEIGHTS = ['norm_mix_g', 'w_in', 'sgu_ln_g', 'sgu_ln_b', 'w_spatial', 'b_spatial', 'conv_w', 'w_branch', 'w_out', 'norm_xa_g', 'mem_norm_g', 'w_q_xa', 'w_k_xa', 'w_v_xa', 'w_o_xa', 'norm_ffn_g', 'w_gate_ffn', 'w_up_ffn', 'w_down_ffn', 'final_g']
TWIN_DIFF_INPUT = 'x'
TWIN_INPUTS = ['x', 'mem', 'norm_mix_g', 'w_in', 'sgu_ln_g', 'sgu_ln_b', 'w_spatial', 'b_spatial', 'conv_w', 'w_branch', 'w_out', 'norm_xa_g', 'mem_norm_g', 'w_q_xa', 'w_k_xa', 'w_v_xa', 'w_o_xa', 'norm_ffn_g', 'w_gate_ffn', 'w_up_ffn', 'w_down_ffn', 'final_g', 'loss_target', 'm_norm_mix_g', 'm_w_in', 'm_sgu_ln_g', 'm_sgu_ln_b', 'm_w_spatial', 'm_b_spatial', 'm_conv_w', 'm_w_branch', 'm_w_out', 'm_norm_xa_g', 'm_mem_norm_g', 'm_w_q_xa', 'm_w_k_xa', 'm_w_v_xa', 'm_w_o_xa', 'm_norm_ffn_g', 'm_w_gate_ffn', 'm_w_up_ffn', 'm_w_down_ffn', 'm_final_g', 'v_norm_mix_g', 'v_w_in', 'v_sgu_ln_g', 'v_sgu_ln_b', 'v_w_spatial', 'v_b_spatial', 'v_conv_w', 'v_w_branch', 'v_w_out', 'v_norm_xa_g', 'v_mem_norm_g', 'v_w_q_xa', 'v_w_k_xa', 'v_w_v_xa', 'v_w_o_xa', 'v_norm_ffn_g', 'v_w_gate_ffn', 'v_w_up_ffn', 'v_w_down_ffn', 'v_final_g']
TWIN_OUTPUTS = ['loss', 'grad_x', 'grad_norm_mix_g', 'grad_w_in', 'grad_sgu_ln_g', 'grad_sgu_ln_b', 'grad_w_spatial', 'grad_b_spatial', 'grad_conv_w', 'grad_w_branch', 'grad_w_out', 'grad_norm_xa_g', 'grad_mem_norm_g', 'grad_w_q_xa', 'grad_w_k_xa', 'grad_w_v_xa', 'grad_w_o_xa', 'grad_norm_ffn_g', 'grad_w_gate_ffn', 'grad_w_up_ffn', 'grad_w_down_ffn', 'grad_final_g', 'delta_norm_mix_g', 'delta_w_in', 'delta_sgu_ln_g', 'delta_sgu_ln_b', 'delta_w_spatial', 'delta_b_spatial', 'delta_conv_w', 'delta_w_branch', 'delta_w_out', 'delta_norm_xa_g', 'delta_mem_norm_g', 'delta_w_q_xa', 'delta_w_k_xa', 'delta_w_v_xa', 'delta_w_o_xa', 'delta_norm_ffn_g', 'delta_w_gate_ffn', 'delta_w_up_ffn', 'delta_w_down_ffn', 'delta_final_g', 'new_m_norm_mix_g', 'new_m_w_in', 'new_m_sgu_ln_g', 'new_m_sgu_ln_b', 'new_m_w_spatial', 'new_m_b_spatial', 'new_m_conv_w', 'new_m_w_branch', 'new_m_w_out', 'new_m_norm_xa_g', 'new_m_mem_norm_g', 'new_m_w_q_xa', 'new_m_w_k_xa', 'new_m_w_v_xa', 'new_m_w_o_xa', 'new_m_norm_ffn_g', 'new_m_w_gate_ffn', 'new_m_w_up_ffn', 'new_m_w_down_ffn', 'new_m_final_g', 'new_v_norm_mix_g', 'new_v_w_in', 'new_v_sgu_ln_g', 'new_v_sgu_ln_b', 'new_v_w_spatial', 'new_v_b_spatial', 'new_v_conv_w', 'new_v_w_branch', 'new_v_w_out', 'new_v_norm_xa_g', 'new_v_mem_norm_g', 'new_v_w_q_xa', 'new_v_w_k_xa', 'new_v_w_v_xa', 'new_v_w_o_xa', 'new_v_norm_ffn_g', 'new_v_w_gate_ffn', 'new_v_w_up_ffn', 'new_v_w_down_ffn', 'new_v_final_g']
TWIN_LEAF_KINDS = {'loss': 'loss', 'grad_x': 'grad_x', 'grad_norm_mix_g': 'grad_w', 'grad_w_in': 'grad_w', 'grad_sgu_ln_g': 'grad_w', 'grad_sgu_ln_b': 'grad_w', 'grad_w_spatial': 'grad_w', 'grad_b_spatial': 'grad_w', 'grad_conv_w': 'grad_w', 'grad_w_branch': 'grad_w', 'grad_w_out': 'grad_w', 'grad_norm_xa_g': 'grad_w', 'grad_mem_norm_g': 'grad_w', 'grad_w_q_xa': 'grad_w', 'grad_w_k_xa': 'grad_w', 'grad_w_v_xa': 'grad_w', 'grad_w_o_xa': 'grad_w', 'grad_norm_ffn_g': 'grad_w', 'grad_w_gate_ffn': 'grad_w', 'grad_w_up_ffn': 'grad_w', 'grad_w_down_ffn': 'grad_w', 'grad_final_g': 'grad_w', 'delta_norm_mix_g': 'delta_w', 'delta_w_in': 'delta_w', 'delta_sgu_ln_g': 'delta_w', 'delta_sgu_ln_b': 'delta_w', 'delta_w_spatial': 'delta_w', 'delta_b_spatial': 'delta_w', 'delta_conv_w': 'delta_w', 'delta_w_branch': 'delta_w', 'delta_w_out': 'delta_w', 'delta_norm_xa_g': 'delta_w', 'delta_mem_norm_g': 'delta_w', 'delta_w_q_xa': 'delta_w', 'delta_w_k_xa': 'delta_w', 'delta_w_v_xa': 'delta_w', 'delta_w_o_xa': 'delta_w', 'delta_norm_ffn_g': 'delta_w', 'delta_w_gate_ffn': 'delta_w', 'delta_w_up_ffn': 'delta_w', 'delta_w_down_ffn': 'delta_w', 'delta_final_g': 'delta_w', 'new_m_norm_mix_g': 'new_m', 'new_m_w_in': 'new_m', 'new_m_sgu_ln_g': 'new_m', 'new_m_sgu_ln_b': 'new_m', 'new_m_w_spatial': 'new_m', 'new_m_b_spatial': 'new_m', 'new_m_conv_w': 'new_m', 'new_m_w_branch': 'new_m', 'new_m_w_out': 'new_m', 'new_m_norm_xa_g': 'new_m', 'new_m_mem_norm_g': 'new_m', 'new_m_w_q_xa': 'new_m', 'new_m_w_k_xa': 'new_m', 'new_m_w_v_xa': 'new_m', 'new_m_w_o_xa': 'new_m', 'new_m_norm_ffn_g': 'new_m', 'new_m_w_gate_ffn': 'new_m', 'new_m_w_up_ffn': 'new_m', 'new_m_w_down_ffn': 'new_m', 'new_m_final_g': 'new_m', 'new_v_norm_mix_g': 'new_v', 'new_v_w_in': 'new_v', 'new_v_sgu_ln_g': 'new_v', 'new_v_sgu_ln_b': 'new_v', 'new_v_w_spatial': 'new_v', 'new_v_b_spatial': 'new_v', 'new_v_conv_w': 'new_v', 'new_v_w_branch': 'new_v', 'new_v_w_out': 'new_v', 'new_v_norm_xa_g': 'new_v', 'new_v_mem_norm_g': 'new_v', 'new_v_w_q_xa': 'new_v', 'new_v_w_k_xa': 'new_v', 'new_v_w_v_xa': 'new_v', 'new_v_w_o_xa': 'new_v', 'new_v_norm_ffn_g': 'new_v', 'new_v_w_gate_ffn': 'new_v', 'new_v_w_up_ffn': 'new_v', 'new_v_w_down_ffn': 'new_v', 'new_v_final_g': 'new_v'}


def _forward(args):
    return _fwd_reference(*[args[k] for k in FWD_PARAMS])


def _output_shape():
    def fwd():
        inp = _fwd_setup_inputs(0)
        return _fwd_reference(*[inp[k] for k in FWD_PARAMS])
    out = _jax.eval_shape(fwd)
    return out.shape, out.dtype

N_MICROBATCH = 1
ADAM_LR = 0.001
ADAM_B1 = 0.9
ADAM_B2 = 0.999
ADAM_EPS = 1e-08
ADAM_WD = 0.01
ADAM_STEP = 10
PER_EXAMPLE_BATCH_AXIS = {'x': 0, 'mem': 0, 'loss_target': 0}
SHARED_INPUTS = []
_WEIGHT_DTYPES = {'norm_mix_g': _jnp.float32, 'w_in': _jnp.float32, 'sgu_ln_g': _jnp.float32, 'sgu_ln_b': _jnp.float32, 'w_spatial': _jnp.float32, 'b_spatial': _jnp.float32, 'conv_w': _jnp.float32, 'w_branch': _jnp.float32, 'w_out': _jnp.float32, 'norm_xa_g': _jnp.float32, 'mem_norm_g': _jnp.float32, 'w_q_xa': _jnp.float32, 'w_k_xa': _jnp.float32, 'w_v_xa': _jnp.float32, 'w_o_xa': _jnp.float32, 'norm_ffn_g': _jnp.float32, 'w_gate_ffn': _jnp.float32, 'w_up_ffn': _jnp.float32, 'w_down_ffn': _jnp.float32, 'final_g': _jnp.float32}
MOMENT_SCALE = {'norm_mix_g': 1.904502e-01, 'w_in': 7.145218e-02, 'sgu_ln_g': 6.891391e-02, 'sgu_ln_b': 7.058223e-02, 'w_spatial': 6.684805e-02, 'b_spatial': 7.851557e-02, 'conv_w': 1.188682e-01, 'w_branch': 7.137495e-02, 'w_out': 1.236690e-01, 'norm_xa_g': 1.625921e-02, 'mem_norm_g': 2.481882e-02, 'w_q_xa': 1.634812e-02, 'w_k_xa': 1.638761e-02, 'w_v_xa': 1.675770e-02, 'w_o_xa': 1.684981e-02, 'norm_ffn_g': 1.153874e-01, 'w_gate_ffn': 4.877584e-02, 'w_up_ffn': 4.734614e-02, 'w_down_ffn': 7.857330e-02, 'final_g': 3.202451e+01}


def _to_microbatches(a, axis):
    t = _jnp.moveaxis(a, axis, 0)
    t = t.reshape((N_MICROBATCH, t.shape[0] // N_MICROBATCH) + t.shape[1:])
    return _jnp.moveaxis(t, 1, axis + 1)


def setup_inputs(seed: int = 0) -> dict:
    inp = _fwd_setup_inputs(seed)
    key = _jax.random.fold_in(_jax.random.key(seed), 7919)
    shape, _ = _output_shape()
    out = dict(inp)
    out["loss_target"] = _jax.random.normal(_jax.random.fold_in(key, 0), shape, _jnp.float32)
    for i, name in enumerate(TWIN_WEIGHTS):
        w = inp[name].astype(_jnp.float32)
        if MOMENT_SCALE is None:
            s = _jnp.sqrt(_jnp.mean(_jnp.square(w)) + 1e-30)
        else:
            s = MOMENT_SCALE[name]
        km, kv = _jax.random.split(_jax.random.fold_in(key, i + 1))
        out[name] = w
        out["m_" + name] = s * _jax.random.normal(km, w.shape, _jnp.float32)
        out["v_" + name] = (s * s) * _jax.random.uniform(kv, w.shape, _jnp.float32, 0.5, 1.5)
    if N_MICROBATCH > 1:
        for name, axis in PER_EXAMPLE_BATCH_AXIS.items():
            out[name] = _to_microbatches(out[name], axis)
    return {'x': out['x'], 'mem': out['mem'], 'norm_mix_g': out['norm_mix_g'], 'w_in': out['w_in'], 'sgu_ln_g': out['sgu_ln_g'], 'sgu_ln_b': out['sgu_ln_b'], 'w_spatial': out['w_spatial'], 'b_spatial': out['b_spatial'], 'conv_w': out['conv_w'], 'w_branch': out['w_branch'], 'w_out': out['w_out'], 'norm_xa_g': out['norm_xa_g'], 'mem_norm_g': out['mem_norm_g'], 'w_q_xa': out['w_q_xa'], 'w_k_xa': out['w_k_xa'], 'w_v_xa': out['w_v_xa'], 'w_o_xa': out['w_o_xa'], 'norm_ffn_g': out['norm_ffn_g'], 'w_gate_ffn': out['w_gate_ffn'], 'w_up_ffn': out['w_up_ffn'], 'w_down_ffn': out['w_down_ffn'], 'final_g': out['final_g'], 'loss_target': out['loss_target'], 'm_norm_mix_g': out['m_norm_mix_g'], 'm_w_in': out['m_w_in'], 'm_sgu_ln_g': out['m_sgu_ln_g'], 'm_sgu_ln_b': out['m_sgu_ln_b'], 'm_w_spatial': out['m_w_spatial'], 'm_b_spatial': out['m_b_spatial'], 'm_conv_w': out['m_conv_w'], 'm_w_branch': out['m_w_branch'], 'm_w_out': out['m_w_out'], 'm_norm_xa_g': out['m_norm_xa_g'], 'm_mem_norm_g': out['m_mem_norm_g'], 'm_w_q_xa': out['m_w_q_xa'], 'm_w_k_xa': out['m_w_k_xa'], 'm_w_v_xa': out['m_w_v_xa'], 'm_w_o_xa': out['m_w_o_xa'], 'm_norm_ffn_g': out['m_norm_ffn_g'], 'm_w_gate_ffn': out['m_w_gate_ffn'], 'm_w_up_ffn': out['m_w_up_ffn'], 'm_w_down_ffn': out['m_w_down_ffn'], 'm_final_g': out['m_final_g'], 'v_norm_mix_g': out['v_norm_mix_g'], 'v_w_in': out['v_w_in'], 'v_sgu_ln_g': out['v_sgu_ln_g'], 'v_sgu_ln_b': out['v_sgu_ln_b'], 'v_w_spatial': out['v_w_spatial'], 'v_b_spatial': out['v_b_spatial'], 'v_conv_w': out['v_conv_w'], 'v_w_branch': out['v_w_branch'], 'v_w_out': out['v_w_out'], 'v_norm_xa_g': out['v_norm_xa_g'], 'v_mem_norm_g': out['v_mem_norm_g'], 'v_w_q_xa': out['v_w_q_xa'], 'v_w_k_xa': out['v_w_k_xa'], 'v_w_v_xa': out['v_w_v_xa'], 'v_w_o_xa': out['v_w_o_xa'], 'v_norm_ffn_g': out['v_norm_ffn_g'], 'v_w_gate_ffn': out['v_w_gate_ffn'], 'v_w_up_ffn': out['v_w_up_ffn'], 'v_w_down_ffn': out['v_w_down_ffn'], 'v_final_g': out['v_final_g']}


def _loss(weights, diff, rest, loss_target):
    with _jax.named_scope("forward"):
        args = {**rest, TWIN_DIFF_INPUT: diff, **{k: w.astype(_WEIGHT_DTYPES[k]) for k, w in weights.items()}}
        y = _forward(args)
    with _jax.named_scope("loss_head"):
        err = _jnp.square(y.astype(_jnp.float32) - loss_target)
        return 0.5 * _jnp.sum(_jnp.mean(err, axis=-1)) if err.ndim else 0.5 * err


def _adamw(w, g, m, v):
    m = ADAM_B1 * m + (1.0 - ADAM_B1) * g
    v = ADAM_B2 * v + (1.0 - ADAM_B2) * _jnp.square(g)
    m_hat = m / (1.0 - ADAM_B1 ** ADAM_STEP)
    v_hat = v / (1.0 - ADAM_B2 ** ADAM_STEP)
    delta = -ADAM_LR * (m_hat / (_jnp.sqrt(v_hat) + ADAM_EPS) + ADAM_WD * w)
    return delta, m, v


def reference(x, mem, norm_mix_g, w_in, sgu_ln_g, sgu_ln_b, w_spatial, b_spatial, conv_w, w_branch, w_out, norm_xa_g, mem_norm_g, w_q_xa, w_k_xa, w_v_xa, w_o_xa, norm_ffn_g, w_gate_ffn, w_up_ffn, w_down_ffn, final_g, loss_target, m_norm_mix_g, m_w_in, m_sgu_ln_g, m_sgu_ln_b, m_w_spatial, m_b_spatial, m_conv_w, m_w_branch, m_w_out, m_norm_xa_g, m_mem_norm_g, m_w_q_xa, m_w_k_xa, m_w_v_xa, m_w_o_xa, m_norm_ffn_g, m_w_gate_ffn, m_w_up_ffn, m_w_down_ffn, m_final_g, v_norm_mix_g, v_w_in, v_sgu_ln_g, v_sgu_ln_b, v_w_spatial, v_b_spatial, v_conv_w, v_w_branch, v_w_out, v_norm_xa_g, v_mem_norm_g, v_w_q_xa, v_w_k_xa, v_w_v_xa, v_w_o_xa, v_norm_ffn_g, v_w_gate_ffn, v_w_up_ffn, v_w_down_ffn, v_final_g):
    given = dict(x=x, mem=mem, norm_mix_g=norm_mix_g, w_in=w_in, sgu_ln_g=sgu_ln_g, sgu_ln_b=sgu_ln_b, w_spatial=w_spatial, b_spatial=b_spatial, conv_w=conv_w, w_branch=w_branch, w_out=w_out, norm_xa_g=norm_xa_g, mem_norm_g=mem_norm_g, w_q_xa=w_q_xa, w_k_xa=w_k_xa, w_v_xa=w_v_xa, w_o_xa=w_o_xa, norm_ffn_g=norm_ffn_g, w_gate_ffn=w_gate_ffn, w_up_ffn=w_up_ffn, w_down_ffn=w_down_ffn, final_g=final_g, loss_target=loss_target, m_norm_mix_g=m_norm_mix_g, m_w_in=m_w_in, m_sgu_ln_g=m_sgu_ln_g, m_sgu_ln_b=m_sgu_ln_b, m_w_spatial=m_w_spatial, m_b_spatial=m_b_spatial, m_conv_w=m_conv_w, m_w_branch=m_w_branch, m_w_out=m_w_out, m_norm_xa_g=m_norm_xa_g, m_mem_norm_g=m_mem_norm_g, m_w_q_xa=m_w_q_xa, m_w_k_xa=m_w_k_xa, m_w_v_xa=m_w_v_xa, m_w_o_xa=m_w_o_xa, m_norm_ffn_g=m_norm_ffn_g, m_w_gate_ffn=m_w_gate_ffn, m_w_up_ffn=m_w_up_ffn, m_w_down_ffn=m_w_down_ffn, m_final_g=m_final_g, v_norm_mix_g=v_norm_mix_g, v_w_in=v_w_in, v_sgu_ln_g=v_sgu_ln_g, v_sgu_ln_b=v_sgu_ln_b, v_w_spatial=v_w_spatial, v_b_spatial=v_b_spatial, v_conv_w=v_conv_w, v_w_branch=v_w_branch, v_w_out=v_w_out, v_norm_xa_g=v_norm_xa_g, v_mem_norm_g=v_mem_norm_g, v_w_q_xa=v_w_q_xa, v_w_k_xa=v_w_k_xa, v_w_v_xa=v_w_v_xa, v_w_o_xa=v_w_o_xa, v_norm_ffn_g=v_norm_ffn_g, v_w_gate_ffn=v_w_gate_ffn, v_w_up_ffn=v_w_up_ffn, v_w_down_ffn=v_w_down_ffn, v_final_g=v_final_g)
    weights = {n: given[n] for n in TWIN_WEIGHTS}
    shared = {n: given[n] for n in SHARED_INPUTS}
    per_example = {n: given[n] for n in ['x', 'mem']}
    grad_fn = _jax.value_and_grad(_loss, argnums=(0, 1))

    def one_microbatch(ex, loss_target):
        ex = dict(ex)
        diff = ex.pop(TWIN_DIFF_INPUT)
        return grad_fn(weights, diff, {**shared, **ex}, loss_target)

    if N_MICROBATCH == 1:
        loss, (grad_w, grad_x) = one_microbatch(per_example, given["loss_target"])
    else:
        def body(carry, xs):
            loss_sum, grad_sum = carry
            l_k, (gw_k, gx_k) = one_microbatch(xs[0], xs[1])
            with _jax.named_scope("update"):
                return (loss_sum + l_k, _jax.tree.map(_jnp.add, grad_sum, gw_k)), gx_k

        init = (_jnp.zeros((), _jnp.float32), _jax.tree.map(_jnp.zeros_like, weights))
        (loss, grad_w), grad_x = _jax.lax.scan(body, init, (per_example, given["loss_target"]))
    with _jax.named_scope("update"):
        delta_w, new_m, new_v = {}, {}, {}
        for n in TWIN_WEIGHTS:
            delta_w[n], new_m[n], new_v[n] = _adamw(weights[n], grad_w[n], given["m_" + n], given["v_" + n])
    return (loss, grad_x, *[grad_w[n] for n in TWIN_WEIGHTS], *[delta_w[n] for n in TWIN_WEIGHTS],
            *[new_m[n] for n in TWIN_WEIGHTS], *[new_v[n] for n in TWIN_WEIGHTS])
```

```python
import functools
import math

import jax
import jax.numpy as jnp
from jax import lax
from jax.experimental import pallas as pl
from jax.experimental.pallas import tpu as pltpu

F32, BF16 = jnp.float32, jnp.bfloat16
MESH = pl.DeviceIdType.MESH
ANY = pl.BlockSpec(memory_space=pl.ANY)

D_MODEL = 1024
DEPTH = 2
BW = 512
SB_HEADS, SB_DH = 8, 64
SGU_LEN, SGU_GROUPS, SGU_GD, SGU_CHUNK = 128, 4, 128, 64
XA_HEADS, XA_DH = 4, 256
FFN_SH = 704
N_CHIPS = 4
IN_COLS = 7168
C_Z, C_CB, C_GATES = 1536, 2560, 4096

ADAM_LR, ADAM_B1, ADAM_B2, ADAM_EPS, ADAM_WD, ADAM_STEP = 0.001, 0.9, 0.999, 1e-08, 0.01, 10

VMEM_LIMIT_V7X = 56 * 1024 * 1024

NN = (((1,), (0,)), ((), ()))
NT = (((1,), (1,)), ((), ()))
TN = (((0,), (0,)), ((), ()))


def _cp(*sem):
    return pltpu.CompilerParams(dimension_semantics=sem, vmem_limit_bytes=VMEM_LIMIT_V7X)


def _tile(n, pref):
    for t in pref:
        if n % t == 0:
            return t
    return n


def _rows(r, row_bytes, block_bytes=1 << 20):
    for t in (1024, 512, 256, 128, 64, 32, 16, 8):
        if r % t == 0 and t * row_bytes <= block_bytes:
            return t
    return r


def _mm(a, b, *, mode, name, out_dtype=F32, res=None, a_kind="2d", b_kind="2d", tm=None, tn=None, tk=None):
    a2, b2 = a.shape[-2:], b.shape[-2:]
    if mode == "nn":
        (M, K), N = a2, b2[1]
    elif mode == "nt":
        (M, K), N = a2, b2[0]
    else:
        (K, M), N = a2, b2[1]
    kchunk = a_kind == "kchunk" or b_kind == "kchunk"
    batch = a_kind == "batch" or b_kind == "batch"
    G = (a.shape[0] if a_kind == "batch" else b.shape[0]) if batch else 1
    tm = tm or _tile(M, (1024, 512, 256, 128))
    tn = tn or _tile(N, (1024, 512, 256, 128))
    if kchunk:
        tk, nk = K, (a.shape[0] if a_kind == "kchunk" else b.shape[0])
    else:
        tk = tk or _tile(K, (1024, 512, 256, 128))
        nk = K // tk

    def spec(kind, blk, idx):
        if kind == "2d":
            return pl.BlockSpec(blk, lambda g, i, j, k: idx(g, i, j, k))
        if kind == "batch":
            return pl.BlockSpec((None,) + blk, lambda g, i, j, k: (g,) + idx(g, i, j, k))
        return pl.BlockSpec((None,) + blk, lambda g, i, j, k: (k,) + idx(g, i, j, 0))

    if mode == "nn":
        a_spec = spec(a_kind, (tm, tk), lambda g, i, j, k: (i, k))
        b_spec = spec(b_kind, (tk, tn), lambda g, i, j, k: (k, j))
    elif mode == "nt":
        a_spec = spec(a_kind, (tm, tk), lambda g, i, j, k: (i, k))
        b_spec = spec(b_kind, (tn, tk), lambda g, i, j, k: (j, k))
    else:
        a_spec = spec(a_kind, (tk, tm), lambda g, i, j, k: (k, i))
        b_spec = spec(b_kind, (tk, tn), lambda g, i, j, k: (k, j))
    o_kind = "batch" if batch else "2d"
    o_spec = spec(o_kind, (tm, tn), lambda g, i, j, k: (i, j))
    o_shape = ((G,) if batch else ()) + (M, N)
    dn = {"nn": NN, "nt": NT, "tn": TN}[mode]
    has_res = res is not None

    def body(*refs):
        if has_res:
            a_ref, b_ref, r_ref, o_ref = refs[:4]
        else:
            a_ref, b_ref, o_ref = refs[:3]
        p = lax.dot_general(a_ref[...].astype(BF16), b_ref[...].astype(BF16), dn, preferred_element_type=F32)

        def finish(r):
            if has_res:
                r = r + r_ref[...]
            o_ref[...] = r.astype(out_dtype)

        if nk == 1:
            finish(p)
        else:
            acc = refs[-1]
            k = pl.program_id(3)

            @pl.when(k == 0)
            def _():
                acc[...] = p

            @pl.when(k > 0)
            def _():
                acc[...] += p

            @pl.when(k == nk - 1)
            def _():
                finish(acc[...])

    in_specs, args = [a_spec, b_spec], [a, b]
    if has_res:
        in_specs.append(spec("2d", (tm, tn), lambda g, i, j, k: (i, j)))
        args.append(res)
    return pl.pallas_call(
        body, name=name, grid=(G, M // tm, N // tn, nk), in_specs=in_specs, out_specs=o_spec,
        out_shape=jax.ShapeDtypeStruct(o_shape, out_dtype),
        scratch_shapes=[pltpu.VMEM((tm, tn), F32)] if nk > 1 else [],
        compiler_params=_cp("parallel", "parallel", "parallel", "arbitrary"),
    )(*args)


def _rms_fwd(x, g, *, name):
    S, Dm = x.shape
    tm = _tile(S, (512, 256))

    def body(x_ref, g_ref, o_ref):
        xv = x_ref[...]
        r = lax.rsqrt(jnp.mean(xv * xv, axis=-1, keepdims=True) + 1e-6)
        o_ref[...] = (xv * r * g_ref[...]).astype(BF16)

    return pl.pallas_call(
        body, name=name, grid=(S // tm,),
        in_specs=[pl.BlockSpec((tm, Dm), lambda i: (i, 0)), pl.BlockSpec((1, Dm), lambda i: (0, 0))],
        out_specs=pl.BlockSpec((tm, Dm), lambda i: (i, 0)), out_shape=jax.ShapeDtypeStruct((S, Dm), BF16),
        compiler_params=_cp("parallel"),
    )(x, g.reshape(1, Dm))


def _rms_bwd(x, g, dh, dres, *, name):
    S, Dm = x.shape
    tm = _tile(S, (512, 256))

    def body(x_ref, g_ref, dh_ref, dr_ref, dx_ref, dg_ref):
        xv, dhv = x_ref[...], dh_ref[...].astype(F32)
        r = lax.rsqrt(jnp.mean(xv * xv, axis=-1, keepdims=True) + 1e-6)
        u = dhv * g_ref[...]
        s = jnp.sum(u * xv, axis=-1, keepdims=True)
        dx_ref[...] = dr_ref[...] + r * u - xv * ((r * r * r) * (s * (1.0 / Dm)))
        part = jnp.sum(dhv * (xv * r), axis=0, keepdims=True)

        @pl.when(pl.program_id(0) == 0)
        def _():
            dg_ref[...] = part

        @pl.when(pl.program_id(0) > 0)
        def _():
            dg_ref[...] += part

    row = pl.BlockSpec((tm, Dm), lambda i: (i, 0))
    vec = pl.BlockSpec((1, Dm), lambda i: (0, 0))
    dx, dg = pl.pallas_call(
        body, name=name, grid=(S // tm,), in_specs=[row, vec, row, row], out_specs=[row, vec],
        out_shape=[jax.ShapeDtypeStruct((S, Dm), F32), jax.ShapeDtypeStruct((1, Dm), F32)],
        compiler_params=_cp("arbitrary"),
    )(x, g.reshape(1, Dm), dh, dres)
    return dx, dg.reshape(Dm)


def _loss_head(x, g, target):
    S, Dm = x.shape
    tm = _tile(S, (512, 256))

    def body(x_ref, g_ref, t_ref, dx_ref, dg_ref, loss_ref):
        xv, gv = x_ref[...], g_ref[...]
        r = lax.rsqrt(jnp.mean(xv * xv, axis=-1, keepdims=True) + 1e-6)
        xn = xv * r
        err = xn * gv - t_ref[...]
        lpart = 0.5 * jnp.sum(jnp.mean(err * err, axis=-1, keepdims=True), axis=0, keepdims=True)
        dy = err * (1.0 / Dm)
        u = dy * gv
        s = jnp.sum(u * xv, axis=-1, keepdims=True)
        dx_ref[...] = r * u - xv * ((r * r * r) * (s * (1.0 / Dm)))
        part = jnp.sum(dy * xn, axis=0, keepdims=True)
        lslab = jnp.broadcast_to(lpart, (8, 128))

        @pl.when(pl.program_id(0) == 0)
        def _():
            dg_ref[...] = part
            loss_ref[...] = lslab

        @pl.when(pl.program_id(0) > 0)
        def _():
            dg_ref[...] += part
            loss_ref[...] += lslab

    row = pl.BlockSpec((tm, Dm), lambda i: (i, 0))
    vec = pl.BlockSpec((1, Dm), lambda i: (0, 0))
    dx, dg, loss = pl.pallas_call(
        body, name="loss_head", grid=(S // tm,), in_specs=[row, vec, row],
        out_specs=[row, vec, pl.BlockSpec((8, 128), lambda i: (0, 0))],
        out_shape=[jax.ShapeDtypeStruct((S, Dm), F32), jax.ShapeDtypeStruct((1, Dm), F32), jax.ShapeDtypeStruct((8, 128), F32)],
        compiler_params=_cp("arbitrary"),
    )(x, g.reshape(1, Dm), target)
    return loss[0, 0], dx, dg.reshape(Dm)


SB_TQ, SB_TK = 256, 128


def _split2(v):
    hi = v.astype(BF16)
    return jnp.concatenate([hi, (v - hi.astype(F32)).astype(BF16)], axis=1)


def _tri2(cmp):
    j = lax.broadcasted_iota(jnp.int32, (2 * SB_TK, SB_TK), 0) % SB_TK
    s = lax.broadcasted_iota(jnp.int32, (2 * SB_TK, SB_TK), 1)
    return cmp(j, s).astype(BF16)


def _sb_scores(qv, kb, k0, q0, tq):
    z = lax.dot_general(qv, kb, NT, preferred_element_type=F32) * (SB_DH ** -0.5)
    t_pos = q0 + lax.broadcasted_iota(jnp.int32, (tq, SB_TK), 0)
    s_pos = k0 + lax.broadcasted_iota(jnp.int32, (tq, SB_TK), 1)
    valid = s_pos < t_pos
    ls = jnp.minimum(z, 0.0) - jnp.log1p(jnp.exp(-jnp.abs(z)))
    l1m = jnp.where(valid, ls - z, 0.0)
    return z, valid, ls, l1m


def _sb_fwd(q, k, v):
    H, S, dh = q.shape
    tq = min(SB_TQ, S)
    kb_per_q = tq // SB_TK

    def body(q_ref, k_ref, v_ref, o_ref, tot_ref, acc, c):
        i = pl.program_id(1)
        q0 = i * tq
        qv = q_ref[...]
        later = _tri2(lambda j, s: j > s)
        acc[...] = jnp.zeros_like(acc)
        c[...] = jnp.zeros_like(c)
        nkb = (i + 1) * kb_per_q

        def step(n, carry):
            k0 = pl.multiple_of((nkb - 1 - n) * SB_TK, SB_TK)
            kb, vb = k_ref[pl.ds(k0, SB_TK), :], v_ref[pl.ds(k0, SB_TK), :]
            z, valid, ls, l1m = _sb_scores(qv, kb, k0, q0, tq)
            after = jnp.dot(_split2(l1m), later, preferred_element_type=F32)
            a = jnp.where(valid, jnp.exp(ls + after + c[...]), 0.0)
            acc[...] += jnp.dot(a.astype(BF16), vb, preferred_element_type=F32)
            c[...] += jnp.sum(l1m, axis=1, keepdims=True)
            return carry

        lax.fori_loop(0, nkb, step, 0)
        o_ref[...] = acc[...].astype(o_ref.dtype)
        tot_ref[...] = c[...]

    qs = pl.BlockSpec((None, tq, dh), lambda h, i: (h, i, 0))
    full = pl.BlockSpec((None, S, dh), lambda h, i: (h, 0, 0))
    return pl.pallas_call(
        body, name="sb_fwd", grid=(H, S // tq), in_specs=[qs, full, full],
        out_specs=[qs, pl.BlockSpec((None, tq, 1), lambda h, i: (h, i, 0))],
        out_shape=[jax.ShapeDtypeStruct((H, S, dh), BF16), jax.ShapeDtypeStruct((H, S, 1), F32)],
        scratch_shapes=[pltpu.VMEM((tq, dh), F32), pltpu.VMEM((tq, 1), F32)],
        compiler_params=_cp("parallel", "parallel"),
    )(q, k, v)


def _sb_bwd(q, k, v, do, tot):
    H, S, dh = q.shape
    tq = min(SB_TQ, S)
    kb_per_q = tq // SB_TK
    scale = SB_DH ** -0.5

    def body(q_ref, k_ref, v_ref, do_ref, tot_ref, dq_ref, dk_ref, dv_ref, dq_acc, pre, gpre):
        i = pl.program_id(1)
        q0 = i * tq
        qv, dov, totv = q_ref[...], do_ref[...], tot_ref[...]
        upto = _tri2(lambda j, s: j <= s)
        before = _tri2(lambda j, s: j < s)

        @pl.when(i == 0)
        def _():
            dk_ref[...] = jnp.zeros_like(dk_ref)
            dv_ref[...] = jnp.zeros_like(dv_ref)

        dq_acc[...] = jnp.zeros_like(dq_acc)
        pre[...] = jnp.zeros_like(pre)
        gpre[...] = jnp.zeros_like(gpre)

        def step(n, carry):
            k0 = pl.multiple_of(n * SB_TK, SB_TK)
            kb, vb = k_ref[pl.ds(k0, SB_TK), :], v_ref[pl.ds(k0, SB_TK), :]
            z, valid, ls, l1m = _sb_scores(qv, kb, k0, q0, tq)
            incl = jnp.dot(_split2(l1m), upto, preferred_element_type=F32)
            rest = totv - (pre[...] + incl)
            a = jnp.where(valid, jnp.exp(ls + rest), 0.0)
            da = lax.dot_general(dov, vb, NT, preferred_element_type=F32)
            g = a * da
            gbefore = jnp.dot(_split2(g), before, preferred_element_type=F32) + gpre[...]
            dz = jnp.where(valid, g * jnp.exp(ls - z) - jnp.exp(ls) * gbefore, 0.0) * scale
            dzb = dz.astype(BF16)
            dq_acc[...] += jnp.dot(dzb, kb, preferred_element_type=F32)
            dk_ref[pl.ds(k0, SB_TK), :] += lax.dot_general(dzb, qv, TN, preferred_element_type=F32)
            dv_ref[pl.ds(k0, SB_TK), :] += lax.dot_general(a.astype(BF16), dov, TN, preferred_element_type=F32)
            pre[...] += jnp.sum(l1m, axis=1, keepdims=True)
            gpre[...] += jnp.sum(g, axis=1, keepdims=True)
            return carry

        lax.fori_loop(0, (i + 1) * kb_per_q, step, 0)
        dq_ref[...] = dq_acc[...].astype(dq_ref.dtype)

    qs = pl.BlockSpec((None, tq, dh), lambda h, i: (h, i, 0))
    full = pl.BlockSpec((None, S, dh), lambda h, i: (h, 0, 0))
    return pl.pallas_call(
        body, name="sb_bwd", grid=(H, S // tq),
        in_specs=[qs, full, full, qs, pl.BlockSpec((None, tq, 1), lambda h, i: (h, i, 0))],
        out_specs=[qs, full, full],
        out_shape=[jax.ShapeDtypeStruct((H, S, dh), BF16), jax.ShapeDtypeStruct((H, S, dh), F32), jax.ShapeDtypeStruct((H, S, dh), F32)],
        scratch_shapes=[pltpu.VMEM((tq, dh), F32), pltpu.VMEM((tq, 1), F32), pltpu.VMEM((tq, 1), F32)],
        compiler_params=_cp("parallel", "arbitrary"),
    )(q, k, v, do, tot)


_INV_SQRT2 = 0.7071067811865476
_INV_SQRT2PI = 0.3989422804014327


def _gelu(x):
    return 0.5 * x * (1.0 + lax.erf(x * _INV_SQRT2))


def _gelu_grad(x):
    return 0.5 * (1.0 + lax.erf(x * _INV_SQRT2)) + x * (_INV_SQRT2PI * jnp.exp(-0.5 * x * x))


def _sgu_mask():
    t = lax.broadcasted_iota(jnp.int32, (SGU_LEN, SGU_LEN), 0) // SGU_CHUNK
    s = lax.broadcasted_iota(jnp.int32, (SGU_LEN, SGU_LEN), 1) // SGU_CHUNK
    return t >= s


def _sgu_mask_t():
    t = lax.broadcasted_iota(jnp.int32, (SGU_LEN, SGU_LEN), 0) // SGU_CHUNK
    s = lax.broadcasted_iota(jnp.int32, (SGU_LEN, SGU_LEN), 1) // SGU_CHUNK
    return s >= t


def _sgu_norm(zv, g, b):
    vv = _gelu(zv)
    xc = vv - jnp.mean(vv, axis=-1, keepdims=True)
    rstd = lax.rsqrt(jnp.mean(xc * xc, axis=-1, keepdims=True) + 1e-5)
    xhat = xc * rstd
    return xhat, rstd, xhat * g + b


SGU_TM = 256


def _sgu_fwd(p, ln_g, ln_b, w_s, b_st):
    S = p.shape[0]
    tm = min(SGU_TM, S)

    def body(zu_ref, zv_ref, g_ref, b_ref, w_ref, bs_ref, o_ref):
        u = _gelu(zu_ref[...])
        _, _, vn = _sgu_norm(zv_ref[...], g_ref[...], b_ref[...])
        vnb = vn.astype(BF16)
        mask = _sgu_mask()
        for gi in range(SGU_GROUPS):
            wg = jnp.where(mask, w_ref[gi], 0.0).astype(BF16)
            cols = slice(gi * SGU_GD, (gi + 1) * SGU_GD)
            for ci in range(tm // SGU_LEN):
                rows = slice(ci * SGU_LEN, (ci + 1) * SGU_LEN)
                vm = jnp.dot(wg, vnb[rows, cols], preferred_element_type=F32) + bs_ref[:, gi:gi + 1]
                o_ref[rows, cols] = (u[rows, cols] * vm).astype(BF16)

    vec = pl.BlockSpec((1, BW), lambda i: (0, 0))
    return pl.pallas_call(
        body, name="sgu_fwd", grid=(S // tm,),
        in_specs=[pl.BlockSpec((tm, BW), lambda i: (i, C_Z // BW)), pl.BlockSpec((tm, BW), lambda i: (i, C_Z // BW + 1)), vec, vec,
                  pl.BlockSpec((SGU_GROUPS, SGU_LEN, SGU_LEN), lambda i: (0, 0, 0)), pl.BlockSpec((SGU_LEN, SGU_GROUPS), lambda i: (0, 0))],
        out_specs=pl.BlockSpec((tm, BW), lambda i: (i, 0)), out_shape=jax.ShapeDtypeStruct((S, BW), BF16),
        compiler_params=_cp("parallel"),
    )(p, p, ln_g.reshape(1, BW), ln_b.reshape(1, BW), w_s, b_st)


def _sgu_bwd(p, dyb, ln_g, ln_b, w_s, w_st, b_st):
    S = p.shape[0]
    tm = min(SGU_TM, S)

    def body(zu_ref, zv_ref, dy_ref, g_ref, b_ref, w_ref, wt_ref, bs_ref, dz_ref, dg_ref, db_ref, dw_ref, dbs_ref, dvn):
        first = pl.program_id(0) == 0

        @pl.when(first)
        def _():
            dg_ref[...] = jnp.zeros_like(dg_ref)
            db_ref[...] = jnp.zeros_like(db_ref)
            dw_ref[...] = jnp.zeros_like(dw_ref)
            dbs_ref[...] = jnp.zeros_like(dbs_ref)

        zu, zv, dy = zu_ref[...], zv_ref[...], dy_ref[...].astype(F32)
        u = _gelu(zu)
        xhat, rstd, vn = _sgu_norm(zv, g_ref[...], b_ref[...])
        vnb = vn.astype(BF16)
        mask = _sgu_mask()
        mask_t = _sgu_mask_t()
        for gi in range(SGU_GROUPS):
            wg = jnp.where(mask, w_ref[gi], 0.0).astype(BF16)
            wgt = jnp.where(mask_t, wt_ref[gi], 0.0).astype(BF16)
            cols = slice(gi * SGU_GD, (gi + 1) * SGU_GD)
            for ci in range(tm // SGU_LEN):
                rows = slice(ci * SGU_LEN, (ci + 1) * SGU_LEN)
                vm = jnp.dot(wg, vnb[rows, cols], preferred_element_type=F32) + bs_ref[:, gi:gi + 1]
                dyc = dy[rows, cols]
                dz_ref[rows, cols] = (dyc * vm * _gelu_grad(zu[rows, cols])).astype(BF16)
                dvm = dyc * u[rows, cols]
                dvmb = dvm.astype(BF16)
                dbs_ref[gi] += jnp.broadcast_to(jnp.sum(dvm, axis=1, keepdims=True), (SGU_LEN, SGU_GD))
                dw_ref[gi] += lax.dot_general(dvmb, vnb[rows, cols], NT, preferred_element_type=F32)
                dvn[rows, cols] = jnp.dot(wgt, dvmb, preferred_element_type=F32)
        dvnv = dvn[...]
        dg_ref[...] += jnp.sum(dvnv * xhat, axis=0, keepdims=True)
        db_ref[...] += jnp.sum(dvnv, axis=0, keepdims=True)
        dxh = dvnv * g_ref[...]
        dvv = rstd * (dxh - jnp.mean(dxh, axis=-1, keepdims=True) - xhat * jnp.mean(dxh * xhat, axis=-1, keepdims=True))
        dz_ref[:, BW:] = (dvv * _gelu_grad(zv)).astype(BF16)

        @pl.when(pl.program_id(0) == n_steps - 1)
        def _():
            for gi in range(SGU_GROUPS):
                dw_ref[gi] = jnp.where(mask, dw_ref[gi], 0.0)

    n_steps = S // tm
    vec = pl.BlockSpec((1, BW), lambda i: (0, 0))
    half = lambda c: pl.BlockSpec((tm, BW), lambda i: (i, c))
    wspec = pl.BlockSpec((SGU_GROUPS, SGU_LEN, SGU_LEN), lambda i: (0, 0, 0))
    dz, dg, db, dw, dbs = pl.pallas_call(
        body, name="sgu_bwd", grid=(n_steps,),
        in_specs=[half(C_Z // BW), half(C_Z // BW + 1), half(0), vec, vec, wspec, wspec,
                  pl.BlockSpec((SGU_LEN, SGU_GROUPS), lambda i: (0, 0))],
        out_specs=[pl.BlockSpec((tm, 2 * BW), lambda i: (i, 0)), vec, vec, wspec, wspec],
        out_shape=[jax.ShapeDtypeStruct((S, 2 * BW), BF16), jax.ShapeDtypeStruct((1, BW), F32), jax.ShapeDtypeStruct((1, BW), F32),
                   jax.ShapeDtypeStruct((SGU_GROUPS, SGU_LEN, SGU_LEN), F32), jax.ShapeDtypeStruct((SGU_GROUPS, SGU_LEN, SGU_GD), F32)],
        scratch_shapes=[pltpu.VMEM((tm, BW), F32)],
        compiler_params=_cp("arbitrary"),
    )(p, p, dyb, ln_g.reshape(1, BW), ln_b.reshape(1, BW), w_s, w_st, b_st)
    return dz, dg.reshape(BW), db.reshape(BW), dw, dbs[:, :, 0]


CONV_TC = 128


def _shift_down(y, n):
    rows = lax.broadcasted_iota(jnp.int32, y.shape, 0)
    return jnp.where(rows < n, 0.0, pltpu.roll(y, n, 0))


def _shift_up(y, n):
    rows = lax.broadcasted_iota(jnp.int32, y.shape, 0)
    return jnp.where(rows >= y.shape[0] - n, 0.0, pltpu.roll(y, y.shape[0] - n, 0))


def _conv_specs(S):
    col = lambda c0: pl.BlockSpec((S, CONV_TC), lambda j: (0, c0 // CONV_TC + j))
    return col(C_CB), col(C_CB + BW), col(C_CB + 2 * BW), pl.BlockSpec((3, CONV_TC), lambda j: (0, j)), pl.BlockSpec((S, CONV_TC), lambda j: (0, j))


def _conv_fwd(p, conv_w):
    S = p.shape[0]

    def body(cb_ref, cc_ref, cx_ref, w_ref, o_ref):
        y = cc_ref[...] * cx_ref[...]
        conv = w_ref[0:1, :] * _shift_down(y, 2) + w_ref[1:2, :] * _shift_down(y, 1) + w_ref[2:3, :] * y
        o_ref[...] = (cb_ref[...] * conv).astype(BF16)

    cb, cc, cx, wspec, out = _conv_specs(S)
    return pl.pallas_call(
        body, name="conv_fwd", grid=(BW // CONV_TC,), in_specs=[cb, cc, cx, wspec], out_specs=out,
        out_shape=jax.ShapeDtypeStruct((S, BW), BF16), compiler_params=_cp("parallel"),
    )(p, p, p, conv_w)


def _conv_bwd(p, conv_w, dyc):
    S = p.shape[0]

    def body(cb_ref, cc_ref, cx_ref, w_ref, dy_ref, db_ref, dc_ref, dx_ref, dw_ref):
        cc, cx, dy = cc_ref[...], cx_ref[...], dy_ref[...].astype(F32)
        y = cc * cx
        w0, w1, w2 = w_ref[0:1, :], w_ref[1:2, :], w_ref[2:3, :]
        y1, y2 = _shift_down(y, 1), _shift_down(y, 2)
        conv = w0 * y2 + w1 * y1 + w2 * y
        db_ref[...] = (dy * conv).astype(BF16)
        dconv = dy * cb_ref[...]
        dyy = w2 * dconv + w1 * _shift_up(dconv, 1) + w0 * _shift_up(dconv, 2)
        dc_ref[...] = (dyy * cx).astype(BF16)
        dx_ref[...] = (dyy * cc).astype(BF16)
        dw_ref[0:1, :] = jnp.sum(dconv * y2, axis=0, keepdims=True)
        dw_ref[1:2, :] = jnp.sum(dconv * y1, axis=0, keepdims=True)
        dw_ref[2:3, :] = jnp.sum(dconv * y, axis=0, keepdims=True)

    cb, cc, cx, wspec, out = _conv_specs(S)
    db, dc, dx, dw = pl.pallas_call(
        body, name="conv_bwd", grid=(BW // CONV_TC,), in_specs=[cb, cc, cx, wspec, out],
        out_specs=[out, out, out, wspec],
        out_shape=[jax.ShapeDtypeStruct((S, BW), BF16)] * 3 + [jax.ShapeDtypeStruct((3, BW), F32)],
        compiler_params=_cp("parallel"),
    )(p, p, p, conv_w, dyc)
    return db, dc, dx, dw


def _merge_specs(S, tm):
    gate = lambda n: pl.BlockSpec((tm, D_MODEL), lambda i: (i, C_GATES // D_MODEL + n))
    return [gate(0), gate(1), gate(2)], pl.BlockSpec((3, tm, D_MODEL), lambda i: (0, i, 0)), pl.BlockSpec((tm, D_MODEL), lambda i: (i, 0))


def _merge_fwd(p, bd):
    S = p.shape[0]
    tm = _tile(S, (256,))

    def body(g0, g1, g2, b_ref, o_ref):
        acc = jax.nn.sigmoid(g0[...]) * b_ref[0]
        acc = acc + jax.nn.sigmoid(g1[...]) * b_ref[1]
        acc = acc + jax.nn.sigmoid(g2[...]) * b_ref[2]
        o_ref[...] = acc.astype(BF16)

    gates, bspec, row = _merge_specs(S, tm)
    return pl.pallas_call(
        body, name="merge_fwd", grid=(S // tm,), in_specs=gates + [bspec], out_specs=row,
        out_shape=jax.ShapeDtypeStruct((S, D_MODEL), BF16), compiler_params=_cp("parallel"),
    )(p, p, p, bd)


def _merge_bwd(p, bd, dm):
    S = p.shape[0]
    tm = _tile(S, (256,))

    def body(g0, g1, g2, b_ref, dm_ref, db_ref, dg_ref):
        dmv = dm_ref[...]
        for n, g_ref in enumerate((g0, g1, g2)):
            sg = jax.nn.sigmoid(g_ref[...])
            db_ref[n] = (dmv * sg).astype(BF16)
            dg_ref[:, n * D_MODEL:(n + 1) * D_MODEL] = (dmv * b_ref[n] * (sg * (1.0 - sg))).astype(BF16)

    gates, bspec, row = _merge_specs(S, tm)
    return pl.pallas_call(
        body, name="merge_bwd", grid=(S // tm,), in_specs=gates + [bspec, row],
        out_specs=[bspec, pl.BlockSpec((tm, 3 * D_MODEL), lambda i: (i, 0))],
        out_shape=[jax.ShapeDtypeStruct((3, S, D_MODEL), BF16), jax.ShapeDtypeStruct((S, 3 * D_MODEL), BF16)],
        compiler_params=_cp("parallel"),
    )(p, p, p, bd, dm)


XA_TM = 512


def _xa_probs(qh, kh):
    s = lax.dot_general(qh, kh, NT, preferred_element_type=F32) * (XA_DH ** -0.5)
    e = jnp.exp(s - jnp.max(s, axis=-1, keepdims=True))
    return e / jnp.sum(e, axis=-1, keepdims=True)


def _xa_fwd(q, kv):
    S = q.shape[0]
    tm = min(XA_TM, S)
    M = kv.shape[1]

    def body(q_ref, kv_ref, o_ref):
        for h in range(XA_HEADS):
            cols = slice(h * XA_DH, (h + 1) * XA_DH)
            pr = _xa_probs(q_ref[:, cols], kv_ref[0, :, cols])
            o_ref[:, cols] = jnp.dot(pr.astype(BF16), kv_ref[1, :, cols], preferred_element_type=F32).astype(BF16)

    row = pl.BlockSpec((tm, D_MODEL), lambda i: (i, 0))
    return pl.pallas_call(
        body, name="xa_fwd", grid=(S // tm,), in_specs=[row, pl.BlockSpec((2, M, D_MODEL), lambda i: (0, 0, 0))], out_specs=row,
        out_shape=jax.ShapeDtypeStruct((S, D_MODEL), BF16), compiler_params=_cp("parallel"),
    )(q, kv)


def _xa_bwd(q, kv, do):
    S = q.shape[0]
    tm = min(XA_TM, S)
    M = kv.shape[1]

    def body(q_ref, kv_ref, do_ref, dq_ref, dkv_ref):
        @pl.when(pl.program_id(0) == 0)
        def _():
            dkv_ref[...] = jnp.zeros_like(dkv_ref)

        for h in range(XA_HEADS):
            cols = slice(h * XA_DH, (h + 1) * XA_DH)
            qh, kh, vh, doh = q_ref[:, cols], kv_ref[0, :, cols], kv_ref[1, :, cols], do_ref[:, cols]
            pr = _xa_probs(qh, kh)
            dkv_ref[1, :, cols] += lax.dot_general(pr.astype(BF16), doh, TN, preferred_element_type=F32)
            dp = lax.dot_general(doh, vh, NT, preferred_element_type=F32)
            ds = (pr * (dp - jnp.sum(dp * pr, axis=-1, keepdims=True)) * (XA_DH ** -0.5)).astype(BF16)
            dq_ref[:, cols] = jnp.dot(ds, kh, preferred_element_type=F32).astype(BF16)
            dkv_ref[0, :, cols] += lax.dot_general(ds, qh, TN, preferred_element_type=F32)

    row = pl.BlockSpec((tm, D_MODEL), lambda i: (i, 0))
    kvs = pl.BlockSpec((2, M, D_MODEL), lambda i: (0, 0, 0))
    return pl.pallas_call(
        body, name="xa_bwd", grid=(S // tm,), in_specs=[row, kvs, row], out_specs=[row, kvs],
        out_shape=[jax.ShapeDtypeStruct((S, D_MODEL), BF16), jax.ShapeDtypeStruct((2, M, D_MODEL), F32)],
        compiler_params=_cp("arbitrary"),
    )(q, kv, do)


def _swiglu_fwd(ab):
    nb, _, S, C = ab.shape
    tm = _tile(S, (512, 256))

    def body(ab_ref, o_ref):
        a = ab_ref[0]
        o_ref[...] = (a * jax.nn.sigmoid(a) * ab_ref[1]).astype(BF16)

    return pl.pallas_call(
        body, name="swiglu_fwd", grid=(nb, S // tm),
        in_specs=[pl.BlockSpec((None, 2, tm, C), lambda j, i: (j, 0, i, 0))], out_specs=pl.BlockSpec((None, tm, C), lambda j, i: (j, i, 0)),
        out_shape=jax.ShapeDtypeStruct((nb, S, C), BF16), compiler_params=_cp("parallel", "parallel"),
    )(ab)


def _swiglu_bwd(ab, dh):
    nb, _, S, C = ab.shape
    tm = _tile(S, (512, 256))

    def body(ab_ref, dh_ref, o_ref):
        a, b, d = ab_ref[0], ab_ref[1], dh_ref[...].astype(F32)
        sg = jax.nn.sigmoid(a)
        o_ref[0] = (d * b * (sg * (1.0 + a * (1.0 - sg)))).astype(BF16)
        o_ref[1] = (d * (a * sg)).astype(BF16)

    pair = pl.BlockSpec((None, 2, tm, C), lambda j, i: (j, 0, i, 0))
    return pl.pallas_call(
        body, name="swiglu_bwd", grid=(nb, S // tm), in_specs=[pair, pl.BlockSpec((None, tm, C), lambda j, i: (j, i, 0))], out_specs=pair,
        out_shape=jax.ShapeDtypeStruct(ab.shape, BF16), compiler_params=_cp("parallel", "parallel"),
    )(ab, dh)


def _reduce_adam(parts, w, m, v, *, name):
    shape = w.shape
    C = shape[-1]
    R = math.prod(shape[:-1])
    tm = _rows(R, 4 * C)
    n = len(parts)
    c1, c2 = 1.0 - ADAM_B1 ** ADAM_STEP, 1.0 - ADAM_B2 ** ADAM_STEP

    def body(*refs):
        g = refs[0][...]
        for r in refs[1:n]:
            g = g + r[...]
        w_ref, m_ref, v_ref, go, do, mo, vo = refs[n:]
        mn = ADAM_B1 * m_ref[...] + (1.0 - ADAM_B1) * g
        vn = ADAM_B2 * v_ref[...] + (1.0 - ADAM_B2) * (g * g)
        go[...] = g
        do[...] = -ADAM_LR * ((mn / c1) / (jnp.sqrt(vn / c2) + ADAM_EPS) + ADAM_WD * w_ref[...])
        mo[...] = mn
        vo[...] = vn

    row = pl.BlockSpec((tm, C), lambda i: (i, 0))
    outs = pl.pallas_call(
        body, name=name, grid=(R // tm,), in_specs=[row] * (n + 3), out_specs=[row] * 4,
        out_shape=[jax.ShapeDtypeStruct((R, C), F32)] * 4, compiler_params=_cp("parallel"),
    )(*[a.reshape(R, C) for a in (*parts, w, m, v)])
    return tuple(o.reshape(shape) for o in outs)


def _add2(a0, a1, r, sel, *, name):
    shape = r.shape
    C = shape[-1]
    R = math.prod(shape[:-1])
    tm = _rows(R, 4 * C)

    def body(s_ref, a0_ref, a1_ref, r_ref, o_ref):
        @pl.when(s_ref[0] == 0)
        def _():
            o_ref[...] = a0_ref[...] + r_ref[...]

        @pl.when(s_ref[0] != 0)
        def _():
            o_ref[...] = a1_ref[...] + r_ref[...]

    row = pl.BlockSpec((tm, C), lambda i, s: (i, 0))
    pick0 = pl.BlockSpec((tm, C), lambda i, s: (jnp.where(s[0] == 0, i, 0), 0))
    pick1 = pl.BlockSpec((tm, C), lambda i, s: (jnp.where(s[0] == 0, 0, i), 0))
    return pl.pallas_call(
        body, name=name,
        grid_spec=pltpu.PrefetchScalarGridSpec(num_scalar_prefetch=1, grid=(R // tm,), in_specs=[pick0, pick1, row], out_specs=row),
        out_shape=jax.ShapeDtypeStruct((R, C), F32), compiler_params=_cp("arbitrary"),
    )(sel.reshape(1).astype(jnp.int32), a0.reshape(R, C), a1.reshape(R, C), r.reshape(R, C)).reshape(shape)


def _sum_slots(r, *, name):
    n, shape = r.shape[0], r.shape[1:]
    C = shape[-1]
    R = math.prod(shape[:-1])
    tm = _rows(R, 4 * C * n)

    def body(r_ref, o_ref):
        acc = r_ref[0]
        for s in range(1, n):
            acc = acc + r_ref[s]
        o_ref[...] = acc

    return pl.pallas_call(
        body, name=name, grid=(R // tm,), in_specs=[pl.BlockSpec((n, tm, C), lambda i: (0, i, 0))],
        out_specs=pl.BlockSpec((tm, C), lambda i: (i, 0)), out_shape=jax.ShapeDtypeStruct((R, C), F32), compiler_params=_cp("parallel"),
    )(r.reshape(n, R, C)).reshape(shape)


def _heads(t, S):
    n = t.shape[1] // BW
    return t.reshape(S, n * SB_HEADS, SB_DH).transpose(1, 0, 2)


def _unheads(t, S):
    return t.transpose(1, 0, 2).reshape(S, BW)


def _layer_fwd(x, mem, wl):
    S = x.shape[0]
    sv = {"x": x}
    h1 = _rms_fwd(x, wl["norm_mix_g"], name="rms_mix")
    p = _mm(h1, wl["w_in"], mode="nn", name="mm_in")
    qkv = _heads(p[:, :3 * BW].astype(BF16), S)
    q, k, v = qkv[:SB_HEADS], qkv[SB_HEADS:2 * SB_HEADS], qkv[2 * SB_HEADS:]
    ya, tot = _sb_fwd(q, k, v)
    b_st = wl["b_spatial"].T
    yb = _sgu_fwd(p, wl["sgu_ln_g"], wl["sgu_ln_b"], wl["w_spatial"], b_st)
    yc = _conv_fwd(p, wl["conv_w"])
    br = jnp.stack([_unheads(ya, S), yb, yc])
    bd = _mm(br, wl["w_br"], mode="nn", a_kind="batch", b_kind="batch", name="mm_branch")
    merged = _merge_fwd(p, bd)
    x1 = _mm(merged, wl["w_sq"][0], mode="nn", res=x, name="mm_out")
    h2 = _rms_fwd(x1, wl["norm_xa_g"], name="rms_xa")
    qx = _mm(h2, wl["w_sq"][1], mode="nn", out_dtype=BF16, name="mm_q")
    mn = _rms_fwd(mem, wl["mem_norm_g"], name="rms_mem")
    kv = _mm(mn, wl["w_sq"][3:5], mode="nn", b_kind="batch", out_dtype=BF16, name="mm_kv")
    o = _xa_fwd(qx, kv)
    x2 = _mm(o, wl["w_sq"][2], mode="nn", res=x1, name="mm_o")
    h3 = _rms_fwd(x2, wl["norm_ffn_g"], name="rms_ffn")
    ab = _mm(h3, wl["w_gu"], mode="nn", b_kind="batch", name="mm_gu").reshape(N_CHIPS, 2, S, FFN_SH)
    hh = _swiglu_fwd(ab)
    x3 = _mm(hh, wl["w_dn"], mode="nn", a_kind="kchunk", b_kind="kchunk", res=x2, name="mm_down")
    sv.update(h1=h1, p=p, q=q, k=k, v=v, tot=tot, br=br, bd=bd, merged=merged, x1=x1, h2=h2, qx=qx, mn=mn, kv=kv, o=o,
              x2=x2, h3=h3, ab=ab, hh=hh, b_st=b_st)
    return x3, sv


def _layer_bwd(dx3, mem, wl, sv):
    S = dx3.shape[0]
    p = sv["p"]
    g = {}
    dhh = _mm(dx3, wl["w_dn"], mode="nt", b_kind="batch", name="mm_down_dx")
    g["w_dn"] = _mm(sv["hh"], dx3, mode="tn", a_kind="batch", name="mm_down_dw")
    dab = _swiglu_bwd(sv["ab"], dhh).reshape(2 * N_CHIPS, S, FFN_SH)
    g["w_gu"] = _mm(sv["h3"], dab, mode="tn", b_kind="batch", name="mm_gu_dw")
    dh3 = _mm(dab, wl["w_gu"], mode="nt", a_kind="kchunk", b_kind="kchunk", name="mm_gu_dx")
    dx2, g["norm_ffn_g"] = _rms_bwd(sv["x2"], wl["norm_ffn_g"], dh3, dx3, name="rms_ffn_bwd")
    do = _mm(dx2, wl["w_sq"][2], mode="nt", out_dtype=BF16, name="mm_o_dx")
    dw_o = _mm(sv["o"], dx2, mode="tn", name="mm_o_dw")
    dq, dkv = _xa_bwd(sv["qx"], sv["kv"], do)
    dw_q = _mm(sv["h2"], dq, mode="tn", name="mm_q_dw")
    dh2 = _mm(dq, wl["w_sq"][1], mode="nt", name="mm_q_dx")
    dw_kv = _mm(sv["mn"], dkv, mode="tn", b_kind="batch", name="mm_kv_dw")
    dmn = _mm(dkv, wl["w_sq"][3:5], mode="nt", a_kind="kchunk", b_kind="kchunk", name="mm_kv_dx")
    _, g["mem_norm_g"] = _rms_bwd(mem, wl["mem_norm_g"], dmn, jnp.zeros_like(mem), name="rms_mem_bwd")
    dx1, g["norm_xa_g"] = _rms_bwd(sv["x1"], wl["norm_xa_g"], dh2, dx2, name="rms_xa_bwd")
    dm = _mm(dx1, wl["w_sq"][0], mode="nt", name="mm_out_dx")
    dw_out = _mm(sv["merged"], dx1, mode="tn", name="mm_out_dw")
    g["w_sq"] = jnp.concatenate([jnp.stack([dw_out, dw_q, dw_o]), dw_kv])
    dbd, dgates = _merge_bwd(p, sv["bd"], dm)
    dbr = _mm(dbd, wl["w_br"], mode="nt", a_kind="batch", b_kind="batch", name="mm_branch_dx")
    g["w_br"] = _mm(sv["br"], dbd, mode="tn", a_kind="batch", b_kind="batch", name="mm_branch_dw")
    dya = dbr[0].astype(BF16).reshape(S, SB_HEADS, SB_DH).transpose(1, 0, 2)
    dq_h, dk_h, dv_h = _sb_bwd(sv["q"], sv["k"], sv["v"], dya, sv["tot"])
    dz, g["sgu_ln_g"], g["sgu_ln_b"], g["w_spatial"], g["b_spatial"] = _sgu_bwd(
        p, dbr[1], wl["sgu_ln_g"], wl["sgu_ln_b"], wl["w_spatial"], wl["w_spatial"].transpose(0, 2, 1), sv["b_st"])
    dcb, dcc, dcx, g["conv_w"] = _conv_bwd(p, wl["conv_w"], dbr[2])
    dp = jnp.concatenate([_unheads(dq_h, S), _unheads(dk_h, S).astype(BF16), _unheads(dv_h, S).astype(BF16),
                          dz, dcb, dcc, dcx, dgates], axis=1)
    g["w_in"] = _mm(sv["h1"], dp, mode="tn", name="mm_in_dw")
    dh1 = _mm(dp, wl["w_in"], mode="nt", name="mm_in_dx")
    dx, g["norm_mix_g"] = _rms_bwd(sv["x"], wl["norm_mix_g"], dh1, dx1, name="rms_mix_bwd")
    return dx, g


def _local_step(x, mem, target, layers, final_g):
    h, saved = x, []
    for wl in layers:
        h, sv = _layer_fwd(h, mem, wl)
        saved.append(sv)
    loss, dx, d_final = _loss_head(h, final_g, target)
    grads = [None] * len(layers)
    for l in reversed(range(len(layers))):
        dx, grads[l] = _layer_bwd(dx, mem, layers[l], saved[l])
    return loss, dx, grads, d_final


_ALL = slice(None)
CONV_ROWS = 8
_SHARD = {
    "w_in": lambda b: (_ALL, pl.ds(1792 * b, 1792)),
    "w_br": lambda b: (_ALL, _ALL, pl.ds(256 * b, 256)),
    "w_sq": lambda b: (_ALL, pl.ds(256 * b, 256), _ALL),
    "w_gu": lambda b: (b,),
    "w_dn": lambda b: (b,),
    "conv_w": lambda b: (_ALL, pl.ds(128 * b, 128)),
}
_FULL_SHAPE = {"w_in": (1024, 7168), "w_br": (3, 512, 1024), "w_sq": (5, 1024, 1024), "w_gu": (4, 2, 1024, 704),
               "w_dn": (4, 704, 1024), "conv_w": (CONV_ROWS, 512)}
_SHARD_SHAPE = {"w_in": (1024, 1792), "w_br": (3, 512, 256), "w_sq": (5, 256, 1024), "w_gu": (2, 1024, 704),
                "w_dn": (704, 1024), "conv_w": (CONV_ROWS, 128)}


def _pos():
    return lax.axis_index("x"), lax.axis_index("y"), lax.axis_index("c")


def _per_chip(fn):
    x, y, _ = _pos()
    for x0 in (0, 1):
        for y0 in (0, 1):
            @pl.when((x == x0) & (y == y0))
            def _():
                fn(x0, y0)


def _other_chips(x0, y0):
    return [(1 - x0, y0), (x0, 1 - y0), (1 - x0, 1 - y0)]


def _rcopy(src, dst, ssem, rsem, dev):
    return pltpu.make_async_remote_copy(src_ref=src, dst_ref=dst, send_sem=ssem, recv_sem=rsem, device_id=dev, device_id_type=MESH)


def _dma_sems(n):
    return pltpu.SemaphoreType.DMA((n,))


def _gather_weights(local):
    names = list(local)
    n = len(names)

    def body(*refs):
        ins, outs = refs[:n], refs[n:2 * n]
        send, recv, lsem = refs[2 * n:]
        c = lax.axis_index("c")

        def run(x0, y0):
            b0 = 2 * x0 + y0
            chips = _other_chips(x0, y0)
            shard = lambda a, layer, b: outs[a].at[(layer,) + _SHARD[names[a]](b)]
            own = [pltpu.make_async_copy(ins[a], outs[a].at[(_ALL,) + _SHARD[names[a]](b0)], lsem.at[a]) for a in range(n)]
            for cp in own:
                cp.start()
            sent = []
            for kk, (px, py) in enumerate(chips):
                for a in range(n):
                    sent.append(_rcopy(ins[a].at[c], shard(a, c, b0), send.at[6 * a + kk], recv.at[6 * a + kk], (px, py, c)))
                    sent[-1].start()
            for kk, (px, py) in enumerate(chips):
                for a in range(n):
                    landed = shard(a, c, 2 * px + py)
                    _rcopy(landed, landed, send.at[6 * a + kk], recv.at[6 * a + kk], (px, py, c)).wait_recv()
                    sent.append(_rcopy(landed, landed, send.at[6 * a + 3 + kk], recv.at[6 * a + 3 + kk], (x0, y0, 1 - c)))
                    sent[-1].start()
            for kk, (px, py) in enumerate(chips):
                for a in range(n):
                    got = shard(a, 1 - c, 2 * px + py)
                    _rcopy(got, got, send.at[6 * a + 3 + kk], recv.at[6 * a + 3 + kk], (x0, y0, 1 - c)).wait_recv()
            for cp in sent:
                cp.wait_send()
            for cp in own:
                cp.wait()

        _per_chip(run)

    outs = pl.pallas_call(
        body, name="gather_weights", in_specs=[ANY] * n, out_specs=[ANY] * n,
        out_shape=[jax.ShapeDtypeStruct((DEPTH,) + _FULL_SHAPE[nm], local[nm].dtype) for nm in names],
        scratch_shapes=[_dma_sems(6 * n), _dma_sems(6 * n), _dma_sems(n)],
    )(*[local[nm] for nm in names])
    return dict(zip(names, outs))


def _sibling_presum_exchange(g0, g1):
    names = list(g0)
    n = len(names)

    def body(*refs):
        l0, l1, outs = refs[:n], refs[n:2 * n], refs[2 * n:3 * n]
        send, recv = refs[3 * n:]
        x, y, c = _pos()
        for c0 in (0, 1):
            @pl.when(c == c0)
            def _():
                srcs = l1 if c0 == 0 else l0
                cps = [_rcopy(srcs[a], outs[a], send.at[a], recv.at[a], (x, y, 1 - c0)) for a in range(n)]
                for cp in cps:
                    cp.start()
                for cp in cps:
                    cp.wait()

    outs = pl.pallas_call(
        body, name="grad_presum_exchange", in_specs=[ANY] * (2 * n), out_specs=[ANY] * n,
        out_shape=[jax.ShapeDtypeStruct(g0[nm].shape, g0[nm].dtype) for nm in names],
        scratch_shapes=[_dma_sems(n), _dma_sems(n)],
    )(*[g0[nm] for nm in names], *[g1[nm] for nm in names])
    return dict(zip(names, outs))


def _shard_exchange(part):
    names = list(part)
    n = len(names)

    def body(*refs):
        ins, outs = refs[:n], refs[n:2 * n]
        send, recv, lsem = refs[2 * n:]
        c = lax.axis_index("c")

        def run(x0, y0):
            chips = _other_chips(x0, y0)
            own = [pltpu.make_async_copy(ins[a].at[_SHARD[names[a]](2 * x0 + y0)], outs[a].at[3], lsem.at[a]) for a in range(n)]
            for cp in own:
                cp.start()
            sent = []
            for kk, (px, py) in enumerate(chips):
                for a in range(n):
                    sent.append(_rcopy(ins[a].at[_SHARD[names[a]](2 * px + py)], outs[a].at[kk], send.at[3 * a + kk], recv.at[3 * a + kk], (px, py, c)))
                    sent[-1].start()
            for kk, (px, py) in enumerate(chips):
                for a in range(n):
                    slot = outs[a].at[kk]
                    _rcopy(slot, slot, send.at[3 * a + kk], recv.at[3 * a + kk], (px, py, c)).wait_recv()
            for cp in sent:
                cp.wait_send()
            for cp in own:
                cp.wait()

        _per_chip(run)

    outs = pl.pallas_call(
        body, name="grad_shard_exchange", in_specs=[ANY] * n, out_specs=[ANY] * n,
        out_shape=[jax.ShapeDtypeStruct((N_CHIPS,) + _SHARD_SHAPE[nm], part[nm].dtype) for nm in names],
        scratch_shapes=[_dma_sems(3 * n), _dma_sems(3 * n), _dma_sems(n)],
    )(*[part[nm] for nm in names])
    return dict(zip(names, outs))


def _sibling_exchange(red):
    names = list(red)
    n = len(names)

    def body(*refs):
        ins, outs = refs[:n], refs[n:2 * n]
        send, recv, lsem = refs[2 * n:]
        x, y, c = _pos()
        own = [pltpu.make_async_copy(ins[a], outs[a].at[c], lsem.at[a]) for a in range(n)]
        cps = [_rcopy(ins[a], outs[a].at[c], send.at[a], recv.at[a], (x, y, 1 - c)) for a in range(n)]
        for cp in own + cps:
            cp.start()
        for a in range(n):
            got = outs[a].at[1 - c]
            _rcopy(got, got, send.at[a], recv.at[a], (x, y, 1 - c)).wait_recv()
        for cp in cps:
            cp.wait_send()
        for cp in own:
            cp.wait()

    outs = pl.pallas_call(
        body, name="grad_sibling_exchange", in_specs=[ANY] * n, out_specs=[ANY] * n,
        out_shape=[jax.ShapeDtypeStruct((DEPTH,) + red[nm].shape, red[nm].dtype) for nm in names],
        scratch_shapes=[_dma_sems(n), _dma_sems(n), _dma_sems(n)],
    )(*[red[nm] for nm in names])
    return dict(zip(names, outs))


def _gather_small(pack):
    flips = [(fx, fy, fc) for fx in (0, 1) for fy in (0, 1) for fc in (0, 1) if fx or fy or fc]

    def body(in_ref, out_ref, send, recv, lsem):
        x, y, c = _pos()
        me = 4 * x + 2 * y + c
        own = pltpu.make_async_copy(in_ref, out_ref.at[me], lsem.at[0])
        own.start()
        peers = [(x ^ fx, y ^ fy, c ^ fc) for fx, fy, fc in flips]
        cps = [_rcopy(in_ref, out_ref.at[me], send.at[k], recv.at[k], peer) for k, peer in enumerate(peers)]
        for cp in cps:
            cp.start()
        for k, (px, py, pc) in enumerate(peers):
            slot = out_ref.at[4 * px + 2 * py + pc]
            _rcopy(slot, slot, send.at[k], recv.at[k], (px, py, pc)).wait_recv()
        for cp in cps:
            cp.wait_send()
        own.wait()

    return pl.pallas_call(
        body, name="gather_small_grads", in_specs=[ANY], out_specs=ANY,
        out_shape=jax.ShapeDtypeStruct((8,) + pack.shape, pack.dtype),
        scratch_shapes=[_dma_sems(len(flips)), _dma_sems(len(flips)), _dma_sems(1)],
    )(pack)


_WEIGHTS = ["norm_mix_g", "w_in", "sgu_ln_g", "sgu_ln_b", "w_spatial", "b_spatial", "conv_w", "w_branch", "w_out", "norm_xa_g",
            "mem_norm_g", "w_q_xa", "w_k_xa", "w_v_xa", "w_o_xa", "norm_ffn_g", "w_gate_ffn", "w_up_ffn", "w_down_ffn", "final_g"]
_REPLICATED = ["norm_mix_g", "sgu_ln_g", "sgu_ln_b", "w_spatial", "b_spatial", "norm_xa_g", "mem_norm_g", "norm_ffn_g", "final_g"]
_SQUARE = ["w_out", "w_q_xa", "w_o_xa", "w_k_xa", "w_v_xa"]
_BIG = ["w_in", "w_br", "w_sq", "w_gu", "w_dn"]


def _pack(arrs):
    return jnp.concatenate([a.reshape(-1) for a in arrs]).reshape(-1, 128)


def _step(a):
    w = {n: a[n] for n in _WEIGHTS}
    x, mem, target = a["x"][0], a["mem"][0], a["loss_target"][0]
    xi, yi, ci = _pos()
    bf = lambda n: w[n].astype(BF16)
    full = _gather_weights({
        "w_in": bf("w_in"), "w_br": bf("w_branch"), "w_sq": jnp.stack([bf(n) for n in _SQUARE], axis=1),
        "w_gu": jnp.stack([bf("w_gate_ffn"), bf("w_up_ffn")], axis=1), "w_dn": bf("w_down_ffn"),
        "conv_w": jnp.pad(w["conv_w"], ((0, 0), (0, CONV_ROWS - 3), (0, 0)))})
    layers = []
    for l in range(DEPTH):
        wl = {n: full[n][l] for n in full}
        wl["conv_w"] = wl["conv_w"][:3]
        wl["w_gu"] = wl["w_gu"].reshape(2 * N_CHIPS, D_MODEL, FFN_SH)
        wl.update({n: w[n][l] for n in _REPLICATED if n != "final_g"})
        layers.append(wl)
    loss, dx, grads, d_final = _local_step(x, mem, target, layers, w["final_g"])
    loss = lax.psum(loss, ("x", "y", "c"))

    for g in grads:
        g["w_gu"] = g["w_gu"].reshape(_FULL_SHAPE["w_gu"])
    g0, g1 = ({n: g[n] for n in _BIG} for g in grads)
    sib = _sibling_presum_exchange(g0, g1)
    part = {n: _add2(g0[n], g1[n], sib[n], ci, name="presum_" + n) for n in _BIG}
    slots = _shard_exchange(part)
    red = _sibling_exchange({n: _sum_slots(slots[n], name="sum_chips_" + n) for n in _BIG})

    out = {}

    def adam(name, g):
        out[name] = _reduce_adam([g], w[name], a["m_" + name], a["v_" + name], name="adam_" + name)

    adam("w_in", red["w_in"])
    adam("w_branch", red["w_br"])
    for t, n in enumerate(_SQUARE):
        adam(n, red["w_sq"][:, t])
    adam("w_gate_ffn", red["w_gu"][:, 0])
    adam("w_up_ffn", red["w_gu"][:, 1])
    adam("w_down_ffn", red["w_dn"])

    small = {n: jnp.stack([g[n] for g in grads]) for n in _REPLICATED if n != "final_g"}
    small["final_g"] = d_final
    conv_g = jnp.stack([g["conv_w"] for g in grads])
    n_rep = sum(w[n].size for n in _REPLICATED) // 128
    summed = _sum_slots(_gather_small(_pack([small[n] for n in _REPLICATED] + [conv_g])), name="sum_devices_small")
    res = _reduce_adam([summed[:n_rep]], _pack([w[n] for n in _REPLICATED]), _pack([a["m_" + n] for n in _REPLICATED]),
                       _pack([a["v_" + n] for n in _REPLICATED]), name="adam_replicated")
    off = 0
    for n in _REPLICATED:
        out[n] = tuple(r.reshape(-1)[off:off + w[n].size].reshape(w[n].shape) for r in res)
        off += w[n].size
    conv_full = summed[n_rep:].reshape(conv_g.shape)
    adam("conv_w", lax.dynamic_slice_in_dim(conv_full, (2 * xi + yi) * 128, 128, axis=2))

    return (loss, dx[None], *[out[n][k] for k in range(4) for n in _WEIGHTS])


def kernel(x, mem, norm_mix_g, w_in, sgu_ln_g, sgu_ln_b, w_spatial, b_spatial, conv_w, w_branch, w_out, norm_xa_g, mem_norm_g, w_q_xa, w_k_xa, w_v_xa, w_o_xa, norm_ffn_g, w_gate_ffn, w_up_ffn, w_down_ffn, final_g, loss_target, m_norm_mix_g, m_w_in, m_sgu_ln_g, m_sgu_ln_b, m_w_spatial, m_b_spatial, m_conv_w, m_w_branch, m_w_out, m_norm_xa_g, m_mem_norm_g, m_w_q_xa, m_w_k_xa, m_w_v_xa, m_w_o_xa, m_norm_ffn_g, m_w_gate_ffn, m_w_up_ffn, m_w_down_ffn, m_final_g, v_norm_mix_g, v_w_in, v_sgu_ln_g, v_sgu_ln_b, v_w_spatial, v_b_spatial, v_conv_w, v_w_branch, v_w_out, v_norm_xa_g, v_mem_norm_g, v_w_q_xa, v_w_k_xa, v_w_v_xa, v_w_o_xa, v_norm_ffn_g, v_w_gate_ffn, v_w_up_ffn, v_w_down_ffn, v_final_g):
    return _step(dict(locals()))
```

```python
import functools
import math

import jax
import jax.numpy as jnp
from jax import lax
from jax.experimental import pallas as pl
from jax.experimental.pallas import tpu as pltpu

F32, BF16 = jnp.float32, jnp.bfloat16
MESH = pl.DeviceIdType.MESH
ANY = pl.BlockSpec(memory_space=pl.ANY)

D_MODEL = 1024
DEPTH = 2
BW = 512
SB_HEADS, SB_DH = 8, 64
SGU_LEN, SGU_GROUPS, SGU_GD, SGU_CHUNK = 128, 4, 128, 64
XA_HEADS, XA_DH = 4, 256
FFN_SH = 704
N_CHIPS = 4
IN_COLS = 7168
C_Z, C_CB, C_GATES = 1536, 2560, 4096

ADAM_LR, ADAM_B1, ADAM_B2, ADAM_EPS, ADAM_WD, ADAM_STEP = 0.001, 0.9, 0.999, 1e-08, 0.01, 10

VMEM_LIMIT_V7X = 56 * 1024 * 1024

NN = (((1,), (0,)), ((), ()))
NT = (((1,), (1,)), ((), ()))
TN = (((0,), (0,)), ((), ()))


def _cp(*sem):
    return pltpu.CompilerParams(dimension_semantics=sem, vmem_limit_bytes=VMEM_LIMIT_V7X)


def _tile(n, pref):
    for t in pref:
        if n % t == 0:
            return t
    return n


def _rows(r, row_bytes, block_bytes=1 << 20):
    for t in (1024, 512, 256, 128, 64, 32, 16, 8):
        if r % t == 0 and t * row_bytes <= block_bytes:
            return t
    return r


def _mm(a, b, *, mode, name, out_dtype=F32, res=None, a_kind="2d", b_kind="2d", tm=None, tn=None, tk=None):
    a2, b2 = a.shape[-2:], b.shape[-2:]
    if mode == "nn":
        (M, K), N = a2, b2[1]
    elif mode == "nt":
        (M, K), N = a2, b2[0]
    else:
        (K, M), N = a2, b2[1]
    kchunk = a_kind == "kchunk" or b_kind == "kchunk"
    batch = a_kind == "batch" or b_kind == "batch"
    G = (a.shape[0] if a_kind == "batch" else b.shape[0]) if batch else 1
    tm = tm or _tile(M, (1024, 512, 256, 128))
    tn = tn or _tile(N, (1024, 512, 256, 128))
    if kchunk:
        tk, nk = K, (a.shape[0] if a_kind == "kchunk" else b.shape[0])
    else:
        tk = tk or _tile(K, (1024, 512, 256, 128))
        nk = K // tk

    def spec(kind, blk, idx):
        if kind == "2d":
            return pl.BlockSpec(blk, lambda g, i, j, k: idx(g, i, j, k))
        if kind == "batch":
            return pl.BlockSpec((None,) + blk, lambda g, i, j, k: (g,) + idx(g, i, j, k))
        return pl.BlockSpec((None,) + blk, lambda g, i, j, k: (k,) + idx(g, i, j, 0))

    if mode == "nn":
        a_spec = spec(a_kind, (tm, tk), lambda g, i, j, k: (i, k))
        b_spec = spec(b_kind, (tk, tn), lambda g, i, j, k: (k, j))
    elif mode == "nt":
        a_spec = spec(a_kind, (tm, tk), lambda g, i, j, k: (i, k))
        b_spec = spec(b_kind, (tn, tk), lambda g, i, j, k: (j, k))
    else:
        a_spec = spec(a_kind, (tk, tm), lambda g, i, j, k: (k, i))
        b_spec = spec(b_kind, (tk, tn), lambda g, i, j, k: (k, j))
    o_kind = "batch" if batch else "2d"
    o_spec = spec(o_kind, (tm, tn), lambda g, i, j, k: (i, j))
    o_shape = ((G,) if batch else ()) + (M, N)
    dn = {"nn": NN, "nt": NT, "tn": TN}[mode]
    has_res = res is not None

    def body(*refs):
        if has_res:
            a_ref, b_ref, r_ref, o_ref = refs[:4]
        else:
            a_ref, b_ref, o_ref = refs[:3]
        p = lax.dot_general(a_ref[...].astype(BF16), b_ref[...].astype(BF16), dn, preferred_element_type=F32)

        def finish(r):
            if has_res:
                r = r + r_ref[...]
            o_ref[...] = r.astype(out_dtype)

        if nk == 1:
            finish(p)
        else:
            acc = refs[-1]
            k = pl.program_id(3)

            @pl.when(k == 0)
            def _():
                acc[...] = p

            @pl.when(k > 0)
            def _():
                acc[...] += p

            @pl.when(k == nk - 1)
            def _():
                finish(acc[...])

    in_specs, args = [a_spec, b_spec], [a, b]
    if has_res:
        in_specs.append(spec("2d", (tm, tn), lambda g, i, j, k: (i, j)))
        args.append(res)
    return pl.pallas_call(
        body, name=name, grid=(G, M // tm, N // tn, nk), in_specs=in_specs, out_specs=o_spec,
        out_shape=jax.ShapeDtypeStruct(o_shape, out_dtype),
        scratch_shapes=[pltpu.VMEM((tm, tn), F32)] if nk > 1 else [],
        compiler_params=_cp("parallel", "parallel", "parallel", "arbitrary"),
    )(*args)


def _rms_fwd(x, g, *, name):
    S, Dm = x.shape
    tm = _tile(S, (512, 256))

    def body(x_ref, g_ref, o_ref):
        xv = x_ref[...]
        r = lax.rsqrt(jnp.mean(xv * xv, axis=-1, keepdims=True) + 1e-6)
        o_ref[...] = (xv * r * g_ref[...]).astype(BF16)

    return pl.pallas_call(
        body, name=name, grid=(S // tm,),
        in_specs=[pl.BlockSpec((tm, Dm), lambda i: (i, 0)), pl.BlockSpec((1, Dm), lambda i: (0, 0))],
        out_specs=pl.BlockSpec((tm, Dm), lambda i: (i, 0)), out_shape=jax.ShapeDtypeStruct((S, Dm), BF16),
        compiler_params=_cp("parallel"),
    )(x, g.reshape(1, Dm))


def _rms_bwd(x, g, dh, dres, *, name):
    S, Dm = x.shape
    tm = _tile(S, (512, 256))

    def body(x_ref, g_ref, dh_ref, dr_ref, dx_ref, dg_ref):
        xv, dhv = x_ref[...], dh_ref[...].astype(F32)
        r = lax.rsqrt(jnp.mean(xv * xv, axis=-1, keepdims=True) + 1e-6)
        u = dhv * g_ref[...]
        s = jnp.sum(u * xv, axis=-1, keepdims=True)
        dx_ref[...] = dr_ref[...] + r * u - xv * ((r * r * r) * (s * (1.0 / Dm)))
        part = jnp.sum(dhv * (xv * r), axis=0, keepdims=True)

        @pl.when(pl.program_id(0) == 0)
        def _():
            dg_ref[...] = part

        @pl.when(pl.program_id(0) > 0)
        def _():
            dg_ref[...] += part

    row = pl.BlockSpec((tm, Dm), lambda i: (i, 0))
    vec = pl.BlockSpec((1, Dm), lambda i: (0, 0))
    dx, dg = pl.pallas_call(
        body, name=name, grid=(S // tm,), in_specs=[row, vec, row, row], out_specs=[row, vec],
        out_shape=[jax.ShapeDtypeStruct((S, Dm), F32), jax.ShapeDtypeStruct((1, Dm), F32)],
        compiler_params=_cp("arbitrary"),
    )(x, g.reshape(1, Dm), dh, dres)
    return dx, dg.reshape(Dm)


def _loss_head(x, g, target):
    S, Dm = x.shape
    tm = _tile(S, (512, 256))

    def body(x_ref, g_ref, t_ref, dx_ref, dg_ref, loss_ref):
        xv, gv = x_ref[...], g_ref[...]
        r = lax.rsqrt(jnp.mean(xv * xv, axis=-1, keepdims=True) + 1e-6)
        xn = xv * r
        err = xn * gv - t_ref[...]
        lpart = 0.5 * jnp.sum(jnp.mean(err * err, axis=-1, keepdims=True), axis=0, keepdims=True)
        dy = err * (1.0 / Dm)
        u = dy * gv
        s = jnp.sum(u * xv, axis=-1, keepdims=True)
        dx_ref[...] = r * u - xv * ((r * r * r) * (s * (1.0 / Dm)))
        part = jnp.sum(dy * xn, axis=0, keepdims=True)
        lslab = jnp.broadcast_to(lpart, (8, 128))

        @pl.when(pl.program_id(0) == 0)
        def _():
            dg_ref[...] = part
            loss_ref[...] = lslab

        @pl.when(pl.program_id(0) > 0)
        def _():
            dg_ref[...] += part
            loss_ref[...] += lslab

    row = pl.BlockSpec((tm, Dm), lambda i: (i, 0))
    vec = pl.BlockSpec((1, Dm), lambda i: (0, 0))
    dx, dg, loss = pl.pallas_call(
        body, name="loss_head", grid=(S // tm,), in_specs=[row, vec, row],
        out_specs=[row, vec, pl.BlockSpec((8, 128), lambda i: (0, 0))],
        out_shape=[jax.ShapeDtypeStruct((S, Dm), F32), jax.ShapeDtypeStruct((1, Dm), F32), jax.ShapeDtypeStruct((8, 128), F32)],
        compiler_params=_cp("arbitrary"),
    )(x, g.reshape(1, Dm), target)
    return loss[0, 0], dx, dg.reshape(Dm)


SB_TQ, SB_TK = 256, 128
SB_EXP_FLOOR = -104.0


def _split2(v):
    hi = v.astype(BF16)
    return jnp.concatenate([hi, (v - hi.astype(F32)).astype(BF16)], axis=1)


def _tri2(cmp):
    j = lax.broadcasted_iota(jnp.int32, (2 * SB_TK, SB_TK), 0) % SB_TK
    s = lax.broadcasted_iota(jnp.int32, (2 * SB_TK, SB_TK), 1)
    return cmp(j, s).astype(BF16)


def _sb_scores(qv, kb, k0, q0, tq):
    z = lax.dot_general(qv, kb, NT, preferred_element_type=F32) * (SB_DH ** -0.5)
    t_pos = q0 + lax.broadcasted_iota(jnp.int32, (tq, SB_TK), 0)
    s_pos = k0 + lax.broadcasted_iota(jnp.int32, (tq, SB_TK), 1)
    valid = s_pos < t_pos
    ls = jnp.minimum(z, 0.0) - jnp.log1p(jnp.exp(-jnp.abs(z)))
    l1m = jnp.where(valid, ls - z, 0.0)
    return z, valid, ls, l1m


def _sb_fwd(q, k, v):
    H, S, dh = q.shape
    tq = min(SB_TQ, S)
    kb_per_q = tq // SB_TK

    def body(q_ref, k_ref, v_ref, o_ref, tot_ref, cnt_ref, acc, c):
        i = pl.program_id(1)
        q0 = i * tq
        qv = q_ref[...]
        later = _tri2(lambda j, s: j > s)
        acc[...] = jnp.zeros_like(acc)
        c[...] = jnp.zeros_like(c)
        nkb = (i + 1) * kb_per_q

        def more(st):
            n, highest = st
            return (n < nkb) & (highest > SB_EXP_FLOOR)

        def step(st):
            n, _ = st
            k0 = pl.multiple_of((nkb - 1 - n) * SB_TK, SB_TK)
            kb, vb = k_ref[pl.ds(k0, SB_TK), :], v_ref[pl.ds(k0, SB_TK), :]
            z, valid, ls, l1m = _sb_scores(qv, kb, k0, q0, tq)
            after = jnp.dot(_split2(l1m), later, preferred_element_type=F32)
            a = jnp.where(valid, jnp.exp(ls + after + c[...]), 0.0)
            acc[...] += jnp.dot(a.astype(BF16), vb, preferred_element_type=F32)
            cn = c[...] + jnp.sum(l1m, axis=1, keepdims=True)
            c[...] = cn
            return n + 1, jnp.max(cn)

        n_done, _ = lax.while_loop(more, step, (jnp.int32(0), jnp.float32(0.0)))
        o_ref[...] = acc[...].astype(o_ref.dtype)
        tot_ref[...] = c[...]
        cnt_ref[...] = jnp.full(cnt_ref.shape, n_done.astype(F32))

    qs = pl.BlockSpec((None, tq, dh), lambda h, i: (h, i, 0))
    full = pl.BlockSpec((None, S, dh), lambda h, i: (h, 0, 0))
    return pl.pallas_call(
        body, name="sb_fwd", grid=(H, S // tq), in_specs=[qs, full, full],
        out_specs=[qs, pl.BlockSpec((None, tq, 1), lambda h, i: (h, i, 0)), pl.BlockSpec((None, None, 8, 128), lambda h, i: (h, i, 0, 0))],
        out_shape=[jax.ShapeDtypeStruct((H, S, dh), BF16), jax.ShapeDtypeStruct((H, S, 1), F32),
                   jax.ShapeDtypeStruct((H, S // tq, 8, 128), F32)],
        scratch_shapes=[pltpu.VMEM((tq, dh), F32), pltpu.VMEM((tq, 1), F32)],
        compiler_params=_cp("parallel", "parallel"),
    )(q, k, v)


def _sb_bwd(q, k, v, do, tot, cnt):
    H, S, dh = q.shape
    tq = min(SB_TQ, S)
    kb_per_q = tq // SB_TK
    scale = SB_DH ** -0.5

    def body(q_ref, k_ref, v_ref, do_ref, tot_ref, cnt_ref, dq_ref, dk_ref, dv_ref, dq_acc, pre, gpre):
        i = pl.program_id(1)
        q0 = i * tq
        qv, dov, totv = q_ref[...], do_ref[...], tot_ref[...]
        upto = _tri2(lambda j, s: j <= s)
        before = _tri2(lambda j, s: j < s)

        @pl.when(i == 0)
        def _():
            dk_ref[...] = jnp.zeros_like(dk_ref)
            dv_ref[...] = jnp.zeros_like(dv_ref)

        dq_acc[...] = jnp.zeros_like(dq_acc)
        pre[...] = jnp.zeros_like(pre)
        gpre[...] = jnp.zeros_like(gpre)

        n_done = jnp.max(cnt_ref[...]).astype(jnp.int32)
        first = (i + 1) * kb_per_q - n_done

        def step(n, carry):
            k0 = pl.multiple_of((first + n) * SB_TK, SB_TK)
            kb, vb = k_ref[pl.ds(k0, SB_TK), :], v_ref[pl.ds(k0, SB_TK), :]
            z, valid, ls, l1m = _sb_scores(qv, kb, k0, q0, tq)
            incl = jnp.dot(_split2(l1m), upto, preferred_element_type=F32)
            rest = totv - (pre[...] + incl)
            a = jnp.where(valid, jnp.exp(ls + rest), 0.0)
            da = lax.dot_general(dov, vb, NT, preferred_element_type=F32)
            g = a * da
            gbefore = jnp.dot(_split2(g), before, preferred_element_type=F32) + gpre[...]
            dz = jnp.where(valid, g * jnp.exp(ls - z) - jnp.exp(ls) * gbefore, 0.0) * scale
            dzb = dz.astype(BF16)
            dq_acc[...] += jnp.dot(dzb, kb, preferred_element_type=F32)
            dk_ref[pl.ds(k0, SB_TK), :] += lax.dot_general(dzb, qv, TN, preferred_element_type=F32)
            dv_ref[pl.ds(k0, SB_TK), :] += lax.dot_general(a.astype(BF16), dov, TN, preferred_element_type=F32)
            pre[...] += jnp.sum(l1m, axis=1, keepdims=True)
            gpre[...] += jnp.sum(g, axis=1, keepdims=True)
            return carry

        lax.fori_loop(0, n_done, step, 0)
        dq_ref[...] = dq_acc[...].astype(dq_ref.dtype)

    qs = pl.BlockSpec((None, tq, dh), lambda h, i: (h, i, 0))
    full = pl.BlockSpec((None, S, dh), lambda h, i: (h, 0, 0))
    return pl.pallas_call(
        body, name="sb_bwd", grid=(H, S // tq),
        in_specs=[qs, full, full, qs, pl.BlockSpec((None, tq, 1), lambda h, i: (h, i, 0)),
                  pl.BlockSpec((None, None, 8, 128), lambda h, i: (h, i, 0, 0))],
        out_specs=[qs, full, full],
        out_shape=[jax.ShapeDtypeStruct((H, S, dh), BF16), jax.ShapeDtypeStruct((H, S, dh), F32), jax.ShapeDtypeStruct((H, S, dh), F32)],
        scratch_shapes=[pltpu.VMEM((tq, dh), F32), pltpu.VMEM((tq, 1), F32), pltpu.VMEM((tq, 1), F32)],
        compiler_params=_cp("parallel", "arbitrary"),
    )(q, k, v, do, tot, cnt)


_INV_SQRT2 = 0.7071067811865476
_INV_SQRT2PI = 0.3989422804014327


def _gelu(x):
    return 0.5 * x * (1.0 + lax.erf(x * _INV_SQRT2))


def _gelu_grad(x):
    return 0.5 * (1.0 + lax.erf(x * _INV_SQRT2)) + x * (_INV_SQRT2PI * jnp.exp(-0.5 * x * x))


def _sgu_mask():
    t = lax.broadcasted_iota(jnp.int32, (SGU_LEN, SGU_LEN), 0) // SGU_CHUNK
    s = lax.broadcasted_iota(jnp.int32, (SGU_LEN, SGU_LEN), 1) // SGU_CHUNK
    return t >= s


def _sgu_mask_t():
    t = lax.broadcasted_iota(jnp.int32, (SGU_LEN, SGU_LEN), 0) // SGU_CHUNK
    s = lax.broadcasted_iota(jnp.int32, (SGU_LEN, SGU_LEN), 1) // SGU_CHUNK
    return s >= t


def _sgu_norm(zv, g, b):
    vv = _gelu(zv)
    xc = vv - jnp.mean(vv, axis=-1, keepdims=True)
    rstd = lax.rsqrt(jnp.mean(xc * xc, axis=-1, keepdims=True) + 1e-5)
    xhat = xc * rstd
    return xhat, rstd, xhat * g + b


SGU_TM = 256


def _sgu_fwd(p, ln_g, ln_b, w_s, b_st):
    S = p.shape[0]
    tm = min(SGU_TM, S)

    def body(zu_ref, zv_ref, g_ref, b_ref, w_ref, bs_ref, o_ref):
        u = _gelu(zu_ref[...])
        _, _, vn = _sgu_norm(zv_ref[...], g_ref[...], b_ref[...])
        vnb = vn.astype(BF16)
        mask = _sgu_mask()
        for gi in range(SGU_GROUPS):
            wg = jnp.where(mask, w_ref[gi], 0.0).astype(BF16)
            cols = slice(gi * SGU_GD, (gi + 1) * SGU_GD)
            for ci in range(tm // SGU_LEN):
                rows = slice(ci * SGU_LEN, (ci + 1) * SGU_LEN)
                vm = jnp.dot(wg, vnb[rows, cols], preferred_element_type=F32) + bs_ref[:, gi:gi + 1]
                o_ref[rows, cols] = (u[rows, cols] * vm).astype(BF16)

    vec = pl.BlockSpec((1, BW), lambda i: (0, 0))
    return pl.pallas_call(
        body, name="sgu_fwd", grid=(S // tm,),
        in_specs=[pl.BlockSpec((tm, BW), lambda i: (i, C_Z // BW)), pl.BlockSpec((tm, BW), lambda i: (i, C_Z // BW + 1)), vec, vec,
                  pl.BlockSpec((SGU_GROUPS, SGU_LEN, SGU_LEN), lambda i: (0, 0, 0)), pl.BlockSpec((SGU_LEN, SGU_GROUPS), lambda i: (0, 0))],
        out_specs=pl.BlockSpec((tm, BW), lambda i: (i, 0)), out_shape=jax.ShapeDtypeStruct((S, BW), BF16),
        compiler_params=_cp("parallel"),
    )(p, p, ln_g.reshape(1, BW), ln_b.reshape(1, BW), w_s, b_st)


def _sgu_bwd(p, dyb, ln_g, ln_b, w_s, w_st, b_st):
    S = p.shape[0]
    tm = min(SGU_TM, S)

    def body(zu_ref, zv_ref, dy_ref, g_ref, b_ref, w_ref, wt_ref, bs_ref, dz_ref, dg_ref, db_ref, dw_ref, dbs_ref, dvn):
        first = pl.program_id(0) == 0

        @pl.when(first)
        def _():
            dg_ref[...] = jnp.zeros_like(dg_ref)
            db_ref[...] = jnp.zeros_like(db_ref)
            dw_ref[...] = jnp.zeros_like(dw_ref)
            dbs_ref[...] = jnp.zeros_like(dbs_ref)

        zu, zv, dy = zu_ref[...], zv_ref[...], dy_ref[...].astype(F32)
        u = _gelu(zu)
        xhat, rstd, vn = _sgu_norm(zv, g_ref[...], b_ref[...])
        vnb = vn.astype(BF16)
        mask = _sgu_mask()
        mask_t = _sgu_mask_t()
        for gi in range(SGU_GROUPS):
            wg = jnp.where(mask, w_ref[gi], 0.0).astype(BF16)
            wgt = jnp.where(mask_t, wt_ref[gi], 0.0).astype(BF16)
            cols = slice(gi * SGU_GD, (gi + 1) * SGU_GD)
            for ci in range(tm // SGU_LEN):
                rows = slice(ci * SGU_LEN, (ci + 1) * SGU_LEN)
                vm = jnp.dot(wg, vnb[rows, cols], preferred_element_type=F32) + bs_ref[:, gi:gi + 1]
                dyc = dy[rows, cols]
                dz_ref[rows, cols] = (dyc * vm * _gelu_grad(zu[rows, cols])).astype(BF16)
                dvm = dyc * u[rows, cols]
                dvmb = dvm.astype(BF16)
                dbs_ref[gi] += jnp.broadcast_to(jnp.sum(dvm, axis=1, keepdims=True), (SGU_LEN, SGU_GD))
                dw_ref[gi] += lax.dot_general(dvmb, vnb[rows, cols], NT, preferred_element_type=F32)
                dvn[rows, cols] = jnp.dot(wgt, dvmb, preferred_element_type=F32)
        dvnv = dvn[...]
        dg_ref[...] += jnp.sum(dvnv * xhat, axis=0, keepdims=True)
        db_ref[...] += jnp.sum(dvnv, axis=0, keepdims=True)
        dxh = dvnv * g_ref[...]
        dvv = rstd * (dxh - jnp.mean(dxh, axis=-1, keepdims=True) - xhat * jnp.mean(dxh * xhat, axis=-1, keepdims=True))
        dz_ref[:, BW:] = (dvv * _gelu_grad(zv)).astype(BF16)

        @pl.when(pl.program_id(0) == n_steps - 1)
        def _():
            for gi in range(SGU_GROUPS):
                dw_ref[gi] = jnp.where(mask, dw_ref[gi], 0.0)

    n_steps = S // tm
    vec = pl.BlockSpec((1, BW), lambda i: (0, 0))
    half = lambda c: pl.BlockSpec((tm, BW), lambda i: (i, c))
    wspec = pl.BlockSpec((SGU_GROUPS, SGU_LEN, SGU_LEN), lambda i: (0, 0, 0))
    dz, dg, db, dw, dbs = pl.pallas_call(
        body, name="sgu_bwd", grid=(n_steps,),
        in_specs=[half(C_Z // BW), half(C_Z // BW + 1), half(0), vec, vec, wspec, wspec,
                  pl.BlockSpec((SGU_LEN, SGU_GROUPS), lambda i: (0, 0))],
        out_specs=[pl.BlockSpec((tm, 2 * BW), lambda i: (i, 0)), vec, vec, wspec, wspec],
        out_shape=[jax.ShapeDtypeStruct((S, 2 * BW), BF16), jax.ShapeDtypeStruct((1, BW), F32), jax.ShapeDtypeStruct((1, BW), F32),
                   jax.ShapeDtypeStruct((SGU_GROUPS, SGU_LEN, SGU_LEN), F32), jax.ShapeDtypeStruct((SGU_GROUPS, SGU_LEN, SGU_GD), F32)],
        scratch_shapes=[pltpu.VMEM((tm, BW), F32)],
        compiler_params=_cp("arbitrary"),
    )(p, p, dyb, ln_g.reshape(1, BW), ln_b.reshape(1, BW), w_s, w_st, b_st)
    return dz, dg.reshape(BW), db.reshape(BW), dw, dbs[:, :, 0]


CONV_TC = 128


def _shift_down(y, n):
    rows = lax.broadcasted_iota(jnp.int32, y.shape, 0)
    return jnp.where(rows < n, 0.0, pltpu.roll(y, n, 0))


def _shift_up(y, n):
    rows = lax.broadcasted_iota(jnp.int32, y.shape, 0)
    return jnp.where(rows >= y.shape[0] - n, 0.0, pltpu.roll(y, y.shape[0] - n, 0))


def _conv_specs(S):
    col = lambda c0: pl.BlockSpec((S, CONV_TC), lambda j: (0, c0 // CONV_TC + j))
    return col(C_CB), col(C_CB + BW), col(C_CB + 2 * BW), pl.BlockSpec((3, CONV_TC), lambda j: (0, j)), pl.BlockSpec((S, CONV_TC), lambda j: (0, j))


def _conv_fwd(p, conv_w):
    S = p.shape[0]

    def body(cb_ref, cc_ref, cx_ref, w_ref, o_ref):
        y = cc_ref[...] * cx_ref[...]
        conv = w_ref[0:1, :] * _shift_down(y, 2) + w_ref[1:2, :] * _shift_down(y, 1) + w_ref[2:3, :] * y
        o_ref[...] = (cb_ref[...] * conv).astype(BF16)

    cb, cc, cx, wspec, out = _conv_specs(S)
    return pl.pallas_call(
        body, name="conv_fwd", grid=(BW // CONV_TC,), in_specs=[cb, cc, cx, wspec], out_specs=out,
        out_shape=jax.ShapeDtypeStruct((S, BW), BF16), compiler_params=_cp("parallel"),
    )(p, p, p, conv_w)


def _conv_bwd(p, conv_w, dyc):
    S = p.shape[0]

    def body(cb_ref, cc_ref, cx_ref, w_ref, dy_ref, db_ref, dc_ref, dx_ref, dw_ref):
        cc, cx, dy = cc_ref[...], cx_ref[...], dy_ref[...].astype(F32)
        y = cc * cx
        w0, w1, w2 = w_ref[0:1, :], w_ref[1:2, :], w_ref[2:3, :]
        y1, y2 = _shift_down(y, 1), _shift_down(y, 2)
        conv = w0 * y2 + w1 * y1 + w2 * y
        db_ref[...] = (dy * conv).astype(BF16)
        dconv = dy * cb_ref[...]
        dyy = w2 * dconv + w1 * _shift_up(dconv, 1) + w0 * _shift_up(dconv, 2)
        dc_ref[...] = (dyy * cx).astype(BF16)
        dx_ref[...] = (dyy * cc).astype(BF16)
        dw_ref[0:1, :] = jnp.sum(dconv * y2, axis=0, keepdims=True)
        dw_ref[1:2, :] = jnp.sum(dconv * y1, axis=0, keepdims=True)
        dw_ref[2:3, :] = jnp.sum(dconv * y, axis=0, keepdims=True)

    cb, cc, cx, wspec, out = _conv_specs(S)
    db, dc, dx, dw = pl.pallas_call(
        body, name="conv_bwd", grid=(BW // CONV_TC,), in_specs=[cb, cc, cx, wspec, out],
        out_specs=[out, out, out, wspec],
        out_shape=[jax.ShapeDtypeStruct((S, BW), BF16)] * 3 + [jax.ShapeDtypeStruct((3, BW), F32)],
        compiler_params=_cp("parallel"),
    )(p, p, p, conv_w, dyc)
    return db, dc, dx, dw


def _merge_specs(S, tm):
    gate = lambda n: pl.BlockSpec((tm, D_MODEL), lambda i: (i, C_GATES // D_MODEL + n))
    return [gate(0), gate(1), gate(2)], pl.BlockSpec((3, tm, D_MODEL), lambda i: (0, i, 0)), pl.BlockSpec((tm, D_MODEL), lambda i: (i, 0))


def _merge_fwd(p, bd):
    S = p.shape[0]
    tm = _tile(S, (256,))

    def body(g0, g1, g2, b_ref, o_ref):
        acc = jax.nn.sigmoid(g0[...]) * b_ref[0]
        acc = acc + jax.nn.sigmoid(g1[...]) * b_ref[1]
        acc = acc + jax.nn.sigmoid(g2[...]) * b_ref[2]
        o_ref[...] = acc.astype(BF16)

    gates, bspec, row = _merge_specs(S, tm)
    return pl.pallas_call(
        body, name="merge_fwd", grid=(S // tm,), in_specs=gates + [bspec], out_specs=row,
        out_shape=jax.ShapeDtypeStruct((S, D_MODEL), BF16), compiler_params=_cp("parallel"),
    )(p, p, p, bd)


def _merge_bwd(p, bd, dm):
    S = p.shape[0]
    tm = _tile(S, (256,))

    def body(g0, g1, g2, b_ref, dm_ref, db_ref, dg_ref):
        dmv = dm_ref[...]
        for n, g_ref in enumerate((g0, g1, g2)):
            sg = jax.nn.sigmoid(g_ref[...])
            db_ref[n] = (dmv * sg).astype(BF16)
            dg_ref[:, n * D_MODEL:(n + 1) * D_MODEL] = (dmv * b_ref[n] * (sg * (1.0 - sg))).astype(BF16)

    gates, bspec, row = _merge_specs(S, tm)
    return pl.pallas_call(
        body, name="merge_bwd", grid=(S // tm,), in_specs=gates + [bspec, row],
        out_specs=[bspec, pl.BlockSpec((tm, 3 * D_MODEL), lambda i: (i, 0))],
        out_shape=[jax.ShapeDtypeStruct((3, S, D_MODEL), BF16), jax.ShapeDtypeStruct((S, 3 * D_MODEL), BF16)],
        compiler_params=_cp("parallel"),
    )(p, p, p, bd, dm)


XA_TM = 512


def _xa_probs(qh, kh):
    s = lax.dot_general(qh, kh, NT, preferred_element_type=F32) * (XA_DH ** -0.5)
    e = jnp.exp(s - jnp.max(s, axis=-1, keepdims=True))
    return e / jnp.sum(e, axis=-1, keepdims=True)


def _xa_fwd(q, kv):
    S = q.shape[0]
    tm = min(XA_TM, S)
    M = kv.shape[1]

    def body(q_ref, kv_ref, o_ref):
        for h in range(XA_HEADS):
            cols = slice(h * XA_DH, (h + 1) * XA_DH)
            pr = _xa_probs(q_ref[:, cols], kv_ref[0, :, cols])
            o_ref[:, cols] = jnp.dot(pr.astype(BF16), kv_ref[1, :, cols], preferred_element_type=F32).astype(BF16)

    row = pl.BlockSpec((tm, D_MODEL), lambda i: (i, 0))
    return pl.pallas_call(
        body, name="xa_fwd", grid=(S // tm,), in_specs=[row, pl.BlockSpec((2, M, D_MODEL), lambda i: (0, 0, 0))], out_specs=row,
        out_shape=jax.ShapeDtypeStruct((S, D_MODEL), BF16), compiler_params=_cp("parallel"),
    )(q, kv)


def _xa_bwd(q, kv, do):
    S = q.shape[0]
    tm = min(XA_TM, S)
    M = kv.shape[1]

    def body(q_ref, kv_ref, do_ref, dq_ref, dkv_ref):
        @pl.when(pl.program_id(0) == 0)
        def _():
            dkv_ref[...] = jnp.zeros_like(dkv_ref)

        for h in range(XA_HEADS):
            cols = slice(h * XA_DH, (h + 1) * XA_DH)
            qh, kh, vh, doh = q_ref[:, cols], kv_ref[0, :, cols], kv_ref[1, :, cols], do_ref[:, cols]
            pr = _xa_probs(qh, kh)
            dkv_ref[1, :, cols] += lax.dot_general(pr.astype(BF16), doh, TN, preferred_element_type=F32)
            dp = lax.dot_general(doh, vh, NT, preferred_element_type=F32)
            ds = (pr * (dp - jnp.sum(dp * pr, axis=-1, keepdims=True)) * (XA_DH ** -0.5)).astype(BF16)
            dq_ref[:, cols] = jnp.dot(ds, kh, preferred_element_type=F32).astype(BF16)
            dkv_ref[0, :, cols] += lax.dot_general(ds, qh, TN, preferred_element_type=F32)

    row = pl.BlockSpec((tm, D_MODEL), lambda i: (i, 0))
    kvs = pl.BlockSpec((2, M, D_MODEL), lambda i: (0, 0, 0))
    return pl.pallas_call(
        body, name="xa_bwd", grid=(S // tm,), in_specs=[row, kvs, row], out_specs=[row, kvs],
        out_shape=[jax.ShapeDtypeStruct((S, D_MODEL), BF16), jax.ShapeDtypeStruct((2, M, D_MODEL), F32)],
        compiler_params=_cp("arbitrary"),
    )(q, kv, do)


def _swiglu_fwd(ab):
    nb, _, S, C = ab.shape
    tm = _tile(S, (512, 256))

    def body(ab_ref, o_ref):
        a = ab_ref[0]
        o_ref[...] = (a * jax.nn.sigmoid(a) * ab_ref[1]).astype(BF16)

    return pl.pallas_call(
        body, name="swiglu_fwd", grid=(nb, S // tm),
        in_specs=[pl.BlockSpec((None, 2, tm, C), lambda j, i: (j, 0, i, 0))], out_specs=pl.BlockSpec((None, tm, C), lambda j, i: (j, i, 0)),
        out_shape=jax.ShapeDtypeStruct((nb, S, C), BF16), compiler_params=_cp("parallel", "parallel"),
    )(ab)


def _swiglu_bwd(ab, dh):
    nb, _, S, C = ab.shape
    tm = _tile(S, (512, 256))

    def body(ab_ref, dh_ref, o_ref):
        a, b, d = ab_ref[0], ab_ref[1], dh_ref[...].astype(F32)
        sg = jax.nn.sigmoid(a)
        o_ref[0] = (d * b * (sg * (1.0 + a * (1.0 - sg)))).astype(BF16)
        o_ref[1] = (d * (a * sg)).astype(BF16)

    pair = pl.BlockSpec((None, 2, tm, C), lambda j, i: (j, 0, i, 0))
    return pl.pallas_call(
        body, name="swiglu_bwd", grid=(nb, S // tm), in_specs=[pair, pl.BlockSpec((None, tm, C), lambda j, i: (j, i, 0))], out_specs=pair,
        out_shape=jax.ShapeDtypeStruct(ab.shape, BF16), compiler_params=_cp("parallel", "parallel"),
    )(ab, dh)


def _reduce_adam(parts, w, m, v, *, name):
    shape = w.shape
    C = shape[-1]
    R = math.prod(shape[:-1])
    tm = _rows(R, 4 * C)
    n = len(parts)
    c1, c2 = 1.0 - ADAM_B1 ** ADAM_STEP, 1.0 - ADAM_B2 ** ADAM_STEP

    def body(*refs):
        g = refs[0][...]
        for r in refs[1:n]:
            g = g + r[...]
        w_ref, m_ref, v_ref, go, do, mo, vo = refs[n:]
        mn = ADAM_B1 * m_ref[...] + (1.0 - ADAM_B1) * g
        vn = ADAM_B2 * v_ref[...] + (1.0 - ADAM_B2) * (g * g)
        go[...] = g
        do[...] = -ADAM_LR * ((mn / c1) / (jnp.sqrt(vn / c2) + ADAM_EPS) + ADAM_WD * w_ref[...])
        mo[...] = mn
        vo[...] = vn

    row = pl.BlockSpec((tm, C), lambda i: (i, 0))
    outs = pl.pallas_call(
        body, name=name, grid=(R // tm,), in_specs=[row] * (n + 3), out_specs=[row] * 4,
        out_shape=[jax.ShapeDtypeStruct((R, C), F32)] * 4, compiler_params=_cp("parallel"),
    )(*[a.reshape(R, C) for a in (*parts, w, m, v)])
    return tuple(o.reshape(shape) for o in outs)


def _add2(a0, a1, r, sel, *, name):
    shape = r.shape
    C = shape[-1]
    R = math.prod(shape[:-1])
    tm = _rows(R, 4 * C)

    def body(s_ref, a0_ref, a1_ref, r_ref, o_ref):
        @pl.when(s_ref[0] == 0)
        def _():
            o_ref[...] = (a0_ref[...] + r_ref[...]).astype(BF16)

        @pl.when(s_ref[0] != 0)
        def _():
            o_ref[...] = (a1_ref[...] + r_ref[...]).astype(BF16)

    row = pl.BlockSpec((tm, C), lambda i, s: (i, 0))
    pick0 = pl.BlockSpec((tm, C), lambda i, s: (jnp.where(s[0] == 0, i, 0), 0))
    pick1 = pl.BlockSpec((tm, C), lambda i, s: (jnp.where(s[0] == 0, 0, i), 0))
    return pl.pallas_call(
        body, name=name,
        grid_spec=pltpu.PrefetchScalarGridSpec(num_scalar_prefetch=1, grid=(R // tm,), in_specs=[pick0, pick1, row], out_specs=row),
        out_shape=jax.ShapeDtypeStruct((R, C), BF16), compiler_params=_cp("arbitrary"),
    )(sel.reshape(1).astype(jnp.int32), a0.reshape(R, C), a1.reshape(R, C), r.reshape(R, C)).reshape(shape)


_VIEW = {
    "w_in": ((1024, 7168), (1024, 1792), 256, lambda i, b: (i, b)),
    "w_br": ((1536, 1024), (1536, 256), 512, lambda i, b: (i, b)),
    "w_sq": ((5120, 1024), (1280, 1024), 256, lambda i, b: (4 * i + b, 0)),
    "w_gu": ((8192, 704), (2048, 704), 512, lambda i, b: (4 * b + i, 0)),
    "w_dn": ((2816, 1024), (704, 1024), 352, lambda i, b: (2 * b + i, 0)),
}


def _sum_chips(name, slots, part, b):
    full2, sh2, tm, idx = _VIEW[name]
    C = sh2[1]

    def body(b_ref, s_ref, own_ref, o_ref):
        o_ref[...] = ((s_ref[0].astype(F32) + s_ref[1].astype(F32)) + s_ref[2].astype(F32)) + own_ref[...].astype(F32)

    return pl.pallas_call(
        body, name="sum_chips_" + name,
        grid_spec=pltpu.PrefetchScalarGridSpec(
            num_scalar_prefetch=1, grid=(sh2[0] // tm,),
            in_specs=[pl.BlockSpec((3, tm, C), lambda i, bs: (0, i, 0)), pl.BlockSpec((tm, C), lambda i, bs: idx(i, bs[0]))],
            out_specs=pl.BlockSpec((tm, C), lambda i, bs: (i, 0))),
        out_shape=jax.ShapeDtypeStruct(sh2, F32), compiler_params=_cp("arbitrary"),
    )(b.reshape(1).astype(jnp.int32), slots.reshape((3,) + sh2), part.reshape(full2)).reshape(_SHARD_SHAPE[name])


def _adam_layers(g_own, g_recv, c, w, m, v, *, name):
    shape = w.shape
    C = shape[-1]
    R = math.prod(shape[1:-1])
    tm = _rows(R, 4 * C)
    c1, c2 = 1.0 - ADAM_B1 ** ADAM_STEP, 1.0 - ADAM_B2 ** ADAM_STEP

    def body(c_ref, own_ref, recv_ref, w_ref, m_ref, v_ref, go, do, mo, vo):
        g = jnp.where(pl.program_id(0) == c_ref[0], own_ref[...], recv_ref[...])
        mn = ADAM_B1 * m_ref[...] + (1.0 - ADAM_B1) * g
        vn = ADAM_B2 * v_ref[...] + (1.0 - ADAM_B2) * (g * g)
        go[...] = g
        do[...] = -ADAM_LR * ((mn / c1) / (jnp.sqrt(vn / c2) + ADAM_EPS) + ADAM_WD * w_ref[...])
        mo[...] = mn
        vo[...] = vn

    own = pl.BlockSpec((tm, C), lambda l, i, cs: (jnp.where(l == cs[0], i, 0), 0))
    recv = pl.BlockSpec((tm, C), lambda l, i, cs: (jnp.where(l == cs[0], 0, i), 0))
    row = pl.BlockSpec((None, tm, C), lambda l, i, cs: (l, i, 0))
    outs = pl.pallas_call(
        body, name=name,
        grid_spec=pltpu.PrefetchScalarGridSpec(num_scalar_prefetch=1, grid=(DEPTH, R // tm), in_specs=[own, recv, row, row, row], out_specs=[row] * 4),
        out_shape=[jax.ShapeDtypeStruct((DEPTH, R, C), F32)] * 4, compiler_params=_cp("arbitrary", "arbitrary"),
    )(c.reshape(1).astype(jnp.int32), g_own.reshape(R, C), g_recv.reshape(R, C), *[t.reshape(DEPTH, R, C) for t in (w, m, v)])
    return tuple(o.reshape(shape) for o in outs)


def _sum_slots(r, *, name):
    n, shape = r.shape[0], r.shape[1:]
    C = shape[-1]
    R = math.prod(shape[:-1])
    tm = _rows(R, 4 * C * n)

    def body(r_ref, o_ref):
        acc = r_ref[0]
        for s in range(1, n):
            acc = acc + r_ref[s]
        o_ref[...] = acc

    return pl.pallas_call(
        body, name=name, grid=(R // tm,), in_specs=[pl.BlockSpec((n, tm, C), lambda i: (0, i, 0))],
        out_specs=pl.BlockSpec((tm, C), lambda i: (i, 0)), out_shape=jax.ShapeDtypeStruct((R, C), F32), compiler_params=_cp("parallel"),
    )(r.reshape(n, R, C)).reshape(shape)


def _heads(t, S):
    n = t.shape[1] // BW
    return t.reshape(S, n * SB_HEADS, SB_DH).transpose(1, 0, 2)


def _unheads(t, S):
    return t.transpose(1, 0, 2).reshape(S, BW)


def _layer_fwd(x, mem, wl):
    S = x.shape[0]
    sv = {"x": x}
    h1 = _rms_fwd(x, wl["norm_mix_g"], name="rms_mix")
    p = _mm(h1, wl["w_in"], mode="nn", name="mm_in")
    qkv = _heads(p[:, :3 * BW].astype(BF16), S)
    q, k, v = qkv[:SB_HEADS], qkv[SB_HEADS:2 * SB_HEADS], qkv[2 * SB_HEADS:]
    ya, tot, cnt = _sb_fwd(q, k, v)
    b_st = wl["b_spatial"].T
    yb = _sgu_fwd(p, wl["sgu_ln_g"], wl["sgu_ln_b"], wl["w_spatial"], b_st)
    yc = _conv_fwd(p, wl["conv_w"])
    br = jnp.stack([_unheads(ya, S), yb, yc])
    bd = _mm(br, wl["w_br"], mode="nn", a_kind="batch", b_kind="batch", name="mm_branch")
    merged = _merge_fwd(p, bd)
    x1 = _mm(merged, wl["w_sq"][0], mode="nn", res=x, name="mm_out")
    h2 = _rms_fwd(x1, wl["norm_xa_g"], name="rms_xa")
    qx = _mm(h2, wl["w_sq"][1], mode="nn", out_dtype=BF16, name="mm_q")
    mn = _rms_fwd(mem, wl["mem_norm_g"], name="rms_mem")
    kv = _mm(mn, wl["w_sq"][3:5], mode="nn", b_kind="batch", out_dtype=BF16, name="mm_kv")
    o = _xa_fwd(qx, kv)
    x2 = _mm(o, wl["w_sq"][2], mode="nn", res=x1, name="mm_o")
    h3 = _rms_fwd(x2, wl["norm_ffn_g"], name="rms_ffn")
    ab = _mm(h3, wl["w_gu"], mode="nn", b_kind="batch", name="mm_gu").reshape(N_CHIPS, 2, S, FFN_SH)
    hh = _swiglu_fwd(ab)
    x3 = _mm(hh, wl["w_dn"], mode="nn", a_kind="kchunk", b_kind="kchunk", res=x2, name="mm_down")
    sv.update(h1=h1, p=p, q=q, k=k, v=v, tot=tot, cnt=cnt, br=br, bd=bd, merged=merged, x1=x1, h2=h2, qx=qx, mn=mn, kv=kv, o=o,
              x2=x2, h3=h3, ab=ab, hh=hh, b_st=b_st)
    return x3, sv


def _layer_bwd(dx3, mem, wl, sv):
    S = dx3.shape[0]
    p = sv["p"]
    g = {}
    dhh = _mm(dx3, wl["w_dn"], mode="nt", b_kind="batch", name="mm_down_dx")
    g["w_dn"] = _mm(sv["hh"], dx3, mode="tn", a_kind="batch", name="mm_down_dw")
    dab = _swiglu_bwd(sv["ab"], dhh).reshape(2 * N_CHIPS, S, FFN_SH)
    g["w_gu"] = _mm(sv["h3"], dab, mode="tn", b_kind="batch", name="mm_gu_dw")
    dh3 = _mm(dab, wl["w_gu"], mode="nt", a_kind="kchunk", b_kind="kchunk", name="mm_gu_dx")
    dx2, g["norm_ffn_g"] = _rms_bwd(sv["x2"], wl["norm_ffn_g"], dh3, dx3, name="rms_ffn_bwd")
    do = _mm(dx2, wl["w_sq"][2], mode="nt", out_dtype=BF16, name="mm_o_dx")
    dw_o = _mm(sv["o"], dx2, mode="tn", name="mm_o_dw")
    dq, dkv = _xa_bwd(sv["qx"], sv["kv"], do)
    dw_q = _mm(sv["h2"], dq, mode="tn", name="mm_q_dw")
    dh2 = _mm(dq, wl["w_sq"][1], mode="nt", name="mm_q_dx")
    dw_kv = _mm(sv["mn"], dkv, mode="tn", b_kind="batch", name="mm_kv_dw")
    dmn = _mm(dkv, wl["w_sq"][3:5], mode="nt", a_kind="kchunk", b_kind="kchunk", name="mm_kv_dx")
    _, g["mem_norm_g"] = _rms_bwd(mem, wl["mem_norm_g"], dmn, jnp.zeros_like(mem), name="rms_mem_bwd")
    dx1, g["norm_xa_g"] = _rms_bwd(sv["x1"], wl["norm_xa_g"], dh2, dx2, name="rms_xa_bwd")
    dm = _mm(dx1, wl["w_sq"][0], mode="nt", name="mm_out_dx")
    dw_out = _mm(sv["merged"], dx1, mode="tn", name="mm_out_dw")
    g["w_sq"] = jnp.concatenate([jnp.stack([dw_out, dw_q, dw_o]), dw_kv])
    dbd, dgates = _merge_bwd(p, sv["bd"], dm)
    dbr = _mm(dbd, wl["w_br"], mode="nt", a_kind="batch", b_kind="batch", name="mm_branch_dx")
    g["w_br"] = _mm(sv["br"], dbd, mode="tn", a_kind="batch", b_kind="batch", name="mm_branch_dw")
    dya = dbr[0].astype(BF16).reshape(S, SB_HEADS, SB_DH).transpose(1, 0, 2)
    dq_h, dk_h, dv_h = _sb_bwd(sv["q"], sv["k"], sv["v"], dya, sv["tot"], sv["cnt"])
    dz, g["sgu_ln_g"], g["sgu_ln_b"], g["w_spatial"], g["b_spatial"] = _sgu_bwd(
        p, dbr[1], wl["sgu_ln_g"], wl["sgu_ln_b"], wl["w_spatial"], wl["w_spatial"].transpose(0, 2, 1), sv["b_st"])
    dcb, dcc, dcx, g["conv_w"] = _conv_bwd(p, wl["conv_w"], dbr[2])
    dp = jnp.concatenate([_unheads(dq_h, S), _unheads(dk_h, S).astype(BF16), _unheads(dv_h, S).astype(BF16),
                          dz, dcb, dcc, dcx, dgates], axis=1)
    g["w_in"] = _mm(sv["h1"], dp, mode="tn", name="mm_in_dw")
    dh1 = _mm(dp, wl["w_in"], mode="nt", name="mm_in_dx")
    dx, g["norm_mix_g"] = _rms_bwd(sv["x"], wl["norm_mix_g"], dh1, dx1, name="rms_mix_bwd")
    return dx, g


def _local_step(x, mem, target, layers, final_g):
    h, saved = x, []
    for wl in layers:
        h, sv = _layer_fwd(h, mem, wl)
        saved.append(sv)
    loss, dx, d_final = _loss_head(h, final_g, target)
    grads = [None] * len(layers)
    for l in reversed(range(len(layers))):
        dx, grads[l] = _layer_bwd(dx, mem, layers[l], saved[l])
    return loss, dx, grads, d_final


_ALL = slice(None)
CONV_ROWS = 8
_SHARD = {
    "w_in": lambda b: (_ALL, pl.ds(1792 * b, 1792)),
    "w_br": lambda b: (_ALL, _ALL, pl.ds(256 * b, 256)),
    "w_sq": lambda b: (_ALL, pl.ds(256 * b, 256), _ALL),
    "w_gu": lambda b: (b,),
    "w_dn": lambda b: (b,),
    "conv_w": lambda b: (_ALL, pl.ds(128 * b, 128)),
}
_FULL_SHAPE = {"w_in": (1024, 7168), "w_br": (3, 512, 1024), "w_sq": (5, 1024, 1024), "w_gu": (4, 2, 1024, 704),
               "w_dn": (4, 704, 1024), "conv_w": (CONV_ROWS, 512)}
_SHARD_SHAPE = {"w_in": (1024, 1792), "w_br": (3, 512, 256), "w_sq": (5, 256, 1024), "w_gu": (2, 1024, 704),
                "w_dn": (704, 1024), "conv_w": (CONV_ROWS, 128)}


def _pos():
    return lax.axis_index("x"), lax.axis_index("y"), lax.axis_index("c")


def _per_chip(fn):
    x, y, _ = _pos()
    for x0 in (0, 1):
        for y0 in (0, 1):
            @pl.when((x == x0) & (y == y0))
            def _():
                fn(x0, y0)


def _other_chips(x0, y0):
    return [(1 - x0, y0), (x0, 1 - y0), (1 - x0, 1 - y0)]


def _rcopy(src, dst, ssem, rsem, dev):
    return pltpu.make_async_remote_copy(src_ref=src, dst_ref=dst, send_sem=ssem, recv_sem=rsem, device_id=dev, device_id_type=MESH)


def _dma_sems(n):
    return pltpu.SemaphoreType.DMA((n,))


def _gather_weights(local):
    names = list(local)
    n = len(names)

    def body(*refs):
        ins, outs = refs[:n], refs[n:2 * n]
        send, recv, lsem = refs[2 * n:]
        c = lax.axis_index("c")

        def run(x0, y0):
            b0 = 2 * x0 + y0
            chips = _other_chips(x0, y0)
            shard = lambda a, layer, b: outs[a].at[(layer,) + _SHARD[names[a]](b)]
            own = [pltpu.make_async_copy(ins[a], outs[a].at[(_ALL,) + _SHARD[names[a]](b0)], lsem.at[a]) for a in range(n)]
            for cp in own:
                cp.start()
            sent = []
            for kk, (px, py) in enumerate(chips):
                for a in range(n):
                    sent.append(_rcopy(ins[a].at[c], shard(a, c, b0), send.at[6 * a + kk], recv.at[6 * a + kk], (px, py, c)))
                    sent[-1].start()
            for kk, (px, py) in enumerate(chips):
                for a in range(n):
                    landed = shard(a, c, 2 * px + py)
                    _rcopy(landed, landed, send.at[6 * a + kk], recv.at[6 * a + kk], (px, py, c)).wait_recv()
                    sent.append(_rcopy(landed, landed, send.at[6 * a + 3 + kk], recv.at[6 * a + 3 + kk], (x0, y0, 1 - c)))
                    sent[-1].start()
            for kk, (px, py) in enumerate(chips):
                for a in range(n):
                    got = shard(a, 1 - c, 2 * px + py)
                    _rcopy(got, got, send.at[6 * a + 3 + kk], recv.at[6 * a + 3 + kk], (x0, y0, 1 - c)).wait_recv()
            for cp in sent:
                cp.wait_send()
            for cp in own:
                cp.wait()

        _per_chip(run)

    outs = pl.pallas_call(
        body, name="gather_weights", in_specs=[ANY] * n, out_specs=[ANY] * n,
        out_shape=[jax.ShapeDtypeStruct((DEPTH,) + _FULL_SHAPE[nm], local[nm].dtype) for nm in names],
        scratch_shapes=[_dma_sems(6 * n), _dma_sems(6 * n), _dma_sems(n)],
    )(*[local[nm] for nm in names])
    return dict(zip(names, outs))


def _sibling_presum_exchange(g0, g1):
    names = list(g0)
    n = len(names)

    def body(*refs):
        l0, l1, outs = refs[:n], refs[n:2 * n], refs[2 * n:3 * n]
        send, recv = refs[3 * n:]
        x, y, c = _pos()
        for c0 in (0, 1):
            @pl.when(c == c0)
            def _():
                srcs = l1 if c0 == 0 else l0
                cps = [_rcopy(srcs[a], outs[a], send.at[a], recv.at[a], (x, y, 1 - c0)) for a in range(n)]
                for cp in cps:
                    cp.start()
                for cp in cps:
                    cp.wait()

    outs = pl.pallas_call(
        body, name="grad_presum_exchange", in_specs=[ANY] * (2 * n), out_specs=[ANY] * n,
        out_shape=[jax.ShapeDtypeStruct(g0[nm].shape, g0[nm].dtype) for nm in names],
        scratch_shapes=[_dma_sems(n), _dma_sems(n)],
    )(*[g0[nm] for nm in names], *[g1[nm] for nm in names])
    return dict(zip(names, outs))


def _shard_exchange(part):
    names = list(part)
    n = len(names)

    def body(*refs):
        ins, outs = refs[:n], refs[n:2 * n]
        send, recv = refs[2 * n:]
        c = lax.axis_index("c")

        def run(x0, y0):
            chips = _other_chips(x0, y0)
            sent = []
            for kk, (px, py) in enumerate(chips):
                for a in range(n):
                    sent.append(_rcopy(ins[a].at[_SHARD[names[a]](2 * px + py)], outs[a].at[kk], send.at[3 * a + kk], recv.at[3 * a + kk], (px, py, c)))
                    sent[-1].start()
            for kk, (px, py) in enumerate(chips):
                for a in range(n):
                    slot = outs[a].at[kk]
                    _rcopy(slot, slot, send.at[3 * a + kk], recv.at[3 * a + kk], (px, py, c)).wait_recv()
            for cp in sent:
                cp.wait_send()

        _per_chip(run)

    outs = pl.pallas_call(
        body, name="grad_shard_exchange", in_specs=[ANY] * n, out_specs=[ANY] * n,
        out_shape=[jax.ShapeDtypeStruct((N_CHIPS - 1,) + _SHARD_SHAPE[nm], part[nm].dtype) for nm in names],
        scratch_shapes=[_dma_sems(3 * n), _dma_sems(3 * n)],
    )(*[part[nm] for nm in names])
    return dict(zip(names, outs))


def _sibling_exchange(red):
    names = list(red)
    n = len(names)

    def body(*refs):
        ins, outs = refs[:n], refs[n:2 * n]
        send, recv = refs[2 * n:]
        x, y, c = _pos()
        cps = [_rcopy(ins[a], outs[a], send.at[a], recv.at[a], (x, y, 1 - c)) for a in range(n)]
        for cp in cps:
            cp.start()
        for cp in cps:
            cp.wait()

    outs = pl.pallas_call(
        body, name="grad_sibling_exchange", in_specs=[ANY] * n, out_specs=[ANY] * n,
        out_shape=[jax.ShapeDtypeStruct(red[nm].shape, red[nm].dtype) for nm in names],
        scratch_shapes=[_dma_sems(n), _dma_sems(n)],
    )(*[red[nm] for nm in names])
    return dict(zip(names, outs))


def _gather_small(pack):
    flips = [(fx, fy, fc) for fx in (0, 1) for fy in (0, 1) for fc in (0, 1) if fx or fy or fc]

    def body(in_ref, out_ref, send, recv, lsem):
        x, y, c = _pos()
        me = 4 * x + 2 * y + c
        own = pltpu.make_async_copy(in_ref, out_ref.at[me], lsem.at[0])
        own.start()
        peers = [(x ^ fx, y ^ fy, c ^ fc) for fx, fy, fc in flips]
        cps = [_rcopy(in_ref, out_ref.at[me], send.at[k], recv.at[k], peer) for k, peer in enumerate(peers)]
        for cp in cps:
            cp.start()
        for k, (px, py, pc) in enumerate(peers):
            slot = out_ref.at[4 * px + 2 * py + pc]
            _rcopy(slot, slot, send.at[k], recv.at[k], (px, py, pc)).wait_recv()
        for cp in cps:
            cp.wait_send()
        own.wait()

    return pl.pallas_call(
        body, name="gather_small_grads", in_specs=[ANY], out_specs=ANY,
        out_shape=jax.ShapeDtypeStruct((8,) + pack.shape, pack.dtype),
        scratch_shapes=[_dma_sems(len(flips)), _dma_sems(len(flips)), _dma_sems(1)],
    )(pack)


_WEIGHTS = ["norm_mix_g", "w_in", "sgu_ln_g", "sgu_ln_b", "w_spatial", "b_spatial", "conv_w", "w_branch", "w_out", "norm_xa_g",
            "mem_norm_g", "w_q_xa", "w_k_xa", "w_v_xa", "w_o_xa", "norm_ffn_g", "w_gate_ffn", "w_up_ffn", "w_down_ffn", "final_g"]
_REPLICATED = ["norm_mix_g", "sgu_ln_g", "sgu_ln_b", "w_spatial", "b_spatial", "norm_xa_g", "mem_norm_g", "norm_ffn_g", "final_g"]
_SQUARE = ["w_out", "w_q_xa", "w_o_xa", "w_k_xa", "w_v_xa"]
_BIG = ["w_in", "w_br", "w_sq", "w_gu", "w_dn"]


def _pack(arrs):
    return jnp.concatenate([a.reshape(-1) for a in arrs]).reshape(-1, 128)


def _step(a):
    w = {n: a[n] for n in _WEIGHTS}
    x, mem, target = a["x"][0], a["mem"][0], a["loss_target"][0]
    xi, yi, ci = _pos()
    bf = lambda n: w[n].astype(BF16)
    full = _gather_weights({
        "w_in": bf("w_in"), "w_br": bf("w_branch"), "w_sq": jnp.stack([bf(n) for n in _SQUARE], axis=1),
        "w_gu": jnp.stack([bf("w_gate_ffn"), bf("w_up_ffn")], axis=1), "w_dn": bf("w_down_ffn"),
        "conv_w": jnp.pad(w["conv_w"], ((0, 0), (0, CONV_ROWS - 3), (0, 0)))})
    layers = []
    for l in range(DEPTH):
        wl = {n: full[n][l] for n in full}
        wl["conv_w"] = wl["conv_w"][:3]
        wl["w_gu"] = wl["w_gu"].reshape(2 * N_CHIPS, D_MODEL, FFN_SH)
        wl.update({n: w[n][l] for n in _REPLICATED if n != "final_g"})
        layers.append(wl)
    loss, dx, grads, d_final = _local_step(x, mem, target, layers, w["final_g"])
    loss = lax.psum(loss, ("x", "y", "c"))

    for g in grads:
        g["w_gu"] = g["w_gu"].reshape(_FULL_SHAPE["w_gu"])
    g0, g1 = ({n: g[n] for n in _BIG} for g in grads)
    sib = _sibling_presum_exchange(g0, g1)
    part = {n: _add2(g0[n], g1[n], sib[n], ci, name="presum_" + n) for n in _BIG}
    slots = _shard_exchange(part)
    mine = {n: _sum_chips(n, slots[n], part[n], 2 * xi + yi) for n in _BIG}
    theirs = _sibling_exchange(mine)

    out = {}

    def adam_layers(name, group, pick=None):
        go, gr = (mine[group], theirs[group]) if pick is None else (mine[group][pick], theirs[group][pick])
        out[name] = _adam_layers(go, gr, ci, w[name], a["m_" + name], a["v_" + name], name="adam_" + name)

    adam_layers("w_in", "w_in")
    adam_layers("w_branch", "w_br")
    for t, n in enumerate(_SQUARE):
        adam_layers(n, "w_sq", t)
    adam_layers("w_gate_ffn", "w_gu", 0)
    adam_layers("w_up_ffn", "w_gu", 1)
    adam_layers("w_down_ffn", "w_dn")

    def adam(name, g):
        out[name] = _reduce_adam([g], w[name], a["m_" + name], a["v_" + name], name="adam_" + name)

    small = {n: jnp.stack([g[n] for g in grads]) for n in _REPLICATED if n != "final_g"}
    small["final_g"] = d_final
    conv_g = jnp.stack([g["conv_w"] for g in grads])
    n_rep = sum(w[n].size for n in _REPLICATED) // 128
    summed = _sum_slots(_gather_small(_pack([small[n] for n in _REPLICATED] + [conv_g])), name="sum_devices_small")
    res = _reduce_adam([summed[:n_rep]], _pack([w[n] for n in _REPLICATED]), _pack([a["m_" + n] for n in _REPLICATED]),
                       _pack([a["v_" + n] for n in _REPLICATED]), name="adam_replicated")
    off = 0
    for n in _REPLICATED:
        out[n] = tuple(r.reshape(-1)[off:off + w[n].size].reshape(w[n].shape) for r in res)
        off += w[n].size
    conv_full = summed[n_rep:].reshape(conv_g.shape)
    adam("conv_w", lax.dynamic_slice_in_dim(conv_full, (2 * xi + yi) * 128, 128, axis=2))

    return (loss, dx[None], *[out[n][k] for k in range(4) for n in _WEIGHTS])


def kernel(x, mem, norm_mix_g, w_in, sgu_ln_g, sgu_ln_b, w_spatial, b_spatial, conv_w, w_branch, w_out, norm_xa_g, mem_norm_g, w_q_xa, w_k_xa, w_v_xa, w_o_xa, norm_ffn_g, w_gate_ffn, w_up_ffn, w_down_ffn, final_g, loss_target, m_norm_mix_g, m_w_in, m_sgu_ln_g, m_sgu_ln_b, m_w_spatial, m_b_spatial, m_conv_w, m_w_branch, m_w_out, m_norm_xa_g, m_mem_norm_g, m_w_q_xa, m_w_k_xa, m_w_v_xa, m_w_o_xa, m_norm_ffn_g, m_w_gate_ffn, m_w_up_ffn, m_w_down_ffn, m_final_g, v_norm_mix_g, v_w_in, v_sgu_ln_g, v_sgu_ln_b, v_w_spatial, v_b_spatial, v_conv_w, v_w_branch, v_w_out, v_norm_xa_g, v_mem_norm_g, v_w_q_xa, v_w_k_xa, v_w_v_xa, v_w_o_xa, v_norm_ffn_g, v_w_gate_ffn, v_w_up_ffn, v_w_down_ffn, v_final_g):
    return _step(dict(locals()))
```

```python
import functools
import math

import jax
import jax.numpy as jnp
from jax import lax
from jax.experimental import pallas as pl
from jax.experimental.pallas import tpu as pltpu

F32, BF16 = jnp.float32, jnp.bfloat16
MESH = pl.DeviceIdType.MESH
ANY = pl.BlockSpec(memory_space=pl.ANY)

D_MODEL = 1024
DEPTH = 2
BW = 512
SB_HEADS, SB_DH = 8, 64
SGU_LEN, SGU_GROUPS, SGU_GD, SGU_CHUNK = 128, 4, 128, 64
XA_HEADS, XA_DH = 4, 256
FFN_SH = 704
N_CHIPS = 4
IN_COLS = 7168
C_Z, C_CB, C_GATES = 1536, 2560, 4096

ADAM_LR, ADAM_B1, ADAM_B2, ADAM_EPS, ADAM_WD, ADAM_STEP = 0.001, 0.9, 0.999, 1e-08, 0.01, 10

VMEM_LIMIT_V7X = 56 * 1024 * 1024

NN = (((1,), (0,)), ((), ()))
NT = (((1,), (1,)), ((), ()))
TN = (((0,), (0,)), ((), ()))


def _cp(*sem):
    return pltpu.CompilerParams(dimension_semantics=sem, vmem_limit_bytes=VMEM_LIMIT_V7X)


def _tile(n, pref):
    for t in pref:
        if n % t == 0:
            return t
    return n


def _rows(r, row_bytes, block_bytes=1 << 20):
    fits = [t for t in range(8, r + 1, 8) if r % t == 0 and t * row_bytes <= block_bytes]
    return max(fits) if fits else r


def _mm(a, b, *, mode, name, out_dtype=F32, res=None, a_kind="2d", b_kind="2d", tm=None, tn=None, tk=None):
    a2, b2 = a.shape[-2:], b.shape[-2:]
    if mode == "nn":
        (M, K), N = a2, b2[1]
    elif mode == "nt":
        (M, K), N = a2, b2[0]
    else:
        (K, M), N = a2, b2[1]
    kchunk = a_kind == "kchunk" or b_kind == "kchunk"
    batch = a_kind == "batch" or b_kind == "batch"
    G = (a.shape[0] if a_kind == "batch" else b.shape[0]) if batch else 1
    tm = tm or _tile(M, (1024, 512, 256, 128))
    tn = tn or _tile(N, (1024, 512, 256, 128))
    if kchunk:
        tk, nk = K, (a.shape[0] if a_kind == "kchunk" else b.shape[0])
    else:
        tk = tk or _tile(K, (1024, 512, 256, 128))
        nk = K // tk

    def spec(kind, blk, idx):
        if kind == "2d":
            return pl.BlockSpec(blk, lambda g, i, j, k: idx(g, i, j, k))
        if kind == "batch":
            return pl.BlockSpec((None,) + blk, lambda g, i, j, k: (g,) + idx(g, i, j, k))
        return pl.BlockSpec((None,) + blk, lambda g, i, j, k: (k,) + idx(g, i, j, 0))

    if mode == "nn":
        a_spec = spec(a_kind, (tm, tk), lambda g, i, j, k: (i, k))
        b_spec = spec(b_kind, (tk, tn), lambda g, i, j, k: (k, j))
    elif mode == "nt":
        a_spec = spec(a_kind, (tm, tk), lambda g, i, j, k: (i, k))
        b_spec = spec(b_kind, (tn, tk), lambda g, i, j, k: (j, k))
    else:
        a_spec = spec(a_kind, (tk, tm), lambda g, i, j, k: (k, i))
        b_spec = spec(b_kind, (tk, tn), lambda g, i, j, k: (k, j))
    o_kind = "batch" if batch else "2d"
    o_spec = spec(o_kind, (tm, tn), lambda g, i, j, k: (i, j))
    o_shape = ((G,) if batch else ()) + (M, N)
    dn = {"nn": NN, "nt": NT, "tn": TN}[mode]
    has_res = res is not None

    def body(*refs):
        if has_res:
            a_ref, b_ref, r_ref, o_ref = refs[:4]
        else:
            a_ref, b_ref, o_ref = refs[:3]
        p = lax.dot_general(a_ref[...].astype(BF16), b_ref[...].astype(BF16), dn, preferred_element_type=F32)

        def finish(r):
            if has_res:
                r = r + r_ref[...]
            o_ref[...] = r.astype(out_dtype)

        if nk == 1:
            finish(p)
        else:
            acc = refs[-1]
            k = pl.program_id(3)

            @pl.when(k == 0)
            def _():
                acc[...] = p

            @pl.when(k > 0)
            def _():
                acc[...] += p

            @pl.when(k == nk - 1)
            def _():
                finish(acc[...])

    in_specs, args = [a_spec, b_spec], [a, b]
    if has_res:
        in_specs.append(spec("2d", (tm, tn), lambda g, i, j, k: (i, j)))
        args.append(res)
    return pl.pallas_call(
        body, name=name, grid=(G, M // tm, N // tn, nk), in_specs=in_specs, out_specs=o_spec,
        out_shape=jax.ShapeDtypeStruct(o_shape, out_dtype),
        scratch_shapes=[pltpu.VMEM((tm, tn), F32)] if nk > 1 else [],
        compiler_params=_cp("parallel", "parallel", "parallel", "arbitrary"),
    )(*args)


def _rms_fwd(x, g, *, name):
    S, Dm = x.shape
    tm = _tile(S, (512, 256))

    def body(x_ref, g_ref, o_ref):
        xv = x_ref[...]
        r = lax.rsqrt(jnp.mean(xv * xv, axis=-1, keepdims=True) + 1e-6)
        o_ref[...] = (xv * r * g_ref[...]).astype(BF16)

    return pl.pallas_call(
        body, name=name, grid=(S // tm,),
        in_specs=[pl.BlockSpec((tm, Dm), lambda i: (i, 0)), pl.BlockSpec((1, Dm), lambda i: (0, 0))],
        out_specs=pl.BlockSpec((tm, Dm), lambda i: (i, 0)), out_shape=jax.ShapeDtypeStruct((S, Dm), BF16),
        compiler_params=_cp("parallel"),
    )(x, g.reshape(1, Dm))


def _rms_bwd(x, g, dh, dres, *, name):
    S, Dm = x.shape
    tm = _tile(S, (512, 256))

    def body(x_ref, g_ref, dh_ref, dr_ref, dx_ref, dg_ref):
        xv, dhv = x_ref[...], dh_ref[...].astype(F32)
        r = lax.rsqrt(jnp.mean(xv * xv, axis=-1, keepdims=True) + 1e-6)
        u = dhv * g_ref[...]
        s = jnp.sum(u * xv, axis=-1, keepdims=True)
        dx_ref[...] = dr_ref[...] + r * u - xv * ((r * r * r) * (s * (1.0 / Dm)))
        part = jnp.sum(dhv * (xv * r), axis=0, keepdims=True)

        @pl.when(pl.program_id(0) == 0)
        def _():
            dg_ref[...] = part

        @pl.when(pl.program_id(0) > 0)
        def _():
            dg_ref[...] += part

    row = pl.BlockSpec((tm, Dm), lambda i: (i, 0))
    vec = pl.BlockSpec((1, Dm), lambda i: (0, 0))
    dx, dg = pl.pallas_call(
        body, name=name, grid=(S // tm,), in_specs=[row, vec, row, row], out_specs=[row, vec],
        out_shape=[jax.ShapeDtypeStruct((S, Dm), F32), jax.ShapeDtypeStruct((1, Dm), F32)],
        compiler_params=_cp("arbitrary"),
    )(x, g.reshape(1, Dm), dh, dres)
    return dx, dg.reshape(Dm)


def _loss_head(x, g, target):
    S, Dm = x.shape
    tm = _tile(S, (512, 256))

    def body(x_ref, g_ref, t_ref, dx_ref, dg_ref, loss_ref):
        xv, gv = x_ref[...], g_ref[...]
        r = lax.rsqrt(jnp.mean(xv * xv, axis=-1, keepdims=True) + 1e-6)
        xn = xv * r
        err = xn * gv - t_ref[...]
        lpart = 0.5 * jnp.sum(jnp.mean(err * err, axis=-1, keepdims=True), axis=0, keepdims=True)
        dy = err * (1.0 / Dm)
        u = dy * gv
        s = jnp.sum(u * xv, axis=-1, keepdims=True)
        dx_ref[...] = r * u - xv * ((r * r * r) * (s * (1.0 / Dm)))
        part = jnp.sum(dy * xn, axis=0, keepdims=True)
        lslab = jnp.broadcast_to(lpart, (8, 128))

        @pl.when(pl.program_id(0) == 0)
        def _():
            dg_ref[...] = part
            loss_ref[...] = lslab

        @pl.when(pl.program_id(0) > 0)
        def _():
            dg_ref[...] += part
            loss_ref[...] += lslab

    row = pl.BlockSpec((tm, Dm), lambda i: (i, 0))
    vec = pl.BlockSpec((1, Dm), lambda i: (0, 0))
    dx, dg, loss = pl.pallas_call(
        body, name="loss_head", grid=(S // tm,), in_specs=[row, vec, row],
        out_specs=[row, vec, pl.BlockSpec((8, 128), lambda i: (0, 0))],
        out_shape=[jax.ShapeDtypeStruct((S, Dm), F32), jax.ShapeDtypeStruct((1, Dm), F32), jax.ShapeDtypeStruct((8, 128), F32)],
        compiler_params=_cp("arbitrary"),
    )(x, g.reshape(1, Dm), target)
    return loss[0, 0], dx, dg.reshape(Dm)


SB_TQ, SB_TK = 256, 128
SB_EXP_FLOOR = -104.0


def _split2(v):
    hi = v.astype(BF16)
    return jnp.concatenate([hi, (v - hi.astype(F32)).astype(BF16)], axis=1)


def _tri2(cmp):
    j = lax.broadcasted_iota(jnp.int32, (2 * SB_TK, SB_TK), 0) % SB_TK
    s = lax.broadcasted_iota(jnp.int32, (2 * SB_TK, SB_TK), 1)
    return cmp(j, s).astype(BF16)


def _sb_scores(qv, kb, k0, q0, tq):
    z = lax.dot_general(qv, kb, NT, preferred_element_type=F32) * (SB_DH ** -0.5)
    t_pos = q0 + lax.broadcasted_iota(jnp.int32, (tq, SB_TK), 0)
    s_pos = k0 + lax.broadcasted_iota(jnp.int32, (tq, SB_TK), 1)
    valid = s_pos < t_pos
    ls = jnp.minimum(z, 0.0) - jnp.log(1.0 + jnp.exp(-jnp.abs(z)))
    l1m = jnp.where(valid, ls - z, 0.0)
    return z, valid, ls, l1m


SB_HG = 2


def _sb_fwd(q, k, v):
    H, S, dh = q.shape
    tq = min(SB_TQ, S)
    kb_per_q = tq // SB_TK
    hg = SB_HG

    def body(q_ref, k_ref, v_ref, o_ref, tot_ref, cnt_ref, acc, c):
        i = pl.program_id(1)
        q0 = i * tq
        later = _tri2(lambda j, s: j > s)
        acc[...] = jnp.zeros_like(acc)
        c[...] = jnp.zeros_like(c)
        nkb = (i + 1) * kb_per_q

        def more(st):
            n, highest = st
            return (n < nkb) & (highest > SB_EXP_FLOOR)

        def step(st):
            n, _ = st
            k0 = pl.multiple_of((nkb - 1 - n) * SB_TK, SB_TK)
            c_old, acc_old = [c[h] for h in range(hg)], [acc[h] for h in range(hg)]
            c_new, acc_new, highest = [], [], None
            for h in range(hg):
                kb, vb = k_ref[h, pl.ds(k0, SB_TK), :], v_ref[h, pl.ds(k0, SB_TK), :]
                z, valid, ls, l1m = _sb_scores(q_ref[h], kb, k0, q0, tq)
                c_new.append(c_old[h] + jnp.sum(l1m, axis=1, keepdims=True))
                top = jnp.max(c_new[h])
                highest = top if highest is None else jnp.maximum(highest, top)
                after = jnp.dot(_split2(l1m), later, preferred_element_type=F32)
                a = jnp.where(valid, jnp.exp(ls + after + c_old[h]), 0.0)
                acc_new.append(acc_old[h] + jnp.dot(a.astype(BF16), vb, preferred_element_type=F32))
            for h in range(hg):
                acc[h] = acc_new[h]
                c[h] = c_new[h]
            return n + 1, highest

        n_done, _ = lax.while_loop(more, step, (jnp.int32(0), jnp.float32(0.0)))
        o_ref[...] = acc[...].astype(o_ref.dtype)
        tot_ref[...] = c[...]
        cnt_ref[...] = jnp.full(cnt_ref.shape, n_done.astype(F32))

    qs = pl.BlockSpec((hg, tq, dh), lambda g, i: (g, i, 0))
    full = pl.BlockSpec((hg, S, dh), lambda g, i: (g, 0, 0))
    return pl.pallas_call(
        body, name="sb_fwd", grid=(H // hg, S // tq), in_specs=[qs, full, full],
        out_specs=[qs, pl.BlockSpec((hg, tq, 1), lambda g, i: (g, i, 0)), pl.BlockSpec((None, None, 8, 128), lambda g, i: (g, i, 0, 0))],
        out_shape=[jax.ShapeDtypeStruct((H, S, dh), BF16), jax.ShapeDtypeStruct((H, S, 1), F32),
                   jax.ShapeDtypeStruct((H // hg, S // tq, 8, 128), F32)],
        scratch_shapes=[pltpu.VMEM((hg, tq, dh), F32), pltpu.VMEM((hg, tq, 1), F32)],
        compiler_params=_cp("parallel", "parallel"),
    )(q, k, v)


def _sb_bwd(q, k, v, do, tot, cnt):
    H, S, dh = q.shape
    tq = min(SB_TQ, S)
    kb_per_q = tq // SB_TK
    scale = SB_DH ** -0.5
    hg = SB_HG

    def body(q_ref, k_ref, v_ref, do_ref, tot_ref, cnt_ref, dq_ref, dk_ref, dv_ref, dq_acc, pre, gpre):
        i = pl.program_id(1)
        q0 = i * tq
        upto = _tri2(lambda j, s: j <= s)
        before = _tri2(lambda j, s: j < s)

        @pl.when(i == 0)
        def _():
            dk_ref[...] = jnp.zeros_like(dk_ref)
            dv_ref[...] = jnp.zeros_like(dv_ref)

        dq_acc[...] = jnp.zeros_like(dq_acc)
        pre[...] = jnp.zeros_like(pre)
        gpre[...] = jnp.zeros_like(gpre)

        n_done = jnp.max(cnt_ref[...]).astype(jnp.int32)
        first = (i + 1) * kb_per_q - n_done

        def step(n, carry):
            k0 = pl.multiple_of((first + n) * SB_TK, SB_TK)
            heads = range(hg)
            old = [(dq_acc[h], dk_ref[h, pl.ds(k0, SB_TK), :], dv_ref[h, pl.ds(k0, SB_TK), :], pre[h], gpre[h]) for h in heads]
            new = []
            for h in heads:
                dq_o, dk_o, dv_o, pre_o, gpre_o = old[h]
                kb, vb = k_ref[h, pl.ds(k0, SB_TK), :], v_ref[h, pl.ds(k0, SB_TK), :]
                qv, dov = q_ref[h], do_ref[h]
                z, valid, ls, l1m = _sb_scores(qv, kb, k0, q0, tq)
                incl = jnp.dot(_split2(l1m), upto, preferred_element_type=F32)
                rest = tot_ref[h] - (pre_o + incl)
                a = jnp.where(valid, jnp.exp(ls + rest), 0.0)
                da = lax.dot_general(dov, vb, NT, preferred_element_type=F32)
                g = a * da
                gbefore = jnp.dot(_split2(g), before, preferred_element_type=F32) + gpre_o
                dz = jnp.where(valid, g * jnp.exp(ls - z) - jnp.exp(ls) * gbefore, 0.0) * scale
                dzb = dz.astype(BF16)
                new.append((dq_o + jnp.dot(dzb, kb, preferred_element_type=F32),
                            dk_o + lax.dot_general(dzb, qv, TN, preferred_element_type=F32),
                            dv_o + lax.dot_general(a.astype(BF16), dov, TN, preferred_element_type=F32),
                            pre_o + jnp.sum(l1m, axis=1, keepdims=True), gpre_o + jnp.sum(g, axis=1, keepdims=True)))
            for h in heads:
                dq_acc[h], dk_ref[h, pl.ds(k0, SB_TK), :], dv_ref[h, pl.ds(k0, SB_TK), :], pre[h], gpre[h] = new[h]
            return carry

        lax.fori_loop(0, n_done, step, 0)
        dq_ref[...] = dq_acc[...].astype(dq_ref.dtype)

    qs = pl.BlockSpec((hg, tq, dh), lambda g, i: (g, i, 0))
    full = pl.BlockSpec((hg, S, dh), lambda g, i: (g, 0, 0))
    return pl.pallas_call(
        body, name="sb_bwd", grid=(H // hg, S // tq),
        in_specs=[qs, full, full, qs, pl.BlockSpec((hg, tq, 1), lambda g, i: (g, i, 0)),
                  pl.BlockSpec((None, None, 8, 128), lambda g, i: (g, i, 0, 0))],
        out_specs=[qs, full, full],
        out_shape=[jax.ShapeDtypeStruct((H, S, dh), BF16), jax.ShapeDtypeStruct((H, S, dh), F32), jax.ShapeDtypeStruct((H, S, dh), F32)],
        scratch_shapes=[pltpu.VMEM((hg, tq, dh), F32), pltpu.VMEM((hg, tq, 1), F32), pltpu.VMEM((hg, tq, 1), F32)],
        compiler_params=_cp("parallel", "arbitrary"),
    )(q, k, v, do, tot, cnt)


_INV_SQRT2 = 0.7071067811865476
_INV_SQRT2PI = 0.3989422804014327


def _gelu(x):
    return 0.5 * x * (1.0 + lax.erf(x * _INV_SQRT2))


def _gelu_grad(x):
    return 0.5 * (1.0 + lax.erf(x * _INV_SQRT2)) + x * (_INV_SQRT2PI * jnp.exp(-0.5 * x * x))


def _sgu_mask():
    t = lax.broadcasted_iota(jnp.int32, (SGU_LEN, SGU_LEN), 0) // SGU_CHUNK
    s = lax.broadcasted_iota(jnp.int32, (SGU_LEN, SGU_LEN), 1) // SGU_CHUNK
    return t >= s


def _sgu_mask_t():
    t = lax.broadcasted_iota(jnp.int32, (SGU_LEN, SGU_LEN), 0) // SGU_CHUNK
    s = lax.broadcasted_iota(jnp.int32, (SGU_LEN, SGU_LEN), 1) // SGU_CHUNK
    return s >= t


def _sgu_norm(zv, g, b):
    vv = _gelu(zv)
    xc = vv - jnp.mean(vv, axis=-1, keepdims=True)
    rstd = lax.rsqrt(jnp.mean(xc * xc, axis=-1, keepdims=True) + 1e-5)
    xhat = xc * rstd
    return xhat, rstd, xhat * g + b


SGU_TM = 256


def _sgu_fwd(p, ln_g, ln_b, w_s, b_st):
    S = p.shape[0]
    tm = min(SGU_TM, S)

    def body(zu_ref, zv_ref, g_ref, b_ref, w_ref, bs_ref, o_ref):
        u = _gelu(zu_ref[...])
        _, _, vn = _sgu_norm(zv_ref[...], g_ref[...], b_ref[...])
        vnb = vn.astype(BF16)
        mask = _sgu_mask()
        for gi in range(SGU_GROUPS):
            wg = jnp.where(mask, w_ref[gi], 0.0).astype(BF16)
            cols = slice(gi * SGU_GD, (gi + 1) * SGU_GD)
            for ci in range(tm // SGU_LEN):
                rows = slice(ci * SGU_LEN, (ci + 1) * SGU_LEN)
                vm = jnp.dot(wg, vnb[rows, cols], preferred_element_type=F32) + bs_ref[:, gi:gi + 1]
                o_ref[rows, cols] = (u[rows, cols] * vm).astype(BF16)

    vec = pl.BlockSpec((1, BW), lambda i: (0, 0))
    return pl.pallas_call(
        body, name="sgu_fwd", grid=(S // tm,),
        in_specs=[pl.BlockSpec((tm, BW), lambda i: (i, C_Z // BW)), pl.BlockSpec((tm, BW), lambda i: (i, C_Z // BW + 1)), vec, vec,
                  pl.BlockSpec((SGU_GROUPS, SGU_LEN, SGU_LEN), lambda i: (0, 0, 0)), pl.BlockSpec((SGU_LEN, SGU_GROUPS), lambda i: (0, 0))],
        out_specs=pl.BlockSpec((tm, BW), lambda i: (i, 0)), out_shape=jax.ShapeDtypeStruct((S, BW), BF16),
        compiler_params=_cp("parallel"),
    )(p, p, ln_g.reshape(1, BW), ln_b.reshape(1, BW), w_s, b_st)


def _sgu_bwd(p, dyb, ln_g, ln_b, w_s, w_st, b_st):
    S = p.shape[0]
    tm = min(SGU_TM, S)

    def body(zu_ref, zv_ref, dy_ref, g_ref, b_ref, w_ref, wt_ref, bs_ref, dz_ref, dg_ref, db_ref, dw_ref, dbs_ref, dvn):
        first = pl.program_id(0) == 0

        @pl.when(first)
        def _():
            dg_ref[...] = jnp.zeros_like(dg_ref)
            db_ref[...] = jnp.zeros_like(db_ref)
            dw_ref[...] = jnp.zeros_like(dw_ref)
            dbs_ref[...] = jnp.zeros_like(dbs_ref)

        zu, zv, dy = zu_ref[...], zv_ref[...], dy_ref[...].astype(F32)
        u = _gelu(zu)
        xhat, rstd, vn = _sgu_norm(zv, g_ref[...], b_ref[...])
        vnb = vn.astype(BF16)
        mask = _sgu_mask()
        mask_t = _sgu_mask_t()
        for gi in range(SGU_GROUPS):
            wg = jnp.where(mask, w_ref[gi], 0.0).astype(BF16)
            wgt = jnp.where(mask_t, wt_ref[gi], 0.0).astype(BF16)
            cols = slice(gi * SGU_GD, (gi + 1) * SGU_GD)
            for ci in range(tm // SGU_LEN):
                rows = slice(ci * SGU_LEN, (ci + 1) * SGU_LEN)
                vm = jnp.dot(wg, vnb[rows, cols], preferred_element_type=F32) + bs_ref[:, gi:gi + 1]
                dyc = dy[rows, cols]
                dz_ref[rows, cols] = (dyc * vm * _gelu_grad(zu[rows, cols])).astype(BF16)
                dvm = dyc * u[rows, cols]
                dvmb = dvm.astype(BF16)
                dbs_ref[gi] += jnp.broadcast_to(jnp.sum(dvm, axis=1, keepdims=True), (SGU_LEN, SGU_GD))
                dw_ref[gi] += lax.dot_general(dvmb, vnb[rows, cols], NT, preferred_element_type=F32)
                dvn[rows, cols] = jnp.dot(wgt, dvmb, preferred_element_type=F32)
        dvnv = dvn[...]
        dg_ref[...] += jnp.sum(dvnv * xhat, axis=0, keepdims=True)
        db_ref[...] += jnp.sum(dvnv, axis=0, keepdims=True)
        dxh = dvnv * g_ref[...]
        dvv = rstd * (dxh - jnp.mean(dxh, axis=-1, keepdims=True) - xhat * jnp.mean(dxh * xhat, axis=-1, keepdims=True))
        dz_ref[:, BW:] = (dvv * _gelu_grad(zv)).astype(BF16)

        @pl.when(pl.program_id(0) == n_steps - 1)
        def _():
            for gi in range(SGU_GROUPS):
                dw_ref[gi] = jnp.where(mask, dw_ref[gi], 0.0)

    n_steps = S // tm
    vec = pl.BlockSpec((1, BW), lambda i: (0, 0))
    half = lambda c: pl.BlockSpec((tm, BW), lambda i: (i, c))
    wspec = pl.BlockSpec((SGU_GROUPS, SGU_LEN, SGU_LEN), lambda i: (0, 0, 0))
    dz, dg, db, dw, dbs = pl.pallas_call(
        body, name="sgu_bwd", grid=(n_steps,),
        in_specs=[half(C_Z // BW), half(C_Z // BW + 1), half(0), vec, vec, wspec, wspec,
                  pl.BlockSpec((SGU_LEN, SGU_GROUPS), lambda i: (0, 0))],
        out_specs=[pl.BlockSpec((tm, 2 * BW), lambda i: (i, 0)), vec, vec, wspec, wspec],
        out_shape=[jax.ShapeDtypeStruct((S, 2 * BW), BF16), jax.ShapeDtypeStruct((1, BW), F32), jax.ShapeDtypeStruct((1, BW), F32),
                   jax.ShapeDtypeStruct((SGU_GROUPS, SGU_LEN, SGU_LEN), F32), jax.ShapeDtypeStruct((SGU_GROUPS, SGU_LEN, SGU_GD), F32)],
        scratch_shapes=[pltpu.VMEM((tm, BW), F32)],
        compiler_params=_cp("arbitrary"),
    )(p, p, dyb, ln_g.reshape(1, BW), ln_b.reshape(1, BW), w_s, w_st, b_st)
    return dz, dg.reshape(BW), db.reshape(BW), dw, dbs[:, :, 0]


CONV_TC = 128


def _shift_down(y, n):
    rows = lax.broadcasted_iota(jnp.int32, y.shape, 0)
    return jnp.where(rows < n, 0.0, pltpu.roll(y, n, 0))


def _shift_up(y, n):
    rows = lax.broadcasted_iota(jnp.int32, y.shape, 0)
    return jnp.where(rows >= y.shape[0] - n, 0.0, pltpu.roll(y, y.shape[0] - n, 0))


def _conv_specs(S):
    col = lambda c0: pl.BlockSpec((S, CONV_TC), lambda j: (0, c0 // CONV_TC + j))
    return col(C_CB), col(C_CB + BW), col(C_CB + 2 * BW), pl.BlockSpec((3, CONV_TC), lambda j: (0, j)), pl.BlockSpec((S, CONV_TC), lambda j: (0, j))


def _conv_fwd(p, conv_w):
    S = p.shape[0]

    def body(cb_ref, cc_ref, cx_ref, w_ref, o_ref):
        y = cc_ref[...] * cx_ref[...]
        conv = w_ref[0:1, :] * _shift_down(y, 2) + w_ref[1:2, :] * _shift_down(y, 1) + w_ref[2:3, :] * y
        o_ref[...] = (cb_ref[...] * conv).astype(BF16)

    cb, cc, cx, wspec, out = _conv_specs(S)
    return pl.pallas_call(
        body, name="conv_fwd", grid=(BW // CONV_TC,), in_specs=[cb, cc, cx, wspec], out_specs=out,
        out_shape=jax.ShapeDtypeStruct((S, BW), BF16), compiler_params=_cp("parallel"),
    )(p, p, p, conv_w)


def _conv_bwd(p, conv_w, dyc):
    S = p.shape[0]

    def body(cb_ref, cc_ref, cx_ref, w_ref, dy_ref, db_ref, dc_ref, dx_ref, dw_ref):
        cc, cx, dy = cc_ref[...], cx_ref[...], dy_ref[...].astype(F32)
        y = cc * cx
        w0, w1, w2 = w_ref[0:1, :], w_ref[1:2, :], w_ref[2:3, :]
        y1, y2 = _shift_down(y, 1), _shift_down(y, 2)
        conv = w0 * y2 + w1 * y1 + w2 * y
        db_ref[...] = (dy * conv).astype(BF16)
        dconv = dy * cb_ref[...]
        dyy = w2 * dconv + w1 * _shift_up(dconv, 1) + w0 * _shift_up(dconv, 2)
        dc_ref[...] = (dyy * cx).astype(BF16)
        dx_ref[...] = (dyy * cc).astype(BF16)
        dw_ref[0:1, :] = jnp.sum(dconv * y2, axis=0, keepdims=True)
        dw_ref[1:2, :] = jnp.sum(dconv * y1, axis=0, keepdims=True)
        dw_ref[2:3, :] = jnp.sum(dconv * y, axis=0, keepdims=True)

    cb, cc, cx, wspec, out = _conv_specs(S)
    db, dc, dx, dw = pl.pallas_call(
        body, name="conv_bwd", grid=(BW // CONV_TC,), in_specs=[cb, cc, cx, wspec, out],
        out_specs=[out, out, out, wspec],
        out_shape=[jax.ShapeDtypeStruct((S, BW), BF16)] * 3 + [jax.ShapeDtypeStruct((3, BW), F32)],
        compiler_params=_cp("parallel"),
    )(p, p, p, conv_w, dyc)
    return db, dc, dx, dw


def _merge_specs(S, tm):
    gate = lambda n: pl.BlockSpec((tm, D_MODEL), lambda i: (i, C_GATES // D_MODEL + n))
    return [gate(0), gate(1), gate(2)], pl.BlockSpec((3, tm, D_MODEL), lambda i: (0, i, 0)), pl.BlockSpec((tm, D_MODEL), lambda i: (i, 0))


def _merge_fwd(p, bd):
    S = p.shape[0]
    tm = _tile(S, (256,))

    def body(g0, g1, g2, b_ref, o_ref):
        acc = jax.nn.sigmoid(g0[...]) * b_ref[0]
        acc = acc + jax.nn.sigmoid(g1[...]) * b_ref[1]
        acc = acc + jax.nn.sigmoid(g2[...]) * b_ref[2]
        o_ref[...] = acc.astype(BF16)

    gates, bspec, row = _merge_specs(S, tm)
    return pl.pallas_call(
        body, name="merge_fwd", grid=(S // tm,), in_specs=gates + [bspec], out_specs=row,
        out_shape=jax.ShapeDtypeStruct((S, D_MODEL), BF16), compiler_params=_cp("parallel"),
    )(p, p, p, bd)


def _merge_bwd(p, bd, dm):
    S = p.shape[0]
    tm = _tile(S, (256,))

    def body(g0, g1, g2, b_ref, dm_ref, db_ref, dg_ref):
        dmv = dm_ref[...]
        for n, g_ref in enumerate((g0, g1, g2)):
            sg = jax.nn.sigmoid(g_ref[...])
            db_ref[n] = (dmv * sg).astype(BF16)
            dg_ref[:, n * D_MODEL:(n + 1) * D_MODEL] = (dmv * b_ref[n] * (sg * (1.0 - sg))).astype(BF16)

    gates, bspec, row = _merge_specs(S, tm)
    return pl.pallas_call(
        body, name="merge_bwd", grid=(S // tm,), in_specs=gates + [bspec, row],
        out_specs=[bspec, pl.BlockSpec((tm, 3 * D_MODEL), lambda i: (i, 0))],
        out_shape=[jax.ShapeDtypeStruct((3, S, D_MODEL), BF16), jax.ShapeDtypeStruct((S, 3 * D_MODEL), BF16)],
        compiler_params=_cp("parallel"),
    )(p, p, p, bd, dm)


XA_TM = 512


def _xa_probs(qh, kh):
    s = lax.dot_general(qh, kh, NT, preferred_element_type=F32) * (XA_DH ** -0.5)
    e = jnp.exp(s - jnp.max(s, axis=-1, keepdims=True))
    return e / jnp.sum(e, axis=-1, keepdims=True)


def _xa_fwd(q, kv):
    S = q.shape[0]
    tm = min(XA_TM, S)
    M = kv.shape[1]

    def body(q_ref, kv_ref, o_ref):
        for h in range(XA_HEADS):
            cols = slice(h * XA_DH, (h + 1) * XA_DH)
            pr = _xa_probs(q_ref[:, cols], kv_ref[0, :, cols])
            o_ref[:, cols] = jnp.dot(pr.astype(BF16), kv_ref[1, :, cols], preferred_element_type=F32).astype(BF16)

    row = pl.BlockSpec((tm, D_MODEL), lambda i: (i, 0))
    return pl.pallas_call(
        body, name="xa_fwd", grid=(S // tm,), in_specs=[row, pl.BlockSpec((2, M, D_MODEL), lambda i: (0, 0, 0))], out_specs=row,
        out_shape=jax.ShapeDtypeStruct((S, D_MODEL), BF16), compiler_params=_cp("parallel"),
    )(q, kv)


def _xa_bwd(q, kv, do):
    S = q.shape[0]
    tm = min(XA_TM, S)
    M = kv.shape[1]

    def body(q_ref, kv_ref, do_ref, dq_ref, dkv_ref):
        @pl.when(pl.program_id(0) == 0)
        def _():
            dkv_ref[...] = jnp.zeros_like(dkv_ref)

        for h in range(XA_HEADS):
            cols = slice(h * XA_DH, (h + 1) * XA_DH)
            qh, kh, vh, doh = q_ref[:, cols], kv_ref[0, :, cols], kv_ref[1, :, cols], do_ref[:, cols]
            pr = _xa_probs(qh, kh)
            dkv_ref[1, :, cols] += lax.dot_general(pr.astype(BF16), doh, TN, preferred_element_type=F32)
            dp = lax.dot_general(doh, vh, NT, preferred_element_type=F32)
            ds = (pr * (dp - jnp.sum(dp * pr, axis=-1, keepdims=True)) * (XA_DH ** -0.5)).astype(BF16)
            dq_ref[:, cols] = jnp.dot(ds, kh, preferred_element_type=F32).astype(BF16)
            dkv_ref[0, :, cols] += lax.dot_general(ds, qh, TN, preferred_element_type=F32)

    row = pl.BlockSpec((tm, D_MODEL), lambda i: (i, 0))
    kvs = pl.BlockSpec((2, M, D_MODEL), lambda i: (0, 0, 0))
    return pl.pallas_call(
        body, name="xa_bwd", grid=(S // tm,), in_specs=[row, kvs, row], out_specs=[row, kvs],
        out_shape=[jax.ShapeDtypeStruct((S, D_MODEL), BF16), jax.ShapeDtypeStruct((2, M, D_MODEL), F32)],
        compiler_params=_cp("arbitrary"),
    )(q, kv, do)


def _swiglu_fwd(ab):
    nb, _, S, C = ab.shape
    tm = _tile(S, (512, 256))

    def body(ab_ref, o_ref):
        a = ab_ref[0]
        o_ref[...] = (a * jax.nn.sigmoid(a) * ab_ref[1]).astype(BF16)

    return pl.pallas_call(
        body, name="swiglu_fwd", grid=(nb, S // tm),
        in_specs=[pl.BlockSpec((None, 2, tm, C), lambda j, i: (j, 0, i, 0))], out_specs=pl.BlockSpec((None, tm, C), lambda j, i: (j, i, 0)),
        out_shape=jax.ShapeDtypeStruct((nb, S, C), BF16), compiler_params=_cp("parallel", "parallel"),
    )(ab)


def _swiglu_bwd(ab, dh):
    nb, _, S, C = ab.shape
    tm = _tile(S, (512, 256))

    def body(ab_ref, dh_ref, o_ref):
        a, b, d = ab_ref[0], ab_ref[1], dh_ref[...].astype(F32)
        sg = jax.nn.sigmoid(a)
        o_ref[0] = (d * b * (sg * (1.0 + a * (1.0 - sg)))).astype(BF16)
        o_ref[1] = (d * (a * sg)).astype(BF16)

    pair = pl.BlockSpec((None, 2, tm, C), lambda j, i: (j, 0, i, 0))
    return pl.pallas_call(
        body, name="swiglu_bwd", grid=(nb, S // tm), in_specs=[pair, pl.BlockSpec((None, tm, C), lambda j, i: (j, i, 0))], out_specs=pair,
        out_shape=jax.ShapeDtypeStruct(ab.shape, BF16), compiler_params=_cp("parallel", "parallel"),
    )(ab, dh)


def _reduce_adam(parts, w, m, v, *, name):
    shape = w.shape
    C = shape[-1]
    R = math.prod(shape[:-1])
    tm = _rows(R, 4 * C)
    n = len(parts)
    c1, c2 = 1.0 - ADAM_B1 ** ADAM_STEP, 1.0 - ADAM_B2 ** ADAM_STEP

    def body(*refs):
        g = refs[0][...]
        for r in refs[1:n]:
            g = g + r[...]
        w_ref, m_ref, v_ref, go, do, mo, vo = refs[n:]
        mn = ADAM_B1 * m_ref[...] + (1.0 - ADAM_B1) * g
        vn = ADAM_B2 * v_ref[...] + (1.0 - ADAM_B2) * (g * g)
        go[...] = g
        do[...] = -ADAM_LR * ((mn / c1) / (jnp.sqrt(vn / c2) + ADAM_EPS) + ADAM_WD * w_ref[...])
        mo[...] = mn
        vo[...] = vn

    row = pl.BlockSpec((tm, C), lambda i: (i, 0))
    outs = pl.pallas_call(
        body, name=name, grid=(R // tm,), in_specs=[row] * (n + 3), out_specs=[row] * 4,
        out_shape=[jax.ShapeDtypeStruct((R, C), F32)] * 4, compiler_params=_cp("parallel"),
    )(*[a.reshape(R, C) for a in (*parts, w, m, v)])
    return tuple(o.reshape(shape) for o in outs)


def _add2(a0, a1, r, sel, *, name):
    shape = r.shape
    C = shape[-1]
    R = math.prod(shape[:-1])
    tm = _rows(R, 4 * C)

    def body(s_ref, a0_ref, a1_ref, r_ref, o_ref):
        @pl.when(s_ref[0] == 0)
        def _():
            o_ref[...] = (a0_ref[...] + r_ref[...]).astype(BF16)

        @pl.when(s_ref[0] != 0)
        def _():
            o_ref[...] = (a1_ref[...] + r_ref[...]).astype(BF16)

    row = pl.BlockSpec((tm, C), lambda i, s: (i, 0))
    pick0 = pl.BlockSpec((tm, C), lambda i, s: (jnp.where(s[0] == 0, i, 0), 0))
    pick1 = pl.BlockSpec((tm, C), lambda i, s: (jnp.where(s[0] == 0, 0, i), 0))
    return pl.pallas_call(
        body, name=name,
        grid_spec=pltpu.PrefetchScalarGridSpec(num_scalar_prefetch=1, grid=(R // tm,), in_specs=[pick0, pick1, row], out_specs=row),
        out_shape=jax.ShapeDtypeStruct((R, C), BF16), compiler_params=_cp("arbitrary"),
    )(sel.reshape(1).astype(jnp.int32), a0.reshape(R, C), a1.reshape(R, C), r.reshape(R, C)).reshape(shape)


_VIEW = {
    "w_in": ((1024, 7168), (1024, 1792), 256, lambda i, b: (i, b)),
    "w_br": ((1536, 1024), (1536, 256), 512, lambda i, b: (i, b)),
    "w_sq": ((5120, 1024), (1280, 1024), 256, lambda i, b: (4 * i + b, 0)),
    "w_gu": ((8192, 704), (2048, 704), 512, lambda i, b: (4 * b + i, 0)),
    "w_dn": ((2816, 1024), (704, 1024), 352, lambda i, b: (2 * b + i, 0)),
}


def _sum_chips(name, slots, part, b):
    full2, sh2, tm, idx = _VIEW[name]
    C = sh2[1]

    def body(b_ref, s_ref, own_ref, o_ref):
        o_ref[...] = ((s_ref[0].astype(F32) + s_ref[1].astype(F32)) + s_ref[2].astype(F32)) + own_ref[...].astype(F32)

    return pl.pallas_call(
        body, name="sum_chips_" + name,
        grid_spec=pltpu.PrefetchScalarGridSpec(
            num_scalar_prefetch=1, grid=(sh2[0] // tm,),
            in_specs=[pl.BlockSpec((3, tm, C), lambda i, bs: (0, i, 0)), pl.BlockSpec((tm, C), lambda i, bs: idx(i, bs[0]))],
            out_specs=pl.BlockSpec((tm, C), lambda i, bs: (i, 0))),
        out_shape=jax.ShapeDtypeStruct(sh2, F32), compiler_params=_cp("arbitrary"),
    )(b.reshape(1).astype(jnp.int32), slots.reshape((3,) + sh2), part.reshape(full2)).reshape(_SHARD_SHAPE[name])


def _adam_layers(g_own, g_recv, c, w, m, v, *, name):
    shape = w.shape
    C = shape[-1]
    R = math.prod(shape[1:-1])
    tm = _rows(R, 4 * C)
    c1, c2 = 1.0 - ADAM_B1 ** ADAM_STEP, 1.0 - ADAM_B2 ** ADAM_STEP

    def body(c_ref, own_ref, recv_ref, w_ref, m_ref, v_ref, go, do, mo, vo):
        g = jnp.where(pl.program_id(0) == c_ref[0], own_ref[...], recv_ref[...])
        mn = ADAM_B1 * m_ref[...] + (1.0 - ADAM_B1) * g
        vn = ADAM_B2 * v_ref[...] + (1.0 - ADAM_B2) * (g * g)
        go[...] = g
        do[...] = -ADAM_LR * ((mn / c1) / (jnp.sqrt(vn / c2) + ADAM_EPS) + ADAM_WD * w_ref[...])
        mo[...] = mn
        vo[...] = vn

    own = pl.BlockSpec((tm, C), lambda l, i, cs: (jnp.where(l == cs[0], i, 0), 0))
    recv = pl.BlockSpec((tm, C), lambda l, i, cs: (jnp.where(l == cs[0], 0, i), 0))
    row = pl.BlockSpec((None, tm, C), lambda l, i, cs: (l, i, 0))
    outs = pl.pallas_call(
        body, name=name,
        grid_spec=pltpu.PrefetchScalarGridSpec(num_scalar_prefetch=1, grid=(DEPTH, R // tm), in_specs=[own, recv, row, row, row], out_specs=[row] * 4),
        out_shape=[jax.ShapeDtypeStruct((DEPTH, R, C), F32)] * 4, compiler_params=_cp("arbitrary", "arbitrary"),
    )(c.reshape(1).astype(jnp.int32), g_own.reshape(R, C), g_recv.reshape(R, C), *[t.reshape(DEPTH, R, C) for t in (w, m, v)])
    return tuple(o.reshape(shape) for o in outs)


def _sum_slots(r, *, name):
    n, shape = r.shape[0], r.shape[1:]
    C = shape[-1]
    R = math.prod(shape[:-1])
    tm = _rows(R, 4 * C * n, 8 << 20)

    def body(r_ref, o_ref):
        acc = r_ref[0]
        for s in range(1, n):
            acc = acc + r_ref[s]
        o_ref[...] = acc

    return pl.pallas_call(
        body, name=name, grid=(R // tm,), in_specs=[pl.BlockSpec((n, tm, C), lambda i: (0, i, 0))],
        out_specs=pl.BlockSpec((tm, C), lambda i: (i, 0)), out_shape=jax.ShapeDtypeStruct((R, C), F32), compiler_params=_cp("parallel"),
    )(r.reshape(n, R, C)).reshape(shape)


def _heads(t, S):
    n = t.shape[1] // BW
    return t.reshape(S, n * SB_HEADS, SB_DH).transpose(1, 0, 2)


def _unheads(t, S):
    return t.transpose(1, 0, 2).reshape(S, BW)


def _layer_fwd(x, mem, wl):
    S = x.shape[0]
    sv = {"x": x}
    h1 = _rms_fwd(x, wl["norm_mix_g"], name="rms_mix")
    p = _mm(h1, wl["w_in"], mode="nn", name="mm_in")
    qkv = _heads(p[:, :3 * BW].astype(BF16), S)
    q, k, v = qkv[:SB_HEADS], qkv[SB_HEADS:2 * SB_HEADS], qkv[2 * SB_HEADS:]
    ya, tot, cnt = _sb_fwd(q, k, v)
    b_st = wl["b_spatial"].T
    yb = _sgu_fwd(p, wl["sgu_ln_g"], wl["sgu_ln_b"], wl["w_spatial"], b_st)
    yc = _conv_fwd(p, wl["conv_w"])
    br = jnp.stack([_unheads(ya, S), yb, yc])
    bd = _mm(br, wl["w_br"], mode="nn", a_kind="batch", b_kind="batch", name="mm_branch")
    merged = _merge_fwd(p, bd)
    x1 = _mm(merged, wl["w_sq"][0], mode="nn", res=x, name="mm_out")
    h2 = _rms_fwd(x1, wl["norm_xa_g"], name="rms_xa")
    qx = _mm(h2, wl["w_sq"][1], mode="nn", out_dtype=BF16, name="mm_q")
    mn = _rms_fwd(mem, wl["mem_norm_g"], name="rms_mem")
    kv = _mm(mn, wl["w_sq"][3:5], mode="nn", b_kind="batch", out_dtype=BF16, name="mm_kv")
    o = _xa_fwd(qx, kv)
    x2 = _mm(o, wl["w_sq"][2], mode="nn", res=x1, name="mm_o")
    h3 = _rms_fwd(x2, wl["norm_ffn_g"], name="rms_ffn")
    ab = _mm(h3, wl["w_gu"], mode="nn", b_kind="batch", name="mm_gu").reshape(N_CHIPS, 2, S, FFN_SH)
    hh = _swiglu_fwd(ab)
    x3 = _mm(hh, wl["w_dn"], mode="nn", a_kind="kchunk", b_kind="kchunk", res=x2, name="mm_down")
    sv.update(h1=h1, p=p, q=q, k=k, v=v, tot=tot, cnt=cnt, br=br, bd=bd, merged=merged, x1=x1, h2=h2, qx=qx, mn=mn, kv=kv, o=o,
              x2=x2, h3=h3, ab=ab, hh=hh, b_st=b_st)
    return x3, sv


def _layer_bwd(dx3, mem, wl, sv):
    S = dx3.shape[0]
    p = sv["p"]
    g = {}
    dhh = _mm(dx3, wl["w_dn"], mode="nt", b_kind="batch", name="mm_down_dx")
    g["w_dn"] = _mm(sv["hh"], dx3, mode="tn", a_kind="batch", name="mm_down_dw")
    dab = _swiglu_bwd(sv["ab"], dhh).reshape(2 * N_CHIPS, S, FFN_SH)
    g["w_gu"] = _mm(sv["h3"], dab, mode="tn", b_kind="batch", name="mm_gu_dw")
    dh3 = _mm(dab, wl["w_gu"], mode="nt", a_kind="kchunk", b_kind="kchunk", name="mm_gu_dx")
    dx2, g["norm_ffn_g"] = _rms_bwd(sv["x2"], wl["norm_ffn_g"], dh3, dx3, name="rms_ffn_bwd")
    do = _mm(dx2, wl["w_sq"][2], mode="nt", out_dtype=BF16, name="mm_o_dx")
    dw_o = _mm(sv["o"], dx2, mode="tn", name="mm_o_dw")
    dq, dkv = _xa_bwd(sv["qx"], sv["kv"], do)
    dw_q = _mm(sv["h2"], dq, mode="tn", name="mm_q_dw")
    dh2 = _mm(dq, wl["w_sq"][1], mode="nt", name="mm_q_dx")
    dw_kv = _mm(sv["mn"], dkv, mode="tn", b_kind="batch", name="mm_kv_dw")
    dmn = _mm(dkv, wl["w_sq"][3:5], mode="nt", a_kind="kchunk", b_kind="kchunk", name="mm_kv_dx")
    _, g["mem_norm_g"] = _rms_bwd(mem, wl["mem_norm_g"], dmn, jnp.zeros_like(mem), name="rms_mem_bwd")
    dx1, g["norm_xa_g"] = _rms_bwd(sv["x1"], wl["norm_xa_g"], dh2, dx2, name="rms_xa_bwd")
    dm = _mm(dx1, wl["w_sq"][0], mode="nt", name="mm_out_dx")
    dw_out = _mm(sv["merged"], dx1, mode="tn", name="mm_out_dw")
    g["w_sq"] = jnp.concatenate([jnp.stack([dw_out, dw_q, dw_o]), dw_kv])
    dbd, dgates = _merge_bwd(p, sv["bd"], dm)
    dbr = _mm(dbd, wl["w_br"], mode="nt", a_kind="batch", b_kind="batch", name="mm_branch_dx")
    g["w_br"] = _mm(sv["br"], dbd, mode="tn", a_kind="batch", b_kind="batch", name="mm_branch_dw")
    dya = dbr[0].astype(BF16).reshape(S, SB_HEADS, SB_DH).transpose(1, 0, 2)
    dq_h, dk_h, dv_h = _sb_bwd(sv["q"], sv["k"], sv["v"], dya, sv["tot"], sv["cnt"])
    dz, g["sgu_ln_g"], g["sgu_ln_b"], g["w_spatial"], g["b_spatial"] = _sgu_bwd(
        p, dbr[1], wl["sgu_ln_g"], wl["sgu_ln_b"], wl["w_spatial"], wl["w_spatial"].transpose(0, 2, 1), sv["b_st"])
    dcb, dcc, dcx, g["conv_w"] = _conv_bwd(p, wl["conv_w"], dbr[2])
    dp = jnp.concatenate([_unheads(dq_h, S), _unheads(dk_h, S).astype(BF16), _unheads(dv_h, S).astype(BF16),
                          dz, dcb, dcc, dcx, dgates], axis=1)
    g["w_in"] = _mm(sv["h1"], dp, mode="tn", name="mm_in_dw")
    dh1 = _mm(dp, wl["w_in"], mode="nt", name="mm_in_dx")
    dx, g["norm_mix_g"] = _rms_bwd(sv["x"], wl["norm_mix_g"], dh1, dx1, name="rms_mix_bwd")
    return dx, g


def _local_step(x, mem, target, layers, final_g):
    h, saved = x, []
    for wl in layers:
        h, sv = _layer_fwd(h, mem, wl)
        saved.append(sv)
    loss, dx, d_final = _loss_head(h, final_g, target)
    grads = [None] * len(layers)
    for l in reversed(range(len(layers))):
        dx, grads[l] = _layer_bwd(dx, mem, layers[l], saved[l])
    return loss, dx, grads, d_final


_ALL = slice(None)
CONV_ROWS = 8
_SHARD = {
    "w_in": lambda b: (_ALL, pl.ds(1792 * b, 1792)),
    "w_br": lambda b: (_ALL, _ALL, pl.ds(256 * b, 256)),
    "w_sq": lambda b: (_ALL, pl.ds(256 * b, 256), _ALL),
    "w_gu": lambda b: (b,),
    "w_dn": lambda b: (b,),
    "conv_w": lambda b: (_ALL, pl.ds(128 * b, 128)),
}
_FULL_SHAPE = {"w_in": (1024, 7168), "w_br": (3, 512, 1024), "w_sq": (5, 1024, 1024), "w_gu": (4, 2, 1024, 704),
               "w_dn": (4, 704, 1024), "conv_w": (CONV_ROWS, 512)}
_SHARD_SHAPE = {"w_in": (1024, 1792), "w_br": (3, 512, 256), "w_sq": (5, 256, 1024), "w_gu": (2, 1024, 704),
                "w_dn": (704, 1024), "conv_w": (CONV_ROWS, 128)}


def _pos():
    return lax.axis_index("x"), lax.axis_index("y"), lax.axis_index("c")


def _per_chip(fn):
    x, y, _ = _pos()
    for x0 in (0, 1):
        for y0 in (0, 1):
            @pl.when((x == x0) & (y == y0))
            def _():
                fn(x0, y0)


def _other_chips(x0, y0):
    return [(1 - x0, y0), (x0, 1 - y0), (1 - x0, 1 - y0)]


def _rcopy(src, dst, ssem, rsem, dev):
    return pltpu.make_async_remote_copy(src_ref=src, dst_ref=dst, send_sem=ssem, recv_sem=rsem, device_id=dev, device_id_type=MESH)


def _dma_sems(n):
    return pltpu.SemaphoreType.DMA((n,))


def _gather_weights(local):
    names = list(local)
    n = len(names)

    def body(*refs):
        ins, outs = refs[:n], refs[n:2 * n]
        send, recv, lsem = refs[2 * n:]
        c = lax.axis_index("c")

        def run(x0, y0):
            b0 = 2 * x0 + y0
            chips = _other_chips(x0, y0)
            shard = lambda a, layer, b: outs[a].at[(layer,) + _SHARD[names[a]](b)]
            own = [pltpu.make_async_copy(ins[a], outs[a].at[(_ALL,) + _SHARD[names[a]](b0)], lsem.at[a]) for a in range(n)]
            for cp in own:
                cp.start()
            sent = []
            for kk, (px, py) in enumerate(chips):
                for a in range(n):
                    sent.append(_rcopy(ins[a].at[c], shard(a, c, b0), send.at[6 * a + kk], recv.at[6 * a + kk], (px, py, c)))
                    sent[-1].start()
            for kk, (px, py) in enumerate(chips):
                for a in range(n):
                    landed = shard(a, c, 2 * px + py)
                    _rcopy(landed, landed, send.at[6 * a + kk], recv.at[6 * a + kk], (px, py, c)).wait_recv()
                    sent.append(_rcopy(landed, landed, send.at[6 * a + 3 + kk], recv.at[6 * a + 3 + kk], (x0, y0, 1 - c)))
                    sent[-1].start()
            for kk, (px, py) in enumerate(chips):
                for a in range(n):
                    got = shard(a, 1 - c, 2 * px + py)
                    _rcopy(got, got, send.at[6 * a + 3 + kk], recv.at[6 * a + 3 + kk], (x0, y0, 1 - c)).wait_recv()
            for cp in sent:
                cp.wait_send()
            for cp in own:
                cp.wait()

        _per_chip(run)

    outs = pl.pallas_call(
        body, name="gather_weights", in_specs=[ANY] * n, out_specs=[ANY] * n,
        out_shape=[jax.ShapeDtypeStruct((DEPTH,) + _FULL_SHAPE[nm], local[nm].dtype) for nm in names],
        scratch_shapes=[_dma_sems(6 * n), _dma_sems(6 * n), _dma_sems(n)],
    )(*[local[nm] for nm in names])
    return dict(zip(names, outs))


def _sibling_presum_exchange(g0, g1):
    names = list(g0)
    n = len(names)

    def body(*refs):
        l0, l1, outs = refs[:n], refs[n:2 * n], refs[2 * n:3 * n]
        send, recv = refs[3 * n:]
        x, y, c = _pos()
        for c0 in (0, 1):
            @pl.when(c == c0)
            def _():
                srcs = l1 if c0 == 0 else l0
                cps = [_rcopy(srcs[a], outs[a], send.at[a], recv.at[a], (x, y, 1 - c0)) for a in range(n)]
                for cp in cps:
                    cp.start()
                for cp in cps:
                    cp.wait()

    outs = pl.pallas_call(
        body, name="grad_presum_exchange", in_specs=[ANY] * (2 * n), out_specs=[ANY] * n,
        out_shape=[jax.ShapeDtypeStruct(g0[nm].shape, g0[nm].dtype) for nm in names],
        scratch_shapes=[_dma_sems(n), _dma_sems(n)],
    )(*[g0[nm] for nm in names], *[g1[nm] for nm in names])
    return dict(zip(names, outs))


def _shard_exchange(part):
    names = list(part)
    n = len(names)

    def body(*refs):
        ins, outs = refs[:n], refs[n:2 * n]
        send, recv = refs[2 * n:]
        c = lax.axis_index("c")

        def run(x0, y0):
            chips = _other_chips(x0, y0)
            sent = []
            for kk, (px, py) in enumerate(chips):
                for a in range(n):
                    sent.append(_rcopy(ins[a].at[_SHARD[names[a]](2 * px + py)], outs[a].at[kk], send.at[3 * a + kk], recv.at[3 * a + kk], (px, py, c)))
                    sent[-1].start()
            for kk, (px, py) in enumerate(chips):
                for a in range(n):
                    slot = outs[a].at[kk]
                    _rcopy(slot, slot, send.at[3 * a + kk], recv.at[3 * a + kk], (px, py, c)).wait_recv()
            for cp in sent:
                cp.wait_send()

        _per_chip(run)

    outs = pl.pallas_call(
        body, name="grad_shard_exchange", in_specs=[ANY] * n, out_specs=[ANY] * n,
        out_shape=[jax.ShapeDtypeStruct((N_CHIPS - 1,) + _SHARD_SHAPE[nm], part[nm].dtype) for nm in names],
        scratch_shapes=[_dma_sems(3 * n), _dma_sems(3 * n)],
    )(*[part[nm] for nm in names])
    return dict(zip(names, outs))


def _sibling_exchange(red):
    names = list(red)
    n = len(names)

    def body(*refs):
        ins, outs = refs[:n], refs[n:2 * n]
        send, recv = refs[2 * n:]
        x, y, c = _pos()
        cps = [_rcopy(ins[a], outs[a], send.at[a], recv.at[a], (x, y, 1 - c)) for a in range(n)]
        for cp in cps:
            cp.start()
        for cp in cps:
            cp.wait()

    outs = pl.pallas_call(
        body, name="grad_sibling_exchange", in_specs=[ANY] * n, out_specs=[ANY] * n,
        out_shape=[jax.ShapeDtypeStruct(red[nm].shape, red[nm].dtype) for nm in names],
        scratch_shapes=[_dma_sems(n), _dma_sems(n)],
    )(*[red[nm] for nm in names])
    return dict(zip(names, outs))


def _gather_small(pack):
    flips = [(fx, fy, fc) for fx in (0, 1) for fy in (0, 1) for fc in (0, 1) if fx or fy or fc]

    def body(in_ref, out_ref, send, recv, lsem):
        x, y, c = _pos()
        me = 4 * x + 2 * y + c
        own = pltpu.make_async_copy(in_ref, out_ref.at[me], lsem.at[0])
        own.start()
        peers = [(x ^ fx, y ^ fy, c ^ fc) for fx, fy, fc in flips]
        cps = [_rcopy(in_ref, out_ref.at[me], send.at[k], recv.at[k], peer) for k, peer in enumerate(peers)]
        for cp in cps:
            cp.start()
        for k, (px, py, pc) in enumerate(peers):
            slot = out_ref.at[4 * px + 2 * py + pc]
            _rcopy(slot, slot, send.at[k], recv.at[k], (px, py, pc)).wait_recv()
        for cp in cps:
            cp.wait_send()
        own.wait()

    return pl.pallas_call(
        body, name="gather_small_grads", in_specs=[ANY], out_specs=ANY,
        out_shape=jax.ShapeDtypeStruct((8,) + pack.shape, pack.dtype),
        scratch_shapes=[_dma_sems(len(flips)), _dma_sems(len(flips)), _dma_sems(1)],
    )(pack)


_WEIGHTS = ["norm_mix_g", "w_in", "sgu_ln_g", "sgu_ln_b", "w_spatial", "b_spatial", "conv_w", "w_branch", "w_out", "norm_xa_g",
            "mem_norm_g", "w_q_xa", "w_k_xa", "w_v_xa", "w_o_xa", "norm_ffn_g", "w_gate_ffn", "w_up_ffn", "w_down_ffn", "final_g"]
_REPLICATED = ["norm_mix_g", "sgu_ln_g", "sgu_ln_b", "w_spatial", "b_spatial", "norm_xa_g", "mem_norm_g", "norm_ffn_g", "final_g"]
_SQUARE = ["w_out", "w_q_xa", "w_o_xa", "w_k_xa", "w_v_xa"]
_BIG = ["w_in", "w_br", "w_sq", "w_gu", "w_dn"]


def _pack(arrs):
    return jnp.concatenate([a.reshape(-1) for a in arrs]).reshape(-1, 128)


def _step(a):
    w = {n: a[n] for n in _WEIGHTS}
    x, mem, target = a["x"][0], a["mem"][0], a["loss_target"][0]
    xi, yi, ci = _pos()
    bf = lambda n: w[n].astype(BF16)
    full = _gather_weights({
        "w_in": bf("w_in"), "w_br": bf("w_branch"), "w_sq": jnp.stack([bf(n) for n in _SQUARE], axis=1),
        "w_gu": jnp.stack([bf("w_gate_ffn"), bf("w_up_ffn")], axis=1), "w_dn": bf("w_down_ffn"),
        "conv_w": jnp.pad(w["conv_w"], ((0, 0), (0, CONV_ROWS - 3), (0, 0)))})
    layers = []
    for l in range(DEPTH):
        wl = {n: full[n][l] for n in full}
        wl["conv_w"] = wl["conv_w"][:3]
        wl["w_gu"] = wl["w_gu"].reshape(2 * N_CHIPS, D_MODEL, FFN_SH)
        wl.update({n: w[n][l] for n in _REPLICATED if n != "final_g"})
        layers.append(wl)
    loss, dx, grads, d_final = _local_step(x, mem, target, layers, w["final_g"])
    loss = lax.psum(loss, ("x", "y", "c"))

    for g in grads:
        g["w_gu"] = g["w_gu"].reshape(_FULL_SHAPE["w_gu"])
    g0, g1 = ({n: g[n] for n in _BIG} for g in grads)
    sib = _sibling_presum_exchange(g0, g1)
    part = {n: _add2(g0[n], g1[n], sib[n], ci, name="presum_" + n) for n in _BIG}
    slots = _shard_exchange(part)
    mine = {n: _sum_chips(n, slots[n], part[n], 2 * xi + yi) for n in _BIG}
    theirs = _sibling_exchange(mine)

    out = {}

    def adam_layers(name, group, pick=None):
        go, gr = (mine[group], theirs[group]) if pick is None else (mine[group][pick], theirs[group][pick])
        out[name] = _adam_layers(go, gr, ci, w[name], a["m_" + name], a["v_" + name], name="adam_" + name)

    adam_layers("w_in", "w_in")
    adam_layers("w_branch", "w_br")
    for t, n in enumerate(_SQUARE):
        adam_layers(n, "w_sq", t)
    adam_layers("w_gate_ffn", "w_gu", 0)
    adam_layers("w_up_ffn", "w_gu", 1)
    adam_layers("w_down_ffn", "w_dn")

    def adam(name, g):
        out[name] = _reduce_adam([g], w[name], a["m_" + name], a["v_" + name], name="adam_" + name)

    small = {n: jnp.stack([g[n] for g in grads]) for n in _REPLICATED if n != "final_g"}
    small["final_g"] = d_final
    conv_g = jnp.stack([g["conv_w"] for g in grads])
    n_rep = sum(w[n].size for n in _REPLICATED) // 128
    summed = _sum_slots(_gather_small(_pack([small[n] for n in _REPLICATED] + [conv_g])), name="sum_devices_small")
    res = _reduce_adam([summed[:n_rep]], _pack([w[n] for n in _REPLICATED]), _pack([a["m_" + n] for n in _REPLICATED]),
                       _pack([a["v_" + n] for n in _REPLICATED]), name="adam_replicated")
    off = 0
    for n in _REPLICATED:
        out[n] = tuple(r.reshape(-1)[off:off + w[n].size].reshape(w[n].shape) for r in res)
        off += w[n].size
    conv_full = summed[n_rep:].reshape(conv_g.shape)
    adam("conv_w", lax.dynamic_slice_in_dim(conv_full, (2 * xi + yi) * 128, 128, axis=2))

    return (loss, dx[None], *[out[n][k] for k in range(4) for n in _WEIGHTS])


def kernel(x, mem, norm_mix_g, w_in, sgu_ln_g, sgu_ln_b, w_spatial, b_spatial, conv_w, w_branch, w_out, norm_xa_g, mem_norm_g, w_q_xa, w_k_xa, w_v_xa, w_o_xa, norm_ffn_g, w_gate_ffn, w_up_ffn, w_down_ffn, final_g, loss_target, m_norm_mix_g, m_w_in, m_sgu_ln_g, m_sgu_ln_b, m_w_spatial, m_b_spatial, m_conv_w, m_w_branch, m_w_out, m_norm_xa_g, m_mem_norm_g, m_w_q_xa, m_w_k_xa, m_w_v_xa, m_w_o_xa, m_norm_ffn_g, m_w_gate_ffn, m_w_up_ffn, m_w_down_ffn, m_final_g, v_norm_mix_g, v_w_in, v_sgu_ln_g, v_sgu_ln_b, v_w_spatial, v_b_spatial, v_conv_w, v_w_branch, v_w_out, v_norm_xa_g, v_mem_norm_g, v_w_q_xa, v_w_k_xa, v_w_v_xa, v_w_o_xa, v_norm_ffn_g, v_w_gate_ffn, v_w_up_ffn, v_w_down_ffn, v_final_g):
    return _step(dict(locals()))
```

```python
import functools
import math

import jax
import jax.numpy as jnp
from jax import lax
from jax.experimental import pallas as pl
from jax.experimental.pallas import tpu as pltpu

F32, BF16 = jnp.float32, jnp.bfloat16
MESH = pl.DeviceIdType.MESH
ANY = pl.BlockSpec(memory_space=pl.ANY)

D_MODEL = 1024
DEPTH = 2
BW = 512
SB_HEADS, SB_DH = 8, 64
SGU_LEN, SGU_GROUPS, SGU_GD, SGU_CHUNK = 128, 4, 128, 64
XA_HEADS, XA_DH = 4, 256
FFN_SH = 704
N_CHIPS = 4
IN_COLS = 7168
C_Z, C_CB, C_GATES = 1536, 2560, 4096

ADAM_LR, ADAM_B1, ADAM_B2, ADAM_EPS, ADAM_WD, ADAM_STEP = 0.001, 0.9, 0.999, 1e-08, 0.01, 10

VMEM_LIMIT_V7X = 56 * 1024 * 1024

NN = (((1,), (0,)), ((), ()))
NT = (((1,), (1,)), ((), ()))
TN = (((0,), (0,)), ((), ()))


def _cp(*sem):
    return pltpu.CompilerParams(dimension_semantics=sem, vmem_limit_bytes=VMEM_LIMIT_V7X)


def _tile(n, pref):
    for t in pref:
        if n % t == 0:
            return t
    return n


def _rows(r, row_bytes, block_bytes=1 << 20):
    fits = [t for t in range(8, r + 1, 8) if r % t == 0 and t * row_bytes <= block_bytes]
    return max(fits) if fits else r


class _Rider:
    def __init__(self, ins, outs, n_sems, start, finish, alias=None):
        self.ins, self.outs, self.n_sems, self.start, self.finish, self.alias = list(ins), list(outs), n_sems, start, finish, alias or {}


def _call_with_rider(rider, body, *, name, grid, in_specs, args, out_specs, out_shape, scratch_shapes, semantics):
    if rider is None:
        return pl.pallas_call(body, name=name, grid=grid, in_specs=in_specs, out_specs=out_specs, out_shape=out_shape,
                              scratch_shapes=scratch_shapes, compiler_params=_cp(*semantics))(*args)
    n_in, n_out, r_in, r_out = len(args), len(out_shape), len(rider.ins), len(rider.outs)

    def riding(*refs):
        ins, rins = refs[:n_in], refs[n_in:n_in + r_in]
        outs, routs = refs[n_in + r_in:n_in + r_in + n_out], refs[n_in + r_in + n_out:n_in + r_in + n_out + r_out]
        rest = refs[n_in + r_in + n_out + r_out:]
        scratch, send, recv = rest[:-2], rest[-2], rest[-1]
        ids = [pl.program_id(ax) for ax in range(len(grid))]
        first, last = ids[0] == 0, ids[0] == grid[0] - 1
        for ax in range(1, len(grid)):
            first, last = first & (ids[ax] == 0), last & (ids[ax] == grid[ax] - 1)

        @pl.when(first)
        def _():
            rider.start(rins, routs, send, recv)

        body(*ins, *outs, *scratch)

        @pl.when(last)
        def _():
            rider.finish(rins, routs, send, recv)

    return pl.pallas_call(
        riding, name=name, grid=grid, in_specs=list(in_specs) + [ANY] * r_in, out_specs=list(out_specs) + [ANY] * r_out,
        out_shape=list(out_shape) + rider.outs, scratch_shapes=list(scratch_shapes) + [_dma_sems(rider.n_sems), _dma_sems(rider.n_sems)],
        input_output_aliases={n_in + i: n_out + o for o, i in rider.alias.items()},
        compiler_params=_cp(*["arbitrary"] * len(grid)),
    )(*args, *rider.ins)


def _run_rider(rider, *, name):
    def nothing(*refs):
        pass

    return _call_with_rider(rider, nothing, name=name, grid=(1,), in_specs=[], args=[], out_specs=[], out_shape=[], scratch_shapes=[],
                            semantics=("arbitrary",))


def _mm(a, b, *, mode, name, out_dtype=F32, res=None, a_kind="2d", b_kind="2d", tm=None, tn=None, tk=None, rider=None):
    a2, b2 = a.shape[-2:], b.shape[-2:]
    if mode == "nn":
        (M, K), N = a2, b2[1]
    elif mode == "nt":
        (M, K), N = a2, b2[0]
    else:
        (K, M), N = a2, b2[1]
    kchunk = a_kind == "kchunk" or b_kind == "kchunk"
    batch = a_kind == "batch" or b_kind == "batch"
    G = (a.shape[0] if a_kind == "batch" else b.shape[0]) if batch else 1
    tm = tm or _tile(M, (1024, 512, 256, 128))
    tn = tn or _tile(N, (1024, 512, 256, 128))
    if kchunk:
        tk, nk = K, (a.shape[0] if a_kind == "kchunk" else b.shape[0])
    else:
        tk = tk or _tile(K, (1024, 512, 256, 128))
        nk = K // tk

    def spec(kind, blk, idx):
        if kind == "2d":
            return pl.BlockSpec(blk, lambda g, i, j, k: idx(g, i, j, k))
        if kind == "batch":
            return pl.BlockSpec((None,) + blk, lambda g, i, j, k: (g,) + idx(g, i, j, k))
        return pl.BlockSpec((None,) + blk, lambda g, i, j, k: (k,) + idx(g, i, j, 0))

    if mode == "nn":
        a_spec = spec(a_kind, (tm, tk), lambda g, i, j, k: (i, k))
        b_spec = spec(b_kind, (tk, tn), lambda g, i, j, k: (k, j))
    elif mode == "nt":
        a_spec = spec(a_kind, (tm, tk), lambda g, i, j, k: (i, k))
        b_spec = spec(b_kind, (tn, tk), lambda g, i, j, k: (j, k))
    else:
        a_spec = spec(a_kind, (tk, tm), lambda g, i, j, k: (k, i))
        b_spec = spec(b_kind, (tk, tn), lambda g, i, j, k: (k, j))
    o_kind = "batch" if batch else "2d"
    o_spec = spec(o_kind, (tm, tn), lambda g, i, j, k: (i, j))
    o_shape = ((G,) if batch else ()) + (M, N)
    dn = {"nn": NN, "nt": NT, "tn": TN}[mode]
    has_res = res is not None

    def body(*refs):
        if has_res:
            a_ref, b_ref, r_ref, o_ref = refs[:4]
        else:
            a_ref, b_ref, o_ref = refs[:3]
        p = lax.dot_general(a_ref[...].astype(BF16), b_ref[...].astype(BF16), dn, preferred_element_type=F32)

        def finish(r):
            if has_res:
                r = r + r_ref[...]
            o_ref[...] = r.astype(out_dtype)

        if nk == 1:
            finish(p)
        else:
            acc = refs[-1]
            k = pl.program_id(3)

            @pl.when(k == 0)
            def _():
                acc[...] = p

            @pl.when(k > 0)
            def _():
                acc[...] += p

            @pl.when(k == nk - 1)
            def _():
                finish(acc[...])

    in_specs, args = [a_spec, b_spec], [a, b]
    if has_res:
        in_specs.append(spec("2d", (tm, tn), lambda g, i, j, k: (i, j)))
        args.append(res)
    outs = _call_with_rider(
        rider, body, name=name, grid=(G, M // tm, N // tn, nk), in_specs=in_specs, args=args, out_specs=[o_spec],
        out_shape=[jax.ShapeDtypeStruct(o_shape, out_dtype)], scratch_shapes=[pltpu.VMEM((tm, tn), F32)] if nk > 1 else [],
        semantics=("parallel", "parallel", "parallel", "arbitrary"))
    return outs[0] if rider is None else (outs[0], outs[1:])


def _rms_fwd(x, g, *, name):
    S, Dm = x.shape
    tm = _tile(S, (512, 256))

    def body(x_ref, g_ref, o_ref):
        xv = x_ref[...]
        r = lax.rsqrt(jnp.mean(xv * xv, axis=-1, keepdims=True) + 1e-6)
        o_ref[...] = (xv * r * g_ref[...]).astype(BF16)

    return pl.pallas_call(
        body, name=name, grid=(S // tm,),
        in_specs=[pl.BlockSpec((tm, Dm), lambda i: (i, 0)), pl.BlockSpec((1, Dm), lambda i: (0, 0))],
        out_specs=pl.BlockSpec((tm, Dm), lambda i: (i, 0)), out_shape=jax.ShapeDtypeStruct((S, Dm), BF16),
        compiler_params=_cp("parallel"),
    )(x, g.reshape(1, Dm))


def _rms_bwd(x, g, dh, dres, *, name):
    S, Dm = x.shape
    tm = _tile(S, (512, 256))

    def body(x_ref, g_ref, dh_ref, dr_ref, dx_ref, dg_ref):
        xv, dhv = x_ref[...], dh_ref[...].astype(F32)
        r = lax.rsqrt(jnp.mean(xv * xv, axis=-1, keepdims=True) + 1e-6)
        u = dhv * g_ref[...]
        s = jnp.sum(u * xv, axis=-1, keepdims=True)
        dx_ref[...] = dr_ref[...] + r * u - xv * ((r * r * r) * (s * (1.0 / Dm)))
        part = jnp.sum(dhv * (xv * r), axis=0, keepdims=True)

        @pl.when(pl.program_id(0) == 0)
        def _():
            dg_ref[...] = part

        @pl.when(pl.program_id(0) > 0)
        def _():
            dg_ref[...] += part

    row = pl.BlockSpec((tm, Dm), lambda i: (i, 0))
    vec = pl.BlockSpec((1, Dm), lambda i: (0, 0))
    dx, dg = pl.pallas_call(
        body, name=name, grid=(S // tm,), in_specs=[row, vec, row, row], out_specs=[row, vec],
        out_shape=[jax.ShapeDtypeStruct((S, Dm), F32), jax.ShapeDtypeStruct((1, Dm), F32)],
        compiler_params=_cp("arbitrary"),
    )(x, g.reshape(1, Dm), dh, dres)
    return dx, dg.reshape(Dm)


def _loss_head(x, g, target):
    S, Dm = x.shape
    tm = _tile(S, (512, 256))

    def body(x_ref, g_ref, t_ref, dx_ref, dg_ref, loss_ref):
        xv, gv = x_ref[...], g_ref[...]
        r = lax.rsqrt(jnp.mean(xv * xv, axis=-1, keepdims=True) + 1e-6)
        xn = xv * r
        err = xn * gv - t_ref[...]
        lpart = 0.5 * jnp.sum(jnp.mean(err * err, axis=-1, keepdims=True), axis=0, keepdims=True)
        dy = err * (1.0 / Dm)
        u = dy * gv
        s = jnp.sum(u * xv, axis=-1, keepdims=True)
        dx_ref[...] = r * u - xv * ((r * r * r) * (s * (1.0 / Dm)))
        part = jnp.sum(dy * xn, axis=0, keepdims=True)
        lslab = jnp.broadcast_to(lpart, (8, 128))

        @pl.when(pl.program_id(0) == 0)
        def _():
            dg_ref[...] = part
            loss_ref[...] = lslab

        @pl.when(pl.program_id(0) > 0)
        def _():
            dg_ref[...] += part
            loss_ref[...] += lslab

    row = pl.BlockSpec((tm, Dm), lambda i: (i, 0))
    vec = pl.BlockSpec((1, Dm), lambda i: (0, 0))
    dx, dg, loss = pl.pallas_call(
        body, name="loss_head", grid=(S // tm,), in_specs=[row, vec, row],
        out_specs=[row, vec, pl.BlockSpec((8, 128), lambda i: (0, 0))],
        out_shape=[jax.ShapeDtypeStruct((S, Dm), F32), jax.ShapeDtypeStruct((1, Dm), F32), jax.ShapeDtypeStruct((8, 128), F32)],
        compiler_params=_cp("arbitrary"),
    )(x, g.reshape(1, Dm), target)
    return loss[0, 0], dx, dg.reshape(Dm)


SB_TQ, SB_TK = 256, 128
SB_EXP_FLOOR = -104.0


def _split2(v):
    hi = v.astype(BF16)
    return jnp.concatenate([hi, (v - hi.astype(F32)).astype(BF16)], axis=1)


def _tri2(cmp):
    j = lax.broadcasted_iota(jnp.int32, (2 * SB_TK, SB_TK), 0) % SB_TK
    s = lax.broadcasted_iota(jnp.int32, (2 * SB_TK, SB_TK), 1)
    return cmp(j, s).astype(BF16)


def _sb_scores(qv, kb, k0, q0, tq):
    z = lax.dot_general(qv, kb, NT, preferred_element_type=F32) * (SB_DH ** -0.5)
    t_pos = q0 + lax.broadcasted_iota(jnp.int32, (tq, SB_TK), 0)
    s_pos = k0 + lax.broadcasted_iota(jnp.int32, (tq, SB_TK), 1)
    valid = s_pos < t_pos
    ls = jnp.minimum(z, 0.0) - jnp.log(1.0 + jnp.exp(-jnp.abs(z)))
    l1m = jnp.where(valid, ls - z, 0.0)
    return z, valid, ls, l1m


SB_HG = 2


def _sb_fwd(q, k, v, rider=None):
    H, S, dh = q.shape
    tq = min(SB_TQ, S)
    kb_per_q = tq // SB_TK
    hg = SB_HG

    def body(q_ref, k_ref, v_ref, o_ref, tot_ref, cnt_ref, acc, c):
        i = pl.program_id(1)
        q0 = i * tq
        later = _tri2(lambda j, s: j > s)
        acc[...] = jnp.zeros_like(acc)
        c[...] = jnp.zeros_like(c)
        nkb = (i + 1) * kb_per_q

        def more(st):
            n, highest = st
            return (n < nkb) & (highest > SB_EXP_FLOOR)

        def step(st):
            n, _ = st
            k0 = pl.multiple_of((nkb - 1 - n) * SB_TK, SB_TK)
            c_old, acc_old = [c[h] for h in range(hg)], [acc[h] for h in range(hg)]
            c_new, acc_new, highest = [], [], None
            for h in range(hg):
                kb, vb = k_ref[h, pl.ds(k0, SB_TK), :], v_ref[h, pl.ds(k0, SB_TK), :]
                z, valid, ls, l1m = _sb_scores(q_ref[h], kb, k0, q0, tq)
                c_new.append(c_old[h] + jnp.sum(l1m, axis=1, keepdims=True))
                top = jnp.max(c_new[h])
                highest = top if highest is None else jnp.maximum(highest, top)
                after = jnp.dot(_split2(l1m), later, preferred_element_type=F32)
                a = jnp.where(valid, jnp.exp(ls + after + c_old[h]), 0.0)
                acc_new.append(acc_old[h] + jnp.dot(a.astype(BF16), vb, preferred_element_type=F32))
            for h in range(hg):
                acc[h] = acc_new[h]
                c[h] = c_new[h]
            return n + 1, highest

        n_done, _ = lax.while_loop(more, step, (jnp.int32(0), jnp.float32(0.0)))
        o_ref[...] = acc[...].astype(o_ref.dtype)
        tot_ref[...] = c[...]
        cnt_ref[...] = jnp.full(cnt_ref.shape, n_done.astype(F32))

    qs = pl.BlockSpec((hg, tq, dh), lambda g, i: (g, i, 0))
    full = pl.BlockSpec((hg, S, dh), lambda g, i: (g, 0, 0))
    outs = _call_with_rider(
        rider, body, name="sb_fwd", grid=(H // hg, S // tq), in_specs=[qs, full, full], args=[q, k, v],
        out_specs=[qs, pl.BlockSpec((hg, tq, 1), lambda g, i: (g, i, 0)), pl.BlockSpec((None, None, 8, 128), lambda g, i: (g, i, 0, 0))],
        out_shape=[jax.ShapeDtypeStruct((H, S, dh), BF16), jax.ShapeDtypeStruct((H, S, 1), F32),
                   jax.ShapeDtypeStruct((H // hg, S // tq, 8, 128), F32)],
        scratch_shapes=[pltpu.VMEM((hg, tq, dh), F32), pltpu.VMEM((hg, tq, 1), F32)], semantics=("parallel", "parallel"))
    return outs[:3], outs[3:]


def _sb_bwd(q, k, v, do, tot, cnt, rider=None):
    H, S, dh = q.shape
    tq = min(SB_TQ, S)
    kb_per_q = tq // SB_TK
    scale = SB_DH ** -0.5
    hg = SB_HG

    def body(q_ref, k_ref, v_ref, do_ref, tot_ref, cnt_ref, dq_ref, dk_ref, dv_ref, dq_acc, pre, gpre):
        i = pl.program_id(1)
        q0 = i * tq
        upto = _tri2(lambda j, s: j <= s)
        before = _tri2(lambda j, s: j < s)

        @pl.when(i == 0)
        def _():
            dk_ref[...] = jnp.zeros_like(dk_ref)
            dv_ref[...] = jnp.zeros_like(dv_ref)

        dq_acc[...] = jnp.zeros_like(dq_acc)
        pre[...] = jnp.zeros_like(pre)
        gpre[...] = jnp.zeros_like(gpre)

        n_done = jnp.max(cnt_ref[...]).astype(jnp.int32)
        first = (i + 1) * kb_per_q - n_done

        def step(n, carry):
            k0 = pl.multiple_of((first + n) * SB_TK, SB_TK)
            heads = range(hg)
            old = [(dq_acc[h], dk_ref[h, pl.ds(k0, SB_TK), :], dv_ref[h, pl.ds(k0, SB_TK), :], pre[h], gpre[h]) for h in heads]
            new = []
            for h in heads:
                dq_o, dk_o, dv_o, pre_o, gpre_o = old[h]
                kb, vb = k_ref[h, pl.ds(k0, SB_TK), :], v_ref[h, pl.ds(k0, SB_TK), :]
                qv, dov = q_ref[h], do_ref[h]
                z, valid, ls, l1m = _sb_scores(qv, kb, k0, q0, tq)
                incl = jnp.dot(_split2(l1m), upto, preferred_element_type=F32)
                rest = tot_ref[h] - (pre_o + incl)
                a = jnp.where(valid, jnp.exp(ls + rest), 0.0)
                da = lax.dot_general(dov, vb, NT, preferred_element_type=F32)
                g = a * da
                gbefore = jnp.dot(_split2(g), before, preferred_element_type=F32) + gpre_o
                dz = jnp.where(valid, g * jnp.exp(ls - z) - jnp.exp(ls) * gbefore, 0.0) * scale
                dzb = dz.astype(BF16)
                new.append((dq_o + jnp.dot(dzb, kb, preferred_element_type=F32),
                            dk_o + lax.dot_general(dzb, qv, TN, preferred_element_type=F32),
                            dv_o + lax.dot_general(a.astype(BF16), dov, TN, preferred_element_type=F32),
                            pre_o + jnp.sum(l1m, axis=1, keepdims=True), gpre_o + jnp.sum(g, axis=1, keepdims=True)))
            for h in heads:
                dq_acc[h], dk_ref[h, pl.ds(k0, SB_TK), :], dv_ref[h, pl.ds(k0, SB_TK), :], pre[h], gpre[h] = new[h]
            return carry

        lax.fori_loop(0, n_done, step, 0)
        dq_ref[...] = dq_acc[...].astype(dq_ref.dtype)

    qs = pl.BlockSpec((hg, tq, dh), lambda g, i: (g, i, 0))
    full = pl.BlockSpec((hg, S, dh), lambda g, i: (g, 0, 0))
    outs = _call_with_rider(
        rider, body, name="sb_bwd", grid=(H // hg, S // tq), args=[q, k, v, do, tot, cnt],
        in_specs=[qs, full, full, qs, pl.BlockSpec((hg, tq, 1), lambda g, i: (g, i, 0)),
                  pl.BlockSpec((None, None, 8, 128), lambda g, i: (g, i, 0, 0))],
        out_specs=[qs, full, full],
        out_shape=[jax.ShapeDtypeStruct((H, S, dh), BF16), jax.ShapeDtypeStruct((H, S, dh), F32), jax.ShapeDtypeStruct((H, S, dh), F32)],
        scratch_shapes=[pltpu.VMEM((hg, tq, dh), F32), pltpu.VMEM((hg, tq, 1), F32), pltpu.VMEM((hg, tq, 1), F32)],
        semantics=("parallel", "arbitrary"))
    return outs[:3], outs[3:]


_INV_SQRT2 = 0.7071067811865476
_INV_SQRT2PI = 0.3989422804014327


def _gelu(x):
    return 0.5 * x * (1.0 + lax.erf(x * _INV_SQRT2))


def _gelu_grad(x):
    return 0.5 * (1.0 + lax.erf(x * _INV_SQRT2)) + x * (_INV_SQRT2PI * jnp.exp(-0.5 * x * x))


def _sgu_mask():
    t = lax.broadcasted_iota(jnp.int32, (SGU_LEN, SGU_LEN), 0) // SGU_CHUNK
    s = lax.broadcasted_iota(jnp.int32, (SGU_LEN, SGU_LEN), 1) // SGU_CHUNK
    return t >= s


def _sgu_mask_t():
    t = lax.broadcasted_iota(jnp.int32, (SGU_LEN, SGU_LEN), 0) // SGU_CHUNK
    s = lax.broadcasted_iota(jnp.int32, (SGU_LEN, SGU_LEN), 1) // SGU_CHUNK
    return s >= t


def _sgu_norm(zv, g, b):
    vv = _gelu(zv)
    xc = vv - jnp.mean(vv, axis=-1, keepdims=True)
    rstd = lax.rsqrt(jnp.mean(xc * xc, axis=-1, keepdims=True) + 1e-5)
    xhat = xc * rstd
    return xhat, rstd, xhat * g + b


SGU_TM = 256


def _sgu_fwd(p, ln_g, ln_b, w_s, b_st):
    S = p.shape[0]
    tm = min(SGU_TM, S)

    def body(zu_ref, zv_ref, g_ref, b_ref, w_ref, bs_ref, o_ref):
        u = _gelu(zu_ref[...])
        _, _, vn = _sgu_norm(zv_ref[...], g_ref[...], b_ref[...])
        vnb = vn.astype(BF16)
        mask = _sgu_mask()
        for gi in range(SGU_GROUPS):
            wg = jnp.where(mask, w_ref[gi], 0.0).astype(BF16)
            cols = slice(gi * SGU_GD, (gi + 1) * SGU_GD)
            for ci in range(tm // SGU_LEN):
                rows = slice(ci * SGU_LEN, (ci + 1) * SGU_LEN)
                vm = jnp.dot(wg, vnb[rows, cols], preferred_element_type=F32) + bs_ref[:, gi:gi + 1]
                o_ref[rows, cols] = (u[rows, cols] * vm).astype(BF16)

    vec = pl.BlockSpec((1, BW), lambda i: (0, 0))
    return pl.pallas_call(
        body, name="sgu_fwd", grid=(S // tm,),
        in_specs=[pl.BlockSpec((tm, BW), lambda i: (i, C_Z // BW)), pl.BlockSpec((tm, BW), lambda i: (i, C_Z // BW + 1)), vec, vec,
                  pl.BlockSpec((SGU_GROUPS, SGU_LEN, SGU_LEN), lambda i: (0, 0, 0)), pl.BlockSpec((SGU_LEN, SGU_GROUPS), lambda i: (0, 0))],
        out_specs=pl.BlockSpec((tm, BW), lambda i: (i, 0)), out_shape=jax.ShapeDtypeStruct((S, BW), BF16),
        compiler_params=_cp("parallel"),
    )(p, p, ln_g.reshape(1, BW), ln_b.reshape(1, BW), w_s, b_st)


def _sgu_bwd(p, dyb, ln_g, ln_b, w_s, w_st, b_st):
    S = p.shape[0]
    tm = min(SGU_TM, S)

    def body(zu_ref, zv_ref, dy_ref, g_ref, b_ref, w_ref, wt_ref, bs_ref, dz_ref, dg_ref, db_ref, dw_ref, dbs_ref, dvn):
        first = pl.program_id(0) == 0

        @pl.when(first)
        def _():
            dg_ref[...] = jnp.zeros_like(dg_ref)
            db_ref[...] = jnp.zeros_like(db_ref)
            dw_ref[...] = jnp.zeros_like(dw_ref)
            dbs_ref[...] = jnp.zeros_like(dbs_ref)

        zu, zv, dy = zu_ref[...], zv_ref[...], dy_ref[...].astype(F32)
        u = _gelu(zu)
        xhat, rstd, vn = _sgu_norm(zv, g_ref[...], b_ref[...])
        vnb = vn.astype(BF16)
        mask = _sgu_mask()
        mask_t = _sgu_mask_t()
        for gi in range(SGU_GROUPS):
            wg = jnp.where(mask, w_ref[gi], 0.0).astype(BF16)
            wgt = jnp.where(mask_t, wt_ref[gi], 0.0).astype(BF16)
            cols = slice(gi * SGU_GD, (gi + 1) * SGU_GD)
            for ci in range(tm // SGU_LEN):
                rows = slice(ci * SGU_LEN, (ci + 1) * SGU_LEN)
                vm = jnp.dot(wg, vnb[rows, cols], preferred_element_type=F32) + bs_ref[:, gi:gi + 1]
                dyc = dy[rows, cols]
                dz_ref[rows, cols] = (dyc * vm * _gelu_grad(zu[rows, cols])).astype(BF16)
                dvm = dyc * u[rows, cols]
                dvmb = dvm.astype(BF16)
                dbs_ref[gi] += jnp.broadcast_to(jnp.sum(dvm, axis=1, keepdims=True), (SGU_LEN, SGU_GD))
                dw_ref[gi] += lax.dot_general(dvmb, vnb[rows, cols], NT, preferred_element_type=F32)
                dvn[rows, cols] = jnp.dot(wgt, dvmb, preferred_element_type=F32)
        dvnv = dvn[...]
        dg_ref[...] += jnp.sum(dvnv * xhat, axis=0, keepdims=True)
        db_ref[...] += jnp.sum(dvnv, axis=0, keepdims=True)
        dxh = dvnv * g_ref[...]
        dvv = rstd * (dxh - jnp.mean(dxh, axis=-1, keepdims=True) - xhat * jnp.mean(dxh * xhat, axis=-1, keepdims=True))
        dz_ref[:, BW:] = (dvv * _gelu_grad(zv)).astype(BF16)

        @pl.when(pl.program_id(0) == n_steps - 1)
        def _():
            for gi in range(SGU_GROUPS):
                dw_ref[gi] = jnp.where(mask, dw_ref[gi], 0.0)

    n_steps = S // tm
    vec = pl.BlockSpec((1, BW), lambda i: (0, 0))
    half = lambda c: pl.BlockSpec((tm, BW), lambda i: (i, c))
    wspec = pl.BlockSpec((SGU_GROUPS, SGU_LEN, SGU_LEN), lambda i: (0, 0, 0))
    dz, dg, db, dw, dbs = pl.pallas_call(
        body, name="sgu_bwd", grid=(n_steps,),
        in_specs=[half(C_Z // BW), half(C_Z // BW + 1), half(0), vec, vec, wspec, wspec,
                  pl.BlockSpec((SGU_LEN, SGU_GROUPS), lambda i: (0, 0))],
        out_specs=[pl.BlockSpec((tm, 2 * BW), lambda i: (i, 0)), vec, vec, wspec, wspec],
        out_shape=[jax.ShapeDtypeStruct((S, 2 * BW), BF16), jax.ShapeDtypeStruct((1, BW), F32), jax.ShapeDtypeStruct((1, BW), F32),
                   jax.ShapeDtypeStruct((SGU_GROUPS, SGU_LEN, SGU_LEN), F32), jax.ShapeDtypeStruct((SGU_GROUPS, SGU_LEN, SGU_GD), F32)],
        scratch_shapes=[pltpu.VMEM((tm, BW), F32)],
        compiler_params=_cp("arbitrary"),
    )(p, p, dyb, ln_g.reshape(1, BW), ln_b.reshape(1, BW), w_s, w_st, b_st)
    return dz, dg.reshape(BW), db.reshape(BW), dw, dbs[:, :, 0]


CONV_TC = 128


def _shift_down(y, n):
    rows = lax.broadcasted_iota(jnp.int32, y.shape, 0)
    return jnp.where(rows < n, 0.0, pltpu.roll(y, n, 0))


def _shift_up(y, n):
    rows = lax.broadcasted_iota(jnp.int32, y.shape, 0)
    return jnp.where(rows >= y.shape[0] - n, 0.0, pltpu.roll(y, y.shape[0] - n, 0))


def _conv_specs(S):
    col = lambda c0: pl.BlockSpec((S, CONV_TC), lambda j: (0, c0 // CONV_TC + j))
    return col(C_CB), col(C_CB + BW), col(C_CB + 2 * BW), pl.BlockSpec((3, CONV_TC), lambda j: (0, j)), pl.BlockSpec((S, CONV_TC), lambda j: (0, j))


def _conv_fwd(p, conv_w):
    S = p.shape[0]

    def body(cb_ref, cc_ref, cx_ref, w_ref, o_ref):
        y = cc_ref[...] * cx_ref[...]
        conv = w_ref[0:1, :] * _shift_down(y, 2) + w_ref[1:2, :] * _shift_down(y, 1) + w_ref[2:3, :] * y
        o_ref[...] = (cb_ref[...] * conv).astype(BF16)

    cb, cc, cx, wspec, out = _conv_specs(S)
    return pl.pallas_call(
        body, name="conv_fwd", grid=(BW // CONV_TC,), in_specs=[cb, cc, cx, wspec], out_specs=out,
        out_shape=jax.ShapeDtypeStruct((S, BW), BF16), compiler_params=_cp("parallel"),
    )(p, p, p, conv_w)


def _conv_bwd(p, conv_w, dyc):
    S = p.shape[0]

    def body(cb_ref, cc_ref, cx_ref, w_ref, dy_ref, db_ref, dc_ref, dx_ref, dw_ref):
        cc, cx, dy = cc_ref[...], cx_ref[...], dy_ref[...].astype(F32)
        y = cc * cx
        w0, w1, w2 = w_ref[0:1, :], w_ref[1:2, :], w_ref[2:3, :]
        y1, y2 = _shift_down(y, 1), _shift_down(y, 2)
        conv = w0 * y2 + w1 * y1 + w2 * y
        db_ref[...] = (dy * conv).astype(BF16)
        dconv = dy * cb_ref[...]
        dyy = w2 * dconv + w1 * _shift_up(dconv, 1) + w0 * _shift_up(dconv, 2)
        dc_ref[...] = (dyy * cx).astype(BF16)
        dx_ref[...] = (dyy * cc).astype(BF16)
        dw_ref[0:1, :] = jnp.sum(dconv * y2, axis=0, keepdims=True)
        dw_ref[1:2, :] = jnp.sum(dconv * y1, axis=0, keepdims=True)
        dw_ref[2:3, :] = jnp.sum(dconv * y, axis=0, keepdims=True)

    cb, cc, cx, wspec, out = _conv_specs(S)
    db, dc, dx, dw = pl.pallas_call(
        body, name="conv_bwd", grid=(BW // CONV_TC,), in_specs=[cb, cc, cx, wspec, out],
        out_specs=[out, out, out, wspec],
        out_shape=[jax.ShapeDtypeStruct((S, BW), BF16)] * 3 + [jax.ShapeDtypeStruct((3, BW), F32)],
        compiler_params=_cp("parallel"),
    )(p, p, p, conv_w, dyc)
    return db, dc, dx, dw


def _merge_specs(S, tm):
    gate = lambda n: pl.BlockSpec((tm, D_MODEL), lambda i: (i, C_GATES // D_MODEL + n))
    return [gate(0), gate(1), gate(2)], pl.BlockSpec((3, tm, D_MODEL), lambda i: (0, i, 0)), pl.BlockSpec((tm, D_MODEL), lambda i: (i, 0))


def _merge_fwd(p, bd):
    S = p.shape[0]
    tm = _tile(S, (256,))

    def body(g0, g1, g2, b_ref, o_ref):
        acc = jax.nn.sigmoid(g0[...]) * b_ref[0]
        acc = acc + jax.nn.sigmoid(g1[...]) * b_ref[1]
        acc = acc + jax.nn.sigmoid(g2[...]) * b_ref[2]
        o_ref[...] = acc.astype(BF16)

    gates, bspec, row = _merge_specs(S, tm)
    return pl.pallas_call(
        body, name="merge_fwd", grid=(S // tm,), in_specs=gates + [bspec], out_specs=row,
        out_shape=jax.ShapeDtypeStruct((S, D_MODEL), BF16), compiler_params=_cp("parallel"),
    )(p, p, p, bd)


def _merge_bwd(p, bd, dm):
    S = p.shape[0]
    tm = _tile(S, (256,))

    def body(g0, g1, g2, b_ref, dm_ref, db_ref, dg_ref):
        dmv = dm_ref[...]
        for n, g_ref in enumerate((g0, g1, g2)):
            sg = jax.nn.sigmoid(g_ref[...])
            db_ref[n] = (dmv * sg).astype(BF16)
            dg_ref[:, n * D_MODEL:(n + 1) * D_MODEL] = (dmv * b_ref[n] * (sg * (1.0 - sg))).astype(BF16)

    gates, bspec, row = _merge_specs(S, tm)
    return pl.pallas_call(
        body, name="merge_bwd", grid=(S // tm,), in_specs=gates + [bspec, row],
        out_specs=[bspec, pl.BlockSpec((tm, 3 * D_MODEL), lambda i: (i, 0))],
        out_shape=[jax.ShapeDtypeStruct((3, S, D_MODEL), BF16), jax.ShapeDtypeStruct((S, 3 * D_MODEL), BF16)],
        compiler_params=_cp("parallel"),
    )(p, p, p, bd, dm)


XA_TM = 512


def _xa_probs(qh, kh):
    s = lax.dot_general(qh, kh, NT, preferred_element_type=F32) * (XA_DH ** -0.5)
    e = jnp.exp(s - jnp.max(s, axis=-1, keepdims=True))
    return e / jnp.sum(e, axis=-1, keepdims=True)


def _xa_fwd(q, kv):
    S = q.shape[0]
    tm = min(XA_TM, S)
    M = kv.shape[1]

    def body(q_ref, kv_ref, o_ref):
        for h in range(XA_HEADS):
            cols = slice(h * XA_DH, (h + 1) * XA_DH)
            pr = _xa_probs(q_ref[:, cols], kv_ref[0, :, cols])
            o_ref[:, cols] = jnp.dot(pr.astype(BF16), kv_ref[1, :, cols], preferred_element_type=F32).astype(BF16)

    row = pl.BlockSpec((tm, D_MODEL), lambda i: (i, 0))
    return pl.pallas_call(
        body, name="xa_fwd", grid=(S // tm,), in_specs=[row, pl.BlockSpec((2, M, D_MODEL), lambda i: (0, 0, 0))], out_specs=row,
        out_shape=jax.ShapeDtypeStruct((S, D_MODEL), BF16), compiler_params=_cp("parallel"),
    )(q, kv)


def _xa_bwd(q, kv, do):
    S = q.shape[0]
    tm = min(XA_TM, S)
    M = kv.shape[1]

    def body(q_ref, kv_ref, do_ref, dq_ref, dkv_ref):
        @pl.when(pl.program_id(0) == 0)
        def _():
            dkv_ref[...] = jnp.zeros_like(dkv_ref)

        for h in range(XA_HEADS):
            cols = slice(h * XA_DH, (h + 1) * XA_DH)
            qh, kh, vh, doh = q_ref[:, cols], kv_ref[0, :, cols], kv_ref[1, :, cols], do_ref[:, cols]
            pr = _xa_probs(qh, kh)
            dkv_ref[1, :, cols] += lax.dot_general(pr.astype(BF16), doh, TN, preferred_element_type=F32)
            dp = lax.dot_general(doh, vh, NT, preferred_element_type=F32)
            ds = (pr * (dp - jnp.sum(dp * pr, axis=-1, keepdims=True)) * (XA_DH ** -0.5)).astype(BF16)
            dq_ref[:, cols] = jnp.dot(ds, kh, preferred_element_type=F32).astype(BF16)
            dkv_ref[0, :, cols] += lax.dot_general(ds, qh, TN, preferred_element_type=F32)

    row = pl.BlockSpec((tm, D_MODEL), lambda i: (i, 0))
    kvs = pl.BlockSpec((2, M, D_MODEL), lambda i: (0, 0, 0))
    return pl.pallas_call(
        body, name="xa_bwd", grid=(S // tm,), in_specs=[row, kvs, row], out_specs=[row, kvs],
        out_shape=[jax.ShapeDtypeStruct((S, D_MODEL), BF16), jax.ShapeDtypeStruct((2, M, D_MODEL), F32)],
        compiler_params=_cp("arbitrary"),
    )(q, kv, do)


def _swiglu_fwd(ab):
    nb, _, S, C = ab.shape
    tm = _tile(S, (512, 256))

    def body(ab_ref, o_ref):
        a = ab_ref[0]
        o_ref[...] = (a * jax.nn.sigmoid(a) * ab_ref[1]).astype(BF16)

    return pl.pallas_call(
        body, name="swiglu_fwd", grid=(nb, S // tm),
        in_specs=[pl.BlockSpec((None, 2, tm, C), lambda j, i: (j, 0, i, 0))], out_specs=pl.BlockSpec((None, tm, C), lambda j, i: (j, i, 0)),
        out_shape=jax.ShapeDtypeStruct((nb, S, C), BF16), compiler_params=_cp("parallel", "parallel"),
    )(ab)


def _swiglu_bwd(ab, dh):
    nb, _, S, C = ab.shape
    tm = _tile(S, (512, 256))

    def body(ab_ref, dh_ref, o_ref):
        a, b, d = ab_ref[0], ab_ref[1], dh_ref[...].astype(F32)
        sg = jax.nn.sigmoid(a)
        o_ref[0] = (d * b * (sg * (1.0 + a * (1.0 - sg)))).astype(BF16)
        o_ref[1] = (d * (a * sg)).astype(BF16)

    pair = pl.BlockSpec((None, 2, tm, C), lambda j, i: (j, 0, i, 0))
    return pl.pallas_call(
        body, name="swiglu_bwd", grid=(nb, S // tm), in_specs=[pair, pl.BlockSpec((None, tm, C), lambda j, i: (j, i, 0))], out_specs=pair,
        out_shape=jax.ShapeDtypeStruct(ab.shape, BF16), compiler_params=_cp("parallel", "parallel"),
    )(ab, dh)


def _reduce_adam(parts, w, m, v, *, name):
    shape = w.shape
    C = shape[-1]
    R = math.prod(shape[:-1])
    tm = _rows(R, 4 * C)
    n = len(parts)
    c1, c2 = 1.0 - ADAM_B1 ** ADAM_STEP, 1.0 - ADAM_B2 ** ADAM_STEP

    def body(*refs):
        g = refs[0][...]
        for r in refs[1:n]:
            g = g + r[...]
        w_ref, m_ref, v_ref, go, do, mo, vo = refs[n:]
        mn = ADAM_B1 * m_ref[...] + (1.0 - ADAM_B1) * g
        vn = ADAM_B2 * v_ref[...] + (1.0 - ADAM_B2) * (g * g)
        go[...] = g
        do[...] = -ADAM_LR * ((mn / c1) / (jnp.sqrt(vn / c2) + ADAM_EPS) + ADAM_WD * w_ref[...])
        mo[...] = mn
        vo[...] = vn

    row = pl.BlockSpec((tm, C), lambda i: (i, 0))
    outs = pl.pallas_call(
        body, name=name, grid=(R // tm,), in_specs=[row] * (n + 3), out_specs=[row] * 4,
        out_shape=[jax.ShapeDtypeStruct((R, C), F32)] * 4, compiler_params=_cp("parallel"),
    )(*[a.reshape(R, C) for a in (*parts, w, m, v)])
    return tuple(o.reshape(shape) for o in outs)


_VIEW = {
    "w_in": ((1024, 7168), (1024, 1792), 256, lambda i, b: (i, b)),
    "w_br": ((1536, 1024), (1536, 256), 512, lambda i, b: (i, b)),
    "w_sq": ((5120, 1024), (1280, 1024), 256, lambda i, b: (4 * i + b, 0)),
    "w_gu": ((8192, 704), (2048, 704), 512, lambda i, b: (4 * b + i, 0)),
    "w_dn": ((2816, 1024), (704, 1024), 352, lambda i, b: (2 * b + i, 0)),
    "conv_w": ((8, 512), (8, 128), 8, lambda i, b: (0, b)),
}


def _scalar(v):
    return jnp.asarray(v, jnp.int32).reshape(1)


def _place(name, local, b):
    full2, sh2, tm, idx = _VIEW[name]
    C = sh2[1]
    dt = local.dtype if name == "conv_w" else BF16

    def body(b_ref, x_ref, o0_ref, o1_ref):
        o0_ref[...] = x_ref[0].astype(dt)
        o1_ref[...] = x_ref[1].astype(dt)

    place = pl.BlockSpec((tm, C), lambda i, bs: idx(i, bs[0]))
    outs = pl.pallas_call(
        body, name="place_" + name,
        grid_spec=pltpu.PrefetchScalarGridSpec(num_scalar_prefetch=1, grid=(sh2[0] // tm,),
                                               in_specs=[pl.BlockSpec((DEPTH, tm, C), lambda i, bs: (0, i, 0))], out_specs=[place, place]),
        out_shape=[jax.ShapeDtypeStruct(full2, dt)] * 2, compiler_params=_cp("arbitrary"),
    )(_scalar(b), local.reshape((DEPTH,) + sh2))
    return [o.reshape(_FULL_SHAPE[name]) for o in outs]


def _add_owner(name, g, land, own):
    shape = g.shape
    C = shape[-1]
    R = math.prod(shape[:-1])
    tm = _rows(R, 4 * C)

    def body(s_ref, g_ref, l_ref, o_ref):
        @pl.when(s_ref[0] != 0)
        def _():
            o_ref[...] = (g_ref[...] + l_ref[...]).astype(BF16)

        @pl.when(s_ref[0] == 0)
        def _():
            o_ref[...] = jnp.zeros_like(o_ref)

    pick = pl.BlockSpec((tm, C), lambda i, s: (jnp.where(s[0] != 0, i, 0), 0))
    return pl.pallas_call(
        body, name="presum_" + name,
        grid_spec=pltpu.PrefetchScalarGridSpec(num_scalar_prefetch=1, grid=(R // tm,), in_specs=[pick, pick],
                                               out_specs=pl.BlockSpec((tm, C), lambda i, s: (i, 0))),
        out_shape=jax.ShapeDtypeStruct((R, C), BF16), compiler_params=_cp("arbitrary"),
    )(_scalar(own), g.reshape(R, C), land.reshape(R, C)).reshape(shape)


def _sum_chips(name, slots, part, b, own):
    full2, sh2, tm, idx = _VIEW[name]
    C = sh2[1]

    def body(s_ref, slot_ref, own_ref, o_ref):
        @pl.when(s_ref[1] != 0)
        def _():
            o_ref[...] = ((slot_ref[0].astype(F32) + slot_ref[1].astype(F32)) + slot_ref[2].astype(F32)) + own_ref[...].astype(F32)

        @pl.when(s_ref[1] == 0)
        def _():
            o_ref[...] = jnp.zeros_like(o_ref)

    return pl.pallas_call(
        body, name="sum_chips_" + name,
        grid_spec=pltpu.PrefetchScalarGridSpec(
            num_scalar_prefetch=1, grid=(sh2[0] // tm,),
            in_specs=[pl.BlockSpec((3, tm, C), lambda i, s: (0, jnp.where(s[1] != 0, i, 0), 0)),
                      pl.BlockSpec((tm, C), lambda i, s: idx(jnp.where(s[1] != 0, i, 0), s[0]))],
            out_specs=pl.BlockSpec((tm, C), lambda i, s: (i, 0))),
        out_shape=jax.ShapeDtypeStruct(sh2, F32), compiler_params=_cp("arbitrary"),
    )(jnp.stack([jnp.asarray(b, jnp.int32), jnp.asarray(own, jnp.int32)]), slots.reshape((3,) + sh2),
      part.reshape(full2)).reshape(_SHARD_SHAPE[name])


def _adam_layers(mine, theirs, c, w, m, v, *, name):
    shape = w.shape
    C = shape[-1]
    R = math.prod(shape[1:-1])
    tm = _rows(R, 4 * C)
    c1, c2 = 1.0 - ADAM_B1 ** ADAM_STEP, 1.0 - ADAM_B2 ** ADAM_STEP

    def body(c_ref, m0_ref, m1_ref, t_ref, w_ref, m_ref, v_ref, go, do, mo, vo):
        layer = pl.program_id(0)
        g = jnp.where(layer == c_ref[0], jnp.where(layer == 0, m0_ref[...], m1_ref[...]), t_ref[...])
        mn = ADAM_B1 * m_ref[...] + (1.0 - ADAM_B1) * g
        vn = ADAM_B2 * v_ref[...] + (1.0 - ADAM_B2) * (g * g)
        go[...] = g
        do[...] = -ADAM_LR * ((mn / c1) / (jnp.sqrt(vn / c2) + ADAM_EPS) + ADAM_WD * w_ref[...])
        mo[...] = mn
        vo[...] = vn

    def own(layer):
        return pl.BlockSpec((tm, C), lambda l, i, cs: (jnp.where((l == layer) & (cs[0] == layer), i, 0), 0))

    recv = pl.BlockSpec((tm, C), lambda l, i, cs: (jnp.where(l == cs[0], 0, i), 0))
    row = pl.BlockSpec((None, tm, C), lambda l, i, cs: (l, i, 0))
    outs = pl.pallas_call(
        body, name=name,
        grid_spec=pltpu.PrefetchScalarGridSpec(num_scalar_prefetch=1, grid=(DEPTH, R // tm),
                                               in_specs=[own(0), own(1), recv, row, row, row], out_specs=[row] * 4),
        out_shape=[jax.ShapeDtypeStruct((DEPTH, R, C), F32)] * 4, compiler_params=_cp("arbitrary", "arbitrary"),
    )(_scalar(c), mine[0].reshape(R, C), mine[1].reshape(R, C), theirs.reshape(R, C), *[t.reshape(DEPTH, R, C) for t in (w, m, v)])
    return tuple(o.reshape(shape) for o in outs)


def _sum_slots(r, *, name):
    n, shape = r.shape[0], r.shape[1:]
    C = shape[-1]
    R = math.prod(shape[:-1])
    tm = _rows(R, 4 * C * n, 8 << 20)

    def body(r_ref, o_ref):
        acc = r_ref[0]
        for s in range(1, n):
            acc = acc + r_ref[s]
        o_ref[...] = acc

    return pl.pallas_call(
        body, name=name, grid=(R // tm,), in_specs=[pl.BlockSpec((n, tm, C), lambda i: (0, i, 0))],
        out_specs=pl.BlockSpec((tm, C), lambda i: (i, 0)), out_shape=jax.ShapeDtypeStruct((R, C), F32), compiler_params=_cp("parallel"),
    )(r.reshape(n, R, C)).reshape(shape)


def _heads(t, S):
    n = t.shape[1] // BW
    return t.reshape(S, n * SB_HEADS, SB_DH).transpose(1, 0, 2)


def _unheads(t, S):
    return t.transpose(1, 0, 2).reshape(S, BW)


def _layer_fwd(x, mem, wl, sb_rider=None):
    S = x.shape[0]
    sv = {"x": x}
    h1 = _rms_fwd(x, wl["norm_mix_g"], name="rms_mix")
    p = _mm(h1, wl["w_in"], mode="nn", name="mm_in")
    qkv = _heads(p[:, :3 * BW].astype(BF16), S)
    q, k, v = qkv[:SB_HEADS], qkv[SB_HEADS:2 * SB_HEADS], qkv[2 * SB_HEADS:]
    (ya, tot, cnt), rode = _sb_fwd(q, k, v, sb_rider)
    b_st = wl["b_spatial"].T
    yb = _sgu_fwd(p, wl["sgu_ln_g"], wl["sgu_ln_b"], wl["w_spatial"], b_st)
    yc = _conv_fwd(p, wl["conv_w"])
    br = jnp.stack([_unheads(ya, S), yb, yc])
    bd = _mm(br, wl["w_br"], mode="nn", a_kind="batch", b_kind="batch", name="mm_branch")
    merged = _merge_fwd(p, bd)
    x1 = _mm(merged, wl["w_sq"][0], mode="nn", res=x, name="mm_out")
    h2 = _rms_fwd(x1, wl["norm_xa_g"], name="rms_xa")
    qx = _mm(h2, wl["w_sq"][1], mode="nn", out_dtype=BF16, name="mm_q")
    mn = _rms_fwd(mem, wl["mem_norm_g"], name="rms_mem")
    kv = _mm(mn, wl["w_sq"][3:5], mode="nn", b_kind="batch", out_dtype=BF16, name="mm_kv")
    o = _xa_fwd(qx, kv)
    x2 = _mm(o, wl["w_sq"][2], mode="nn", res=x1, name="mm_o")
    h3 = _rms_fwd(x2, wl["norm_ffn_g"], name="rms_ffn")
    ab = _mm(h3, wl["w_gu"], mode="nn", b_kind="batch", name="mm_gu").reshape(N_CHIPS, 2, S, FFN_SH)
    hh = _swiglu_fwd(ab)
    x3 = _mm(hh, wl["w_dn"], mode="nn", a_kind="kchunk", b_kind="kchunk", res=x2, name="mm_down")
    sv.update(h1=h1, p=p, q=q, k=k, v=v, tot=tot, cnt=cnt, br=br, bd=bd, merged=merged, x1=x1, h2=h2, qx=qx, mn=mn, kv=kv, o=o,
              x2=x2, h3=h3, ab=ab, hh=hh, b_st=b_st)
    return x3, sv, rode


def _layer_bwd(dx3, mem, wl, sv, presum_layer=None, sb_rider=None):
    S = dx3.shape[0]
    p = sv["p"]
    g, land = {}, {}

    def mm_carrying(names, *args, **kw):
        if presum_layer is None:
            return _mm(*args, **kw)
        out, arrived = _mm(*args, **kw, rider=_presum_rider(presum_layer, {n: g[n] for n in names}))
        land.update(zip(names, arrived))
        return out

    dhh = _mm(dx3, wl["w_dn"], mode="nt", b_kind="batch", name="mm_down_dx")
    g["w_dn"] = _mm(sv["hh"], dx3, mode="tn", a_kind="batch", name="mm_down_dw")
    dab = _swiglu_bwd(sv["ab"], dhh).reshape(2 * N_CHIPS, S, FFN_SH)
    g["w_gu"] = _mm(sv["h3"], dab, mode="tn", b_kind="batch", name="mm_gu_dw").reshape(_FULL_SHAPE["w_gu"])
    dh3 = mm_carrying(["w_dn", "w_gu"], dab, wl["w_gu"], mode="nt", a_kind="kchunk", b_kind="kchunk", name="mm_gu_dx")
    dx2, g["norm_ffn_g"] = _rms_bwd(sv["x2"], wl["norm_ffn_g"], dh3, dx3, name="rms_ffn_bwd")
    do = _mm(dx2, wl["w_sq"][2], mode="nt", out_dtype=BF16, name="mm_o_dx")
    dw_o = _mm(sv["o"], dx2, mode="tn", name="mm_o_dw")
    dq, dkv = _xa_bwd(sv["qx"], sv["kv"], do)
    dw_q = _mm(sv["h2"], dq, mode="tn", name="mm_q_dw")
    dh2 = _mm(dq, wl["w_sq"][1], mode="nt", name="mm_q_dx")
    dw_kv = _mm(sv["mn"], dkv, mode="tn", b_kind="batch", name="mm_kv_dw")
    dmn = _mm(dkv, wl["w_sq"][3:5], mode="nt", a_kind="kchunk", b_kind="kchunk", name="mm_kv_dx")
    _, g["mem_norm_g"] = _rms_bwd(mem, wl["mem_norm_g"], dmn, jnp.zeros_like(mem), name="rms_mem_bwd")
    dx1, g["norm_xa_g"] = _rms_bwd(sv["x1"], wl["norm_xa_g"], dh2, dx2, name="rms_xa_bwd")
    dm = _mm(dx1, wl["w_sq"][0], mode="nt", name="mm_out_dx")
    dw_out = _mm(sv["merged"], dx1, mode="tn", name="mm_out_dw")
    g["w_sq"] = jnp.concatenate([jnp.stack([dw_out, dw_q, dw_o]), dw_kv])
    dbd, dgates = _merge_bwd(p, sv["bd"], dm)
    dbr = mm_carrying(["w_sq"], dbd, wl["w_br"], mode="nt", a_kind="batch", b_kind="batch", name="mm_branch_dx")
    g["w_br"] = _mm(sv["br"], dbd, mode="tn", a_kind="batch", b_kind="batch", name="mm_branch_dw")
    dya = dbr[0].astype(BF16).reshape(S, SB_HEADS, SB_DH).transpose(1, 0, 2)
    (dq_h, dk_h, dv_h), rode = _sb_bwd(sv["q"], sv["k"], sv["v"], dya, sv["tot"], sv["cnt"], sb_rider)
    dz, g["sgu_ln_g"], g["sgu_ln_b"], g["w_spatial"], g["b_spatial"] = _sgu_bwd(
        p, dbr[1], wl["sgu_ln_g"], wl["sgu_ln_b"], wl["w_spatial"], wl["w_spatial"].transpose(0, 2, 1), sv["b_st"])
    dcb, dcc, dcx, g["conv_w"] = _conv_bwd(p, wl["conv_w"], dbr[2])
    dp = jnp.concatenate([_unheads(dq_h, S), _unheads(dk_h, S).astype(BF16), _unheads(dv_h, S).astype(BF16),
                          dz, dcb, dcc, dcx, dgates], axis=1)
    g["w_in"] = _mm(sv["h1"], dp, mode="tn", name="mm_in_dw")
    dh1 = mm_carrying(["w_br", "w_in"], dp, wl["w_in"], mode="nt", name="mm_in_dx")
    dx, g["norm_mix_g"] = _rms_bwd(sv["x"], wl["norm_mix_g"], dh1, dx1, name="rms_mix_bwd")
    return dx, g, land, rode


def _local_step(x, mem, target, layers, final_g):
    h, saved = x, []
    for wl in layers:
        h, sv, _ = _layer_fwd(h, mem, wl)
        saved.append(sv)
    loss, dx, d_final = _loss_head(h, final_g, target)
    grads = [None] * len(layers)
    for l in reversed(range(len(layers))):
        dx, grads[l], _, _ = _layer_bwd(dx, mem, layers[l], saved[l])
    return loss, dx, grads, d_final


_ALL = slice(None)
CONV_ROWS = 8
_SHARD = {
    "w_in": lambda b: (_ALL, pl.ds(1792 * b, 1792)),
    "w_br": lambda b: (_ALL, _ALL, pl.ds(256 * b, 256)),
    "w_sq": lambda b: (_ALL, pl.ds(256 * b, 256), _ALL),
    "w_gu": lambda b: (b,),
    "w_dn": lambda b: (b,),
    "conv_w": lambda b: (_ALL, pl.ds(128 * b, 128)),
}
_FULL_SHAPE = {"w_in": (1024, 7168), "w_br": (3, 512, 1024), "w_sq": (5, 1024, 1024), "w_gu": (4, 2, 1024, 704),
               "w_dn": (4, 704, 1024), "conv_w": (CONV_ROWS, 512)}
_SHARD_SHAPE = {"w_in": (1024, 1792), "w_br": (3, 512, 256), "w_sq": (5, 256, 1024), "w_gu": (2, 1024, 704),
                "w_dn": (704, 1024), "conv_w": (CONV_ROWS, 128)}


def _pos():
    return lax.axis_index("x"), lax.axis_index("y"), lax.axis_index("c")


def _per_chip(fn):
    x, y, _ = _pos()
    for x0 in (0, 1):
        for y0 in (0, 1):
            @pl.when((x == x0) & (y == y0))
            def _():
                fn(x0, y0)


def _other_chips(x0, y0):
    return [(1 - x0, y0), (x0, 1 - y0), (1 - x0, 1 - y0)]


def _rcopy(src, dst, ssem, rsem, dev):
    return pltpu.make_async_remote_copy(src_ref=src, dst_ref=dst, send_sem=ssem, recv_sem=rsem, device_id=dev, device_id_type=MESH)


def _dma_sems(n):
    return pltpu.SemaphoreType.DMA((n,))


def _gather_rider(layer, placed):
    names = list(placed)
    n = len(names)
    shard = lambda refs, a, b: refs[a].at[_SHARD[names[a]](b)]

    def start(ins, outs, send, recv):
        @pl.when(lax.axis_index("c") == layer)
        def _():
            def run(x0, y0):
                for kk, (px, py) in enumerate(_other_chips(x0, y0)):
                    for a in range(n):
                        own = shard(outs, a, 2 * x0 + y0)
                        _rcopy(own, own, send.at[6 * a + kk], recv.at[6 * a + kk], (px, py, layer)).start()

            _per_chip(run)

    def finish(ins, outs, send, recv):
        c = lax.axis_index("c")

        def run(x0, y0):
            chips = _other_chips(x0, y0)

            @pl.when(c == layer)
            def _():
                passed = []
                for kk, (px, py) in enumerate(chips):
                    for a in range(n):
                        landed = shard(outs, a, 2 * px + py)
                        _rcopy(landed, landed, send.at[6 * a + kk], recv.at[6 * a + kk], (px, py, layer)).wait_recv()
                        passed.append(_rcopy(landed, landed, send.at[6 * a + 3 + kk], recv.at[6 * a + 3 + kk], (x0, y0, 1 - layer)))
                        passed[-1].start()
                for kk, (px, py) in enumerate(chips):
                    for a in range(n):
                        own = shard(outs, a, 2 * x0 + y0)
                        _rcopy(own, own, send.at[6 * a + kk], recv.at[6 * a + kk], (px, py, layer)).wait_send()
                for cp in passed:
                    cp.wait_send()

            @pl.when(c != layer)
            def _():
                for kk, (px, py) in enumerate(chips):
                    for a in range(n):
                        got = shard(outs, a, 2 * px + py)
                        _rcopy(got, got, send.at[6 * a + 3 + kk], recv.at[6 * a + 3 + kk], (x0, y0, layer)).wait_recv()

        _per_chip(run)

    arrs = [placed[nm] for nm in names]
    return _Rider(arrs, [jax.ShapeDtypeStruct(t.shape, t.dtype) for t in arrs], 6 * n, start, finish, alias={a: a for a in range(n)})


def _presum_rider(layer, grads):
    names = list(grads)
    n = len(names)

    def start(ins, outs, send, recv):
        x, y, c = _pos()

        @pl.when(c != layer)
        def _():
            for a in range(n):
                _rcopy(ins[a], outs[a], send.at[a], recv.at[a], (x, y, layer)).start()

    def finish(ins, outs, send, recv):
        x, y, c = _pos()

        @pl.when(c != layer)
        def _():
            for a in range(n):
                _rcopy(ins[a], outs[a], send.at[a], recv.at[a], (x, y, layer)).wait_send()

        @pl.when(c == layer)
        def _():
            for a in range(n):
                _rcopy(outs[a], outs[a], send.at[a], recv.at[a], (x, y, 1 - layer)).wait_recv()

    arrs = [grads[nm] for nm in names]
    return _Rider(arrs, [jax.ShapeDtypeStruct(t.shape, t.dtype) for t in arrs], n, start, finish)


def _shard_rider(layer, part):
    names = list(part)
    n = len(names)

    def each(fn):
        @pl.when(lax.axis_index("c") == layer)
        def _():
            def run(x0, y0):
                for kk, (px, py) in enumerate(_other_chips(x0, y0)):
                    for a in range(n):
                        fn(a, kk, 2 * px + py, (px, py, layer))

            _per_chip(run)

    def start(ins, outs, send, recv):
        each(lambda a, kk, bp, peer: _rcopy(ins[a].at[_SHARD[names[a]](bp)], outs[a].at[kk], send.at[3 * a + kk], recv.at[3 * a + kk], peer).start())

    def finish(ins, outs, send, recv):
        each(lambda a, kk, bp, peer: _rcopy(outs[a].at[kk], outs[a].at[kk], send.at[3 * a + kk], recv.at[3 * a + kk], peer).wait_recv())
        each(lambda a, kk, bp, peer: _rcopy(ins[a].at[_SHARD[names[a]](bp)], outs[a].at[kk], send.at[3 * a + kk], recv.at[3 * a + kk], peer).wait_send())

    return _Rider([part[nm] for nm in names], [jax.ShapeDtypeStruct((N_CHIPS - 1,) + _SHARD_SHAPE[nm], part[nm].dtype) for nm in names],
                  3 * n, start, finish)


def _sibling_exchange(mine0, mine1):
    names = list(mine0)
    n = len(names)

    def body(*refs):
        l0, l1, outs = refs[:n], refs[n:2 * n], refs[2 * n:3 * n]
        send, recv = refs[3 * n:]
        x, y, c = _pos()
        for c0 in (0, 1):
            @pl.when(c == c0)
            def _():
                srcs = l0 if c0 == 0 else l1
                cps = [_rcopy(srcs[a], outs[a], send.at[a], recv.at[a], (x, y, 1 - c0)) for a in range(n)]
                for cp in cps:
                    cp.start()
                for cp in cps:
                    cp.wait()

    outs = pl.pallas_call(
        body, name="grad_sibling_exchange", in_specs=[ANY] * (2 * n), out_specs=[ANY] * n,
        out_shape=[jax.ShapeDtypeStruct(mine0[nm].shape, mine0[nm].dtype) for nm in names],
        scratch_shapes=[_dma_sems(n), _dma_sems(n)],
    )(*[mine0[nm] for nm in names], *[mine1[nm] for nm in names])
    return dict(zip(names, outs))


def _gather_small(pack):
    flips = [(fx, fy, fc) for fx in (0, 1) for fy in (0, 1) for fc in (0, 1) if fx or fy or fc]

    def body(in_ref, out_ref, send, recv, lsem):
        x, y, c = _pos()
        me = 4 * x + 2 * y + c
        own = pltpu.make_async_copy(in_ref, out_ref.at[me], lsem.at[0])
        own.start()
        peers = [(x ^ fx, y ^ fy, c ^ fc) for fx, fy, fc in flips]
        cps = [_rcopy(in_ref, out_ref.at[me], send.at[k], recv.at[k], peer) for k, peer in enumerate(peers)]
        for cp in cps:
            cp.start()
        for k, (px, py, pc) in enumerate(peers):
            slot = out_ref.at[4 * px + 2 * py + pc]
            _rcopy(slot, slot, send.at[k], recv.at[k], (px, py, pc)).wait_recv()
        for cp in cps:
            cp.wait_send()
        own.wait()

    return pl.pallas_call(
        body, name="gather_small_grads", in_specs=[ANY], out_specs=ANY,
        out_shape=jax.ShapeDtypeStruct((8,) + pack.shape, pack.dtype),
        scratch_shapes=[_dma_sems(len(flips)), _dma_sems(len(flips)), _dma_sems(1)],
    )(pack)


_WEIGHTS = ["norm_mix_g", "w_in", "sgu_ln_g", "sgu_ln_b", "w_spatial", "b_spatial", "conv_w", "w_branch", "w_out", "norm_xa_g",
            "mem_norm_g", "w_q_xa", "w_k_xa", "w_v_xa", "w_o_xa", "norm_ffn_g", "w_gate_ffn", "w_up_ffn", "w_down_ffn", "final_g"]
_REPLICATED = ["norm_mix_g", "sgu_ln_g", "sgu_ln_b", "w_spatial", "b_spatial", "norm_xa_g", "mem_norm_g", "norm_ffn_g", "final_g"]
_SQUARE = ["w_out", "w_q_xa", "w_o_xa", "w_k_xa", "w_v_xa"]
_BIG = ["w_in", "w_br", "w_sq", "w_gu", "w_dn"]


def _pack(arrs):
    return jnp.concatenate([a.reshape(-1) for a in arrs]).reshape(-1, 128)


def _step(a):
    w = {n: a[n] for n in _WEIGHTS}
    x, mem, target = a["x"][0], a["mem"][0], a["loss_target"][0]
    xi, yi, ci = _pos()
    bi = 2 * xi + yi
    groups = list(_FULL_SHAPE)

    local = {"w_in": w["w_in"], "w_br": w["w_branch"], "w_sq": jnp.stack([w[n] for n in _SQUARE], axis=1),
             "w_gu": jnp.stack([w["w_gate_ffn"], w["w_up_ffn"]], axis=1), "w_dn": w["w_down_ffn"],
             "conv_w": jnp.pad(w["conv_w"], ((0, 0), (0, CONV_ROWS - 3), (0, 0)))}
    placed = [dict(), dict()]
    for n in groups:
        placed[0][n], placed[1][n] = _place(n, local[n], bi)

    def weights(l, full):
        wl = dict(zip(groups, full))
        wl["conv_w"] = wl["conv_w"][:3]
        wl["w_gu"] = wl["w_gu"].reshape(2 * N_CHIPS, D_MODEL, FFN_SH)
        wl.update({n: w[n][l] for n in _REPLICATED if n != "final_g"})
        return wl

    wl0 = weights(0, _run_rider(_gather_rider(0, placed[0]), name="gather_layer0"))
    h, sv0, full1 = _layer_fwd(x, mem, wl0, _gather_rider(1, placed[1]))
    wl1 = weights(1, full1)
    h, sv1, _ = _layer_fwd(h, mem, wl1)
    loss, dx, d_final = _loss_head(h, w["final_g"], target)
    loss = lax.psum(loss, ("x", "y", "c"))

    def owner_sum(l, g, land):
        return {n: _add_owner(n, g[n], land[n], (ci == l).astype(jnp.int32)) for n in _BIG}

    def reduced(l, slots, part):
        return {n: _sum_chips(n, s, part[n], bi, (ci == l).astype(jnp.int32)) for n, s in zip(_BIG, slots)}

    dx, g1, land1, _ = _layer_bwd(dx, mem, wl1, sv1, presum_layer=1)
    part1 = owner_sum(1, g1, land1)
    dx, g0, land0, slots1 = _layer_bwd(dx, mem, wl0, sv0, presum_layer=0, sb_rider=_shard_rider(1, part1))
    grads = [g0, g1]
    part0 = owner_sum(0, g0, land0)
    slots0 = _run_rider(_shard_rider(0, part0), name="grad_shard_exchange_layer0")
    mine = [reduced(0, slots0, part0), reduced(1, slots1, part1)]
    theirs = _sibling_exchange(mine[0], mine[1])

    out = {}

    def adam_layers(name, group, pick=None):
        sel = (lambda t: t[group]) if pick is None else (lambda t: t[group][pick])
        out[name] = _adam_layers([sel(mine[0]), sel(mine[1])], sel(theirs), ci, w[name], a["m_" + name], a["v_" + name], name="adam_" + name)

    adam_layers("w_in", "w_in")
    adam_layers("w_branch", "w_br")
    for t, n in enumerate(_SQUARE):
        adam_layers(n, "w_sq", t)
    adam_layers("w_gate_ffn", "w_gu", 0)
    adam_layers("w_up_ffn", "w_gu", 1)
    adam_layers("w_down_ffn", "w_dn")

    def adam(name, g):
        out[name] = _reduce_adam([g], w[name], a["m_" + name], a["v_" + name], name="adam_" + name)

    small = {n: jnp.stack([g[n] for g in grads]) for n in _REPLICATED if n != "final_g"}
    small["final_g"] = d_final
    conv_g = jnp.stack([g["conv_w"] for g in grads])
    n_rep = sum(w[n].size for n in _REPLICATED) // 128
    summed = _sum_slots(_gather_small(_pack([small[n] for n in _REPLICATED] + [conv_g])), name="sum_devices_small")
    res = _reduce_adam([summed[:n_rep]], _pack([w[n] for n in _REPLICATED]), _pack([a["m_" + n] for n in _REPLICATED]),
                       _pack([a["v_" + n] for n in _REPLICATED]), name="adam_replicated")
    off = 0
    for n in _REPLICATED:
        out[n] = tuple(r.reshape(-1)[off:off + w[n].size].reshape(w[n].shape) for r in res)
        off += w[n].size
    conv_full = summed[n_rep:].reshape(conv_g.shape)
    adam("conv_w", lax.dynamic_slice_in_dim(conv_full, (2 * xi + yi) * 128, 128, axis=2))

    return (loss, dx[None], *[out[n][k] for k in range(4) for n in _WEIGHTS])


def kernel(x, mem, norm_mix_g, w_in, sgu_ln_g, sgu_ln_b, w_spatial, b_spatial, conv_w, w_branch, w_out, norm_xa_g, mem_norm_g, w_q_xa, w_k_xa, w_v_xa, w_o_xa, norm_ffn_g, w_gate_ffn, w_up_ffn, w_down_ffn, final_g, loss_target, m_norm_mix_g, m_w_in, m_sgu_ln_g, m_sgu_ln_b, m_w_spatial, m_b_spatial, m_conv_w, m_w_branch, m_w_out, m_norm_xa_g, m_mem_norm_g, m_w_q_xa, m_w_k_xa, m_w_v_xa, m_w_o_xa, m_norm_ffn_g, m_w_gate_ffn, m_w_up_ffn, m_w_down_ffn, m_final_g, v_norm_mix_g, v_w_in, v_sgu_ln_g, v_sgu_ln_b, v_w_spatial, v_b_spatial, v_conv_w, v_w_branch, v_w_out, v_norm_xa_g, v_mem_norm_g, v_w_q_xa, v_w_k_xa, v_w_v_xa, v_w_o_xa, v_norm_ffn_g, v_w_gate_ffn, v_w_up_ffn, v_w_down_ffn, v_final_g):
    return _step(dict(locals()))
```

```python
import functools
import math

import jax
import jax.numpy as jnp
from jax import lax
from jax.experimental import pallas as pl
from jax.experimental.pallas import tpu as pltpu

F32, BF16 = jnp.float32, jnp.bfloat16
MESH = pl.DeviceIdType.MESH
ANY = pl.BlockSpec(memory_space=pl.ANY)

D_MODEL = 1024
DEPTH = 2
BW = 512
SB_HEADS, SB_DH = 8, 64
SGU_LEN, SGU_GROUPS, SGU_GD, SGU_CHUNK = 128, 4, 128, 64
XA_HEADS, XA_DH = 4, 256
FFN_SH = 704
N_CHIPS = 4
IN_COLS = 7168
C_Z, C_CB, C_GATES = 1536, 2560, 4096

ADAM_LR, ADAM_B1, ADAM_B2, ADAM_EPS, ADAM_WD, ADAM_STEP = 0.001, 0.9, 0.999, 1e-08, 0.01, 10

VMEM_LIMIT_V7X = 56 * 1024 * 1024

NN = (((1,), (0,)), ((), ()))
NT = (((1,), (1,)), ((), ()))
TN = (((0,), (0,)), ((), ()))


def _cp(*sem):
    return pltpu.CompilerParams(dimension_semantics=sem, vmem_limit_bytes=VMEM_LIMIT_V7X)


def _tile(n, pref):
    for t in pref:
        if n % t == 0:
            return t
    return n


def _rows(r, row_bytes, block_bytes=1 << 20):
    fits = [t for t in range(8, r + 1, 8) if r % t == 0 and t * row_bytes <= block_bytes]
    return max(fits) if fits else r


class _Rider:
    def __init__(self, ins, outs, n_sems, start, finish, alias=None):
        self.ins, self.outs, self.n_sems, self.start, self.finish, self.alias = list(ins), list(outs), n_sems, start, finish, alias or {}


def _call_with_rider(rider, body, *, name, grid, in_specs, args, out_specs, out_shape, scratch_shapes, semantics):
    if rider is None:
        return pl.pallas_call(body, name=name, grid=grid, in_specs=in_specs, out_specs=out_specs, out_shape=out_shape,
                              scratch_shapes=scratch_shapes, compiler_params=_cp(*semantics))(*args)
    n_in, n_out, r_in, r_out = len(args), len(out_shape), len(rider.ins), len(rider.outs)

    def riding(*refs):
        ins, rins = refs[:n_in], refs[n_in:n_in + r_in]
        outs, routs = refs[n_in + r_in:n_in + r_in + n_out], refs[n_in + r_in + n_out:n_in + r_in + n_out + r_out]
        rest = refs[n_in + r_in + n_out + r_out:]
        scratch, send, recv = rest[:-2], rest[-2], rest[-1]
        ids = [pl.program_id(ax) for ax in range(len(grid))]
        first, last = ids[0] == 0, ids[0] == grid[0] - 1
        for ax in range(1, len(grid)):
            first, last = first & (ids[ax] == 0), last & (ids[ax] == grid[ax] - 1)

        @pl.when(first)
        def _():
            rider.start(rins, routs, send, recv)

        body(*ins, *outs, *scratch)

        @pl.when(last)
        def _():
            rider.finish(rins, routs, send, recv)

    return pl.pallas_call(
        riding, name=name, grid=grid, in_specs=list(in_specs) + [ANY] * r_in, out_specs=list(out_specs) + [ANY] * r_out,
        out_shape=list(out_shape) + rider.outs, scratch_shapes=list(scratch_shapes) + [_dma_sems(rider.n_sems), _dma_sems(rider.n_sems)],
        input_output_aliases={n_in + i: n_out + o for o, i in rider.alias.items()},
        compiler_params=_cp(*["arbitrary"] * len(grid)),
    )(*args, *rider.ins)


def _run_rider(rider, *, name):
    def nothing(*refs):
        pass

    return _call_with_rider(rider, nothing, name=name, grid=(1,), in_specs=[], args=[], out_specs=[], out_shape=[], scratch_shapes=[],
                            semantics=("arbitrary",))


def _mm(a, b, *, mode, name, out_dtype=F32, res=None, a_kind="2d", b_kind="2d", tm=None, tn=None, tk=None, rider=None):
    a2, b2 = a.shape[-2:], b.shape[-2:]
    if mode == "nn":
        (M, K), N = a2, b2[1]
    elif mode == "nt":
        (M, K), N = a2, b2[0]
    else:
        (K, M), N = a2, b2[1]
    kchunk = a_kind == "kchunk" or b_kind == "kchunk"
    batch = a_kind == "batch" or b_kind == "batch"
    G = (a.shape[0] if a_kind == "batch" else b.shape[0]) if batch else 1
    tm = tm or _tile(M, (1024, 512, 256, 128))
    tn = tn or _tile(N, (1024, 512, 256, 128))
    if kchunk:
        tk, nk = K, (a.shape[0] if a_kind == "kchunk" else b.shape[0])
    else:
        tk = tk or _tile(K, (1024, 512, 256, 128))
        nk = K // tk

    def spec(kind, blk, idx):
        if kind == "2d":
            return pl.BlockSpec(blk, lambda g, i, j, k: idx(g, i, j, k))
        if kind == "batch":
            return pl.BlockSpec((None,) + blk, lambda g, i, j, k: (g,) + idx(g, i, j, k))
        return pl.BlockSpec((None,) + blk, lambda g, i, j, k: (k,) + idx(g, i, j, 0))

    if mode == "nn":
        a_spec = spec(a_kind, (tm, tk), lambda g, i, j, k: (i, k))
        b_spec = spec(b_kind, (tk, tn), lambda g, i, j, k: (k, j))
    elif mode == "nt":
        a_spec = spec(a_kind, (tm, tk), lambda g, i, j, k: (i, k))
        b_spec = spec(b_kind, (tn, tk), lambda g, i, j, k: (j, k))
    else:
        a_spec = spec(a_kind, (tk, tm), lambda g, i, j, k: (k, i))
        b_spec = spec(b_kind, (tk, tn), lambda g, i, j, k: (k, j))
    o_kind = "batch" if batch else "2d"
    o_spec = spec(o_kind, (tm, tn), lambda g, i, j, k: (i, j))
    o_shape = ((G,) if batch else ()) + (M, N)
    dn = {"nn": NN, "nt": NT, "tn": TN}[mode]
    has_res = res is not None

    def body(*refs):
        if has_res:
            a_ref, b_ref, r_ref, o_ref = refs[:4]
        else:
            a_ref, b_ref, o_ref = refs[:3]
        p = lax.dot_general(a_ref[...].astype(BF16), b_ref[...].astype(BF16), dn, preferred_element_type=F32)

        def finish(r):
            if has_res:
                r = r + r_ref[...]
            o_ref[...] = r.astype(out_dtype)

        if nk == 1:
            finish(p)
        else:
            acc = refs[-1]
            k = pl.program_id(3)

            @pl.when(k == 0)
            def _():
                acc[...] = p

            @pl.when(k > 0)
            def _():
                acc[...] += p

            @pl.when(k == nk - 1)
            def _():
                finish(acc[...])

    in_specs, args = [a_spec, b_spec], [a, b]
    if has_res:
        in_specs.append(spec("2d", (tm, tn), lambda g, i, j, k: (i, j)))
        args.append(res)
    outs = _call_with_rider(
        rider, body, name=name, grid=(G, M // tm, N // tn, nk), in_specs=in_specs, args=args, out_specs=[o_spec],
        out_shape=[jax.ShapeDtypeStruct(o_shape, out_dtype)], scratch_shapes=[pltpu.VMEM((tm, tn), F32)] if nk > 1 else [],
        semantics=("parallel", "parallel", "parallel", "arbitrary"))
    return outs[0] if rider is None else (outs[0], outs[1:])


def _rms_fwd(x, g, *, name):
    S, Dm = x.shape
    tm = _tile(S, (512, 256))

    def body(x_ref, g_ref, o_ref):
        xv = x_ref[...]
        r = lax.rsqrt(jnp.mean(xv * xv, axis=-1, keepdims=True) + 1e-6)
        o_ref[...] = (xv * r * g_ref[...]).astype(BF16)

    return pl.pallas_call(
        body, name=name, grid=(S // tm,),
        in_specs=[pl.BlockSpec((tm, Dm), lambda i: (i, 0)), pl.BlockSpec((1, Dm), lambda i: (0, 0))],
        out_specs=pl.BlockSpec((tm, Dm), lambda i: (i, 0)), out_shape=jax.ShapeDtypeStruct((S, Dm), BF16),
        compiler_params=_cp("parallel"),
    )(x, g.reshape(1, Dm))


def _rms_bwd(x, g, dh, dres, *, name):
    S, Dm = x.shape
    tm = _tile(S, (512, 256))

    def body(x_ref, g_ref, dh_ref, dr_ref, dx_ref, dg_ref):
        xv, dhv = x_ref[...], dh_ref[...].astype(F32)
        r = lax.rsqrt(jnp.mean(xv * xv, axis=-1, keepdims=True) + 1e-6)
        u = dhv * g_ref[...]
        s = jnp.sum(u * xv, axis=-1, keepdims=True)
        dx_ref[...] = dr_ref[...] + r * u - xv * ((r * r * r) * (s * (1.0 / Dm)))
        part = jnp.sum(dhv * (xv * r), axis=0, keepdims=True)

        @pl.when(pl.program_id(0) == 0)
        def _():
            dg_ref[...] = part

        @pl.when(pl.program_id(0) > 0)
        def _():
            dg_ref[...] += part

    row = pl.BlockSpec((tm, Dm), lambda i: (i, 0))
    vec = pl.BlockSpec((1, Dm), lambda i: (0, 0))
    dx, dg = pl.pallas_call(
        body, name=name, grid=(S // tm,), in_specs=[row, vec, row, row], out_specs=[row, vec],
        out_shape=[jax.ShapeDtypeStruct((S, Dm), F32), jax.ShapeDtypeStruct((1, Dm), F32)],
        compiler_params=_cp("arbitrary"),
    )(x, g.reshape(1, Dm), dh, dres)
    return dx, dg.reshape(Dm)


def _loss_head(x, g, target):
    S, Dm = x.shape
    tm = _tile(S, (512, 256))

    def body(x_ref, g_ref, t_ref, dx_ref, dg_ref, loss_ref):
        xv, gv = x_ref[...], g_ref[...]
        r = lax.rsqrt(jnp.mean(xv * xv, axis=-1, keepdims=True) + 1e-6)
        xn = xv * r
        err = xn * gv - t_ref[...]
        lpart = 0.5 * jnp.sum(jnp.mean(err * err, axis=-1, keepdims=True), axis=0, keepdims=True)
        dy = err * (1.0 / Dm)
        u = dy * gv
        s = jnp.sum(u * xv, axis=-1, keepdims=True)
        dx_ref[...] = r * u - xv * ((r * r * r) * (s * (1.0 / Dm)))
        part = jnp.sum(dy * xn, axis=0, keepdims=True)
        lslab = jnp.broadcast_to(lpart, (8, 128))

        @pl.when(pl.program_id(0) == 0)
        def _():
            dg_ref[...] = part
            loss_ref[...] = lslab

        @pl.when(pl.program_id(0) > 0)
        def _():
            dg_ref[...] += part
            loss_ref[...] += lslab

    row = pl.BlockSpec((tm, Dm), lambda i: (i, 0))
    vec = pl.BlockSpec((1, Dm), lambda i: (0, 0))
    dx, dg, loss = pl.pallas_call(
        body, name="loss_head", grid=(S // tm,), in_specs=[row, vec, row],
        out_specs=[row, vec, pl.BlockSpec((8, 128), lambda i: (0, 0))],
        out_shape=[jax.ShapeDtypeStruct((S, Dm), F32), jax.ShapeDtypeStruct((1, Dm), F32), jax.ShapeDtypeStruct((8, 128), F32)],
        compiler_params=_cp("arbitrary"),
    )(x, g.reshape(1, Dm), target)
    return loss[0, 0], dx, dg.reshape(Dm)


SB_TQ, SB_TK = 256, 256
SB_EXP_FLOOR = -104.0


def _split2(v):
    hi = v.astype(BF16)
    return jnp.concatenate([hi, (v - hi.astype(F32)).astype(BF16)], axis=1)


def _tri2(cmp):
    j = lax.broadcasted_iota(jnp.int32, (2 * SB_TK, SB_TK), 0) % SB_TK
    s = lax.broadcasted_iota(jnp.int32, (2 * SB_TK, SB_TK), 1)
    return cmp(j, s).astype(BF16)


def _sb_scores(qv, kb, k0, q0, tq):
    rows = qv.shape[0]
    z = lax.dot_general(qv, kb, NT, preferred_element_type=F32) * (SB_DH ** -0.5)
    t_pos = q0 + lax.broadcasted_iota(jnp.int32, (rows, SB_TK), 0) % tq
    s_pos = k0 + lax.broadcasted_iota(jnp.int32, (rows, SB_TK), 1)
    valid = s_pos < t_pos
    ls = jnp.minimum(z, 0.0) - jnp.log(1.0 + jnp.exp(-jnp.abs(z)))
    l1m = jnp.where(valid, ls - z, 0.0)
    return z, valid, ls, l1m


SB_PAIRS = SB_HEADS // 2
_Q_BLK, _K_BLK, _V_BLK = 0, SB_PAIRS, 2 * SB_PAIRS


def _wide(x):
    return x if SB_TK == 128 else jnp.concatenate([x] * (SB_TK // 128), axis=1)


def _lanes_of(h, shape):
    lane = lax.broadcasted_iota(jnp.int32, shape, len(shape) - 1)
    return (lane < SB_DH) if h == 0 else (lane >= SB_DH)


def _sb2_fwd(p, rider=None):
    S = p.shape[0]
    tq = min(SB_TQ, S)
    kb_per_q = tq // SB_TK

    def body(q_ref, k_ref, v_ref, o_ref, tot_ref, cnt_ref, qm, acc, c):
        i = pl.program_id(1)
        q0 = i * tq
        later = _tri2(lambda j, s: j > s)
        q2 = q_ref[...]
        for h in range(2):
            qm[h * tq:(h + 1) * tq, :] = jnp.where(_lanes_of(h, q2.shape), q2, 0.0).astype(BF16)
        acc[...] = jnp.zeros_like(acc)
        c[...] = jnp.zeros_like(c)
        nkb = (i + 1) * kb_per_q

        def more(st):
            n, highest = st
            return (n < nkb) & (highest > SB_EXP_FLOOR)

        def step(st):
            n, _ = st
            k0 = pl.multiple_of((nkb - 1 - n) * SB_TK, SB_TK)
            kb, vb = k_ref[pl.ds(k0, SB_TK), :].astype(BF16), v_ref[pl.ds(k0, SB_TK), :].astype(BF16)
            c_old = c[...]
            z, valid, ls, l1m = _sb_scores(qm[...], kb, k0, q0, tq)
            c_new = c_old + jnp.sum(l1m, axis=1, keepdims=True)
            after = jnp.dot(_split2(l1m), later, preferred_element_type=F32)
            a = jnp.where(valid, jnp.exp(ls + after + _wide(c_old)), 0.0)
            av = jnp.dot(a.astype(BF16), vb, preferred_element_type=F32)
            acc[...] += jnp.where(_lanes_of(0, (tq, 128)), av[:tq], av[tq:])
            c[...] = c_new
            return n + 1, jnp.max(c_new)

        n_done, _ = lax.while_loop(more, step, (jnp.int32(0), jnp.float32(0.0)))
        o_ref[...] = acc[...].astype(o_ref.dtype)
        for h in range(2):
            tot_ref[h] = c[h * tq:(h + 1) * tq, :]
        cnt_ref[...] = jnp.full(cnt_ref.shape, n_done.astype(F32))

    col = lambda first: pl.BlockSpec((S, 128), lambda g, i: (0, first + g))
    outs = _call_with_rider(
        rider, body, name="sb_fwd", grid=(SB_PAIRS, S // tq), args=[p, p, p],
        in_specs=[pl.BlockSpec((tq, 128), lambda g, i: (i, _Q_BLK + g)), col(_K_BLK), col(_V_BLK)],
        out_specs=[pl.BlockSpec((tq, 128), lambda g, i: (i, g)), pl.BlockSpec((2, tq, 128), lambda g, i: (g, i, 0)),
                   pl.BlockSpec((None, None, 8, 128), lambda g, i: (g, i, 0, 0))],
        out_shape=[jax.ShapeDtypeStruct((S, BW), BF16), jax.ShapeDtypeStruct((SB_HEADS, S, 128), F32),
                   jax.ShapeDtypeStruct((SB_PAIRS, S // tq, 8, 128), F32)],
        scratch_shapes=[pltpu.VMEM((2 * tq, 128), BF16), pltpu.VMEM((tq, 128), F32), pltpu.VMEM((2 * tq, 128), F32)],
        semantics=("parallel", "parallel"))
    return outs[:3], outs[3:]


def _sb2_bwd(p, dbr, tot, cnt, rider=None):
    S = p.shape[0]
    tq = min(SB_TQ, S)
    kb_per_q = tq // SB_TK
    scale = SB_DH ** -0.5

    def body(q_ref, k_ref, v_ref, do_ref, tot_ref, cnt_ref, dq_ref, dk_ref, dv_ref, qm, dom, tot, dq_acc, pre, gpre):
        i = pl.program_id(1)
        q0 = i * tq
        upto = _tri2(lambda j, s: j <= s)
        before = _tri2(lambda j, s: j < s)

        @pl.when(i == 0)
        def _():
            dk_ref[...] = jnp.zeros_like(dk_ref)
            dv_ref[...] = jnp.zeros_like(dv_ref)

        q2, do2 = q_ref[...], do_ref[...]
        for h in range(2):
            rows = slice(h * tq, (h + 1) * tq)
            qm[rows, :] = jnp.where(_lanes_of(h, q2.shape), q2, 0.0).astype(BF16)
            dom[rows, :] = jnp.where(_lanes_of(h, do2.shape), do2, 0.0).astype(BF16)
            tot[rows, :] = tot_ref[h]
        dq_acc[...] = jnp.zeros_like(dq_acc)
        pre[...] = jnp.zeros_like(pre)
        gpre[...] = jnp.zeros_like(gpre)

        n_done = jnp.max(cnt_ref[...]).astype(jnp.int32)
        first = (i + 1) * kb_per_q - n_done

        def step(n, carry):
            k0 = pl.multiple_of((first + n) * SB_TK, SB_TK)
            kb, vb = k_ref[pl.ds(k0, SB_TK), :].astype(BF16), v_ref[pl.ds(k0, SB_TK), :].astype(BF16)
            pre_o, gpre_o = pre[...], gpre[...]
            z, valid, ls, l1m = _sb_scores(qm[...], kb, k0, q0, tq)
            incl = jnp.dot(_split2(l1m), upto, preferred_element_type=F32)
            rest = _wide(tot[...] - pre_o) - incl
            a = jnp.where(valid, jnp.exp(ls + rest), 0.0)
            da = lax.dot_general(dom[...], vb, NT, preferred_element_type=F32)
            g = a * da
            gbefore = jnp.dot(_split2(g), before, preferred_element_type=F32) + _wide(gpre_o)
            dz = jnp.where(valid, g * jnp.exp(ls - z) - jnp.exp(ls) * gbefore, 0.0) * scale
            dzb = dz.astype(BF16)
            dq_p = jnp.dot(dzb, kb, preferred_element_type=F32)
            dq_acc[...] += jnp.where(_lanes_of(0, (tq, 128)), dq_p[:tq], dq_p[tq:])
            dk_ref[pl.ds(k0, SB_TK), :] += lax.dot_general(dzb, qm[...], TN, preferred_element_type=F32)
            dv_ref[pl.ds(k0, SB_TK), :] += lax.dot_general(a.astype(BF16), dom[...], TN, preferred_element_type=F32)
            pre[...] = pre_o + jnp.sum(l1m, axis=1, keepdims=True)
            gpre[...] = gpre_o + jnp.sum(g, axis=1, keepdims=True)
            return carry

        lax.fori_loop(0, n_done, step, 0)
        dq_ref[...] = dq_acc[...].astype(dq_ref.dtype)

    col = lambda first: pl.BlockSpec((S, 128), lambda g, i: (0, first + g))
    tile = pl.BlockSpec((tq, 128), lambda g, i: (i, g))
    whole = pl.BlockSpec((S, 128), lambda g, i: (0, g))
    outs = _call_with_rider(
        rider, body, name="sb_bwd", grid=(SB_PAIRS, S // tq), args=[p, p, p, dbr, tot, cnt],
        in_specs=[pl.BlockSpec((tq, 128), lambda g, i: (i, _Q_BLK + g)), col(_K_BLK), col(_V_BLK),
                  pl.BlockSpec((None, tq, 128), lambda g, i: (0, i, g)), pl.BlockSpec((2, tq, 128), lambda g, i: (g, i, 0)),
                  pl.BlockSpec((None, None, 8, 128), lambda g, i: (g, i, 0, 0))],
        out_specs=[tile, whole, whole],
        out_shape=[jax.ShapeDtypeStruct((S, BW), BF16), jax.ShapeDtypeStruct((S, BW), F32), jax.ShapeDtypeStruct((S, BW), F32)],
        scratch_shapes=[pltpu.VMEM((2 * tq, 128), BF16), pltpu.VMEM((2 * tq, 128), BF16), pltpu.VMEM((2 * tq, 128), F32),
                        pltpu.VMEM((tq, 128), F32), pltpu.VMEM((2 * tq, 128), F32), pltpu.VMEM((2 * tq, 128), F32)],
        semantics=("parallel", "arbitrary"))
    return outs[:3], outs[3:]


_INV_SQRT2 = 0.7071067811865476
_INV_SQRT2PI = 0.3989422804014327


def _gelu(x):
    return 0.5 * x * (1.0 + lax.erf(x * _INV_SQRT2))


def _gelu_grad(x):
    return 0.5 * (1.0 + lax.erf(x * _INV_SQRT2)) + x * (_INV_SQRT2PI * jnp.exp(-0.5 * x * x))


def _sgu_mask():
    t = lax.broadcasted_iota(jnp.int32, (SGU_LEN, SGU_LEN), 0) // SGU_CHUNK
    s = lax.broadcasted_iota(jnp.int32, (SGU_LEN, SGU_LEN), 1) // SGU_CHUNK
    return t >= s


def _sgu_mask_t():
    t = lax.broadcasted_iota(jnp.int32, (SGU_LEN, SGU_LEN), 0) // SGU_CHUNK
    s = lax.broadcasted_iota(jnp.int32, (SGU_LEN, SGU_LEN), 1) // SGU_CHUNK
    return s >= t


def _sgu_norm(zv, g, b):
    vv = _gelu(zv)
    xc = vv - jnp.mean(vv, axis=-1, keepdims=True)
    rstd = lax.rsqrt(jnp.mean(xc * xc, axis=-1, keepdims=True) + 1e-5)
    xhat = xc * rstd
    return xhat, rstd, xhat * g + b


SGU_TM = 256


def _sgu_fwd(p, ln_g, ln_b, w_s, b_st):
    S = p.shape[0]
    tm = min(SGU_TM, S)

    def body(zu_ref, zv_ref, g_ref, b_ref, w_ref, bs_ref, o_ref):
        u = _gelu(zu_ref[...])
        _, _, vn = _sgu_norm(zv_ref[...], g_ref[...], b_ref[...])
        vnb = vn.astype(BF16)
        mask = _sgu_mask()
        for gi in range(SGU_GROUPS):
            wg = jnp.where(mask, w_ref[gi], 0.0).astype(BF16)
            cols = slice(gi * SGU_GD, (gi + 1) * SGU_GD)
            for ci in range(tm // SGU_LEN):
                rows = slice(ci * SGU_LEN, (ci + 1) * SGU_LEN)
                vm = jnp.dot(wg, vnb[rows, cols], preferred_element_type=F32) + bs_ref[:, gi:gi + 1]
                o_ref[rows, cols] = (u[rows, cols] * vm).astype(BF16)

    vec = pl.BlockSpec((1, BW), lambda i: (0, 0))
    return pl.pallas_call(
        body, name="sgu_fwd", grid=(S // tm,),
        in_specs=[pl.BlockSpec((tm, BW), lambda i: (i, C_Z // BW)), pl.BlockSpec((tm, BW), lambda i: (i, C_Z // BW + 1)), vec, vec,
                  pl.BlockSpec((SGU_GROUPS, SGU_LEN, SGU_LEN), lambda i: (0, 0, 0)), pl.BlockSpec((SGU_LEN, SGU_GROUPS), lambda i: (0, 0))],
        out_specs=pl.BlockSpec((tm, BW), lambda i: (i, 0)), out_shape=jax.ShapeDtypeStruct((S, BW), BF16),
        compiler_params=_cp("parallel"),
    )(p, p, ln_g.reshape(1, BW), ln_b.reshape(1, BW), w_s, b_st)


def _sgu_bwd(p, dyb, ln_g, ln_b, w_s, w_st, b_st):
    S = p.shape[0]
    tm = min(SGU_TM, S)

    def body(zu_ref, zv_ref, dy_ref, g_ref, b_ref, w_ref, wt_ref, bs_ref, dz_ref, dg_ref, db_ref, dw_ref, dbs_ref, dvn):
        first = pl.program_id(0) == 0

        @pl.when(first)
        def _():
            dg_ref[...] = jnp.zeros_like(dg_ref)
            db_ref[...] = jnp.zeros_like(db_ref)
            dw_ref[...] = jnp.zeros_like(dw_ref)
            dbs_ref[...] = jnp.zeros_like(dbs_ref)

        zu, zv, dy = zu_ref[...], zv_ref[...], dy_ref[...].astype(F32)
        u = _gelu(zu)
        xhat, rstd, vn = _sgu_norm(zv, g_ref[...], b_ref[...])
        vnb = vn.astype(BF16)
        mask = _sgu_mask()
        mask_t = _sgu_mask_t()
        for gi in range(SGU_GROUPS):
            wg = jnp.where(mask, w_ref[gi], 0.0).astype(BF16)
            wgt = jnp.where(mask_t, wt_ref[gi], 0.0).astype(BF16)
            cols = slice(gi * SGU_GD, (gi + 1) * SGU_GD)
            for ci in range(tm // SGU_LEN):
                rows = slice(ci * SGU_LEN, (ci + 1) * SGU_LEN)
                vm = jnp.dot(wg, vnb[rows, cols], preferred_element_type=F32) + bs_ref[:, gi:gi + 1]
                dyc = dy[rows, cols]
                dz_ref[rows, cols] = (dyc * vm * _gelu_grad(zu[rows, cols])).astype(BF16)
                dvm = dyc * u[rows, cols]
                dvmb = dvm.astype(BF16)
                dbs_ref[gi] += jnp.broadcast_to(jnp.sum(dvm, axis=1, keepdims=True), (SGU_LEN, SGU_GD))
                dw_ref[gi] += lax.dot_general(dvmb, vnb[rows, cols], NT, preferred_element_type=F32)
                dvn[rows, cols] = jnp.dot(wgt, dvmb, preferred_element_type=F32)
        dvnv = dvn[...]
        dg_ref[...] += jnp.sum(dvnv * xhat, axis=0, keepdims=True)
        db_ref[...] += jnp.sum(dvnv, axis=0, keepdims=True)
        dxh = dvnv * g_ref[...]
        dvv = rstd * (dxh - jnp.mean(dxh, axis=-1, keepdims=True) - xhat * jnp.mean(dxh * xhat, axis=-1, keepdims=True))
        dz_ref[:, BW:] = (dvv * _gelu_grad(zv)).astype(BF16)

        @pl.when(pl.program_id(0) == n_steps - 1)
        def _():
            for gi in range(SGU_GROUPS):
                dw_ref[gi] = jnp.where(mask, dw_ref[gi], 0.0)

    n_steps = S // tm
    vec = pl.BlockSpec((1, BW), lambda i: (0, 0))
    half = lambda c: pl.BlockSpec((tm, BW), lambda i: (i, c))
    wspec = pl.BlockSpec((SGU_GROUPS, SGU_LEN, SGU_LEN), lambda i: (0, 0, 0))
    dz, dg, db, dw, dbs = pl.pallas_call(
        body, name="sgu_bwd", grid=(n_steps,),
        in_specs=[half(C_Z // BW), half(C_Z // BW + 1), half(0), vec, vec, wspec, wspec,
                  pl.BlockSpec((SGU_LEN, SGU_GROUPS), lambda i: (0, 0))],
        out_specs=[pl.BlockSpec((tm, 2 * BW), lambda i: (i, 0)), vec, vec, wspec, wspec],
        out_shape=[jax.ShapeDtypeStruct((S, 2 * BW), BF16), jax.ShapeDtypeStruct((1, BW), F32), jax.ShapeDtypeStruct((1, BW), F32),
                   jax.ShapeDtypeStruct((SGU_GROUPS, SGU_LEN, SGU_LEN), F32), jax.ShapeDtypeStruct((SGU_GROUPS, SGU_LEN, SGU_GD), F32)],
        scratch_shapes=[pltpu.VMEM((tm, BW), F32)],
        compiler_params=_cp("arbitrary"),
    )(p, p, dyb, ln_g.reshape(1, BW), ln_b.reshape(1, BW), w_s, w_st, b_st)
    return dz, dg.reshape(BW), db.reshape(BW), dw, dbs[:, :, 0]


CONV_TC = 128


def _shift_down(y, n):
    rows = lax.broadcasted_iota(jnp.int32, y.shape, 0)
    return jnp.where(rows < n, 0.0, pltpu.roll(y, n, 0))


def _shift_up(y, n):
    rows = lax.broadcasted_iota(jnp.int32, y.shape, 0)
    return jnp.where(rows >= y.shape[0] - n, 0.0, pltpu.roll(y, y.shape[0] - n, 0))


def _conv_specs(S):
    col = lambda c0: pl.BlockSpec((S, CONV_TC), lambda j: (0, c0 // CONV_TC + j))
    return col(C_CB), col(C_CB + BW), col(C_CB + 2 * BW), pl.BlockSpec((3, CONV_TC), lambda j: (0, j)), pl.BlockSpec((S, CONV_TC), lambda j: (0, j))


def _conv_fwd(p, conv_w):
    S = p.shape[0]

    def body(cb_ref, cc_ref, cx_ref, w_ref, o_ref):
        y = cc_ref[...] * cx_ref[...]
        conv = w_ref[0:1, :] * _shift_down(y, 2) + w_ref[1:2, :] * _shift_down(y, 1) + w_ref[2:3, :] * y
        o_ref[...] = (cb_ref[...] * conv).astype(BF16)

    cb, cc, cx, wspec, out = _conv_specs(S)
    return pl.pallas_call(
        body, name="conv_fwd", grid=(BW // CONV_TC,), in_specs=[cb, cc, cx, wspec], out_specs=out,
        out_shape=jax.ShapeDtypeStruct((S, BW), BF16), compiler_params=_cp("parallel"),
    )(p, p, p, conv_w)


def _conv_bwd(p, conv_w, dyc):
    S = p.shape[0]

    def body(cb_ref, cc_ref, cx_ref, w_ref, dy_ref, db_ref, dc_ref, dx_ref, dw_ref):
        cc, cx, dy = cc_ref[...], cx_ref[...], dy_ref[...].astype(F32)
        y = cc * cx
        w0, w1, w2 = w_ref[0:1, :], w_ref[1:2, :], w_ref[2:3, :]
        y1, y2 = _shift_down(y, 1), _shift_down(y, 2)
        conv = w0 * y2 + w1 * y1 + w2 * y
        db_ref[...] = (dy * conv).astype(BF16)
        dconv = dy * cb_ref[...]
        dyy = w2 * dconv + w1 * _shift_up(dconv, 1) + w0 * _shift_up(dconv, 2)
        dc_ref[...] = (dyy * cx).astype(BF16)
        dx_ref[...] = (dyy * cc).astype(BF16)
        dw_ref[0:1, :] = jnp.sum(dconv * y2, axis=0, keepdims=True)
        dw_ref[1:2, :] = jnp.sum(dconv * y1, axis=0, keepdims=True)
        dw_ref[2:3, :] = jnp.sum(dconv * y, axis=0, keepdims=True)

    cb, cc, cx, wspec, out = _conv_specs(S)
    db, dc, dx, dw = pl.pallas_call(
        body, name="conv_bwd", grid=(BW // CONV_TC,), in_specs=[cb, cc, cx, wspec, out],
        out_specs=[out, out, out, wspec],
        out_shape=[jax.ShapeDtypeStruct((S, BW), BF16)] * 3 + [jax.ShapeDtypeStruct((3, BW), F32)],
        compiler_params=_cp("parallel"),
    )(p, p, p, conv_w, dyc)
    return db, dc, dx, dw


def _merge_specs(S, tm):
    gate = lambda n: pl.BlockSpec((tm, D_MODEL), lambda i: (i, C_GATES // D_MODEL + n))
    return [gate(0), gate(1), gate(2)], pl.BlockSpec((3, tm, D_MODEL), lambda i: (0, i, 0)), pl.BlockSpec((tm, D_MODEL), lambda i: (i, 0))


def _merge_fwd(p, bd):
    S = p.shape[0]
    tm = _tile(S, (256,))

    def body(g0, g1, g2, b_ref, o_ref):
        acc = jax.nn.sigmoid(g0[...]) * b_ref[0]
        acc = acc + jax.nn.sigmoid(g1[...]) * b_ref[1]
        acc = acc + jax.nn.sigmoid(g2[...]) * b_ref[2]
        o_ref[...] = acc.astype(BF16)

    gates, bspec, row = _merge_specs(S, tm)
    return pl.pallas_call(
        body, name="merge_fwd", grid=(S // tm,), in_specs=gates + [bspec], out_specs=row,
        out_shape=jax.ShapeDtypeStruct((S, D_MODEL), BF16), compiler_params=_cp("parallel"),
    )(p, p, p, bd)


def _merge_bwd(p, bd, dm):
    S = p.shape[0]
    tm = _tile(S, (256,))

    def body(g0, g1, g2, b_ref, dm_ref, db_ref, dg_ref):
        dmv = dm_ref[...]
        for n, g_ref in enumerate((g0, g1, g2)):
            sg = jax.nn.sigmoid(g_ref[...])
            db_ref[n] = (dmv * sg).astype(BF16)
            dg_ref[:, n * D_MODEL:(n + 1) * D_MODEL] = (dmv * b_ref[n] * (sg * (1.0 - sg))).astype(BF16)

    gates, bspec, row = _merge_specs(S, tm)
    return pl.pallas_call(
        body, name="merge_bwd", grid=(S // tm,), in_specs=gates + [bspec, row],
        out_specs=[bspec, pl.BlockSpec((tm, 3 * D_MODEL), lambda i: (i, 0))],
        out_shape=[jax.ShapeDtypeStruct((3, S, D_MODEL), BF16), jax.ShapeDtypeStruct((S, 3 * D_MODEL), BF16)],
        compiler_params=_cp("parallel"),
    )(p, p, p, bd, dm)


XA_TM = 512


def _xa_probs(qh, kh):
    s = lax.dot_general(qh, kh, NT, preferred_element_type=F32) * (XA_DH ** -0.5)
    e = jnp.exp(s - jnp.max(s, axis=-1, keepdims=True))
    return e / jnp.sum(e, axis=-1, keepdims=True)


def _xa_fwd(q, kv):
    S = q.shape[0]
    tm = min(XA_TM, S)
    M = kv.shape[1]

    def body(q_ref, kv_ref, o_ref):
        for h in range(XA_HEADS):
            cols = slice(h * XA_DH, (h + 1) * XA_DH)
            pr = _xa_probs(q_ref[:, cols], kv_ref[0, :, cols])
            o_ref[:, cols] = jnp.dot(pr.astype(BF16), kv_ref[1, :, cols], preferred_element_type=F32).astype(BF16)

    row = pl.BlockSpec((tm, D_MODEL), lambda i: (i, 0))
    return pl.pallas_call(
        body, name="xa_fwd", grid=(S // tm,), in_specs=[row, pl.BlockSpec((2, M, D_MODEL), lambda i: (0, 0, 0))], out_specs=row,
        out_shape=jax.ShapeDtypeStruct((S, D_MODEL), BF16), compiler_params=_cp("parallel"),
    )(q, kv)


def _xa_bwd(q, kv, do):
    S = q.shape[0]
    tm = min(XA_TM, S)
    M = kv.shape[1]

    def body(q_ref, kv_ref, do_ref, dq_ref, dkv_ref):
        @pl.when(pl.program_id(0) == 0)
        def _():
            dkv_ref[...] = jnp.zeros_like(dkv_ref)

        for h in range(XA_HEADS):
            cols = slice(h * XA_DH, (h + 1) * XA_DH)
            qh, kh, vh, doh = q_ref[:, cols], kv_ref[0, :, cols], kv_ref[1, :, cols], do_ref[:, cols]
            pr = _xa_probs(qh, kh)
            dkv_ref[1, :, cols] += lax.dot_general(pr.astype(BF16), doh, TN, preferred_element_type=F32)
            dp = lax.dot_general(doh, vh, NT, preferred_element_type=F32)
            ds = (pr * (dp - jnp.sum(dp * pr, axis=-1, keepdims=True)) * (XA_DH ** -0.5)).astype(BF16)
            dq_ref[:, cols] = jnp.dot(ds, kh, preferred_element_type=F32).astype(BF16)
            dkv_ref[0, :, cols] += lax.dot_general(ds, qh, TN, preferred_element_type=F32)

    row = pl.BlockSpec((tm, D_MODEL), lambda i: (i, 0))
    kvs = pl.BlockSpec((2, M, D_MODEL), lambda i: (0, 0, 0))
    return pl.pallas_call(
        body, name="xa_bwd", grid=(S // tm,), in_specs=[row, kvs, row], out_specs=[row, kvs],
        out_shape=[jax.ShapeDtypeStruct((S, D_MODEL), BF16), jax.ShapeDtypeStruct((2, M, D_MODEL), F32)],
        compiler_params=_cp("arbitrary"),
    )(q, kv, do)


def _swiglu_fwd(ab):
    nb, _, S, C = ab.shape
    tm = _tile(S, (512, 256))

    def body(ab_ref, o_ref):
        a = ab_ref[0]
        o_ref[...] = (a * jax.nn.sigmoid(a) * ab_ref[1]).astype(BF16)

    return pl.pallas_call(
        body, name="swiglu_fwd", grid=(nb, S // tm),
        in_specs=[pl.BlockSpec((None, 2, tm, C), lambda j, i: (j, 0, i, 0))], out_specs=pl.BlockSpec((None, tm, C), lambda j, i: (j, i, 0)),
        out_shape=jax.ShapeDtypeStruct((nb, S, C), BF16), compiler_params=_cp("parallel", "parallel"),
    )(ab)


def _swiglu_bwd(ab, dh):
    nb, _, S, C = ab.shape
    tm = _tile(S, (512, 256))

    def body(ab_ref, dh_ref, o_ref):
        a, b, d = ab_ref[0], ab_ref[1], dh_ref[...].astype(F32)
        sg = jax.nn.sigmoid(a)
        o_ref[0] = (d * b * (sg * (1.0 + a * (1.0 - sg)))).astype(BF16)
        o_ref[1] = (d * (a * sg)).astype(BF16)

    pair = pl.BlockSpec((None, 2, tm, C), lambda j, i: (j, 0, i, 0))
    return pl.pallas_call(
        body, name="swiglu_bwd", grid=(nb, S // tm), in_specs=[pair, pl.BlockSpec((None, tm, C), lambda j, i: (j, i, 0))], out_specs=pair,
        out_shape=jax.ShapeDtypeStruct(ab.shape, BF16), compiler_params=_cp("parallel", "parallel"),
    )(ab, dh)


def _reduce_adam(parts, w, m, v, *, name):
    shape = w.shape
    C = shape[-1]
    R = math.prod(shape[:-1])
    tm = _rows(R, 4 * C)
    n = len(parts)
    c1, c2 = 1.0 - ADAM_B1 ** ADAM_STEP, 1.0 - ADAM_B2 ** ADAM_STEP

    def body(*refs):
        g = refs[0][...]
        for r in refs[1:n]:
            g = g + r[...]
        w_ref, m_ref, v_ref, go, do, mo, vo = refs[n:]
        mn = ADAM_B1 * m_ref[...] + (1.0 - ADAM_B1) * g
        vn = ADAM_B2 * v_ref[...] + (1.0 - ADAM_B2) * (g * g)
        go[...] = g
        do[...] = -ADAM_LR * ((mn / c1) / (jnp.sqrt(vn / c2) + ADAM_EPS) + ADAM_WD * w_ref[...])
        mo[...] = mn
        vo[...] = vn

    row = pl.BlockSpec((tm, C), lambda i: (i, 0))
    outs = pl.pallas_call(
        body, name=name, grid=(R // tm,), in_specs=[row] * (n + 3), out_specs=[row] * 4,
        out_shape=[jax.ShapeDtypeStruct((R, C), F32)] * 4, compiler_params=_cp("parallel"),
    )(*[a.reshape(R, C) for a in (*parts, w, m, v)])
    return tuple(o.reshape(shape) for o in outs)


_VIEW = {
    "w_in": ((1024, 7168), (1024, 1792), 256, lambda i, b: (i, b)),
    "w_br": ((1536, 1024), (1536, 256), 512, lambda i, b: (i, b)),
    "w_sq": ((5120, 1024), (1280, 1024), 256, lambda i, b: (4 * i + b, 0)),
    "w_gu": ((8192, 704), (2048, 704), 512, lambda i, b: (4 * b + i, 0)),
    "w_dn": ((2816, 1024), (704, 1024), 352, lambda i, b: (2 * b + i, 0)),
    "conv_w": ((8, 512), (8, 128), 8, lambda i, b: (0, b)),
}


def _scalar(v):
    return jnp.asarray(v, jnp.int32).reshape(1)


def _place(name, local, b):
    full2, sh2, tm, idx = _VIEW[name]
    C = sh2[1]
    dt = local.dtype if name == "conv_w" else BF16

    def body(b_ref, x_ref, o0_ref, o1_ref):
        o0_ref[...] = x_ref[0].astype(dt)
        o1_ref[...] = x_ref[1].astype(dt)

    place = pl.BlockSpec((tm, C), lambda i, bs: idx(i, bs[0]))
    outs = pl.pallas_call(
        body, name="place_" + name,
        grid_spec=pltpu.PrefetchScalarGridSpec(num_scalar_prefetch=1, grid=(sh2[0] // tm,),
                                               in_specs=[pl.BlockSpec((DEPTH, tm, C), lambda i, bs: (0, i, 0))], out_specs=[place, place]),
        out_shape=[jax.ShapeDtypeStruct(full2, dt)] * 2, compiler_params=_cp("arbitrary"),
    )(_scalar(b), local.reshape((DEPTH,) + sh2))
    return [o.reshape(_FULL_SHAPE[name]) for o in outs]


def _add_owner(name, g, land, own):
    shape = g.shape
    C = shape[-1]
    R = math.prod(shape[:-1])
    tm = _rows(R, 4 * C)

    def body(s_ref, g_ref, l_ref, o_ref):
        @pl.when(s_ref[0] != 0)
        def _():
            o_ref[...] = (g_ref[...] + l_ref[...]).astype(BF16)

        @pl.when(s_ref[0] == 0)
        def _():
            o_ref[...] = jnp.zeros_like(o_ref)

    pick = pl.BlockSpec((tm, C), lambda i, s: (jnp.where(s[0] != 0, i, 0), 0))
    return pl.pallas_call(
        body, name="presum_" + name,
        grid_spec=pltpu.PrefetchScalarGridSpec(num_scalar_prefetch=1, grid=(R // tm,), in_specs=[pick, pick],
                                               out_specs=pl.BlockSpec((tm, C), lambda i, s: (i, 0))),
        out_shape=jax.ShapeDtypeStruct((R, C), BF16), compiler_params=_cp("arbitrary"),
    )(_scalar(own), g.reshape(R, C), land.reshape(R, C)).reshape(shape)


def _sum_chips(name, slots, part, b, own):
    full2, sh2, tm, idx = _VIEW[name]
    C = sh2[1]

    def body(s_ref, slot_ref, own_ref, o_ref):
        @pl.when(s_ref[1] != 0)
        def _():
            o_ref[...] = ((slot_ref[0].astype(F32) + slot_ref[1].astype(F32)) + slot_ref[2].astype(F32)) + own_ref[...].astype(F32)

        @pl.when(s_ref[1] == 0)
        def _():
            o_ref[...] = jnp.zeros_like(o_ref)

    return pl.pallas_call(
        body, name="sum_chips_" + name,
        grid_spec=pltpu.PrefetchScalarGridSpec(
            num_scalar_prefetch=1, grid=(sh2[0] // tm,),
            in_specs=[pl.BlockSpec((3, tm, C), lambda i, s: (0, jnp.where(s[1] != 0, i, 0), 0)),
                      pl.BlockSpec((tm, C), lambda i, s: idx(jnp.where(s[1] != 0, i, 0), s[0]))],
            out_specs=pl.BlockSpec((tm, C), lambda i, s: (i, 0))),
        out_shape=jax.ShapeDtypeStruct(sh2, F32), compiler_params=_cp("arbitrary"),
    )(jnp.stack([jnp.asarray(b, jnp.int32), jnp.asarray(own, jnp.int32)]), slots.reshape((3,) + sh2),
      part.reshape(full2)).reshape(_SHARD_SHAPE[name])


def _adam_layers(mine, theirs, c, w, m, v, *, name):
    shape = w.shape
    C = shape[-1]
    R = math.prod(shape[1:-1])
    tm = _rows(R, 4 * C)
    c1, c2 = 1.0 - ADAM_B1 ** ADAM_STEP, 1.0 - ADAM_B2 ** ADAM_STEP

    def body(c_ref, m0_ref, m1_ref, t_ref, w_ref, m_ref, v_ref, go, do, mo, vo):
        layer = pl.program_id(0)
        g = jnp.where(layer == c_ref[0], jnp.where(layer == 0, m0_ref[...], m1_ref[...]), t_ref[...])
        mn = ADAM_B1 * m_ref[...] + (1.0 - ADAM_B1) * g
        vn = ADAM_B2 * v_ref[...] + (1.0 - ADAM_B2) * (g * g)
        go[...] = g
        do[...] = -ADAM_LR * ((mn / c1) / (jnp.sqrt(vn / c2) + ADAM_EPS) + ADAM_WD * w_ref[...])
        mo[...] = mn
        vo[...] = vn

    def own(layer):
        return pl.BlockSpec((tm, C), lambda l, i, cs: (jnp.where((l == layer) & (cs[0] == layer), i, 0), 0))

    recv = pl.BlockSpec((tm, C), lambda l, i, cs: (jnp.where(l == cs[0], 0, i), 0))
    row = pl.BlockSpec((None, tm, C), lambda l, i, cs: (l, i, 0))
    outs = pl.pallas_call(
        body, name=name,
        grid_spec=pltpu.PrefetchScalarGridSpec(num_scalar_prefetch=1, grid=(DEPTH, R // tm),
                                               in_specs=[own(0), own(1), recv, row, row, row], out_specs=[row] * 4),
        out_shape=[jax.ShapeDtypeStruct((DEPTH, R, C), F32)] * 4, compiler_params=_cp("arbitrary", "arbitrary"),
    )(_scalar(c), mine[0].reshape(R, C), mine[1].reshape(R, C), theirs.reshape(R, C), *[t.reshape(DEPTH, R, C) for t in (w, m, v)])
    return tuple(o.reshape(shape) for o in outs)


def _sum_slots(r, *, name):
    n, shape = r.shape[0], r.shape[1:]
    C = shape[-1]
    R = math.prod(shape[:-1])
    tm = _rows(R, 4 * C * n, 8 << 20)

    def body(r_ref, o_ref):
        acc = r_ref[0]
        for s in range(1, n):
            acc = acc + r_ref[s]
        o_ref[...] = acc

    return pl.pallas_call(
        body, name=name, grid=(R // tm,), in_specs=[pl.BlockSpec((n, tm, C), lambda i: (0, i, 0))],
        out_specs=pl.BlockSpec((tm, C), lambda i: (i, 0)), out_shape=jax.ShapeDtypeStruct((R, C), F32), compiler_params=_cp("parallel"),
    )(r.reshape(n, R, C)).reshape(shape)


def _layer_fwd(x, mem, wl, sb_rider=None):
    S = x.shape[0]
    sv = {"x": x}
    h1 = _rms_fwd(x, wl["norm_mix_g"], name="rms_mix")
    p = _mm(h1, wl["w_in"], mode="nn", name="mm_in")
    (ya, tot, cnt), rode = _sb2_fwd(p, sb_rider)
    b_st = wl["b_spatial"].T
    yb = _sgu_fwd(p, wl["sgu_ln_g"], wl["sgu_ln_b"], wl["w_spatial"], b_st)
    yc = _conv_fwd(p, wl["conv_w"])
    br = jnp.stack([ya, yb, yc])
    bd = _mm(br, wl["w_br"], mode="nn", a_kind="batch", b_kind="batch", name="mm_branch")
    merged = _merge_fwd(p, bd)
    x1 = _mm(merged, wl["w_sq"][0], mode="nn", res=x, name="mm_out")
    h2 = _rms_fwd(x1, wl["norm_xa_g"], name="rms_xa")
    qx = _mm(h2, wl["w_sq"][1], mode="nn", out_dtype=BF16, name="mm_q")
    mn = _rms_fwd(mem, wl["mem_norm_g"], name="rms_mem")
    kv = _mm(mn, wl["w_sq"][3:5], mode="nn", b_kind="batch", out_dtype=BF16, name="mm_kv")
    o = _xa_fwd(qx, kv)
    x2 = _mm(o, wl["w_sq"][2], mode="nn", res=x1, name="mm_o")
    h3 = _rms_fwd(x2, wl["norm_ffn_g"], name="rms_ffn")
    ab = _mm(h3, wl["w_gu"], mode="nn", b_kind="batch", name="mm_gu").reshape(N_CHIPS, 2, S, FFN_SH)
    hh = _swiglu_fwd(ab)
    x3 = _mm(hh, wl["w_dn"], mode="nn", a_kind="kchunk", b_kind="kchunk", res=x2, name="mm_down")
    sv.update(h1=h1, p=p, tot=tot, cnt=cnt, br=br, bd=bd, merged=merged, x1=x1, h2=h2, qx=qx, mn=mn, kv=kv, o=o,
              x2=x2, h3=h3, ab=ab, hh=hh, b_st=b_st)
    return x3, sv, rode


def _layer_bwd(dx3, mem, wl, sv, presum_layer=None, sb_rider=None):
    S = dx3.shape[0]
    p = sv["p"]
    g, land = {}, {}

    def mm_carrying(names, *args, **kw):
        if presum_layer is None:
            return _mm(*args, **kw)
        out, arrived = _mm(*args, **kw, rider=_presum_rider(presum_layer, {n: g[n] for n in names}))
        land.update(zip(names, arrived))
        return out

    dhh = _mm(dx3, wl["w_dn"], mode="nt", b_kind="batch", name="mm_down_dx")
    g["w_dn"] = _mm(sv["hh"], dx3, mode="tn", a_kind="batch", name="mm_down_dw")
    dab = _swiglu_bwd(sv["ab"], dhh).reshape(2 * N_CHIPS, S, FFN_SH)
    g["w_gu"] = _mm(sv["h3"], dab, mode="tn", b_kind="batch", name="mm_gu_dw").reshape(_FULL_SHAPE["w_gu"])
    dh3 = mm_carrying(["w_dn", "w_gu"], dab, wl["w_gu"], mode="nt", a_kind="kchunk", b_kind="kchunk", name="mm_gu_dx")
    dx2, g["norm_ffn_g"] = _rms_bwd(sv["x2"], wl["norm_ffn_g"], dh3, dx3, name="rms_ffn_bwd")
    do = _mm(dx2, wl["w_sq"][2], mode="nt", out_dtype=BF16, name="mm_o_dx")
    dw_o = _mm(sv["o"], dx2, mode="tn", name="mm_o_dw")
    dq, dkv = _xa_bwd(sv["qx"], sv["kv"], do)
    dw_q = _mm(sv["h2"], dq, mode="tn", name="mm_q_dw")
    dh2 = _mm(dq, wl["w_sq"][1], mode="nt", name="mm_q_dx")
    dw_kv = _mm(sv["mn"], dkv, mode="tn", b_kind="batch", name="mm_kv_dw")
    dmn = _mm(dkv, wl["w_sq"][3:5], mode="nt", a_kind="kchunk", b_kind="kchunk", name="mm_kv_dx")
    _, g["mem_norm_g"] = _rms_bwd(mem, wl["mem_norm_g"], dmn, jnp.zeros_like(mem), name="rms_mem_bwd")
    dx1, g["norm_xa_g"] = _rms_bwd(sv["x1"], wl["norm_xa_g"], dh2, dx2, name="rms_xa_bwd")
    dm = _mm(dx1, wl["w_sq"][0], mode="nt", name="mm_out_dx")
    dw_out = _mm(sv["merged"], dx1, mode="tn", name="mm_out_dw")
    g["w_sq"] = jnp.concatenate([jnp.stack([dw_out, dw_q, dw_o]), dw_kv])
    dbd, dgates = _merge_bwd(p, sv["bd"], dm)
    dbr = mm_carrying(["w_sq"], dbd, wl["w_br"], mode="nt", a_kind="batch", b_kind="batch", name="mm_branch_dx")
    g["w_br"] = _mm(sv["br"], dbd, mode="tn", a_kind="batch", b_kind="batch", name="mm_branch_dw")
    (dq, dk, dv), rode = _sb2_bwd(p, dbr, sv["tot"], sv["cnt"], sb_rider)
    dz, g["sgu_ln_g"], g["sgu_ln_b"], g["w_spatial"], g["b_spatial"] = _sgu_bwd(
        p, dbr[1], wl["sgu_ln_g"], wl["sgu_ln_b"], wl["w_spatial"], wl["w_spatial"].transpose(0, 2, 1), sv["b_st"])
    dcb, dcc, dcx, g["conv_w"] = _conv_bwd(p, wl["conv_w"], dbr[2])
    dp = jnp.concatenate([dq, dk.astype(BF16), dv.astype(BF16), dz, dcb, dcc, dcx, dgates], axis=1)
    g["w_in"] = _mm(sv["h1"], dp, mode="tn", name="mm_in_dw")
    dh1 = mm_carrying(["w_br", "w_in"], dp, wl["w_in"], mode="nt", name="mm_in_dx")
    dx, g["norm_mix_g"] = _rms_bwd(sv["x"], wl["norm_mix_g"], dh1, dx1, name="rms_mix_bwd")
    return dx, g, land, rode


def _local_step(x, mem, target, layers, final_g):
    h, saved = x, []
    for wl in layers:
        h, sv, _ = _layer_fwd(h, mem, wl)
        saved.append(sv)
    loss, dx, d_final = _loss_head(h, final_g, target)
    grads = [None] * len(layers)
    for l in reversed(range(len(layers))):
        dx, grads[l], _, _ = _layer_bwd(dx, mem, layers[l], saved[l])
    return loss, dx, grads, d_final


_ALL = slice(None)
CONV_ROWS = 8
_SHARD = {
    "w_in": lambda b: (_ALL, pl.ds(1792 * b, 1792)),
    "w_br": lambda b: (_ALL, _ALL, pl.ds(256 * b, 256)),
    "w_sq": lambda b: (_ALL, pl.ds(256 * b, 256), _ALL),
    "w_gu": lambda b: (b,),
    "w_dn": lambda b: (b,),
    "conv_w": lambda b: (_ALL, pl.ds(128 * b, 128)),
}
_FULL_SHAPE = {"w_in": (1024, 7168), "w_br": (3, 512, 1024), "w_sq": (5, 1024, 1024), "w_gu": (4, 2, 1024, 704),
               "w_dn": (4, 704, 1024), "conv_w": (CONV_ROWS, 512)}
_SHARD_SHAPE = {"w_in": (1024, 1792), "w_br": (3, 512, 256), "w_sq": (5, 256, 1024), "w_gu": (2, 1024, 704),
                "w_dn": (704, 1024), "conv_w": (CONV_ROWS, 128)}


def _pos():
    return lax.axis_index("x"), lax.axis_index("y"), lax.axis_index("c")


def _per_chip(fn):
    x, y, _ = _pos()
    for x0 in (0, 1):
        for y0 in (0, 1):
            @pl.when((x == x0) & (y == y0))
            def _():
                fn(x0, y0)


def _other_chips(x0, y0):
    return [(1 - x0, y0), (x0, 1 - y0), (1 - x0, 1 - y0)]


def _rcopy(src, dst, ssem, rsem, dev):
    return pltpu.make_async_remote_copy(src_ref=src, dst_ref=dst, send_sem=ssem, recv_sem=rsem, device_id=dev, device_id_type=MESH)


def _dma_sems(n):
    return pltpu.SemaphoreType.DMA((n,))


def _gather_rider(layer, placed):
    names = list(placed)
    n = len(names)
    shard = lambda refs, a, b: refs[a].at[_SHARD[names[a]](b)]

    def start(ins, outs, send, recv):
        @pl.when(lax.axis_index("c") == layer)
        def _():
            def run(x0, y0):
                for kk, (px, py) in enumerate(_other_chips(x0, y0)):
                    for a in range(n):
                        own = shard(outs, a, 2 * x0 + y0)
                        _rcopy(own, own, send.at[6 * a + kk], recv.at[6 * a + kk], (px, py, layer)).start()

            _per_chip(run)

    def finish(ins, outs, send, recv):
        c = lax.axis_index("c")

        def run(x0, y0):
            chips = _other_chips(x0, y0)

            @pl.when(c == layer)
            def _():
                passed = []
                for kk, (px, py) in enumerate(chips):
                    for a in range(n):
                        landed = shard(outs, a, 2 * px + py)
                        _rcopy(landed, landed, send.at[6 * a + kk], recv.at[6 * a + kk], (px, py, layer)).wait_recv()
                        passed.append(_rcopy(landed, landed, send.at[6 * a + 3 + kk], recv.at[6 * a + 3 + kk], (x0, y0, 1 - layer)))
                        passed[-1].start()
                for kk, (px, py) in enumerate(chips):
                    for a in range(n):
                        own = shard(outs, a, 2 * x0 + y0)
                        _rcopy(own, own, send.at[6 * a + kk], recv.at[6 * a + kk], (px, py, layer)).wait_send()
                for cp in passed:
                    cp.wait_send()

            @pl.when(c != layer)
            def _():
                for kk, (px, py) in enumerate(chips):
                    for a in range(n):
                        got = shard(outs, a, 2 * px + py)
                        _rcopy(got, got, send.at[6 * a + 3 + kk], recv.at[6 * a + 3 + kk], (x0, y0, layer)).wait_recv()

        _per_chip(run)

    arrs = [placed[nm] for nm in names]
    return _Rider(arrs, [jax.ShapeDtypeStruct(t.shape, t.dtype) for t in arrs], 6 * n, start, finish, alias={a: a for a in range(n)})


def _presum_rider(layer, grads):
    names = list(grads)
    n = len(names)

    def start(ins, outs, send, recv):
        x, y, c = _pos()

        @pl.when(c != layer)
        def _():
            for a in range(n):
                _rcopy(ins[a], outs[a], send.at[a], recv.at[a], (x, y, layer)).start()

    def finish(ins, outs, send, recv):
        x, y, c = _pos()

        @pl.when(c != layer)
        def _():
            for a in range(n):
                _rcopy(ins[a], outs[a], send.at[a], recv.at[a], (x, y, layer)).wait_send()

        @pl.when(c == layer)
        def _():
            for a in range(n):
                _rcopy(outs[a], outs[a], send.at[a], recv.at[a], (x, y, 1 - layer)).wait_recv()

    arrs = [grads[nm] for nm in names]
    return _Rider(arrs, [jax.ShapeDtypeStruct(t.shape, t.dtype) for t in arrs], n, start, finish)


def _shard_rider(layer, part):
    names = list(part)
    n = len(names)

    def each(fn):
        @pl.when(lax.axis_index("c") == layer)
        def _():
            def run(x0, y0):
                for kk, (px, py) in enumerate(_other_chips(x0, y0)):
                    for a in range(n):
                        fn(a, kk, 2 * px + py, (px, py, layer))

            _per_chip(run)

    def start(ins, outs, send, recv):
        each(lambda a, kk, bp, peer: _rcopy(ins[a].at[_SHARD[names[a]](bp)], outs[a].at[kk], send.at[3 * a + kk], recv.at[3 * a + kk], peer).start())

    def finish(ins, outs, send, recv):
        each(lambda a, kk, bp, peer: _rcopy(outs[a].at[kk], outs[a].at[kk], send.at[3 * a + kk], recv.at[3 * a + kk], peer).wait_recv())
        each(lambda a, kk, bp, peer: _rcopy(ins[a].at[_SHARD[names[a]](bp)], outs[a].at[kk], send.at[3 * a + kk], recv.at[3 * a + kk], peer).wait_send())

    return _Rider([part[nm] for nm in names], [jax.ShapeDtypeStruct((N_CHIPS - 1,) + _SHARD_SHAPE[nm], part[nm].dtype) for nm in names],
                  3 * n, start, finish)


def _sibling_exchange(mine0, mine1):
    names = list(mine0)
    n = len(names)

    def body(*refs):
        l0, l1, outs = refs[:n], refs[n:2 * n], refs[2 * n:3 * n]
        send, recv = refs[3 * n:]
        x, y, c = _pos()
        for c0 in (0, 1):
            @pl.when(c == c0)
            def _():
                srcs = l0 if c0 == 0 else l1
                cps = [_rcopy(srcs[a], outs[a], send.at[a], recv.at[a], (x, y, 1 - c0)) for a in range(n)]
                for cp in cps:
                    cp.start()
                for cp in cps:
                    cp.wait()

    outs = pl.pallas_call(
        body, name="grad_sibling_exchange", in_specs=[ANY] * (2 * n), out_specs=[ANY] * n,
        out_shape=[jax.ShapeDtypeStruct(mine0[nm].shape, mine0[nm].dtype) for nm in names],
        scratch_shapes=[_dma_sems(n), _dma_sems(n)],
    )(*[mine0[nm] for nm in names], *[mine1[nm] for nm in names])
    return dict(zip(names, outs))


def _gather_small(pack):
    flips = [(fx, fy, fc) for fx in (0, 1) for fy in (0, 1) for fc in (0, 1) if fx or fy or fc]

    def body(in_ref, out_ref, send, recv, lsem):
        x, y, c = _pos()
        me = 4 * x + 2 * y + c
        own = pltpu.make_async_copy(in_ref, out_ref.at[me], lsem.at[0])
        own.start()
        peers = [(x ^ fx, y ^ fy, c ^ fc) for fx, fy, fc in flips]
        cps = [_rcopy(in_ref, out_ref.at[me], send.at[k], recv.at[k], peer) for k, peer in enumerate(peers)]
        for cp in cps:
            cp.start()
        for k, (px, py, pc) in enumerate(peers):
            slot = out_ref.at[4 * px + 2 * py + pc]
            _rcopy(slot, slot, send.at[k], recv.at[k], (px, py, pc)).wait_recv()
        for cp in cps:
            cp.wait_send()
        own.wait()

    return pl.pallas_call(
        body, name="gather_small_grads", in_specs=[ANY], out_specs=ANY,
        out_shape=jax.ShapeDtypeStruct((8,) + pack.shape, pack.dtype),
        scratch_shapes=[_dma_sems(len(flips)), _dma_sems(len(flips)), _dma_sems(1)],
    )(pack)


_WEIGHTS = ["norm_mix_g", "w_in", "sgu_ln_g", "sgu_ln_b", "w_spatial", "b_spatial", "conv_w", "w_branch", "w_out", "norm_xa_g",
            "mem_norm_g", "w_q_xa", "w_k_xa", "w_v_xa", "w_o_xa", "norm_ffn_g", "w_gate_ffn", "w_up_ffn", "w_down_ffn", "final_g"]
_REPLICATED = ["norm_mix_g", "sgu_ln_g", "sgu_ln_b", "w_spatial", "b_spatial", "norm_xa_g", "mem_norm_g", "norm_ffn_g", "final_g"]
_SQUARE = ["w_out", "w_q_xa", "w_o_xa", "w_k_xa", "w_v_xa"]
_BIG = ["w_in", "w_br", "w_sq", "w_gu", "w_dn"]


def _pack(arrs):
    return jnp.concatenate([a.reshape(-1) for a in arrs]).reshape(-1, 128)


def _step(a):
    w = {n: a[n] for n in _WEIGHTS}
    x, mem, target = a["x"][0], a["mem"][0], a["loss_target"][0]
    xi, yi, ci = _pos()
    bi = 2 * xi + yi
    groups = list(_FULL_SHAPE)

    local = {"w_in": w["w_in"], "w_br": w["w_branch"], "w_sq": jnp.stack([w[n] for n in _SQUARE], axis=1),
             "w_gu": jnp.stack([w["w_gate_ffn"], w["w_up_ffn"]], axis=1), "w_dn": w["w_down_ffn"],
             "conv_w": jnp.pad(w["conv_w"], ((0, 0), (0, CONV_ROWS - 3), (0, 0)))}
    placed = [dict(), dict()]
    for n in groups:
        placed[0][n], placed[1][n] = _place(n, local[n], bi)

    def weights(l, full):
        wl = dict(zip(groups, full))
        wl["conv_w"] = wl["conv_w"][:3]
        wl["w_gu"] = wl["w_gu"].reshape(2 * N_CHIPS, D_MODEL, FFN_SH)
        wl.update({n: w[n][l] for n in _REPLICATED if n != "final_g"})
        return wl

    wl0 = weights(0, _run_rider(_gather_rider(0, placed[0]), name="gather_layer0"))
    h, sv0, full1 = _layer_fwd(x, mem, wl0, _gather_rider(1, placed[1]))
    wl1 = weights(1, full1)
    h, sv1, _ = _layer_fwd(h, mem, wl1)
    loss, dx, d_final = _loss_head(h, w["final_g"], target)
    loss = lax.psum(loss, ("x", "y", "c"))

    def owner_sum(l, g, land):
        return {n: _add_owner(n, g[n], land[n], (ci == l).astype(jnp.int32)) for n in _BIG}

    def reduced(l, slots, part):
        return {n: _sum_chips(n, s, part[n], bi, (ci == l).astype(jnp.int32)) for n, s in zip(_BIG, slots)}

    dx, g1, land1, _ = _layer_bwd(dx, mem, wl1, sv1, presum_layer=1)
    part1 = owner_sum(1, g1, land1)
    dx, g0, land0, slots1 = _layer_bwd(dx, mem, wl0, sv0, presum_layer=0, sb_rider=_shard_rider(1, part1))
    grads = [g0, g1]
    part0 = owner_sum(0, g0, land0)
    slots0 = _run_rider(_shard_rider(0, part0), name="grad_shard_exchange_layer0")
    mine = [reduced(0, slots0, part0), reduced(1, slots1, part1)]
    theirs = _sibling_exchange(mine[0], mine[1])

    out = {}

    def adam_layers(name, group, pick=None):
        sel = (lambda t: t[group]) if pick is None else (lambda t: t[group][pick])
        out[name] = _adam_layers([sel(mine[0]), sel(mine[1])], sel(theirs), ci, w[name], a["m_" + name], a["v_" + name], name="adam_" + name)

    adam_layers("w_in", "w_in")
    adam_layers("w_branch", "w_br")
    for t, n in enumerate(_SQUARE):
        adam_layers(n, "w_sq", t)
    adam_layers("w_gate_ffn", "w_gu", 0)
    adam_layers("w_up_ffn", "w_gu", 1)
    adam_layers("w_down_ffn", "w_dn")

    def adam(name, g):
        out[name] = _reduce_adam([g], w[name], a["m_" + name], a["v_" + name], name="adam_" + name)

    small = {n: jnp.stack([g[n] for g in grads]) for n in _REPLICATED if n != "final_g"}
    small["final_g"] = d_final
    conv_g = jnp.stack([g["conv_w"] for g in grads])
    n_rep = sum(w[n].size for n in _REPLICATED) // 128
    summed = _sum_slots(_gather_small(_pack([small[n] for n in _REPLICATED] + [conv_g])), name="sum_devices_small")
    res = _reduce_adam([summed[:n_rep]], _pack([w[n] for n in _REPLICATED]), _pack([a["m_" + n] for n in _REPLICATED]),
                       _pack([a["v_" + n] for n in _REPLICATED]), name="adam_replicated")
    off = 0
    for n in _REPLICATED:
        out[n] = tuple(r.reshape(-1)[off:off + w[n].size].reshape(w[n].shape) for r in res)
        off += w[n].size
    conv_full = summed[n_rep:].reshape(conv_g.shape)
    adam("conv_w", lax.dynamic_slice_in_dim(conv_full, (2 * xi + yi) * 128, 128, axis=2))

    return (loss, dx[None], *[out[n][k] for k in range(4) for n in _WEIGHTS])


def kernel(x, mem, norm_mix_g, w_in, sgu_ln_g, sgu_ln_b, w_spatial, b_spatial, conv_w, w_branch, w_out, norm_xa_g, mem_norm_g, w_q_xa, w_k_xa, w_v_xa, w_o_xa, norm_ffn_g, w_gate_ffn, w_up_ffn, w_down_ffn, final_g, loss_target, m_norm_mix_g, m_w_in, m_sgu_ln_g, m_sgu_ln_b, m_w_spatial, m_b_spatial, m_conv_w, m_w_branch, m_w_out, m_norm_xa_g, m_mem_norm_g, m_w_q_xa, m_w_k_xa, m_w_v_xa, m_w_o_xa, m_norm_ffn_g, m_w_gate_ffn, m_w_up_ffn, m_w_down_ffn, m_final_g, v_norm_mix_g, v_w_in, v_sgu_ln_g, v_sgu_ln_b, v_w_spatial, v_b_spatial, v_conv_w, v_w_branch, v_w_out, v_norm_xa_g, v_mem_norm_g, v_w_q_xa, v_w_k_xa, v_w_v_xa, v_w_o_xa, v_norm_ffn_g, v_w_gate_ffn, v_w_up_ffn, v_w_down_ffn, v_final_g):
    return _step(dict(locals()))
```

```python
import functools
import math

import jax
import jax.numpy as jnp
from jax import lax
from jax.experimental import pallas as pl
from jax.experimental.pallas import tpu as pltpu

F32, BF16 = jnp.float32, jnp.bfloat16
MESH = pl.DeviceIdType.MESH
ANY = pl.BlockSpec(memory_space=pl.ANY)

D_MODEL = 1024
DEPTH = 2
BW = 512
SB_HEADS, SB_DH = 8, 64
SGU_LEN, SGU_GROUPS, SGU_GD, SGU_CHUNK = 128, 4, 128, 64
XA_HEADS, XA_DH = 4, 256
FFN_SH = 704
N_CHIPS = 4
IN_COLS = 7168
C_Z, C_CB, C_GATES = 1536, 2560, 4096

ADAM_LR, ADAM_B1, ADAM_B2, ADAM_EPS, ADAM_WD, ADAM_STEP = 0.001, 0.9, 0.999, 1e-08, 0.01, 10

VMEM_LIMIT_V7X = 56 * 1024 * 1024

NN = (((1,), (0,)), ((), ()))
NT = (((1,), (1,)), ((), ()))
TN = (((0,), (0,)), ((), ()))


def _cp(*sem):
    return pltpu.CompilerParams(dimension_semantics=sem, vmem_limit_bytes=VMEM_LIMIT_V7X)


def _tile(n, pref):
    for t in pref:
        if n % t == 0:
            return t
    return n


def _rows(r, row_bytes, block_bytes=1 << 20):
    fits = [t for t in range(8, r + 1, 8) if r % t == 0 and t * row_bytes <= block_bytes]
    return max(fits) if fits else r


class _Rider:
    def __init__(self, ins, outs, n_sems, start, finish, alias=None, middle=None):
        self.ins, self.outs, self.n_sems, self.alias = list(ins), list(outs), n_sems, alias or {}
        self.start, self.middle, self.finish = start, middle, finish


class _Sems:
    def __init__(self, ref, first):
        self.ref, self.first = ref, first

    @property
    def at(self):
        return self

    def __getitem__(self, k):
        return self.ref.at[self.first + k]


def _join(riders):
    riders = [r for r in riders if r is not None]
    if not riders:
        return None
    spans, i0, o0, s0 = [], 0, 0, 0
    for r in riders:
        spans.append((r, i0, o0, s0))
        i0, o0, s0 = i0 + len(r.ins), o0 + len(r.outs), s0 + r.n_sems

    def phase(which):
        def run(ins, outs, send, recv):
            for r, i, o, s in spans:
                fn = getattr(r, which)
                if fn is not None:
                    fn(ins[i:i + len(r.ins)], outs[o:o + len(r.outs)], _Sems(send, s), _Sems(recv, s))
        return run

    joined = _Rider([a for r in riders for a in r.ins], [a for r in riders for a in r.outs], s0, phase("start"), phase("finish"),
                    alias={o + k: i + v for r, i, o, s in spans for k, v in r.alias.items()}, middle=phase("middle"))
    joined.split = lambda landed: [list(landed[o:o + len(r.outs)]) for r, i, o, s in spans]
    return joined


def _call_with_rider(rider, body, *, name, grid, in_specs, args, out_specs, out_shape, scratch_shapes, semantics):
    if rider is None:
        return pl.pallas_call(body, name=name, grid=grid, in_specs=in_specs, out_specs=out_specs, out_shape=out_shape,
                              scratch_shapes=scratch_shapes, compiler_params=_cp(*semantics))(*args)
    n_in, n_out, r_in, r_out = len(args), len(out_shape), len(rider.ins), len(rider.outs)

    def riding(*refs):
        ins, rins = refs[:n_in], refs[n_in:n_in + r_in]
        outs, routs = refs[n_in + r_in:n_in + r_in + n_out], refs[n_in + r_in + n_out:n_in + r_in + n_out + r_out]
        rest = refs[n_in + r_in + n_out + r_out:]
        scratch, send, recv = rest[:-2], rest[-2], rest[-1]
        step = pl.program_id(0)
        for ax in range(1, len(grid)):
            step = step * grid[ax] + pl.program_id(ax)
        n_steps = math.prod(grid)

        @pl.when(step == 0)
        def _():
            rider.start(rins, routs, send, recv)

        body(*ins, *outs, *scratch)

        if rider.middle is not None:
            @pl.when(step == (3 * n_steps) // 5)
            def _():
                rider.middle(rins, routs, send, recv)

        @pl.when(step == n_steps - 1)
        def _():
            rider.finish(rins, routs, send, recv)

    return pl.pallas_call(
        riding, name=name, grid=grid, in_specs=list(in_specs) + [ANY] * r_in, out_specs=list(out_specs) + [ANY] * r_out,
        out_shape=list(out_shape) + rider.outs, scratch_shapes=list(scratch_shapes) + [_dma_sems(rider.n_sems), _dma_sems(rider.n_sems)],
        input_output_aliases={n_in + i: n_out + o for o, i in rider.alias.items()},
        compiler_params=_cp(*["arbitrary"] * len(grid)),
    )(*args, *rider.ins)


def _run_rider(rider, *, name):
    def nothing(*refs):
        pass

    return _call_with_rider(rider, nothing, name=name, grid=(1,), in_specs=[], args=[], out_specs=[], out_shape=[], scratch_shapes=[],
                            semantics=("arbitrary",))


def _mm(a, b, *, mode, name, out_dtype=F32, res=None, a_kind="2d", b_kind="2d", tm=None, tn=None, tk=None, rider=None):
    a2, b2 = a.shape[-2:], b.shape[-2:]
    if mode == "nn":
        (M, K), N = a2, b2[1]
    elif mode == "nt":
        (M, K), N = a2, b2[0]
    else:
        (K, M), N = a2, b2[1]
    kchunk = a_kind == "kchunk" or b_kind == "kchunk"
    batch = a_kind == "batch" or b_kind == "batch"
    G = (a.shape[0] if a_kind == "batch" else b.shape[0]) if batch else 1
    tm = tm or _tile(M, (1024, 512, 256, 128))
    tn = tn or _tile(N, (1024, 512, 256, 128))
    if kchunk:
        tk, nk = K, (a.shape[0] if a_kind == "kchunk" else b.shape[0])
    else:
        tk = tk or _tile(K, (1024, 512, 256, 128))
        nk = K // tk

    def spec(kind, blk, idx):
        if kind == "2d":
            return pl.BlockSpec(blk, lambda g, i, j, k: idx(g, i, j, k))
        if kind == "batch":
            return pl.BlockSpec((None,) + blk, lambda g, i, j, k: (g,) + idx(g, i, j, k))
        return pl.BlockSpec((None,) + blk, lambda g, i, j, k: (k,) + idx(g, i, j, 0))

    if mode == "nn":
        a_spec = spec(a_kind, (tm, tk), lambda g, i, j, k: (i, k))
        b_spec = spec(b_kind, (tk, tn), lambda g, i, j, k: (k, j))
    elif mode == "nt":
        a_spec = spec(a_kind, (tm, tk), lambda g, i, j, k: (i, k))
        b_spec = spec(b_kind, (tn, tk), lambda g, i, j, k: (j, k))
    else:
        a_spec = spec(a_kind, (tk, tm), lambda g, i, j, k: (k, i))
        b_spec = spec(b_kind, (tk, tn), lambda g, i, j, k: (k, j))
    o_kind = "batch" if batch else "2d"
    o_spec = spec(o_kind, (tm, tn), lambda g, i, j, k: (i, j))
    o_shape = ((G,) if batch else ()) + (M, N)
    dn = {"nn": NN, "nt": NT, "tn": TN}[mode]
    has_res = res is not None

    def body(*refs):
        if has_res:
            a_ref, b_ref, r_ref, o_ref = refs[:4]
        else:
            a_ref, b_ref, o_ref = refs[:3]
        p = lax.dot_general(a_ref[...].astype(BF16), b_ref[...].astype(BF16), dn, preferred_element_type=F32)

        def finish(r):
            if has_res:
                r = r + r_ref[...]
            o_ref[...] = r.astype(out_dtype)

        if nk == 1:
            finish(p)
        else:
            acc = refs[-1]
            k = pl.program_id(3)

            @pl.when(k == 0)
            def _():
                acc[...] = p

            @pl.when(k > 0)
            def _():
                acc[...] += p

            @pl.when(k == nk - 1)
            def _():
                finish(acc[...])

    in_specs, args = [a_spec, b_spec], [a, b]
    if has_res:
        in_specs.append(spec("2d", (tm, tn), lambda g, i, j, k: (i, j)))
        args.append(res)
    outs = _call_with_rider(
        rider, body, name=name, grid=(G, M // tm, N // tn, nk), in_specs=in_specs, args=args, out_specs=[o_spec],
        out_shape=[jax.ShapeDtypeStruct(o_shape, out_dtype)], scratch_shapes=[pltpu.VMEM((tm, tn), F32)] if nk > 1 else [],
        semantics=("parallel", "parallel", "parallel", "arbitrary"))
    return outs[0] if rider is None else (outs[0], outs[1:])


def _rms_fwd(x, g, *, name):
    S, Dm = x.shape
    tm = _tile(S, (512, 256))

    def body(x_ref, g_ref, o_ref):
        xv = x_ref[...]
        r = lax.rsqrt(jnp.mean(xv * xv, axis=-1, keepdims=True) + 1e-6)
        o_ref[...] = (xv * r * g_ref[...]).astype(BF16)

    return pl.pallas_call(
        body, name=name, grid=(S // tm,),
        in_specs=[pl.BlockSpec((tm, Dm), lambda i: (i, 0)), pl.BlockSpec((1, Dm), lambda i: (0, 0))],
        out_specs=pl.BlockSpec((tm, Dm), lambda i: (i, 0)), out_shape=jax.ShapeDtypeStruct((S, Dm), BF16),
        compiler_params=_cp("parallel"),
    )(x, g.reshape(1, Dm))


def _rms_bwd(x, g, dh, dres, *, name):
    S, Dm = x.shape
    tm = _tile(S, (512, 256))

    def body(x_ref, g_ref, dh_ref, dr_ref, dx_ref, dg_ref):
        xv, dhv = x_ref[...], dh_ref[...].astype(F32)
        r = lax.rsqrt(jnp.mean(xv * xv, axis=-1, keepdims=True) + 1e-6)
        u = dhv * g_ref[...]
        s = jnp.sum(u * xv, axis=-1, keepdims=True)
        dx_ref[...] = dr_ref[...] + r * u - xv * ((r * r * r) * (s * (1.0 / Dm)))
        part = jnp.sum(dhv * (xv * r), axis=0, keepdims=True)

        @pl.when(pl.program_id(0) == 0)
        def _():
            dg_ref[...] = part

        @pl.when(pl.program_id(0) > 0)
        def _():
            dg_ref[...] += part

    row = pl.BlockSpec((tm, Dm), lambda i: (i, 0))
    vec = pl.BlockSpec((1, Dm), lambda i: (0, 0))
    dx, dg = pl.pallas_call(
        body, name=name, grid=(S // tm,), in_specs=[row, vec, row, row], out_specs=[row, vec],
        out_shape=[jax.ShapeDtypeStruct((S, Dm), F32), jax.ShapeDtypeStruct((1, Dm), F32)],
        compiler_params=_cp("arbitrary"),
    )(x, g.reshape(1, Dm), dh, dres)
    return dx, dg.reshape(Dm)


def _loss_head(x, g, target):
    S, Dm = x.shape
    tm = _tile(S, (512, 256))

    def body(x_ref, g_ref, t_ref, dx_ref, dg_ref, loss_ref):
        xv, gv = x_ref[...], g_ref[...]
        r = lax.rsqrt(jnp.mean(xv * xv, axis=-1, keepdims=True) + 1e-6)
        xn = xv * r
        err = xn * gv - t_ref[...]
        lpart = 0.5 * jnp.sum(jnp.mean(err * err, axis=-1, keepdims=True), axis=0, keepdims=True)
        dy = err * (1.0 / Dm)
        u = dy * gv
        s = jnp.sum(u * xv, axis=-1, keepdims=True)
        dx_ref[...] = r * u - xv * ((r * r * r) * (s * (1.0 / Dm)))
        part = jnp.sum(dy * xn, axis=0, keepdims=True)
        lslab = jnp.broadcast_to(lpart, (8, 128))

        @pl.when(pl.program_id(0) == 0)
        def _():
            dg_ref[...] = part
            loss_ref[...] = lslab

        @pl.when(pl.program_id(0) > 0)
        def _():
            dg_ref[...] += part
            loss_ref[...] += lslab

    row = pl.BlockSpec((tm, Dm), lambda i: (i, 0))
    vec = pl.BlockSpec((1, Dm), lambda i: (0, 0))
    dx, dg, loss = pl.pallas_call(
        body, name="loss_head", grid=(S // tm,), in_specs=[row, vec, row],
        out_specs=[row, vec, pl.BlockSpec((8, 128), lambda i: (0, 0))],
        out_shape=[jax.ShapeDtypeStruct((S, Dm), F32), jax.ShapeDtypeStruct((1, Dm), F32), jax.ShapeDtypeStruct((8, 128), F32)],
        compiler_params=_cp("arbitrary"),
    )(x, g.reshape(1, Dm), target)
    return loss[0, 0], dx, dg.reshape(Dm)


SB_TQ, SB_TK = 256, 256
SB_EXP_FLOOR = -104.0


def _split2(v):
    hi = v.astype(BF16)
    return jnp.concatenate([hi, (v - hi.astype(F32)).astype(BF16)], axis=1)


def _tri2(cmp):
    j = lax.broadcasted_iota(jnp.int32, (2 * SB_TK, SB_TK), 0) % SB_TK
    s = lax.broadcasted_iota(jnp.int32, (2 * SB_TK, SB_TK), 1)
    return cmp(j, s).astype(BF16)


def _sb_scores(qv, kb, k0, q0, tq):
    rows = qv.shape[0]
    z = lax.dot_general(qv, kb, NT, preferred_element_type=F32) * (SB_DH ** -0.5)
    t_pos = q0 + lax.broadcasted_iota(jnp.int32, (rows, SB_TK), 0) % tq
    s_pos = k0 + lax.broadcasted_iota(jnp.int32, (rows, SB_TK), 1)
    valid = s_pos < t_pos
    ls = jnp.minimum(z, 0.0) - jnp.log(1.0 + jnp.exp(-jnp.abs(z)))
    l1m = jnp.where(valid, ls - z, 0.0)
    return z, valid, ls, l1m


SB_PAIRS = SB_HEADS // 2
_Q_BLK, _K_BLK, _V_BLK = 0, SB_PAIRS, 2 * SB_PAIRS


def _wide(x):
    return x if SB_TK == 128 else jnp.concatenate([x] * (SB_TK // 128), axis=1)


def _lanes_of(h, shape):
    lane = lax.broadcasted_iota(jnp.int32, shape, len(shape) - 1)
    return (lane < SB_DH) if h == 0 else (lane >= SB_DH)


def _sb2_fwd(p, rider=None):
    S = p.shape[0]
    tq = min(SB_TQ, S)
    kb_per_q = tq // SB_TK

    def body(q_ref, k_ref, v_ref, o_ref, tot_ref, cnt_ref, qm, acc, c):
        i = pl.program_id(1)
        q0 = i * tq
        later = _tri2(lambda j, s: j > s)
        q2 = q_ref[...]
        for h in range(2):
            qm[h * tq:(h + 1) * tq, :] = jnp.where(_lanes_of(h, q2.shape), q2, 0.0).astype(BF16)
        acc[...] = jnp.zeros_like(acc)
        c[...] = jnp.zeros_like(c)
        nkb = (i + 1) * kb_per_q

        def more(st):
            n, highest = st
            return (n < nkb) & (highest > SB_EXP_FLOOR)

        def step(st):
            n, _ = st
            k0 = pl.multiple_of((nkb - 1 - n) * SB_TK, SB_TK)
            kb, vb = k_ref[pl.ds(k0, SB_TK), :].astype(BF16), v_ref[pl.ds(k0, SB_TK), :].astype(BF16)
            c_old = c[...]
            z, valid, ls, l1m = _sb_scores(qm[...], kb, k0, q0, tq)
            c_new = c_old + jnp.sum(l1m, axis=1, keepdims=True)
            after = jnp.dot(_split2(l1m), later, preferred_element_type=F32)
            a = jnp.where(valid, jnp.exp(ls + after + _wide(c_old)), 0.0)
            av = jnp.dot(a.astype(BF16), vb, preferred_element_type=F32)
            acc[...] += jnp.where(_lanes_of(0, (tq, 128)), av[:tq], av[tq:])
            c[...] = c_new
            return n + 1, jnp.max(c_new)

        n_done, _ = lax.while_loop(more, step, (jnp.int32(0), jnp.float32(0.0)))
        o_ref[...] = acc[...].astype(o_ref.dtype)
        for h in range(2):
            tot_ref[h] = c[h * tq:(h + 1) * tq, :]
        cnt_ref[...] = jnp.full(cnt_ref.shape, n_done.astype(F32))

    col = lambda first: pl.BlockSpec((S, 128), lambda g, i: (0, first + g))
    outs = _call_with_rider(
        rider, body, name="sb_fwd", grid=(SB_PAIRS, S // tq), args=[p, p, p],
        in_specs=[pl.BlockSpec((tq, 128), lambda g, i: (i, _Q_BLK + g)), col(_K_BLK), col(_V_BLK)],
        out_specs=[pl.BlockSpec((tq, 128), lambda g, i: (i, g)), pl.BlockSpec((2, tq, 128), lambda g, i: (g, i, 0)),
                   pl.BlockSpec((None, None, 8, 128), lambda g, i: (g, i, 0, 0))],
        out_shape=[jax.ShapeDtypeStruct((S, BW), BF16), jax.ShapeDtypeStruct((SB_HEADS, S, 128), F32),
                   jax.ShapeDtypeStruct((SB_PAIRS, S // tq, 8, 128), F32)],
        scratch_shapes=[pltpu.VMEM((2 * tq, 128), BF16), pltpu.VMEM((tq, 128), F32), pltpu.VMEM((2 * tq, 128), F32)],
        semantics=("parallel", "parallel"))
    return outs[:3], outs[3:]


def _sb2_bwd(p, dbr, tot, cnt, rider=None):
    S = p.shape[0]
    tq = min(SB_TQ, S)
    kb_per_q = tq // SB_TK
    scale = SB_DH ** -0.5

    def body(q_ref, k_ref, v_ref, do_ref, tot_ref, cnt_ref, dq_ref, dk_ref, dv_ref, qm, dom, tot, dq_acc, pre, gpre):
        i = pl.program_id(1)
        q0 = i * tq
        upto = _tri2(lambda j, s: j <= s)
        before = _tri2(lambda j, s: j < s)

        @pl.when(i == 0)
        def _():
            dk_ref[...] = jnp.zeros_like(dk_ref)
            dv_ref[...] = jnp.zeros_like(dv_ref)

        q2, do2 = q_ref[...], do_ref[...]
        for h in range(2):
            rows = slice(h * tq, (h + 1) * tq)
            qm[rows, :] = jnp.where(_lanes_of(h, q2.shape), q2, 0.0).astype(BF16)
            dom[rows, :] = jnp.where(_lanes_of(h, do2.shape), do2, 0.0).astype(BF16)
            tot[rows, :] = tot_ref[h]
        dq_acc[...] = jnp.zeros_like(dq_acc)
        pre[...] = jnp.zeros_like(pre)
        gpre[...] = jnp.zeros_like(gpre)

        n_done = jnp.max(cnt_ref[...]).astype(jnp.int32)
        first = (i + 1) * kb_per_q - n_done

        def step(n, carry):
            k0 = pl.multiple_of((first + n) * SB_TK, SB_TK)
            kb, vb = k_ref[pl.ds(k0, SB_TK), :].astype(BF16), v_ref[pl.ds(k0, SB_TK), :].astype(BF16)
            pre_o, gpre_o = pre[...], gpre[...]
            z, valid, ls, l1m = _sb_scores(qm[...], kb, k0, q0, tq)
            incl = jnp.dot(_split2(l1m), upto, preferred_element_type=F32)
            rest = _wide(tot[...] - pre_o) - incl
            a = jnp.where(valid, jnp.exp(ls + rest), 0.0)
            da = lax.dot_general(dom[...], vb, NT, preferred_element_type=F32)
            g = a * da
            gbefore = jnp.dot(_split2(g), before, preferred_element_type=F32) + _wide(gpre_o)
            dz = jnp.where(valid, g * jnp.exp(ls - z) - jnp.exp(ls) * gbefore, 0.0) * scale
            dzb = dz.astype(BF16)
            dq_p = jnp.dot(dzb, kb, preferred_element_type=F32)
            dq_acc[...] += jnp.where(_lanes_of(0, (tq, 128)), dq_p[:tq], dq_p[tq:])
            dk_ref[pl.ds(k0, SB_TK), :] += lax.dot_general(dzb, qm[...], TN, preferred_element_type=F32)
            dv_ref[pl.ds(k0, SB_TK), :] += lax.dot_general(a.astype(BF16), dom[...], TN, preferred_element_type=F32)
            pre[...] = pre_o + jnp.sum(l1m, axis=1, keepdims=True)
            gpre[...] = gpre_o + jnp.sum(g, axis=1, keepdims=True)
            return carry

        lax.fori_loop(0, n_done, step, 0)
        dq_ref[...] = dq_acc[...].astype(dq_ref.dtype)

    col = lambda first: pl.BlockSpec((S, 128), lambda g, i: (0, first + g))
    tile = pl.BlockSpec((tq, 128), lambda g, i: (i, g))
    whole = pl.BlockSpec((S, 128), lambda g, i: (0, g))
    outs = _call_with_rider(
        rider, body, name="sb_bwd", grid=(SB_PAIRS, S // tq), args=[p, p, p, dbr, tot, cnt],
        in_specs=[pl.BlockSpec((tq, 128), lambda g, i: (i, _Q_BLK + g)), col(_K_BLK), col(_V_BLK),
                  pl.BlockSpec((None, tq, 128), lambda g, i: (0, i, g)), pl.BlockSpec((2, tq, 128), lambda g, i: (g, i, 0)),
                  pl.BlockSpec((None, None, 8, 128), lambda g, i: (g, i, 0, 0))],
        out_specs=[tile, whole, whole],
        out_shape=[jax.ShapeDtypeStruct((S, BW), BF16), jax.ShapeDtypeStruct((S, BW), F32), jax.ShapeDtypeStruct((S, BW), F32)],
        scratch_shapes=[pltpu.VMEM((2 * tq, 128), BF16), pltpu.VMEM((2 * tq, 128), BF16), pltpu.VMEM((2 * tq, 128), F32),
                        pltpu.VMEM((tq, 128), F32), pltpu.VMEM((2 * tq, 128), F32), pltpu.VMEM((2 * tq, 128), F32)],
        semantics=("parallel", "arbitrary"))
    return outs[:3], outs[3:]


_INV_SQRT2 = 0.7071067811865476
_INV_SQRT2PI = 0.3989422804014327


def _gelu(x):
    return 0.5 * x * (1.0 + lax.erf(x * _INV_SQRT2))


def _gelu_grad(x):
    return 0.5 * (1.0 + lax.erf(x * _INV_SQRT2)) + x * (_INV_SQRT2PI * jnp.exp(-0.5 * x * x))


def _sgu_mask():
    t = lax.broadcasted_iota(jnp.int32, (SGU_LEN, SGU_LEN), 0) // SGU_CHUNK
    s = lax.broadcasted_iota(jnp.int32, (SGU_LEN, SGU_LEN), 1) // SGU_CHUNK
    return t >= s


def _sgu_mask_t():
    t = lax.broadcasted_iota(jnp.int32, (SGU_LEN, SGU_LEN), 0) // SGU_CHUNK
    s = lax.broadcasted_iota(jnp.int32, (SGU_LEN, SGU_LEN), 1) // SGU_CHUNK
    return s >= t


def _sgu_norm(zv, g, b):
    vv = _gelu(zv)
    xc = vv - jnp.mean(vv, axis=-1, keepdims=True)
    rstd = lax.rsqrt(jnp.mean(xc * xc, axis=-1, keepdims=True) + 1e-5)
    xhat = xc * rstd
    return xhat, rstd, xhat * g + b


SGU_TM = 256


def _sgu_fwd(p, ln_g, ln_b, w_s, b_st):
    S = p.shape[0]
    tm = min(SGU_TM, S)

    def body(zu_ref, zv_ref, g_ref, b_ref, w_ref, bs_ref, o_ref):
        u = _gelu(zu_ref[...])
        _, _, vn = _sgu_norm(zv_ref[...], g_ref[...], b_ref[...])
        vnb = vn.astype(BF16)
        mask = _sgu_mask()
        for gi in range(SGU_GROUPS):
            wg = jnp.where(mask, w_ref[gi], 0.0).astype(BF16)
            cols = slice(gi * SGU_GD, (gi + 1) * SGU_GD)
            for ci in range(tm // SGU_LEN):
                rows = slice(ci * SGU_LEN, (ci + 1) * SGU_LEN)
                vm = jnp.dot(wg, vnb[rows, cols], preferred_element_type=F32) + bs_ref[:, gi:gi + 1]
                o_ref[rows, cols] = (u[rows, cols] * vm).astype(BF16)

    vec = pl.BlockSpec((1, BW), lambda i: (0, 0))
    return pl.pallas_call(
        body, name="sgu_fwd", grid=(S // tm,),
        in_specs=[pl.BlockSpec((tm, BW), lambda i: (i, C_Z // BW)), pl.BlockSpec((tm, BW), lambda i: (i, C_Z // BW + 1)), vec, vec,
                  pl.BlockSpec((SGU_GROUPS, SGU_LEN, SGU_LEN), lambda i: (0, 0, 0)), pl.BlockSpec((SGU_LEN, SGU_GROUPS), lambda i: (0, 0))],
        out_specs=pl.BlockSpec((tm, BW), lambda i: (i, 0)), out_shape=jax.ShapeDtypeStruct((S, BW), BF16),
        compiler_params=_cp("parallel"),
    )(p, p, ln_g.reshape(1, BW), ln_b.reshape(1, BW), w_s, b_st)


def _sgu_bwd(p, dyb, ln_g, ln_b, w_s, w_st, b_st):
    S = p.shape[0]
    tm = min(SGU_TM, S)

    def body(zu_ref, zv_ref, dy_ref, g_ref, b_ref, w_ref, wt_ref, bs_ref, dz_ref, dg_ref, db_ref, dw_ref, dbs_ref, dvn):
        first = pl.program_id(0) == 0

        @pl.when(first)
        def _():
            dg_ref[...] = jnp.zeros_like(dg_ref)
            db_ref[...] = jnp.zeros_like(db_ref)
            dw_ref[...] = jnp.zeros_like(dw_ref)
            dbs_ref[...] = jnp.zeros_like(dbs_ref)

        zu, zv, dy = zu_ref[...], zv_ref[...], dy_ref[...].astype(F32)
        u = _gelu(zu)
        xhat, rstd, vn = _sgu_norm(zv, g_ref[...], b_ref[...])
        vnb = vn.astype(BF16)
        mask = _sgu_mask()
        mask_t = _sgu_mask_t()
        for gi in range(SGU_GROUPS):
            wg = jnp.where(mask, w_ref[gi], 0.0).astype(BF16)
            wgt = jnp.where(mask_t, wt_ref[gi], 0.0).astype(BF16)
            cols = slice(gi * SGU_GD, (gi + 1) * SGU_GD)
            for ci in range(tm // SGU_LEN):
                rows = slice(ci * SGU_LEN, (ci + 1) * SGU_LEN)
                vm = jnp.dot(wg, vnb[rows, cols], preferred_element_type=F32) + bs_ref[:, gi:gi + 1]
                dyc = dy[rows, cols]
                dz_ref[rows, cols] = (dyc * vm * _gelu_grad(zu[rows, cols])).astype(BF16)
                dvm = dyc * u[rows, cols]
                dvmb = dvm.astype(BF16)
                dbs_ref[gi] += jnp.broadcast_to(jnp.sum(dvm, axis=1, keepdims=True), (SGU_LEN, SGU_GD))
                dw_ref[gi] += lax.dot_general(dvmb, vnb[rows, cols], NT, preferred_element_type=F32)
                dvn[rows, cols] = jnp.dot(wgt, dvmb, preferred_element_type=F32)
        dvnv = dvn[...]
        dg_ref[...] += jnp.sum(dvnv * xhat, axis=0, keepdims=True)
        db_ref[...] += jnp.sum(dvnv, axis=0, keepdims=True)
        dxh = dvnv * g_ref[...]
        dvv = rstd * (dxh - jnp.mean(dxh, axis=-1, keepdims=True) - xhat * jnp.mean(dxh * xhat, axis=-1, keepdims=True))
        dz_ref[:, BW:] = (dvv * _gelu_grad(zv)).astype(BF16)

        @pl.when(pl.program_id(0) == n_steps - 1)
        def _():
            for gi in range(SGU_GROUPS):
                dw_ref[gi] = jnp.where(mask, dw_ref[gi], 0.0)

    n_steps = S // tm
    vec = pl.BlockSpec((1, BW), lambda i: (0, 0))
    half = lambda c: pl.BlockSpec((tm, BW), lambda i: (i, c))
    wspec = pl.BlockSpec((SGU_GROUPS, SGU_LEN, SGU_LEN), lambda i: (0, 0, 0))
    dz, dg, db, dw, dbs = pl.pallas_call(
        body, name="sgu_bwd", grid=(n_steps,),
        in_specs=[half(C_Z // BW), half(C_Z // BW + 1), half(0), vec, vec, wspec, wspec,
                  pl.BlockSpec((SGU_LEN, SGU_GROUPS), lambda i: (0, 0))],
        out_specs=[pl.BlockSpec((tm, 2 * BW), lambda i: (i, 0)), vec, vec, wspec, wspec],
        out_shape=[jax.ShapeDtypeStruct((S, 2 * BW), BF16), jax.ShapeDtypeStruct((1, BW), F32), jax.ShapeDtypeStruct((1, BW), F32),
                   jax.ShapeDtypeStruct((SGU_GROUPS, SGU_LEN, SGU_LEN), F32), jax.ShapeDtypeStruct((SGU_GROUPS, SGU_LEN, SGU_GD), F32)],
        scratch_shapes=[pltpu.VMEM((tm, BW), F32)],
        compiler_params=_cp("arbitrary"),
    )(p, p, dyb, ln_g.reshape(1, BW), ln_b.reshape(1, BW), w_s, w_st, b_st)
    return dz, dg.reshape(BW), db.reshape(BW), dw, dbs[:, :, 0]


CONV_TC = 128


def _shift_down(y, n):
    rows = lax.broadcasted_iota(jnp.int32, y.shape, 0)
    return jnp.where(rows < n, 0.0, pltpu.roll(y, n, 0))


def _shift_up(y, n):
    rows = lax.broadcasted_iota(jnp.int32, y.shape, 0)
    return jnp.where(rows >= y.shape[0] - n, 0.0, pltpu.roll(y, y.shape[0] - n, 0))


def _conv_specs(S):
    col = lambda c0: pl.BlockSpec((S, CONV_TC), lambda j: (0, c0 // CONV_TC + j))
    return col(C_CB), col(C_CB + BW), col(C_CB + 2 * BW), pl.BlockSpec((3, CONV_TC), lambda j: (0, j)), pl.BlockSpec((S, CONV_TC), lambda j: (0, j))


def _conv_fwd(p, conv_w):
    S = p.shape[0]

    def body(cb_ref, cc_ref, cx_ref, w_ref, o_ref):
        y = cc_ref[...] * cx_ref[...]
        conv = w_ref[0:1, :] * _shift_down(y, 2) + w_ref[1:2, :] * _shift_down(y, 1) + w_ref[2:3, :] * y
        o_ref[...] = (cb_ref[...] * conv).astype(BF16)

    cb, cc, cx, wspec, out = _conv_specs(S)
    return pl.pallas_call(
        body, name="conv_fwd", grid=(BW // CONV_TC,), in_specs=[cb, cc, cx, wspec], out_specs=out,
        out_shape=jax.ShapeDtypeStruct((S, BW), BF16), compiler_params=_cp("parallel"),
    )(p, p, p, conv_w)


def _conv_bwd(p, conv_w, dyc):
    S = p.shape[0]

    def body(cb_ref, cc_ref, cx_ref, w_ref, dy_ref, db_ref, dc_ref, dx_ref, dw_ref):
        cc, cx, dy = cc_ref[...], cx_ref[...], dy_ref[...].astype(F32)
        y = cc * cx
        w0, w1, w2 = w_ref[0:1, :], w_ref[1:2, :], w_ref[2:3, :]
        y1, y2 = _shift_down(y, 1), _shift_down(y, 2)
        conv = w0 * y2 + w1 * y1 + w2 * y
        db_ref[...] = (dy * conv).astype(BF16)
        dconv = dy * cb_ref[...]
        dyy = w2 * dconv + w1 * _shift_up(dconv, 1) + w0 * _shift_up(dconv, 2)
        dc_ref[...] = (dyy * cx).astype(BF16)
        dx_ref[...] = (dyy * cc).astype(BF16)
        dw_ref[0:1, :] = jnp.sum(dconv * y2, axis=0, keepdims=True)
        dw_ref[1:2, :] = jnp.sum(dconv * y1, axis=0, keepdims=True)
        dw_ref[2:3, :] = jnp.sum(dconv * y, axis=0, keepdims=True)

    cb, cc, cx, wspec, out = _conv_specs(S)
    db, dc, dx, dw = pl.pallas_call(
        body, name="conv_bwd", grid=(BW // CONV_TC,), in_specs=[cb, cc, cx, wspec, out],
        out_specs=[out, out, out, wspec],
        out_shape=[jax.ShapeDtypeStruct((S, BW), BF16)] * 3 + [jax.ShapeDtypeStruct((3, BW), F32)],
        compiler_params=_cp("parallel"),
    )(p, p, p, conv_w, dyc)
    return db, dc, dx, dw


def _merge_specs(S, tm):
    gate = lambda n: pl.BlockSpec((tm, D_MODEL), lambda i: (i, C_GATES // D_MODEL + n))
    return [gate(0), gate(1), gate(2)], pl.BlockSpec((3, tm, D_MODEL), lambda i: (0, i, 0)), pl.BlockSpec((tm, D_MODEL), lambda i: (i, 0))


def _merge_fwd(p, bd):
    S = p.shape[0]
    tm = _tile(S, (256,))

    def body(g0, g1, g2, b_ref, o_ref):
        acc = jax.nn.sigmoid(g0[...]) * b_ref[0]
        acc = acc + jax.nn.sigmoid(g1[...]) * b_ref[1]
        acc = acc + jax.nn.sigmoid(g2[...]) * b_ref[2]
        o_ref[...] = acc.astype(BF16)

    gates, bspec, row = _merge_specs(S, tm)
    return pl.pallas_call(
        body, name="merge_fwd", grid=(S // tm,), in_specs=gates + [bspec], out_specs=row,
        out_shape=jax.ShapeDtypeStruct((S, D_MODEL), BF16), compiler_params=_cp("parallel"),
    )(p, p, p, bd)


def _merge_bwd(p, bd, dm):
    S = p.shape[0]
    tm = _tile(S, (256,))

    def body(g0, g1, g2, b_ref, dm_ref, db_ref, dg_ref):
        dmv = dm_ref[...]
        for n, g_ref in enumerate((g0, g1, g2)):
            sg = jax.nn.sigmoid(g_ref[...])
            db_ref[n] = (dmv * sg).astype(BF16)
            dg_ref[:, n * D_MODEL:(n + 1) * D_MODEL] = (dmv * b_ref[n] * (sg * (1.0 - sg))).astype(BF16)

    gates, bspec, row = _merge_specs(S, tm)
    return pl.pallas_call(
        body, name="merge_bwd", grid=(S // tm,), in_specs=gates + [bspec, row],
        out_specs=[bspec, pl.BlockSpec((tm, 3 * D_MODEL), lambda i: (i, 0))],
        out_shape=[jax.ShapeDtypeStruct((3, S, D_MODEL), BF16), jax.ShapeDtypeStruct((S, 3 * D_MODEL), BF16)],
        compiler_params=_cp("parallel"),
    )(p, p, p, bd, dm)


XA_TM = 512


def _xa_probs(qh, kh):
    s = lax.dot_general(qh, kh, NT, preferred_element_type=F32) * (XA_DH ** -0.5)
    e = jnp.exp(s - jnp.max(s, axis=-1, keepdims=True))
    return e / jnp.sum(e, axis=-1, keepdims=True)


def _xa_fwd(q, kv):
    S = q.shape[0]
    tm = min(XA_TM, S)
    M = kv.shape[1]

    def body(q_ref, kv_ref, o_ref):
        for h in range(XA_HEADS):
            cols = slice(h * XA_DH, (h + 1) * XA_DH)
            pr = _xa_probs(q_ref[:, cols], kv_ref[0, :, cols])
            o_ref[:, cols] = jnp.dot(pr.astype(BF16), kv_ref[1, :, cols], preferred_element_type=F32).astype(BF16)

    row = pl.BlockSpec((tm, D_MODEL), lambda i: (i, 0))
    return pl.pallas_call(
        body, name="xa_fwd", grid=(S // tm,), in_specs=[row, pl.BlockSpec((2, M, D_MODEL), lambda i: (0, 0, 0))], out_specs=row,
        out_shape=jax.ShapeDtypeStruct((S, D_MODEL), BF16), compiler_params=_cp("parallel"),
    )(q, kv)


def _xa_bwd(q, kv, do):
    S = q.shape[0]
    tm = min(XA_TM, S)
    M = kv.shape[1]

    def body(q_ref, kv_ref, do_ref, dq_ref, dkv_ref):
        @pl.when(pl.program_id(0) == 0)
        def _():
            dkv_ref[...] = jnp.zeros_like(dkv_ref)

        for h in range(XA_HEADS):
            cols = slice(h * XA_DH, (h + 1) * XA_DH)
            qh, kh, vh, doh = q_ref[:, cols], kv_ref[0, :, cols], kv_ref[1, :, cols], do_ref[:, cols]
            pr = _xa_probs(qh, kh)
            dkv_ref[1, :, cols] += lax.dot_general(pr.astype(BF16), doh, TN, preferred_element_type=F32)
            dp = lax.dot_general(doh, vh, NT, preferred_element_type=F32)
            ds = (pr * (dp - jnp.sum(dp * pr, axis=-1, keepdims=True)) * (XA_DH ** -0.5)).astype(BF16)
            dq_ref[:, cols] = jnp.dot(ds, kh, preferred_element_type=F32).astype(BF16)
            dkv_ref[0, :, cols] += lax.dot_general(ds, qh, TN, preferred_element_type=F32)

    row = pl.BlockSpec((tm, D_MODEL), lambda i: (i, 0))
    kvs = pl.BlockSpec((2, M, D_MODEL), lambda i: (0, 0, 0))
    return pl.pallas_call(
        body, name="xa_bwd", grid=(S // tm,), in_specs=[row, kvs, row], out_specs=[row, kvs],
        out_shape=[jax.ShapeDtypeStruct((S, D_MODEL), BF16), jax.ShapeDtypeStruct((2, M, D_MODEL), F32)],
        compiler_params=_cp("arbitrary"),
    )(q, kv, do)


def _swiglu_fwd(ab):
    nb, _, S, C = ab.shape
    tm = _tile(S, (512, 256))

    def body(ab_ref, o_ref):
        a = ab_ref[0]
        o_ref[...] = (a * jax.nn.sigmoid(a) * ab_ref[1]).astype(BF16)

    return pl.pallas_call(
        body, name="swiglu_fwd", grid=(nb, S // tm),
        in_specs=[pl.BlockSpec((None, 2, tm, C), lambda j, i: (j, 0, i, 0))], out_specs=pl.BlockSpec((None, tm, C), lambda j, i: (j, i, 0)),
        out_shape=jax.ShapeDtypeStruct((nb, S, C), BF16), compiler_params=_cp("parallel", "parallel"),
    )(ab)


def _swiglu_bwd(ab, dh):
    nb, _, S, C = ab.shape
    tm = _tile(S, (512, 256))

    def body(ab_ref, dh_ref, o_ref):
        a, b, d = ab_ref[0], ab_ref[1], dh_ref[...].astype(F32)
        sg = jax.nn.sigmoid(a)
        o_ref[0] = (d * b * (sg * (1.0 + a * (1.0 - sg)))).astype(BF16)
        o_ref[1] = (d * (a * sg)).astype(BF16)

    pair = pl.BlockSpec((None, 2, tm, C), lambda j, i: (j, 0, i, 0))
    return pl.pallas_call(
        body, name="swiglu_bwd", grid=(nb, S // tm), in_specs=[pair, pl.BlockSpec((None, tm, C), lambda j, i: (j, i, 0))], out_specs=pair,
        out_shape=jax.ShapeDtypeStruct(ab.shape, BF16), compiler_params=_cp("parallel", "parallel"),
    )(ab, dh)


def _reduce_adam(parts, w, m, v, *, name):
    shape = w.shape
    C = shape[-1]
    R = math.prod(shape[:-1])
    tm = _rows(R, 4 * C)
    n = len(parts)
    c1, c2 = 1.0 - ADAM_B1 ** ADAM_STEP, 1.0 - ADAM_B2 ** ADAM_STEP

    def body(*refs):
        g = refs[0][...]
        for r in refs[1:n]:
            g = g + r[...]
        w_ref, m_ref, v_ref, go, do, mo, vo = refs[n:]
        mn = ADAM_B1 * m_ref[...] + (1.0 - ADAM_B1) * g
        vn = ADAM_B2 * v_ref[...] + (1.0 - ADAM_B2) * (g * g)
        go[...] = g
        do[...] = -ADAM_LR * ((mn / c1) / (jnp.sqrt(vn / c2) + ADAM_EPS) + ADAM_WD * w_ref[...])
        mo[...] = mn
        vo[...] = vn

    row = pl.BlockSpec((tm, C), lambda i: (i, 0))
    outs = pl.pallas_call(
        body, name=name, grid=(R // tm,), in_specs=[row] * (n + 3), out_specs=[row] * 4,
        out_shape=[jax.ShapeDtypeStruct((R, C), F32)] * 4, compiler_params=_cp("parallel"),
    )(*[a.reshape(R, C) for a in (*parts, w, m, v)])
    return tuple(o.reshape(shape) for o in outs)


_VIEW = {
    "w_in": ((1024, 7168), (1024, 1792), 256, lambda i, b: (i, b)),
    "w_br": ((1536, 1024), (1536, 256), 512, lambda i, b: (i, b)),
    "w_sq": ((5120, 1024), (1280, 1024), 256, lambda i, b: (4 * i + b, 0)),
    "w_gu": ((8192, 704), (2048, 704), 512, lambda i, b: (4 * b + i, 0)),
    "w_dn": ((2816, 1024), (704, 1024), 352, lambda i, b: (2 * b + i, 0)),
    "conv_w": ((8, 512), (8, 128), 8, lambda i, b: (0, b)),
}


def _scalar(v):
    return jnp.asarray(v, jnp.int32).reshape(1)


def _place(name, local, b):
    full2, sh2, tm, idx = _VIEW[name]
    C = sh2[1]
    dt = local.dtype if name == "conv_w" else BF16

    def body(b_ref, x_ref, o0_ref, o1_ref):
        o0_ref[...] = x_ref[0].astype(dt)
        o1_ref[...] = x_ref[1].astype(dt)

    place = pl.BlockSpec((tm, C), lambda i, bs: idx(i, bs[0]))
    outs = pl.pallas_call(
        body, name="place_" + name,
        grid_spec=pltpu.PrefetchScalarGridSpec(num_scalar_prefetch=1, grid=(sh2[0] // tm,),
                                               in_specs=[pl.BlockSpec((DEPTH, tm, C), lambda i, bs: (0, i, 0))], out_specs=[place, place]),
        out_shape=[jax.ShapeDtypeStruct(full2, dt)] * 2, compiler_params=_cp("arbitrary"),
    )(_scalar(b), local.reshape((DEPTH,) + sh2))
    return [o.reshape(_FULL_SHAPE[name]) for o in outs]


def _add_owner(name, g, land, own):
    shape = g.shape
    C = shape[-1]
    R = math.prod(shape[:-1])
    tm = _rows(R, 4 * C)

    def body(s_ref, g_ref, l_ref, o_ref):
        @pl.when(s_ref[0] != 0)
        def _():
            o_ref[...] = (g_ref[...] + l_ref[...]).astype(BF16)

        @pl.when(s_ref[0] == 0)
        def _():
            o_ref[...] = jnp.zeros_like(o_ref)

    pick = pl.BlockSpec((tm, C), lambda i, s: (jnp.where(s[0] != 0, i, 0), 0))
    return pl.pallas_call(
        body, name="presum_" + name,
        grid_spec=pltpu.PrefetchScalarGridSpec(num_scalar_prefetch=1, grid=(R // tm,), in_specs=[pick, pick],
                                               out_specs=pl.BlockSpec((tm, C), lambda i, s: (i, 0))),
        out_shape=jax.ShapeDtypeStruct((R, C), BF16), compiler_params=_cp("arbitrary"),
    )(_scalar(own), g.reshape(R, C), land.reshape(R, C)).reshape(shape)


def _sum_chips(name, slots, part, b, own):
    full2, sh2, tm, idx = _VIEW[name]
    C = sh2[1]

    def body(s_ref, slot_ref, own_ref, o_ref):
        @pl.when(s_ref[1] != 0)
        def _():
            o_ref[...] = ((slot_ref[0].astype(F32) + slot_ref[1].astype(F32)) + slot_ref[2].astype(F32)) + own_ref[...].astype(F32)

        @pl.when(s_ref[1] == 0)
        def _():
            o_ref[...] = jnp.zeros_like(o_ref)

    return pl.pallas_call(
        body, name="sum_chips_" + name,
        grid_spec=pltpu.PrefetchScalarGridSpec(
            num_scalar_prefetch=1, grid=(sh2[0] // tm,),
            in_specs=[pl.BlockSpec((3, tm, C), lambda i, s: (0, jnp.where(s[1] != 0, i, 0), 0)),
                      pl.BlockSpec((tm, C), lambda i, s: idx(jnp.where(s[1] != 0, i, 0), s[0]))],
            out_specs=pl.BlockSpec((tm, C), lambda i, s: (i, 0))),
        out_shape=jax.ShapeDtypeStruct(sh2, F32), compiler_params=_cp("arbitrary"),
    )(jnp.stack([jnp.asarray(b, jnp.int32), jnp.asarray(own, jnp.int32)]), slots.reshape((3,) + sh2),
      part.reshape(full2)).reshape(_SHARD_SHAPE[name])


def _adam_layers(mine, theirs, c, w, m, v, *, name):
    shape = w.shape
    C = shape[-1]
    R = math.prod(shape[1:-1])
    tm = _rows(R, 4 * C)
    c1, c2 = 1.0 - ADAM_B1 ** ADAM_STEP, 1.0 - ADAM_B2 ** ADAM_STEP

    def body(c_ref, m0_ref, m1_ref, t_ref, w_ref, m_ref, v_ref, go, do, mo, vo):
        layer = pl.program_id(0)
        g = jnp.where(layer == c_ref[0], jnp.where(layer == 0, m0_ref[...], m1_ref[...]), t_ref[...])
        mn = ADAM_B1 * m_ref[...] + (1.0 - ADAM_B1) * g
        vn = ADAM_B2 * v_ref[...] + (1.0 - ADAM_B2) * (g * g)
        go[...] = g
        do[...] = -ADAM_LR * ((mn / c1) / (jnp.sqrt(vn / c2) + ADAM_EPS) + ADAM_WD * w_ref[...])
        mo[...] = mn
        vo[...] = vn

    def own(layer):
        return pl.BlockSpec((tm, C), lambda l, i, cs: (jnp.where((l == layer) & (cs[0] == layer), i, 0), 0))

    recv = pl.BlockSpec((tm, C), lambda l, i, cs: (jnp.where(l == cs[0], 0, i), 0))
    row = pl.BlockSpec((None, tm, C), lambda l, i, cs: (l, i, 0))
    outs = pl.pallas_call(
        body, name=name,
        grid_spec=pltpu.PrefetchScalarGridSpec(num_scalar_prefetch=1, grid=(DEPTH, R // tm),
                                               in_specs=[own(0), own(1), recv, row, row, row], out_specs=[row] * 4),
        out_shape=[jax.ShapeDtypeStruct((DEPTH, R, C), F32)] * 4, compiler_params=_cp("arbitrary", "arbitrary"),
    )(_scalar(c), mine[0].reshape(R, C), mine[1].reshape(R, C), theirs.reshape(R, C), *[t.reshape(DEPTH, R, C) for t in (w, m, v)])
    return tuple(o.reshape(shape) for o in outs)


def _sum_slots(r, *, name):
    n, shape = r.shape[0], r.shape[1:]
    C = shape[-1]
    R = math.prod(shape[:-1])
    tm = _rows(R, 4 * C * n, 8 << 20)

    def body(r_ref, o_ref):
        acc = r_ref[0]
        for s in range(1, n):
            acc = acc + r_ref[s]
        o_ref[...] = acc

    return pl.pallas_call(
        body, name=name, grid=(R // tm,), in_specs=[pl.BlockSpec((n, tm, C), lambda i: (0, i, 0))],
        out_specs=pl.BlockSpec((tm, C), lambda i: (i, 0)), out_shape=jax.ShapeDtypeStruct((R, C), F32), compiler_params=_cp("parallel"),
    )(r.reshape(n, R, C)).reshape(shape)


def _take_weights(wl, names, landed):
    for n, t in zip(names, landed):
        wl[n] = t[:3] if n == "conv_w" else t.reshape(2 * N_CHIPS, D_MODEL, FFN_SH) if n == "w_gu" else t


_GATHER_LATE = ["w_br", "conv_w", "w_sq"]
_GATHER_LAST = ["w_gu", "w_dn"]


def _layer_fwd(x, mem, wl, ride=None):
    S = x.shape[0]
    sv = {"x": x}
    wl = dict(wl)
    h1 = _rms_fwd(x, wl["norm_mix_g"], name="rms_mix")
    if ride is None:
        ride = {"mm_in": None, "sb": None, "mm_gu": None}
        p = _mm(h1, wl["w_in"], mode="nn", name="mm_in")
    else:
        p, landed = _mm(h1, wl["w_in"], mode="nn", name="mm_in", rider=ride["mm_in"])
        _take_weights(wl, _GATHER_LATE, landed)
    (ya, tot, cnt), landed = _sb2_fwd(p, ride["sb"])
    _take_weights(wl, _GATHER_LAST, landed)
    b_st = wl["b_spatial"].T
    yb = _sgu_fwd(p, wl["sgu_ln_g"], wl["sgu_ln_b"], wl["w_spatial"], b_st)
    yc = _conv_fwd(p, wl["conv_w"])
    br = jnp.stack([ya, yb, yc])
    bd = _mm(br, wl["w_br"], mode="nn", a_kind="batch", b_kind="batch", name="mm_branch")
    merged = _merge_fwd(p, bd)
    x1 = _mm(merged, wl["w_sq"][0], mode="nn", res=x, name="mm_out")
    h2 = _rms_fwd(x1, wl["norm_xa_g"], name="rms_xa")
    qx = _mm(h2, wl["w_sq"][1], mode="nn", out_dtype=BF16, name="mm_q")
    mn = _rms_fwd(mem, wl["mem_norm_g"], name="rms_mem")
    kv = _mm(mn, wl["w_sq"][3:5], mode="nn", b_kind="batch", out_dtype=BF16, name="mm_kv")
    o = _xa_fwd(qx, kv)
    x2 = _mm(o, wl["w_sq"][2], mode="nn", res=x1, name="mm_o")
    h3 = _rms_fwd(x2, wl["norm_ffn_g"], name="rms_ffn")
    if ride["mm_gu"] is None:
        ab, rode = _mm(h3, wl["w_gu"], mode="nn", b_kind="batch", name="mm_gu"), []
    else:
        ab, rode = _mm(h3, wl["w_gu"], mode="nn", b_kind="batch", name="mm_gu", rider=ride["mm_gu"])
    ab = ab.reshape(N_CHIPS, 2, S, FFN_SH)
    hh = _swiglu_fwd(ab)
    x3 = _mm(hh, wl["w_dn"], mode="nn", a_kind="kchunk", b_kind="kchunk", res=x2, name="mm_down")
    sv.update(h1=h1, p=p, tot=tot, cnt=cnt, br=br, bd=bd, merged=merged, x1=x1, h2=h2, qx=qx, mn=mn, kv=kv, o=o,
              x2=x2, h3=h3, ab=ab, hh=hh, b_st=b_st, wl=wl)
    return x3, sv, rode


class _GradPipe:
    def __init__(self, ci, bi):
        self.ci, self.bi, self.queue = ci, bi, []
        self.part, self.slots = [dict(), dict()], [dict(), dict()]

    def to_owner(self, layer, names, g):
        def arrived(land):
            own = (self.ci == layer).astype(jnp.int32)
            part = {n: _add_owner(n, g[n], t, own) for n, t in zip(names, land)}
            self.part[layer].update(part)
            self.queue.append((layer, names, part))

        return _presum_rider(layer, {n: g[n] for n in names}), arrived

    def exchange(self):
        if not self.queue:
            return None
        layer, names, part = self.queue.pop(0)
        return _shard_rider(layer, part), lambda slots: self.slots[layer].update(zip(names, slots))

    def drain(self):
        while self.queue:
            rider, store = self.exchange()
            store(_run_rider(rider, name="grad_shard_exchange_last"))

    def reduced(self, layer):
        own = (self.ci == layer).astype(jnp.int32)
        return {n: _sum_chips(n, self.slots[layer][n], self.part[layer][n], self.bi, own) for n in _BIG}


def _layer_bwd(dx3, mem, sv, layer=None, pipe=None):
    S = dx3.shape[0]
    p, wl = sv["p"], sv["wl"]
    g = {}

    def mm(*args, job=None, **kw):
        if job is None:
            return _mm(*args, **kw)
        out, landed = _mm(*args, **kw, rider=job[0])
        job[1](landed)
        return out

    to_owner = (lambda names: pipe.to_owner(layer, names, g)) if pipe else (lambda names: None)
    exchange = pipe.exchange if pipe else (lambda: None)

    dhh = _mm(dx3, wl["w_dn"], mode="nt", b_kind="batch", name="mm_down_dx")
    g["w_dn"] = _mm(sv["hh"], dx3, mode="tn", a_kind="batch", name="mm_down_dw")
    dab = _swiglu_bwd(sv["ab"], dhh).reshape(2 * N_CHIPS, S, FFN_SH)
    g["w_gu"] = mm(sv["h3"], dab, mode="tn", b_kind="batch", name="mm_gu_dw", job=exchange()).reshape(_FULL_SHAPE["w_gu"])
    dh3 = mm(dab, wl["w_gu"], mode="nt", a_kind="kchunk", b_kind="kchunk", name="mm_gu_dx", job=to_owner(["w_dn", "w_gu"]))
    dx2, g["norm_ffn_g"] = _rms_bwd(sv["x2"], wl["norm_ffn_g"], dh3, dx3, name="rms_ffn_bwd")
    do = _mm(dx2, wl["w_sq"][2], mode="nt", out_dtype=BF16, name="mm_o_dx")
    dw_o = _mm(sv["o"], dx2, mode="tn", name="mm_o_dw")
    dq, dkv = _xa_bwd(sv["qx"], sv["kv"], do)
    dw_q = _mm(sv["h2"], dq, mode="tn", name="mm_q_dw")
    dh2 = _mm(dq, wl["w_sq"][1], mode="nt", name="mm_q_dx")
    dw_kv = _mm(sv["mn"], dkv, mode="tn", b_kind="batch", name="mm_kv_dw")
    dmn = _mm(dkv, wl["w_sq"][3:5], mode="nt", a_kind="kchunk", b_kind="kchunk", name="mm_kv_dx")
    _, g["mem_norm_g"] = _rms_bwd(mem, wl["mem_norm_g"], dmn, jnp.zeros_like(mem), name="rms_mem_bwd")
    dx1, g["norm_xa_g"] = _rms_bwd(sv["x1"], wl["norm_xa_g"], dh2, dx2, name="rms_xa_bwd")
    dm = _mm(dx1, wl["w_sq"][0], mode="nt", name="mm_out_dx")
    dw_out = _mm(sv["merged"], dx1, mode="tn", name="mm_out_dw")
    g["w_sq"] = jnp.concatenate([jnp.stack([dw_out, dw_q, dw_o]), dw_kv])
    dbd, dgates = _merge_bwd(p, sv["bd"], dm)
    dbr = mm(dbd, wl["w_br"], mode="nt", a_kind="batch", b_kind="batch", name="mm_branch_dx", job=to_owner(["w_sq"]))
    g["w_br"] = _mm(sv["br"], dbd, mode="tn", a_kind="batch", b_kind="batch", name="mm_branch_dw")
    job = exchange()
    (dq, dk, dv), landed = _sb2_bwd(p, dbr, sv["tot"], sv["cnt"], job[0] if job else None)
    if job:
        job[1](landed)
    dz, g["sgu_ln_g"], g["sgu_ln_b"], g["w_spatial"], g["b_spatial"] = _sgu_bwd(
        p, dbr[1], wl["sgu_ln_g"], wl["sgu_ln_b"], wl["w_spatial"], wl["w_spatial"].transpose(0, 2, 1), sv["b_st"])
    dcb, dcc, dcx, g["conv_w"] = _conv_bwd(p, wl["conv_w"], dbr[2])
    dp = jnp.concatenate([dq, dk.astype(BF16), dv.astype(BF16), dz, dcb, dcc, dcx, dgates], axis=1)
    g["w_in"] = mm(sv["h1"], dp, mode="tn", name="mm_in_dw", job=exchange())
    dh1 = mm(dp, wl["w_in"], mode="nt", name="mm_in_dx", job=to_owner(["w_br", "w_in"]))
    dx, g["norm_mix_g"] = _rms_bwd(sv["x"], wl["norm_mix_g"], dh1, dx1, name="rms_mix_bwd")
    return dx, g


def _local_step(x, mem, target, layers, final_g):
    h, saved = x, []
    for wl in layers:
        h, sv, _ = _layer_fwd(h, mem, wl)
        saved.append(sv)
    loss, dx, d_final = _loss_head(h, final_g, target)
    grads = [None] * len(layers)
    for l in reversed(range(len(layers))):
        dx, grads[l] = _layer_bwd(dx, mem, saved[l])
    return loss, dx, grads, d_final


_ALL = slice(None)
CONV_ROWS = 8
_SHARD = {
    "w_in": lambda b: (_ALL, pl.ds(1792 * b, 1792)),
    "w_br": lambda b: (_ALL, _ALL, pl.ds(256 * b, 256)),
    "w_sq": lambda b: (_ALL, pl.ds(256 * b, 256), _ALL),
    "w_gu": lambda b: (b,),
    "w_dn": lambda b: (b,),
    "conv_w": lambda b: (_ALL, pl.ds(128 * b, 128)),
}
_FULL_SHAPE = {"w_in": (1024, 7168), "w_br": (3, 512, 1024), "w_sq": (5, 1024, 1024), "w_gu": (4, 2, 1024, 704),
               "w_dn": (4, 704, 1024), "conv_w": (CONV_ROWS, 512)}
_SHARD_SHAPE = {"w_in": (1024, 1792), "w_br": (3, 512, 256), "w_sq": (5, 256, 1024), "w_gu": (2, 1024, 704),
                "w_dn": (704, 1024), "conv_w": (CONV_ROWS, 128)}


def _pos():
    return lax.axis_index("x"), lax.axis_index("y"), lax.axis_index("c")


def _per_chip(fn):
    x, y, _ = _pos()
    for x0 in (0, 1):
        for y0 in (0, 1):
            @pl.when((x == x0) & (y == y0))
            def _():
                fn(x0, y0)


def _other_chips(x0, y0):
    return [(1 - x0, y0), (x0, 1 - y0), (1 - x0, 1 - y0)]


def _rcopy(src, dst, ssem, rsem, dev):
    return pltpu.make_async_remote_copy(src_ref=src, dst_ref=dst, send_sem=ssem, recv_sem=rsem, device_id=dev, device_id_type=MESH)


def _dma_sems(n):
    return pltpu.SemaphoreType.DMA((n,))


def _gather_rider(layer, placed):
    names = list(placed)
    n = len(names)
    shard = lambda refs, a, b: refs[a].at[_SHARD[names[a]](b)]

    def start(ins, outs, send, recv):
        @pl.when(lax.axis_index("c") == layer)
        def _():
            def run(x0, y0):
                for kk, (px, py) in enumerate(_other_chips(x0, y0)):
                    for a in range(n):
                        own = shard(outs, a, 2 * x0 + y0)
                        _rcopy(own, own, send.at[6 * a + kk], recv.at[6 * a + kk], (px, py, layer)).start()

            _per_chip(run)

    def passing(outs, send, recv, a, kk, bp, x0, y0):
        landed = shard(outs, a, bp)
        return _rcopy(landed, landed, send.at[6 * a + 3 + kk], recv.at[6 * a + 3 + kk], (x0, y0, 1 - layer))

    def middle(ins, outs, send, recv):
        @pl.when(lax.axis_index("c") == layer)
        def _():
            def run(x0, y0):
                for kk, (px, py) in enumerate(_other_chips(x0, y0)):
                    for a in range(n):
                        landed = shard(outs, a, 2 * px + py)
                        _rcopy(landed, landed, send.at[6 * a + kk], recv.at[6 * a + kk], (px, py, layer)).wait_recv()
                        passing(outs, send, recv, a, kk, 2 * px + py, x0, y0).start()

            _per_chip(run)

    def finish(ins, outs, send, recv):
        c = lax.axis_index("c")

        def run(x0, y0):
            chips = _other_chips(x0, y0)

            @pl.when(c == layer)
            def _():
                for kk, (px, py) in enumerate(chips):
                    for a in range(n):
                        own = shard(outs, a, 2 * x0 + y0)
                        _rcopy(own, own, send.at[6 * a + kk], recv.at[6 * a + kk], (px, py, layer)).wait_send()
                        passing(outs, send, recv, a, kk, 2 * px + py, x0, y0).wait_send()

            @pl.when(c != layer)
            def _():
                for kk, (px, py) in enumerate(chips):
                    for a in range(n):
                        got = shard(outs, a, 2 * px + py)
                        _rcopy(got, got, send.at[6 * a + 3 + kk], recv.at[6 * a + 3 + kk], (x0, y0, layer)).wait_recv()

        _per_chip(run)

    arrs = [placed[nm] for nm in names]
    return _Rider(arrs, [jax.ShapeDtypeStruct(t.shape, t.dtype) for t in arrs], 6 * n, start, finish, alias={a: a for a in range(n)}, middle=middle)


def _presum_rider(layer, grads):
    names = list(grads)
    n = len(names)

    def start(ins, outs, send, recv):
        x, y, c = _pos()

        @pl.when(c != layer)
        def _():
            for a in range(n):
                _rcopy(ins[a], outs[a], send.at[a], recv.at[a], (x, y, layer)).start()

    def finish(ins, outs, send, recv):
        x, y, c = _pos()

        @pl.when(c != layer)
        def _():
            for a in range(n):
                _rcopy(ins[a], outs[a], send.at[a], recv.at[a], (x, y, layer)).wait_send()

        @pl.when(c == layer)
        def _():
            for a in range(n):
                _rcopy(outs[a], outs[a], send.at[a], recv.at[a], (x, y, 1 - layer)).wait_recv()

    arrs = [grads[nm] for nm in names]
    return _Rider(arrs, [jax.ShapeDtypeStruct(t.shape, t.dtype) for t in arrs], n, start, finish)


def _shard_rider(layer, part):
    names = list(part)
    n = len(names)

    def each(fn):
        @pl.when(lax.axis_index("c") == layer)
        def _():
            def run(x0, y0):
                for kk, (px, py) in enumerate(_other_chips(x0, y0)):
                    for a in range(n):
                        fn(a, kk, 2 * px + py, (px, py, layer))

            _per_chip(run)

    def start(ins, outs, send, recv):
        each(lambda a, kk, bp, peer: _rcopy(ins[a].at[_SHARD[names[a]](bp)], outs[a].at[kk], send.at[3 * a + kk], recv.at[3 * a + kk], peer).start())

    def finish(ins, outs, send, recv):
        each(lambda a, kk, bp, peer: _rcopy(outs[a].at[kk], outs[a].at[kk], send.at[3 * a + kk], recv.at[3 * a + kk], peer).wait_recv())
        each(lambda a, kk, bp, peer: _rcopy(ins[a].at[_SHARD[names[a]](bp)], outs[a].at[kk], send.at[3 * a + kk], recv.at[3 * a + kk], peer).wait_send())

    return _Rider([part[nm] for nm in names], [jax.ShapeDtypeStruct((N_CHIPS - 1,) + _SHARD_SHAPE[nm], part[nm].dtype) for nm in names],
                  3 * n, start, finish)


def _sibling_exchange(mine0, mine1):
    names = list(mine0)
    n = len(names)

    def body(*refs):
        l0, l1, outs = refs[:n], refs[n:2 * n], refs[2 * n:3 * n]
        send, recv = refs[3 * n:]
        x, y, c = _pos()
        for c0 in (0, 1):
            @pl.when(c == c0)
            def _():
                srcs = l0 if c0 == 0 else l1
                cps = [_rcopy(srcs[a], outs[a], send.at[a], recv.at[a], (x, y, 1 - c0)) for a in range(n)]
                for cp in cps:
                    cp.start()
                for cp in cps:
                    cp.wait()

    outs = pl.pallas_call(
        body, name="grad_sibling_exchange", in_specs=[ANY] * (2 * n), out_specs=[ANY] * n,
        out_shape=[jax.ShapeDtypeStruct(mine0[nm].shape, mine0[nm].dtype) for nm in names],
        scratch_shapes=[_dma_sems(n), _dma_sems(n)],
    )(*[mine0[nm] for nm in names], *[mine1[nm] for nm in names])
    return dict(zip(names, outs))


def _gather_small(pack):
    flips = [(fx, fy, fc) for fx in (0, 1) for fy in (0, 1) for fc in (0, 1) if fx or fy or fc]

    def body(in_ref, out_ref, send, recv, lsem):
        x, y, c = _pos()
        me = 4 * x + 2 * y + c
        own = pltpu.make_async_copy(in_ref, out_ref.at[me], lsem.at[0])
        own.start()
        peers = [(x ^ fx, y ^ fy, c ^ fc) for fx, fy, fc in flips]
        cps = [_rcopy(in_ref, out_ref.at[me], send.at[k], recv.at[k], peer) for k, peer in enumerate(peers)]
        for cp in cps:
            cp.start()
        for k, (px, py, pc) in enumerate(peers):
            slot = out_ref.at[4 * px + 2 * py + pc]
            _rcopy(slot, slot, send.at[k], recv.at[k], (px, py, pc)).wait_recv()
        for cp in cps:
            cp.wait_send()
        own.wait()

    return pl.pallas_call(
        body, name="gather_small_grads", in_specs=[ANY], out_specs=ANY,
        out_shape=jax.ShapeDtypeStruct((8,) + pack.shape, pack.dtype),
        scratch_shapes=[_dma_sems(len(flips)), _dma_sems(len(flips)), _dma_sems(1)],
    )(pack)


_WEIGHTS = ["norm_mix_g", "w_in", "sgu_ln_g", "sgu_ln_b", "w_spatial", "b_spatial", "conv_w", "w_branch", "w_out", "norm_xa_g",
            "mem_norm_g", "w_q_xa", "w_k_xa", "w_v_xa", "w_o_xa", "norm_ffn_g", "w_gate_ffn", "w_up_ffn", "w_down_ffn", "final_g"]
_REPLICATED = ["norm_mix_g", "sgu_ln_g", "sgu_ln_b", "w_spatial", "b_spatial", "norm_xa_g", "mem_norm_g", "norm_ffn_g", "final_g"]
_SQUARE = ["w_out", "w_q_xa", "w_o_xa", "w_k_xa", "w_v_xa"]
_BIG = ["w_in", "w_br", "w_sq", "w_gu", "w_dn"]


def _pack(arrs):
    return jnp.concatenate([a.reshape(-1) for a in arrs]).reshape(-1, 128)


def _step(a):
    w = {n: a[n] for n in _WEIGHTS}
    x, mem, target = a["x"][0], a["mem"][0], a["loss_target"][0]
    xi, yi, ci = _pos()
    bi = 2 * xi + yi
    groups = list(_FULL_SHAPE)

    local = {"w_in": w["w_in"], "w_br": w["w_branch"], "w_sq": jnp.stack([w[n] for n in _SQUARE], axis=1),
             "w_gu": jnp.stack([w["w_gate_ffn"], w["w_up_ffn"]], axis=1), "w_dn": w["w_down_ffn"],
             "conv_w": jnp.pad(w["conv_w"], ((0, 0), (0, CONV_ROWS - 3), (0, 0)))}
    placed = [dict(), dict()]
    for n in groups:
        placed[0][n], placed[1][n] = _place(n, local[n], bi)

    def gather(l, names):
        return _gather_rider(l, {n: placed[l][n] for n in names})

    def start_of(l, w_in):
        return {"w_in": w_in, **{n: w[n][l] for n in _REPLICATED if n != "final_g"}}

    w_in0, = _run_rider(gather(0, ["w_in"]), name="gather_first")
    h, sv0, (w_in1,) = _layer_fwd(x, mem, start_of(0, w_in0), {"mm_in": gather(0, _GATHER_LATE), "sb": gather(0, _GATHER_LAST),
                                                                "mm_gu": gather(1, ["w_in"])})
    h, sv1, _ = _layer_fwd(h, mem, start_of(1, w_in1), {"mm_in": gather(1, _GATHER_LATE), "sb": gather(1, _GATHER_LAST), "mm_gu": None})
    loss, dx, d_final = _loss_head(h, w["final_g"], target)
    loss = lax.psum(loss, ("x", "y", "c"))

    pipe = _GradPipe(ci, bi)
    dx, g1 = _layer_bwd(dx, mem, sv1, 1, pipe)
    dx, g0 = _layer_bwd(dx, mem, sv0, 0, pipe)
    pipe.drain()
    grads = [g0, g1]
    mine = [pipe.reduced(0), pipe.reduced(1)]
    theirs = _sibling_exchange(mine[0], mine[1])

    out = {}

    def adam_layers(name, group, pick=None):
        sel = (lambda t: t[group]) if pick is None else (lambda t: t[group][pick])
        out[name] = _adam_layers([sel(mine[0]), sel(mine[1])], sel(theirs), ci, w[name], a["m_" + name], a["v_" + name], name="adam_" + name)

    adam_layers("w_in", "w_in")
    adam_layers("w_branch", "w_br")
    for t, n in enumerate(_SQUARE):
        adam_layers(n, "w_sq", t)
    adam_layers("w_gate_ffn", "w_gu", 0)
    adam_layers("w_up_ffn", "w_gu", 1)
    adam_layers("w_down_ffn", "w_dn")

    def adam(name, g):
        out[name] = _reduce_adam([g], w[name], a["m_" + name], a["v_" + name], name="adam_" + name)

    small = {n: jnp.stack([g[n] for g in grads]) for n in _REPLICATED if n != "final_g"}
    small["final_g"] = d_final
    conv_g = jnp.stack([g["conv_w"] for g in grads])
    n_rep = sum(w[n].size for n in _REPLICATED) // 128
    summed = _sum_slots(_gather_small(_pack([small[n] for n in _REPLICATED] + [conv_g])), name="sum_devices_small")
    res = _reduce_adam([summed[:n_rep]], _pack([w[n] for n in _REPLICATED]), _pack([a["m_" + n] for n in _REPLICATED]),
                       _pack([a["v_" + n] for n in _REPLICATED]), name="adam_replicated")
    off = 0
    for n in _REPLICATED:
        out[n] = tuple(r.reshape(-1)[off:off + w[n].size].reshape(w[n].shape) for r in res)
        off += w[n].size
    conv_full = summed[n_rep:].reshape(conv_g.shape)
    adam("conv_w", lax.dynamic_slice_in_dim(conv_full, (2 * xi + yi) * 128, 128, axis=2))

    return (loss, dx[None], *[out[n][k] for k in range(4) for n in _WEIGHTS])


def kernel(x, mem, norm_mix_g, w_in, sgu_ln_g, sgu_ln_b, w_spatial, b_spatial, conv_w, w_branch, w_out, norm_xa_g, mem_norm_g, w_q_xa, w_k_xa, w_v_xa, w_o_xa, norm_ffn_g, w_gate_ffn, w_up_ffn, w_down_ffn, final_g, loss_target, m_norm_mix_g, m_w_in, m_sgu_ln_g, m_sgu_ln_b, m_w_spatial, m_b_spatial, m_conv_w, m_w_branch, m_w_out, m_norm_xa_g, m_mem_norm_g, m_w_q_xa, m_w_k_xa, m_w_v_xa, m_w_o_xa, m_norm_ffn_g, m_w_gate_ffn, m_w_up_ffn, m_w_down_ffn, m_final_g, v_norm_mix_g, v_w_in, v_sgu_ln_g, v_sgu_ln_b, v_w_spatial, v_b_spatial, v_conv_w, v_w_branch, v_w_out, v_norm_xa_g, v_mem_norm_g, v_w_q_xa, v_w_k_xa, v_w_v_xa, v_w_o_xa, v_norm_ffn_g, v_w_gate_ffn, v_w_up_ffn, v_w_down_ffn, v_final_g):
    return _step(dict(locals()))
```

```python
import functools
import math

import jax
import jax.numpy as jnp
from jax import lax
from jax.experimental import pallas as pl
from jax.experimental.pallas import tpu as pltpu

F32, BF16 = jnp.float32, jnp.bfloat16
MESH = pl.DeviceIdType.MESH
ANY = pl.BlockSpec(memory_space=pl.ANY)

D_MODEL = 1024
DEPTH = 2
BW = 512
SB_HEADS, SB_DH = 8, 64
SGU_LEN, SGU_GROUPS, SGU_GD, SGU_CHUNK = 128, 4, 128, 64
XA_HEADS, XA_DH = 4, 256
FFN_SH = 704
N_CHIPS = 4
IN_COLS = 7168
C_Z, C_CB, C_GATES = 1536, 2560, 4096

ADAM_LR, ADAM_B1, ADAM_B2, ADAM_EPS, ADAM_WD, ADAM_STEP = 0.001, 0.9, 0.999, 1e-08, 0.01, 10

VMEM_LIMIT_V7X = 56 * 1024 * 1024

NN = (((1,), (0,)), ((), ()))
NT = (((1,), (1,)), ((), ()))
TN = (((0,), (0,)), ((), ()))


def _cp(*sem):
    return pltpu.CompilerParams(dimension_semantics=sem, vmem_limit_bytes=VMEM_LIMIT_V7X)


def _tile(n, pref):
    for t in pref:
        if n % t == 0:
            return t
    return n


def _rows(r, row_bytes, block_bytes=1 << 20):
    fits = [t for t in range(8, r + 1, 8) if r % t == 0 and t * row_bytes <= block_bytes]
    return max(fits) if fits else r


class _Rider:
    def __init__(self, ins, outs, n_sems, start, finish, alias=None, middle=None):
        self.ins, self.outs, self.n_sems, self.alias = list(ins), list(outs), n_sems, alias or {}
        self.start, self.middle, self.finish = start, middle, finish


class _Sems:
    def __init__(self, ref, first):
        self.ref, self.first = ref, first

    @property
    def at(self):
        return self

    def __getitem__(self, k):
        return self.ref.at[self.first + k]


def _join(riders):
    riders = [r for r in riders if r is not None]
    if not riders:
        return None
    spans, i0, o0, s0 = [], 0, 0, 0
    for r in riders:
        spans.append((r, i0, o0, s0))
        i0, o0, s0 = i0 + len(r.ins), o0 + len(r.outs), s0 + r.n_sems

    def phase(which):
        def run(ins, outs, send, recv):
            for r, i, o, s in spans:
                fn = getattr(r, which)
                if fn is not None:
                    fn(ins[i:i + len(r.ins)], outs[o:o + len(r.outs)], _Sems(send, s), _Sems(recv, s))
        return run

    joined = _Rider([a for r in riders for a in r.ins], [a for r in riders for a in r.outs], s0, phase("start"), phase("finish"),
                    alias={o + k: i + v for r, i, o, s in spans for k, v in r.alias.items()}, middle=phase("middle"))
    joined.split = lambda landed: [list(landed[o:o + len(r.outs)]) for r, i, o, s in spans]
    return joined


def _call_with_rider(rider, body, *, name, grid, in_specs, args, out_specs, out_shape, scratch_shapes, semantics):
    if rider is None:
        return pl.pallas_call(body, name=name, grid=grid, in_specs=in_specs, out_specs=out_specs, out_shape=out_shape,
                              scratch_shapes=scratch_shapes, compiler_params=_cp(*semantics))(*args)
    n_in, n_out, r_in, r_out = len(args), len(out_shape), len(rider.ins), len(rider.outs)

    def riding(*refs):
        ins, rins = refs[:n_in], refs[n_in:n_in + r_in]
        outs, routs = refs[n_in + r_in:n_in + r_in + n_out], refs[n_in + r_in + n_out:n_in + r_in + n_out + r_out]
        rest = refs[n_in + r_in + n_out + r_out:]
        scratch, send, recv = rest[:-2], rest[-2], rest[-1]
        step = pl.program_id(0)
        for ax in range(1, len(grid)):
            step = step * grid[ax] + pl.program_id(ax)
        n_steps = math.prod(grid)

        @pl.when(step == 0)
        def _():
            rider.start(rins, routs, send, recv)

        body(*ins, *outs, *scratch)

        if rider.middle is not None:
            @pl.when(step == (3 * n_steps) // 5)
            def _():
                rider.middle(rins, routs, send, recv)

        @pl.when(step == n_steps - 1)
        def _():
            rider.finish(rins, routs, send, recv)

    return pl.pallas_call(
        riding, name=name, grid=grid, in_specs=list(in_specs) + [ANY] * r_in, out_specs=list(out_specs) + [ANY] * r_out,
        out_shape=list(out_shape) + rider.outs, scratch_shapes=list(scratch_shapes) + [_dma_sems(rider.n_sems), _dma_sems(rider.n_sems)],
        input_output_aliases={n_in + i: n_out + o for o, i in rider.alias.items()},
        compiler_params=_cp(*["arbitrary"] * len(grid)),
    )(*args, *rider.ins)


def _run_rider(rider, *, name):
    def nothing(*refs):
        pass

    return _call_with_rider(rider, nothing, name=name, grid=(1,), in_specs=[], args=[], out_specs=[], out_shape=[], scratch_shapes=[],
                            semantics=("arbitrary",))


def _mm(a, b, *, mode, name, out_dtype=F32, res=None, a_kind="2d", b_kind="2d", tm=None, tn=None, tk=None, rider=None):
    a2, b2 = a.shape[-2:], b.shape[-2:]
    if mode == "nn":
        (M, K), N = a2, b2[1]
    elif mode == "nt":
        (M, K), N = a2, b2[0]
    else:
        (K, M), N = a2, b2[1]
    kchunk = a_kind == "kchunk" or b_kind == "kchunk"
    batch = a_kind == "batch" or b_kind == "batch"
    G = (a.shape[0] if a_kind == "batch" else b.shape[0]) if batch else 1
    tm = tm or _tile(M, (1024, 512, 256, 128))
    tn = tn or _tile(N, (1024, 512, 256, 128))
    if kchunk:
        tk, nk = K, (a.shape[0] if a_kind == "kchunk" else b.shape[0])
    else:
        tk = tk or _tile(K, (1024, 512, 256, 128))
        nk = K // tk

    def spec(kind, blk, idx):
        if kind == "2d":
            return pl.BlockSpec(blk, lambda g, i, j, k: idx(g, i, j, k))
        if kind == "batch":
            return pl.BlockSpec((None,) + blk, lambda g, i, j, k: (g,) + idx(g, i, j, k))
        return pl.BlockSpec((None,) + blk, lambda g, i, j, k: (k,) + idx(g, i, j, 0))

    if mode == "nn":
        a_spec = spec(a_kind, (tm, tk), lambda g, i, j, k: (i, k))
        b_spec = spec(b_kind, (tk, tn), lambda g, i, j, k: (k, j))
    elif mode == "nt":
        a_spec = spec(a_kind, (tm, tk), lambda g, i, j, k: (i, k))
        b_spec = spec(b_kind, (tn, tk), lambda g, i, j, k: (j, k))
    else:
        a_spec = spec(a_kind, (tk, tm), lambda g, i, j, k: (k, i))
        b_spec = spec(b_kind, (tk, tn), lambda g, i, j, k: (k, j))
    o_kind = "batch" if batch else "2d"
    o_spec = spec(o_kind, (tm, tn), lambda g, i, j, k: (i, j))
    o_shape = ((G,) if batch else ()) + (M, N)
    dn = {"nn": NN, "nt": NT, "tn": TN}[mode]
    has_res = res is not None

    def body(*refs):
        if has_res:
            a_ref, b_ref, r_ref, o_ref = refs[:4]
        else:
            a_ref, b_ref, o_ref = refs[:3]
        p = lax.dot_general(a_ref[...].astype(BF16), b_ref[...].astype(BF16), dn, preferred_element_type=F32)

        def finish(r):
            if has_res:
                r = r + r_ref[...]
            o_ref[...] = r.astype(out_dtype)

        if nk == 1:
            finish(p)
        else:
            acc = refs[-1]
            k = pl.program_id(3)

            @pl.when(k == 0)
            def _():
                acc[...] = p

            @pl.when(k > 0)
            def _():
                acc[...] += p

            @pl.when(k == nk - 1)
            def _():
                finish(acc[...])

    in_specs, args = [a_spec, b_spec], [a, b]
    if has_res:
        in_specs.append(spec("2d", (tm, tn), lambda g, i, j, k: (i, j)))
        args.append(res)
    outs = _call_with_rider(
        rider, body, name=name, grid=(G, M // tm, N // tn, nk), in_specs=in_specs, args=args, out_specs=[o_spec],
        out_shape=[jax.ShapeDtypeStruct(o_shape, out_dtype)], scratch_shapes=[pltpu.VMEM((tm, tn), F32)] if nk > 1 else [],
        semantics=("parallel", "parallel", "parallel", "arbitrary"))
    return outs[0] if rider is None else (outs[0], outs[1:])


def _rms_fwd(x, g, *, name):
    S, Dm = x.shape
    tm = _tile(S, (512, 256))

    def body(x_ref, g_ref, o_ref):
        xv = x_ref[...]
        r = lax.rsqrt(jnp.mean(xv * xv, axis=-1, keepdims=True) + 1e-6)
        o_ref[...] = (xv * r * g_ref[...]).astype(BF16)

    return pl.pallas_call(
        body, name=name, grid=(S // tm,),
        in_specs=[pl.BlockSpec((tm, Dm), lambda i: (i, 0)), pl.BlockSpec((1, Dm), lambda i: (0, 0))],
        out_specs=pl.BlockSpec((tm, Dm), lambda i: (i, 0)), out_shape=jax.ShapeDtypeStruct((S, Dm), BF16),
        compiler_params=_cp("parallel"),
    )(x, g.reshape(1, Dm))


def _rms_bwd(x, g, dh, dres, *, name):
    S, Dm = x.shape
    tm = _tile(S, (512, 256))

    def body(x_ref, g_ref, dh_ref, dr_ref, dx_ref, dg_ref):
        xv, dhv = x_ref[...], dh_ref[...].astype(F32)
        r = lax.rsqrt(jnp.mean(xv * xv, axis=-1, keepdims=True) + 1e-6)
        u = dhv * g_ref[...]
        s = jnp.sum(u * xv, axis=-1, keepdims=True)
        dx_ref[...] = dr_ref[...] + r * u - xv * ((r * r * r) * (s * (1.0 / Dm)))
        part = jnp.sum(dhv * (xv * r), axis=0, keepdims=True)

        @pl.when(pl.program_id(0) == 0)
        def _():
            dg_ref[...] = part

        @pl.when(pl.program_id(0) > 0)
        def _():
            dg_ref[...] += part

    row = pl.BlockSpec((tm, Dm), lambda i: (i, 0))
    vec = pl.BlockSpec((1, Dm), lambda i: (0, 0))
    dx, dg = pl.pallas_call(
        body, name=name, grid=(S // tm,), in_specs=[row, vec, row, row], out_specs=[row, vec],
        out_shape=[jax.ShapeDtypeStruct((S, Dm), F32), jax.ShapeDtypeStruct((1, Dm), F32)],
        compiler_params=_cp("arbitrary"),
    )(x, g.reshape(1, Dm), dh, dres)
    return dx, dg.reshape(Dm)


def _loss_head(x, g, target):
    S, Dm = x.shape
    tm = _tile(S, (512, 256))

    def body(x_ref, g_ref, t_ref, dx_ref, dg_ref, loss_ref):
        xv, gv = x_ref[...], g_ref[...]
        r = lax.rsqrt(jnp.mean(xv * xv, axis=-1, keepdims=True) + 1e-6)
        xn = xv * r
        err = xn * gv - t_ref[...]
        lpart = 0.5 * jnp.sum(jnp.mean(err * err, axis=-1, keepdims=True), axis=0, keepdims=True)
        dy = err * (1.0 / Dm)
        u = dy * gv
        s = jnp.sum(u * xv, axis=-1, keepdims=True)
        dx_ref[...] = r * u - xv * ((r * r * r) * (s * (1.0 / Dm)))
        part = jnp.sum(dy * xn, axis=0, keepdims=True)
        lslab = jnp.broadcast_to(lpart, (8, 128))

        @pl.when(pl.program_id(0) == 0)
        def _():
            dg_ref[...] = part
            loss_ref[...] = lslab

        @pl.when(pl.program_id(0) > 0)
        def _():
            dg_ref[...] += part
            loss_ref[...] += lslab

    row = pl.BlockSpec((tm, Dm), lambda i: (i, 0))
    vec = pl.BlockSpec((1, Dm), lambda i: (0, 0))
    dx, dg, loss = pl.pallas_call(
        body, name="loss_head", grid=(S // tm,), in_specs=[row, vec, row],
        out_specs=[row, vec, pl.BlockSpec((8, 128), lambda i: (0, 0))],
        out_shape=[jax.ShapeDtypeStruct((S, Dm), F32), jax.ShapeDtypeStruct((1, Dm), F32), jax.ShapeDtypeStruct((8, 128), F32)],
        compiler_params=_cp("arbitrary"),
    )(x, g.reshape(1, Dm), target)
    return loss[0, 0], dx, dg.reshape(Dm)


SB_TQ, SB_TK = 256, 256
SB_EXP_FLOOR = -104.0


def _split2(v):
    hi = v.astype(BF16)
    return jnp.concatenate([hi, (v - hi.astype(F32)).astype(BF16)], axis=1)


def _tri2(cmp):
    j = lax.broadcasted_iota(jnp.int32, (2 * SB_TK, SB_TK), 0) % SB_TK
    s = lax.broadcasted_iota(jnp.int32, (2 * SB_TK, SB_TK), 1)
    return cmp(j, s).astype(BF16)


def _sb_scores(qv, kb, k0, q0, tq):
    rows = qv.shape[0]
    z = lax.dot_general(qv, kb, NT, preferred_element_type=F32) * (SB_DH ** -0.5)
    t_pos = q0 + lax.broadcasted_iota(jnp.int32, (rows, SB_TK), 0) % tq
    s_pos = k0 + lax.broadcasted_iota(jnp.int32, (rows, SB_TK), 1)
    valid = s_pos < t_pos
    ls = jnp.minimum(z, 0.0) - jnp.log(1.0 + jnp.exp(-jnp.abs(z)))
    l1m = jnp.where(valid, ls - z, 0.0)
    return z, valid, ls, l1m


SB_PAIRS = SB_HEADS // 2
_Q_BLK, _K_BLK, _V_BLK = 0, SB_PAIRS, 2 * SB_PAIRS


def _wide(x):
    return x if SB_TK == 128 else jnp.concatenate([x] * (SB_TK // 128), axis=1)


def _lanes_of(h, shape):
    lane = lax.broadcasted_iota(jnp.int32, shape, len(shape) - 1)
    return (lane < SB_DH) if h == 0 else (lane >= SB_DH)


def _sb2_fwd(p, rider=None):
    S = p.shape[0]
    tq = min(SB_TQ, S)
    kb_per_q = tq // SB_TK

    def body(q_ref, k_ref, v_ref, o_ref, tot_ref, cnt_ref, qm, acc, c):
        i = pl.program_id(1)
        q0 = i * tq
        later = _tri2(lambda j, s: j > s)
        q2 = q_ref[...]
        for h in range(2):
            qm[h * tq:(h + 1) * tq, :] = jnp.where(_lanes_of(h, q2.shape), q2, 0.0).astype(BF16)
        acc[...] = jnp.zeros_like(acc)
        c[...] = jnp.zeros_like(c)
        nkb = (i + 1) * kb_per_q

        def more(st):
            n, highest = st
            return (n < nkb) & (highest > SB_EXP_FLOOR)

        def step(st):
            n, _ = st
            k0 = pl.multiple_of((nkb - 1 - n) * SB_TK, SB_TK)
            kb, vb = k_ref[pl.ds(k0, SB_TK), :].astype(BF16), v_ref[pl.ds(k0, SB_TK), :].astype(BF16)
            c_old = c[...]
            z, valid, ls, l1m = _sb_scores(qm[...], kb, k0, q0, tq)
            c_new = c_old + jnp.sum(l1m, axis=1, keepdims=True)
            after = jnp.dot(_split2(l1m), later, preferred_element_type=F32)
            a = jnp.where(valid, jnp.exp(ls + after + _wide(c_old)), 0.0)
            av = jnp.dot(a.astype(BF16), vb, preferred_element_type=F32)
            acc[...] += jnp.where(_lanes_of(0, (tq, 128)), av[:tq], av[tq:])
            c[...] = c_new
            return n + 1, jnp.max(c_new)

        n_done, _ = lax.while_loop(more, step, (jnp.int32(0), jnp.float32(0.0)))
        o_ref[...] = acc[...].astype(o_ref.dtype)
        for h in range(2):
            tot_ref[h] = c[h * tq:(h + 1) * tq, :]
        cnt_ref[...] = jnp.full(cnt_ref.shape, n_done.astype(F32))

    col = lambda first: pl.BlockSpec((S, 128), lambda g, i: (0, first + g))
    outs = _call_with_rider(
        rider, body, name="sb_fwd", grid=(SB_PAIRS, S // tq), args=[p, p, p],
        in_specs=[pl.BlockSpec((tq, 128), lambda g, i: (i, _Q_BLK + g)), col(_K_BLK), col(_V_BLK)],
        out_specs=[pl.BlockSpec((tq, 128), lambda g, i: (i, g)), pl.BlockSpec((2, tq, 128), lambda g, i: (g, i, 0)),
                   pl.BlockSpec((None, None, 8, 128), lambda g, i: (g, i, 0, 0))],
        out_shape=[jax.ShapeDtypeStruct((S, BW), BF16), jax.ShapeDtypeStruct((SB_HEADS, S, 128), F32),
                   jax.ShapeDtypeStruct((SB_PAIRS, S // tq, 8, 128), F32)],
        scratch_shapes=[pltpu.VMEM((2 * tq, 128), BF16), pltpu.VMEM((tq, 128), F32), pltpu.VMEM((2 * tq, 128), F32)],
        semantics=("parallel", "parallel"))
    return outs[:3], outs[3:]


def _sb2_bwd(p, dbr, tot, cnt, rider=None):
    S = p.shape[0]
    tq = min(SB_TQ, S)
    kb_per_q = tq // SB_TK
    scale = SB_DH ** -0.5

    def body(q_ref, k_ref, v_ref, do_ref, tot_ref, cnt_ref, dq_ref, dk_ref, dv_ref, qm, dom, tot, dq_acc, pre, gpre):
        i = pl.program_id(1)
        q0 = i * tq
        upto = _tri2(lambda j, s: j <= s)
        before = _tri2(lambda j, s: j < s)

        @pl.when(i == 0)
        def _():
            dk_ref[...] = jnp.zeros_like(dk_ref)
            dv_ref[...] = jnp.zeros_like(dv_ref)

        q2, do2 = q_ref[...], do_ref[...]
        for h in range(2):
            rows = slice(h * tq, (h + 1) * tq)
            qm[rows, :] = jnp.where(_lanes_of(h, q2.shape), q2, 0.0).astype(BF16)
            dom[rows, :] = jnp.where(_lanes_of(h, do2.shape), do2, 0.0).astype(BF16)
            tot[rows, :] = tot_ref[h]
        dq_acc[...] = jnp.zeros_like(dq_acc)
        pre[...] = jnp.zeros_like(pre)
        gpre[...] = jnp.zeros_like(gpre)

        n_done = jnp.max(cnt_ref[...]).astype(jnp.int32)
        first = (i + 1) * kb_per_q - n_done

        def step(n, carry):
            k0 = pl.multiple_of((first + n) * SB_TK, SB_TK)
            kb, vb = k_ref[pl.ds(k0, SB_TK), :].astype(BF16), v_ref[pl.ds(k0, SB_TK), :].astype(BF16)
            pre_o, gpre_o = pre[...], gpre[...]
            z, valid, ls, l1m = _sb_scores(qm[...], kb, k0, q0, tq)
            incl = jnp.dot(_split2(l1m), upto, preferred_element_type=F32)
            rest = _wide(tot[...] - pre_o) - incl
            a = jnp.where(valid, jnp.exp(ls + rest), 0.0)
            da = lax.dot_general(dom[...], vb, NT, preferred_element_type=F32)
            g = a * da
            gbefore = jnp.dot(_split2(g), before, preferred_element_type=F32) + _wide(gpre_o)
            dz = jnp.where(valid, g * jnp.exp(ls - z) - jnp.exp(ls) * gbefore, 0.0) * scale
            dzb = dz.astype(BF16)
            dq_p = jnp.dot(dzb, kb, preferred_element_type=F32)
            dq_acc[...] += jnp.where(_lanes_of(0, (tq, 128)), dq_p[:tq], dq_p[tq:])
            dk_ref[pl.ds(k0, SB_TK), :] += lax.dot_general(dzb, qm[...], TN, preferred_element_type=F32)
            dv_ref[pl.ds(k0, SB_TK), :] += lax.dot_general(a.astype(BF16), dom[...], TN, preferred_element_type=F32)
            pre[...] = pre_o + jnp.sum(l1m, axis=1, keepdims=True)
            gpre[...] = gpre_o + jnp.sum(g, axis=1, keepdims=True)
            return carry

        lax.fori_loop(0, n_done, step, 0)
        dq_ref[...] = dq_acc[...].astype(dq_ref.dtype)

    col = lambda first: pl.BlockSpec((S, 128), lambda g, i: (0, first + g))
    tile = pl.BlockSpec((tq, 128), lambda g, i: (i, g))
    whole = pl.BlockSpec((S, 128), lambda g, i: (0, g))
    outs = _call_with_rider(
        rider, body, name="sb_bwd", grid=(SB_PAIRS, S // tq), args=[p, p, p, dbr, tot, cnt],
        in_specs=[pl.BlockSpec((tq, 128), lambda g, i: (i, _Q_BLK + g)), col(_K_BLK), col(_V_BLK),
                  pl.BlockSpec((None, tq, 128), lambda g, i: (0, i, g)), pl.BlockSpec((2, tq, 128), lambda g, i: (g, i, 0)),
                  pl.BlockSpec((None, None, 8, 128), lambda g, i: (g, i, 0, 0))],
        out_specs=[tile, whole, whole],
        out_shape=[jax.ShapeDtypeStruct((S, BW), BF16), jax.ShapeDtypeStruct((S, BW), F32), jax.ShapeDtypeStruct((S, BW), F32)],
        scratch_shapes=[pltpu.VMEM((2 * tq, 128), BF16), pltpu.VMEM((2 * tq, 128), BF16), pltpu.VMEM((2 * tq, 128), F32),
                        pltpu.VMEM((tq, 128), F32), pltpu.VMEM((2 * tq, 128), F32), pltpu.VMEM((2 * tq, 128), F32)],
        semantics=("parallel", "arbitrary"))
    return outs[:3], outs[3:]


_INV_SQRT2 = 0.7071067811865476
_INV_SQRT2PI = 0.3989422804014327


def _gelu(x):
    return 0.5 * x * (1.0 + lax.erf(x * _INV_SQRT2))


def _gelu_grad(x):
    return 0.5 * (1.0 + lax.erf(x * _INV_SQRT2)) + x * (_INV_SQRT2PI * jnp.exp(-0.5 * x * x))


def _sgu_mask():
    t = lax.broadcasted_iota(jnp.int32, (SGU_LEN, SGU_LEN), 0) // SGU_CHUNK
    s = lax.broadcasted_iota(jnp.int32, (SGU_LEN, SGU_LEN), 1) // SGU_CHUNK
    return t >= s


def _sgu_mask_t():
    t = lax.broadcasted_iota(jnp.int32, (SGU_LEN, SGU_LEN), 0) // SGU_CHUNK
    s = lax.broadcasted_iota(jnp.int32, (SGU_LEN, SGU_LEN), 1) // SGU_CHUNK
    return s >= t


def _sgu_norm(zv, g, b):
    vv = _gelu(zv)
    xc = vv - jnp.mean(vv, axis=-1, keepdims=True)
    rstd = lax.rsqrt(jnp.mean(xc * xc, axis=-1, keepdims=True) + 1e-5)
    xhat = xc * rstd
    return xhat, rstd, xhat * g + b


SGU_TM = 256


def _sgu_fwd(p, ln_g, ln_b, w_s, b_st):
    S = p.shape[0]
    tm = min(SGU_TM, S)

    def body(zu_ref, zv_ref, g_ref, b_ref, w_ref, bs_ref, o_ref):
        u = _gelu(zu_ref[...])
        _, _, vn = _sgu_norm(zv_ref[...], g_ref[...], b_ref[...])
        vnb = vn.astype(BF16)
        mask = _sgu_mask()
        for gi in range(SGU_GROUPS):
            wg = jnp.where(mask, w_ref[gi], 0.0).astype(BF16)
            cols = slice(gi * SGU_GD, (gi + 1) * SGU_GD)
            for ci in range(tm // SGU_LEN):
                rows = slice(ci * SGU_LEN, (ci + 1) * SGU_LEN)
                vm = jnp.dot(wg, vnb[rows, cols], preferred_element_type=F32) + bs_ref[:, gi:gi + 1]
                o_ref[rows, cols] = (u[rows, cols] * vm).astype(BF16)

    vec = pl.BlockSpec((1, BW), lambda i: (0, 0))
    return pl.pallas_call(
        body, name="sgu_fwd", grid=(S // tm,),
        in_specs=[pl.BlockSpec((tm, BW), lambda i: (i, C_Z // BW)), pl.BlockSpec((tm, BW), lambda i: (i, C_Z // BW + 1)), vec, vec,
                  pl.BlockSpec((SGU_GROUPS, SGU_LEN, SGU_LEN), lambda i: (0, 0, 0)), pl.BlockSpec((SGU_LEN, SGU_GROUPS), lambda i: (0, 0))],
        out_specs=pl.BlockSpec((tm, BW), lambda i: (i, 0)), out_shape=jax.ShapeDtypeStruct((S, BW), BF16),
        compiler_params=_cp("parallel"),
    )(p, p, ln_g.reshape(1, BW), ln_b.reshape(1, BW), w_s, b_st)


def _sgu_bwd(p, dyb, ln_g, ln_b, w_s, w_st, b_st):
    S = p.shape[0]
    tm = min(SGU_TM, S)

    def body(zu_ref, zv_ref, dy_ref, g_ref, b_ref, w_ref, wt_ref, bs_ref, dz_ref, dg_ref, db_ref, dw_ref, dbs_ref, dvn):
        first = pl.program_id(0) == 0

        @pl.when(first)
        def _():
            dg_ref[...] = jnp.zeros_like(dg_ref)
            db_ref[...] = jnp.zeros_like(db_ref)
            dw_ref[...] = jnp.zeros_like(dw_ref)
            dbs_ref[...] = jnp.zeros_like(dbs_ref)

        zu, zv, dy = zu_ref[...], zv_ref[...], dy_ref[...].astype(F32)
        u = _gelu(zu)
        xhat, rstd, vn = _sgu_norm(zv, g_ref[...], b_ref[...])
        vnb = vn.astype(BF16)
        mask = _sgu_mask()
        mask_t = _sgu_mask_t()
        for gi in range(SGU_GROUPS):
            wg = jnp.where(mask, w_ref[gi], 0.0).astype(BF16)
            wgt = jnp.where(mask_t, wt_ref[gi], 0.0).astype(BF16)
            cols = slice(gi * SGU_GD, (gi + 1) * SGU_GD)
            for ci in range(tm // SGU_LEN):
                rows = slice(ci * SGU_LEN, (ci + 1) * SGU_LEN)
                vm = jnp.dot(wg, vnb[rows, cols], preferred_element_type=F32) + bs_ref[:, gi:gi + 1]
                dyc = dy[rows, cols]
                dz_ref[rows, cols] = (dyc * vm * _gelu_grad(zu[rows, cols])).astype(BF16)
                dvm = dyc * u[rows, cols]
                dvmb = dvm.astype(BF16)
                dbs_ref[gi] += jnp.broadcast_to(jnp.sum(dvm, axis=1, keepdims=True), (SGU_LEN, SGU_GD))
                dw_ref[gi] += lax.dot_general(dvmb, vnb[rows, cols], NT, preferred_element_type=F32)
                dvn[rows, cols] = jnp.dot(wgt, dvmb, preferred_element_type=F32)
        dvnv = dvn[...]
        dg_ref[...] += jnp.sum(dvnv * xhat, axis=0, keepdims=True)
        db_ref[...] += jnp.sum(dvnv, axis=0, keepdims=True)
        dxh = dvnv * g_ref[...]
        dvv = rstd * (dxh - jnp.mean(dxh, axis=-1, keepdims=True) - xhat * jnp.mean(dxh * xhat, axis=-1, keepdims=True))
        dz_ref[:, BW:] = (dvv * _gelu_grad(zv)).astype(BF16)

        @pl.when(pl.program_id(0) == n_steps - 1)
        def _():
            for gi in range(SGU_GROUPS):
                dw_ref[gi] = jnp.where(mask, dw_ref[gi], 0.0)

    n_steps = S // tm
    vec = pl.BlockSpec((1, BW), lambda i: (0, 0))
    half = lambda c: pl.BlockSpec((tm, BW), lambda i: (i, c))
    wspec = pl.BlockSpec((SGU_GROUPS, SGU_LEN, SGU_LEN), lambda i: (0, 0, 0))
    dz, dg, db, dw, dbs = pl.pallas_call(
        body, name="sgu_bwd", grid=(n_steps,),
        in_specs=[half(C_Z // BW), half(C_Z // BW + 1), half(0), vec, vec, wspec, wspec,
                  pl.BlockSpec((SGU_LEN, SGU_GROUPS), lambda i: (0, 0))],
        out_specs=[pl.BlockSpec((tm, 2 * BW), lambda i: (i, 0)), vec, vec, wspec, wspec],
        out_shape=[jax.ShapeDtypeStruct((S, 2 * BW), BF16), jax.ShapeDtypeStruct((1, BW), F32), jax.ShapeDtypeStruct((1, BW), F32),
                   jax.ShapeDtypeStruct((SGU_GROUPS, SGU_LEN, SGU_LEN), F32), jax.ShapeDtypeStruct((SGU_GROUPS, SGU_LEN, SGU_GD), F32)],
        scratch_shapes=[pltpu.VMEM((tm, BW), F32)],
        compiler_params=_cp("arbitrary"),
    )(p, p, dyb, ln_g.reshape(1, BW), ln_b.reshape(1, BW), w_s, w_st, b_st)
    return dz, dg.reshape(BW), db.reshape(BW), dw, dbs[:, :, 0]


CONV_TC = 128


def _shift_down(y, n):
    rows = lax.broadcasted_iota(jnp.int32, y.shape, 0)
    return jnp.where(rows < n, 0.0, pltpu.roll(y, n, 0))


def _shift_up(y, n):
    rows = lax.broadcasted_iota(jnp.int32, y.shape, 0)
    return jnp.where(rows >= y.shape[0] - n, 0.0, pltpu.roll(y, y.shape[0] - n, 0))


def _conv_specs(S):
    col = lambda c0: pl.BlockSpec((S, CONV_TC), lambda j: (0, c0 // CONV_TC + j))
    return col(C_CB), col(C_CB + BW), col(C_CB + 2 * BW), pl.BlockSpec((3, CONV_TC), lambda j: (0, j)), pl.BlockSpec((S, CONV_TC), lambda j: (0, j))


def _conv_fwd(p, conv_w):
    S = p.shape[0]

    def body(cb_ref, cc_ref, cx_ref, w_ref, o_ref):
        y = cc_ref[...] * cx_ref[...]
        conv = w_ref[0:1, :] * _shift_down(y, 2) + w_ref[1:2, :] * _shift_down(y, 1) + w_ref[2:3, :] * y
        o_ref[...] = (cb_ref[...] * conv).astype(BF16)

    cb, cc, cx, wspec, out = _conv_specs(S)
    return pl.pallas_call(
        body, name="conv_fwd", grid=(BW // CONV_TC,), in_specs=[cb, cc, cx, wspec], out_specs=out,
        out_shape=jax.ShapeDtypeStruct((S, BW), BF16), compiler_params=_cp("parallel"),
    )(p, p, p, conv_w)


def _conv_bwd(p, conv_w, dyc):
    S = p.shape[0]

    def body(cb_ref, cc_ref, cx_ref, w_ref, dy_ref, db_ref, dc_ref, dx_ref, dw_ref):
        cc, cx, dy = cc_ref[...], cx_ref[...], dy_ref[...].astype(F32)
        y = cc * cx
        w0, w1, w2 = w_ref[0:1, :], w_ref[1:2, :], w_ref[2:3, :]
        y1, y2 = _shift_down(y, 1), _shift_down(y, 2)
        conv = w0 * y2 + w1 * y1 + w2 * y
        db_ref[...] = (dy * conv).astype(BF16)
        dconv = dy * cb_ref[...]
        dyy = w2 * dconv + w1 * _shift_up(dconv, 1) + w0 * _shift_up(dconv, 2)
        dc_ref[...] = (dyy * cx).astype(BF16)
        dx_ref[...] = (dyy * cc).astype(BF16)
        dw_ref[0:1, :] = jnp.sum(dconv * y2, axis=0, keepdims=True)
        dw_ref[1:2, :] = jnp.sum(dconv * y1, axis=0, keepdims=True)
        dw_ref[2:3, :] = jnp.sum(dconv * y, axis=0, keepdims=True)

    cb, cc, cx, wspec, out = _conv_specs(S)
    db, dc, dx, dw = pl.pallas_call(
        body, name="conv_bwd", grid=(BW // CONV_TC,), in_specs=[cb, cc, cx, wspec, out],
        out_specs=[out, out, out, wspec],
        out_shape=[jax.ShapeDtypeStruct((S, BW), BF16)] * 3 + [jax.ShapeDtypeStruct((3, BW), F32)],
        compiler_params=_cp("parallel"),
    )(p, p, p, conv_w, dyc)
    return db, dc, dx, dw


def _merge_specs(S, tm):
    gate = lambda n: pl.BlockSpec((tm, D_MODEL), lambda i: (i, C_GATES // D_MODEL + n))
    return [gate(0), gate(1), gate(2)], pl.BlockSpec((3, tm, D_MODEL), lambda i: (0, i, 0)), pl.BlockSpec((tm, D_MODEL), lambda i: (i, 0))


def _branch_merge(br, w_br, p):
    S = p.shape[0]
    tm = _tile(S, (512, 256))

    def body(br_ref, w_ref, g0, g1, g2, bd_ref, o_ref):
        acc = None
        for n, g_ref in enumerate((g0, g1, g2)):
            bdn = jnp.dot(br_ref[n], w_ref[n], preferred_element_type=F32)
            bd_ref[n] = bdn
            term = jax.nn.sigmoid(g_ref[...]) * bdn
            acc = term if acc is None else acc + term
        o_ref[...] = acc.astype(BF16)

    gates, bspec, row = _merge_specs(S, tm)
    return pl.pallas_call(
        body, name="mm_branch", grid=(S // tm,),
        in_specs=[pl.BlockSpec((3, tm, BW), lambda i: (0, i, 0)), pl.BlockSpec((3, BW, D_MODEL), lambda i: (0, 0, 0))] + gates,
        out_specs=[bspec, row], out_shape=[jax.ShapeDtypeStruct((3, S, D_MODEL), F32), jax.ShapeDtypeStruct((S, D_MODEL), BF16)],
        compiler_params=_cp("parallel"),
    )(br, w_br, p, p, p)


def _merge_bwd(p, bd, dm):
    S = p.shape[0]
    tm = _tile(S, (256,))

    def body(g0, g1, g2, b_ref, dm_ref, db_ref, dg_ref):
        dmv = dm_ref[...]
        for n, g_ref in enumerate((g0, g1, g2)):
            sg = jax.nn.sigmoid(g_ref[...])
            db_ref[n] = (dmv * sg).astype(BF16)
            dg_ref[:, n * D_MODEL:(n + 1) * D_MODEL] = (dmv * b_ref[n] * (sg * (1.0 - sg))).astype(BF16)

    gates, bspec, row = _merge_specs(S, tm)
    return pl.pallas_call(
        body, name="merge_bwd", grid=(S // tm,), in_specs=gates + [bspec, row],
        out_specs=[bspec, pl.BlockSpec((tm, 3 * D_MODEL), lambda i: (i, 0))],
        out_shape=[jax.ShapeDtypeStruct((3, S, D_MODEL), BF16), jax.ShapeDtypeStruct((S, 3 * D_MODEL), BF16)],
        compiler_params=_cp("parallel"),
    )(p, p, p, bd, dm)


XA_TM = 512


def _xa_probs(qh, kh):
    s = lax.dot_general(qh, kh, NT, preferred_element_type=F32) * (XA_DH ** -0.5)
    e = jnp.exp(s - jnp.max(s, axis=-1, keepdims=True))
    return e / jnp.sum(e, axis=-1, keepdims=True)


def _xa_fwd(q, kv):
    S = q.shape[0]
    tm = min(XA_TM, S)
    M = kv.shape[1]

    def body(q_ref, kv_ref, o_ref):
        for h in range(XA_HEADS):
            cols = slice(h * XA_DH, (h + 1) * XA_DH)
            pr = _xa_probs(q_ref[:, cols], kv_ref[0, :, cols])
            o_ref[:, cols] = jnp.dot(pr.astype(BF16), kv_ref[1, :, cols], preferred_element_type=F32).astype(BF16)

    row = pl.BlockSpec((tm, D_MODEL), lambda i: (i, 0))
    return pl.pallas_call(
        body, name="xa_fwd", grid=(S // tm,), in_specs=[row, pl.BlockSpec((2, M, D_MODEL), lambda i: (0, 0, 0))], out_specs=row,
        out_shape=jax.ShapeDtypeStruct((S, D_MODEL), BF16), compiler_params=_cp("parallel"),
    )(q, kv)


def _xa_bwd(q, kv, do):
    S = q.shape[0]
    tm = min(XA_TM, S)
    M = kv.shape[1]

    def body(q_ref, kv_ref, do_ref, dq_ref, dkv_ref):
        @pl.when(pl.program_id(0) == 0)
        def _():
            dkv_ref[...] = jnp.zeros_like(dkv_ref)

        for h in range(XA_HEADS):
            cols = slice(h * XA_DH, (h + 1) * XA_DH)
            qh, kh, vh, doh = q_ref[:, cols], kv_ref[0, :, cols], kv_ref[1, :, cols], do_ref[:, cols]
            pr = _xa_probs(qh, kh)
            dkv_ref[1, :, cols] += lax.dot_general(pr.astype(BF16), doh, TN, preferred_element_type=F32)
            dp = lax.dot_general(doh, vh, NT, preferred_element_type=F32)
            ds = (pr * (dp - jnp.sum(dp * pr, axis=-1, keepdims=True)) * (XA_DH ** -0.5)).astype(BF16)
            dq_ref[:, cols] = jnp.dot(ds, kh, preferred_element_type=F32).astype(BF16)
            dkv_ref[0, :, cols] += lax.dot_general(ds, qh, TN, preferred_element_type=F32)

    row = pl.BlockSpec((tm, D_MODEL), lambda i: (i, 0))
    kvs = pl.BlockSpec((2, M, D_MODEL), lambda i: (0, 0, 0))
    return pl.pallas_call(
        body, name="xa_bwd", grid=(S // tm,), in_specs=[row, kvs, row], out_specs=[row, kvs],
        out_shape=[jax.ShapeDtypeStruct((S, D_MODEL), BF16), jax.ShapeDtypeStruct((2, M, D_MODEL), F32)],
        compiler_params=_cp("arbitrary"),
    )(q, kv, do)


def _ffn_up(h3, w_gu, rider=None):
    S = h3.shape[0]
    tm = _tile(S, (1024, 512, 256))

    def body(h_ref, w_ref, ab_ref, hh_ref):
        h = h_ref[...]
        a = jnp.dot(h, w_ref[0], preferred_element_type=F32)
        b = jnp.dot(h, w_ref[1], preferred_element_type=F32)
        ab_ref[0] = a
        ab_ref[1] = b
        hh_ref[...] = (a * jax.nn.sigmoid(a) * b).astype(BF16)

    pair = pl.BlockSpec((None, 2, tm, FFN_SH), lambda j, i: (j, 0, i, 0))
    outs = _call_with_rider(
        rider, body, name="mm_gu", grid=(N_CHIPS, S // tm), args=[h3, w_gu],
        in_specs=[pl.BlockSpec((tm, D_MODEL), lambda j, i: (i, 0)), pl.BlockSpec((None, 2, D_MODEL, FFN_SH), lambda j, i: (j, 0, 0, 0))],
        out_specs=[pair, pl.BlockSpec((None, tm, FFN_SH), lambda j, i: (j, i, 0))],
        out_shape=[jax.ShapeDtypeStruct((N_CHIPS, 2, S, FFN_SH), F32), jax.ShapeDtypeStruct((N_CHIPS, S, FFN_SH), BF16)],
        scratch_shapes=[], semantics=("parallel", "parallel"))
    return outs[0], outs[1], outs[2:]


def _ffn_down_bwd(dx3, w_dn, ab):
    S = dx3.shape[0]
    tm = _tile(S, (1024, 512, 256))

    def body(dx_ref, w_ref, ab_ref, o_ref):
        d = lax.dot_general(dx_ref[...].astype(BF16), w_ref[...], NT, preferred_element_type=F32)
        a, b = ab_ref[0], ab_ref[1]
        sg = jax.nn.sigmoid(a)
        o_ref[0] = (d * b * (sg * (1.0 + a * (1.0 - sg)))).astype(BF16)
        o_ref[1] = (d * (a * sg)).astype(BF16)

    pair = pl.BlockSpec((None, 2, tm, FFN_SH), lambda j, i: (j, 0, i, 0))
    return pl.pallas_call(
        body, name="mm_down_dx", grid=(N_CHIPS, S // tm),
        in_specs=[pl.BlockSpec((tm, D_MODEL), lambda j, i: (i, 0)), pl.BlockSpec((None, FFN_SH, D_MODEL), lambda j, i: (j, 0, 0)), pair],
        out_specs=pair, out_shape=jax.ShapeDtypeStruct(ab.shape, BF16), compiler_params=_cp("parallel", "parallel"),
    )(dx3, w_dn, ab)


def _reduce_adam(parts, w, m, v, *, name):
    shape = w.shape
    C = shape[-1]
    R = math.prod(shape[:-1])
    tm = _rows(R, 4 * C)
    n = len(parts)
    c1, c2 = 1.0 - ADAM_B1 ** ADAM_STEP, 1.0 - ADAM_B2 ** ADAM_STEP

    def body(*refs):
        g = refs[0][...]
        for r in refs[1:n]:
            g = g + r[...]
        w_ref, m_ref, v_ref, go, do, mo, vo = refs[n:]
        mn = ADAM_B1 * m_ref[...] + (1.0 - ADAM_B1) * g
        vn = ADAM_B2 * v_ref[...] + (1.0 - ADAM_B2) * (g * g)
        go[...] = g
        do[...] = -ADAM_LR * ((mn / c1) / (jnp.sqrt(vn / c2) + ADAM_EPS) + ADAM_WD * w_ref[...])
        mo[...] = mn
        vo[...] = vn

    row = pl.BlockSpec((tm, C), lambda i: (i, 0))
    outs = pl.pallas_call(
        body, name=name, grid=(R // tm,), in_specs=[row] * (n + 3), out_specs=[row] * 4,
        out_shape=[jax.ShapeDtypeStruct((R, C), F32)] * 4, compiler_params=_cp("parallel"),
    )(*[a.reshape(R, C) for a in (*parts, w, m, v)])
    return tuple(o.reshape(shape) for o in outs)


_VIEW = {
    "w_in": ((1024, 7168), (1024, 1792), 256, lambda i, b: (i, b)),
    "w_br": ((1536, 1024), (1536, 256), 512, lambda i, b: (i, b)),
    "w_sq": ((5120, 1024), (1280, 1024), 256, lambda i, b: (4 * i + b, 0)),
    "w_gu": ((8192, 704), (2048, 704), 512, lambda i, b: (4 * b + i, 0)),
    "w_dn": ((2816, 1024), (704, 1024), 352, lambda i, b: (2 * b + i, 0)),
    "conv_w": ((8, 512), (8, 128), 8, lambda i, b: (0, b)),
}


def _scalar(v):
    return jnp.asarray(v, jnp.int32).reshape(1)


def _place(name, local, b):
    full2, sh2, tm, idx = _VIEW[name]
    C = sh2[1]
    dt = local.dtype if name == "conv_w" else BF16

    def body(b_ref, x_ref, o0_ref, o1_ref):
        o0_ref[...] = x_ref[0].astype(dt)
        o1_ref[...] = x_ref[1].astype(dt)

    place = pl.BlockSpec((tm, C), lambda i, bs: idx(i, bs[0]))
    outs = pl.pallas_call(
        body, name="place_" + name,
        grid_spec=pltpu.PrefetchScalarGridSpec(num_scalar_prefetch=1, grid=(sh2[0] // tm,),
                                               in_specs=[pl.BlockSpec((DEPTH, tm, C), lambda i, bs: (0, i, 0))], out_specs=[place, place]),
        out_shape=[jax.ShapeDtypeStruct(full2, dt)] * 2, compiler_params=_cp("arbitrary"),
    )(_scalar(b), local.reshape((DEPTH,) + sh2))
    return [o.reshape(_FULL_SHAPE[name]) for o in outs]


def _add_owner(name, g, land, own):
    shape = g.shape
    C = shape[-1]
    R = math.prod(shape[:-1])
    tm = _rows(R, 4 * C)

    def body(s_ref, g_ref, l_ref, o_ref):
        @pl.when(s_ref[0] != 0)
        def _():
            o_ref[...] = (g_ref[...] + l_ref[...]).astype(BF16)

        @pl.when(s_ref[0] == 0)
        def _():
            o_ref[...] = jnp.zeros_like(o_ref)

    pick = pl.BlockSpec((tm, C), lambda i, s: (jnp.where(s[0] != 0, i, 0), 0))
    return pl.pallas_call(
        body, name="presum_" + name,
        grid_spec=pltpu.PrefetchScalarGridSpec(num_scalar_prefetch=1, grid=(R // tm,), in_specs=[pick, pick],
                                               out_specs=pl.BlockSpec((tm, C), lambda i, s: (i, 0))),
        out_shape=jax.ShapeDtypeStruct((R, C), BF16), compiler_params=_cp("arbitrary"),
    )(_scalar(own), g.reshape(R, C), land.reshape(R, C)).reshape(shape)


def _sum_chips(name, slots, part, b, own):
    full2, sh2, tm, idx = _VIEW[name]
    C = sh2[1]

    def body(s_ref, slot_ref, own_ref, o_ref):
        @pl.when(s_ref[1] != 0)
        def _():
            o_ref[...] = ((slot_ref[0].astype(F32) + slot_ref[1].astype(F32)) + slot_ref[2].astype(F32)) + own_ref[...].astype(F32)

        @pl.when(s_ref[1] == 0)
        def _():
            o_ref[...] = jnp.zeros_like(o_ref)

    return pl.pallas_call(
        body, name="sum_chips_" + name,
        grid_spec=pltpu.PrefetchScalarGridSpec(
            num_scalar_prefetch=1, grid=(sh2[0] // tm,),
            in_specs=[pl.BlockSpec((3, tm, C), lambda i, s: (0, jnp.where(s[1] != 0, i, 0), 0)),
                      pl.BlockSpec((tm, C), lambda i, s: idx(jnp.where(s[1] != 0, i, 0), s[0]))],
            out_specs=pl.BlockSpec((tm, C), lambda i, s: (i, 0))),
        out_shape=jax.ShapeDtypeStruct(sh2, F32), compiler_params=_cp("arbitrary"),
    )(jnp.stack([jnp.asarray(b, jnp.int32), jnp.asarray(own, jnp.int32)]), slots.reshape((3,) + sh2),
      part.reshape(full2)).reshape(_SHARD_SHAPE[name])


def _adam_layers(mine, theirs, c, w, m, v, *, name):
    shape = w.shape
    C = shape[-1]
    R = math.prod(shape[1:-1])
    tm = _rows(R, 4 * C)
    c1, c2 = 1.0 - ADAM_B1 ** ADAM_STEP, 1.0 - ADAM_B2 ** ADAM_STEP

    def body(c_ref, m0_ref, m1_ref, t_ref, w_ref, m_ref, v_ref, go, do, mo, vo):
        layer = pl.program_id(0)
        g = jnp.where(layer == c_ref[0], jnp.where(layer == 0, m0_ref[...], m1_ref[...]), t_ref[...])
        mn = ADAM_B1 * m_ref[...] + (1.0 - ADAM_B1) * g
        vn = ADAM_B2 * v_ref[...] + (1.0 - ADAM_B2) * (g * g)
        go[...] = g
        do[...] = -ADAM_LR * ((mn / c1) / (jnp.sqrt(vn / c2) + ADAM_EPS) + ADAM_WD * w_ref[...])
        mo[...] = mn
        vo[...] = vn

    def own(layer):
        return pl.BlockSpec((tm, C), lambda l, i, cs: (jnp.where((l == layer) & (cs[0] == layer), i, 0), 0))

    recv = pl.BlockSpec((tm, C), lambda l, i, cs: (jnp.where(l == cs[0], 0, i), 0))
    row = pl.BlockSpec((None, tm, C), lambda l, i, cs: (l, i, 0))
    outs = pl.pallas_call(
        body, name=name,
        grid_spec=pltpu.PrefetchScalarGridSpec(num_scalar_prefetch=1, grid=(DEPTH, R // tm),
                                               in_specs=[own(0), own(1), recv, row, row, row], out_specs=[row] * 4),
        out_shape=[jax.ShapeDtypeStruct((DEPTH, R, C), F32)] * 4, compiler_params=_cp("arbitrary", "arbitrary"),
    )(_scalar(c), mine[0].reshape(R, C), mine[1].reshape(R, C), theirs.reshape(R, C), *[t.reshape(DEPTH, R, C) for t in (w, m, v)])
    return tuple(o.reshape(shape) for o in outs)


def _take_weights(wl, names, landed):
    for n, t in zip(names, landed):
        wl[n] = t[:3] if n == "conv_w" else t.reshape(2 * N_CHIPS, D_MODEL, FFN_SH) if n == "w_gu" else t


_GATHER_LATE = ["w_br", "conv_w", "w_sq"]
_GATHER_LAST = ["w_gu", "w_dn"]


def _layer_fwd(x, mem, wl, ride=None):
    S = x.shape[0]
    sv = {"x": x}
    wl = dict(wl)
    h1 = _rms_fwd(x, wl["norm_mix_g"], name="rms_mix")
    if ride is None:
        ride = {"mm_in": None, "sb": None, "mm_gu": None}
        p = _mm(h1, wl["w_in"], mode="nn", name="mm_in")
    else:
        p, landed = _mm(h1, wl["w_in"], mode="nn", name="mm_in", rider=ride["mm_in"])
        _take_weights(wl, _GATHER_LATE, landed)
    (ya, tot, cnt), landed = _sb2_fwd(p, ride["sb"])
    _take_weights(wl, _GATHER_LAST, landed)
    b_st = wl["b_spatial"].T
    yb = _sgu_fwd(p, wl["sgu_ln_g"], wl["sgu_ln_b"], wl["w_spatial"], b_st)
    yc = _conv_fwd(p, wl["conv_w"])
    br = jnp.stack([ya, yb, yc])
    bd, merged = _branch_merge(br, wl["w_br"], p)
    x1 = _mm(merged, wl["w_sq"][0], mode="nn", res=x, name="mm_out")
    h2 = _rms_fwd(x1, wl["norm_xa_g"], name="rms_xa")
    qx = _mm(h2, wl["w_sq"][1], mode="nn", out_dtype=BF16, name="mm_q")
    mn = _rms_fwd(mem, wl["mem_norm_g"], name="rms_mem")
    kv = _mm(mn, wl["w_sq"][3:5], mode="nn", b_kind="batch", out_dtype=BF16, name="mm_kv")
    o = _xa_fwd(qx, kv)
    x2 = _mm(o, wl["w_sq"][2], mode="nn", res=x1, name="mm_o")
    h3 = _rms_fwd(x2, wl["norm_ffn_g"], name="rms_ffn")
    ab, hh, rode = _ffn_up(h3, wl["w_gu"].reshape(N_CHIPS, 2, D_MODEL, FFN_SH), ride["mm_gu"])
    x3 = _mm(hh, wl["w_dn"], mode="nn", a_kind="kchunk", b_kind="kchunk", res=x2, name="mm_down")
    sv.update(h1=h1, p=p, tot=tot, cnt=cnt, br=br, bd=bd, merged=merged, x1=x1, h2=h2, qx=qx, mn=mn, kv=kv, o=o,
              x2=x2, h3=h3, ab=ab, hh=hh, b_st=b_st, wl=wl)
    return x3, sv, rode


class _GradPipe:
    def __init__(self, ci, bi):
        self.ci, self.bi, self.queue = ci, bi, []
        self.part, self.slots = [dict(), dict()], [dict(), dict()]

    def to_owner(self, layer, names, g):
        def arrived(land):
            own = (self.ci == layer).astype(jnp.int32)
            part = {n: _add_owner(n, g[n], t, own) for n, t in zip(names, land)}
            self.part[layer].update(part)
            self.queue.append((layer, names, part))

        return _presum_rider(layer, {n: g[n] for n in names}), arrived

    def exchange(self):
        if not self.queue:
            return None
        layer, names, part = self.queue.pop(0)
        return _shard_rider(layer, part), lambda slots: self.slots[layer].update(zip(names, slots))

    def drain(self, also):
        landed = None
        while self.queue or landed is None:
            job = self.exchange()
            both = _join([job[0] if job else None, also if landed is None else None])
            parts = both.split(_run_rider(both, name="grad_exchange_last"))
            if job:
                job[1](parts[0])
            if landed is None:
                landed = parts[-1]
        return landed

    def reduced(self, layer):
        own = (self.ci == layer).astype(jnp.int32)
        return {n: _sum_chips(n, self.slots[layer][n], self.part[layer][n], self.bi, own) for n in _BIG}


def _layer_bwd(dx3, mem, sv, layer=None, pipe=None):
    S = dx3.shape[0]
    p, wl = sv["p"], sv["wl"]
    g = {}

    def mm(*args, job=None, **kw):
        if job is None:
            return _mm(*args, **kw)
        out, landed = _mm(*args, **kw, rider=job[0])
        job[1](landed)
        return out

    to_owner = (lambda names: pipe.to_owner(layer, names, g)) if pipe else (lambda names: None)
    exchange = pipe.exchange if pipe else (lambda: None)

    g["w_dn"] = _mm(sv["hh"], dx3, mode="tn", a_kind="batch", name="mm_down_dw")
    dab = _ffn_down_bwd(dx3, wl["w_dn"], sv["ab"]).reshape(2 * N_CHIPS, S, FFN_SH)
    g["w_gu"] = mm(sv["h3"], dab, mode="tn", b_kind="batch", name="mm_gu_dw", job=exchange()).reshape(_FULL_SHAPE["w_gu"])
    dh3 = mm(dab, wl["w_gu"], mode="nt", a_kind="kchunk", b_kind="kchunk", name="mm_gu_dx", job=to_owner(["w_dn", "w_gu"]))
    dx2, g["norm_ffn_g"] = _rms_bwd(sv["x2"], wl["norm_ffn_g"], dh3, dx3, name="rms_ffn_bwd")
    do = _mm(dx2, wl["w_sq"][2], mode="nt", out_dtype=BF16, name="mm_o_dx")
    dw_o = _mm(sv["o"], dx2, mode="tn", name="mm_o_dw")
    dq, dkv = _xa_bwd(sv["qx"], sv["kv"], do)
    dw_q = _mm(sv["h2"], dq, mode="tn", name="mm_q_dw")
    dh2 = _mm(dq, wl["w_sq"][1], mode="nt", name="mm_q_dx")
    dw_kv = _mm(sv["mn"], dkv, mode="tn", b_kind="batch", name="mm_kv_dw")
    dmn = _mm(dkv, wl["w_sq"][3:5], mode="nt", a_kind="kchunk", b_kind="kchunk", name="mm_kv_dx")
    _, g["mem_norm_g"] = _rms_bwd(mem, wl["mem_norm_g"], dmn, jnp.zeros_like(mem), name="rms_mem_bwd")
    dx1, g["norm_xa_g"] = _rms_bwd(sv["x1"], wl["norm_xa_g"], dh2, dx2, name="rms_xa_bwd")
    dm = _mm(dx1, wl["w_sq"][0], mode="nt", name="mm_out_dx")
    dw_out = _mm(sv["merged"], dx1, mode="tn", name="mm_out_dw")
    g["w_sq"] = jnp.concatenate([jnp.stack([dw_out, dw_q, dw_o]), dw_kv])
    dbd, dgates = _merge_bwd(p, sv["bd"], dm)
    dbr = mm(dbd, wl["w_br"], mode="nt", a_kind="batch", b_kind="batch", name="mm_branch_dx", job=to_owner(["w_sq"]))
    g["w_br"] = _mm(sv["br"], dbd, mode="tn", a_kind="batch", b_kind="batch", name="mm_branch_dw")
    job = exchange()
    (dq, dk, dv), landed = _sb2_bwd(p, dbr, sv["tot"], sv["cnt"], job[0] if job else None)
    if job:
        job[1](landed)
    dz, g["sgu_ln_g"], g["sgu_ln_b"], g["w_spatial"], g["b_spatial"] = _sgu_bwd(
        p, dbr[1], wl["sgu_ln_g"], wl["sgu_ln_b"], wl["w_spatial"], wl["w_spatial"].transpose(0, 2, 1), sv["b_st"])
    dcb, dcc, dcx, g["conv_w"] = _conv_bwd(p, wl["conv_w"], dbr[2])
    dp = jnp.concatenate([dq, dk.astype(BF16), dv.astype(BF16), dz, dcb, dcc, dcx, dgates], axis=1)
    g["w_in"] = mm(sv["h1"], dp, mode="tn", name="mm_in_dw", job=exchange())
    dh1 = mm(dp, wl["w_in"], mode="nt", name="mm_in_dx", job=to_owner(["w_br", "w_in"]))
    dx, g["norm_mix_g"] = _rms_bwd(sv["x"], wl["norm_mix_g"], dh1, dx1, name="rms_mix_bwd")
    return dx, g


def _local_step(x, mem, target, layers, final_g):
    h, saved = x, []
    for wl in layers:
        h, sv, _ = _layer_fwd(h, mem, wl)
        saved.append(sv)
    loss, dx, d_final = _loss_head(h, final_g, target)
    grads = [None] * len(layers)
    for l in reversed(range(len(layers))):
        dx, grads[l] = _layer_bwd(dx, mem, saved[l])
    return loss, dx, grads, d_final


_ALL = slice(None)
CONV_ROWS = 8
_SHARD = {
    "w_in": lambda b: (_ALL, pl.ds(1792 * b, 1792)),
    "w_br": lambda b: (_ALL, _ALL, pl.ds(256 * b, 256)),
    "w_sq": lambda b: (_ALL, pl.ds(256 * b, 256), _ALL),
    "w_gu": lambda b: (b,),
    "w_dn": lambda b: (b,),
    "conv_w": lambda b: (_ALL, pl.ds(128 * b, 128)),
}
_FULL_SHAPE = {"w_in": (1024, 7168), "w_br": (3, 512, 1024), "w_sq": (5, 1024, 1024), "w_gu": (4, 2, 1024, 704),
               "w_dn": (4, 704, 1024), "conv_w": (CONV_ROWS, 512)}
_SHARD_SHAPE = {"w_in": (1024, 1792), "w_br": (3, 512, 256), "w_sq": (5, 256, 1024), "w_gu": (2, 1024, 704),
                "w_dn": (704, 1024), "conv_w": (CONV_ROWS, 128)}


def _pos():
    return lax.axis_index("x"), lax.axis_index("y"), lax.axis_index("c")


def _per_chip(fn):
    x, y, _ = _pos()
    for x0 in (0, 1):
        for y0 in (0, 1):
            @pl.when((x == x0) & (y == y0))
            def _():
                fn(x0, y0)


def _other_chips(x0, y0):
    return [(1 - x0, y0), (x0, 1 - y0), (1 - x0, 1 - y0)]


def _rcopy(src, dst, ssem, rsem, dev):
    return pltpu.make_async_remote_copy(src_ref=src, dst_ref=dst, send_sem=ssem, recv_sem=rsem, device_id=dev, device_id_type=MESH)


def _dma_sems(n):
    return pltpu.SemaphoreType.DMA((n,))


def _gather_rider(layer, placed):
    names = list(placed)
    n = len(names)
    shard = lambda refs, a, b: refs[a].at[_SHARD[names[a]](b)]

    def start(ins, outs, send, recv):
        @pl.when(lax.axis_index("c") == layer)
        def _():
            def run(x0, y0):
                for kk, (px, py) in enumerate(_other_chips(x0, y0)):
                    for a in range(n):
                        own = shard(outs, a, 2 * x0 + y0)
                        _rcopy(own, own, send.at[6 * a + kk], recv.at[6 * a + kk], (px, py, layer)).start()

            _per_chip(run)

    def passing(outs, send, recv, a, kk, bp, x0, y0):
        landed = shard(outs, a, bp)
        return _rcopy(landed, landed, send.at[6 * a + 3 + kk], recv.at[6 * a + 3 + kk], (x0, y0, 1 - layer))

    def middle(ins, outs, send, recv):
        @pl.when(lax.axis_index("c") == layer)
        def _():
            def run(x0, y0):
                for kk, (px, py) in enumerate(_other_chips(x0, y0)):
                    for a in range(n):
                        landed = shard(outs, a, 2 * px + py)
                        _rcopy(landed, landed, send.at[6 * a + kk], recv.at[6 * a + kk], (px, py, layer)).wait_recv()
                        passing(outs, send, recv, a, kk, 2 * px + py, x0, y0).start()

            _per_chip(run)

    def finish(ins, outs, send, recv):
        c = lax.axis_index("c")

        def run(x0, y0):
            chips = _other_chips(x0, y0)

            @pl.when(c == layer)
            def _():
                for kk, (px, py) in enumerate(chips):
                    for a in range(n):
                        own = shard(outs, a, 2 * x0 + y0)
                        _rcopy(own, own, send.at[6 * a + kk], recv.at[6 * a + kk], (px, py, layer)).wait_send()
                        passing(outs, send, recv, a, kk, 2 * px + py, x0, y0).wait_send()

            @pl.when(c != layer)
            def _():
                for kk, (px, py) in enumerate(chips):
                    for a in range(n):
                        got = shard(outs, a, 2 * px + py)
                        _rcopy(got, got, send.at[6 * a + 3 + kk], recv.at[6 * a + 3 + kk], (x0, y0, layer)).wait_recv()

        _per_chip(run)

    arrs = [placed[nm] for nm in names]
    return _Rider(arrs, [jax.ShapeDtypeStruct(t.shape, t.dtype) for t in arrs], 6 * n, start, finish, alias={a: a for a in range(n)}, middle=middle)


def _presum_rider(layer, grads):
    names = list(grads)
    n = len(names)

    def start(ins, outs, send, recv):
        x, y, c = _pos()

        @pl.when(c != layer)
        def _():
            for a in range(n):
                _rcopy(ins[a], outs[a], send.at[a], recv.at[a], (x, y, layer)).start()

    def finish(ins, outs, send, recv):
        x, y, c = _pos()

        @pl.when(c != layer)
        def _():
            for a in range(n):
                _rcopy(ins[a], outs[a], send.at[a], recv.at[a], (x, y, layer)).wait_send()

        @pl.when(c == layer)
        def _():
            for a in range(n):
                _rcopy(outs[a], outs[a], send.at[a], recv.at[a], (x, y, 1 - layer)).wait_recv()

    arrs = [grads[nm] for nm in names]
    return _Rider(arrs, [jax.ShapeDtypeStruct(t.shape, t.dtype) for t in arrs], n, start, finish)


def _shard_rider(layer, part):
    names = list(part)
    n = len(names)

    def each(fn):
        @pl.when(lax.axis_index("c") == layer)
        def _():
            def run(x0, y0):
                for kk, (px, py) in enumerate(_other_chips(x0, y0)):
                    for a in range(n):
                        fn(a, kk, 2 * px + py, (px, py, layer))

            _per_chip(run)

    def start(ins, outs, send, recv):
        each(lambda a, kk, bp, peer: _rcopy(ins[a].at[_SHARD[names[a]](bp)], outs[a].at[kk], send.at[3 * a + kk], recv.at[3 * a + kk], peer).start())

    def finish(ins, outs, send, recv):
        each(lambda a, kk, bp, peer: _rcopy(outs[a].at[kk], outs[a].at[kk], send.at[3 * a + kk], recv.at[3 * a + kk], peer).wait_recv())
        each(lambda a, kk, bp, peer: _rcopy(ins[a].at[_SHARD[names[a]](bp)], outs[a].at[kk], send.at[3 * a + kk], recv.at[3 * a + kk], peer).wait_send())

    return _Rider([part[nm] for nm in names], [jax.ShapeDtypeStruct((N_CHIPS - 1,) + _SHARD_SHAPE[nm], part[nm].dtype) for nm in names],
                  3 * n, start, finish)


def _sibling_exchange(mine0, mine1):
    names = list(mine0)
    n = len(names)

    def body(*refs):
        l0, l1, outs = refs[:n], refs[n:2 * n], refs[2 * n:3 * n]
        send, recv = refs[3 * n:]
        x, y, c = _pos()
        for c0 in (0, 1):
            @pl.when(c == c0)
            def _():
                srcs = l0 if c0 == 0 else l1
                cps = [_rcopy(srcs[a], outs[a], send.at[a], recv.at[a], (x, y, 1 - c0)) for a in range(n)]
                for cp in cps:
                    cp.start()
                for cp in cps:
                    cp.wait()

    outs = pl.pallas_call(
        body, name="grad_sibling_exchange", in_specs=[ANY] * (2 * n), out_specs=[ANY] * n,
        out_shape=[jax.ShapeDtypeStruct(mine0[nm].shape, mine0[nm].dtype) for nm in names],
        scratch_shapes=[_dma_sems(n), _dma_sems(n)],
    )(*[mine0[nm] for nm in names], *[mine1[nm] for nm in names])
    return dict(zip(names, outs))


def _small_rider(pack):
    flips = [(fx, fy, fc) for fx in (0, 1) for fy in (0, 1) for fc in (0, 1) if fx or fy or fc]

    def peers():
        x, y, c = _pos()
        return 4 * x + 2 * y + c, [(x ^ fx, y ^ fy, c ^ fc) for fx, fy, fc in flips]

    def start(ins, outs, send, recv):
        me, to = peers()
        for k, peer in enumerate(to):
            _rcopy(ins[0], outs[0].at[me], send.at[k], recv.at[k], peer).start()

    def finish(ins, outs, send, recv):
        me, to = peers()
        for k, (px, py, pc) in enumerate(to):
            slot = outs[0].at[4 * px + 2 * py + pc]
            _rcopy(slot, slot, send.at[k], recv.at[k], (px, py, pc)).wait_recv()
        for k, peer in enumerate(to):
            _rcopy(ins[0], outs[0].at[me], send.at[k], recv.at[k], peer).wait_send()

    return _Rider([pack], [jax.ShapeDtypeStruct((8,) + pack.shape, pack.dtype)], len(flips), start, finish)


def _sum_devices(gathered, pack, me):
    n, R, C = gathered.shape

    def body(me_ref, r_ref, own_ref, o_ref):
        acc = jnp.where(me_ref[0] == 0, own_ref[...], r_ref[0])
        for s in range(1, n):
            acc = acc + jnp.where(me_ref[0] == s, own_ref[...], r_ref[s])
        o_ref[...] = acc

    return pl.pallas_call(
        body, name="sum_devices_small",
        grid_spec=pltpu.PrefetchScalarGridSpec(num_scalar_prefetch=1, grid=(1,),
                                               in_specs=[pl.BlockSpec((n, R, C), lambda i, m: (0, 0, 0)), pl.BlockSpec((R, C), lambda i, m: (0, 0))],
                                               out_specs=pl.BlockSpec((R, C), lambda i, m: (0, 0))),
        out_shape=jax.ShapeDtypeStruct((R, C), F32), compiler_params=_cp("arbitrary"),
    )(_scalar(me), gathered, pack)


_WEIGHTS = ["norm_mix_g", "w_in", "sgu_ln_g", "sgu_ln_b", "w_spatial", "b_spatial", "conv_w", "w_branch", "w_out", "norm_xa_g",
            "mem_norm_g", "w_q_xa", "w_k_xa", "w_v_xa", "w_o_xa", "norm_ffn_g", "w_gate_ffn", "w_up_ffn", "w_down_ffn", "final_g"]
_REPLICATED = ["norm_mix_g", "sgu_ln_g", "sgu_ln_b", "w_spatial", "b_spatial", "norm_xa_g", "mem_norm_g", "norm_ffn_g", "final_g"]
_SQUARE = ["w_out", "w_q_xa", "w_o_xa", "w_k_xa", "w_v_xa"]
_BIG = ["w_in", "w_br", "w_sq", "w_gu", "w_dn"]


def _pack(arrs):
    return jnp.concatenate([a.reshape(-1) for a in arrs]).reshape(-1, 128)


def _step(a):
    w = {n: a[n] for n in _WEIGHTS}
    x, mem, target = a["x"][0], a["mem"][0], a["loss_target"][0]
    xi, yi, ci = _pos()
    bi = 2 * xi + yi
    groups = list(_FULL_SHAPE)

    local = {"w_in": w["w_in"], "w_br": w["w_branch"], "w_sq": jnp.stack([w[n] for n in _SQUARE], axis=1),
             "w_gu": jnp.stack([w["w_gate_ffn"], w["w_up_ffn"]], axis=1), "w_dn": w["w_down_ffn"],
             "conv_w": jnp.pad(w["conv_w"], ((0, 0), (0, CONV_ROWS - 3), (0, 0)))}
    placed = [dict(), dict()]
    for n in groups:
        placed[0][n], placed[1][n] = _place(n, local[n], bi)

    def gather(l, names):
        return _gather_rider(l, {n: placed[l][n] for n in names})

    def start_of(l, w_in):
        return {"w_in": w_in, **{n: w[n][l] for n in _REPLICATED if n != "final_g"}}

    w_in0, = _run_rider(gather(0, ["w_in"]), name="gather_first")
    h, sv0, (w_in1,) = _layer_fwd(x, mem, start_of(0, w_in0), {"mm_in": gather(0, _GATHER_LATE), "sb": gather(0, _GATHER_LAST),
                                                                "mm_gu": gather(1, ["w_in"])})
    h, sv1, _ = _layer_fwd(h, mem, start_of(1, w_in1), {"mm_in": gather(1, _GATHER_LATE), "sb": gather(1, _GATHER_LAST), "mm_gu": None})
    loss, dx, d_final = _loss_head(h, w["final_g"], target)
    loss = lax.psum(loss, ("x", "y", "c"))

    pipe = _GradPipe(ci, bi)
    dx, g1 = _layer_bwd(dx, mem, sv1, 1, pipe)
    dx, g0 = _layer_bwd(dx, mem, sv0, 0, pipe)
    grads = [g0, g1]
    small = {n: jnp.stack([g[n] for g in grads]) for n in _REPLICATED if n != "final_g"}
    small["final_g"] = d_final
    conv_g = jnp.stack([g["conv_w"] for g in grads])
    small_pack = _pack([small[n] for n in _REPLICATED] + [conv_g])
    gathered, = pipe.drain(_small_rider(small_pack))
    mine = [pipe.reduced(0), pipe.reduced(1)]
    theirs = _sibling_exchange(mine[0], mine[1])

    out = {}

    def adam_layers(name, group, pick=None):
        sel = (lambda t: t[group]) if pick is None else (lambda t: t[group][pick])
        out[name] = _adam_layers([sel(mine[0]), sel(mine[1])], sel(theirs), ci, w[name], a["m_" + name], a["v_" + name], name="adam_" + name)

    adam_layers("w_in", "w_in")
    adam_layers("w_branch", "w_br")
    for t, n in enumerate(_SQUARE):
        adam_layers(n, "w_sq", t)
    adam_layers("w_gate_ffn", "w_gu", 0)
    adam_layers("w_up_ffn", "w_gu", 1)
    adam_layers("w_down_ffn", "w_dn")

    def adam(name, g):
        out[name] = _reduce_adam([g], w[name], a["m_" + name], a["v_" + name], name="adam_" + name)

    n_rep = sum(w[n].size for n in _REPLICATED) // 128
    summed = _sum_devices(gathered, small_pack, 4 * xi + 2 * yi + ci)
    res = _reduce_adam([summed[:n_rep]], _pack([w[n] for n in _REPLICATED]), _pack([a["m_" + n] for n in _REPLICATED]),
                       _pack([a["v_" + n] for n in _REPLICATED]), name="adam_replicated")
    off = 0
    for n in _REPLICATED:
        out[n] = tuple(r.reshape(-1)[off:off + w[n].size].reshape(w[n].shape) for r in res)
        off += w[n].size
    conv_full = summed[n_rep:].reshape(conv_g.shape)
    adam("conv_w", lax.dynamic_slice_in_dim(conv_full, (2 * xi + yi) * 128, 128, axis=2))

    return (loss, dx[None], *[out[n][k] for k in range(4) for n in _WEIGHTS])


def kernel(x, mem, norm_mix_g, w_in, sgu_ln_g, sgu_ln_b, w_spatial, b_spatial, conv_w, w_branch, w_out, norm_xa_g, mem_norm_g, w_q_xa, w_k_xa, w_v_xa, w_o_xa, norm_ffn_g, w_gate_ffn, w_up_ffn, w_down_ffn, final_g, loss_target, m_norm_mix_g, m_w_in, m_sgu_ln_g, m_sgu_ln_b, m_w_spatial, m_b_spatial, m_conv_w, m_w_branch, m_w_out, m_norm_xa_g, m_mem_norm_g, m_w_q_xa, m_w_k_xa, m_w_v_xa, m_w_o_xa, m_norm_ffn_g, m_w_gate_ffn, m_w_up_ffn, m_w_down_ffn, m_final_g, v_norm_mix_g, v_w_in, v_sgu_ln_g, v_sgu_ln_b, v_w_spatial, v_b_spatial, v_conv_w, v_w_branch, v_w_out, v_norm_xa_g, v_mem_norm_g, v_w_q_xa, v_w_k_xa, v_w_v_xa, v_w_o_xa, v_norm_ffn_g, v_w_gate_ffn, v_w_up_ffn, v_w_down_ffn, v_final_g):
    return _step(dict(locals()))
```

```python
import functools
import math

import jax
import jax.numpy as jnp
from jax import lax
from jax.experimental import pallas as pl
from jax.experimental.pallas import tpu as pltpu

F32, BF16 = jnp.float32, jnp.bfloat16
MESH = pl.DeviceIdType.MESH
ANY = pl.BlockSpec(memory_space=pl.ANY)

D_MODEL = 1024
DEPTH = 2
BW = 512
SB_HEADS, SB_DH = 8, 64
SGU_LEN, SGU_GROUPS, SGU_GD, SGU_CHUNK = 128, 4, 128, 64
XA_HEADS, XA_DH = 4, 256
FFN_SH = 704
N_CHIPS = 4
IN_COLS = 7168
C_Z, C_CB, C_GATES = 1536, 2560, 4096

ADAM_LR, ADAM_B1, ADAM_B2, ADAM_EPS, ADAM_WD, ADAM_STEP = 0.001, 0.9, 0.999, 1e-08, 0.01, 10

VMEM_LIMIT_V7X = 56 * 1024 * 1024

NN = (((1,), (0,)), ((), ()))
NT = (((1,), (1,)), ((), ()))
TN = (((0,), (0,)), ((), ()))


def _cp(*sem):
    return pltpu.CompilerParams(dimension_semantics=sem, vmem_limit_bytes=VMEM_LIMIT_V7X)


def _tile(n, pref):
    for t in pref:
        if n % t == 0:
            return t
    return n


def _rows(r, row_bytes, block_bytes=1 << 20):
    fits = [t for t in range(8, r + 1, 8) if r % t == 0 and t * row_bytes <= block_bytes]
    return max(fits) if fits else r


class _Rider:
    def __init__(self, ins, outs, n_sems, start, finish, alias=None, middle=None):
        self.ins, self.outs, self.n_sems, self.alias = list(ins), list(outs), n_sems, alias or {}
        self.start, self.middle, self.finish = start, middle, finish


class _Sems:
    def __init__(self, ref, first):
        self.ref, self.first = ref, first

    @property
    def at(self):
        return self

    def __getitem__(self, k):
        return self.ref.at[self.first + k]


def _join(riders):
    riders = [r for r in riders if r is not None]
    if not riders:
        return None
    spans, i0, o0, s0 = [], 0, 0, 0
    for r in riders:
        spans.append((r, i0, o0, s0))
        i0, o0, s0 = i0 + len(r.ins), o0 + len(r.outs), s0 + r.n_sems

    def phase(which):
        def run(ins, outs, send, recv):
            for r, i, o, s in spans:
                fn = getattr(r, which)
                if fn is not None:
                    fn(ins[i:i + len(r.ins)], outs[o:o + len(r.outs)], _Sems(send, s), _Sems(recv, s))
        return run

    joined = _Rider([a for r in riders for a in r.ins], [a for r in riders for a in r.outs], s0, phase("start"), phase("finish"),
                    alias={o + k: i + v for r, i, o, s in spans for k, v in r.alias.items()}, middle=phase("middle"))
    joined.split = lambda landed: [list(landed[o:o + len(r.outs)]) for r, i, o, s in spans]
    return joined


def _call_with_rider(rider, body, *, name, grid, in_specs, args, out_specs, out_shape, scratch_shapes, semantics):
    if rider is None:
        return pl.pallas_call(body, name=name, grid=grid, in_specs=in_specs, out_specs=out_specs, out_shape=out_shape,
                              scratch_shapes=scratch_shapes, compiler_params=_cp(*semantics))(*args)
    n_in, n_out, r_in, r_out = len(args), len(out_shape), len(rider.ins), len(rider.outs)

    def riding(*refs):
        ins, rins = refs[:n_in], refs[n_in:n_in + r_in]
        outs, routs = refs[n_in + r_in:n_in + r_in + n_out], refs[n_in + r_in + n_out:n_in + r_in + n_out + r_out]
        rest = refs[n_in + r_in + n_out + r_out:]
        scratch, send, recv = rest[:-2], rest[-2], rest[-1]
        step = pl.program_id(0)
        for ax in range(1, len(grid)):
            step = step * grid[ax] + pl.program_id(ax)
        n_steps = math.prod(grid)

        @pl.when(step == 0)
        def _():
            rider.start(rins, routs, send, recv)

        body(*ins, *outs, *scratch)

        if rider.middle is not None:
            @pl.when(step == (3 * n_steps) // 5)
            def _():
                rider.middle(rins, routs, send, recv)

        @pl.when(step == n_steps - 1)
        def _():
            rider.finish(rins, routs, send, recv)

    return pl.pallas_call(
        riding, name=name, grid=grid, in_specs=list(in_specs) + [ANY] * r_in, out_specs=list(out_specs) + [ANY] * r_out,
        out_shape=list(out_shape) + rider.outs, scratch_shapes=list(scratch_shapes) + [_dma_sems(rider.n_sems), _dma_sems(rider.n_sems)],
        input_output_aliases={n_in + i: n_out + o for o, i in rider.alias.items()},
        compiler_params=_cp(*["arbitrary"] * len(grid)),
    )(*args, *rider.ins)


def _run_rider(rider, *, name):
    def nothing(*refs):
        pass

    return _call_with_rider(rider, nothing, name=name, grid=(1,), in_specs=[], args=[], out_specs=[], out_shape=[], scratch_shapes=[],
                            semantics=("arbitrary",))


def _mm(a, b, *, mode, name, out_dtype=F32, res=None, rms_bwd=None, a_kind="2d", b_kind="2d", tm=None, tn=None, tk=None, rider=None):
    a2, b2 = a.shape[-2:], b.shape[-2:]
    if mode == "nn":
        (M, K), N = a2, b2[1]
    elif mode == "nt":
        (M, K), N = a2, b2[0]
    else:
        (K, M), N = a2, b2[1]
    kchunk = a_kind == "kchunk" or b_kind == "kchunk"
    batch = a_kind == "batch" or b_kind == "batch"
    G = (a.shape[0] if a_kind == "batch" else b.shape[0]) if batch else 1
    tm = tm or _tile(M, (1024, 512, 256, 128))
    tn = tn or _tile(N, (1024, 512, 256, 128))
    if kchunk:
        tk, nk = K, (a.shape[0] if a_kind == "kchunk" else b.shape[0])
    else:
        tk = tk or _tile(K, (1024, 512, 256, 128))
        nk = K // tk

    def spec(kind, blk, idx):
        if kind == "2d":
            return pl.BlockSpec(blk, lambda g, i, j, k: idx(g, i, j, k))
        if kind == "batch":
            return pl.BlockSpec((None,) + blk, lambda g, i, j, k: (g,) + idx(g, i, j, k))
        return pl.BlockSpec((None,) + blk, lambda g, i, j, k: (k,) + idx(g, i, j, 0))

    if mode == "nn":
        a_spec = spec(a_kind, (tm, tk), lambda g, i, j, k: (i, k))
        b_spec = spec(b_kind, (tk, tn), lambda g, i, j, k: (k, j))
    elif mode == "nt":
        a_spec = spec(a_kind, (tm, tk), lambda g, i, j, k: (i, k))
        b_spec = spec(b_kind, (tn, tk), lambda g, i, j, k: (j, k))
    else:
        a_spec = spec(a_kind, (tk, tm), lambda g, i, j, k: (k, i))
        b_spec = spec(b_kind, (tk, tn), lambda g, i, j, k: (k, j))
    o_kind = "batch" if batch else "2d"
    o_spec = spec(o_kind, (tm, tn), lambda g, i, j, k: (i, j))
    o_shape = ((G,) if batch else ()) + (M, N)
    dn = {"nn": NN, "nt": NT, "tn": TN}[mode]
    has_res, has_rms = res is not None, rms_bwd is not None
    assert not (has_res and has_rms) and (not has_rms or (tn == N and G == 1))

    def body(*refs):
        if has_res:
            a_ref, b_ref, r_ref, o_ref = refs[:4]
        elif has_rms:
            a_ref, b_ref, x_ref, g_ref, dres_ref, o_ref, dg_ref = refs[:7]
        else:
            a_ref, b_ref, o_ref = refs[:3]
        p = lax.dot_general(a_ref[...].astype(BF16), b_ref[...].astype(BF16), dn, preferred_element_type=F32)
        first_rows = pl.program_id(1) == 0

        def finish(r):
            if has_rms:
                xv = x_ref[...]
                rs = lax.rsqrt(jnp.mean(xv * xv, axis=-1, keepdims=True) + 1e-6)
                u = r * g_ref[...]
                s = jnp.sum(u * xv, axis=-1, keepdims=True)
                o_ref[...] = dres_ref[...] + rs * u - xv * ((rs * rs * rs) * (s * (1.0 / N)))
                part = jnp.sum(r * (xv * rs), axis=0, keepdims=True)

                @pl.when(first_rows)
                def _():
                    dg_ref[...] = part

                @pl.when(jnp.logical_not(first_rows))
                def _():
                    dg_ref[...] += part
                return
            if has_res:
                r = r + r_ref[...]
            o_ref[...] = r.astype(out_dtype)

        if nk == 1:
            finish(p)
        else:
            acc = refs[-1]
            k = pl.program_id(3)

            @pl.when(k == 0)
            def _():
                acc[...] = p

            @pl.when(k > 0)
            def _():
                acc[...] += p

            @pl.when(k == nk - 1)
            def _():
                finish(acc[...])

    in_specs, args = [a_spec, b_spec], [a, b]
    tile = spec("2d", (tm, tn), lambda g, i, j, k: (i, j))
    out_specs, out_shape = [o_spec], [jax.ShapeDtypeStruct(o_shape, out_dtype)]
    if has_res:
        in_specs.append(tile)
        args.append(res)
    if has_rms:
        x, gain, dres = rms_bwd
        vec = pl.BlockSpec((1, tn), lambda g, i, j, k: (0, 0))
        in_specs += [tile, vec, tile]
        args += [x, gain.reshape(1, N), dres]
        out_specs.append(vec)
        out_shape.append(jax.ShapeDtypeStruct((1, N), F32))
    outs = _call_with_rider(
        rider, body, name=name, grid=(G, M // tm, N // tn, nk), in_specs=in_specs, args=args, out_specs=out_specs,
        out_shape=out_shape, scratch_shapes=[pltpu.VMEM((tm, tn), F32)] if nk > 1 else [],
        semantics=("parallel", "arbitrary" if has_rms else "parallel", "parallel", "arbitrary"))
    main = (outs[0], outs[1].reshape(N)) if has_rms else outs[0]
    return main if rider is None else (main, outs[len(out_shape):])


def _rms_fwd(x, g, *, name):
    S, Dm = x.shape
    tm = _tile(S, (512, 256))

    def body(x_ref, g_ref, o_ref):
        xv = x_ref[...]
        r = lax.rsqrt(jnp.mean(xv * xv, axis=-1, keepdims=True) + 1e-6)
        o_ref[...] = (xv * r * g_ref[...]).astype(BF16)

    return pl.pallas_call(
        body, name=name, grid=(S // tm,),
        in_specs=[pl.BlockSpec((tm, Dm), lambda i: (i, 0)), pl.BlockSpec((1, Dm), lambda i: (0, 0))],
        out_specs=pl.BlockSpec((tm, Dm), lambda i: (i, 0)), out_shape=jax.ShapeDtypeStruct((S, Dm), BF16),
        compiler_params=_cp("parallel"),
    )(x, g.reshape(1, Dm))


def _rms_bwd(x, g, dh, dres, *, name):
    S, Dm = x.shape
    tm = _tile(S, (512, 256))

    def body(x_ref, g_ref, dh_ref, dr_ref, dx_ref, dg_ref):
        xv, dhv = x_ref[...], dh_ref[...].astype(F32)
        r = lax.rsqrt(jnp.mean(xv * xv, axis=-1, keepdims=True) + 1e-6)
        u = dhv * g_ref[...]
        s = jnp.sum(u * xv, axis=-1, keepdims=True)
        dx_ref[...] = dr_ref[...] + r * u - xv * ((r * r * r) * (s * (1.0 / Dm)))
        part = jnp.sum(dhv * (xv * r), axis=0, keepdims=True)

        @pl.when(pl.program_id(0) == 0)
        def _():
            dg_ref[...] = part

        @pl.when(pl.program_id(0) > 0)
        def _():
            dg_ref[...] += part

    row = pl.BlockSpec((tm, Dm), lambda i: (i, 0))
    vec = pl.BlockSpec((1, Dm), lambda i: (0, 0))
    dx, dg = pl.pallas_call(
        body, name=name, grid=(S // tm,), in_specs=[row, vec, row, row], out_specs=[row, vec],
        out_shape=[jax.ShapeDtypeStruct((S, Dm), F32), jax.ShapeDtypeStruct((1, Dm), F32)],
        compiler_params=_cp("arbitrary"),
    )(x, g.reshape(1, Dm), dh, dres)
    return dx, dg.reshape(Dm)


def _loss_head(x, g, target):
    S, Dm = x.shape
    tm = _tile(S, (512, 256))

    def body(x_ref, g_ref, t_ref, dx_ref, dg_ref, loss_ref):
        xv, gv = x_ref[...], g_ref[...]
        r = lax.rsqrt(jnp.mean(xv * xv, axis=-1, keepdims=True) + 1e-6)
        xn = xv * r
        err = xn * gv - t_ref[...]
        lpart = 0.5 * jnp.sum(jnp.mean(err * err, axis=-1, keepdims=True), axis=0, keepdims=True)
        dy = err * (1.0 / Dm)
        u = dy * gv
        s = jnp.sum(u * xv, axis=-1, keepdims=True)
        dx_ref[...] = r * u - xv * ((r * r * r) * (s * (1.0 / Dm)))
        part = jnp.sum(dy * xn, axis=0, keepdims=True)
        lslab = jnp.broadcast_to(lpart, (8, 128))

        @pl.when(pl.program_id(0) == 0)
        def _():
            dg_ref[...] = part
            loss_ref[...] = lslab

        @pl.when(pl.program_id(0) > 0)
        def _():
            dg_ref[...] += part
            loss_ref[...] += lslab

    row = pl.BlockSpec((tm, Dm), lambda i: (i, 0))
    vec = pl.BlockSpec((1, Dm), lambda i: (0, 0))
    dx, dg, loss = pl.pallas_call(
        body, name="loss_head", grid=(S // tm,), in_specs=[row, vec, row],
        out_specs=[row, vec, pl.BlockSpec((8, 128), lambda i: (0, 0))],
        out_shape=[jax.ShapeDtypeStruct((S, Dm), F32), jax.ShapeDtypeStruct((1, Dm), F32), jax.ShapeDtypeStruct((8, 128), F32)],
        compiler_params=_cp("arbitrary"),
    )(x, g.reshape(1, Dm), target)
    return loss[0, 0], dx, dg.reshape(Dm)


SB_TQ, SB_TK = 256, 256
SB_EXP_FLOOR = -104.0


def _split2(v):
    hi = v.astype(BF16)
    return jnp.concatenate([hi, (v - hi.astype(F32)).astype(BF16)], axis=1)


def _tri2(cmp):
    j = lax.broadcasted_iota(jnp.int32, (2 * SB_TK, SB_TK), 0) % SB_TK
    s = lax.broadcasted_iota(jnp.int32, (2 * SB_TK, SB_TK), 1)
    return cmp(j, s).astype(BF16)


def _sb_scores(qv, kb, k0, q0, tq):
    rows = qv.shape[0]
    z = lax.dot_general(qv, kb, NT, preferred_element_type=F32) * (SB_DH ** -0.5)
    t_pos = q0 + lax.broadcasted_iota(jnp.int32, (rows, SB_TK), 0) % tq
    s_pos = k0 + lax.broadcasted_iota(jnp.int32, (rows, SB_TK), 1)
    valid = s_pos < t_pos
    ls = jnp.minimum(z, 0.0) - jnp.log(1.0 + jnp.exp(-jnp.abs(z)))
    l1m = jnp.where(valid, ls - z, 0.0)
    return z, valid, ls, l1m


SB_PAIRS = SB_HEADS // 2
_Q_BLK, _K_BLK, _V_BLK = 0, SB_PAIRS, 2 * SB_PAIRS


def _wide(x):
    return x if SB_TK == 128 else jnp.concatenate([x] * (SB_TK // 128), axis=1)


def _lanes_of(h, shape):
    lane = lax.broadcasted_iota(jnp.int32, shape, len(shape) - 1)
    return (lane < SB_DH) if h == 0 else (lane >= SB_DH)


def _sb2_fwd(p, rider=None):
    S = p.shape[0]
    tq = min(SB_TQ, S)
    kb_per_q = tq // SB_TK

    def body(q_ref, k_ref, v_ref, o_ref, tot_ref, cnt_ref, qm, acc, c):
        i = pl.program_id(1)
        q0 = i * tq
        later = _tri2(lambda j, s: j > s)
        q2 = q_ref[...]
        for h in range(2):
            qm[h * tq:(h + 1) * tq, :] = jnp.where(_lanes_of(h, q2.shape), q2, 0.0).astype(BF16)
        acc[...] = jnp.zeros_like(acc)
        c[...] = jnp.zeros_like(c)
        nkb = (i + 1) * kb_per_q

        def more(st):
            n, highest = st
            return (n < nkb) & (highest > SB_EXP_FLOOR)

        def step(st):
            n, _ = st
            k0 = pl.multiple_of((nkb - 1 - n) * SB_TK, SB_TK)
            kb, vb = k_ref[pl.ds(k0, SB_TK), :].astype(BF16), v_ref[pl.ds(k0, SB_TK), :].astype(BF16)
            c_old = c[...]
            z, valid, ls, l1m = _sb_scores(qm[...], kb, k0, q0, tq)
            c_new = c_old + jnp.sum(l1m, axis=1, keepdims=True)
            after = jnp.dot(_split2(l1m), later, preferred_element_type=F32)
            a = jnp.where(valid, jnp.exp(ls + after + _wide(c_old)), 0.0)
            av = jnp.dot(a.astype(BF16), vb, preferred_element_type=F32)
            acc[...] += jnp.where(_lanes_of(0, (tq, 128)), av[:tq], av[tq:])
            c[...] = c_new
            return n + 1, jnp.max(c_new)

        n_done, _ = lax.while_loop(more, step, (jnp.int32(0), jnp.float32(0.0)))
        o_ref[...] = acc[...].astype(o_ref.dtype)
        for h in range(2):
            tot_ref[h] = c[h * tq:(h + 1) * tq, :]
        cnt_ref[...] = jnp.full(cnt_ref.shape, n_done.astype(F32))

    col = lambda first: pl.BlockSpec((S, 128), lambda g, i: (0, first + g))
    outs = _call_with_rider(
        rider, body, name="sb_fwd", grid=(SB_PAIRS, S // tq), args=[p, p, p],
        in_specs=[pl.BlockSpec((tq, 128), lambda g, i: (i, _Q_BLK + g)), col(_K_BLK), col(_V_BLK)],
        out_specs=[pl.BlockSpec((tq, 128), lambda g, i: (i, g)), pl.BlockSpec((2, tq, 128), lambda g, i: (g, i, 0)),
                   pl.BlockSpec((None, None, 8, 128), lambda g, i: (g, i, 0, 0))],
        out_shape=[jax.ShapeDtypeStruct((S, BW), BF16), jax.ShapeDtypeStruct((SB_HEADS, S, 128), F32),
                   jax.ShapeDtypeStruct((SB_PAIRS, S // tq, 8, 128), F32)],
        scratch_shapes=[pltpu.VMEM((2 * tq, 128), BF16), pltpu.VMEM((tq, 128), F32), pltpu.VMEM((2 * tq, 128), F32)],
        semantics=("parallel", "parallel"))
    return outs[:3], outs[3:]


def _sb2_bwd(p, dbr, tot, cnt, rider=None):
    S = p.shape[0]
    tq = min(SB_TQ, S)
    kb_per_q = tq // SB_TK
    scale = SB_DH ** -0.5

    def body(q_ref, k_ref, v_ref, do_ref, tot_ref, cnt_ref, dq_ref, dk_ref, dv_ref, qm, dom, tot, dq_acc, pre, gpre):
        i = pl.program_id(1)
        q0 = i * tq
        upto = _tri2(lambda j, s: j <= s)
        before = _tri2(lambda j, s: j < s)

        @pl.when(i == 0)
        def _():
            dk_ref[...] = jnp.zeros_like(dk_ref)
            dv_ref[...] = jnp.zeros_like(dv_ref)

        q2, do2 = q_ref[...], do_ref[...]
        for h in range(2):
            rows = slice(h * tq, (h + 1) * tq)
            qm[rows, :] = jnp.where(_lanes_of(h, q2.shape), q2, 0.0).astype(BF16)
            dom[rows, :] = jnp.where(_lanes_of(h, do2.shape), do2, 0.0).astype(BF16)
            tot[rows, :] = tot_ref[h]
        dq_acc[...] = jnp.zeros_like(dq_acc)
        pre[...] = jnp.zeros_like(pre)
        gpre[...] = jnp.zeros_like(gpre)

        n_done = jnp.max(cnt_ref[...]).astype(jnp.int32)
        first = (i + 1) * kb_per_q - n_done

        def step(n, carry):
            k0 = pl.multiple_of((first + n) * SB_TK, SB_TK)
            kb, vb = k_ref[pl.ds(k0, SB_TK), :].astype(BF16), v_ref[pl.ds(k0, SB_TK), :].astype(BF16)
            pre_o, gpre_o = pre[...], gpre[...]
            z, valid, ls, l1m = _sb_scores(qm[...], kb, k0, q0, tq)
            incl = jnp.dot(_split2(l1m), upto, preferred_element_type=F32)
            rest = _wide(tot[...] - pre_o) - incl
            a = jnp.where(valid, jnp.exp(ls + rest), 0.0)
            da = lax.dot_general(dom[...], vb, NT, preferred_element_type=F32)
            g = a * da
            gbefore = jnp.dot(_split2(g), before, preferred_element_type=F32) + _wide(gpre_o)
            dz = jnp.where(valid, g * jnp.exp(ls - z) - jnp.exp(ls) * gbefore, 0.0) * scale
            dzb = dz.astype(BF16)
            dq_p = jnp.dot(dzb, kb, preferred_element_type=F32)
            dq_acc[...] += jnp.where(_lanes_of(0, (tq, 128)), dq_p[:tq], dq_p[tq:])
            dk_ref[pl.ds(k0, SB_TK), :] += lax.dot_general(dzb, qm[...], TN, preferred_element_type=F32)
            dv_ref[pl.ds(k0, SB_TK), :] += lax.dot_general(a.astype(BF16), dom[...], TN, preferred_element_type=F32)
            pre[...] = pre_o + jnp.sum(l1m, axis=1, keepdims=True)
            gpre[...] = gpre_o + jnp.sum(g, axis=1, keepdims=True)
            return carry

        lax.fori_loop(0, n_done, step, 0)
        dq_ref[...] = dq_acc[...].astype(dq_ref.dtype)

    col = lambda first: pl.BlockSpec((S, 128), lambda g, i: (0, first + g))
    tile = pl.BlockSpec((tq, 128), lambda g, i: (i, g))
    whole = pl.BlockSpec((S, 128), lambda g, i: (0, g))
    outs = _call_with_rider(
        rider, body, name="sb_bwd", grid=(SB_PAIRS, S // tq), args=[p, p, p, dbr, tot, cnt],
        in_specs=[pl.BlockSpec((tq, 128), lambda g, i: (i, _Q_BLK + g)), col(_K_BLK), col(_V_BLK),
                  pl.BlockSpec((None, tq, 128), lambda g, i: (0, i, g)), pl.BlockSpec((2, tq, 128), lambda g, i: (g, i, 0)),
                  pl.BlockSpec((None, None, 8, 128), lambda g, i: (g, i, 0, 0))],
        out_specs=[tile, whole, whole],
        out_shape=[jax.ShapeDtypeStruct((S, BW), BF16), jax.ShapeDtypeStruct((S, BW), F32), jax.ShapeDtypeStruct((S, BW), F32)],
        scratch_shapes=[pltpu.VMEM((2 * tq, 128), BF16), pltpu.VMEM((2 * tq, 128), BF16), pltpu.VMEM((2 * tq, 128), F32),
                        pltpu.VMEM((tq, 128), F32), pltpu.VMEM((2 * tq, 128), F32), pltpu.VMEM((2 * tq, 128), F32)],
        semantics=("parallel", "arbitrary"))
    return outs[:3], outs[3:]


_INV_SQRT2 = 0.7071067811865476
_INV_SQRT2PI = 0.3989422804014327


def _gelu(x):
    return 0.5 * x * (1.0 + lax.erf(x * _INV_SQRT2))


def _gelu_grad(x):
    return 0.5 * (1.0 + lax.erf(x * _INV_SQRT2)) + x * (_INV_SQRT2PI * jnp.exp(-0.5 * x * x))


def _sgu_mask():
    t = lax.broadcasted_iota(jnp.int32, (SGU_LEN, SGU_LEN), 0) // SGU_CHUNK
    s = lax.broadcasted_iota(jnp.int32, (SGU_LEN, SGU_LEN), 1) // SGU_CHUNK
    return t >= s


def _sgu_mask_t():
    t = lax.broadcasted_iota(jnp.int32, (SGU_LEN, SGU_LEN), 0) // SGU_CHUNK
    s = lax.broadcasted_iota(jnp.int32, (SGU_LEN, SGU_LEN), 1) // SGU_CHUNK
    return s >= t


def _sgu_norm(zv, g, b):
    vv = _gelu(zv)
    xc = vv - jnp.mean(vv, axis=-1, keepdims=True)
    rstd = lax.rsqrt(jnp.mean(xc * xc, axis=-1, keepdims=True) + 1e-5)
    xhat = xc * rstd
    return xhat, rstd, xhat * g + b


SGU_TM = 256


def _sgu_fwd(p, ln_g, ln_b, w_s, b_st):
    S = p.shape[0]
    tm = min(SGU_TM, S)

    def body(zu_ref, zv_ref, g_ref, b_ref, w_ref, bs_ref, o_ref):
        u = _gelu(zu_ref[...])
        _, _, vn = _sgu_norm(zv_ref[...], g_ref[...], b_ref[...])
        vnb = vn.astype(BF16)
        mask = _sgu_mask()
        for gi in range(SGU_GROUPS):
            wg = jnp.where(mask, w_ref[gi], 0.0).astype(BF16)
            cols = slice(gi * SGU_GD, (gi + 1) * SGU_GD)
            for ci in range(tm // SGU_LEN):
                rows = slice(ci * SGU_LEN, (ci + 1) * SGU_LEN)
                vm = jnp.dot(wg, vnb[rows, cols], preferred_element_type=F32) + bs_ref[:, gi:gi + 1]
                o_ref[rows, cols] = (u[rows, cols] * vm).astype(BF16)

    vec = pl.BlockSpec((1, BW), lambda i: (0, 0))
    return pl.pallas_call(
        body, name="sgu_fwd", grid=(S // tm,),
        in_specs=[pl.BlockSpec((tm, BW), lambda i: (i, C_Z // BW)), pl.BlockSpec((tm, BW), lambda i: (i, C_Z // BW + 1)), vec, vec,
                  pl.BlockSpec((SGU_GROUPS, SGU_LEN, SGU_LEN), lambda i: (0, 0, 0)), pl.BlockSpec((SGU_LEN, SGU_GROUPS), lambda i: (0, 0))],
        out_specs=pl.BlockSpec((tm, BW), lambda i: (i, 0)), out_shape=jax.ShapeDtypeStruct((S, BW), BF16),
        compiler_params=_cp("parallel"),
    )(p, p, ln_g.reshape(1, BW), ln_b.reshape(1, BW), w_s, b_st)


def _sgu_bwd(p, dyb, ln_g, ln_b, w_s, w_st, b_st):
    S = p.shape[0]
    tm = min(SGU_TM, S)

    def body(zu_ref, zv_ref, dy_ref, g_ref, b_ref, w_ref, wt_ref, bs_ref, dz_ref, dg_ref, db_ref, dw_ref, dbs_ref, dvn):
        first = pl.program_id(0) == 0

        @pl.when(first)
        def _():
            dg_ref[...] = jnp.zeros_like(dg_ref)
            db_ref[...] = jnp.zeros_like(db_ref)
            dw_ref[...] = jnp.zeros_like(dw_ref)
            dbs_ref[...] = jnp.zeros_like(dbs_ref)

        zu, zv, dy = zu_ref[...], zv_ref[...], dy_ref[...].astype(F32)
        u = _gelu(zu)
        xhat, rstd, vn = _sgu_norm(zv, g_ref[...], b_ref[...])
        vnb = vn.astype(BF16)
        mask = _sgu_mask()
        mask_t = _sgu_mask_t()
        for gi in range(SGU_GROUPS):
            wg = jnp.where(mask, w_ref[gi], 0.0).astype(BF16)
            wgt = jnp.where(mask_t, wt_ref[gi], 0.0).astype(BF16)
            cols = slice(gi * SGU_GD, (gi + 1) * SGU_GD)
            for ci in range(tm // SGU_LEN):
                rows = slice(ci * SGU_LEN, (ci + 1) * SGU_LEN)
                vm = jnp.dot(wg, vnb[rows, cols], preferred_element_type=F32) + bs_ref[:, gi:gi + 1]
                dyc = dy[rows, cols]
                dz_ref[rows, cols] = (dyc * vm * _gelu_grad(zu[rows, cols])).astype(BF16)
                dvm = dyc * u[rows, cols]
                dvmb = dvm.astype(BF16)
                dbs_ref[gi] += jnp.broadcast_to(jnp.sum(dvm, axis=1, keepdims=True), (SGU_LEN, SGU_GD))
                dw_ref[gi] += lax.dot_general(dvmb, vnb[rows, cols], NT, preferred_element_type=F32)
                dvn[rows, cols] = jnp.dot(wgt, dvmb, preferred_element_type=F32)
        dvnv = dvn[...]
        dg_ref[...] += jnp.sum(dvnv * xhat, axis=0, keepdims=True)
        db_ref[...] += jnp.sum(dvnv, axis=0, keepdims=True)
        dxh = dvnv * g_ref[...]
        dvv = rstd * (dxh - jnp.mean(dxh, axis=-1, keepdims=True) - xhat * jnp.mean(dxh * xhat, axis=-1, keepdims=True))
        dz_ref[:, BW:] = (dvv * _gelu_grad(zv)).astype(BF16)

        @pl.when(pl.program_id(0) == n_steps - 1)
        def _():
            for gi in range(SGU_GROUPS):
                dw_ref[gi] = jnp.where(mask, dw_ref[gi], 0.0)

    n_steps = S // tm
    vec = pl.BlockSpec((1, BW), lambda i: (0, 0))
    half = lambda c: pl.BlockSpec((tm, BW), lambda i: (i, c))
    wspec = pl.BlockSpec((SGU_GROUPS, SGU_LEN, SGU_LEN), lambda i: (0, 0, 0))
    dz, dg, db, dw, dbs = pl.pallas_call(
        body, name="sgu_bwd", grid=(n_steps,),
        in_specs=[half(C_Z // BW), half(C_Z // BW + 1), half(0), vec, vec, wspec, wspec,
                  pl.BlockSpec((SGU_LEN, SGU_GROUPS), lambda i: (0, 0))],
        out_specs=[pl.BlockSpec((tm, 2 * BW), lambda i: (i, 0)), vec, vec, wspec, wspec],
        out_shape=[jax.ShapeDtypeStruct((S, 2 * BW), BF16), jax.ShapeDtypeStruct((1, BW), F32), jax.ShapeDtypeStruct((1, BW), F32),
                   jax.ShapeDtypeStruct((SGU_GROUPS, SGU_LEN, SGU_LEN), F32), jax.ShapeDtypeStruct((SGU_GROUPS, SGU_LEN, SGU_GD), F32)],
        scratch_shapes=[pltpu.VMEM((tm, BW), F32)],
        compiler_params=_cp("arbitrary"),
    )(p, p, dyb, ln_g.reshape(1, BW), ln_b.reshape(1, BW), w_s, w_st, b_st)
    return dz, dg.reshape(BW), db.reshape(BW), dw, dbs[:, :, 0]


CONV_TC = 128


def _shift_down(y, n):
    rows = lax.broadcasted_iota(jnp.int32, y.shape, 0)
    return jnp.where(rows < n, 0.0, pltpu.roll(y, n, 0))


def _shift_up(y, n):
    rows = lax.broadcasted_iota(jnp.int32, y.shape, 0)
    return jnp.where(rows >= y.shape[0] - n, 0.0, pltpu.roll(y, y.shape[0] - n, 0))


def _conv_specs(S):
    col = lambda c0: pl.BlockSpec((S, CONV_TC), lambda j: (0, c0 // CONV_TC + j))
    return col(C_CB), col(C_CB + BW), col(C_CB + 2 * BW), pl.BlockSpec((3, CONV_TC), lambda j: (0, j)), pl.BlockSpec((S, CONV_TC), lambda j: (0, j))


def _conv_fwd(p, conv_w):
    S = p.shape[0]

    def body(cb_ref, cc_ref, cx_ref, w_ref, o_ref):
        y = cc_ref[...] * cx_ref[...]
        conv = w_ref[0:1, :] * _shift_down(y, 2) + w_ref[1:2, :] * _shift_down(y, 1) + w_ref[2:3, :] * y
        o_ref[...] = (cb_ref[...] * conv).astype(BF16)

    cb, cc, cx, wspec, out = _conv_specs(S)
    return pl.pallas_call(
        body, name="conv_fwd", grid=(BW // CONV_TC,), in_specs=[cb, cc, cx, wspec], out_specs=out,
        out_shape=jax.ShapeDtypeStruct((S, BW), BF16), compiler_params=_cp("parallel"),
    )(p, p, p, conv_w)


def _conv_bwd(p, conv_w, dyc):
    S = p.shape[0]

    def body(cb_ref, cc_ref, cx_ref, w_ref, dy_ref, db_ref, dc_ref, dx_ref, dw_ref):
        cc, cx, dy = cc_ref[...], cx_ref[...], dy_ref[...].astype(F32)
        y = cc * cx
        w0, w1, w2 = w_ref[0:1, :], w_ref[1:2, :], w_ref[2:3, :]
        y1, y2 = _shift_down(y, 1), _shift_down(y, 2)
        conv = w0 * y2 + w1 * y1 + w2 * y
        db_ref[...] = (dy * conv).astype(BF16)
        dconv = dy * cb_ref[...]
        dyy = w2 * dconv + w1 * _shift_up(dconv, 1) + w0 * _shift_up(dconv, 2)
        dc_ref[...] = (dyy * cx).astype(BF16)
        dx_ref[...] = (dyy * cc).astype(BF16)
        dw_ref[0:1, :] = jnp.sum(dconv * y2, axis=0, keepdims=True)
        dw_ref[1:2, :] = jnp.sum(dconv * y1, axis=0, keepdims=True)
        dw_ref[2:3, :] = jnp.sum(dconv * y, axis=0, keepdims=True)

    cb, cc, cx, wspec, out = _conv_specs(S)
    db, dc, dx, dw = pl.pallas_call(
        body, name="conv_bwd", grid=(BW // CONV_TC,), in_specs=[cb, cc, cx, wspec, out],
        out_specs=[out, out, out, wspec],
        out_shape=[jax.ShapeDtypeStruct((S, BW), BF16)] * 3 + [jax.ShapeDtypeStruct((3, BW), F32)],
        compiler_params=_cp("parallel"),
    )(p, p, p, conv_w, dyc)
    return db, dc, dx, dw


def _merge_specs(S, tm):
    gate = lambda n: pl.BlockSpec((tm, D_MODEL), lambda i: (i, C_GATES // D_MODEL + n))
    return [gate(0), gate(1), gate(2)], pl.BlockSpec((3, tm, D_MODEL), lambda i: (0, i, 0)), pl.BlockSpec((tm, D_MODEL), lambda i: (i, 0))


def _branch_merge(br, w_br, p):
    S = p.shape[0]
    tm = _tile(S, (512, 256))

    def body(br_ref, w_ref, g0, g1, g2, bd_ref, o_ref):
        acc = None
        for n, g_ref in enumerate((g0, g1, g2)):
            bdn = jnp.dot(br_ref[n], w_ref[n], preferred_element_type=F32)
            bd_ref[n] = bdn
            term = jax.nn.sigmoid(g_ref[...]) * bdn
            acc = term if acc is None else acc + term
        o_ref[...] = acc.astype(BF16)

    gates, bspec, row = _merge_specs(S, tm)
    return pl.pallas_call(
        body, name="mm_branch", grid=(S // tm,),
        in_specs=[pl.BlockSpec((3, tm, BW), lambda i: (0, i, 0)), pl.BlockSpec((3, BW, D_MODEL), lambda i: (0, 0, 0))] + gates,
        out_specs=[bspec, row], out_shape=[jax.ShapeDtypeStruct((3, S, D_MODEL), F32), jax.ShapeDtypeStruct((S, D_MODEL), BF16)],
        compiler_params=_cp("parallel"),
    )(br, w_br, p, p, p)


def _merge_bwd(p, bd, dm):
    S = p.shape[0]
    tm = _tile(S, (256,))

    def body(g0, g1, g2, b_ref, dm_ref, db_ref, dg_ref):
        dmv = dm_ref[...]
        for n, g_ref in enumerate((g0, g1, g2)):
            sg = jax.nn.sigmoid(g_ref[...])
            db_ref[n] = (dmv * sg).astype(BF16)
            dg_ref[:, n * D_MODEL:(n + 1) * D_MODEL] = (dmv * b_ref[n] * (sg * (1.0 - sg))).astype(BF16)

    gates, bspec, row = _merge_specs(S, tm)
    return pl.pallas_call(
        body, name="merge_bwd", grid=(S // tm,), in_specs=gates + [bspec, row],
        out_specs=[bspec, pl.BlockSpec((tm, 3 * D_MODEL), lambda i: (i, 0))],
        out_shape=[jax.ShapeDtypeStruct((3, S, D_MODEL), BF16), jax.ShapeDtypeStruct((S, 3 * D_MODEL), BF16)],
        compiler_params=_cp("parallel"),
    )(p, p, p, bd, dm)


XA_TM = 512


def _xa_probs(qh, kh):
    s = lax.dot_general(qh, kh, NT, preferred_element_type=F32) * (XA_DH ** -0.5)
    e = jnp.exp(s - jnp.max(s, axis=-1, keepdims=True))
    return e / jnp.sum(e, axis=-1, keepdims=True)


def _xa_fwd(q, kv):
    S = q.shape[0]
    tm = min(XA_TM, S)
    M = kv.shape[1]

    def body(q_ref, kv_ref, o_ref):
        for h in range(XA_HEADS):
            cols = slice(h * XA_DH, (h + 1) * XA_DH)
            pr = _xa_probs(q_ref[:, cols], kv_ref[0, :, cols])
            o_ref[:, cols] = jnp.dot(pr.astype(BF16), kv_ref[1, :, cols], preferred_element_type=F32).astype(BF16)

    row = pl.BlockSpec((tm, D_MODEL), lambda i: (i, 0))
    return pl.pallas_call(
        body, name="xa_fwd", grid=(S // tm,), in_specs=[row, pl.BlockSpec((2, M, D_MODEL), lambda i: (0, 0, 0))], out_specs=row,
        out_shape=jax.ShapeDtypeStruct((S, D_MODEL), BF16), compiler_params=_cp("parallel"),
    )(q, kv)


def _xa_bwd(q, kv, do):
    S = q.shape[0]
    tm = min(XA_TM, S)
    M = kv.shape[1]

    def body(q_ref, kv_ref, do_ref, dq_ref, dkv_ref):
        @pl.when(pl.program_id(0) == 0)
        def _():
            dkv_ref[...] = jnp.zeros_like(dkv_ref)

        for h in range(XA_HEADS):
            cols = slice(h * XA_DH, (h + 1) * XA_DH)
            qh, kh, vh, doh = q_ref[:, cols], kv_ref[0, :, cols], kv_ref[1, :, cols], do_ref[:, cols]
            pr = _xa_probs(qh, kh)
            dkv_ref[1, :, cols] += lax.dot_general(pr.astype(BF16), doh, TN, preferred_element_type=F32)
            dp = lax.dot_general(doh, vh, NT, preferred_element_type=F32)
            ds = (pr * (dp - jnp.sum(dp * pr, axis=-1, keepdims=True)) * (XA_DH ** -0.5)).astype(BF16)
            dq_ref[:, cols] = jnp.dot(ds, kh, preferred_element_type=F32).astype(BF16)
            dkv_ref[0, :, cols] += lax.dot_general(ds, qh, TN, preferred_element_type=F32)

    row = pl.BlockSpec((tm, D_MODEL), lambda i: (i, 0))
    kvs = pl.BlockSpec((2, M, D_MODEL), lambda i: (0, 0, 0))
    return pl.pallas_call(
        body, name="xa_bwd", grid=(S // tm,), in_specs=[row, kvs, row], out_specs=[row, kvs],
        out_shape=[jax.ShapeDtypeStruct((S, D_MODEL), BF16), jax.ShapeDtypeStruct((2, M, D_MODEL), F32)],
        compiler_params=_cp("arbitrary"),
    )(q, kv, do)


def _ffn_up(h3, w_gu, rider=None):
    S = h3.shape[0]
    tm = _tile(S, (1024, 512, 256))

    def body(h_ref, w_ref, ab_ref, hh_ref):
        h = h_ref[...]
        a = jnp.dot(h, w_ref[0], preferred_element_type=F32)
        b = jnp.dot(h, w_ref[1], preferred_element_type=F32)
        ab_ref[0] = a
        ab_ref[1] = b
        hh_ref[...] = (a * jax.nn.sigmoid(a) * b).astype(BF16)

    pair = pl.BlockSpec((None, 2, tm, FFN_SH), lambda j, i: (j, 0, i, 0))
    outs = _call_with_rider(
        rider, body, name="mm_gu", grid=(N_CHIPS, S // tm), args=[h3, w_gu],
        in_specs=[pl.BlockSpec((tm, D_MODEL), lambda j, i: (i, 0)), pl.BlockSpec((None, 2, D_MODEL, FFN_SH), lambda j, i: (j, 0, 0, 0))],
        out_specs=[pair, pl.BlockSpec((None, tm, FFN_SH), lambda j, i: (j, i, 0))],
        out_shape=[jax.ShapeDtypeStruct((N_CHIPS, 2, S, FFN_SH), F32), jax.ShapeDtypeStruct((N_CHIPS, S, FFN_SH), BF16)],
        scratch_shapes=[], semantics=("parallel", "parallel"))
    return outs[0], outs[1], outs[2:]


def _ffn_down_bwd(dx3, w_dn, ab):
    S = dx3.shape[0]
    tm = _tile(S, (1024, 512, 256))

    def body(dx_ref, w_ref, ab_ref, o_ref):
        d = lax.dot_general(dx_ref[...].astype(BF16), w_ref[...], NT, preferred_element_type=F32)
        a, b = ab_ref[0], ab_ref[1]
        sg = jax.nn.sigmoid(a)
        o_ref[0] = (d * b * (sg * (1.0 + a * (1.0 - sg)))).astype(BF16)
        o_ref[1] = (d * (a * sg)).astype(BF16)

    pair = pl.BlockSpec((None, 2, tm, FFN_SH), lambda j, i: (j, 0, i, 0))
    return pl.pallas_call(
        body, name="mm_down_dx", grid=(N_CHIPS, S // tm),
        in_specs=[pl.BlockSpec((tm, D_MODEL), lambda j, i: (i, 0)), pl.BlockSpec((None, FFN_SH, D_MODEL), lambda j, i: (j, 0, 0)), pair],
        out_specs=pair, out_shape=jax.ShapeDtypeStruct(ab.shape, BF16), compiler_params=_cp("parallel", "parallel"),
    )(dx3, w_dn, ab)


def _reduce_adam(parts, w, m, v, *, name):
    shape = w.shape
    C = shape[-1]
    R = math.prod(shape[:-1])
    tm = _rows(R, 4 * C)
    n = len(parts)
    c1, c2 = 1.0 - ADAM_B1 ** ADAM_STEP, 1.0 - ADAM_B2 ** ADAM_STEP

    def body(*refs):
        g = refs[0][...]
        for r in refs[1:n]:
            g = g + r[...]
        w_ref, m_ref, v_ref, go, do, mo, vo = refs[n:]
        mn = ADAM_B1 * m_ref[...] + (1.0 - ADAM_B1) * g
        vn = ADAM_B2 * v_ref[...] + (1.0 - ADAM_B2) * (g * g)
        go[...] = g
        do[...] = -ADAM_LR * ((mn / c1) / (jnp.sqrt(vn / c2) + ADAM_EPS) + ADAM_WD * w_ref[...])
        mo[...] = mn
        vo[...] = vn

    row = pl.BlockSpec((tm, C), lambda i: (i, 0))
    outs = pl.pallas_call(
        body, name=name, grid=(R // tm,), in_specs=[row] * (n + 3), out_specs=[row] * 4,
        out_shape=[jax.ShapeDtypeStruct((R, C), F32)] * 4, compiler_params=_cp("parallel"),
    )(*[a.reshape(R, C) for a in (*parts, w, m, v)])
    return tuple(o.reshape(shape) for o in outs)


_VIEW = {
    "w_in": ((1024, 7168), (1024, 1792), 256, lambda i, b: (i, b)),
    "w_br": ((1536, 1024), (1536, 256), 512, lambda i, b: (i, b)),
    "w_sq": ((5120, 1024), (1280, 1024), 256, lambda i, b: (4 * i + b, 0)),
    "w_gu": ((8192, 704), (2048, 704), 512, lambda i, b: (4 * b + i, 0)),
    "w_dn": ((2816, 1024), (704, 1024), 352, lambda i, b: (2 * b + i, 0)),
    "conv_w": ((8, 512), (8, 128), 8, lambda i, b: (0, b)),
}


def _scalar(v):
    return jnp.asarray(v, jnp.int32).reshape(1)


def _place(name, local, b):
    full2, sh2, tm, idx = _VIEW[name]
    C = sh2[1]
    dt = local.dtype if name == "conv_w" else BF16

    def body(b_ref, x_ref, o0_ref, o1_ref):
        o0_ref[...] = x_ref[0].astype(dt)
        o1_ref[...] = x_ref[1].astype(dt)

    place = pl.BlockSpec((tm, C), lambda i, bs: idx(i, bs[0]))
    outs = pl.pallas_call(
        body, name="place_" + name,
        grid_spec=pltpu.PrefetchScalarGridSpec(num_scalar_prefetch=1, grid=(sh2[0] // tm,),
                                               in_specs=[pl.BlockSpec((DEPTH, tm, C), lambda i, bs: (0, i, 0))], out_specs=[place, place]),
        out_shape=[jax.ShapeDtypeStruct(full2, dt)] * 2, compiler_params=_cp("arbitrary"),
    )(_scalar(b), local.reshape((DEPTH,) + sh2))
    return [o.reshape(_FULL_SHAPE[name]) for o in outs]


def _add_owner(name, g, land, own):
    shape = g.shape
    C = shape[-1]
    R = math.prod(shape[:-1])
    tm = _rows(R, 4 * C)

    def body(s_ref, g_ref, l_ref, o_ref):
        @pl.when(s_ref[0] != 0)
        def _():
            o_ref[...] = (g_ref[...].astype(F32) + l_ref[...].astype(F32)).astype(BF16)

        @pl.when(s_ref[0] == 0)
        def _():
            o_ref[...] = jnp.zeros_like(o_ref)

    pick = pl.BlockSpec((tm, C), lambda i, s: (jnp.where(s[0] != 0, i, 0), 0))
    return pl.pallas_call(
        body, name="presum_" + name,
        grid_spec=pltpu.PrefetchScalarGridSpec(num_scalar_prefetch=1, grid=(R // tm,), in_specs=[pick, pick],
                                               out_specs=pl.BlockSpec((tm, C), lambda i, s: (i, 0))),
        out_shape=jax.ShapeDtypeStruct((R, C), BF16), compiler_params=_cp("arbitrary"),
    )(_scalar(own), g.reshape(R, C), land.reshape(R, C)).reshape(shape)


def _sum_chips(name, slots, part, b, own):
    full2, sh2, tm, idx = _VIEW[name]
    C = sh2[1]

    def body(s_ref, slot_ref, own_ref, o_ref):
        @pl.when(s_ref[1] != 0)
        def _():
            o_ref[...] = ((slot_ref[0].astype(F32) + slot_ref[1].astype(F32)) + slot_ref[2].astype(F32)) + own_ref[...].astype(F32)

        @pl.when(s_ref[1] == 0)
        def _():
            o_ref[...] = jnp.zeros_like(o_ref)

    return pl.pallas_call(
        body, name="sum_chips_" + name,
        grid_spec=pltpu.PrefetchScalarGridSpec(
            num_scalar_prefetch=1, grid=(sh2[0] // tm,),
            in_specs=[pl.BlockSpec((3, tm, C), lambda i, s: (0, jnp.where(s[1] != 0, i, 0), 0)),
                      pl.BlockSpec((tm, C), lambda i, s: idx(jnp.where(s[1] != 0, i, 0), s[0]))],
            out_specs=pl.BlockSpec((tm, C), lambda i, s: (i, 0))),
        out_shape=jax.ShapeDtypeStruct(sh2, F32), compiler_params=_cp("arbitrary"),
    )(jnp.stack([jnp.asarray(b, jnp.int32), jnp.asarray(own, jnp.int32)]), slots.reshape((3,) + sh2),
      part.reshape(full2)).reshape(_SHARD_SHAPE[name])


def _adam_layers(mine, theirs, c, w, m, v, *, name):
    shape = w.shape
    C = shape[-1]
    R = math.prod(shape[1:-1])
    tm = _rows(R, 4 * C)
    c1, c2 = 1.0 - ADAM_B1 ** ADAM_STEP, 1.0 - ADAM_B2 ** ADAM_STEP

    def body(c_ref, m0_ref, m1_ref, t_ref, w_ref, m_ref, v_ref, go, do, mo, vo):
        layer = pl.program_id(0)
        g = jnp.where(layer == c_ref[0], jnp.where(layer == 0, m0_ref[...], m1_ref[...]), t_ref[...])
        mn = ADAM_B1 * m_ref[...] + (1.0 - ADAM_B1) * g
        vn = ADAM_B2 * v_ref[...] + (1.0 - ADAM_B2) * (g * g)
        go[...] = g
        do[...] = -ADAM_LR * ((mn / c1) / (jnp.sqrt(vn / c2) + ADAM_EPS) + ADAM_WD * w_ref[...])
        mo[...] = mn
        vo[...] = vn

    def own(layer):
        return pl.BlockSpec((tm, C), lambda l, i, cs: (jnp.where((l == layer) & (cs[0] == layer), i, 0), 0))

    recv = pl.BlockSpec((tm, C), lambda l, i, cs: (jnp.where(l == cs[0], 0, i), 0))
    row = pl.BlockSpec((None, tm, C), lambda l, i, cs: (l, i, 0))
    outs = pl.pallas_call(
        body, name=name,
        grid_spec=pltpu.PrefetchScalarGridSpec(num_scalar_prefetch=1, grid=(DEPTH, R // tm),
                                               in_specs=[own(0), own(1), recv, row, row, row], out_specs=[row] * 4),
        out_shape=[jax.ShapeDtypeStruct((DEPTH, R, C), F32)] * 4, compiler_params=_cp("arbitrary", "arbitrary"),
    )(_scalar(c), mine[0].reshape(R, C), mine[1].reshape(R, C), theirs.reshape(R, C), *[t.reshape(DEPTH, R, C) for t in (w, m, v)])
    return tuple(o.reshape(shape) for o in outs)


def _take_weights(wl, names, landed):
    for n, t in zip(names, landed):
        wl[n] = t[:3] if n == "conv_w" else t.reshape(2 * N_CHIPS, D_MODEL, FFN_SH) if n == "w_gu" else t


_GATHER_LATE = ["w_br", "conv_w", "w_sq"]
_GATHER_LAST = ["w_gu", "w_dn"]


def _layer_fwd(x, mem, wl, ride=None):
    S = x.shape[0]
    sv = {"x": x}
    wl = dict(wl)
    h1 = _rms_fwd(x, wl["norm_mix_g"], name="rms_mix")
    if ride is None:
        ride = {"mm_in": None, "sb": None, "mm_gu": None}
        p = _mm(h1, wl["w_in"], mode="nn", name="mm_in")
    else:
        p, landed = _mm(h1, wl["w_in"], mode="nn", name="mm_in", rider=ride["mm_in"])
        _take_weights(wl, _GATHER_LATE, landed)
    (ya, tot, cnt), landed = _sb2_fwd(p, ride["sb"])
    _take_weights(wl, _GATHER_LAST, landed)
    b_st = wl["b_spatial"].T
    yb = _sgu_fwd(p, wl["sgu_ln_g"], wl["sgu_ln_b"], wl["w_spatial"], b_st)
    yc = _conv_fwd(p, wl["conv_w"])
    br = jnp.stack([ya, yb, yc])
    bd, merged = _branch_merge(br, wl["w_br"], p)
    x1 = _mm(merged, wl["w_sq"][0], mode="nn", res=x, name="mm_out")
    h2 = _rms_fwd(x1, wl["norm_xa_g"], name="rms_xa")
    qx = _mm(h2, wl["w_sq"][1], mode="nn", out_dtype=BF16, name="mm_q")
    mn = _rms_fwd(mem, wl["mem_norm_g"], name="rms_mem")
    kv = _mm(mn, wl["w_sq"][3:5], mode="nn", b_kind="batch", out_dtype=BF16, name="mm_kv")
    o = _xa_fwd(qx, kv)
    x2 = _mm(o, wl["w_sq"][2], mode="nn", res=x1, name="mm_o")
    h3 = _rms_fwd(x2, wl["norm_ffn_g"], name="rms_ffn")
    ab, hh, rode = _ffn_up(h3, wl["w_gu"].reshape(N_CHIPS, 2, D_MODEL, FFN_SH), ride["mm_gu"])
    x3 = _mm(hh, wl["w_dn"], mode="nn", a_kind="kchunk", b_kind="kchunk", res=x2, name="mm_down")
    sv.update(h1=h1, p=p, tot=tot, cnt=cnt, br=br, bd=bd, merged=merged, x1=x1, h2=h2, qx=qx, mn=mn, kv=kv, o=o,
              x2=x2, h3=h3, ab=ab, hh=hh, b_st=b_st, wl=wl)
    return x3, sv, rode


class _GradPipe:
    def __init__(self, ci, bi):
        self.ci, self.bi, self.queue = ci, bi, []
        self.part, self.slots = [dict(), dict()], [dict(), dict()]

    def to_owner(self, layer, names, g):
        def arrived(land):
            own = (self.ci == layer).astype(jnp.int32)
            part = {n: _add_owner(n, g[n], t, own) for n, t in zip(names, land)}
            self.part[layer].update(part)
            self.queue.append((layer, names, part))

        return _presum_rider(layer, {n: g[n] for n in names}), arrived

    def exchange(self):
        if not self.queue:
            return None
        layer, names, part = self.queue.pop(0)
        return _shard_rider(layer, part), lambda slots: self.slots[layer].update(zip(names, slots))

    def drain(self, also):
        landed = None
        while self.queue or landed is None:
            job = self.exchange()
            both = _join([job[0] if job else None, also if landed is None else None])
            parts = both.split(_run_rider(both, name="grad_exchange_last"))
            if job:
                job[1](parts[0])
            if landed is None:
                landed = parts[-1]
        return landed

    def reduced(self, layer):
        own = (self.ci == layer).astype(jnp.int32)
        return {n: _sum_chips(n, self.slots[layer][n], self.part[layer][n], self.bi, own) for n in _BIG}


def _layer_bwd(dx3, mem, sv, layer=None, pipe=None):
    S = dx3.shape[0]
    p, wl = sv["p"], sv["wl"]
    g = {}

    def mm(*args, job=None, **kw):
        if job is None:
            return _mm(*args, **kw)
        out, landed = _mm(*args, **kw, rider=job[0])
        job[1](landed)
        return out

    to_owner = (lambda names: pipe.to_owner(layer, names, g)) if pipe else (lambda names: None)
    exchange = pipe.exchange if pipe else (lambda: None)

    g["w_dn"] = _mm(sv["hh"], dx3, mode="tn", a_kind="batch", out_dtype=BF16, name="mm_down_dw")
    dab = _ffn_down_bwd(dx3, wl["w_dn"], sv["ab"]).reshape(2 * N_CHIPS, S, FFN_SH)
    g["w_gu"] = mm(sv["h3"], dab, mode="tn", b_kind="batch", out_dtype=BF16, name="mm_gu_dw", job=exchange()).reshape(_FULL_SHAPE["w_gu"])
    dx2, g["norm_ffn_g"] = mm(dab, wl["w_gu"], mode="nt", a_kind="kchunk", b_kind="kchunk", name="mm_gu_dx",
                               rms_bwd=(sv["x2"], wl["norm_ffn_g"], dx3), job=to_owner(["w_dn", "w_gu"]))
    do = _mm(dx2, wl["w_sq"][2], mode="nt", out_dtype=BF16, name="mm_o_dx")
    dw_o = _mm(sv["o"], dx2, mode="tn", out_dtype=BF16, name="mm_o_dw")
    dq, dkv = _xa_bwd(sv["qx"], sv["kv"], do)
    dw_q = _mm(sv["h2"], dq, mode="tn", out_dtype=BF16, name="mm_q_dw")
    dx1, g["norm_xa_g"] = _mm(dq, wl["w_sq"][1], mode="nt", name="mm_q_dx", rms_bwd=(sv["x1"], wl["norm_xa_g"], dx2))
    dw_kv = _mm(sv["mn"], dkv, mode="tn", b_kind="batch", out_dtype=BF16, name="mm_kv_dw")
    dmn = _mm(dkv, wl["w_sq"][3:5], mode="nt", a_kind="kchunk", b_kind="kchunk", name="mm_kv_dx")
    _, g["mem_norm_g"] = _rms_bwd(mem, wl["mem_norm_g"], dmn, jnp.zeros_like(mem), name="rms_mem_bwd")
    dm = _mm(dx1, wl["w_sq"][0], mode="nt", name="mm_out_dx")
    dw_out = _mm(sv["merged"], dx1, mode="tn", out_dtype=BF16, name="mm_out_dw")
    g["w_sq"] = jnp.concatenate([jnp.stack([dw_out, dw_q, dw_o]), dw_kv])
    dbd, dgates = _merge_bwd(p, sv["bd"], dm)
    dbr = mm(dbd, wl["w_br"], mode="nt", a_kind="batch", b_kind="batch", name="mm_branch_dx", job=to_owner(["w_sq"]))
    g["w_br"] = _mm(sv["br"], dbd, mode="tn", a_kind="batch", b_kind="batch", out_dtype=BF16, name="mm_branch_dw")
    job = exchange()
    (dq, dk, dv), landed = _sb2_bwd(p, dbr, sv["tot"], sv["cnt"], job[0] if job else None)
    if job:
        job[1](landed)
    dz, g["sgu_ln_g"], g["sgu_ln_b"], g["w_spatial"], g["b_spatial"] = _sgu_bwd(
        p, dbr[1], wl["sgu_ln_g"], wl["sgu_ln_b"], wl["w_spatial"], wl["w_spatial"].transpose(0, 2, 1), sv["b_st"])
    dcb, dcc, dcx, g["conv_w"] = _conv_bwd(p, wl["conv_w"], dbr[2])
    dp = jnp.concatenate([dq, dk.astype(BF16), dv.astype(BF16), dz, dcb, dcc, dcx, dgates], axis=1)
    g["w_in"] = mm(sv["h1"], dp, mode="tn", out_dtype=BF16, name="mm_in_dw", job=exchange())
    dx, g["norm_mix_g"] = mm(dp, wl["w_in"], mode="nt", name="mm_in_dx", rms_bwd=(sv["x"], wl["norm_mix_g"], dx1),
                             job=to_owner(["w_br", "w_in"]))
    return dx, g


def _local_step(x, mem, target, layers, final_g):
    h, saved = x, []
    for wl in layers:
        h, sv, _ = _layer_fwd(h, mem, wl)
        saved.append(sv)
    loss, dx, d_final = _loss_head(h, final_g, target)
    grads = [None] * len(layers)
    for l in reversed(range(len(layers))):
        dx, grads[l] = _layer_bwd(dx, mem, saved[l])
    return loss, dx, grads, d_final


_ALL = slice(None)
CONV_ROWS = 8
_SHARD = {
    "w_in": lambda b: (_ALL, pl.ds(1792 * b, 1792)),
    "w_br": lambda b: (_ALL, _ALL, pl.ds(256 * b, 256)),
    "w_sq": lambda b: (_ALL, pl.ds(256 * b, 256), _ALL),
    "w_gu": lambda b: (b,),
    "w_dn": lambda b: (b,),
    "conv_w": lambda b: (_ALL, pl.ds(128 * b, 128)),
}
_FULL_SHAPE = {"w_in": (1024, 7168), "w_br": (3, 512, 1024), "w_sq": (5, 1024, 1024), "w_gu": (4, 2, 1024, 704),
               "w_dn": (4, 704, 1024), "conv_w": (CONV_ROWS, 512)}
_SHARD_SHAPE = {"w_in": (1024, 1792), "w_br": (3, 512, 256), "w_sq": (5, 256, 1024), "w_gu": (2, 1024, 704),
                "w_dn": (704, 1024), "conv_w": (CONV_ROWS, 128)}


def _pos():
    return lax.axis_index("x"), lax.axis_index("y"), lax.axis_index("c")


def _per_chip(fn):
    x, y, _ = _pos()
    for x0 in (0, 1):
        for y0 in (0, 1):
            @pl.when((x == x0) & (y == y0))
            def _():
                fn(x0, y0)


def _other_chips(x0, y0):
    return [(1 - x0, y0), (x0, 1 - y0), (1 - x0, 1 - y0)]


def _rcopy(src, dst, ssem, rsem, dev):
    return pltpu.make_async_remote_copy(src_ref=src, dst_ref=dst, send_sem=ssem, recv_sem=rsem, device_id=dev, device_id_type=MESH)


def _dma_sems(n):
    return pltpu.SemaphoreType.DMA((n,))


def _gather_rider(layer, placed):
    names = list(placed)
    n = len(names)
    shard = lambda refs, a, b: refs[a].at[_SHARD[names[a]](b)]

    def start(ins, outs, send, recv):
        @pl.when(lax.axis_index("c") == layer)
        def _():
            def run(x0, y0):
                for kk, (px, py) in enumerate(_other_chips(x0, y0)):
                    for a in range(n):
                        own = shard(outs, a, 2 * x0 + y0)
                        _rcopy(own, own, send.at[6 * a + kk], recv.at[6 * a + kk], (px, py, layer)).start()

            _per_chip(run)

    def passing(outs, send, recv, a, kk, bp, x0, y0):
        landed = shard(outs, a, bp)
        return _rcopy(landed, landed, send.at[6 * a + 3 + kk], recv.at[6 * a + 3 + kk], (x0, y0, 1 - layer))

    def middle(ins, outs, send, recv):
        @pl.when(lax.axis_index("c") == layer)
        def _():
            def run(x0, y0):
                for kk, (px, py) in enumerate(_other_chips(x0, y0)):
                    for a in range(n):
                        landed = shard(outs, a, 2 * px + py)
                        _rcopy(landed, landed, send.at[6 * a + kk], recv.at[6 * a + kk], (px, py, layer)).wait_recv()
                        passing(outs, send, recv, a, kk, 2 * px + py, x0, y0).start()

            _per_chip(run)

    def finish(ins, outs, send, recv):
        c = lax.axis_index("c")

        def run(x0, y0):
            chips = _other_chips(x0, y0)

            @pl.when(c == layer)
            def _():
                for kk, (px, py) in enumerate(chips):
                    for a in range(n):
                        own = shard(outs, a, 2 * x0 + y0)
                        _rcopy(own, own, send.at[6 * a + kk], recv.at[6 * a + kk], (px, py, layer)).wait_send()
                        passing(outs, send, recv, a, kk, 2 * px + py, x0, y0).wait_send()

            @pl.when(c != layer)
            def _():
                for kk, (px, py) in enumerate(chips):
                    for a in range(n):
                        got = shard(outs, a, 2 * px + py)
                        _rcopy(got, got, send.at[6 * a + 3 + kk], recv.at[6 * a + 3 + kk], (x0, y0, layer)).wait_recv()

        _per_chip(run)

    arrs = [placed[nm] for nm in names]
    return _Rider(arrs, [jax.ShapeDtypeStruct(t.shape, t.dtype) for t in arrs], 6 * n, start, finish, alias={a: a for a in range(n)}, middle=middle)


def _presum_rider(layer, grads):
    names = list(grads)
    n = len(names)

    def start(ins, outs, send, recv):
        x, y, c = _pos()

        @pl.when(c != layer)
        def _():
            for a in range(n):
                _rcopy(ins[a], outs[a], send.at[a], recv.at[a], (x, y, layer)).start()

    def finish(ins, outs, send, recv):
        x, y, c = _pos()

        @pl.when(c != layer)
        def _():
            for a in range(n):
                _rcopy(ins[a], outs[a], send.at[a], recv.at[a], (x, y, layer)).wait_send()

        @pl.when(c == layer)
        def _():
            for a in range(n):
                _rcopy(outs[a], outs[a], send.at[a], recv.at[a], (x, y, 1 - layer)).wait_recv()

    arrs = [grads[nm] for nm in names]
    return _Rider(arrs, [jax.ShapeDtypeStruct(t.shape, t.dtype) for t in arrs], n, start, finish)


def _shard_rider(layer, part):
    names = list(part)
    n = len(names)

    def each(fn):
        @pl.when(lax.axis_index("c") == layer)
        def _():
            def run(x0, y0):
                for kk, (px, py) in enumerate(_other_chips(x0, y0)):
                    for a in range(n):
                        fn(a, kk, 2 * px + py, (px, py, layer))

            _per_chip(run)

    def start(ins, outs, send, recv):
        each(lambda a, kk, bp, peer: _rcopy(ins[a].at[_SHARD[names[a]](bp)], outs[a].at[kk], send.at[3 * a + kk], recv.at[3 * a + kk], peer).start())

    def finish(ins, outs, send, recv):
        each(lambda a, kk, bp, peer: _rcopy(outs[a].at[kk], outs[a].at[kk], send.at[3 * a + kk], recv.at[3 * a + kk], peer).wait_recv())
        each(lambda a, kk, bp, peer: _rcopy(ins[a].at[_SHARD[names[a]](bp)], outs[a].at[kk], send.at[3 * a + kk], recv.at[3 * a + kk], peer).wait_send())

    return _Rider([part[nm] for nm in names], [jax.ShapeDtypeStruct((N_CHIPS - 1,) + _SHARD_SHAPE[nm], part[nm].dtype) for nm in names],
                  3 * n, start, finish)


def _sibling_exchange(mine0, mine1):
    names = list(mine0)
    n = len(names)

    def body(*refs):
        l0, l1, outs = refs[:n], refs[n:2 * n], refs[2 * n:3 * n]
        send, recv = refs[3 * n:]
        x, y, c = _pos()
        for c0 in (0, 1):
            @pl.when(c == c0)
            def _():
                srcs = l0 if c0 == 0 else l1
                cps = [_rcopy(srcs[a], outs[a], send.at[a], recv.at[a], (x, y, 1 - c0)) for a in range(n)]
                for cp in cps:
                    cp.start()
                for cp in cps:
                    cp.wait()

    outs = pl.pallas_call(
        body, name="grad_sibling_exchange", in_specs=[ANY] * (2 * n), out_specs=[ANY] * n,
        out_shape=[jax.ShapeDtypeStruct(mine0[nm].shape, mine0[nm].dtype) for nm in names],
        scratch_shapes=[_dma_sems(n), _dma_sems(n)],
    )(*[mine0[nm] for nm in names], *[mine1[nm] for nm in names])
    return dict(zip(names, outs))


def _small_rider(pack):
    flips = [(fx, fy, fc) for fx in (0, 1) for fy in (0, 1) for fc in (0, 1) if fx or fy or fc]

    def peers():
        x, y, c = _pos()
        return 4 * x + 2 * y + c, [(x ^ fx, y ^ fy, c ^ fc) for fx, fy, fc in flips]

    def start(ins, outs, send, recv):
        me, to = peers()
        for k, peer in enumerate(to):
            _rcopy(ins[0], outs[0].at[me], send.at[k], recv.at[k], peer).start()

    def finish(ins, outs, send, recv):
        me, to = peers()
        for k, (px, py, pc) in enumerate(to):
            slot = outs[0].at[4 * px + 2 * py + pc]
            _rcopy(slot, slot, send.at[k], recv.at[k], (px, py, pc)).wait_recv()
        for k, peer in enumerate(to):
            _rcopy(ins[0], outs[0].at[me], send.at[k], recv.at[k], peer).wait_send()

    return _Rider([pack], [jax.ShapeDtypeStruct((8,) + pack.shape, pack.dtype)], len(flips), start, finish)


def _sum_devices(gathered, pack, me):
    n, R, C = gathered.shape

    def body(me_ref, r_ref, own_ref, o_ref):
        acc = jnp.where(me_ref[0] == 0, own_ref[...], r_ref[0])
        for s in range(1, n):
            acc = acc + jnp.where(me_ref[0] == s, own_ref[...], r_ref[s])
        o_ref[...] = acc

    return pl.pallas_call(
        body, name="sum_devices_small",
        grid_spec=pltpu.PrefetchScalarGridSpec(num_scalar_prefetch=1, grid=(1,),
                                               in_specs=[pl.BlockSpec((n, R, C), lambda i, m: (0, 0, 0)), pl.BlockSpec((R, C), lambda i, m: (0, 0))],
                                               out_specs=pl.BlockSpec((R, C), lambda i, m: (0, 0))),
        out_shape=jax.ShapeDtypeStruct((R, C), F32), compiler_params=_cp("arbitrary"),
    )(_scalar(me), gathered, pack)


_WEIGHTS = ["norm_mix_g", "w_in", "sgu_ln_g", "sgu_ln_b", "w_spatial", "b_spatial", "conv_w", "w_branch", "w_out", "norm_xa_g",
            "mem_norm_g", "w_q_xa", "w_k_xa", "w_v_xa", "w_o_xa", "norm_ffn_g", "w_gate_ffn", "w_up_ffn", "w_down_ffn", "final_g"]
_REPLICATED = ["norm_mix_g", "sgu_ln_g", "sgu_ln_b", "w_spatial", "b_spatial", "norm_xa_g", "mem_norm_g", "norm_ffn_g", "final_g"]
_SQUARE = ["w_out", "w_q_xa", "w_o_xa", "w_k_xa", "w_v_xa"]
_BIG = ["w_in", "w_br", "w_sq", "w_gu", "w_dn"]


def _pack(arrs):
    return jnp.concatenate([a.reshape(-1) for a in arrs]).reshape(-1, 128)


def _step(a):
    w = {n: a[n] for n in _WEIGHTS}
    x, mem, target = a["x"][0], a["mem"][0], a["loss_target"][0]
    xi, yi, ci = _pos()
    bi = 2 * xi + yi
    groups = list(_FULL_SHAPE)

    local = {"w_in": w["w_in"], "w_br": w["w_branch"], "w_sq": jnp.stack([w[n] for n in _SQUARE], axis=1),
             "w_gu": jnp.stack([w["w_gate_ffn"], w["w_up_ffn"]], axis=1), "w_dn": w["w_down_ffn"],
             "conv_w": jnp.pad(w["conv_w"], ((0, 0), (0, CONV_ROWS - 3), (0, 0)))}
    placed = [dict(), dict()]
    for n in groups:
        placed[0][n], placed[1][n] = _place(n, local[n], bi)

    def gather(l, names):
        return _gather_rider(l, {n: placed[l][n] for n in names})

    def start_of(l, w_in):
        return {"w_in": w_in, **{n: w[n][l] for n in _REPLICATED if n != "final_g"}}

    w_in0, = _run_rider(gather(0, ["w_in"]), name="gather_first")
    h, sv0, (w_in1,) = _layer_fwd(x, mem, start_of(0, w_in0), {"mm_in": gather(0, _GATHER_LATE), "sb": gather(0, _GATHER_LAST),
                                                                "mm_gu": gather(1, ["w_in"])})
    h, sv1, _ = _layer_fwd(h, mem, start_of(1, w_in1), {"mm_in": gather(1, _GATHER_LATE), "sb": gather(1, _GATHER_LAST), "mm_gu": None})
    loss, dx, d_final = _loss_head(h, w["final_g"], target)
    loss = lax.psum(loss, ("x", "y", "c"))

    pipe = _GradPipe(ci, bi)
    dx, g1 = _layer_bwd(dx, mem, sv1, 1, pipe)
    dx, g0 = _layer_bwd(dx, mem, sv0, 0, pipe)
    grads = [g0, g1]
    small = {n: jnp.stack([g[n] for g in grads]) for n in _REPLICATED if n != "final_g"}
    small["final_g"] = d_final
    conv_g = jnp.stack([g["conv_w"] for g in grads])
    small_pack = _pack([small[n] for n in _REPLICATED] + [conv_g])
    gathered, = pipe.drain(_small_rider(small_pack))
    mine = [pipe.reduced(0), pipe.reduced(1)]
    theirs = _sibling_exchange(mine[0], mine[1])

    out = {}

    def adam_layers(name, group, pick=None):
        sel = (lambda t: t[group]) if pick is None else (lambda t: t[group][pick])
        out[name] = _adam_layers([sel(mine[0]), sel(mine[1])], sel(theirs), ci, w[name], a["m_" + name], a["v_" + name], name="adam_" + name)

    adam_layers("w_in", "w_in")
    adam_layers("w_branch", "w_br")
    for t, n in enumerate(_SQUARE):
        adam_layers(n, "w_sq", t)
    adam_layers("w_gate_ffn", "w_gu", 0)
    adam_layers("w_up_ffn", "w_gu", 1)
    adam_layers("w_down_ffn", "w_dn")

    def adam(name, g):
        out[name] = _reduce_adam([g], w[name], a["m_" + name], a["v_" + name], name="adam_" + name)

    n_rep = sum(w[n].size for n in _REPLICATED) // 128
    summed = _sum_devices(gathered, small_pack, 4 * xi + 2 * yi + ci)
    res = _reduce_adam([summed[:n_rep]], _pack([w[n] for n in _REPLICATED]), _pack([a["m_" + n] for n in _REPLICATED]),
                       _pack([a["v_" + n] for n in _REPLICATED]), name="adam_replicated")
    off = 0
    for n in _REPLICATED:
        out[n] = tuple(r.reshape(-1)[off:off + w[n].size].reshape(w[n].shape) for r in res)
        off += w[n].size
    conv_full = summed[n_rep:].reshape(conv_g.shape)
    adam("conv_w", lax.dynamic_slice_in_dim(conv_full, (2 * xi + yi) * 128, 128, axis=2))

    return (loss, dx[None], *[out[n][k] for k in range(4) for n in _WEIGHTS])


def kernel(x, mem, norm_mix_g, w_in, sgu_ln_g, sgu_ln_b, w_spatial, b_spatial, conv_w, w_branch, w_out, norm_xa_g, mem_norm_g, w_q_xa, w_k_xa, w_v_xa, w_o_xa, norm_ffn_g, w_gate_ffn, w_up_ffn, w_down_ffn, final_g, loss_target, m_norm_mix_g, m_w_in, m_sgu_ln_g, m_sgu_ln_b, m_w_spatial, m_b_spatial, m_conv_w, m_w_branch, m_w_out, m_norm_xa_g, m_mem_norm_g, m_w_q_xa, m_w_k_xa, m_w_v_xa, m_w_o_xa, m_norm_ffn_g, m_w_gate_ffn, m_w_up_ffn, m_w_down_ffn, m_final_g, v_norm_mix_g, v_w_in, v_sgu_ln_g, v_sgu_ln_b, v_w_spatial, v_b_spatial, v_conv_w, v_w_branch, v_w_out, v_norm_xa_g, v_mem_norm_g, v_w_q_xa, v_w_k_xa, v_w_v_xa, v_w_o_xa, v_norm_ffn_g, v_w_gate_ffn, v_w_up_ffn, v_w_down_ffn, v_final_g):
    return _step(dict(locals()))
```

```python
import functools
import math

import jax
import jax.numpy as jnp
from jax import lax
from jax.experimental import pallas as pl
from jax.experimental.pallas import tpu as pltpu

F32, BF16 = jnp.float32, jnp.bfloat16
MESH = pl.DeviceIdType.MESH
ANY = pl.BlockSpec(memory_space=pl.ANY)

D_MODEL = 1024
DEPTH = 2
BW = 512
SB_HEADS, SB_DH = 8, 64
SGU_LEN, SGU_GROUPS, SGU_GD, SGU_CHUNK = 128, 4, 128, 64
XA_HEADS, XA_DH = 4, 256
FFN_SH = 704
N_CHIPS = 4
IN_COLS = 7168
C_Z, C_CB, C_GATES = 1536, 2560, 4096

ADAM_LR, ADAM_B1, ADAM_B2, ADAM_EPS, ADAM_WD, ADAM_STEP = 0.001, 0.9, 0.999, 1e-08, 0.01, 10

VMEM_LIMIT_V7X = 56 * 1024 * 1024

NN = (((1,), (0,)), ((), ()))
NT = (((1,), (1,)), ((), ()))
TN = (((0,), (0,)), ((), ()))


def _cp(*sem):
    return pltpu.CompilerParams(dimension_semantics=sem, vmem_limit_bytes=VMEM_LIMIT_V7X)


def _tile(n, pref):
    for t in pref:
        if n % t == 0:
            return t
    return n


def _rows(r, row_bytes, block_bytes=1 << 20):
    fits = [t for t in range(8, r + 1, 8) if r % t == 0 and t * row_bytes <= block_bytes]
    return max(fits) if fits else r


class _Rider:
    def __init__(self, ins, outs, n_sems, start, finish, alias=None, middle=None):
        self.ins, self.outs, self.n_sems, self.alias = list(ins), list(outs), n_sems, alias or {}
        self.start, self.middle, self.finish = start, middle, finish


class _Sems:
    def __init__(self, ref, first):
        self.ref, self.first = ref, first

    @property
    def at(self):
        return self

    def __getitem__(self, k):
        return self.ref.at[self.first + k]


def _join(riders):
    riders = [r for r in riders if r is not None]
    if not riders:
        return None
    spans, i0, o0, s0 = [], 0, 0, 0
    for r in riders:
        spans.append((r, i0, o0, s0))
        i0, o0, s0 = i0 + len(r.ins), o0 + len(r.outs), s0 + r.n_sems

    def phase(which):
        def run(ins, outs, send, recv):
            for r, i, o, s in spans:
                fn = getattr(r, which)
                if fn is not None:
                    fn(ins[i:i + len(r.ins)], outs[o:o + len(r.outs)], _Sems(send, s), _Sems(recv, s))
        return run

    joined = _Rider([a for r in riders for a in r.ins], [a for r in riders for a in r.outs], s0, phase("start"), phase("finish"),
                    alias={o + k: i + v for r, i, o, s in spans for k, v in r.alias.items()}, middle=phase("middle"))
    joined.split = lambda landed: [list(landed[o:o + len(r.outs)]) for r, i, o, s in spans]
    return joined


def _call_with_rider(rider, body, *, name, grid, in_specs, args, out_specs, out_shape, scratch_shapes, semantics):
    if rider is None:
        return pl.pallas_call(body, name=name, grid=grid, in_specs=in_specs, out_specs=out_specs, out_shape=out_shape,
                              scratch_shapes=scratch_shapes, compiler_params=_cp(*semantics))(*args)
    n_in, n_out, r_in, r_out = len(args), len(out_shape), len(rider.ins), len(rider.outs)

    def riding(*refs):
        ins, rins = refs[:n_in], refs[n_in:n_in + r_in]
        outs, routs = refs[n_in + r_in:n_in + r_in + n_out], refs[n_in + r_in + n_out:n_in + r_in + n_out + r_out]
        rest = refs[n_in + r_in + n_out + r_out:]
        scratch, send, recv = rest[:-2], rest[-2], rest[-1]
        step = pl.program_id(0)
        for ax in range(1, len(grid)):
            step = step * grid[ax] + pl.program_id(ax)
        n_steps = math.prod(grid)

        @pl.when(step == 0)
        def _():
            rider.start(rins, routs, send, recv)

        body(*ins, *outs, *scratch)

        if rider.middle is not None:
            @pl.when(step == (3 * n_steps) // 5)
            def _():
                rider.middle(rins, routs, send, recv)

        @pl.when(step == n_steps - 1)
        def _():
            rider.finish(rins, routs, send, recv)

    return pl.pallas_call(
        riding, name=name, grid=grid, in_specs=list(in_specs) + [ANY] * r_in, out_specs=list(out_specs) + [ANY] * r_out,
        out_shape=list(out_shape) + rider.outs, scratch_shapes=list(scratch_shapes) + [_dma_sems(rider.n_sems), _dma_sems(rider.n_sems)],
        input_output_aliases={n_in + i: n_out + o for o, i in rider.alias.items()},
        compiler_params=_cp(*["arbitrary"] * len(grid)),
    )(*args, *rider.ins)


def _run_rider(rider, *, name):
    def nothing(*refs):
        pass

    return _call_with_rider(rider, nothing, name=name, grid=(1,), in_specs=[], args=[], out_specs=[], out_shape=[], scratch_shapes=[],
                            semantics=("arbitrary",))


def _mm(a, b, *, mode, name, out_dtype=F32, res=None, rms_bwd=None, a_kind="2d", b_kind="2d", tm=None, tn=None, tk=None, rider=None):
    a2, b2 = a.shape[-2:], b.shape[-2:]
    if mode == "nn":
        (M, K), N = a2, b2[1]
    elif mode == "nt":
        (M, K), N = a2, b2[0]
    else:
        (K, M), N = a2, b2[1]
    kchunk = a_kind == "kchunk" or b_kind == "kchunk"
    batch = a_kind == "batch" or b_kind == "batch"
    G = (a.shape[0] if a_kind == "batch" else b.shape[0]) if batch else 1
    tm = tm or _tile(M, (1024, 512, 256, 128))
    tn = tn or _tile(N, (1024, 512, 256, 128))
    if kchunk:
        tk, nk = K, (a.shape[0] if a_kind == "kchunk" else b.shape[0])
    else:
        tk = tk or _tile(K, (1024, 512, 256, 128))
        nk = K // tk

    def spec(kind, blk, idx):
        if kind == "2d":
            return pl.BlockSpec(blk, lambda g, i, j, k: idx(g, i, j, k))
        if kind == "batch":
            return pl.BlockSpec((None,) + blk, lambda g, i, j, k: (g,) + idx(g, i, j, k))
        return pl.BlockSpec((None,) + blk, lambda g, i, j, k: (k,) + idx(g, i, j, 0))

    if mode == "nn":
        a_spec = spec(a_kind, (tm, tk), lambda g, i, j, k: (i, k))
        b_spec = spec(b_kind, (tk, tn), lambda g, i, j, k: (k, j))
    elif mode == "nt":
        a_spec = spec(a_kind, (tm, tk), lambda g, i, j, k: (i, k))
        b_spec = spec(b_kind, (tn, tk), lambda g, i, j, k: (j, k))
    else:
        a_spec = spec(a_kind, (tk, tm), lambda g, i, j, k: (k, i))
        b_spec = spec(b_kind, (tk, tn), lambda g, i, j, k: (k, j))
    o_kind = "batch" if batch else "2d"
    o_spec = spec(o_kind, (tm, tn), lambda g, i, j, k: (i, j))
    o_shape = ((G,) if batch else ()) + (M, N)
    dn = {"nn": NN, "nt": NT, "tn": TN}[mode]
    has_res, has_rms = res is not None, rms_bwd is not None
    assert not (has_res and has_rms) and (not has_rms or (tn == N and G == 1))

    def body(*refs):
        if has_res:
            a_ref, b_ref, r_ref, o_ref = refs[:4]
        elif has_rms:
            a_ref, b_ref, x_ref, g_ref, dres_ref, o_ref, dg_ref = refs[:7]
        else:
            a_ref, b_ref, o_ref = refs[:3]
        p = lax.dot_general(a_ref[...].astype(BF16), b_ref[...].astype(BF16), dn, preferred_element_type=F32)
        first_rows = pl.program_id(1) == 0

        def finish(r):
            if has_rms:
                xv = x_ref[...]
                rs = lax.rsqrt(jnp.mean(xv * xv, axis=-1, keepdims=True) + 1e-6)
                u = r * g_ref[...]
                s = jnp.sum(u * xv, axis=-1, keepdims=True)
                o_ref[...] = dres_ref[...] + rs * u - xv * ((rs * rs * rs) * (s * (1.0 / N)))
                part = jnp.sum(r * (xv * rs), axis=0, keepdims=True)

                @pl.when(first_rows)
                def _():
                    dg_ref[...] = part

                @pl.when(jnp.logical_not(first_rows))
                def _():
                    dg_ref[...] += part
                return
            if has_res:
                r = r + r_ref[...]
            o_ref[...] = r.astype(out_dtype)

        if nk == 1:
            finish(p)
        else:
            acc = refs[-1]
            k = pl.program_id(3)

            @pl.when(k == 0)
            def _():
                acc[...] = p

            @pl.when(k > 0)
            def _():
                acc[...] += p

            @pl.when(k == nk - 1)
            def _():
                finish(acc[...])

    in_specs, args = [a_spec, b_spec], [a, b]
    tile = spec("2d", (tm, tn), lambda g, i, j, k: (i, j))
    out_specs, out_shape = [o_spec], [jax.ShapeDtypeStruct(o_shape, out_dtype)]
    if has_res:
        in_specs.append(tile)
        args.append(res)
    if has_rms:
        x, gain, dres = rms_bwd
        vec = pl.BlockSpec((1, tn), lambda g, i, j, k: (0, 0))
        in_specs += [tile, vec, tile]
        args += [x, gain.reshape(1, N), dres]
        out_specs.append(vec)
        out_shape.append(jax.ShapeDtypeStruct((1, N), F32))
    outs = _call_with_rider(
        rider, body, name=name, grid=(G, M // tm, N // tn, nk), in_specs=in_specs, args=args, out_specs=out_specs,
        out_shape=out_shape, scratch_shapes=[pltpu.VMEM((tm, tn), F32)] if nk > 1 else [],
        semantics=("parallel", "arbitrary" if has_rms else "parallel", "parallel", "arbitrary"))
    main = (outs[0], outs[1].reshape(N)) if has_rms else outs[0]
    return main if rider is None else (main, outs[len(out_shape):])


def _rms_fwd(x, g, *, name):
    S, Dm = x.shape
    tm = _tile(S, (512, 256))

    def body(x_ref, g_ref, o_ref):
        xv = x_ref[...]
        r = lax.rsqrt(jnp.mean(xv * xv, axis=-1, keepdims=True) + 1e-6)
        o_ref[...] = (xv * r * g_ref[...]).astype(BF16)

    return pl.pallas_call(
        body, name=name, grid=(S // tm,),
        in_specs=[pl.BlockSpec((tm, Dm), lambda i: (i, 0)), pl.BlockSpec((1, Dm), lambda i: (0, 0))],
        out_specs=pl.BlockSpec((tm, Dm), lambda i: (i, 0)), out_shape=jax.ShapeDtypeStruct((S, Dm), BF16),
        compiler_params=_cp("parallel"),
    )(x, g.reshape(1, Dm))


def _rms_bwd(x, g, dh, dres, *, name):
    S, Dm = x.shape
    tm = _tile(S, (512, 256))

    def body(x_ref, g_ref, dh_ref, dr_ref, dx_ref, dg_ref):
        xv, dhv = x_ref[...], dh_ref[...].astype(F32)
        r = lax.rsqrt(jnp.mean(xv * xv, axis=-1, keepdims=True) + 1e-6)
        u = dhv * g_ref[...]
        s = jnp.sum(u * xv, axis=-1, keepdims=True)
        dx_ref[...] = dr_ref[...] + r * u - xv * ((r * r * r) * (s * (1.0 / Dm)))
        part = jnp.sum(dhv * (xv * r), axis=0, keepdims=True)

        @pl.when(pl.program_id(0) == 0)
        def _():
            dg_ref[...] = part

        @pl.when(pl.program_id(0) > 0)
        def _():
            dg_ref[...] += part

    row = pl.BlockSpec((tm, Dm), lambda i: (i, 0))
    vec = pl.BlockSpec((1, Dm), lambda i: (0, 0))
    dx, dg = pl.pallas_call(
        body, name=name, grid=(S // tm,), in_specs=[row, vec, row, row], out_specs=[row, vec],
        out_shape=[jax.ShapeDtypeStruct((S, Dm), F32), jax.ShapeDtypeStruct((1, Dm), F32)],
        compiler_params=_cp("arbitrary"),
    )(x, g.reshape(1, Dm), dh, dres)
    return dx, dg.reshape(Dm)


def _loss_head(x, g, target):
    S, Dm = x.shape
    tm = _tile(S, (512, 256))

    def body(x_ref, g_ref, t_ref, dx_ref, dg_ref, loss_ref):
        xv, gv = x_ref[...], g_ref[...]
        r = lax.rsqrt(jnp.mean(xv * xv, axis=-1, keepdims=True) + 1e-6)
        xn = xv * r
        err = xn * gv - t_ref[...]
        lpart = 0.5 * jnp.sum(jnp.mean(err * err, axis=-1, keepdims=True), axis=0, keepdims=True)
        dy = err * (1.0 / Dm)
        u = dy * gv
        s = jnp.sum(u * xv, axis=-1, keepdims=True)
        dx_ref[...] = r * u - xv * ((r * r * r) * (s * (1.0 / Dm)))
        part = jnp.sum(dy * xn, axis=0, keepdims=True)
        lslab = jnp.broadcast_to(lpart, (8, 128))

        @pl.when(pl.program_id(0) == 0)
        def _():
            dg_ref[...] = part
            loss_ref[...] = lslab

        @pl.when(pl.program_id(0) > 0)
        def _():
            dg_ref[...] += part
            loss_ref[...] += lslab

    row = pl.BlockSpec((tm, Dm), lambda i: (i, 0))
    vec = pl.BlockSpec((1, Dm), lambda i: (0, 0))
    dx, dg, loss = pl.pallas_call(
        body, name="loss_head", grid=(S // tm,), in_specs=[row, vec, row],
        out_specs=[row, vec, pl.BlockSpec((8, 128), lambda i: (0, 0))],
        out_shape=[jax.ShapeDtypeStruct((S, Dm), F32), jax.ShapeDtypeStruct((1, Dm), F32), jax.ShapeDtypeStruct((8, 128), F32)],
        compiler_params=_cp("arbitrary"),
    )(x, g.reshape(1, Dm), target)
    return loss[0, 0], dx, dg.reshape(Dm)


SB_TQ, SB_TK = 256, 256
SB_EXP_FLOOR = -104.0


def _split2(v):
    hi = v.astype(BF16)
    return jnp.concatenate([hi, (v - hi.astype(F32)).astype(BF16)], axis=1)


def _tri2(cmp):
    j = lax.broadcasted_iota(jnp.int32, (2 * SB_TK, SB_TK), 0) % SB_TK
    s = lax.broadcasted_iota(jnp.int32, (2 * SB_TK, SB_TK), 1)
    return cmp(j, s).astype(BF16)


def _sb_scores(qv, kb, k0, q0, tq):
    rows = qv.shape[0]
    z = lax.dot_general(qv, kb, NT, preferred_element_type=F32) * (SB_DH ** -0.5)
    t_pos = q0 + lax.broadcasted_iota(jnp.int32, (rows, SB_TK), 0) % tq
    s_pos = k0 + lax.broadcasted_iota(jnp.int32, (rows, SB_TK), 1)
    valid = s_pos < t_pos
    ls = jnp.minimum(z, 0.0) - jnp.log(1.0 + jnp.exp(-jnp.abs(z)))
    l1m = jnp.where(valid, ls - z, 0.0)
    return z, valid, ls, l1m


SB_PAIRS = SB_HEADS // 2
_Q_BLK, _K_BLK, _V_BLK = 0, SB_PAIRS, 2 * SB_PAIRS


def _wide(x):
    return x if SB_TK == 128 else jnp.concatenate([x] * (SB_TK // 128), axis=1)


def _lanes_of(h, shape):
    lane = lax.broadcasted_iota(jnp.int32, shape, len(shape) - 1)
    return (lane < SB_DH) if h == 0 else (lane >= SB_DH)


def _sb2_fwd(p, rider=None):
    S = p.shape[0]
    tq = min(SB_TQ, S)
    kb_per_q = tq // SB_TK

    def body(q_ref, k_ref, v_ref, o_ref, tot_ref, cnt_ref, qm, acc, c):
        i = pl.program_id(1)
        q0 = i * tq
        later = _tri2(lambda j, s: j > s)
        q2 = q_ref[...]
        for h in range(2):
            qm[h * tq:(h + 1) * tq, :] = jnp.where(_lanes_of(h, q2.shape), q2, 0.0).astype(BF16)
        acc[...] = jnp.zeros_like(acc)
        c[...] = jnp.zeros_like(c)
        nkb = (i + 1) * kb_per_q

        def more(st):
            n, highest = st
            return (n < nkb) & (highest > SB_EXP_FLOOR)

        def step(st):
            n, _ = st
            k0 = pl.multiple_of((nkb - 1 - n) * SB_TK, SB_TK)
            kb, vb = k_ref[pl.ds(k0, SB_TK), :].astype(BF16), v_ref[pl.ds(k0, SB_TK), :].astype(BF16)
            c_old = c[...]
            z, valid, ls, l1m = _sb_scores(qm[...], kb, k0, q0, tq)
            c_new = c_old + jnp.sum(l1m, axis=1, keepdims=True)
            after = jnp.dot(_split2(l1m), later, preferred_element_type=F32)
            a = jnp.where(valid, jnp.exp(ls + after + _wide(c_old)), 0.0)
            av = jnp.dot(a.astype(BF16), vb, preferred_element_type=F32)
            acc[...] += jnp.where(_lanes_of(0, (tq, 128)), av[:tq], av[tq:])
            c[...] = c_new
            return n + 1, jnp.max(c_new)

        n_done, _ = lax.while_loop(more, step, (jnp.int32(0), jnp.float32(0.0)))
        o_ref[...] = acc[...].astype(o_ref.dtype)
        for h in range(2):
            tot_ref[h] = c[h * tq:(h + 1) * tq, :]
        cnt_ref[...] = jnp.full(cnt_ref.shape, n_done.astype(F32))

    col = lambda first: pl.BlockSpec((S, 128), lambda g, i: (0, first + g))
    outs = _call_with_rider(
        rider, body, name="sb_fwd", grid=(SB_PAIRS, S // tq), args=[p, p, p],
        in_specs=[pl.BlockSpec((tq, 128), lambda g, i: (i, _Q_BLK + g)), col(_K_BLK), col(_V_BLK)],
        out_specs=[pl.BlockSpec((tq, 128), lambda g, i: (i, g)), pl.BlockSpec((2, tq, 128), lambda g, i: (g, i, 0)),
                   pl.BlockSpec((None, None, 8, 128), lambda g, i: (g, i, 0, 0))],
        out_shape=[jax.ShapeDtypeStruct((S, BW), BF16), jax.ShapeDtypeStruct((SB_HEADS, S, 128), F32),
                   jax.ShapeDtypeStruct((SB_PAIRS, S // tq, 8, 128), F32)],
        scratch_shapes=[pltpu.VMEM((2 * tq, 128), BF16), pltpu.VMEM((tq, 128), F32), pltpu.VMEM((2 * tq, 128), F32)],
        semantics=("parallel", "parallel"))
    return outs[:3], outs[3:]


def _sb2_bwd(p, dbr, tot, cnt, rider=None):
    S = p.shape[0]
    tq = min(SB_TQ, S)
    kb_per_q = tq // SB_TK
    scale = SB_DH ** -0.5

    def body(q_ref, k_ref, v_ref, do_ref, tot_ref, cnt_ref, dq_ref, dk_ref, dv_ref, qm, dom, tot, dq_acc, pre, gpre):
        i = pl.program_id(1)
        q0 = i * tq
        upto = _tri2(lambda j, s: j <= s)
        before = _tri2(lambda j, s: j < s)

        @pl.when(i == 0)
        def _():
            dk_ref[...] = jnp.zeros_like(dk_ref)
            dv_ref[...] = jnp.zeros_like(dv_ref)

        q2, do2 = q_ref[...], do_ref[...]
        for h in range(2):
            rows = slice(h * tq, (h + 1) * tq)
            qm[rows, :] = jnp.where(_lanes_of(h, q2.shape), q2, 0.0).astype(BF16)
            dom[rows, :] = jnp.where(_lanes_of(h, do2.shape), do2, 0.0).astype(BF16)
            tot[rows, :] = tot_ref[h]
        dq_acc[...] = jnp.zeros_like(dq_acc)
        pre[...] = jnp.zeros_like(pre)
        gpre[...] = jnp.zeros_like(gpre)

        n_done = jnp.max(cnt_ref[...]).astype(jnp.int32)
        first = (i + 1) * kb_per_q - n_done

        def step(n, carry):
            k0 = pl.multiple_of((first + n) * SB_TK, SB_TK)
            kb, vb = k_ref[pl.ds(k0, SB_TK), :].astype(BF16), v_ref[pl.ds(k0, SB_TK), :].astype(BF16)
            pre_o, gpre_o = pre[...], gpre[...]
            z, valid, ls, l1m = _sb_scores(qm[...], kb, k0, q0, tq)
            incl = jnp.dot(_split2(l1m), upto, preferred_element_type=F32)
            rest = _wide(tot[...] - pre_o) - incl
            a = jnp.where(valid, jnp.exp(ls + rest), 0.0)
            da = lax.dot_general(dom[...], vb, NT, preferred_element_type=F32)
            g = a * da
            gbefore = jnp.dot(_split2(g), before, preferred_element_type=F32) + _wide(gpre_o)
            dz = jnp.where(valid, g * jnp.exp(ls - z) - jnp.exp(ls) * gbefore, 0.0) * scale
            dzb = dz.astype(BF16)
            dq_p = jnp.dot(dzb, kb, preferred_element_type=F32)
            dq_acc[...] += jnp.where(_lanes_of(0, (tq, 128)), dq_p[:tq], dq_p[tq:])
            dk_ref[pl.ds(k0, SB_TK), :] += lax.dot_general(dzb, qm[...], TN, preferred_element_type=F32)
            dv_ref[pl.ds(k0, SB_TK), :] += lax.dot_general(a.astype(BF16), dom[...], TN, preferred_element_type=F32)
            pre[...] = pre_o + jnp.sum(l1m, axis=1, keepdims=True)
            gpre[...] = gpre_o + jnp.sum(g, axis=1, keepdims=True)
            return carry

        lax.fori_loop(0, n_done, step, 0)
        dq_ref[...] = dq_acc[...].astype(dq_ref.dtype)

    col = lambda first: pl.BlockSpec((S, 128), lambda g, i: (0, first + g))
    tile = pl.BlockSpec((tq, 128), lambda g, i: (i, g))
    whole = pl.BlockSpec((S, 128), lambda g, i: (0, g))
    outs = _call_with_rider(
        rider, body, name="sb_bwd", grid=(SB_PAIRS, S // tq), args=[p, p, p, dbr, tot, cnt],
        in_specs=[pl.BlockSpec((tq, 128), lambda g, i: (i, _Q_BLK + g)), col(_K_BLK), col(_V_BLK),
                  pl.BlockSpec((None, tq, 128), lambda g, i: (0, i, g)), pl.BlockSpec((2, tq, 128), lambda g, i: (g, i, 0)),
                  pl.BlockSpec((None, None, 8, 128), lambda g, i: (g, i, 0, 0))],
        out_specs=[tile, whole, whole],
        out_shape=[jax.ShapeDtypeStruct((S, BW), BF16), jax.ShapeDtypeStruct((S, BW), F32), jax.ShapeDtypeStruct((S, BW), F32)],
        scratch_shapes=[pltpu.VMEM((2 * tq, 128), BF16), pltpu.VMEM((2 * tq, 128), BF16), pltpu.VMEM((2 * tq, 128), F32),
                        pltpu.VMEM((tq, 128), F32), pltpu.VMEM((2 * tq, 128), F32), pltpu.VMEM((2 * tq, 128), F32)],
        semantics=("parallel", "arbitrary"))
    return outs[:3], outs[3:]


_INV_SQRT2 = 0.7071067811865476
_INV_SQRT2PI = 0.3989422804014327


def _gelu(x):
    return 0.5 * x * (1.0 + lax.erf(x * _INV_SQRT2))


def _gelu_grad(x):
    return 0.5 * (1.0 + lax.erf(x * _INV_SQRT2)) + x * (_INV_SQRT2PI * jnp.exp(-0.5 * x * x))


def _sgu_mask():
    t = lax.broadcasted_iota(jnp.int32, (SGU_LEN, SGU_LEN), 0) // SGU_CHUNK
    s = lax.broadcasted_iota(jnp.int32, (SGU_LEN, SGU_LEN), 1) // SGU_CHUNK
    return t >= s


def _sgu_mask_t():
    t = lax.broadcasted_iota(jnp.int32, (SGU_LEN, SGU_LEN), 0) // SGU_CHUNK
    s = lax.broadcasted_iota(jnp.int32, (SGU_LEN, SGU_LEN), 1) // SGU_CHUNK
    return s >= t


def _sgu_norm(zv, g, b):
    vv = _gelu(zv)
    xc = vv - jnp.mean(vv, axis=-1, keepdims=True)
    rstd = lax.rsqrt(jnp.mean(xc * xc, axis=-1, keepdims=True) + 1e-5)
    xhat = xc * rstd
    return xhat, rstd, xhat * g + b


SGU_TM = 256


def _sgu_fwd(p, ln_g, ln_b, w_s, b_st):
    S = p.shape[0]
    tm = min(SGU_TM, S)

    def body(zu_ref, zv_ref, g_ref, b_ref, w_ref, bs_ref, o_ref):
        u = _gelu(zu_ref[...])
        _, _, vn = _sgu_norm(zv_ref[...], g_ref[...], b_ref[...])
        vnb = vn.astype(BF16)
        mask = _sgu_mask()
        for gi in range(SGU_GROUPS):
            wg = jnp.where(mask, w_ref[gi], 0.0).astype(BF16)
            cols = slice(gi * SGU_GD, (gi + 1) * SGU_GD)
            for ci in range(tm // SGU_LEN):
                rows = slice(ci * SGU_LEN, (ci + 1) * SGU_LEN)
                vm = jnp.dot(wg, vnb[rows, cols], preferred_element_type=F32) + bs_ref[:, gi:gi + 1]
                o_ref[rows, cols] = (u[rows, cols] * vm).astype(BF16)

    vec = pl.BlockSpec((1, BW), lambda i: (0, 0))
    return pl.pallas_call(
        body, name="sgu_fwd", grid=(S // tm,),
        in_specs=[pl.BlockSpec((tm, BW), lambda i: (i, C_Z // BW)), pl.BlockSpec((tm, BW), lambda i: (i, C_Z // BW + 1)), vec, vec,
                  pl.BlockSpec((SGU_GROUPS, SGU_LEN, SGU_LEN), lambda i: (0, 0, 0)), pl.BlockSpec((SGU_LEN, SGU_GROUPS), lambda i: (0, 0))],
        out_specs=pl.BlockSpec((tm, BW), lambda i: (i, 0)), out_shape=jax.ShapeDtypeStruct((S, BW), BF16),
        compiler_params=_cp("parallel"),
    )(p, p, ln_g.reshape(1, BW), ln_b.reshape(1, BW), w_s, b_st)


def _sgu_bwd(p, dyb, ln_g, ln_b, w_s, w_st, b_st):
    S = p.shape[0]
    tm = min(SGU_TM, S)

    def body(zu_ref, zv_ref, dy_ref, g_ref, b_ref, w_ref, wt_ref, bs_ref, dz_ref, dg_ref, db_ref, dw_ref, dbs_ref, dvn):
        first = pl.program_id(0) == 0

        @pl.when(first)
        def _():
            dg_ref[...] = jnp.zeros_like(dg_ref)
            db_ref[...] = jnp.zeros_like(db_ref)
            dw_ref[...] = jnp.zeros_like(dw_ref)
            dbs_ref[...] = jnp.zeros_like(dbs_ref)

        zu, zv, dy = zu_ref[...], zv_ref[...], dy_ref[...].astype(F32)
        u = _gelu(zu)
        xhat, rstd, vn = _sgu_norm(zv, g_ref[...], b_ref[...])
        vnb = vn.astype(BF16)
        mask = _sgu_mask()
        mask_t = _sgu_mask_t()
        for gi in range(SGU_GROUPS):
            wg = jnp.where(mask, w_ref[gi], 0.0).astype(BF16)
            wgt = jnp.where(mask_t, wt_ref[gi], 0.0).astype(BF16)
            cols = slice(gi * SGU_GD, (gi + 1) * SGU_GD)
            for ci in range(tm // SGU_LEN):
                rows = slice(ci * SGU_LEN, (ci + 1) * SGU_LEN)
                vm = jnp.dot(wg, vnb[rows, cols], preferred_element_type=F32) + bs_ref[:, gi:gi + 1]
                dyc = dy[rows, cols]
                dz_ref[rows, cols] = (dyc * vm * _gelu_grad(zu[rows, cols])).astype(BF16)
                dvm = dyc * u[rows, cols]
                dvmb = dvm.astype(BF16)
                dbs_ref[gi] += jnp.broadcast_to(jnp.sum(dvm, axis=1, keepdims=True), (SGU_LEN, SGU_GD))
                dw_ref[gi] += lax.dot_general(dvmb, vnb[rows, cols], NT, preferred_element_type=F32)
                dvn[rows, cols] = jnp.dot(wgt, dvmb, preferred_element_type=F32)
        dvnv = dvn[...]
        dg_ref[...] += jnp.sum(dvnv * xhat, axis=0, keepdims=True)
        db_ref[...] += jnp.sum(dvnv, axis=0, keepdims=True)
        dxh = dvnv * g_ref[...]
        dvv = rstd * (dxh - jnp.mean(dxh, axis=-1, keepdims=True) - xhat * jnp.mean(dxh * xhat, axis=-1, keepdims=True))
        dz_ref[:, BW:] = (dvv * _gelu_grad(zv)).astype(BF16)

        @pl.when(pl.program_id(0) == n_steps - 1)
        def _():
            for gi in range(SGU_GROUPS):
                dw_ref[gi] = jnp.where(mask, dw_ref[gi], 0.0)

    n_steps = S // tm
    vec = pl.BlockSpec((1, BW), lambda i: (0, 0))
    half = lambda c: pl.BlockSpec((tm, BW), lambda i: (i, c))
    wspec = pl.BlockSpec((SGU_GROUPS, SGU_LEN, SGU_LEN), lambda i: (0, 0, 0))
    dz, dg, db, dw, dbs = pl.pallas_call(
        body, name="sgu_bwd", grid=(n_steps,),
        in_specs=[half(C_Z // BW), half(C_Z // BW + 1), half(0), vec, vec, wspec, wspec,
                  pl.BlockSpec((SGU_LEN, SGU_GROUPS), lambda i: (0, 0))],
        out_specs=[pl.BlockSpec((tm, 2 * BW), lambda i: (i, 0)), vec, vec, wspec, wspec],
        out_shape=[jax.ShapeDtypeStruct((S, 2 * BW), BF16), jax.ShapeDtypeStruct((1, BW), F32), jax.ShapeDtypeStruct((1, BW), F32),
                   jax.ShapeDtypeStruct((SGU_GROUPS, SGU_LEN, SGU_LEN), F32), jax.ShapeDtypeStruct((SGU_GROUPS, SGU_LEN, SGU_GD), F32)],
        scratch_shapes=[pltpu.VMEM((tm, BW), F32)],
        compiler_params=_cp("arbitrary"),
    )(p, p, dyb, ln_g.reshape(1, BW), ln_b.reshape(1, BW), w_s, w_st, b_st)
    return dz, dg.reshape(BW), db.reshape(BW), dw, dbs[:, :, 0]


CONV_TC = 128


def _shift_down(y, n):
    rows = lax.broadcasted_iota(jnp.int32, y.shape, 0)
    return jnp.where(rows < n, 0.0, pltpu.roll(y, n, 0))


def _shift_up(y, n):
    rows = lax.broadcasted_iota(jnp.int32, y.shape, 0)
    return jnp.where(rows >= y.shape[0] - n, 0.0, pltpu.roll(y, y.shape[0] - n, 0))


def _conv_specs(S):
    col = lambda c0: pl.BlockSpec((S, CONV_TC), lambda j: (0, c0 // CONV_TC + j))
    return col(C_CB), col(C_CB + BW), col(C_CB + 2 * BW), pl.BlockSpec((3, CONV_TC), lambda j: (0, j)), pl.BlockSpec((S, CONV_TC), lambda j: (0, j))


def _conv_fwd(p, conv_w):
    S = p.shape[0]

    def body(cb_ref, cc_ref, cx_ref, w_ref, o_ref):
        y = cc_ref[...] * cx_ref[...]
        conv = w_ref[0:1, :] * _shift_down(y, 2) + w_ref[1:2, :] * _shift_down(y, 1) + w_ref[2:3, :] * y
        o_ref[...] = (cb_ref[...] * conv).astype(BF16)

    cb, cc, cx, wspec, out = _conv_specs(S)
    return pl.pallas_call(
        body, name="conv_fwd", grid=(BW // CONV_TC,), in_specs=[cb, cc, cx, wspec], out_specs=out,
        out_shape=jax.ShapeDtypeStruct((S, BW), BF16), compiler_params=_cp("parallel"),
    )(p, p, p, conv_w)


def _conv_bwd(p, conv_w, dyc):
    S = p.shape[0]

    def body(cb_ref, cc_ref, cx_ref, w_ref, dy_ref, db_ref, dc_ref, dx_ref, dw_ref):
        cc, cx, dy = cc_ref[...], cx_ref[...], dy_ref[...].astype(F32)
        y = cc * cx
        w0, w1, w2 = w_ref[0:1, :], w_ref[1:2, :], w_ref[2:3, :]
        y1, y2 = _shift_down(y, 1), _shift_down(y, 2)
        conv = w0 * y2 + w1 * y1 + w2 * y
        db_ref[...] = (dy * conv).astype(BF16)
        dconv = dy * cb_ref[...]
        dyy = w2 * dconv + w1 * _shift_up(dconv, 1) + w0 * _shift_up(dconv, 2)
        dc_ref[...] = (dyy * cx).astype(BF16)
        dx_ref[...] = (dyy * cc).astype(BF16)
        dw_ref[0:1, :] = jnp.sum(dconv * y2, axis=0, keepdims=True)
        dw_ref[1:2, :] = jnp.sum(dconv * y1, axis=0, keepdims=True)
        dw_ref[2:3, :] = jnp.sum(dconv * y, axis=0, keepdims=True)

    cb, cc, cx, wspec, out = _conv_specs(S)
    db, dc, dx, dw = pl.pallas_call(
        body, name="conv_bwd", grid=(BW // CONV_TC,), in_specs=[cb, cc, cx, wspec, out],
        out_specs=[out, out, out, wspec],
        out_shape=[jax.ShapeDtypeStruct((S, BW), BF16)] * 3 + [jax.ShapeDtypeStruct((3, BW), F32)],
        compiler_params=_cp("parallel"),
    )(p, p, p, conv_w, dyc)
    return db, dc, dx, dw


def _merge_specs(S, tm):
    gate = lambda n: pl.BlockSpec((tm, D_MODEL), lambda i: (i, C_GATES // D_MODEL + n))
    return [gate(0), gate(1), gate(2)], pl.BlockSpec((3, tm, D_MODEL), lambda i: (0, i, 0)), pl.BlockSpec((tm, D_MODEL), lambda i: (i, 0))


def _branch_merge(br, w_br, p):
    S = p.shape[0]
    tm = _tile(S, (512, 256))

    def body(br_ref, w_ref, g0, g1, g2, bd_ref, o_ref):
        acc = None
        for n, g_ref in enumerate((g0, g1, g2)):
            bdn = jnp.dot(br_ref[n], w_ref[n], preferred_element_type=F32)
            bd_ref[n] = bdn
            term = jax.nn.sigmoid(g_ref[...]) * bdn
            acc = term if acc is None else acc + term
        o_ref[...] = acc.astype(BF16)

    gates, bspec, row = _merge_specs(S, tm)
    return pl.pallas_call(
        body, name="mm_branch", grid=(S // tm,),
        in_specs=[pl.BlockSpec((3, tm, BW), lambda i: (0, i, 0)), pl.BlockSpec((3, BW, D_MODEL), lambda i: (0, 0, 0))] + gates,
        out_specs=[bspec, row], out_shape=[jax.ShapeDtypeStruct((3, S, D_MODEL), F32), jax.ShapeDtypeStruct((S, D_MODEL), BF16)],
        compiler_params=_cp("parallel"),
    )(br, w_br, p, p, p)


def _merge_bwd(p, bd, dm):
    S = p.shape[0]
    tm = _tile(S, (256,))

    def body(g0, g1, g2, b_ref, dm_ref, db_ref, dg_ref):
        dmv = dm_ref[...]
        for n, g_ref in enumerate((g0, g1, g2)):
            sg = jax.nn.sigmoid(g_ref[...])
            db_ref[n] = (dmv * sg).astype(BF16)
            dg_ref[:, n * D_MODEL:(n + 1) * D_MODEL] = (dmv * b_ref[n] * (sg * (1.0 - sg))).astype(BF16)

    gates, bspec, row = _merge_specs(S, tm)
    return pl.pallas_call(
        body, name="merge_bwd", grid=(S // tm,), in_specs=gates + [bspec, row],
        out_specs=[bspec, pl.BlockSpec((tm, 3 * D_MODEL), lambda i: (i, 0))],
        out_shape=[jax.ShapeDtypeStruct((3, S, D_MODEL), BF16), jax.ShapeDtypeStruct((S, 3 * D_MODEL), BF16)],
        compiler_params=_cp("parallel"),
    )(p, p, p, bd, dm)


XA_TM = 512


def _xa_probs(qh, kh):
    s = lax.dot_general(qh, kh, NT, preferred_element_type=F32) * (XA_DH ** -0.5)
    e = jnp.exp(s - jnp.max(s, axis=-1, keepdims=True))
    return e / jnp.sum(e, axis=-1, keepdims=True)


def _xa_fwd(q, kv):
    S = q.shape[0]
    tm = min(XA_TM, S)
    M = kv.shape[1]

    def body(q_ref, kv_ref, o_ref):
        for h in range(XA_HEADS):
            cols = slice(h * XA_DH, (h + 1) * XA_DH)
            pr = _xa_probs(q_ref[:, cols], kv_ref[0, :, cols])
            o_ref[:, cols] = jnp.dot(pr.astype(BF16), kv_ref[1, :, cols], preferred_element_type=F32).astype(BF16)

    row = pl.BlockSpec((tm, D_MODEL), lambda i: (i, 0))
    return pl.pallas_call(
        body, name="xa_fwd", grid=(S // tm,), in_specs=[row, pl.BlockSpec((2, M, D_MODEL), lambda i: (0, 0, 0))], out_specs=row,
        out_shape=jax.ShapeDtypeStruct((S, D_MODEL), BF16), compiler_params=_cp("parallel"),
    )(q, kv)


def _xa_bwd(q, kv, do):
    S = q.shape[0]
    tm = min(XA_TM, S)
    M = kv.shape[1]

    def body(q_ref, kv_ref, do_ref, dq_ref, dkv_ref):
        @pl.when(pl.program_id(0) == 0)
        def _():
            dkv_ref[...] = jnp.zeros_like(dkv_ref)

        for h in range(XA_HEADS):
            cols = slice(h * XA_DH, (h + 1) * XA_DH)
            qh, kh, vh, doh = q_ref[:, cols], kv_ref[0, :, cols], kv_ref[1, :, cols], do_ref[:, cols]
            pr = _xa_probs(qh, kh)
            dkv_ref[1, :, cols] += lax.dot_general(pr.astype(BF16), doh, TN, preferred_element_type=F32)
            dp = lax.dot_general(doh, vh, NT, preferred_element_type=F32)
            ds = (pr * (dp - jnp.sum(dp * pr, axis=-1, keepdims=True)) * (XA_DH ** -0.5)).astype(BF16)
            dq_ref[:, cols] = jnp.dot(ds, kh, preferred_element_type=F32).astype(BF16)
            dkv_ref[0, :, cols] += lax.dot_general(ds, qh, TN, preferred_element_type=F32)

    row = pl.BlockSpec((tm, D_MODEL), lambda i: (i, 0))
    kvs = pl.BlockSpec((2, M, D_MODEL), lambda i: (0, 0, 0))
    return pl.pallas_call(
        body, name="xa_bwd", grid=(S // tm,), in_specs=[row, kvs, row], out_specs=[row, kvs],
        out_shape=[jax.ShapeDtypeStruct((S, D_MODEL), BF16), jax.ShapeDtypeStruct((2, M, D_MODEL), F32)],
        compiler_params=_cp("arbitrary"),
    )(q, kv, do)


def _ffn_up(h3, w_gu, rider=None):
    S = h3.shape[0]
    tm = _tile(S, (1024, 512, 256))

    def body(h_ref, w_ref, ab_ref, hh_ref):
        h = h_ref[...]
        a = lax.dot_general(h, w_ref[0], NT, preferred_element_type=F32)
        b = lax.dot_general(h, w_ref[1], NT, preferred_element_type=F32)
        ab_ref[0] = a
        ab_ref[1] = b
        hh_ref[...] = (a * jax.nn.sigmoid(a) * b).astype(BF16)

    pair = pl.BlockSpec((None, 2, tm, FFN_SH), lambda j, i: (j, 0, i, 0))
    outs = _call_with_rider(
        rider, body, name="mm_gu", grid=(N_CHIPS, S // tm), args=[h3, w_gu],
        in_specs=[pl.BlockSpec((tm, D_MODEL), lambda j, i: (i, 0)), pl.BlockSpec((None, 2, FFN_SH, D_MODEL), lambda j, i: (j, 0, 0, 0))],
        out_specs=[pair, pl.BlockSpec((None, tm, FFN_SH), lambda j, i: (j, i, 0))],
        out_shape=[jax.ShapeDtypeStruct((N_CHIPS, 2, S, FFN_SH), F32), jax.ShapeDtypeStruct((N_CHIPS, S, FFN_SH), BF16)],
        scratch_shapes=[], semantics=("parallel", "parallel"))
    return outs[0], outs[1], outs[2:]


def _ffn_down_bwd(dx3, w_dn, ab):
    S = dx3.shape[0]
    tm = _tile(S, (1024, 512, 256))

    def body(dx_ref, w_ref, ab_ref, o_ref):
        d = lax.dot_general(dx_ref[...].astype(BF16), w_ref[...], NT, preferred_element_type=F32)
        a, b = ab_ref[0], ab_ref[1]
        sg = jax.nn.sigmoid(a)
        o_ref[0] = (d * b * (sg * (1.0 + a * (1.0 - sg)))).astype(BF16)
        o_ref[1] = (d * (a * sg)).astype(BF16)

    pair = pl.BlockSpec((None, 2, tm, FFN_SH), lambda j, i: (j, 0, i, 0))
    return pl.pallas_call(
        body, name="mm_down_dx", grid=(N_CHIPS, S // tm),
        in_specs=[pl.BlockSpec((tm, D_MODEL), lambda j, i: (i, 0)), pl.BlockSpec((None, FFN_SH, D_MODEL), lambda j, i: (j, 0, 0)), pair],
        out_specs=pair, out_shape=jax.ShapeDtypeStruct(ab.shape, BF16), compiler_params=_cp("parallel", "parallel"),
    )(dx3, w_dn, ab)


def _reduce_adam(parts, w, m, v, *, name):
    shape = w.shape
    C = shape[-1]
    R = math.prod(shape[:-1])
    tm = _rows(R, 4 * C)
    n = len(parts)
    c1, c2 = 1.0 - ADAM_B1 ** ADAM_STEP, 1.0 - ADAM_B2 ** ADAM_STEP

    def body(*refs):
        g = refs[0][...]
        for r in refs[1:n]:
            g = g + r[...]
        w_ref, m_ref, v_ref, go, do, mo, vo = refs[n:]
        mn = ADAM_B1 * m_ref[...] + (1.0 - ADAM_B1) * g
        vn = ADAM_B2 * v_ref[...] + (1.0 - ADAM_B2) * (g * g)
        go[...] = g
        do[...] = -ADAM_LR * ((mn / c1) / (jnp.sqrt(vn / c2) + ADAM_EPS) + ADAM_WD * w_ref[...])
        mo[...] = mn
        vo[...] = vn

    row = pl.BlockSpec((tm, C), lambda i: (i, 0))
    outs = pl.pallas_call(
        body, name=name, grid=(R // tm,), in_specs=[row] * (n + 3), out_specs=[row] * 4,
        out_shape=[jax.ShapeDtypeStruct((R, C), F32)] * 4, compiler_params=_cp("parallel"),
    )(*[a.reshape(R, C) for a in (*parts, w, m, v)])
    return tuple(o.reshape(shape) for o in outs)


_VIEW = {
    "w_in": ((1024, 7168), (1024, 1792), 256, lambda i, b: (i, b)),
    "w_br": ((1536, 1024), (1536, 256), 512, lambda i, b: (i, b)),
    "w_sq": ((5120, 1024), (1280, 1024), 256, lambda i, b: (4 * i + b, 0)),
    "w_gu": ((5632, 1024), (1408, 1024), 352, lambda i, b: (4 * b + i, 0)),
    "w_dn": ((2816, 1024), (704, 1024), 352, lambda i, b: (2 * b + i, 0)),
    "conv_w": ((8, 512), (8, 128), 8, lambda i, b: (0, b)),
}


def _scalar(v):
    return jnp.asarray(v, jnp.int32).reshape(1)


def _place(name, local, b):
    full2, sh2, tm, idx = _VIEW[name]
    C = sh2[1]
    dt = local.dtype if name == "conv_w" else BF16

    def body(b_ref, x_ref, o0_ref, o1_ref):
        o0_ref[...] = x_ref[0].astype(dt)
        o1_ref[...] = x_ref[1].astype(dt)

    place = pl.BlockSpec((tm, C), lambda i, bs: idx(i, bs[0]))
    outs = pl.pallas_call(
        body, name="place_" + name,
        grid_spec=pltpu.PrefetchScalarGridSpec(num_scalar_prefetch=1, grid=(sh2[0] // tm,),
                                               in_specs=[pl.BlockSpec((DEPTH, tm, C), lambda i, bs: (0, i, 0))], out_specs=[place, place]),
        out_shape=[jax.ShapeDtypeStruct(full2, dt)] * 2, compiler_params=_cp("arbitrary"),
    )(_scalar(b), local.reshape((DEPTH,) + sh2))
    return [o.reshape(_FULL_SHAPE[name]) for o in outs]


def _add_owner(name, g, land, own):
    shape = g.shape
    C = shape[-1]
    R = math.prod(shape[:-1])
    tm = _rows(R, 4 * C)

    def body(s_ref, g_ref, l_ref, o_ref):
        @pl.when(s_ref[0] != 0)
        def _():
            o_ref[...] = (g_ref[...].astype(F32) + l_ref[...].astype(F32)).astype(BF16)

        @pl.when(s_ref[0] == 0)
        def _():
            o_ref[...] = jnp.zeros_like(o_ref)

    pick = pl.BlockSpec((tm, C), lambda i, s: (jnp.where(s[0] != 0, i, 0), 0))
    return pl.pallas_call(
        body, name="presum_" + name,
        grid_spec=pltpu.PrefetchScalarGridSpec(num_scalar_prefetch=1, grid=(R // tm,), in_specs=[pick, pick],
                                               out_specs=pl.BlockSpec((tm, C), lambda i, s: (i, 0))),
        out_shape=jax.ShapeDtypeStruct((R, C), BF16), compiler_params=_cp("arbitrary"),
    )(_scalar(own), g.reshape(R, C), land.reshape(R, C)).reshape(shape)


def _sum_chips(name, slots, part, b, own):
    full2, sh2, tm, idx = _VIEW[name]
    C = sh2[1]

    def body(s_ref, slot_ref, own_ref, o_ref):
        @pl.when(s_ref[1] != 0)
        def _():
            o_ref[...] = ((slot_ref[0].astype(F32) + slot_ref[1].astype(F32)) + slot_ref[2].astype(F32)) + own_ref[...].astype(F32)

        @pl.when(s_ref[1] == 0)
        def _():
            o_ref[...] = jnp.zeros_like(o_ref)

    return pl.pallas_call(
        body, name="sum_chips_" + name,
        grid_spec=pltpu.PrefetchScalarGridSpec(
            num_scalar_prefetch=1, grid=(sh2[0] // tm,),
            in_specs=[pl.BlockSpec((3, tm, C), lambda i, s: (0, jnp.where(s[1] != 0, i, 0), 0)),
                      pl.BlockSpec((tm, C), lambda i, s: idx(jnp.where(s[1] != 0, i, 0), s[0]))],
            out_specs=pl.BlockSpec((tm, C), lambda i, s: (i, 0))),
        out_shape=jax.ShapeDtypeStruct(sh2, F32), compiler_params=_cp("arbitrary"),
    )(jnp.stack([jnp.asarray(b, jnp.int32), jnp.asarray(own, jnp.int32)]), slots.reshape((3,) + sh2),
      part.reshape(full2)).reshape(_SHARD_SHAPE[name])


def _adam_layers(mine, theirs, c, w, m, v, *, name):
    shape = w.shape
    C = shape[-1]
    R = math.prod(shape[1:-1])
    tm = _rows(R, 4 * C)
    c1, c2 = 1.0 - ADAM_B1 ** ADAM_STEP, 1.0 - ADAM_B2 ** ADAM_STEP

    def body(c_ref, m0_ref, m1_ref, t_ref, w_ref, m_ref, v_ref, go, do, mo, vo):
        layer = pl.program_id(0)
        g = jnp.where(layer == c_ref[0], jnp.where(layer == 0, m0_ref[...], m1_ref[...]), t_ref[...])
        mn = ADAM_B1 * m_ref[...] + (1.0 - ADAM_B1) * g
        vn = ADAM_B2 * v_ref[...] + (1.0 - ADAM_B2) * (g * g)
        go[...] = g
        do[...] = -ADAM_LR * ((mn / c1) / (jnp.sqrt(vn / c2) + ADAM_EPS) + ADAM_WD * w_ref[...])
        mo[...] = mn
        vo[...] = vn

    def own(layer):
        return pl.BlockSpec((tm, C), lambda l, i, cs: (jnp.where((l == layer) & (cs[0] == layer), i, 0), 0))

    recv = pl.BlockSpec((tm, C), lambda l, i, cs: (jnp.where(l == cs[0], 0, i), 0))
    row = pl.BlockSpec((None, tm, C), lambda l, i, cs: (l, i, 0))
    outs = pl.pallas_call(
        body, name=name,
        grid_spec=pltpu.PrefetchScalarGridSpec(num_scalar_prefetch=1, grid=(DEPTH, R // tm),
                                               in_specs=[own(0), own(1), recv, row, row, row], out_specs=[row] * 4),
        out_shape=[jax.ShapeDtypeStruct((DEPTH, R, C), F32)] * 4, compiler_params=_cp("arbitrary", "arbitrary"),
    )(_scalar(c), mine[0].reshape(R, C), mine[1].reshape(R, C), theirs.reshape(R, C), *[t.reshape(DEPTH, R, C) for t in (w, m, v)])
    return tuple(o.reshape(shape) for o in outs)


def _take_weights(wl, names, landed):
    for n, t in zip(names, landed):
        wl[n] = t[:3] if n == "conv_w" else t


_GATHER_LATE = ["w_br", "conv_w", "w_sq"]
_GATHER_LAST = ["w_gu", "w_dn"]


def _layer_fwd(x, mem, wl, ride=None):
    S = x.shape[0]
    sv = {"x": x}
    wl = dict(wl)
    h1 = _rms_fwd(x, wl["norm_mix_g"], name="rms_mix")
    if ride is None:
        ride = {"mm_in": None, "sb": None, "mm_gu": None}
        p = _mm(h1, wl["w_in"], mode="nn", name="mm_in")
    else:
        p, landed = _mm(h1, wl["w_in"], mode="nn", name="mm_in", rider=ride["mm_in"])
        _take_weights(wl, _GATHER_LATE, landed)
    (ya, tot, cnt), landed = _sb2_fwd(p, ride["sb"])
    _take_weights(wl, _GATHER_LAST, landed)
    b_st = wl["b_spatial"].T
    yb = _sgu_fwd(p, wl["sgu_ln_g"], wl["sgu_ln_b"], wl["w_spatial"], b_st)
    yc = _conv_fwd(p, wl["conv_w"])
    br = jnp.stack([ya, yb, yc])
    bd, merged = _branch_merge(br, wl["w_br"], p)
    x1 = _mm(merged, wl["w_sq"][0], mode="nn", res=x, name="mm_out")
    h2 = _rms_fwd(x1, wl["norm_xa_g"], name="rms_xa")
    qx = _mm(h2, wl["w_sq"][1], mode="nn", out_dtype=BF16, name="mm_q")
    mn = _rms_fwd(mem, wl["mem_norm_g"], name="rms_mem")
    kv = _mm(mn, wl["w_sq"][3:5], mode="nn", b_kind="batch", out_dtype=BF16, name="mm_kv")
    o = _xa_fwd(qx, kv)
    x2 = _mm(o, wl["w_sq"][2], mode="nn", res=x1, name="mm_o")
    h3 = _rms_fwd(x2, wl["norm_ffn_g"], name="rms_ffn")
    ab, hh, rode = _ffn_up(h3, wl["w_gu"], ride["mm_gu"])
    x3 = _mm(hh, wl["w_dn"], mode="nn", a_kind="kchunk", b_kind="kchunk", res=x2, name="mm_down")
    sv.update(h1=h1, p=p, tot=tot, cnt=cnt, br=br, bd=bd, merged=merged, x1=x1, h2=h2, qx=qx, mn=mn, kv=kv, o=o,
              x2=x2, h3=h3, ab=ab, hh=hh, b_st=b_st, wl=wl)
    return x3, sv, rode


class _GradPipe:
    def __init__(self, ci, bi):
        self.ci, self.bi, self.queue = ci, bi, []
        self.part, self.slots = [dict(), dict()], [dict(), dict()]

    def to_owner(self, layer, names, g):
        def arrived(land):
            own = (self.ci == layer).astype(jnp.int32)
            part = {n: _add_owner(n, g[n], t, own) for n, t in zip(names, land)}
            self.part[layer].update(part)
            self.queue.append((layer, names, part))

        return _presum_rider(layer, {n: g[n] for n in names}), arrived

    def exchange(self):
        if not self.queue:
            return None
        layer, names, part = self.queue.pop(0)
        return _shard_rider(layer, part), lambda slots: self.slots[layer].update(zip(names, slots))

    def drain(self, also):
        landed = None
        while self.queue or landed is None:
            job = self.exchange()
            both = _join([job[0] if job else None, also if landed is None else None])
            parts = both.split(_run_rider(both, name="grad_exchange_last"))
            if job:
                job[1](parts[0])
            if landed is None:
                landed = parts[-1]
        return landed

    def reduced(self, layer):
        own = (self.ci == layer).astype(jnp.int32)
        return {n: _sum_chips(n, self.slots[layer][n], self.part[layer][n], self.bi, own) for n in _BIG}


def _layer_bwd(dx3, mem, sv, layer=None, pipe=None):
    S = dx3.shape[0]
    p, wl = sv["p"], sv["wl"]
    g = {}

    def mm(*args, job=None, **kw):
        if job is None:
            return _mm(*args, **kw)
        out, landed = _mm(*args, **kw, rider=job[0])
        job[1](landed)
        return out

    to_owner = (lambda names: pipe.to_owner(layer, names, g)) if pipe else (lambda names: None)
    exchange = pipe.exchange if pipe else (lambda: None)

    g["w_dn"] = _mm(sv["hh"], dx3, mode="tn", a_kind="batch", out_dtype=BF16, name="mm_down_dw")
    dab = _ffn_down_bwd(dx3, wl["w_dn"], sv["ab"]).reshape(2 * N_CHIPS, S, FFN_SH)
    g["w_gu"] = mm(dab, sv["h3"], mode="tn", a_kind="batch", out_dtype=BF16, name="mm_gu_dw", job=exchange()).reshape(_FULL_SHAPE["w_gu"])
    dx2, g["norm_ffn_g"] = mm(dab, wl["w_gu"].reshape(2 * N_CHIPS, FFN_SH, D_MODEL), mode="nn", a_kind="kchunk", b_kind="kchunk", name="mm_gu_dx",
                               rms_bwd=(sv["x2"], wl["norm_ffn_g"], dx3), job=to_owner(["w_dn", "w_gu"]))
    do = _mm(dx2, wl["w_sq"][2], mode="nt", out_dtype=BF16, name="mm_o_dx")
    dw_o = _mm(sv["o"], dx2, mode="tn", out_dtype=BF16, name="mm_o_dw")
    dq, dkv = _xa_bwd(sv["qx"], sv["kv"], do)
    dw_q = _mm(sv["h2"], dq, mode="tn", out_dtype=BF16, name="mm_q_dw")
    dx1, g["norm_xa_g"] = _mm(dq, wl["w_sq"][1], mode="nt", name="mm_q_dx", rms_bwd=(sv["x1"], wl["norm_xa_g"], dx2))
    dw_kv = _mm(sv["mn"], dkv, mode="tn", b_kind="batch", out_dtype=BF16, name="mm_kv_dw")
    dmn = _mm(dkv, wl["w_sq"][3:5], mode="nt", a_kind="kchunk", b_kind="kchunk", name="mm_kv_dx")
    _, g["mem_norm_g"] = _rms_bwd(mem, wl["mem_norm_g"], dmn, jnp.zeros_like(mem), name="rms_mem_bwd")
    dm = _mm(dx1, wl["w_sq"][0], mode="nt", name="mm_out_dx")
    dw_out = _mm(sv["merged"], dx1, mode="tn", out_dtype=BF16, name="mm_out_dw")
    g["w_sq"] = jnp.concatenate([jnp.stack([dw_out, dw_q, dw_o]), dw_kv])
    dbd, dgates = _merge_bwd(p, sv["bd"], dm)
    dbr = mm(dbd, wl["w_br"], mode="nt", a_kind="batch", b_kind="batch", name="mm_branch_dx", job=to_owner(["w_sq"]))
    g["w_br"] = _mm(sv["br"], dbd, mode="tn", a_kind="batch", b_kind="batch", out_dtype=BF16, name="mm_branch_dw")
    job = exchange()
    (dq, dk, dv), landed = _sb2_bwd(p, dbr, sv["tot"], sv["cnt"], job[0] if job else None)
    if job:
        job[1](landed)
    dz, g["sgu_ln_g"], g["sgu_ln_b"], g["w_spatial"], g["b_spatial"] = _sgu_bwd(
        p, dbr[1], wl["sgu_ln_g"], wl["sgu_ln_b"], wl["w_spatial"], wl["w_spatial"].transpose(0, 2, 1), sv["b_st"])
    dcb, dcc, dcx, g["conv_w"] = _conv_bwd(p, wl["conv_w"], dbr[2])
    dp = jnp.concatenate([dq, dk.astype(BF16), dv.astype(BF16), dz, dcb, dcc, dcx, dgates], axis=1)
    g["w_in"] = mm(sv["h1"], dp, mode="tn", out_dtype=BF16, name="mm_in_dw", job=exchange())
    dx, g["norm_mix_g"] = mm(dp, wl["w_in"], mode="nt", name="mm_in_dx", rms_bwd=(sv["x"], wl["norm_mix_g"], dx1),
                             job=to_owner(["w_br", "w_in"]))
    return dx, g


def _local_step(x, mem, target, layers, final_g):
    h, saved = x, []
    for wl in layers:
        h, sv, _ = _layer_fwd(h, mem, wl)
        saved.append(sv)
    loss, dx, d_final = _loss_head(h, final_g, target)
    grads = [None] * len(layers)
    for l in reversed(range(len(layers))):
        dx, grads[l] = _layer_bwd(dx, mem, saved[l])
    return loss, dx, grads, d_final


_ALL = slice(None)
CONV_ROWS = 8
_SHARD = {
    "w_in": lambda b: (_ALL, pl.ds(1792 * b, 1792)),
    "w_br": lambda b: (_ALL, _ALL, pl.ds(256 * b, 256)),
    "w_sq": lambda b: (_ALL, pl.ds(256 * b, 256), _ALL),
    "w_gu": lambda b: (b,),
    "w_dn": lambda b: (b,),
    "conv_w": lambda b: (_ALL, pl.ds(128 * b, 128)),
}
_FULL_SHAPE = {"w_in": (1024, 7168), "w_br": (3, 512, 1024), "w_sq": (5, 1024, 1024), "w_gu": (4, 2, 704, 1024),
               "w_dn": (4, 704, 1024), "conv_w": (CONV_ROWS, 512)}
_SHARD_SHAPE = {"w_in": (1024, 1792), "w_br": (3, 512, 256), "w_sq": (5, 256, 1024), "w_gu": (2, 704, 1024),
                "w_dn": (704, 1024), "conv_w": (CONV_ROWS, 128)}


def _pos():
    return lax.axis_index("x"), lax.axis_index("y"), lax.axis_index("c")


def _per_chip(fn):
    x, y, _ = _pos()
    for x0 in (0, 1):
        for y0 in (0, 1):
            @pl.when((x == x0) & (y == y0))
            def _():
                fn(x0, y0)


def _other_chips(x0, y0):
    return [(1 - x0, y0), (x0, 1 - y0), (1 - x0, 1 - y0)]


def _rcopy(src, dst, ssem, rsem, dev):
    return pltpu.make_async_remote_copy(src_ref=src, dst_ref=dst, send_sem=ssem, recv_sem=rsem, device_id=dev, device_id_type=MESH)


def _dma_sems(n):
    return pltpu.SemaphoreType.DMA((n,))


def _gather_rider(layer, placed):
    names = list(placed)
    n = len(names)
    shard = lambda refs, a, b: refs[a].at[_SHARD[names[a]](b)]

    def start(ins, outs, send, recv):
        @pl.when(lax.axis_index("c") == layer)
        def _():
            def run(x0, y0):
                for kk, (px, py) in enumerate(_other_chips(x0, y0)):
                    for a in range(n):
                        own = shard(outs, a, 2 * x0 + y0)
                        _rcopy(own, own, send.at[6 * a + kk], recv.at[6 * a + kk], (px, py, layer)).start()

            _per_chip(run)

    def passing(outs, send, recv, a, kk, bp, x0, y0):
        landed = shard(outs, a, bp)
        return _rcopy(landed, landed, send.at[6 * a + 3 + kk], recv.at[6 * a + 3 + kk], (x0, y0, 1 - layer))

    def middle(ins, outs, send, recv):
        @pl.when(lax.axis_index("c") == layer)
        def _():
            def run(x0, y0):
                for kk, (px, py) in enumerate(_other_chips(x0, y0)):
                    for a in range(n):
                        landed = shard(outs, a, 2 * px + py)
                        _rcopy(landed, landed, send.at[6 * a + kk], recv.at[6 * a + kk], (px, py, layer)).wait_recv()
                        passing(outs, send, recv, a, kk, 2 * px + py, x0, y0).start()

            _per_chip(run)

    def finish(ins, outs, send, recv):
        c = lax.axis_index("c")

        def run(x0, y0):
            chips = _other_chips(x0, y0)

            @pl.when(c == layer)
            def _():
                for kk, (px, py) in enumerate(chips):
                    for a in range(n):
                        own = shard(outs, a, 2 * x0 + y0)
                        _rcopy(own, own, send.at[6 * a + kk], recv.at[6 * a + kk], (px, py, layer)).wait_send()
                        passing(outs, send, recv, a, kk, 2 * px + py, x0, y0).wait_send()

            @pl.when(c != layer)
            def _():
                for kk, (px, py) in enumerate(chips):
                    for a in range(n):
                        got = shard(outs, a, 2 * px + py)
                        _rcopy(got, got, send.at[6 * a + 3 + kk], recv.at[6 * a + 3 + kk], (x0, y0, layer)).wait_recv()

        _per_chip(run)

    arrs = [placed[nm] for nm in names]
    return _Rider(arrs, [jax.ShapeDtypeStruct(t.shape, t.dtype) for t in arrs], 6 * n, start, finish, alias={a: a for a in range(n)}, middle=middle)


def _presum_rider(layer, grads):
    names = list(grads)
    n = len(names)

    def start(ins, outs, send, recv):
        x, y, c = _pos()

        @pl.when(c != layer)
        def _():
            for a in range(n):
                _rcopy(ins[a], outs[a], send.at[a], recv.at[a], (x, y, layer)).start()

    def finish(ins, outs, send, recv):
        x, y, c = _pos()

        @pl.when(c != layer)
        def _():
            for a in range(n):
                _rcopy(ins[a], outs[a], send.at[a], recv.at[a], (x, y, layer)).wait_send()

        @pl.when(c == layer)
        def _():
            for a in range(n):
                _rcopy(outs[a], outs[a], send.at[a], recv.at[a], (x, y, 1 - layer)).wait_recv()

    arrs = [grads[nm] for nm in names]
    return _Rider(arrs, [jax.ShapeDtypeStruct(t.shape, t.dtype) for t in arrs], n, start, finish)


def _shard_rider(layer, part):
    names = list(part)
    n = len(names)

    def each(fn):
        @pl.when(lax.axis_index("c") == layer)
        def _():
            def run(x0, y0):
                for kk, (px, py) in enumerate(_other_chips(x0, y0)):
                    for a in range(n):
                        fn(a, kk, 2 * px + py, (px, py, layer))

            _per_chip(run)

    def start(ins, outs, send, recv):
        each(lambda a, kk, bp, peer: _rcopy(ins[a].at[_SHARD[names[a]](bp)], outs[a].at[kk], send.at[3 * a + kk], recv.at[3 * a + kk], peer).start())

    def finish(ins, outs, send, recv):
        each(lambda a, kk, bp, peer: _rcopy(outs[a].at[kk], outs[a].at[kk], send.at[3 * a + kk], recv.at[3 * a + kk], peer).wait_recv())
        each(lambda a, kk, bp, peer: _rcopy(ins[a].at[_SHARD[names[a]](bp)], outs[a].at[kk], send.at[3 * a + kk], recv.at[3 * a + kk], peer).wait_send())

    return _Rider([part[nm] for nm in names], [jax.ShapeDtypeStruct((N_CHIPS - 1,) + _SHARD_SHAPE[nm], part[nm].dtype) for nm in names],
                  3 * n, start, finish)


def _sibling_exchange(mine0, mine1):
    names = list(mine0)
    n = len(names)

    def body(*refs):
        l0, l1, outs = refs[:n], refs[n:2 * n], refs[2 * n:3 * n]
        send, recv = refs[3 * n:]
        x, y, c = _pos()
        for c0 in (0, 1):
            @pl.when(c == c0)
            def _():
                srcs = l0 if c0 == 0 else l1
                cps = [_rcopy(srcs[a], outs[a], send.at[a], recv.at[a], (x, y, 1 - c0)) for a in range(n)]
                for cp in cps:
                    cp.start()
                for cp in cps:
                    cp.wait()

    outs = pl.pallas_call(
        body, name="grad_sibling_exchange", in_specs=[ANY] * (2 * n), out_specs=[ANY] * n,
        out_shape=[jax.ShapeDtypeStruct(mine0[nm].shape, mine0[nm].dtype) for nm in names],
        scratch_shapes=[_dma_sems(n), _dma_sems(n)],
    )(*[mine0[nm] for nm in names], *[mine1[nm] for nm in names])
    return dict(zip(names, outs))


def _small_rider(pack):
    flips = [(fx, fy, fc) for fx in (0, 1) for fy in (0, 1) for fc in (0, 1) if fx or fy or fc]

    def peers():
        x, y, c = _pos()
        return 4 * x + 2 * y + c, [(x ^ fx, y ^ fy, c ^ fc) for fx, fy, fc in flips]

    def start(ins, outs, send, recv):
        me, to = peers()
        for k, peer in enumerate(to):
            _rcopy(ins[0], outs[0].at[me], send.at[k], recv.at[k], peer).start()

    def finish(ins, outs, send, recv):
        me, to = peers()
        for k, (px, py, pc) in enumerate(to):
            slot = outs[0].at[4 * px + 2 * py + pc]
            _rcopy(slot, slot, send.at[k], recv.at[k], (px, py, pc)).wait_recv()
        for k, peer in enumerate(to):
            _rcopy(ins[0], outs[0].at[me], send.at[k], recv.at[k], peer).wait_send()

    return _Rider([pack], [jax.ShapeDtypeStruct((8,) + pack.shape, pack.dtype)], len(flips), start, finish)


def _sum_devices(gathered, pack, me):
    n, R, C = gathered.shape

    def body(me_ref, r_ref, own_ref, o_ref):
        acc = jnp.where(me_ref[0] == 0, own_ref[...], r_ref[0])
        for s in range(1, n):
            acc = acc + jnp.where(me_ref[0] == s, own_ref[...], r_ref[s])
        o_ref[...] = acc

    return pl.pallas_call(
        body, name="sum_devices_small",
        grid_spec=pltpu.PrefetchScalarGridSpec(num_scalar_prefetch=1, grid=(1,),
                                               in_specs=[pl.BlockSpec((n, R, C), lambda i, m: (0, 0, 0)), pl.BlockSpec((R, C), lambda i, m: (0, 0))],
                                               out_specs=pl.BlockSpec((R, C), lambda i, m: (0, 0))),
        out_shape=jax.ShapeDtypeStruct((R, C), F32), compiler_params=_cp("arbitrary"),
    )(_scalar(me), gathered, pack)


_WEIGHTS = ["norm_mix_g", "w_in", "sgu_ln_g", "sgu_ln_b", "w_spatial", "b_spatial", "conv_w", "w_branch", "w_out", "norm_xa_g",
            "mem_norm_g", "w_q_xa", "w_k_xa", "w_v_xa", "w_o_xa", "norm_ffn_g", "w_gate_ffn", "w_up_ffn", "w_down_ffn", "final_g"]
_REPLICATED = ["norm_mix_g", "sgu_ln_g", "sgu_ln_b", "w_spatial", "b_spatial", "norm_xa_g", "mem_norm_g", "norm_ffn_g", "final_g"]
_SQUARE = ["w_out", "w_q_xa", "w_o_xa", "w_k_xa", "w_v_xa"]
_BIG = ["w_in", "w_br", "w_sq", "w_gu", "w_dn"]


def _pack(arrs):
    return jnp.concatenate([a.reshape(-1) for a in arrs]).reshape(-1, 128)


def _step(a):
    w = {n: a[n] for n in _WEIGHTS}
    x, mem, target = a["x"][0], a["mem"][0], a["loss_target"][0]
    xi, yi, ci = _pos()
    bi = 2 * xi + yi
    groups = list(_FULL_SHAPE)

    tr = lambda t: jnp.swapaxes(t, 1, 2)
    local = {"w_in": w["w_in"], "w_br": w["w_branch"], "w_sq": jnp.stack([w[n] for n in _SQUARE], axis=1),
             "w_gu": jnp.stack([tr(w["w_gate_ffn"]), tr(w["w_up_ffn"])], axis=1), "w_dn": w["w_down_ffn"],
             "conv_w": jnp.pad(w["conv_w"], ((0, 0), (0, CONV_ROWS - 3), (0, 0)))}
    placed = [dict(), dict()]
    for n in groups:
        placed[0][n], placed[1][n] = _place(n, local[n], bi)

    def gather(l, names):
        return _gather_rider(l, {n: placed[l][n] for n in names})

    def start_of(l, w_in):
        return {"w_in": w_in, **{n: w[n][l] for n in _REPLICATED if n != "final_g"}}

    w_in0, = _run_rider(gather(0, ["w_in"]), name="gather_first")
    h, sv0, (w_in1,) = _layer_fwd(x, mem, start_of(0, w_in0), {"mm_in": gather(0, _GATHER_LATE), "sb": gather(0, _GATHER_LAST),
                                                                "mm_gu": gather(1, ["w_in"])})
    h, sv1, _ = _layer_fwd(h, mem, start_of(1, w_in1), {"mm_in": gather(1, _GATHER_LATE), "sb": gather(1, _GATHER_LAST), "mm_gu": None})
    loss, dx, d_final = _loss_head(h, w["final_g"], target)
    loss = lax.psum(loss, ("x", "y", "c"))

    pipe = _GradPipe(ci, bi)
    dx, g1 = _layer_bwd(dx, mem, sv1, 1, pipe)
    dx, g0 = _layer_bwd(dx, mem, sv0, 0, pipe)
    grads = [g0, g1]
    small = {n: jnp.stack([g[n] for g in grads]) for n in _REPLICATED if n != "final_g"}
    small["final_g"] = d_final
    conv_g = jnp.stack([g["conv_w"] for g in grads])
    small_pack = _pack([small[n] for n in _REPLICATED] + [conv_g])
    gathered, = pipe.drain(_small_rider(small_pack))
    mine = [pipe.reduced(0), pipe.reduced(1)]
    theirs = _sibling_exchange(mine[0], mine[1])

    out = {}

    def adam_layers(name, group, pick=None, view=lambda t: t):
        sel = (lambda t: t[group]) if pick is None else (lambda t: t[group][pick])
        res = _adam_layers([sel(mine[0]), sel(mine[1])], sel(theirs), ci, view(w[name]), view(a["m_" + name]), view(a["v_" + name]), name="adam_" + name)
        out[name] = tuple(view(r) for r in res)

    adam_layers("w_in", "w_in")
    adam_layers("w_branch", "w_br")
    for t, n in enumerate(_SQUARE):
        adam_layers(n, "w_sq", t)
    adam_layers("w_gate_ffn", "w_gu", 0, tr)
    adam_layers("w_up_ffn", "w_gu", 1, tr)
    adam_layers("w_down_ffn", "w_dn")

    def adam(name, g):
        out[name] = _reduce_adam([g], w[name], a["m_" + name], a["v_" + name], name="adam_" + name)

    n_rep = sum(w[n].size for n in _REPLICATED) // 128
    summed = _sum_devices(gathered, small_pack, 4 * xi + 2 * yi + ci)
    res = _reduce_adam([summed[:n_rep]], _pack([w[n] for n in _REPLICATED]), _pack([a["m_" + n] for n in _REPLICATED]),
                       _pack([a["v_" + n] for n in _REPLICATED]), name="adam_replicated")
    off = 0
    for n in _REPLICATED:
        out[n] = tuple(r.reshape(-1)[off:off + w[n].size].reshape(w[n].shape) for r in res)
        off += w[n].size
    conv_full = summed[n_rep:].reshape(conv_g.shape)
    adam("conv_w", lax.dynamic_slice_in_dim(conv_full, (2 * xi + yi) * 128, 128, axis=2))

    return (loss, dx[None], *[out[n][k] for k in range(4) for n in _WEIGHTS])


def kernel(x, mem, norm_mix_g, w_in, sgu_ln_g, sgu_ln_b, w_spatial, b_spatial, conv_w, w_branch, w_out, norm_xa_g, mem_norm_g, w_q_xa, w_k_xa, w_v_xa, w_o_xa, norm_ffn_g, w_gate_ffn, w_up_ffn, w_down_ffn, final_g, loss_target, m_norm_mix_g, m_w_in, m_sgu_ln_g, m_sgu_ln_b, m_w_spatial, m_b_spatial, m_conv_w, m_w_branch, m_w_out, m_norm_xa_g, m_mem_norm_g, m_w_q_xa, m_w_k_xa, m_w_v_xa, m_w_o_xa, m_norm_ffn_g, m_w_gate_ffn, m_w_up_ffn, m_w_down_ffn, m_final_g, v_norm_mix_g, v_w_in, v_sgu_ln_g, v_sgu_ln_b, v_w_spatial, v_b_spatial, v_conv_w, v_w_branch, v_w_out, v_norm_xa_g, v_mem_norm_g, v_w_q_xa, v_w_k_xa, v_w_v_xa, v_w_o_xa, v_norm_ffn_g, v_w_gate_ffn, v_w_up_ffn, v_w_down_ffn, v_final_g):
    return _step(dict(locals()))
```

```python
import functools
import math

import jax
import jax.numpy as jnp
from jax import lax
from jax.experimental import pallas as pl
from jax.experimental.pallas import tpu as pltpu

F32, BF16 = jnp.float32, jnp.bfloat16
MESH = pl.DeviceIdType.MESH
ANY = pl.BlockSpec(memory_space=pl.ANY)

D_MODEL = 1024
DEPTH = 2
BW = 512
SB_HEADS, SB_DH = 8, 64
SGU_LEN, SGU_GROUPS, SGU_GD, SGU_CHUNK = 128, 4, 128, 64
XA_HEADS, XA_DH = 4, 256
FFN_SH = 704
N_CHIPS = 4
IN_COLS = 7168
C_Z, C_CB, C_GATES = 1536, 2560, 4096

ADAM_LR, ADAM_B1, ADAM_B2, ADAM_EPS, ADAM_WD, ADAM_STEP = 0.001, 0.9, 0.999, 1e-08, 0.01, 10

VMEM_LIMIT_V7X = 56 * 1024 * 1024

NN = (((1,), (0,)), ((), ()))
NT = (((1,), (1,)), ((), ()))
TN = (((0,), (0,)), ((), ()))


def _cp(*sem):
    return pltpu.CompilerParams(dimension_semantics=sem, vmem_limit_bytes=VMEM_LIMIT_V7X)


def _tile(n, pref):
    for t in pref:
        if n % t == 0:
            return t
    return n


def _rows(r, row_bytes, block_bytes=1 << 20):
    fits = [t for t in range(8, r + 1, 8) if r % t == 0 and t * row_bytes <= block_bytes]
    return max(fits) if fits else r


class _Rider:
    def __init__(self, ins, outs, n_sems, start, finish, alias=None, middle=None):
        self.ins, self.outs, self.n_sems, self.alias = list(ins), list(outs), n_sems, alias or {}
        self.start, self.middle, self.finish = start, middle, finish


class _Sems:
    def __init__(self, ref, first):
        self.ref, self.first = ref, first

    @property
    def at(self):
        return self

    def __getitem__(self, k):
        return self.ref.at[self.first + k]


def _join(riders):
    riders = [r for r in riders if r is not None]
    if not riders:
        return None
    spans, i0, o0, s0 = [], 0, 0, 0
    for r in riders:
        spans.append((r, i0, o0, s0))
        i0, o0, s0 = i0 + len(r.ins), o0 + len(r.outs), s0 + r.n_sems

    def phase(which):
        def run(ins, outs, send, recv):
            for r, i, o, s in spans:
                fn = getattr(r, which)
                if fn is not None:
                    fn(ins[i:i + len(r.ins)], outs[o:o + len(r.outs)], _Sems(send, s), _Sems(recv, s))
        return run

    joined = _Rider([a for r in riders for a in r.ins], [a for r in riders for a in r.outs], s0, phase("start"), phase("finish"),
                    alias={o + k: i + v for r, i, o, s in spans for k, v in r.alias.items()}, middle=phase("middle"))
    joined.split = lambda landed: [list(landed[o:o + len(r.outs)]) for r, i, o, s in spans]
    return joined


def _call_with_rider(rider, body, *, name, grid, in_specs, args, out_specs, out_shape, scratch_shapes, semantics):
    if rider is None:
        return pl.pallas_call(body, name=name, grid=grid, in_specs=in_specs, out_specs=out_specs, out_shape=out_shape,
                              scratch_shapes=scratch_shapes, compiler_params=_cp(*semantics))(*args)
    n_in, n_out, r_in, r_out = len(args), len(out_shape), len(rider.ins), len(rider.outs)

    def riding(*refs):
        ins, rins = refs[:n_in], refs[n_in:n_in + r_in]
        outs, routs = refs[n_in + r_in:n_in + r_in + n_out], refs[n_in + r_in + n_out:n_in + r_in + n_out + r_out]
        rest = refs[n_in + r_in + n_out + r_out:]
        scratch, send, recv = rest[:-2], rest[-2], rest[-1]
        step = pl.program_id(0)
        for ax in range(1, len(grid)):
            step = step * grid[ax] + pl.program_id(ax)
        n_steps = math.prod(grid)

        @pl.when(step == 0)
        def _():
            rider.start(rins, routs, send, recv)

        body(*ins, *outs, *scratch)

        if rider.middle is not None:
            @pl.when(step == (3 * n_steps) // 5)
            def _():
                rider.middle(rins, routs, send, recv)

        @pl.when(step == n_steps - 1)
        def _():
            rider.finish(rins, routs, send, recv)

    return pl.pallas_call(
        riding, name=name, grid=grid, in_specs=list(in_specs) + [ANY] * r_in, out_specs=list(out_specs) + [ANY] * r_out,
        out_shape=list(out_shape) + rider.outs, scratch_shapes=list(scratch_shapes) + [_dma_sems(rider.n_sems), _dma_sems(rider.n_sems)],
        input_output_aliases={n_in + i: n_out + o for o, i in rider.alias.items()},
        compiler_params=_cp(*["arbitrary"] * len(grid)),
    )(*args, *rider.ins)


def _run_rider(rider, *, name):
    def nothing(*refs):
        pass

    return _call_with_rider(rider, nothing, name=name, grid=(1,), in_specs=[], args=[], out_specs=[], out_shape=[], scratch_shapes=[],
                            semantics=("arbitrary",))


def _mm(a, b, *, mode, name, out_dtype=F32, res=None, rms_bwd=None, a_kind="2d", b_kind="2d", tm=None, tn=None, tk=None, rider=None):
    a2, b2 = a.shape[-2:], b.shape[-2:]
    if mode == "nn":
        (M, K), N = a2, b2[1]
    elif mode == "nt":
        (M, K), N = a2, b2[0]
    else:
        (K, M), N = a2, b2[1]
    kchunk = a_kind == "kchunk" or b_kind == "kchunk"
    batch = a_kind == "batch" or b_kind == "batch"
    G = (a.shape[0] if a_kind == "batch" else b.shape[0]) if batch else 1
    tm = tm or _tile(M, (1024, 512, 256, 128))
    tn = tn or _tile(N, (1024, 512, 256, 128))
    if kchunk:
        tk, nk = K, (a.shape[0] if a_kind == "kchunk" else b.shape[0])
    else:
        tk = tk or _tile(K, (1024, 512, 256, 128))
        nk = K // tk

    def spec(kind, blk, idx):
        if kind == "2d":
            return pl.BlockSpec(blk, lambda g, i, j, k: idx(g, i, j, k))
        if kind == "batch":
            return pl.BlockSpec((None,) + blk, lambda g, i, j, k: (g,) + idx(g, i, j, k))
        return pl.BlockSpec((None,) + blk, lambda g, i, j, k: (k,) + idx(g, i, j, 0))

    if mode == "nn":
        a_spec = spec(a_kind, (tm, tk), lambda g, i, j, k: (i, k))
        b_spec = spec(b_kind, (tk, tn), lambda g, i, j, k: (k, j))
    elif mode == "nt":
        a_spec = spec(a_kind, (tm, tk), lambda g, i, j, k: (i, k))
        b_spec = spec(b_kind, (tn, tk), lambda g, i, j, k: (j, k))
    else:
        a_spec = spec(a_kind, (tk, tm), lambda g, i, j, k: (k, i))
        b_spec = spec(b_kind, (tk, tn), lambda g, i, j, k: (k, j))
    o_kind = "batch" if batch else "2d"
    o_spec = spec(o_kind, (tm, tn), lambda g, i, j, k: (i, j))
    o_shape = ((G,) if batch else ()) + (M, N)
    dn = {"nn": NN, "nt": NT, "tn": TN}[mode]
    has_res, has_rms = res is not None, rms_bwd is not None
    assert not (has_res and has_rms) and (not has_rms or (tn == N and G == 1))

    def body(*refs):
        if has_res:
            a_ref, b_ref, r_ref, o_ref = refs[:4]
        elif has_rms:
            a_ref, b_ref, x_ref, g_ref, dres_ref, o_ref, dg_ref = refs[:7]
        else:
            a_ref, b_ref, o_ref = refs[:3]
        p = lax.dot_general(a_ref[...].astype(BF16), b_ref[...].astype(BF16), dn, preferred_element_type=F32)
        first_rows = pl.program_id(1) == 0

        def finish(r):
            if has_rms:
                xv = x_ref[...]
                rs = lax.rsqrt(jnp.mean(xv * xv, axis=-1, keepdims=True) + 1e-6)
                u = r * g_ref[...]
                s = jnp.sum(u * xv, axis=-1, keepdims=True)
                o_ref[...] = dres_ref[...] + rs * u - xv * ((rs * rs * rs) * (s * (1.0 / N)))
                part = jnp.sum(r * (xv * rs), axis=0, keepdims=True)

                @pl.when(first_rows)
                def _():
                    dg_ref[...] = part

                @pl.when(jnp.logical_not(first_rows))
                def _():
                    dg_ref[...] += part
                return
            if has_res:
                r = r + r_ref[...]
            o_ref[...] = r.astype(out_dtype)

        if nk == 1:
            finish(p)
        else:
            acc = refs[-1]
            k = pl.program_id(3)

            @pl.when(k == 0)
            def _():
                acc[...] = p

            @pl.when(k > 0)
            def _():
                acc[...] += p

            @pl.when(k == nk - 1)
            def _():
                finish(acc[...])

    in_specs, args = [a_spec, b_spec], [a, b]
    tile = spec("2d", (tm, tn), lambda g, i, j, k: (i, j))
    out_specs, out_shape = [o_spec], [jax.ShapeDtypeStruct(o_shape, out_dtype)]
    if has_res:
        in_specs.append(tile)
        args.append(res)
    if has_rms:
        x, gain, dres = rms_bwd
        vec = pl.BlockSpec((1, tn), lambda g, i, j, k: (0, 0))
        in_specs += [tile, vec, tile]
        args += [x, gain.reshape(1, N), dres]
        out_specs.append(vec)
        out_shape.append(jax.ShapeDtypeStruct((1, N), F32))
    outs = _call_with_rider(
        rider, body, name=name, grid=(G, M // tm, N // tn, nk), in_specs=in_specs, args=args, out_specs=out_specs,
        out_shape=out_shape, scratch_shapes=[pltpu.VMEM((tm, tn), F32)] if nk > 1 else [],
        semantics=("parallel", "arbitrary" if has_rms else "parallel", "parallel", "arbitrary"))
    main = (outs[0], outs[1].reshape(N)) if has_rms else outs[0]
    return main if rider is None else (main, outs[len(out_shape):])


def _rms_fwd(x, g, *, name):
    S, Dm = x.shape
    tm = _tile(S, (512, 256))

    def body(x_ref, g_ref, o_ref):
        xv = x_ref[...]
        r = lax.rsqrt(jnp.mean(xv * xv, axis=-1, keepdims=True) + 1e-6)
        o_ref[...] = (xv * r * g_ref[...]).astype(BF16)

    return pl.pallas_call(
        body, name=name, grid=(S // tm,),
        in_specs=[pl.BlockSpec((tm, Dm), lambda i: (i, 0)), pl.BlockSpec((1, Dm), lambda i: (0, 0))],
        out_specs=pl.BlockSpec((tm, Dm), lambda i: (i, 0)), out_shape=jax.ShapeDtypeStruct((S, Dm), BF16),
        compiler_params=_cp("parallel"),
    )(x, g.reshape(1, Dm))


def _rms_bwd(x, g, dh, dres, *, name):
    S, Dm = x.shape
    tm = _tile(S, (512, 256))

    def body(x_ref, g_ref, dh_ref, dr_ref, dx_ref, dg_ref):
        xv, dhv = x_ref[...], dh_ref[...].astype(F32)
        r = lax.rsqrt(jnp.mean(xv * xv, axis=-1, keepdims=True) + 1e-6)
        u = dhv * g_ref[...]
        s = jnp.sum(u * xv, axis=-1, keepdims=True)
        dx_ref[...] = dr_ref[...] + r * u - xv * ((r * r * r) * (s * (1.0 / Dm)))
        part = jnp.sum(dhv * (xv * r), axis=0, keepdims=True)

        @pl.when(pl.program_id(0) == 0)
        def _():
            dg_ref[...] = part

        @pl.when(pl.program_id(0) > 0)
        def _():
            dg_ref[...] += part

    row = pl.BlockSpec((tm, Dm), lambda i: (i, 0))
    vec = pl.BlockSpec((1, Dm), lambda i: (0, 0))
    dx, dg = pl.pallas_call(
        body, name=name, grid=(S // tm,), in_specs=[row, vec, row, row], out_specs=[row, vec],
        out_shape=[jax.ShapeDtypeStruct((S, Dm), F32), jax.ShapeDtypeStruct((1, Dm), F32)],
        compiler_params=_cp("arbitrary"),
    )(x, g.reshape(1, Dm), dh, dres)
    return dx, dg.reshape(Dm)


def _loss_head(x, g, target):
    S, Dm = x.shape
    tm = _tile(S, (512, 256))

    def body(x_ref, g_ref, t_ref, dx_ref, dg_ref, loss_ref):
        xv, gv = x_ref[...], g_ref[...]
        r = lax.rsqrt(jnp.mean(xv * xv, axis=-1, keepdims=True) + 1e-6)
        xn = xv * r
        err = xn * gv - t_ref[...]
        lpart = 0.5 * jnp.sum(jnp.mean(err * err, axis=-1, keepdims=True), axis=0, keepdims=True)
        dy = err * (1.0 / Dm)
        u = dy * gv
        s = jnp.sum(u * xv, axis=-1, keepdims=True)
        dx_ref[...] = r * u - xv * ((r * r * r) * (s * (1.0 / Dm)))
        part = jnp.sum(dy * xn, axis=0, keepdims=True)
        lslab = jnp.broadcast_to(lpart, (8, 128))

        @pl.when(pl.program_id(0) == 0)
        def _():
            dg_ref[...] = part
            loss_ref[...] = lslab

        @pl.when(pl.program_id(0) > 0)
        def _():
            dg_ref[...] += part
            loss_ref[...] += lslab

    row = pl.BlockSpec((tm, Dm), lambda i: (i, 0))
    vec = pl.BlockSpec((1, Dm), lambda i: (0, 0))
    dx, dg, loss = pl.pallas_call(
        body, name="loss_head", grid=(S // tm,), in_specs=[row, vec, row],
        out_specs=[row, vec, pl.BlockSpec((8, 128), lambda i: (0, 0))],
        out_shape=[jax.ShapeDtypeStruct((S, Dm), F32), jax.ShapeDtypeStruct((1, Dm), F32), jax.ShapeDtypeStruct((8, 128), F32)],
        compiler_params=_cp("arbitrary"),
    )(x, g.reshape(1, Dm), target)
    return loss[0, 0], dx, dg.reshape(Dm)


SB_TQ, SB_TK = 256, 256
SB_EXP_FLOOR = -104.0


def _split2(v):
    hi = v.astype(BF16)
    return jnp.concatenate([hi, (v - hi.astype(F32)).astype(BF16)], axis=1)


def _tri2(cmp):
    j = lax.broadcasted_iota(jnp.int32, (2 * SB_TK, SB_TK), 0) % SB_TK
    s = lax.broadcasted_iota(jnp.int32, (2 * SB_TK, SB_TK), 1)
    return cmp(j, s).astype(BF16)


def _sb_scores(qv, kb, k0, q0, tq):
    rows = qv.shape[0]
    z = lax.dot_general(qv, kb, NT, preferred_element_type=F32) * (SB_DH ** -0.5)
    t_pos = q0 + lax.broadcasted_iota(jnp.int32, (rows, SB_TK), 0) % tq
    s_pos = k0 + lax.broadcasted_iota(jnp.int32, (rows, SB_TK), 1)
    valid = s_pos < t_pos
    ls = jnp.minimum(z, 0.0) - jnp.log(1.0 + jnp.exp(-jnp.abs(z)))
    l1m = jnp.where(valid, ls - z, 0.0)
    return z, valid, ls, l1m


SB_PAIRS = SB_HEADS // 2
_Q_BLK, _K_BLK, _V_BLK = 0, SB_PAIRS, 2 * SB_PAIRS


def _wide(x):
    return x if SB_TK == 128 else jnp.concatenate([x] * (SB_TK // 128), axis=1)


def _lanes_of(h, shape):
    lane = lax.broadcasted_iota(jnp.int32, shape, len(shape) - 1)
    return (lane < SB_DH) if h == 0 else (lane >= SB_DH)


def _sb2_fwd(p, rider=None):
    S = p.shape[0]
    tq = min(SB_TQ, S)
    kb_per_q = tq // SB_TK

    def body(q_ref, k_ref, v_ref, o_ref, tot_ref, cnt_ref, qm, acc, c):
        i = pl.program_id(1)
        q0 = i * tq
        later = _tri2(lambda j, s: j > s)
        q2 = q_ref[...]
        for h in range(2):
            qm[h * tq:(h + 1) * tq, :] = jnp.where(_lanes_of(h, q2.shape), q2, 0.0).astype(BF16)
        acc[...] = jnp.zeros_like(acc)
        c[...] = jnp.zeros_like(c)
        nkb = (i + 1) * kb_per_q

        def more(st):
            n, highest = st
            return (n < nkb) & (highest > SB_EXP_FLOOR)

        def step(st):
            n, _ = st
            k0 = pl.multiple_of((nkb - 1 - n) * SB_TK, SB_TK)
            kb, vb = k_ref[pl.ds(k0, SB_TK), :].astype(BF16), v_ref[pl.ds(k0, SB_TK), :].astype(BF16)
            c_old = c[...]
            z, valid, ls, l1m = _sb_scores(qm[...], kb, k0, q0, tq)
            c_new = c_old + jnp.sum(l1m, axis=1, keepdims=True)
            after = jnp.dot(_split2(l1m), later, preferred_element_type=F32)
            a = jnp.where(valid, jnp.exp(ls + after + _wide(c_old)), 0.0)
            av = jnp.dot(a.astype(BF16), vb, preferred_element_type=F32)
            acc[...] += jnp.where(_lanes_of(0, (tq, 128)), av[:tq], av[tq:])
            c[...] = c_new
            return n + 1, jnp.max(c_new)

        n_done, _ = lax.while_loop(more, step, (jnp.int32(0), jnp.float32(0.0)))
        o_ref[...] = acc[...].astype(o_ref.dtype)
        for h in range(2):
            tot_ref[h] = c[h * tq:(h + 1) * tq, :]
        cnt_ref[...] = jnp.full(cnt_ref.shape, n_done.astype(F32))

    col = lambda first: pl.BlockSpec((S, 128), lambda g, i: (0, first + g))
    outs = _call_with_rider(
        rider, body, name="sb_fwd", grid=(SB_PAIRS, S // tq), args=[p, p, p],
        in_specs=[pl.BlockSpec((tq, 128), lambda g, i: (i, _Q_BLK + g)), col(_K_BLK), col(_V_BLK)],
        out_specs=[pl.BlockSpec((tq, 128), lambda g, i: (i, g)), pl.BlockSpec((2, tq, 128), lambda g, i: (g, i, 0)),
                   pl.BlockSpec((None, None, 8, 128), lambda g, i: (g, i, 0, 0))],
        out_shape=[jax.ShapeDtypeStruct((S, BW), BF16), jax.ShapeDtypeStruct((SB_HEADS, S, 128), F32),
                   jax.ShapeDtypeStruct((SB_PAIRS, S // tq, 8, 128), F32)],
        scratch_shapes=[pltpu.VMEM((2 * tq, 128), BF16), pltpu.VMEM((tq, 128), F32), pltpu.VMEM((2 * tq, 128), F32)],
        semantics=("parallel", "parallel"))
    return outs[:3], outs[3:]


def _sb2_bwd(p, dbr, tot, cnt, rider=None):
    S = p.shape[0]
    tq = min(SB_TQ, S)
    kb_per_q = tq // SB_TK
    scale = SB_DH ** -0.5

    def body(q_ref, k_ref, v_ref, do_ref, tot_ref, cnt_ref, dq_ref, dk_ref, dv_ref, qm, dom, tot, dq_acc, pre, gpre):
        i = pl.program_id(1)
        q0 = i * tq
        upto = _tri2(lambda j, s: j <= s)
        before = _tri2(lambda j, s: j < s)

        @pl.when(i == 0)
        def _():
            dk_ref[...] = jnp.zeros_like(dk_ref)
            dv_ref[...] = jnp.zeros_like(dv_ref)

        q2, do2 = q_ref[...], do_ref[...]
        for h in range(2):
            rows = slice(h * tq, (h + 1) * tq)
            qm[rows, :] = jnp.where(_lanes_of(h, q2.shape), q2, 0.0).astype(BF16)
            dom[rows, :] = jnp.where(_lanes_of(h, do2.shape), do2, 0.0).astype(BF16)
            tot[rows, :] = tot_ref[h]
        dq_acc[...] = jnp.zeros_like(dq_acc)
        pre[...] = jnp.zeros_like(pre)
        gpre[...] = jnp.zeros_like(gpre)

        n_done = jnp.max(cnt_ref[...]).astype(jnp.int32)
        first = (i + 1) * kb_per_q - n_done

        def step(n, carry):
            k0 = pl.multiple_of((first + n) * SB_TK, SB_TK)
            kb, vb = k_ref[pl.ds(k0, SB_TK), :].astype(BF16), v_ref[pl.ds(k0, SB_TK), :].astype(BF16)
            pre_o, gpre_o = pre[...], gpre[...]
            z, valid, ls, l1m = _sb_scores(qm[...], kb, k0, q0, tq)
            incl = jnp.dot(_split2(l1m), upto, preferred_element_type=F32)
            rest = _wide(tot[...] - pre_o) - incl
            a = jnp.where(valid, jnp.exp(ls + rest), 0.0)
            da = lax.dot_general(dom[...], vb, NT, preferred_element_type=F32)
            g = a * da
            gbefore = jnp.dot(_split2(g), before, preferred_element_type=F32) + _wide(gpre_o)
            dz = jnp.where(valid, g * jnp.exp(ls - z) - jnp.exp(ls) * gbefore, 0.0) * scale
            dzb = dz.astype(BF16)
            dq_p = jnp.dot(dzb, kb, preferred_element_type=F32)
            dq_acc[...] += jnp.where(_lanes_of(0, (tq, 128)), dq_p[:tq], dq_p[tq:])
            dk_ref[pl.ds(k0, SB_TK), :] += lax.dot_general(dzb, qm[...], TN, preferred_element_type=F32)
            dv_ref[pl.ds(k0, SB_TK), :] += lax.dot_general(a.astype(BF16), dom[...], TN, preferred_element_type=F32)
            pre[...] = pre_o + jnp.sum(l1m, axis=1, keepdims=True)
            gpre[...] = gpre_o + jnp.sum(g, axis=1, keepdims=True)
            return carry

        lax.fori_loop(0, n_done, step, 0)
        dq_ref[...] = dq_acc[...].astype(dq_ref.dtype)

    col = lambda first: pl.BlockSpec((S, 128), lambda g, i: (0, first + g))
    tile = pl.BlockSpec((tq, 128), lambda g, i: (i, g))
    whole = pl.BlockSpec((S, 128), lambda g, i: (0, g))
    outs = _call_with_rider(
        rider, body, name="sb_bwd", grid=(SB_PAIRS, S // tq), args=[p, p, p, dbr, tot, cnt],
        in_specs=[pl.BlockSpec((tq, 128), lambda g, i: (i, _Q_BLK + g)), col(_K_BLK), col(_V_BLK),
                  pl.BlockSpec((None, tq, 128), lambda g, i: (0, i, g)), pl.BlockSpec((2, tq, 128), lambda g, i: (g, i, 0)),
                  pl.BlockSpec((None, None, 8, 128), lambda g, i: (g, i, 0, 0))],
        out_specs=[tile, whole, whole],
        out_shape=[jax.ShapeDtypeStruct((S, BW), BF16), jax.ShapeDtypeStruct((S, BW), F32), jax.ShapeDtypeStruct((S, BW), F32)],
        scratch_shapes=[pltpu.VMEM((2 * tq, 128), BF16), pltpu.VMEM((2 * tq, 128), BF16), pltpu.VMEM((2 * tq, 128), F32),
                        pltpu.VMEM((tq, 128), F32), pltpu.VMEM((2 * tq, 128), F32), pltpu.VMEM((2 * tq, 128), F32)],
        semantics=("parallel", "arbitrary"))
    return outs[:3], outs[3:]


_INV_SQRT2 = 0.7071067811865476
_INV_SQRT2PI = 0.3989422804014327


def _gelu(x):
    return 0.5 * x * (1.0 + lax.erf(x * _INV_SQRT2))


def _gelu_grad(x):
    return 0.5 * (1.0 + lax.erf(x * _INV_SQRT2)) + x * (_INV_SQRT2PI * jnp.exp(-0.5 * x * x))


def _sgu_mask():
    t = lax.broadcasted_iota(jnp.int32, (SGU_LEN, SGU_LEN), 0) // SGU_CHUNK
    s = lax.broadcasted_iota(jnp.int32, (SGU_LEN, SGU_LEN), 1) // SGU_CHUNK
    return t >= s


def _sgu_mask_t():
    t = lax.broadcasted_iota(jnp.int32, (SGU_LEN, SGU_LEN), 0) // SGU_CHUNK
    s = lax.broadcasted_iota(jnp.int32, (SGU_LEN, SGU_LEN), 1) // SGU_CHUNK
    return s >= t


def _sgu_norm(zv, g, b):
    vv = _gelu(zv)
    xc = vv - jnp.mean(vv, axis=-1, keepdims=True)
    rstd = lax.rsqrt(jnp.mean(xc * xc, axis=-1, keepdims=True) + 1e-5)
    xhat = xc * rstd
    return xhat, rstd, xhat * g + b


SGU_TM = 256


def _sgu_fwd(p, ln_g, ln_b, w_s, b_st):
    S = p.shape[0]
    tm = min(SGU_TM, S)

    def body(zu_ref, zv_ref, g_ref, b_ref, w_ref, bs_ref, o_ref):
        u = _gelu(zu_ref[...])
        _, _, vn = _sgu_norm(zv_ref[...], g_ref[...], b_ref[...])
        vnb = vn.astype(BF16)
        mask = _sgu_mask()
        for gi in range(SGU_GROUPS):
            wg = jnp.where(mask, w_ref[gi], 0.0).astype(BF16)
            cols = slice(gi * SGU_GD, (gi + 1) * SGU_GD)
            for ci in range(tm // SGU_LEN):
                rows = slice(ci * SGU_LEN, (ci + 1) * SGU_LEN)
                vm = jnp.dot(wg, vnb[rows, cols], preferred_element_type=F32) + bs_ref[:, gi:gi + 1]
                o_ref[rows, cols] = (u[rows, cols] * vm).astype(BF16)

    vec = pl.BlockSpec((1, BW), lambda i: (0, 0))
    return pl.pallas_call(
        body, name="sgu_fwd", grid=(S // tm,),
        in_specs=[pl.BlockSpec((tm, BW), lambda i: (i, C_Z // BW)), pl.BlockSpec((tm, BW), lambda i: (i, C_Z // BW + 1)), vec, vec,
                  pl.BlockSpec((SGU_GROUPS, SGU_LEN, SGU_LEN), lambda i: (0, 0, 0)), pl.BlockSpec((SGU_LEN, SGU_GROUPS), lambda i: (0, 0))],
        out_specs=pl.BlockSpec((tm, BW), lambda i: (i, 0)), out_shape=jax.ShapeDtypeStruct((S, BW), BF16),
        compiler_params=_cp("parallel"),
    )(p, p, ln_g.reshape(1, BW), ln_b.reshape(1, BW), w_s, b_st)


def _sgu_bwd(p, dyb, ln_g, ln_b, w_s, w_st, b_st):
    S = p.shape[0]
    tm = min(SGU_TM, S)

    def body(zu_ref, zv_ref, dy_ref, g_ref, b_ref, w_ref, wt_ref, bs_ref, dz_ref, dg_ref, db_ref, dw_ref, dbs_ref, dvn):
        first = pl.program_id(0) == 0

        @pl.when(first)
        def _():
            dg_ref[...] = jnp.zeros_like(dg_ref)
            db_ref[...] = jnp.zeros_like(db_ref)
            dw_ref[...] = jnp.zeros_like(dw_ref)
            dbs_ref[...] = jnp.zeros_like(dbs_ref)

        zu, zv, dy = zu_ref[...], zv_ref[...], dy_ref[...].astype(F32)
        u = _gelu(zu)
        xhat, rstd, vn = _sgu_norm(zv, g_ref[...], b_ref[...])
        vnb = vn.astype(BF16)
        mask = _sgu_mask()
        mask_t = _sgu_mask_t()
        for gi in range(SGU_GROUPS):
            wg = jnp.where(mask, w_ref[gi], 0.0).astype(BF16)
            wgt = jnp.where(mask_t, wt_ref[gi], 0.0).astype(BF16)
            cols = slice(gi * SGU_GD, (gi + 1) * SGU_GD)
            for ci in range(tm // SGU_LEN):
                rows = slice(ci * SGU_LEN, (ci + 1) * SGU_LEN)
                vm = jnp.dot(wg, vnb[rows, cols], preferred_element_type=F32) + bs_ref[:, gi:gi + 1]
                dyc = dy[rows, cols]
                dz_ref[rows, cols] = (dyc * vm * _gelu_grad(zu[rows, cols])).astype(BF16)
                dvm = dyc * u[rows, cols]
                dvmb = dvm.astype(BF16)
                dbs_ref[gi] += jnp.broadcast_to(jnp.sum(dvm, axis=1, keepdims=True), (SGU_LEN, SGU_GD))
                dw_ref[gi] += lax.dot_general(dvmb, vnb[rows, cols], NT, preferred_element_type=F32)
                dvn[rows, cols] = jnp.dot(wgt, dvmb, preferred_element_type=F32)
        dvnv = dvn[...]
        dg_ref[...] += jnp.sum(dvnv * xhat, axis=0, keepdims=True)
        db_ref[...] += jnp.sum(dvnv, axis=0, keepdims=True)
        dxh = dvnv * g_ref[...]
        dvv = rstd * (dxh - jnp.mean(dxh, axis=-1, keepdims=True) - xhat * jnp.mean(dxh * xhat, axis=-1, keepdims=True))
        dz_ref[:, BW:] = (dvv * _gelu_grad(zv)).astype(BF16)

        @pl.when(pl.program_id(0) == n_steps - 1)
        def _():
            for gi in range(SGU_GROUPS):
                dw_ref[gi] = jnp.where(mask, dw_ref[gi], 0.0)

    n_steps = S // tm
    vec = pl.BlockSpec((1, BW), lambda i: (0, 0))
    half = lambda c: pl.BlockSpec((tm, BW), lambda i: (i, c))
    wspec = pl.BlockSpec((SGU_GROUPS, SGU_LEN, SGU_LEN), lambda i: (0, 0, 0))
    dz, dg, db, dw, dbs = pl.pallas_call(
        body, name="sgu_bwd", grid=(n_steps,),
        in_specs=[half(C_Z // BW), half(C_Z // BW + 1), half(0), vec, vec, wspec, wspec,
                  pl.BlockSpec((SGU_LEN, SGU_GROUPS), lambda i: (0, 0))],
        out_specs=[pl.BlockSpec((tm, 2 * BW), lambda i: (i, 0)), vec, vec, wspec, wspec],
        out_shape=[jax.ShapeDtypeStruct((S, 2 * BW), BF16), jax.ShapeDtypeStruct((1, BW), F32), jax.ShapeDtypeStruct((1, BW), F32),
                   jax.ShapeDtypeStruct((SGU_GROUPS, SGU_LEN, SGU_LEN), F32), jax.ShapeDtypeStruct((SGU_GROUPS, SGU_LEN, SGU_GD), F32)],
        scratch_shapes=[pltpu.VMEM((tm, BW), F32)],
        compiler_params=_cp("arbitrary"),
    )(p, p, dyb, ln_g.reshape(1, BW), ln_b.reshape(1, BW), w_s, w_st, b_st)
    return dz, dg.reshape(BW), db.reshape(BW), dw, dbs[:, :, 0]


CONV_TC = 128


def _shift_down(y, n):
    rows = lax.broadcasted_iota(jnp.int32, y.shape, 0)
    return jnp.where(rows < n, 0.0, pltpu.roll(y, n, 0))


def _shift_up(y, n):
    rows = lax.broadcasted_iota(jnp.int32, y.shape, 0)
    return jnp.where(rows >= y.shape[0] - n, 0.0, pltpu.roll(y, y.shape[0] - n, 0))


def _conv_specs(S):
    col = lambda c0: pl.BlockSpec((S, CONV_TC), lambda j: (0, c0 // CONV_TC + j))
    return col(C_CB), col(C_CB + BW), col(C_CB + 2 * BW), pl.BlockSpec((3, CONV_TC), lambda j: (0, j)), pl.BlockSpec((S, CONV_TC), lambda j: (0, j))


def _conv_fwd(p, conv_w):
    S = p.shape[0]

    def body(cb_ref, cc_ref, cx_ref, w_ref, o_ref):
        y = cc_ref[...] * cx_ref[...]
        conv = w_ref[0:1, :] * _shift_down(y, 2) + w_ref[1:2, :] * _shift_down(y, 1) + w_ref[2:3, :] * y
        o_ref[...] = (cb_ref[...] * conv).astype(BF16)

    cb, cc, cx, wspec, out = _conv_specs(S)
    return pl.pallas_call(
        body, name="conv_fwd", grid=(BW // CONV_TC,), in_specs=[cb, cc, cx, wspec], out_specs=out,
        out_shape=jax.ShapeDtypeStruct((S, BW), BF16), compiler_params=_cp("parallel"),
    )(p, p, p, conv_w)


def _conv_bwd(p, conv_w, dyc):
    S = p.shape[0]

    def body(cb_ref, cc_ref, cx_ref, w_ref, dy_ref, db_ref, dc_ref, dx_ref, dw_ref):
        cc, cx, dy = cc_ref[...], cx_ref[...], dy_ref[...].astype(F32)
        y = cc * cx
        w0, w1, w2 = w_ref[0:1, :], w_ref[1:2, :], w_ref[2:3, :]
        y1, y2 = _shift_down(y, 1), _shift_down(y, 2)
        conv = w0 * y2 + w1 * y1 + w2 * y
        db_ref[...] = (dy * conv).astype(BF16)
        dconv = dy * cb_ref[...]
        dyy = w2 * dconv + w1 * _shift_up(dconv, 1) + w0 * _shift_up(dconv, 2)
        dc_ref[...] = (dyy * cx).astype(BF16)
        dx_ref[...] = (dyy * cc).astype(BF16)
        dw_ref[0:1, :] = jnp.sum(dconv * y2, axis=0, keepdims=True)
        dw_ref[1:2, :] = jnp.sum(dconv * y1, axis=0, keepdims=True)
        dw_ref[2:3, :] = jnp.sum(dconv * y, axis=0, keepdims=True)

    cb, cc, cx, wspec, out = _conv_specs(S)
    db, dc, dx, dw = pl.pallas_call(
        body, name="conv_bwd", grid=(BW // CONV_TC,), in_specs=[cb, cc, cx, wspec, out],
        out_specs=[out, out, out, wspec],
        out_shape=[jax.ShapeDtypeStruct((S, BW), BF16)] * 3 + [jax.ShapeDtypeStruct((3, BW), F32)],
        compiler_params=_cp("parallel"),
    )(p, p, p, conv_w, dyc)
    return db, dc, dx, dw


def _merge_specs(S, tm):
    gate = lambda n: pl.BlockSpec((tm, D_MODEL), lambda i: (i, C_GATES // D_MODEL + n))
    return [gate(0), gate(1), gate(2)], pl.BlockSpec((3, tm, D_MODEL), lambda i: (0, i, 0)), pl.BlockSpec((tm, D_MODEL), lambda i: (i, 0))


def _branch_merge(br, w_br, p):
    S = p.shape[0]
    tm = _tile(S, (512, 256))

    def body(br_ref, w_ref, g0, g1, g2, bd_ref, o_ref):
        acc = None
        for n, g_ref in enumerate((g0, g1, g2)):
            bdn = jnp.dot(br_ref[n], w_ref[n], preferred_element_type=F32)
            bd_ref[n] = bdn.astype(BF16)
            term = jax.nn.sigmoid(g_ref[...]) * bdn
            acc = term if acc is None else acc + term
        o_ref[...] = acc.astype(BF16)

    gates, bspec, row = _merge_specs(S, tm)
    return pl.pallas_call(
        body, name="mm_branch", grid=(S // tm,),
        in_specs=[pl.BlockSpec((3, tm, BW), lambda i: (0, i, 0)), pl.BlockSpec((3, BW, D_MODEL), lambda i: (0, 0, 0))] + gates,
        out_specs=[bspec, row], out_shape=[jax.ShapeDtypeStruct((3, S, D_MODEL), BF16), jax.ShapeDtypeStruct((S, D_MODEL), BF16)],
        compiler_params=_cp("parallel"),
    )(br, w_br, p, p, p)


def _merge_bwd(p, bd, dm):
    S = p.shape[0]
    tm = _tile(S, (256,))

    def body(g0, g1, g2, b_ref, dm_ref, db_ref, dg_ref):
        dmv = dm_ref[...]
        for n, g_ref in enumerate((g0, g1, g2)):
            sg = jax.nn.sigmoid(g_ref[...])
            db_ref[n] = (dmv * sg).astype(BF16)
            dg_ref[:, n * D_MODEL:(n + 1) * D_MODEL] = (dmv * b_ref[n].astype(F32) * (sg * (1.0 - sg))).astype(BF16)

    gates, bspec, row = _merge_specs(S, tm)
    return pl.pallas_call(
        body, name="merge_bwd", grid=(S // tm,), in_specs=gates + [bspec, row],
        out_specs=[bspec, pl.BlockSpec((tm, 3 * D_MODEL), lambda i: (i, 0))],
        out_shape=[jax.ShapeDtypeStruct((3, S, D_MODEL), BF16), jax.ShapeDtypeStruct((S, 3 * D_MODEL), BF16)],
        compiler_params=_cp("parallel"),
    )(p, p, p, bd, dm)


XA_TM = 512


def _xa_probs(qh, kh):
    s = lax.dot_general(qh, kh, NT, preferred_element_type=F32) * (XA_DH ** -0.5)
    e = jnp.exp(s - jnp.max(s, axis=-1, keepdims=True))
    return e / jnp.sum(e, axis=-1, keepdims=True)


def _xa_fwd(q, kv):
    S = q.shape[0]
    tm = min(XA_TM, S)
    M = kv.shape[1]

    def body(q_ref, kv_ref, o_ref):
        for h in range(XA_HEADS):
            cols = slice(h * XA_DH, (h + 1) * XA_DH)
            pr = _xa_probs(q_ref[:, cols], kv_ref[0, :, cols])
            o_ref[:, cols] = jnp.dot(pr.astype(BF16), kv_ref[1, :, cols], preferred_element_type=F32).astype(BF16)

    row = pl.BlockSpec((tm, D_MODEL), lambda i: (i, 0))
    return pl.pallas_call(
        body, name="xa_fwd", grid=(S // tm,), in_specs=[row, pl.BlockSpec((2, M, D_MODEL), lambda i: (0, 0, 0))], out_specs=row,
        out_shape=jax.ShapeDtypeStruct((S, D_MODEL), BF16), compiler_params=_cp("parallel"),
    )(q, kv)


def _xa_bwd(q, kv, do):
    S = q.shape[0]
    tm = min(XA_TM, S)
    M = kv.shape[1]

    def body(q_ref, kv_ref, do_ref, dq_ref, dkv_ref):
        @pl.when(pl.program_id(0) == 0)
        def _():
            dkv_ref[...] = jnp.zeros_like(dkv_ref)

        for h in range(XA_HEADS):
            cols = slice(h * XA_DH, (h + 1) * XA_DH)
            qh, kh, vh, doh = q_ref[:, cols], kv_ref[0, :, cols], kv_ref[1, :, cols], do_ref[:, cols]
            pr = _xa_probs(qh, kh)
            dkv_ref[1, :, cols] += lax.dot_general(pr.astype(BF16), doh, TN, preferred_element_type=F32)
            dp = lax.dot_general(doh, vh, NT, preferred_element_type=F32)
            ds = (pr * (dp - jnp.sum(dp * pr, axis=-1, keepdims=True)) * (XA_DH ** -0.5)).astype(BF16)
            dq_ref[:, cols] = jnp.dot(ds, kh, preferred_element_type=F32).astype(BF16)
            dkv_ref[0, :, cols] += lax.dot_general(ds, qh, TN, preferred_element_type=F32)

    row = pl.BlockSpec((tm, D_MODEL), lambda i: (i, 0))
    kvs = pl.BlockSpec((2, M, D_MODEL), lambda i: (0, 0, 0))
    return pl.pallas_call(
        body, name="xa_bwd", grid=(S // tm,), in_specs=[row, kvs, row], out_specs=[row, kvs],
        out_shape=[jax.ShapeDtypeStruct((S, D_MODEL), BF16), jax.ShapeDtypeStruct((2, M, D_MODEL), F32)],
        compiler_params=_cp("arbitrary"),
    )(q, kv, do)


def _ffn_up(h3, w_gu, rider=None):
    S = h3.shape[0]
    tm = _tile(S, (1024, 512, 256))

    def body(h_ref, w_ref, ab_ref, hh_ref):
        h = h_ref[...]
        a = lax.dot_general(h, w_ref[0], NT, preferred_element_type=F32)
        b = lax.dot_general(h, w_ref[1], NT, preferred_element_type=F32)
        ab_ref[0] = a.astype(BF16)
        ab_ref[1] = b.astype(BF16)
        hh_ref[...] = (a * jax.nn.sigmoid(a) * b).astype(BF16)

    pair = pl.BlockSpec((None, 2, tm, FFN_SH), lambda j, i: (j, 0, i, 0))
    outs = _call_with_rider(
        rider, body, name="mm_gu", grid=(N_CHIPS, S // tm), args=[h3, w_gu],
        in_specs=[pl.BlockSpec((tm, D_MODEL), lambda j, i: (i, 0)), pl.BlockSpec((None, 2, FFN_SH, D_MODEL), lambda j, i: (j, 0, 0, 0))],
        out_specs=[pair, pl.BlockSpec((None, tm, FFN_SH), lambda j, i: (j, i, 0))],
        out_shape=[jax.ShapeDtypeStruct((N_CHIPS, 2, S, FFN_SH), BF16), jax.ShapeDtypeStruct((N_CHIPS, S, FFN_SH), BF16)],
        scratch_shapes=[], semantics=("parallel", "parallel"))
    return outs[0], outs[1], outs[2:]


def _ffn_down_bwd(dx3, w_dn, ab):
    S = dx3.shape[0]
    tm = _tile(S, (1024, 512, 256))

    def body(dx_ref, w_ref, ab_ref, o_ref):
        d = lax.dot_general(dx_ref[...].astype(BF16), w_ref[...], NT, preferred_element_type=F32)
        a, b = ab_ref[0].astype(F32), ab_ref[1].astype(F32)
        sg = jax.nn.sigmoid(a)
        o_ref[0] = (d * b * (sg * (1.0 + a * (1.0 - sg)))).astype(BF16)
        o_ref[1] = (d * (a * sg)).astype(BF16)

    pair = pl.BlockSpec((None, 2, tm, FFN_SH), lambda j, i: (j, 0, i, 0))
    return pl.pallas_call(
        body, name="mm_down_dx", grid=(N_CHIPS, S // tm),
        in_specs=[pl.BlockSpec((tm, D_MODEL), lambda j, i: (i, 0)), pl.BlockSpec((None, FFN_SH, D_MODEL), lambda j, i: (j, 0, 0)), pair],
        out_specs=pair, out_shape=jax.ShapeDtypeStruct(ab.shape, BF16), compiler_params=_cp("parallel", "parallel"),
    )(dx3, w_dn, ab)


def _reduce_adam(parts, w, m, v, *, name):
    shape = w.shape
    C = shape[-1]
    R = math.prod(shape[:-1])
    tm = _rows(R, 4 * C)
    n = len(parts)
    c1, c2 = 1.0 - ADAM_B1 ** ADAM_STEP, 1.0 - ADAM_B2 ** ADAM_STEP

    def body(*refs):
        g = refs[0][...]
        for r in refs[1:n]:
            g = g + r[...]
        w_ref, m_ref, v_ref, go, do, mo, vo = refs[n:]
        mn = ADAM_B1 * m_ref[...] + (1.0 - ADAM_B1) * g
        vn = ADAM_B2 * v_ref[...] + (1.0 - ADAM_B2) * (g * g)
        go[...] = g
        do[...] = -ADAM_LR * ((mn / c1) / (jnp.sqrt(vn / c2) + ADAM_EPS) + ADAM_WD * w_ref[...])
        mo[...] = mn
        vo[...] = vn

    row = pl.BlockSpec((tm, C), lambda i: (i, 0))
    outs = pl.pallas_call(
        body, name=name, grid=(R // tm,), in_specs=[row] * (n + 3), out_specs=[row] * 4,
        out_shape=[jax.ShapeDtypeStruct((R, C), F32)] * 4, compiler_params=_cp("parallel"),
    )(*[a.reshape(R, C) for a in (*parts, w, m, v)])
    return tuple(o.reshape(shape) for o in outs)


_VIEW = {
    "w_in": ((1024, 7168), (1024, 1792), 256, lambda i, b: (i, b)),
    "w_br": ((1536, 1024), (1536, 256), 512, lambda i, b: (i, b)),
    "w_sq": ((5120, 1024), (1280, 1024), 256, lambda i, b: (4 * i + b, 0)),
    "w_gu": ((5632, 1024), (1408, 1024), 352, lambda i, b: (4 * b + i, 0)),
    "w_dn": ((2816, 1024), (704, 1024), 352, lambda i, b: (2 * b + i, 0)),
    "conv_w": ((8, 512), (8, 128), 8, lambda i, b: (0, b)),
}


def _scalar(v):
    return jnp.asarray(v, jnp.int32).reshape(1)


def _place(name, local, b):
    full2, sh2, tm, idx = _VIEW[name]
    C = sh2[1]
    dt = local.dtype if name == "conv_w" else BF16

    def body(b_ref, x_ref, o0_ref, o1_ref):
        o0_ref[...] = x_ref[0].astype(dt)
        o1_ref[...] = x_ref[1].astype(dt)

    place = pl.BlockSpec((tm, C), lambda i, bs: idx(i, bs[0]))
    outs = pl.pallas_call(
        body, name="place_" + name,
        grid_spec=pltpu.PrefetchScalarGridSpec(num_scalar_prefetch=1, grid=(sh2[0] // tm,),
                                               in_specs=[pl.BlockSpec((DEPTH, tm, C), lambda i, bs: (0, i, 0))], out_specs=[place, place]),
        out_shape=[jax.ShapeDtypeStruct(full2, dt)] * 2, compiler_params=_cp("arbitrary"),
    )(_scalar(b), local.reshape((DEPTH,) + sh2))
    return [o.reshape(_FULL_SHAPE[name]) for o in outs]


def _add_owner(name, g, land, own):
    shape = g.shape
    C = shape[-1]
    R = math.prod(shape[:-1])
    tm = _rows(R, 4 * C)

    def body(s_ref, g_ref, l_ref, o_ref):
        @pl.when(s_ref[0] != 0)
        def _():
            o_ref[...] = (g_ref[...].astype(F32) + l_ref[...].astype(F32)).astype(BF16)

        @pl.when(s_ref[0] == 0)
        def _():
            o_ref[...] = jnp.zeros_like(o_ref)

    pick = pl.BlockSpec((tm, C), lambda i, s: (jnp.where(s[0] != 0, i, 0), 0))
    return pl.pallas_call(
        body, name="presum_" + name,
        grid_spec=pltpu.PrefetchScalarGridSpec(num_scalar_prefetch=1, grid=(R // tm,), in_specs=[pick, pick],
                                               out_specs=pl.BlockSpec((tm, C), lambda i, s: (i, 0))),
        out_shape=jax.ShapeDtypeStruct((R, C), BF16), compiler_params=_cp("arbitrary"),
    )(_scalar(own), g.reshape(R, C), land.reshape(R, C)).reshape(shape)


def _sum_chips(name, slots, part, b, own):
    full2, sh2, tm, idx = _VIEW[name]
    C = sh2[1]

    def body(s_ref, slot_ref, own_ref, o_ref):
        @pl.when(s_ref[1] != 0)
        def _():
            o_ref[...] = ((slot_ref[0].astype(F32) + slot_ref[1].astype(F32)) + slot_ref[2].astype(F32)) + own_ref[...].astype(F32)

        @pl.when(s_ref[1] == 0)
        def _():
            o_ref[...] = jnp.zeros_like(o_ref)

    return pl.pallas_call(
        body, name="sum_chips_" + name,
        grid_spec=pltpu.PrefetchScalarGridSpec(
            num_scalar_prefetch=1, grid=(sh2[0] // tm,),
            in_specs=[pl.BlockSpec((3, tm, C), lambda i, s: (0, jnp.where(s[1] != 0, i, 0), 0)),
                      pl.BlockSpec((tm, C), lambda i, s: idx(jnp.where(s[1] != 0, i, 0), s[0]))],
            out_specs=pl.BlockSpec((tm, C), lambda i, s: (i, 0))),
        out_shape=jax.ShapeDtypeStruct(sh2, F32), compiler_params=_cp("arbitrary"),
    )(jnp.stack([jnp.asarray(b, jnp.int32), jnp.asarray(own, jnp.int32)]), slots.reshape((3,) + sh2),
      part.reshape(full2)).reshape(_SHARD_SHAPE[name])


def _adam_layers(mine, theirs, c, w, m, v, *, name):
    shape = w.shape
    C = shape[-1]
    R = math.prod(shape[1:-1])
    tm = _rows(R, 4 * C)
    c1, c2 = 1.0 - ADAM_B1 ** ADAM_STEP, 1.0 - ADAM_B2 ** ADAM_STEP

    def body(c_ref, m0_ref, m1_ref, t_ref, w_ref, m_ref, v_ref, go, do, mo, vo):
        layer = pl.program_id(0)
        g = jnp.where(layer == c_ref[0], jnp.where(layer == 0, m0_ref[...], m1_ref[...]), t_ref[...])
        mn = ADAM_B1 * m_ref[...] + (1.0 - ADAM_B1) * g
        vn = ADAM_B2 * v_ref[...] + (1.0 - ADAM_B2) * (g * g)
        go[...] = g
        do[...] = -ADAM_LR * ((mn / c1) / (jnp.sqrt(vn / c2) + ADAM_EPS) + ADAM_WD * w_ref[...])
        mo[...] = mn
        vo[...] = vn

    def own(layer):
        return pl.BlockSpec((tm, C), lambda l, i, cs: (jnp.where((l == layer) & (cs[0] == layer), i, 0), 0))

    recv = pl.BlockSpec((tm, C), lambda l, i, cs: (jnp.where(l == cs[0], 0, i), 0))
    row = pl.BlockSpec((None, tm, C), lambda l, i, cs: (l, i, 0))
    outs = pl.pallas_call(
        body, name=name,
        grid_spec=pltpu.PrefetchScalarGridSpec(num_scalar_prefetch=1, grid=(DEPTH, R // tm),
                                               in_specs=[own(0), own(1), recv, row, row, row], out_specs=[row] * 4),
        out_shape=[jax.ShapeDtypeStruct((DEPTH, R, C), F32)] * 4, compiler_params=_cp("arbitrary", "arbitrary"),
    )(_scalar(c), mine[0].reshape(R, C), mine[1].reshape(R, C), theirs.reshape(R, C), *[t.reshape(DEPTH, R, C) for t in (w, m, v)])
    return tuple(o.reshape(shape) for o in outs)


def _take_weights(wl, names, landed):
    for n, t in zip(names, landed):
        wl[n] = t[:3] if n == "conv_w" else t


_GATHER_LATE = ["w_br", "conv_w", "w_sq"]
_GATHER_LAST = ["w_gu", "w_dn"]


def _layer_fwd(x, mem, wl, ride=None):
    S = x.shape[0]
    sv = {"x": x}
    wl = dict(wl)
    h1 = _rms_fwd(x, wl["norm_mix_g"], name="rms_mix")
    if ride is None:
        ride = {"mm_in": None, "sb": None, "mm_gu": None}
        p = _mm(h1, wl["w_in"], mode="nn", name="mm_in")
    else:
        p, landed = _mm(h1, wl["w_in"], mode="nn", name="mm_in", rider=ride["mm_in"])
        _take_weights(wl, _GATHER_LATE, landed)
    (ya, tot, cnt), landed = _sb2_fwd(p, ride["sb"])
    _take_weights(wl, _GATHER_LAST, landed)
    b_st = wl["b_spatial"].T
    yb = _sgu_fwd(p, wl["sgu_ln_g"], wl["sgu_ln_b"], wl["w_spatial"], b_st)
    yc = _conv_fwd(p, wl["conv_w"])
    br = jnp.stack([ya, yb, yc])
    bd, merged = _branch_merge(br, wl["w_br"], p)
    x1 = _mm(merged, wl["w_sq"][0], mode="nn", res=x, name="mm_out")
    h2 = _rms_fwd(x1, wl["norm_xa_g"], name="rms_xa")
    qx = _mm(h2, wl["w_sq"][1], mode="nn", out_dtype=BF16, name="mm_q")
    mn = _rms_fwd(mem, wl["mem_norm_g"], name="rms_mem")
    kv = _mm(mn, wl["w_sq"][3:5], mode="nn", b_kind="batch", out_dtype=BF16, name="mm_kv")
    o = _xa_fwd(qx, kv)
    x2 = _mm(o, wl["w_sq"][2], mode="nn", res=x1, name="mm_o")
    h3 = _rms_fwd(x2, wl["norm_ffn_g"], name="rms_ffn")
    ab, hh, rode = _ffn_up(h3, wl["w_gu"], ride["mm_gu"])
    x3 = _mm(hh, wl["w_dn"], mode="nn", a_kind="kchunk", b_kind="kchunk", res=x2, name="mm_down")
    sv.update(h1=h1, p=p, tot=tot, cnt=cnt, br=br, bd=bd, merged=merged, x1=x1, h2=h2, qx=qx, mn=mn, kv=kv, o=o,
              x2=x2, h3=h3, ab=ab, hh=hh, b_st=b_st, wl=wl)
    return x3, sv, rode


class _GradPipe:
    def __init__(self, ci, bi):
        self.ci, self.bi, self.queue = ci, bi, []
        self.part, self.slots = [dict(), dict()], [dict(), dict()]

    def to_owner(self, layer, names, g):
        def arrived(land):
            own = (self.ci == layer).astype(jnp.int32)
            part = {n: _add_owner(n, g[n], t, own) for n, t in zip(names, land)}
            self.part[layer].update(part)
            self.queue.append((layer, names, part))

        return _presum_rider(layer, {n: g[n] for n in names}), arrived

    def exchange(self):
        if not self.queue:
            return None
        layer, names, part = self.queue.pop(0)
        return _shard_rider(layer, part), lambda slots: self.slots[layer].update(zip(names, slots))

    def drain(self, also):
        landed = None
        while self.queue or landed is None:
            job = self.exchange()
            both = _join([job[0] if job else None, also if landed is None else None])
            parts = both.split(_run_rider(both, name="grad_exchange_last"))
            if job:
                job[1](parts[0])
            if landed is None:
                landed = parts[-1]
        return landed

    def reduced(self, layer):
        own = (self.ci == layer).astype(jnp.int32)
        return {n: _sum_chips(n, self.slots[layer][n], self.part[layer][n], self.bi, own) for n in _BIG}


def _layer_bwd(dx3, mem, sv, layer=None, pipe=None):
    S = dx3.shape[0]
    p, wl = sv["p"], sv["wl"]
    g = {}

    def carrying(jobs, fn):
        jobs = [j for j in jobs if j]
        joined = _join([j[0] for j in jobs])
        out, landed = fn(joined)
        if joined is not None:
            for j, part in zip(jobs, joined.split(landed)):
                j[1](part)
        return out

    def mm(*args, job=None, jobs=(), **kw):
        return carrying([job, *jobs], lambda r: (_mm(*args, **kw), None) if r is None else _mm(*args, **kw, rider=r))

    to_owner = (lambda names: pipe.to_owner(layer, names, g)) if pipe else (lambda names: None)
    exchange = pipe.exchange if pipe else (lambda: None)

    g["w_dn"] = _mm(sv["hh"], dx3, mode="tn", a_kind="batch", out_dtype=BF16, name="mm_down_dw")
    dab = _ffn_down_bwd(dx3, wl["w_dn"], sv["ab"]).reshape(2 * N_CHIPS, S, FFN_SH)
    g["w_gu"] = mm(dab, sv["h3"], mode="tn", a_kind="batch", out_dtype=BF16, name="mm_gu_dw", job=exchange()).reshape(_FULL_SHAPE["w_gu"])
    dx2, g["norm_ffn_g"] = mm(dab, wl["w_gu"].reshape(2 * N_CHIPS, FFN_SH, D_MODEL), mode="nn", a_kind="kchunk", b_kind="kchunk", name="mm_gu_dx",
                               rms_bwd=(sv["x2"], wl["norm_ffn_g"], dx3), job=to_owner(["w_dn", "w_gu"]))
    do = _mm(dx2, wl["w_sq"][2], mode="nt", out_dtype=BF16, name="mm_o_dx")
    dw_o = _mm(sv["o"], dx2, mode="tn", out_dtype=BF16, name="mm_o_dw")
    dq, dkv = _xa_bwd(sv["qx"], sv["kv"], do)
    dw_q = _mm(sv["h2"], dq, mode="tn", out_dtype=BF16, name="mm_q_dw")
    dx1, g["norm_xa_g"] = _mm(dq, wl["w_sq"][1], mode="nt", name="mm_q_dx", rms_bwd=(sv["x1"], wl["norm_xa_g"], dx2))
    dw_kv = _mm(sv["mn"], dkv, mode="tn", b_kind="batch", out_dtype=BF16, name="mm_kv_dw")
    dmn = _mm(dkv, wl["w_sq"][3:5], mode="nt", a_kind="kchunk", b_kind="kchunk", name="mm_kv_dx")
    _, g["mem_norm_g"] = _rms_bwd(mem, wl["mem_norm_g"], dmn, jnp.zeros_like(mem), name="rms_mem_bwd")
    dm = _mm(dx1, wl["w_sq"][0], mode="nt", name="mm_out_dx")
    dw_out = _mm(sv["merged"], dx1, mode="tn", out_dtype=BF16, name="mm_out_dw")
    g["w_sq"] = jnp.concatenate([jnp.stack([dw_out, dw_q, dw_o]), dw_kv])
    dbd, dgates = _merge_bwd(p, sv["bd"], dm)
    dbr = mm(dbd, wl["w_br"], mode="nt", a_kind="batch", b_kind="batch", name="mm_branch_dx", job=to_owner(["w_sq"]))
    g["w_br"] = _mm(sv["br"], dbd, mode="tn", a_kind="batch", b_kind="batch", out_dtype=BF16, name="mm_branch_dw")
    dq, dk, dv = carrying([exchange(), to_owner(["w_br"])], lambda r: _sb2_bwd(p, dbr, sv["tot"], sv["cnt"], r))
    dz, g["sgu_ln_g"], g["sgu_ln_b"], g["w_spatial"], g["b_spatial"] = _sgu_bwd(
        p, dbr[1], wl["sgu_ln_g"], wl["sgu_ln_b"], wl["w_spatial"], wl["w_spatial"].transpose(0, 2, 1), sv["b_st"])
    dcb, dcc, dcx, g["conv_w"] = _conv_bwd(p, wl["conv_w"], dbr[2])
    dp = jnp.concatenate([dq, dk.astype(BF16), dv.astype(BF16), dz, dcb, dcc, dcx, dgates], axis=1)
    g["w_in"] = mm(sv["h1"], dp, mode="tn", out_dtype=BF16, name="mm_in_dw", jobs=[exchange(), exchange()])
    dx, g["norm_mix_g"] = mm(dp, wl["w_in"], mode="nt", name="mm_in_dx", rms_bwd=(sv["x"], wl["norm_mix_g"], dx1),
                             job=to_owner(["w_in"]))
    return dx, g


def _local_step(x, mem, target, layers, final_g):
    h, saved = x, []
    for wl in layers:
        h, sv, _ = _layer_fwd(h, mem, wl)
        saved.append(sv)
    loss, dx, d_final = _loss_head(h, final_g, target)
    grads = [None] * len(layers)
    for l in reversed(range(len(layers))):
        dx, grads[l] = _layer_bwd(dx, mem, saved[l])
    return loss, dx, grads, d_final


_ALL = slice(None)
CONV_ROWS = 8
_SHARD = {
    "w_in": lambda b: (_ALL, pl.ds(1792 * b, 1792)),
    "w_br": lambda b: (_ALL, _ALL, pl.ds(256 * b, 256)),
    "w_sq": lambda b: (_ALL, pl.ds(256 * b, 256), _ALL),
    "w_gu": lambda b: (b,),
    "w_dn": lambda b: (b,),
    "conv_w": lambda b: (_ALL, pl.ds(128 * b, 128)),
}
_FULL_SHAPE = {"w_in": (1024, 7168), "w_br": (3, 512, 1024), "w_sq": (5, 1024, 1024), "w_gu": (4, 2, 704, 1024),
               "w_dn": (4, 704, 1024), "conv_w": (CONV_ROWS, 512)}
_SHARD_SHAPE = {"w_in": (1024, 1792), "w_br": (3, 512, 256), "w_sq": (5, 256, 1024), "w_gu": (2, 704, 1024),
                "w_dn": (704, 1024), "conv_w": (CONV_ROWS, 128)}


def _pos():
    return lax.axis_index("x"), lax.axis_index("y"), lax.axis_index("c")


def _per_chip(fn):
    x, y, _ = _pos()
    for x0 in (0, 1):
        for y0 in (0, 1):
            @pl.when((x == x0) & (y == y0))
            def _():
                fn(x0, y0)


def _other_chips(x0, y0):
    return [(1 - x0, y0), (x0, 1 - y0), (1 - x0, 1 - y0)]


def _rcopy(src, dst, ssem, rsem, dev):
    return pltpu.make_async_remote_copy(src_ref=src, dst_ref=dst, send_sem=ssem, recv_sem=rsem, device_id=dev, device_id_type=MESH)


def _dma_sems(n):
    return pltpu.SemaphoreType.DMA((n,))


def _gather_rider(layer, placed):
    names = list(placed)
    n = len(names)
    shard = lambda refs, a, b: refs[a].at[_SHARD[names[a]](b)]

    def start(ins, outs, send, recv):
        @pl.when(lax.axis_index("c") == layer)
        def _():
            def run(x0, y0):
                for kk, (px, py) in enumerate(_other_chips(x0, y0)):
                    for a in range(n):
                        own = shard(outs, a, 2 * x0 + y0)
                        _rcopy(own, own, send.at[6 * a + kk], recv.at[6 * a + kk], (px, py, layer)).start()

            _per_chip(run)

    def passing(outs, send, recv, a, kk, bp, x0, y0):
        landed = shard(outs, a, bp)
        return _rcopy(landed, landed, send.at[6 * a + 3 + kk], recv.at[6 * a + 3 + kk], (x0, y0, 1 - layer))

    def middle(ins, outs, send, recv):
        @pl.when(lax.axis_index("c") == layer)
        def _():
            def run(x0, y0):
                for kk, (px, py) in enumerate(_other_chips(x0, y0)):
                    for a in range(n):
                        landed = shard(outs, a, 2 * px + py)
                        _rcopy(landed, landed, send.at[6 * a + kk], recv.at[6 * a + kk], (px, py, layer)).wait_recv()
                        passing(outs, send, recv, a, kk, 2 * px + py, x0, y0).start()

            _per_chip(run)

    def finish(ins, outs, send, recv):
        c = lax.axis_index("c")

        def run(x0, y0):
            chips = _other_chips(x0, y0)

            @pl.when(c == layer)
            def _():
                for kk, (px, py) in enumerate(chips):
                    for a in range(n):
                        own = shard(outs, a, 2 * x0 + y0)
                        _rcopy(own, own, send.at[6 * a + kk], recv.at[6 * a + kk], (px, py, layer)).wait_send()
                        passing(outs, send, recv, a, kk, 2 * px + py, x0, y0).wait_send()

            @pl.when(c != layer)
            def _():
                for kk, (px, py) in enumerate(chips):
                    for a in range(n):
                        got = shard(outs, a, 2 * px + py)
                        _rcopy(got, got, send.at[6 * a + 3 + kk], recv.at[6 * a + 3 + kk], (x0, y0, layer)).wait_recv()

        _per_chip(run)

    arrs = [placed[nm] for nm in names]
    return _Rider(arrs, [jax.ShapeDtypeStruct(t.shape, t.dtype) for t in arrs], 6 * n, start, finish, alias={a: a for a in range(n)}, middle=middle)


def _presum_rider(layer, grads):
    names = list(grads)
    n = len(names)

    def start(ins, outs, send, recv):
        x, y, c = _pos()

        @pl.when(c != layer)
        def _():
            for a in range(n):
                _rcopy(ins[a], outs[a], send.at[a], recv.at[a], (x, y, layer)).start()

    def finish(ins, outs, send, recv):
        x, y, c = _pos()

        @pl.when(c != layer)
        def _():
            for a in range(n):
                _rcopy(ins[a], outs[a], send.at[a], recv.at[a], (x, y, layer)).wait_send()

        @pl.when(c == layer)
        def _():
            for a in range(n):
                _rcopy(outs[a], outs[a], send.at[a], recv.at[a], (x, y, 1 - layer)).wait_recv()

    arrs = [grads[nm] for nm in names]
    return _Rider(arrs, [jax.ShapeDtypeStruct(t.shape, t.dtype) for t in arrs], n, start, finish)


def _shard_rider(layer, part):
    names = list(part)
    n = len(names)

    def each(fn):
        @pl.when(lax.axis_index("c") == layer)
        def _():
            def run(x0, y0):
                for kk, (px, py) in enumerate(_other_chips(x0, y0)):
                    for a in range(n):
                        fn(a, kk, 2 * px + py, (px, py, layer))

            _per_chip(run)

    def start(ins, outs, send, recv):
        each(lambda a, kk, bp, peer: _rcopy(ins[a].at[_SHARD[names[a]](bp)], outs[a].at[kk], send.at[3 * a + kk], recv.at[3 * a + kk], peer).start())

    def finish(ins, outs, send, recv):
        each(lambda a, kk, bp, peer: _rcopy(outs[a].at[kk], outs[a].at[kk], send.at[3 * a + kk], recv.at[3 * a + kk], peer).wait_recv())
        each(lambda a, kk, bp, peer: _rcopy(ins[a].at[_SHARD[names[a]](bp)], outs[a].at[kk], send.at[3 * a + kk], recv.at[3 * a + kk], peer).wait_send())

    return _Rider([part[nm] for nm in names], [jax.ShapeDtypeStruct((N_CHIPS - 1,) + _SHARD_SHAPE[nm], part[nm].dtype) for nm in names],
                  3 * n, start, finish)


def _sibling_exchange(mine0, mine1):
    names = list(mine0)
    n = len(names)

    def body(*refs):
        l0, l1, outs = refs[:n], refs[n:2 * n], refs[2 * n:3 * n]
        send, recv = refs[3 * n:]
        x, y, c = _pos()
        for c0 in (0, 1):
            @pl.when(c == c0)
            def _():
                srcs = l0 if c0 == 0 else l1
                cps = [_rcopy(srcs[a], outs[a], send.at[a], recv.at[a], (x, y, 1 - c0)) for a in range(n)]
                for cp in cps:
                    cp.start()
                for cp in cps:
                    cp.wait()

    outs = pl.pallas_call(
        body, name="grad_sibling_exchange", in_specs=[ANY] * (2 * n), out_specs=[ANY] * n,
        out_shape=[jax.ShapeDtypeStruct(mine0[nm].shape, mine0[nm].dtype) for nm in names],
        scratch_shapes=[_dma_sems(n), _dma_sems(n)],
    )(*[mine0[nm] for nm in names], *[mine1[nm] for nm in names])
    return dict(zip(names, outs))


def _small_rider(pack):
    flips = [(fx, fy, fc) for fx in (0, 1) for fy in (0, 1) for fc in (0, 1) if fx or fy or fc]

    def peers():
        x, y, c = _pos()
        return 4 * x + 2 * y + c, [(x ^ fx, y ^ fy, c ^ fc) for fx, fy, fc in flips]

    def start(ins, outs, send, recv):
        me, to = peers()
        for k, peer in enumerate(to):
            _rcopy(ins[0], outs[0].at[me], send.at[k], recv.at[k], peer).start()

    def finish(ins, outs, send, recv):
        me, to = peers()
        for k, (px, py, pc) in enumerate(to):
            slot = outs[0].at[4 * px + 2 * py + pc]
            _rcopy(slot, slot, send.at[k], recv.at[k], (px, py, pc)).wait_recv()
        for k, peer in enumerate(to):
            _rcopy(ins[0], outs[0].at[me], send.at[k], recv.at[k], peer).wait_send()

    return _Rider([pack], [jax.ShapeDtypeStruct((8,) + pack.shape, pack.dtype)], len(flips), start, finish)


def _sum_devices(gathered, pack, me):
    n, R, C = gathered.shape

    def body(me_ref, r_ref, own_ref, o_ref):
        acc = jnp.where(me_ref[0] == 0, own_ref[...], r_ref[0])
        for s in range(1, n):
            acc = acc + jnp.where(me_ref[0] == s, own_ref[...], r_ref[s])
        o_ref[...] = acc

    return pl.pallas_call(
        body, name="sum_devices_small",
        grid_spec=pltpu.PrefetchScalarGridSpec(num_scalar_prefetch=1, grid=(1,),
                                               in_specs=[pl.BlockSpec((n, R, C), lambda i, m: (0, 0, 0)), pl.BlockSpec((R, C), lambda i, m: (0, 0))],
                                               out_specs=pl.BlockSpec((R, C), lambda i, m: (0, 0))),
        out_shape=jax.ShapeDtypeStruct((R, C), F32), compiler_params=_cp("arbitrary"),
    )(_scalar(me), gathered, pack)


_WEIGHTS = ["norm_mix_g", "w_in", "sgu_ln_g", "sgu_ln_b", "w_spatial", "b_spatial", "conv_w", "w_branch", "w_out", "norm_xa_g",
            "mem_norm_g", "w_q_xa", "w_k_xa", "w_v_xa", "w_o_xa", "norm_ffn_g", "w_gate_ffn", "w_up_ffn", "w_down_ffn", "final_g"]
_REPLICATED = ["norm_mix_g", "sgu_ln_g", "sgu_ln_b", "w_spatial", "b_spatial", "norm_xa_g", "mem_norm_g", "norm_ffn_g", "final_g"]
_SQUARE = ["w_out", "w_q_xa", "w_o_xa", "w_k_xa", "w_v_xa"]
_BIG = ["w_in", "w_br", "w_sq", "w_gu", "w_dn"]


def _pack(arrs):
    return jnp.concatenate([a.reshape(-1) for a in arrs]).reshape(-1, 128)


def _step(a):
    w = {n: a[n] for n in _WEIGHTS}
    x, mem, target = a["x"][0], a["mem"][0], a["loss_target"][0]
    xi, yi, ci = _pos()
    bi = 2 * xi + yi
    groups = list(_FULL_SHAPE)

    tr = lambda t: jnp.swapaxes(t, 1, 2)
    local = {"w_in": w["w_in"], "w_br": w["w_branch"], "w_sq": jnp.stack([w[n] for n in _SQUARE], axis=1),
             "w_gu": jnp.stack([tr(w["w_gate_ffn"]), tr(w["w_up_ffn"])], axis=1), "w_dn": w["w_down_ffn"],
             "conv_w": jnp.pad(w["conv_w"], ((0, 0), (0, CONV_ROWS - 3), (0, 0)))}
    placed = [dict(), dict()]
    for n in groups:
        placed[0][n], placed[1][n] = _place(n, local[n], bi)

    def gather(l, names):
        return _gather_rider(l, {n: placed[l][n] for n in names})

    def start_of(l, w_in):
        return {"w_in": w_in, **{n: w[n][l] for n in _REPLICATED if n != "final_g"}}

    w_in0, = _run_rider(gather(0, ["w_in"]), name="gather_first")
    h, sv0, (w_in1,) = _layer_fwd(x, mem, start_of(0, w_in0), {"mm_in": gather(0, _GATHER_LATE), "sb": gather(0, _GATHER_LAST),
                                                                "mm_gu": gather(1, ["w_in"])})
    h, sv1, _ = _layer_fwd(h, mem, start_of(1, w_in1), {"mm_in": gather(1, _GATHER_LATE), "sb": gather(1, _GATHER_LAST), "mm_gu": None})
    loss, dx, d_final = _loss_head(h, w["final_g"], target)
    loss = lax.psum(loss, ("x", "y", "c"))

    pipe = _GradPipe(ci, bi)
    dx, g1 = _layer_bwd(dx, mem, sv1, 1, pipe)
    dx, g0 = _layer_bwd(dx, mem, sv0, 0, pipe)
    grads = [g0, g1]
    small = {n: jnp.stack([g[n] for g in grads]) for n in _REPLICATED if n != "final_g"}
    small["final_g"] = d_final
    conv_g = jnp.stack([g["conv_w"] for g in grads])
    small_pack = _pack([small[n] for n in _REPLICATED] + [conv_g])
    gathered, = pipe.drain(_small_rider(small_pack))
    mine = [pipe.reduced(0), pipe.reduced(1)]
    theirs = _sibling_exchange(mine[0], mine[1])

    out = {}

    def adam_layers(name, group, pick=None, view=lambda t: t):
        sel = (lambda t: t[group]) if pick is None else (lambda t: t[group][pick])
        res = _adam_layers([sel(mine[0]), sel(mine[1])], sel(theirs), ci, view(w[name]), view(a["m_" + name]), view(a["v_" + name]), name="adam_" + name)
        out[name] = tuple(view(r) for r in res)

    adam_layers("w_in", "w_in")
    adam_layers("w_branch", "w_br")
    for t, n in enumerate(_SQUARE):
        adam_layers(n, "w_sq", t)
    adam_layers("w_gate_ffn", "w_gu", 0, tr)
    adam_layers("w_up_ffn", "w_gu", 1, tr)
    adam_layers("w_down_ffn", "w_dn")

    def adam(name, g):
        out[name] = _reduce_adam([g], w[name], a["m_" + name], a["v_" + name], name="adam_" + name)

    n_rep = sum(w[n].size for n in _REPLICATED) // 128
    summed = _sum_devices(gathered, small_pack, 4 * xi + 2 * yi + ci)
    res = _reduce_adam([summed[:n_rep]], _pack([w[n] for n in _REPLICATED]), _pack([a["m_" + n] for n in _REPLICATED]),
                       _pack([a["v_" + n] for n in _REPLICATED]), name="adam_replicated")
    off = 0
    for n in _REPLICATED:
        out[n] = tuple(r.reshape(-1)[off:off + w[n].size].reshape(w[n].shape) for r in res)
        off += w[n].size
    conv_full = summed[n_rep:].reshape(conv_g.shape)
    adam("conv_w", lax.dynamic_slice_in_dim(conv_full, (2 * xi + yi) * 128, 128, axis=2))

    return (loss, dx[None], *[out[n][k] for k in range(4) for n in _WEIGHTS])


def kernel(x, mem, norm_mix_g, w_in, sgu_ln_g, sgu_ln_b, w_spatial, b_spatial, conv_w, w_branch, w_out, norm_xa_g, mem_norm_g, w_q_xa, w_k_xa, w_v_xa, w_o_xa, norm_ffn_g, w_gate_ffn, w_up_ffn, w_down_ffn, final_g, loss_target, m_norm_mix_g, m_w_in, m_sgu_ln_g, m_sgu_ln_b, m_w_spatial, m_b_spatial, m_conv_w, m_w_branch, m_w_out, m_norm_xa_g, m_mem_norm_g, m_w_q_xa, m_w_k_xa, m_w_v_xa, m_w_o_xa, m_norm_ffn_g, m_w_gate_ffn, m_w_up_ffn, m_w_down_ffn, m_final_g, v_norm_mix_g, v_w_in, v_sgu_ln_g, v_sgu_ln_b, v_w_spatial, v_b_spatial, v_conv_w, v_w_branch, v_w_out, v_norm_xa_g, v_mem_norm_g, v_w_q_xa, v_w_k_xa, v_w_v_xa, v_w_o_xa, v_norm_ffn_g, v_w_gate_ffn, v_w_up_ffn, v_w_down_ffn, v_final_g):
    return _step(dict(locals()))
```

```python
import functools
import math

import jax
import jax.numpy as jnp
from jax import lax
from jax.experimental import pallas as pl
from jax.experimental.pallas import tpu as pltpu

F32, BF16 = jnp.float32, jnp.bfloat16
MESH = pl.DeviceIdType.MESH
ANY = pl.BlockSpec(memory_space=pl.ANY)

D_MODEL = 1024
DEPTH = 2
BW = 512
SB_HEADS, SB_DH = 8, 64
SGU_LEN, SGU_GROUPS, SGU_GD, SGU_CHUNK = 128, 4, 128, 64
XA_HEADS, XA_DH = 4, 256
FFN_SH = 704
N_CHIPS = 4
IN_COLS = 7168
C_Z, C_CB, C_GATES = 1536, 2560, 4096

ADAM_LR, ADAM_B1, ADAM_B2, ADAM_EPS, ADAM_WD, ADAM_STEP = 0.001, 0.9, 0.999, 1e-08, 0.01, 10

VMEM_LIMIT_V7X = 56 * 1024 * 1024

NN = (((1,), (0,)), ((), ()))
NT = (((1,), (1,)), ((), ()))
TN = (((0,), (0,)), ((), ()))


def _cp(*sem):
    return pltpu.CompilerParams(dimension_semantics=sem, vmem_limit_bytes=VMEM_LIMIT_V7X)


def _tile(n, pref):
    for t in pref:
        if n % t == 0:
            return t
    return n


def _rows(r, row_bytes, block_bytes=1 << 20):
    fits = [t for t in range(8, r + 1, 8) if r % t == 0 and t * row_bytes <= block_bytes]
    return max(fits) if fits else r


class _Rider:
    def __init__(self, ins, outs, n_sems, start, finish, alias=None, middle=None):
        self.ins, self.outs, self.n_sems, self.alias = list(ins), list(outs), n_sems, alias or {}
        self.start, self.middle, self.finish = start, middle, finish


class _Sems:
    def __init__(self, ref, first):
        self.ref, self.first = ref, first

    @property
    def at(self):
        return self

    def __getitem__(self, k):
        return self.ref.at[self.first + k]


def _join(riders):
    riders = [r for r in riders if r is not None]
    if not riders:
        return None
    spans, i0, o0, s0 = [], 0, 0, 0
    for r in riders:
        spans.append((r, i0, o0, s0))
        i0, o0, s0 = i0 + len(r.ins), o0 + len(r.outs), s0 + r.n_sems

    def phase(which):
        def run(ins, outs, send, recv):
            for r, i, o, s in spans:
                fn = getattr(r, which)
                if fn is not None:
                    fn(ins[i:i + len(r.ins)], outs[o:o + len(r.outs)], _Sems(send, s), _Sems(recv, s))
        return run

    joined = _Rider([a for r in riders for a in r.ins], [a for r in riders for a in r.outs], s0, phase("start"), phase("finish"),
                    alias={o + k: i + v for r, i, o, s in spans for k, v in r.alias.items()}, middle=phase("middle"))
    joined.split = lambda landed: [list(landed[o:o + len(r.outs)]) for r, i, o, s in spans]
    return joined


def _call_with_rider(rider, body, *, name, grid, in_specs, args, out_specs, out_shape, scratch_shapes, semantics):
    if rider is None:
        return pl.pallas_call(body, name=name, grid=grid, in_specs=in_specs, out_specs=out_specs, out_shape=out_shape,
                              scratch_shapes=scratch_shapes, compiler_params=_cp(*semantics))(*args)
    n_in, n_out, r_in, r_out = len(args), len(out_shape), len(rider.ins), len(rider.outs)

    def riding(*refs):
        ins, rins = refs[:n_in], refs[n_in:n_in + r_in]
        outs, routs = refs[n_in + r_in:n_in + r_in + n_out], refs[n_in + r_in + n_out:n_in + r_in + n_out + r_out]
        rest = refs[n_in + r_in + n_out + r_out:]
        scratch, send, recv = rest[:-2], rest[-2], rest[-1]
        step = pl.program_id(0)
        for ax in range(1, len(grid)):
            step = step * grid[ax] + pl.program_id(ax)
        n_steps = math.prod(grid)

        @pl.when(step == 0)
        def _():
            rider.start(rins, routs, send, recv)

        body(*ins, *outs, *scratch)

        if rider.middle is not None:
            @pl.when(step == (3 * n_steps) // 5)
            def _():
                rider.middle(rins, routs, send, recv)

        @pl.when(step == n_steps - 1)
        def _():
            rider.finish(rins, routs, send, recv)

    return pl.pallas_call(
        riding, name=name, grid=grid, in_specs=list(in_specs) + [ANY] * r_in, out_specs=list(out_specs) + [ANY] * r_out,
        out_shape=list(out_shape) + rider.outs, scratch_shapes=list(scratch_shapes) + [_dma_sems(rider.n_sems), _dma_sems(rider.n_sems)],
        input_output_aliases={n_in + i: n_out + o for o, i in rider.alias.items()},
        compiler_params=_cp(*["arbitrary"] * len(grid)),
    )(*args, *rider.ins)


def _run_rider(rider, *, name):
    def nothing(*refs):
        pass

    return _call_with_rider(rider, nothing, name=name, grid=(1,), in_specs=[], args=[], out_specs=[], out_shape=[], scratch_shapes=[],
                            semantics=("arbitrary",))


def _mm(a, b, *, mode, name, out_dtype=F32, res=None, rms_bwd=None, a_kind="2d", b_kind="2d", tm=None, tn=None, tk=None, rider=None):
    a2, b2 = a.shape[-2:], b.shape[-2:]
    if mode == "nn":
        (M, K), N = a2, b2[1]
    elif mode == "nt":
        (M, K), N = a2, b2[0]
    else:
        (K, M), N = a2, b2[1]
    kchunk = a_kind == "kchunk" or b_kind == "kchunk"
    batch = a_kind == "batch" or b_kind == "batch"
    G = (a.shape[0] if a_kind == "batch" else b.shape[0]) if batch else 1
    tm = tm or _tile(M, (1024, 512, 256, 128))
    tn = tn or _tile(N, (1024, 512, 256, 128))
    if kchunk:
        tk, nk = K, (a.shape[0] if a_kind == "kchunk" else b.shape[0])
    else:
        tk = tk or _tile(K, (1024, 512, 256, 128))
        nk = K // tk

    def spec(kind, blk, idx):
        if kind == "2d":
            return pl.BlockSpec(blk, lambda g, i, j, k: idx(g, i, j, k))
        if kind == "batch":
            return pl.BlockSpec((None,) + blk, lambda g, i, j, k: (g,) + idx(g, i, j, k))
        return pl.BlockSpec((None,) + blk, lambda g, i, j, k: (k,) + idx(g, i, j, 0))

    if mode == "nn":
        a_spec = spec(a_kind, (tm, tk), lambda g, i, j, k: (i, k))
        b_spec = spec(b_kind, (tk, tn), lambda g, i, j, k: (k, j))
    elif mode == "nt":
        a_spec = spec(a_kind, (tm, tk), lambda g, i, j, k: (i, k))
        b_spec = spec(b_kind, (tn, tk), lambda g, i, j, k: (j, k))
    else:
        a_spec = spec(a_kind, (tk, tm), lambda g, i, j, k: (k, i))
        b_spec = spec(b_kind, (tk, tn), lambda g, i, j, k: (k, j))
    o_kind = "batch" if batch else "2d"
    o_spec = spec(o_kind, (tm, tn), lambda g, i, j, k: (i, j))
    o_shape = ((G,) if batch else ()) + (M, N)
    dn = {"nn": NN, "nt": NT, "tn": TN}[mode]
    has_res, has_rms = res is not None, rms_bwd is not None
    assert not (has_res and has_rms) and (not has_rms or (tn == N and G == 1))

    def body(*refs):
        if has_res:
            a_ref, b_ref, r_ref, o_ref = refs[:4]
        elif has_rms:
            a_ref, b_ref, x_ref, g_ref, dres_ref, o_ref, dg_ref = refs[:7]
        else:
            a_ref, b_ref, o_ref = refs[:3]
        p = lax.dot_general(a_ref[...].astype(BF16), b_ref[...].astype(BF16), dn, preferred_element_type=F32)
        first_rows = pl.program_id(1) == 0

        def finish(r):
            if has_rms:
                xv = x_ref[...]
                rs = lax.rsqrt(jnp.mean(xv * xv, axis=-1, keepdims=True) + 1e-6)
                u = r * g_ref[...]
                s = jnp.sum(u * xv, axis=-1, keepdims=True)
                o_ref[...] = dres_ref[...] + rs * u - xv * ((rs * rs * rs) * (s * (1.0 / N)))
                part = jnp.sum(r * (xv * rs), axis=0, keepdims=True)

                @pl.when(first_rows)
                def _():
                    dg_ref[...] = part

                @pl.when(jnp.logical_not(first_rows))
                def _():
                    dg_ref[...] += part
                return
            if has_res:
                r = r + r_ref[...]
            o_ref[...] = r.astype(out_dtype)

        if nk == 1:
            finish(p)
        else:
            acc = refs[-1]
            k = pl.program_id(3)

            @pl.when(k == 0)
            def _():
                acc[...] = p

            @pl.when(k > 0)
            def _():
                acc[...] += p

            @pl.when(k == nk - 1)
            def _():
                finish(acc[...])

    in_specs, args = [a_spec, b_spec], [a, b]
    tile = spec("2d", (tm, tn), lambda g, i, j, k: (i, j))
    out_specs, out_shape = [o_spec], [jax.ShapeDtypeStruct(o_shape, out_dtype)]
    if has_res:
        in_specs.append(tile)
        args.append(res)
    if has_rms:
        x, gain, dres = rms_bwd
        vec = pl.BlockSpec((1, tn), lambda g, i, j, k: (0, 0))
        in_specs += [tile, vec, tile]
        args += [x, gain.reshape(1, N), dres]
        out_specs.append(vec)
        out_shape.append(jax.ShapeDtypeStruct((1, N), F32))
    outs = _call_with_rider(
        rider, body, name=name, grid=(G, M // tm, N // tn, nk), in_specs=in_specs, args=args, out_specs=out_specs,
        out_shape=out_shape, scratch_shapes=[pltpu.VMEM((tm, tn), F32)] if nk > 1 else [],
        semantics=("parallel", "arbitrary" if has_rms else "parallel", "parallel", "arbitrary"))
    main = (outs[0], outs[1].reshape(N)) if has_rms else outs[0]
    return main if rider is None else (main, outs[len(out_shape):])


def _rms_fwd(x, g, *, name):
    S, Dm = x.shape
    tm = _tile(S, (512, 256))

    def body(x_ref, g_ref, o_ref):
        xv = x_ref[...]
        r = lax.rsqrt(jnp.mean(xv * xv, axis=-1, keepdims=True) + 1e-6)
        o_ref[...] = (xv * r * g_ref[...]).astype(BF16)

    return pl.pallas_call(
        body, name=name, grid=(S // tm,),
        in_specs=[pl.BlockSpec((tm, Dm), lambda i: (i, 0)), pl.BlockSpec((1, Dm), lambda i: (0, 0))],
        out_specs=pl.BlockSpec((tm, Dm), lambda i: (i, 0)), out_shape=jax.ShapeDtypeStruct((S, Dm), BF16),
        compiler_params=_cp("parallel"),
    )(x, g.reshape(1, Dm))


def _rms_bwd(x, g, dh, dres, *, name):
    S, Dm = x.shape
    tm = _tile(S, (512, 256))

    def body(x_ref, g_ref, dh_ref, dr_ref, dx_ref, dg_ref):
        xv, dhv = x_ref[...], dh_ref[...].astype(F32)
        r = lax.rsqrt(jnp.mean(xv * xv, axis=-1, keepdims=True) + 1e-6)
        u = dhv * g_ref[...]
        s = jnp.sum(u * xv, axis=-1, keepdims=True)
        dx_ref[...] = dr_ref[...] + r * u - xv * ((r * r * r) * (s * (1.0 / Dm)))
        part = jnp.sum(dhv * (xv * r), axis=0, keepdims=True)

        @pl.when(pl.program_id(0) == 0)
        def _():
            dg_ref[...] = part

        @pl.when(pl.program_id(0) > 0)
        def _():
            dg_ref[...] += part

    row = pl.BlockSpec((tm, Dm), lambda i: (i, 0))
    vec = pl.BlockSpec((1, Dm), lambda i: (0, 0))
    dx, dg = pl.pallas_call(
        body, name=name, grid=(S // tm,), in_specs=[row, vec, row, row], out_specs=[row, vec],
        out_shape=[jax.ShapeDtypeStruct((S, Dm), F32), jax.ShapeDtypeStruct((1, Dm), F32)],
        compiler_params=_cp("arbitrary"),
    )(x, g.reshape(1, Dm), dh, dres)
    return dx, dg.reshape(Dm)


def _loss_head(x, g, target):
    S, Dm = x.shape
    tm = _tile(S, (512, 256))

    def body(x_ref, g_ref, t_ref, dx_ref, dg_ref, loss_ref):
        xv, gv = x_ref[...], g_ref[...]
        r = lax.rsqrt(jnp.mean(xv * xv, axis=-1, keepdims=True) + 1e-6)
        xn = xv * r
        err = xn * gv - t_ref[...]
        lpart = 0.5 * jnp.sum(jnp.mean(err * err, axis=-1, keepdims=True), axis=0, keepdims=True)
        dy = err * (1.0 / Dm)
        u = dy * gv
        s = jnp.sum(u * xv, axis=-1, keepdims=True)
        dx_ref[...] = r * u - xv * ((r * r * r) * (s * (1.0 / Dm)))
        part = jnp.sum(dy * xn, axis=0, keepdims=True)
        lslab = jnp.broadcast_to(lpart, (8, 128))

        @pl.when(pl.program_id(0) == 0)
        def _():
            dg_ref[...] = part
            loss_ref[...] = lslab

        @pl.when(pl.program_id(0) > 0)
        def _():
            dg_ref[...] += part
            loss_ref[...] += lslab

    row = pl.BlockSpec((tm, Dm), lambda i: (i, 0))
    vec = pl.BlockSpec((1, Dm), lambda i: (0, 0))
    dx, dg, loss = pl.pallas_call(
        body, name="loss_head", grid=(S // tm,), in_specs=[row, vec, row],
        out_specs=[row, vec, pl.BlockSpec((8, 128), lambda i: (0, 0))],
        out_shape=[jax.ShapeDtypeStruct((S, Dm), F32), jax.ShapeDtypeStruct((1, Dm), F32), jax.ShapeDtypeStruct((8, 128), F32)],
        compiler_params=_cp("arbitrary"),
    )(x, g.reshape(1, Dm), target)
    return loss[0, 0], dx, dg.reshape(Dm)


SB_TQ, SB_TK = 256, 256
SB_EXP_FLOOR = -104.0


def _split2(v):
    hi = v.astype(BF16)
    return jnp.concatenate([hi, (v - hi.astype(F32)).astype(BF16)], axis=1)


def _tri2(cmp):
    j = lax.broadcasted_iota(jnp.int32, (2 * SB_TK, SB_TK), 0) % SB_TK
    s = lax.broadcasted_iota(jnp.int32, (2 * SB_TK, SB_TK), 1)
    return cmp(j, s).astype(BF16)


def _sb_scores(qv, kb, k0, q0, tq):
    rows = qv.shape[0]
    z = lax.dot_general(qv, kb, NT, preferred_element_type=F32) * (SB_DH ** -0.5)
    t_pos = q0 + lax.broadcasted_iota(jnp.int32, (rows, SB_TK), 0) % tq
    s_pos = k0 + lax.broadcasted_iota(jnp.int32, (rows, SB_TK), 1)
    valid = s_pos < t_pos
    ls = jnp.minimum(z, 0.0) - jnp.log(1.0 + jnp.exp(-jnp.abs(z)))
    l1m = jnp.where(valid, ls - z, 0.0)
    return z, valid, ls, l1m


SB_PAIRS = SB_HEADS // 2
_Q_BLK, _K_BLK, _V_BLK = 0, SB_PAIRS, 2 * SB_PAIRS


def _wide(x):
    return x if SB_TK == 128 else jnp.concatenate([x] * (SB_TK // 128), axis=1)


def _lanes_of(h, shape):
    lane = lax.broadcasted_iota(jnp.int32, shape, len(shape) - 1)
    return (lane < SB_DH) if h == 0 else (lane >= SB_DH)


def _sb2_fwd(p, rider=None):
    S = p.shape[0]
    tq = min(SB_TQ, S)
    kb_per_q = tq // SB_TK

    def body(q_ref, k_ref, v_ref, o_ref, tot_ref, cnt_ref, qm, acc, c):
        i = pl.program_id(1)
        q0 = i * tq
        later = _tri2(lambda j, s: j > s)
        q2 = q_ref[...]
        for h in range(2):
            qm[h * tq:(h + 1) * tq, :] = jnp.where(_lanes_of(h, q2.shape), q2, 0.0).astype(BF16)
        acc[...] = jnp.zeros_like(acc)
        c[...] = jnp.zeros_like(c)
        nkb = (i + 1) * kb_per_q

        def more(st):
            n, highest = st
            return (n < nkb) & (highest > SB_EXP_FLOOR)

        def step(st):
            n, _ = st
            k0 = pl.multiple_of((nkb - 1 - n) * SB_TK, SB_TK)
            kb, vb = k_ref[pl.ds(k0, SB_TK), :].astype(BF16), v_ref[pl.ds(k0, SB_TK), :].astype(BF16)
            c_old = c[...]
            z, valid, ls, l1m = _sb_scores(qm[...], kb, k0, q0, tq)
            c_new = c_old + jnp.sum(l1m, axis=1, keepdims=True)
            after = jnp.dot(_split2(l1m), later, preferred_element_type=F32)
            a = jnp.where(valid, jnp.exp(ls + after + _wide(c_old)), 0.0)
            av = jnp.dot(a.astype(BF16), vb, preferred_element_type=F32)
            acc[...] += jnp.where(_lanes_of(0, (tq, 128)), av[:tq], av[tq:])
            c[...] = c_new
            return n + 1, jnp.max(c_new)

        n_done, _ = lax.while_loop(more, step, (jnp.int32(0), jnp.float32(0.0)))
        o_ref[...] = acc[...].astype(o_ref.dtype)
        for h in range(2):
            tot_ref[h] = c[h * tq:(h + 1) * tq, :]
        cnt_ref[...] = jnp.full(cnt_ref.shape, n_done.astype(F32))

    col = lambda first: pl.BlockSpec((S, 128), lambda g, i: (0, first + g))
    outs = _call_with_rider(
        rider, body, name="sb_fwd", grid=(SB_PAIRS, S // tq), args=[p, p, p],
        in_specs=[pl.BlockSpec((tq, 128), lambda g, i: (i, _Q_BLK + g)), col(_K_BLK), col(_V_BLK)],
        out_specs=[pl.BlockSpec((tq, 128), lambda g, i: (i, g)), pl.BlockSpec((2, tq, 128), lambda g, i: (g, i, 0)),
                   pl.BlockSpec((None, None, 8, 128), lambda g, i: (g, i, 0, 0))],
        out_shape=[jax.ShapeDtypeStruct((S, BW), BF16), jax.ShapeDtypeStruct((SB_HEADS, S, 128), F32),
                   jax.ShapeDtypeStruct((SB_PAIRS, S // tq, 8, 128), F32)],
        scratch_shapes=[pltpu.VMEM((2 * tq, 128), BF16), pltpu.VMEM((tq, 128), F32), pltpu.VMEM((2 * tq, 128), F32)],
        semantics=("parallel", "parallel"))
    return outs[:3], outs[3:]


def _sb2_bwd(p, dbr, tot, cnt, rider=None):
    S = p.shape[0]
    tq = min(SB_TQ, S)
    kb_per_q = tq // SB_TK
    scale = SB_DH ** -0.5

    def body(q_ref, k_ref, v_ref, do_ref, tot_ref, cnt_ref, dq_ref, dk_ref, dv_ref, qm, dom, tot, dq_acc, pre, gpre):
        i = pl.program_id(1)
        q0 = i * tq
        upto = _tri2(lambda j, s: j <= s)
        before = _tri2(lambda j, s: j < s)

        @pl.when(i == 0)
        def _():
            dk_ref[...] = jnp.zeros_like(dk_ref)
            dv_ref[...] = jnp.zeros_like(dv_ref)

        q2, do2 = q_ref[...], do_ref[...]
        for h in range(2):
            rows = slice(h * tq, (h + 1) * tq)
            qm[rows, :] = jnp.where(_lanes_of(h, q2.shape), q2, 0.0).astype(BF16)
            dom[rows, :] = jnp.where(_lanes_of(h, do2.shape), do2, 0.0).astype(BF16)
            tot[rows, :] = tot_ref[h]
        dq_acc[...] = jnp.zeros_like(dq_acc)
        pre[...] = jnp.zeros_like(pre)
        gpre[...] = jnp.zeros_like(gpre)

        n_done = jnp.max(cnt_ref[...]).astype(jnp.int32)
        first = (i + 1) * kb_per_q - n_done

        def step(n, carry):
            k0 = pl.multiple_of((first + n) * SB_TK, SB_TK)
            kb, vb = k_ref[pl.ds(k0, SB_TK), :].astype(BF16), v_ref[pl.ds(k0, SB_TK), :].astype(BF16)
            pre_o, gpre_o = pre[...], gpre[...]
            z, valid, ls, l1m = _sb_scores(qm[...], kb, k0, q0, tq)
            incl = jnp.dot(_split2(l1m), upto, preferred_element_type=F32)
            rest = _wide(tot[...] - pre_o) - incl
            a = jnp.where(valid, jnp.exp(ls + rest), 0.0)
            da = lax.dot_general(dom[...], vb, NT, preferred_element_type=F32)
            g = a * da
            gbefore = jnp.dot(_split2(g), before, preferred_element_type=F32) + _wide(gpre_o)
            dz = jnp.where(valid, g * jnp.exp(ls - z) - jnp.exp(ls) * gbefore, 0.0) * scale
            dzb = dz.astype(BF16)
            dq_p = jnp.dot(dzb, kb, preferred_element_type=F32)
            dq_acc[...] += jnp.where(_lanes_of(0, (tq, 128)), dq_p[:tq], dq_p[tq:])
            dk_ref[pl.ds(k0, SB_TK), :] += lax.dot_general(dzb, qm[...], TN, preferred_element_type=F32)
            dv_ref[pl.ds(k0, SB_TK), :] += lax.dot_general(a.astype(BF16), dom[...], TN, preferred_element_type=F32)
            pre[...] = pre_o + jnp.sum(l1m, axis=1, keepdims=True)
            gpre[...] = gpre_o + jnp.sum(g, axis=1, keepdims=True)
            return carry

        lax.fori_loop(0, n_done, step, 0)
        dq_ref[...] = dq_acc[...].astype(dq_ref.dtype)

    col = lambda first: pl.BlockSpec((S, 128), lambda g, i: (0, first + g))
    tile = pl.BlockSpec((tq, 128), lambda g, i: (i, g))
    whole = pl.BlockSpec((S, 128), lambda g, i: (0, g))
    outs = _call_with_rider(
        rider, body, name="sb_bwd", grid=(SB_PAIRS, S // tq), args=[p, p, p, dbr, tot, cnt],
        in_specs=[pl.BlockSpec((tq, 128), lambda g, i: (i, _Q_BLK + g)), col(_K_BLK), col(_V_BLK),
                  pl.BlockSpec((None, tq, 128), lambda g, i: (0, i, g)), pl.BlockSpec((2, tq, 128), lambda g, i: (g, i, 0)),
                  pl.BlockSpec((None, None, 8, 128), lambda g, i: (g, i, 0, 0))],
        out_specs=[tile, whole, whole],
        out_shape=[jax.ShapeDtypeStruct((S, BW), BF16), jax.ShapeDtypeStruct((S, BW), F32), jax.ShapeDtypeStruct((S, BW), F32)],
        scratch_shapes=[pltpu.VMEM((2 * tq, 128), BF16), pltpu.VMEM((2 * tq, 128), BF16), pltpu.VMEM((2 * tq, 128), F32),
                        pltpu.VMEM((tq, 128), F32), pltpu.VMEM((2 * tq, 128), F32), pltpu.VMEM((2 * tq, 128), F32)],
        semantics=("parallel", "arbitrary"))
    return outs[:3], outs[3:]


_INV_SQRT2 = 0.7071067811865476
_INV_SQRT2PI = 0.3989422804014327


def _gelu(x):
    return 0.5 * x * (1.0 + lax.erf(x * _INV_SQRT2))


def _gelu_grad(x):
    return 0.5 * (1.0 + lax.erf(x * _INV_SQRT2)) + x * (_INV_SQRT2PI * jnp.exp(-0.5 * x * x))


def _sgu_mask():
    t = lax.broadcasted_iota(jnp.int32, (SGU_LEN, SGU_LEN), 0) // SGU_CHUNK
    s = lax.broadcasted_iota(jnp.int32, (SGU_LEN, SGU_LEN), 1) // SGU_CHUNK
    return t >= s


def _sgu_mask_t():
    t = lax.broadcasted_iota(jnp.int32, (SGU_LEN, SGU_LEN), 0) // SGU_CHUNK
    s = lax.broadcasted_iota(jnp.int32, (SGU_LEN, SGU_LEN), 1) // SGU_CHUNK
    return s >= t


def _sgu_norm(zv, g, b):
    vv = _gelu(zv)
    xc = vv - jnp.mean(vv, axis=-1, keepdims=True)
    rstd = lax.rsqrt(jnp.mean(xc * xc, axis=-1, keepdims=True) + 1e-5)
    xhat = xc * rstd
    return xhat, rstd, xhat * g + b


SGU_TM = 256


def _sgu_fwd(p, ln_g, ln_b, w_s, b_st):
    S = p.shape[0]
    tm = min(SGU_TM, S)

    def body(zu_ref, zv_ref, g_ref, b_ref, w_ref, bs_ref, o_ref):
        u = _gelu(zu_ref[...].astype(F32))
        _, _, vn = _sgu_norm(zv_ref[...].astype(F32), g_ref[...], b_ref[...])
        vnb = vn.astype(BF16)
        mask = _sgu_mask()
        for gi in range(SGU_GROUPS):
            wg = jnp.where(mask, w_ref[gi], 0.0).astype(BF16)
            cols = slice(gi * SGU_GD, (gi + 1) * SGU_GD)
            for ci in range(tm // SGU_LEN):
                rows = slice(ci * SGU_LEN, (ci + 1) * SGU_LEN)
                vm = jnp.dot(wg, vnb[rows, cols], preferred_element_type=F32) + bs_ref[:, gi:gi + 1]
                o_ref[rows, cols] = (u[rows, cols] * vm).astype(BF16)

    vec = pl.BlockSpec((1, BW), lambda i: (0, 0))
    return pl.pallas_call(
        body, name="sgu_fwd", grid=(S // tm,),
        in_specs=[pl.BlockSpec((tm, BW), lambda i: (i, C_Z // BW)), pl.BlockSpec((tm, BW), lambda i: (i, C_Z // BW + 1)), vec, vec,
                  pl.BlockSpec((SGU_GROUPS, SGU_LEN, SGU_LEN), lambda i: (0, 0, 0)), pl.BlockSpec((SGU_LEN, SGU_GROUPS), lambda i: (0, 0))],
        out_specs=pl.BlockSpec((tm, BW), lambda i: (i, 0)), out_shape=jax.ShapeDtypeStruct((S, BW), BF16),
        compiler_params=_cp("parallel"),
    )(p, p, ln_g.reshape(1, BW), ln_b.reshape(1, BW), w_s, b_st)


def _sgu_bwd(p, dyb, ln_g, ln_b, w_s, w_st, b_st):
    S = p.shape[0]
    tm = min(SGU_TM, S)

    def body(zu_ref, zv_ref, dy_ref, g_ref, b_ref, w_ref, wt_ref, bs_ref, dz_ref, dg_ref, db_ref, dw_ref, dbs_ref, dvn):
        first = pl.program_id(0) == 0

        @pl.when(first)
        def _():
            dg_ref[...] = jnp.zeros_like(dg_ref)
            db_ref[...] = jnp.zeros_like(db_ref)
            dw_ref[...] = jnp.zeros_like(dw_ref)
            dbs_ref[...] = jnp.zeros_like(dbs_ref)

        zu, zv, dy = zu_ref[...].astype(F32), zv_ref[...].astype(F32), dy_ref[...].astype(F32)
        u = _gelu(zu)
        xhat, rstd, vn = _sgu_norm(zv, g_ref[...], b_ref[...])
        vnb = vn.astype(BF16)
        mask = _sgu_mask()
        mask_t = _sgu_mask_t()
        for gi in range(SGU_GROUPS):
            wg = jnp.where(mask, w_ref[gi], 0.0).astype(BF16)
            wgt = jnp.where(mask_t, wt_ref[gi], 0.0).astype(BF16)
            cols = slice(gi * SGU_GD, (gi + 1) * SGU_GD)
            for ci in range(tm // SGU_LEN):
                rows = slice(ci * SGU_LEN, (ci + 1) * SGU_LEN)
                vm = jnp.dot(wg, vnb[rows, cols], preferred_element_type=F32) + bs_ref[:, gi:gi + 1]
                dyc = dy[rows, cols]
                dz_ref[rows, cols] = (dyc * vm * _gelu_grad(zu[rows, cols])).astype(BF16)
                dvm = dyc * u[rows, cols]
                dvmb = dvm.astype(BF16)
                dbs_ref[gi] += jnp.broadcast_to(jnp.sum(dvm, axis=1, keepdims=True), (SGU_LEN, SGU_GD))
                dw_ref[gi] += lax.dot_general(dvmb, vnb[rows, cols], NT, preferred_element_type=F32)
                dvn[rows, cols] = jnp.dot(wgt, dvmb, preferred_element_type=F32)
        dvnv = dvn[...]
        dg_ref[...] += jnp.sum(dvnv * xhat, axis=0, keepdims=True)
        db_ref[...] += jnp.sum(dvnv, axis=0, keepdims=True)
        dxh = dvnv * g_ref[...]
        dvv = rstd * (dxh - jnp.mean(dxh, axis=-1, keepdims=True) - xhat * jnp.mean(dxh * xhat, axis=-1, keepdims=True))
        dz_ref[:, BW:] = (dvv * _gelu_grad(zv)).astype(BF16)

        @pl.when(pl.program_id(0) == n_steps - 1)
        def _():
            for gi in range(SGU_GROUPS):
                dw_ref[gi] = jnp.where(mask, dw_ref[gi], 0.0)

    n_steps = S // tm
    vec = pl.BlockSpec((1, BW), lambda i: (0, 0))
    half = lambda c: pl.BlockSpec((tm, BW), lambda i: (i, c))
    wspec = pl.BlockSpec((SGU_GROUPS, SGU_LEN, SGU_LEN), lambda i: (0, 0, 0))
    dz, dg, db, dw, dbs = pl.pallas_call(
        body, name="sgu_bwd", grid=(n_steps,),
        in_specs=[half(C_Z // BW), half(C_Z // BW + 1), half(0), vec, vec, wspec, wspec,
                  pl.BlockSpec((SGU_LEN, SGU_GROUPS), lambda i: (0, 0))],
        out_specs=[pl.BlockSpec((tm, 2 * BW), lambda i: (i, 0)), vec, vec, wspec, wspec],
        out_shape=[jax.ShapeDtypeStruct((S, 2 * BW), BF16), jax.ShapeDtypeStruct((1, BW), F32), jax.ShapeDtypeStruct((1, BW), F32),
                   jax.ShapeDtypeStruct((SGU_GROUPS, SGU_LEN, SGU_LEN), F32), jax.ShapeDtypeStruct((SGU_GROUPS, SGU_LEN, SGU_GD), F32)],
        scratch_shapes=[pltpu.VMEM((tm, BW), F32)],
        compiler_params=_cp("arbitrary"),
    )(p, p, dyb, ln_g.reshape(1, BW), ln_b.reshape(1, BW), w_s, w_st, b_st)
    return dz, dg.reshape(BW), db.reshape(BW), dw, dbs[:, :, 0]


CONV_TC = 128


def _shift_down(y, n):
    rows = lax.broadcasted_iota(jnp.int32, y.shape, 0)
    return jnp.where(rows < n, 0.0, pltpu.roll(y, n, 0))


def _shift_up(y, n):
    rows = lax.broadcasted_iota(jnp.int32, y.shape, 0)
    return jnp.where(rows >= y.shape[0] - n, 0.0, pltpu.roll(y, y.shape[0] - n, 0))


def _conv_specs(S):
    col = lambda c0: pl.BlockSpec((S, CONV_TC), lambda j: (0, c0 // CONV_TC + j))
    return col(C_CB), col(C_CB + BW), col(C_CB + 2 * BW), pl.BlockSpec((3, CONV_TC), lambda j: (0, j)), pl.BlockSpec((S, CONV_TC), lambda j: (0, j))


def _conv_fwd(p, conv_w):
    S = p.shape[0]

    def body(cb_ref, cc_ref, cx_ref, w_ref, o_ref):
        y = cc_ref[...].astype(F32) * cx_ref[...].astype(F32)
        conv = w_ref[0:1, :] * _shift_down(y, 2) + w_ref[1:2, :] * _shift_down(y, 1) + w_ref[2:3, :] * y
        o_ref[...] = (cb_ref[...].astype(F32) * conv).astype(BF16)

    cb, cc, cx, wspec, out = _conv_specs(S)
    return pl.pallas_call(
        body, name="conv_fwd", grid=(BW // CONV_TC,), in_specs=[cb, cc, cx, wspec], out_specs=out,
        out_shape=jax.ShapeDtypeStruct((S, BW), BF16), compiler_params=_cp("parallel"),
    )(p, p, p, conv_w)


def _conv_bwd(p, conv_w, dyc):
    S = p.shape[0]

    def body(cb_ref, cc_ref, cx_ref, w_ref, dy_ref, db_ref, dc_ref, dx_ref, dw_ref):
        cc, cx, dy = cc_ref[...].astype(F32), cx_ref[...].astype(F32), dy_ref[...].astype(F32)
        y = cc * cx
        w0, w1, w2 = w_ref[0:1, :], w_ref[1:2, :], w_ref[2:3, :]
        y1, y2 = _shift_down(y, 1), _shift_down(y, 2)
        conv = w0 * y2 + w1 * y1 + w2 * y
        db_ref[...] = (dy * conv).astype(BF16)
        dconv = dy * cb_ref[...].astype(F32)
        dyy = w2 * dconv + w1 * _shift_up(dconv, 1) + w0 * _shift_up(dconv, 2)
        dc_ref[...] = (dyy * cx).astype(BF16)
        dx_ref[...] = (dyy * cc).astype(BF16)
        dw_ref[0:1, :] = jnp.sum(dconv * y2, axis=0, keepdims=True)
        dw_ref[1:2, :] = jnp.sum(dconv * y1, axis=0, keepdims=True)
        dw_ref[2:3, :] = jnp.sum(dconv * y, axis=0, keepdims=True)

    cb, cc, cx, wspec, out = _conv_specs(S)
    db, dc, dx, dw = pl.pallas_call(
        body, name="conv_bwd", grid=(BW // CONV_TC,), in_specs=[cb, cc, cx, wspec, out],
        out_specs=[out, out, out, wspec],
        out_shape=[jax.ShapeDtypeStruct((S, BW), BF16)] * 3 + [jax.ShapeDtypeStruct((3, BW), F32)],
        compiler_params=_cp("parallel"),
    )(p, p, p, conv_w, dyc)
    return db, dc, dx, dw


def _merge_specs(S, tm):
    gate = lambda n: pl.BlockSpec((tm, D_MODEL), lambda i: (i, C_GATES // D_MODEL + n))
    return [gate(0), gate(1), gate(2)], pl.BlockSpec((3, tm, D_MODEL), lambda i: (0, i, 0)), pl.BlockSpec((tm, D_MODEL), lambda i: (i, 0))


def _branch_merge(br, w_br, p):
    S = p.shape[0]
    tm = _tile(S, (512, 256))

    def body(br_ref, w_ref, g0, g1, g2, bd_ref, o_ref):
        acc = None
        for n, g_ref in enumerate((g0, g1, g2)):
            bdn = jnp.dot(br_ref[n], w_ref[n], preferred_element_type=F32)
            bd_ref[n] = bdn.astype(BF16)
            term = jax.nn.sigmoid(g_ref[...].astype(F32)) * bdn
            acc = term if acc is None else acc + term
        o_ref[...] = acc.astype(BF16)

    gates, bspec, row = _merge_specs(S, tm)
    return pl.pallas_call(
        body, name="mm_branch", grid=(S // tm,),
        in_specs=[pl.BlockSpec((3, tm, BW), lambda i: (0, i, 0)), pl.BlockSpec((3, BW, D_MODEL), lambda i: (0, 0, 0))] + gates,
        out_specs=[bspec, row], out_shape=[jax.ShapeDtypeStruct((3, S, D_MODEL), BF16), jax.ShapeDtypeStruct((S, D_MODEL), BF16)],
        compiler_params=_cp("parallel"),
    )(br, w_br, p, p, p)


def _merge_bwd(p, bd, dm):
    S = p.shape[0]
    tm = _tile(S, (256,))

    def body(g0, g1, g2, b_ref, dm_ref, db_ref, dg_ref):
        dmv = dm_ref[...]
        for n, g_ref in enumerate((g0, g1, g2)):
            sg = jax.nn.sigmoid(g_ref[...].astype(F32))
            db_ref[n] = (dmv * sg).astype(BF16)
            dg_ref[:, n * D_MODEL:(n + 1) * D_MODEL] = (dmv * b_ref[n].astype(F32) * (sg * (1.0 - sg))).astype(BF16)

    gates, bspec, row = _merge_specs(S, tm)
    return pl.pallas_call(
        body, name="merge_bwd", grid=(S // tm,), in_specs=gates + [bspec, row],
        out_specs=[bspec, pl.BlockSpec((tm, 3 * D_MODEL), lambda i: (i, 0))],
        out_shape=[jax.ShapeDtypeStruct((3, S, D_MODEL), BF16), jax.ShapeDtypeStruct((S, 3 * D_MODEL), BF16)],
        compiler_params=_cp("parallel"),
    )(p, p, p, bd, dm)


XA_TM = 512


def _xa_probs(qh, kh):
    s = lax.dot_general(qh, kh, NT, preferred_element_type=F32) * (XA_DH ** -0.5)
    e = jnp.exp(s - jnp.max(s, axis=-1, keepdims=True))
    return e / jnp.sum(e, axis=-1, keepdims=True)


def _xa_fwd(q, kv):
    S = q.shape[0]
    tm = min(XA_TM, S)
    M = kv.shape[1]

    def body(q_ref, kv_ref, o_ref):
        for h in range(XA_HEADS):
            cols = slice(h * XA_DH, (h + 1) * XA_DH)
            pr = _xa_probs(q_ref[:, cols], kv_ref[0, :, cols])
            o_ref[:, cols] = jnp.dot(pr.astype(BF16), kv_ref[1, :, cols], preferred_element_type=F32).astype(BF16)

    row = pl.BlockSpec((tm, D_MODEL), lambda i: (i, 0))
    return pl.pallas_call(
        body, name="xa_fwd", grid=(S // tm,), in_specs=[row, pl.BlockSpec((2, M, D_MODEL), lambda i: (0, 0, 0))], out_specs=row,
        out_shape=jax.ShapeDtypeStruct((S, D_MODEL), BF16), compiler_params=_cp("parallel"),
    )(q, kv)


def _xa_bwd(q, kv, do):
    S = q.shape[0]
    tm = min(XA_TM, S)
    M = kv.shape[1]

    def body(q_ref, kv_ref, do_ref, dq_ref, dkv_ref):
        @pl.when(pl.program_id(0) == 0)
        def _():
            dkv_ref[...] = jnp.zeros_like(dkv_ref)

        for h in range(XA_HEADS):
            cols = slice(h * XA_DH, (h + 1) * XA_DH)
            qh, kh, vh, doh = q_ref[:, cols], kv_ref[0, :, cols], kv_ref[1, :, cols], do_ref[:, cols]
            pr = _xa_probs(qh, kh)
            dkv_ref[1, :, cols] += lax.dot_general(pr.astype(BF16), doh, TN, preferred_element_type=F32)
            dp = lax.dot_general(doh, vh, NT, preferred_element_type=F32)
            ds = (pr * (dp - jnp.sum(dp * pr, axis=-1, keepdims=True)) * (XA_DH ** -0.5)).astype(BF16)
            dq_ref[:, cols] = jnp.dot(ds, kh, preferred_element_type=F32).astype(BF16)
            dkv_ref[0, :, cols] += lax.dot_general(ds, qh, TN, preferred_element_type=F32)

    row = pl.BlockSpec((tm, D_MODEL), lambda i: (i, 0))
    kvs = pl.BlockSpec((2, M, D_MODEL), lambda i: (0, 0, 0))
    return pl.pallas_call(
        body, name="xa_bwd", grid=(S // tm,), in_specs=[row, kvs, row], out_specs=[row, kvs],
        out_shape=[jax.ShapeDtypeStruct((S, D_MODEL), BF16), jax.ShapeDtypeStruct((2, M, D_MODEL), F32)],
        compiler_params=_cp("arbitrary"),
    )(q, kv, do)


def _ffn_up(h3, w_gu, rider=None):
    S = h3.shape[0]
    tm = _tile(S, (1024, 512, 256))

    def body(h_ref, w_ref, ab_ref, hh_ref):
        h = h_ref[...]
        a = lax.dot_general(h, w_ref[0], NT, preferred_element_type=F32)
        b = lax.dot_general(h, w_ref[1], NT, preferred_element_type=F32)
        ab_ref[0] = a.astype(BF16)
        ab_ref[1] = b.astype(BF16)
        hh_ref[...] = (a * jax.nn.sigmoid(a) * b).astype(BF16)

    pair = pl.BlockSpec((None, 2, tm, FFN_SH), lambda j, i: (j, 0, i, 0))
    outs = _call_with_rider(
        rider, body, name="mm_gu", grid=(N_CHIPS, S // tm), args=[h3, w_gu],
        in_specs=[pl.BlockSpec((tm, D_MODEL), lambda j, i: (i, 0)), pl.BlockSpec((None, 2, FFN_SH, D_MODEL), lambda j, i: (j, 0, 0, 0))],
        out_specs=[pair, pl.BlockSpec((None, tm, FFN_SH), lambda j, i: (j, i, 0))],
        out_shape=[jax.ShapeDtypeStruct((N_CHIPS, 2, S, FFN_SH), BF16), jax.ShapeDtypeStruct((N_CHIPS, S, FFN_SH), BF16)],
        scratch_shapes=[], semantics=("parallel", "parallel"))
    return outs[0], outs[1], outs[2:]


def _ffn_down_bwd(dx3, w_dn, ab):
    S = dx3.shape[0]
    tm = _tile(S, (1024, 512, 256))

    def body(dx_ref, w_ref, ab_ref, o_ref):
        d = lax.dot_general(dx_ref[...].astype(BF16), w_ref[...], NT, preferred_element_type=F32)
        a, b = ab_ref[0].astype(F32), ab_ref[1].astype(F32)
        sg = jax.nn.sigmoid(a)
        o_ref[0] = (d * b * (sg * (1.0 + a * (1.0 - sg)))).astype(BF16)
        o_ref[1] = (d * (a * sg)).astype(BF16)

    pair = pl.BlockSpec((None, 2, tm, FFN_SH), lambda j, i: (j, 0, i, 0))
    return pl.pallas_call(
        body, name="mm_down_dx", grid=(N_CHIPS, S // tm),
        in_specs=[pl.BlockSpec((tm, D_MODEL), lambda j, i: (i, 0)), pl.BlockSpec((None, FFN_SH, D_MODEL), lambda j, i: (j, 0, 0)), pair],
        out_specs=pair, out_shape=jax.ShapeDtypeStruct(ab.shape, BF16), compiler_params=_cp("parallel", "parallel"),
    )(dx3, w_dn, ab)


def _reduce_adam(parts, w, m, v, *, name):
    shape = w.shape
    C = shape[-1]
    R = math.prod(shape[:-1])
    tm = _rows(R, 4 * C)
    n = len(parts)
    c1, c2 = 1.0 - ADAM_B1 ** ADAM_STEP, 1.0 - ADAM_B2 ** ADAM_STEP

    def body(*refs):
        g = refs[0][...]
        for r in refs[1:n]:
            g = g + r[...]
        w_ref, m_ref, v_ref, go, do, mo, vo = refs[n:]
        mn = ADAM_B1 * m_ref[...] + (1.0 - ADAM_B1) * g
        vn = ADAM_B2 * v_ref[...] + (1.0 - ADAM_B2) * (g * g)
        go[...] = g
        do[...] = -ADAM_LR * ((mn / c1) / (jnp.sqrt(vn / c2) + ADAM_EPS) + ADAM_WD * w_ref[...])
        mo[...] = mn
        vo[...] = vn

    row = pl.BlockSpec((tm, C), lambda i: (i, 0))
    outs = pl.pallas_call(
        body, name=name, grid=(R // tm,), in_specs=[row] * (n + 3), out_specs=[row] * 4,
        out_shape=[jax.ShapeDtypeStruct((R, C), F32)] * 4, compiler_params=_cp("parallel"),
    )(*[a.reshape(R, C) for a in (*parts, w, m, v)])
    return tuple(o.reshape(shape) for o in outs)


_VIEW = {
    "w_in": ((1024, 7168), (1024, 1792), 256, lambda i, b: (i, b)),
    "w_br": ((1536, 1024), (1536, 256), 512, lambda i, b: (i, b)),
    "w_sq": ((5120, 1024), (1280, 1024), 256, lambda i, b: (4 * i + b, 0)),
    "w_gu": ((5632, 1024), (1408, 1024), 352, lambda i, b: (4 * b + i, 0)),
    "w_dn": ((2816, 1024), (704, 1024), 352, lambda i, b: (2 * b + i, 0)),
    "conv_w": ((8, 512), (8, 128), 8, lambda i, b: (0, b)),
}


def _scalar(v):
    return jnp.asarray(v, jnp.int32).reshape(1)


def _place(name, local, b):
    full2, sh2, tm, idx = _VIEW[name]
    C = sh2[1]
    dt = local.dtype if name == "conv_w" else BF16

    def body(b_ref, x_ref, o0_ref, o1_ref):
        o0_ref[...] = x_ref[0].astype(dt)
        o1_ref[...] = x_ref[1].astype(dt)

    place = pl.BlockSpec((tm, C), lambda i, bs: idx(i, bs[0]))
    outs = pl.pallas_call(
        body, name="place_" + name,
        grid_spec=pltpu.PrefetchScalarGridSpec(num_scalar_prefetch=1, grid=(sh2[0] // tm,),
                                               in_specs=[pl.BlockSpec((DEPTH, tm, C), lambda i, bs: (0, i, 0))], out_specs=[place, place]),
        out_shape=[jax.ShapeDtypeStruct(full2, dt)] * 2, compiler_params=_cp("arbitrary"),
    )(_scalar(b), local.reshape((DEPTH,) + sh2))
    return [o.reshape(_FULL_SHAPE[name]) for o in outs]


def _add_owner(name, g, land, own):
    shape = g.shape
    C = shape[-1]
    R = math.prod(shape[:-1])
    tm = _rows(R, 4 * C)

    def body(s_ref, g_ref, l_ref, o_ref):
        @pl.when(s_ref[0] != 0)
        def _():
            o_ref[...] = (g_ref[...].astype(F32) + l_ref[...].astype(F32)).astype(BF16)

        @pl.when(s_ref[0] == 0)
        def _():
            o_ref[...] = jnp.zeros_like(o_ref)

    pick = pl.BlockSpec((tm, C), lambda i, s: (jnp.where(s[0] != 0, i, 0), 0))
    return pl.pallas_call(
        body, name="presum_" + name,
        grid_spec=pltpu.PrefetchScalarGridSpec(num_scalar_prefetch=1, grid=(R // tm,), in_specs=[pick, pick],
                                               out_specs=pl.BlockSpec((tm, C), lambda i, s: (i, 0))),
        out_shape=jax.ShapeDtypeStruct((R, C), BF16), compiler_params=_cp("arbitrary"),
    )(_scalar(own), g.reshape(R, C), land.reshape(R, C)).reshape(shape)


def _sum_chips(name, slots, part, b, own):
    full2, sh2, tm, idx = _VIEW[name]
    C = sh2[1]

    def body(s_ref, slot_ref, own_ref, o_ref):
        @pl.when(s_ref[1] != 0)
        def _():
            o_ref[...] = ((slot_ref[0].astype(F32) + slot_ref[1].astype(F32)) + slot_ref[2].astype(F32)) + own_ref[...].astype(F32)

        @pl.when(s_ref[1] == 0)
        def _():
            o_ref[...] = jnp.zeros_like(o_ref)

    return pl.pallas_call(
        body, name="sum_chips_" + name,
        grid_spec=pltpu.PrefetchScalarGridSpec(
            num_scalar_prefetch=1, grid=(sh2[0] // tm,),
            in_specs=[pl.BlockSpec((3, tm, C), lambda i, s: (0, jnp.where(s[1] != 0, i, 0), 0)),
                      pl.BlockSpec((tm, C), lambda i, s: idx(jnp.where(s[1] != 0, i, 0), s[0]))],
            out_specs=pl.BlockSpec((tm, C), lambda i, s: (i, 0))),
        out_shape=jax.ShapeDtypeStruct(sh2, F32), compiler_params=_cp("arbitrary"),
    )(jnp.stack([jnp.asarray(b, jnp.int32), jnp.asarray(own, jnp.int32)]), slots.reshape((3,) + sh2),
      part.reshape(full2)).reshape(_SHARD_SHAPE[name])


def _adam_layers(mine, theirs, c, w, m, v, *, name):
    shape = w.shape
    C = shape[-1]
    R = math.prod(shape[1:-1])
    tm = _rows(R, 4 * C)
    c1, c2 = 1.0 - ADAM_B1 ** ADAM_STEP, 1.0 - ADAM_B2 ** ADAM_STEP

    def body(c_ref, m0_ref, m1_ref, t_ref, w_ref, m_ref, v_ref, go, do, mo, vo):
        layer = pl.program_id(0)
        g = jnp.where(layer == c_ref[0], jnp.where(layer == 0, m0_ref[...], m1_ref[...]), t_ref[...])
        mn = ADAM_B1 * m_ref[...] + (1.0 - ADAM_B1) * g
        vn = ADAM_B2 * v_ref[...] + (1.0 - ADAM_B2) * (g * g)
        go[...] = g
        do[...] = -ADAM_LR * ((mn / c1) / (jnp.sqrt(vn / c2) + ADAM_EPS) + ADAM_WD * w_ref[...])
        mo[...] = mn
        vo[...] = vn

    def own(layer):
        return pl.BlockSpec((tm, C), lambda l, i, cs: (jnp.where((l == layer) & (cs[0] == layer), i, 0), 0))

    recv = pl.BlockSpec((tm, C), lambda l, i, cs: (jnp.where(l == cs[0], 0, i), 0))
    row = pl.BlockSpec((None, tm, C), lambda l, i, cs: (l, i, 0))
    outs = pl.pallas_call(
        body, name=name,
        grid_spec=pltpu.PrefetchScalarGridSpec(num_scalar_prefetch=1, grid=(DEPTH, R // tm),
                                               in_specs=[own(0), own(1), recv, row, row, row], out_specs=[row] * 4),
        out_shape=[jax.ShapeDtypeStruct((DEPTH, R, C), F32)] * 4, compiler_params=_cp("arbitrary", "arbitrary"),
    )(_scalar(c), mine[0].reshape(R, C), mine[1].reshape(R, C), theirs.reshape(R, C), *[t.reshape(DEPTH, R, C) for t in (w, m, v)])
    return tuple(o.reshape(shape) for o in outs)


def _take_weights(wl, names, landed):
    for n, t in zip(names, landed):
        wl[n] = t[:3] if n == "conv_w" else t


_GATHER_LATE = ["w_br", "conv_w", "w_sq"]
_GATHER_LAST = ["w_gu", "w_dn"]


def _layer_fwd(x, mem, wl, ride=None):
    S = x.shape[0]
    sv = {"x": x}
    wl = dict(wl)
    h1 = _rms_fwd(x, wl["norm_mix_g"], name="rms_mix")
    if ride is None:
        ride = {"mm_in": None, "sb": None, "mm_gu": None}
        p = _mm(h1, wl["w_in"], mode="nn", out_dtype=BF16, name="mm_in")
    else:
        p, landed = _mm(h1, wl["w_in"], mode="nn", out_dtype=BF16, name="mm_in", rider=ride["mm_in"])
        _take_weights(wl, _GATHER_LATE, landed)
    (ya, tot, cnt), landed = _sb2_fwd(p, ride["sb"])
    _take_weights(wl, _GATHER_LAST, landed)
    b_st = wl["b_spatial"].T
    yb = _sgu_fwd(p, wl["sgu_ln_g"], wl["sgu_ln_b"], wl["w_spatial"], b_st)
    yc = _conv_fwd(p, wl["conv_w"])
    br = jnp.stack([ya, yb, yc])
    bd, merged = _branch_merge(br, wl["w_br"], p)
    x1 = _mm(merged, wl["w_sq"][0], mode="nn", res=x, name="mm_out")
    h2 = _rms_fwd(x1, wl["norm_xa_g"], name="rms_xa")
    qx = _mm(h2, wl["w_sq"][1], mode="nn", out_dtype=BF16, name="mm_q")
    mn = _rms_fwd(mem, wl["mem_norm_g"], name="rms_mem")
    kv = _mm(mn, wl["w_sq"][3:5], mode="nn", b_kind="batch", out_dtype=BF16, name="mm_kv")
    o = _xa_fwd(qx, kv)
    x2 = _mm(o, wl["w_sq"][2], mode="nn", res=x1, name="mm_o")
    h3 = _rms_fwd(x2, wl["norm_ffn_g"], name="rms_ffn")
    ab, hh, rode = _ffn_up(h3, wl["w_gu"], ride["mm_gu"])
    x3 = _mm(hh, wl["w_dn"], mode="nn", a_kind="kchunk", b_kind="kchunk", res=x2, name="mm_down")
    sv.update(h1=h1, p=p, tot=tot, cnt=cnt, br=br, bd=bd, merged=merged, x1=x1, h2=h2, qx=qx, mn=mn, kv=kv, o=o,
              x2=x2, h3=h3, ab=ab, hh=hh, b_st=b_st, wl=wl)
    return x3, sv, rode


class _GradPipe:
    def __init__(self, ci, bi):
        self.ci, self.bi, self.queue = ci, bi, []
        self.part, self.slots = [dict(), dict()], [dict(), dict()]

    def to_owner(self, layer, names, g):
        def arrived(land):
            own = (self.ci == layer).astype(jnp.int32)
            part = {n: _add_owner(n, g[n], t, own) for n, t in zip(names, land)}
            self.part[layer].update(part)
            self.queue.append((layer, names, part))

        return _presum_rider(layer, {n: g[n] for n in names}), arrived

    def exchange(self):
        if not self.queue:
            return None
        layer, names, part = self.queue.pop(0)
        return _shard_rider(layer, part), lambda slots: self.slots[layer].update(zip(names, slots))

    def drain(self, also):
        landed = None
        while self.queue or landed is None:
            job = self.exchange()
            both = _join([job[0] if job else None, also if landed is None else None])
            parts = both.split(_run_rider(both, name="grad_exchange_last"))
            if job:
                job[1](parts[0])
            if landed is None:
                landed = parts[-1]
        return landed

    def reduced(self, layer):
        own = (self.ci == layer).astype(jnp.int32)
        return {n: _sum_chips(n, self.slots[layer][n], self.part[layer][n], self.bi, own) for n in _BIG}


def _layer_bwd(dx3, mem, sv, layer=None, pipe=None):
    S = dx3.shape[0]
    p, wl = sv["p"], sv["wl"]
    g = {}

    def carrying(jobs, fn):
        jobs = [j for j in jobs if j]
        joined = _join([j[0] for j in jobs])
        out, landed = fn(joined)
        if joined is not None:
            for j, part in zip(jobs, joined.split(landed)):
                j[1](part)
        return out

    def mm(*args, job=None, jobs=(), **kw):
        return carrying([job, *jobs], lambda r: (_mm(*args, **kw), None) if r is None else _mm(*args, **kw, rider=r))

    to_owner = (lambda names: pipe.to_owner(layer, names, g)) if pipe else (lambda names: None)
    exchange = pipe.exchange if pipe else (lambda: None)

    g["w_dn"] = _mm(sv["hh"], dx3, mode="tn", a_kind="batch", out_dtype=BF16, name="mm_down_dw")
    dab = _ffn_down_bwd(dx3, wl["w_dn"], sv["ab"]).reshape(2 * N_CHIPS, S, FFN_SH)
    g["w_gu"] = mm(dab, sv["h3"], mode="tn", a_kind="batch", out_dtype=BF16, name="mm_gu_dw", job=exchange()).reshape(_FULL_SHAPE["w_gu"])
    dx2, g["norm_ffn_g"] = mm(dab, wl["w_gu"].reshape(2 * N_CHIPS, FFN_SH, D_MODEL), mode="nn", a_kind="kchunk", b_kind="kchunk", name="mm_gu_dx",
                               rms_bwd=(sv["x2"], wl["norm_ffn_g"], dx3), job=to_owner(["w_dn", "w_gu"]))
    do = _mm(dx2, wl["w_sq"][2], mode="nt", out_dtype=BF16, name="mm_o_dx")
    dw_o = _mm(sv["o"], dx2, mode="tn", out_dtype=BF16, name="mm_o_dw")
    dq, dkv = _xa_bwd(sv["qx"], sv["kv"], do)
    dw_q = _mm(sv["h2"], dq, mode="tn", out_dtype=BF16, name="mm_q_dw")
    dx1, g["norm_xa_g"] = _mm(dq, wl["w_sq"][1], mode="nt", name="mm_q_dx", rms_bwd=(sv["x1"], wl["norm_xa_g"], dx2))
    dw_kv = _mm(sv["mn"], dkv, mode="tn", b_kind="batch", out_dtype=BF16, name="mm_kv_dw")
    dmn = _mm(dkv, wl["w_sq"][3:5], mode="nt", a_kind="kchunk", b_kind="kchunk", name="mm_kv_dx")
    _, g["mem_norm_g"] = _rms_bwd(mem, wl["mem_norm_g"], dmn, jnp.zeros_like(mem), name="rms_mem_bwd")
    dm = _mm(dx1, wl["w_sq"][0], mode="nt", name="mm_out_dx")
    dw_out = _mm(sv["merged"], dx1, mode="tn", out_dtype=BF16, name="mm_out_dw")
    g["w_sq"] = jnp.concatenate([jnp.stack([dw_out, dw_q, dw_o]), dw_kv])
    dbd, dgates = _merge_bwd(p, sv["bd"], dm)
    dbr = mm(dbd, wl["w_br"], mode="nt", a_kind="batch", b_kind="batch", name="mm_branch_dx", job=to_owner(["w_sq"]))
    g["w_br"] = _mm(sv["br"], dbd, mode="tn", a_kind="batch", b_kind="batch", out_dtype=BF16, name="mm_branch_dw")
    dq, dk, dv = carrying([exchange(), to_owner(["w_br"])], lambda r: _sb2_bwd(p, dbr, sv["tot"], sv["cnt"], r))
    dz, g["sgu_ln_g"], g["sgu_ln_b"], g["w_spatial"], g["b_spatial"] = _sgu_bwd(
        p, dbr[1], wl["sgu_ln_g"], wl["sgu_ln_b"], wl["w_spatial"], wl["w_spatial"].transpose(0, 2, 1), sv["b_st"])
    dcb, dcc, dcx, g["conv_w"] = _conv_bwd(p, wl["conv_w"], dbr[2])
    dp = jnp.concatenate([dq, dk.astype(BF16), dv.astype(BF16), dz, dcb, dcc, dcx, dgates], axis=1)
    g["w_in"] = mm(sv["h1"], dp, mode="tn", out_dtype=BF16, name="mm_in_dw", jobs=[exchange(), exchange()])
    dx, g["norm_mix_g"] = mm(dp, wl["w_in"], mode="nt", name="mm_in_dx", rms_bwd=(sv["x"], wl["norm_mix_g"], dx1),
                             job=to_owner(["w_in"]))
    return dx, g


def _local_step(x, mem, target, layers, final_g):
    h, saved = x, []
    for wl in layers:
        h, sv, _ = _layer_fwd(h, mem, wl)
        saved.append(sv)
    loss, dx, d_final = _loss_head(h, final_g, target)
    grads = [None] * len(layers)
    for l in reversed(range(len(layers))):
        dx, grads[l] = _layer_bwd(dx, mem, saved[l])
    return loss, dx, grads, d_final


_ALL = slice(None)
CONV_ROWS = 8
_SHARD = {
    "w_in": lambda b: (_ALL, pl.ds(1792 * b, 1792)),
    "w_br": lambda b: (_ALL, _ALL, pl.ds(256 * b, 256)),
    "w_sq": lambda b: (_ALL, pl.ds(256 * b, 256), _ALL),
    "w_gu": lambda b: (b,),
    "w_dn": lambda b: (b,),
    "conv_w": lambda b: (_ALL, pl.ds(128 * b, 128)),
}
_FULL_SHAPE = {"w_in": (1024, 7168), "w_br": (3, 512, 1024), "w_sq": (5, 1024, 1024), "w_gu": (4, 2, 704, 1024),
               "w_dn": (4, 704, 1024), "conv_w": (CONV_ROWS, 512)}
_SHARD_SHAPE = {"w_in": (1024, 1792), "w_br": (3, 512, 256), "w_sq": (5, 256, 1024), "w_gu": (2, 704, 1024),
                "w_dn": (704, 1024), "conv_w": (CONV_ROWS, 128)}


def _pos():
    return lax.axis_index("x"), lax.axis_index("y"), lax.axis_index("c")


def _per_chip(fn):
    x, y, _ = _pos()
    for x0 in (0, 1):
        for y0 in (0, 1):
            @pl.when((x == x0) & (y == y0))
            def _():
                fn(x0, y0)


def _other_chips(x0, y0):
    return [(1 - x0, y0), (x0, 1 - y0), (1 - x0, 1 - y0)]


def _rcopy(src, dst, ssem, rsem, dev):
    return pltpu.make_async_remote_copy(src_ref=src, dst_ref=dst, send_sem=ssem, recv_sem=rsem, device_id=dev, device_id_type=MESH)


def _dma_sems(n):
    return pltpu.SemaphoreType.DMA((n,))


def _gather_rider(layer, placed):
    names = list(placed)
    n = len(names)
    shard = lambda refs, a, b: refs[a].at[_SHARD[names[a]](b)]

    def start(ins, outs, send, recv):
        @pl.when(lax.axis_index("c") == layer)
        def _():
            def run(x0, y0):
                for kk, (px, py) in enumerate(_other_chips(x0, y0)):
                    for a in range(n):
                        own = shard(outs, a, 2 * x0 + y0)
                        _rcopy(own, own, send.at[6 * a + kk], recv.at[6 * a + kk], (px, py, layer)).start()

            _per_chip(run)

    def passing(outs, send, recv, a, kk, bp, x0, y0):
        landed = shard(outs, a, bp)
        return _rcopy(landed, landed, send.at[6 * a + 3 + kk], recv.at[6 * a + 3 + kk], (x0, y0, 1 - layer))

    def middle(ins, outs, send, recv):
        @pl.when(lax.axis_index("c") == layer)
        def _():
            def run(x0, y0):
                for kk, (px, py) in enumerate(_other_chips(x0, y0)):
                    for a in range(n):
                        landed = shard(outs, a, 2 * px + py)
                        _rcopy(landed, landed, send.at[6 * a + kk], recv.at[6 * a + kk], (px, py, layer)).wait_recv()
                        passing(outs, send, recv, a, kk, 2 * px + py, x0, y0).start()

            _per_chip(run)

    def finish(ins, outs, send, recv):
        c = lax.axis_index("c")

        def run(x0, y0):
            chips = _other_chips(x0, y0)

            @pl.when(c == layer)
            def _():
                for kk, (px, py) in enumerate(chips):
                    for a in range(n):
                        own = shard(outs, a, 2 * x0 + y0)
                        _rcopy(own, own, send.at[6 * a + kk], recv.at[6 * a + kk], (px, py, layer)).wait_send()
                        passing(outs, send, recv, a, kk, 2 * px + py, x0, y0).wait_send()

            @pl.when(c != layer)
            def _():
                for kk, (px, py) in enumerate(chips):
                    for a in range(n):
                        got = shard(outs, a, 2 * px + py)
                        _rcopy(got, got, send.at[6 * a + 3 + kk], recv.at[6 * a + 3 + kk], (x0, y0, layer)).wait_recv()

        _per_chip(run)

    arrs = [placed[nm] for nm in names]
    return _Rider(arrs, [jax.ShapeDtypeStruct(t.shape, t.dtype) for t in arrs], 6 * n, start, finish, alias={a: a for a in range(n)}, middle=middle)


def _presum_rider(layer, grads):
    names = list(grads)
    n = len(names)

    def start(ins, outs, send, recv):
        x, y, c = _pos()

        @pl.when(c != layer)
        def _():
            for a in range(n):
                _rcopy(ins[a], outs[a], send.at[a], recv.at[a], (x, y, layer)).start()

    def finish(ins, outs, send, recv):
        x, y, c = _pos()

        @pl.when(c != layer)
        def _():
            for a in range(n):
                _rcopy(ins[a], outs[a], send.at[a], recv.at[a], (x, y, layer)).wait_send()

        @pl.when(c == layer)
        def _():
            for a in range(n):
                _rcopy(outs[a], outs[a], send.at[a], recv.at[a], (x, y, 1 - layer)).wait_recv()

    arrs = [grads[nm] for nm in names]
    return _Rider(arrs, [jax.ShapeDtypeStruct(t.shape, t.dtype) for t in arrs], n, start, finish)


def _shard_rider(layer, part):
    names = list(part)
    n = len(names)

    def each(fn):
        @pl.when(lax.axis_index("c") == layer)
        def _():
            def run(x0, y0):
                for kk, (px, py) in enumerate(_other_chips(x0, y0)):
                    for a in range(n):
                        fn(a, kk, 2 * px + py, (px, py, layer))

            _per_chip(run)

    def start(ins, outs, send, recv):
        each(lambda a, kk, bp, peer: _rcopy(ins[a].at[_SHARD[names[a]](bp)], outs[a].at[kk], send.at[3 * a + kk], recv.at[3 * a + kk], peer).start())

    def finish(ins, outs, send, recv):
        each(lambda a, kk, bp, peer: _rcopy(outs[a].at[kk], outs[a].at[kk], send.at[3 * a + kk], recv.at[3 * a + kk], peer).wait_recv())
        each(lambda a, kk, bp, peer: _rcopy(ins[a].at[_SHARD[names[a]](bp)], outs[a].at[kk], send.at[3 * a + kk], recv.at[3 * a + kk], peer).wait_send())

    return _Rider([part[nm] for nm in names], [jax.ShapeDtypeStruct((N_CHIPS - 1,) + _SHARD_SHAPE[nm], part[nm].dtype) for nm in names],
                  3 * n, start, finish)


def _sibling_exchange(mine0, mine1):
    names = list(mine0)
    n = len(names)

    def body(*refs):
        l0, l1, outs = refs[:n], refs[n:2 * n], refs[2 * n:3 * n]
        send, recv = refs[3 * n:]
        x, y, c = _pos()
        for c0 in (0, 1):
            @pl.when(c == c0)
            def _():
                srcs = l0 if c0 == 0 else l1
                cps = [_rcopy(srcs[a], outs[a], send.at[a], recv.at[a], (x, y, 1 - c0)) for a in range(n)]
                for cp in cps:
                    cp.start()
                for cp in cps:
                    cp.wait()

    outs = pl.pallas_call(
        body, name="grad_sibling_exchange", in_specs=[ANY] * (2 * n), out_specs=[ANY] * n,
        out_shape=[jax.ShapeDtypeStruct(mine0[nm].shape, mine0[nm].dtype) for nm in names],
        scratch_shapes=[_dma_sems(n), _dma_sems(n)],
    )(*[mine0[nm] for nm in names], *[mine1[nm] for nm in names])
    return dict(zip(names, outs))


def _small_rider(pack):
    flips = [(fx, fy, fc) for fx in (0, 1) for fy in (0, 1) for fc in (0, 1) if fx or fy or fc]

    def peers():
        x, y, c = _pos()
        return 4 * x + 2 * y + c, [(x ^ fx, y ^ fy, c ^ fc) for fx, fy, fc in flips]

    def start(ins, outs, send, recv):
        me, to = peers()
        for k, peer in enumerate(to):
            _rcopy(ins[0], outs[0].at[me], send.at[k], recv.at[k], peer).start()

    def finish(ins, outs, send, recv):
        me, to = peers()
        for k, (px, py, pc) in enumerate(to):
            slot = outs[0].at[4 * px + 2 * py + pc]
            _rcopy(slot, slot, send.at[k], recv.at[k], (px, py, pc)).wait_recv()
        for k, peer in enumerate(to):
            _rcopy(ins[0], outs[0].at[me], send.at[k], recv.at[k], peer).wait_send()

    return _Rider([pack], [jax.ShapeDtypeStruct((8,) + pack.shape, pack.dtype)], len(flips), start, finish)


def _sum_devices(gathered, pack, me):
    n, R, C = gathered.shape

    def body(me_ref, r_ref, own_ref, o_ref):
        acc = jnp.where(me_ref[0] == 0, own_ref[...], r_ref[0])
        for s in range(1, n):
            acc = acc + jnp.where(me_ref[0] == s, own_ref[...], r_ref[s])
        o_ref[...] = acc

    return pl.pallas_call(
        body, name="sum_devices_small",
        grid_spec=pltpu.PrefetchScalarGridSpec(num_scalar_prefetch=1, grid=(1,),
                                               in_specs=[pl.BlockSpec((n, R, C), lambda i, m: (0, 0, 0)), pl.BlockSpec((R, C), lambda i, m: (0, 0))],
                                               out_specs=pl.BlockSpec((R, C), lambda i, m: (0, 0))),
        out_shape=jax.ShapeDtypeStruct((R, C), F32), compiler_params=_cp("arbitrary"),
    )(_scalar(me), gathered, pack)


_WEIGHTS = ["norm_mix_g", "w_in", "sgu_ln_g", "sgu_ln_b", "w_spatial", "b_spatial", "conv_w", "w_branch", "w_out", "norm_xa_g",
            "mem_norm_g", "w_q_xa", "w_k_xa", "w_v_xa", "w_o_xa", "norm_ffn_g", "w_gate_ffn", "w_up_ffn", "w_down_ffn", "final_g"]
_REPLICATED = ["norm_mix_g", "sgu_ln_g", "sgu_ln_b", "w_spatial", "b_spatial", "norm_xa_g", "mem_norm_g", "norm_ffn_g", "final_g"]
_SQUARE = ["w_out", "w_q_xa", "w_o_xa", "w_k_xa", "w_v_xa"]
_BIG = ["w_in", "w_br", "w_sq", "w_gu", "w_dn"]


def _pack(arrs):
    return jnp.concatenate([a.reshape(-1) for a in arrs]).reshape(-1, 128)


def _step(a):
    w = {n: a[n] for n in _WEIGHTS}
    x, mem, target = a["x"][0], a["mem"][0], a["loss_target"][0]
    xi, yi, ci = _pos()
    bi = 2 * xi + yi
    groups = list(_FULL_SHAPE)

    tr = lambda t: jnp.swapaxes(t, 1, 2)
    local = {"w_in": w["w_in"], "w_br": w["w_branch"], "w_sq": jnp.stack([w[n] for n in _SQUARE], axis=1),
             "w_gu": jnp.stack([tr(w["w_gate_ffn"]), tr(w["w_up_ffn"])], axis=1), "w_dn": w["w_down_ffn"],
             "conv_w": jnp.pad(w["conv_w"], ((0, 0), (0, CONV_ROWS - 3), (0, 0)))}
    placed = [dict(), dict()]
    for n in groups:
        placed[0][n], placed[1][n] = _place(n, local[n], bi)

    def gather(l, names):
        return _gather_rider(l, {n: placed[l][n] for n in names})

    def start_of(l, w_in):
        return {"w_in": w_in, **{n: w[n][l] for n in _REPLICATED if n != "final_g"}}

    w_in0, = _run_rider(gather(0, ["w_in"]), name="gather_first")
    h, sv0, (w_in1,) = _layer_fwd(x, mem, start_of(0, w_in0), {"mm_in": gather(0, _GATHER_LATE), "sb": gather(0, _GATHER_LAST),
                                                                "mm_gu": gather(1, ["w_in"])})
    h, sv1, _ = _layer_fwd(h, mem, start_of(1, w_in1), {"mm_in": gather(1, _GATHER_LATE), "sb": gather(1, _GATHER_LAST), "mm_gu": None})
    loss, dx, d_final = _loss_head(h, w["final_g"], target)
    loss = lax.psum(loss, ("x", "y", "c"))

    pipe = _GradPipe(ci, bi)
    dx, g1 = _layer_bwd(dx, mem, sv1, 1, pipe)
    dx, g0 = _layer_bwd(dx, mem, sv0, 0, pipe)
    grads = [g0, g1]
    small = {n: jnp.stack([g[n] for g in grads]) for n in _REPLICATED if n != "final_g"}
    small["final_g"] = d_final
    conv_g = jnp.stack([g["conv_w"] for g in grads])
    small_pack = _pack([small[n] for n in _REPLICATED] + [conv_g])
    gathered, = pipe.drain(_small_rider(small_pack))
    mine = [pipe.reduced(0), pipe.reduced(1)]
    theirs = _sibling_exchange(mine[0], mine[1])

    out = {}

    def adam_layers(name, group, pick=None, view=lambda t: t):
        sel = (lambda t: t[group]) if pick is None else (lambda t: t[group][pick])
        res = _adam_layers([sel(mine[0]), sel(mine[1])], sel(theirs), ci, view(w[name]), view(a["m_" + name]), view(a["v_" + name]), name="adam_" + name)
        out[name] = tuple(view(r) for r in res)

    adam_layers("w_in", "w_in")
    adam_layers("w_branch", "w_br")
    for t, n in enumerate(_SQUARE):
        adam_layers(n, "w_sq", t)
    adam_layers("w_gate_ffn", "w_gu", 0, tr)
    adam_layers("w_up_ffn", "w_gu", 1, tr)
    adam_layers("w_down_ffn", "w_dn")

    def adam(name, g):
        out[name] = _reduce_adam([g], w[name], a["m_" + name], a["v_" + name], name="adam_" + name)

    n_rep = sum(w[n].size for n in _REPLICATED) // 128
    summed = _sum_devices(gathered, small_pack, 4 * xi + 2 * yi + ci)
    res = _reduce_adam([summed[:n_rep]], _pack([w[n] for n in _REPLICATED]), _pack([a["m_" + n] for n in _REPLICATED]),
                       _pack([a["v_" + n] for n in _REPLICATED]), name="adam_replicated")
    off = 0
    for n in _REPLICATED:
        out[n] = tuple(r.reshape(-1)[off:off + w[n].size].reshape(w[n].shape) for r in res)
        off += w[n].size
    conv_full = summed[n_rep:].reshape(conv_g.shape)
    adam("conv_w", lax.dynamic_slice_in_dim(conv_full, (2 * xi + yi) * 128, 128, axis=2))

    return (loss, dx[None], *[out[n][k] for k in range(4) for n in _WEIGHTS])


def kernel(x, mem, norm_mix_g, w_in, sgu_ln_g, sgu_ln_b, w_spatial, b_spatial, conv_w, w_branch, w_out, norm_xa_g, mem_norm_g, w_q_xa, w_k_xa, w_v_xa, w_o_xa, norm_ffn_g, w_gate_ffn, w_up_ffn, w_down_ffn, final_g, loss_target, m_norm_mix_g, m_w_in, m_sgu_ln_g, m_sgu_ln_b, m_w_spatial, m_b_spatial, m_conv_w, m_w_branch, m_w_out, m_norm_xa_g, m_mem_norm_g, m_w_q_xa, m_w_k_xa, m_w_v_xa, m_w_o_xa, m_norm_ffn_g, m_w_gate_ffn, m_w_up_ffn, m_w_down_ffn, m_final_g, v_norm_mix_g, v_w_in, v_sgu_ln_g, v_sgu_ln_b, v_w_spatial, v_b_spatial, v_conv_w, v_w_branch, v_w_out, v_norm_xa_g, v_mem_norm_g, v_w_q_xa, v_w_k_xa, v_w_v_xa, v_w_o_xa, v_norm_ffn_g, v_w_gate_ffn, v_w_up_ffn, v_w_down_ffn, v_final_g):
    return _step(dict(locals()))
```

```python
import math

import jax
import jax.numpy as jnp
from jax import lax
from jax.experimental import pallas as pl
from jax.experimental.pallas import tpu as pltpu

F32, BF16 = jnp.float32, jnp.bfloat16
MESH = pl.DeviceIdType.MESH
ANY = pl.BlockSpec(memory_space=pl.ANY)

D_MODEL = 1024
DEPTH = 2
BW = 512
SB_HEADS, SB_DH = 8, 64
SGU_LEN, SGU_GROUPS, SGU_GD, SGU_CHUNK = 128, 4, 128, 64
XA_HEADS, XA_DH = 4, 256
FFN_SH = 704
N_CHIPS = 4
C_Z, C_CB, C_GATES = 1536, 2560, 4096

ADAM_LR, ADAM_B1, ADAM_B2, ADAM_EPS, ADAM_WD, ADAM_STEP = 0.001, 0.9, 0.999, 1e-08, 0.01, 10

VMEM_LIMIT_V7X = 56 * 1024 * 1024

NN = (((1,), (0,)), ((), ()))
NT = (((1,), (1,)), ((), ()))
TN = (((0,), (0,)), ((), ()))


def _cp(*sem):
    return pltpu.CompilerParams(dimension_semantics=sem, vmem_limit_bytes=VMEM_LIMIT_V7X)


def _tile(n, pref):
    for t in pref:
        if n % t == 0:
            return t
    return n


def _rows(r, row_bytes, block_bytes=1 << 20):
    fits = [t for t in range(8, r + 1, 8) if r % t == 0 and t * row_bytes <= block_bytes]
    return max(fits) if fits else r


class _Rider:
    def __init__(self, ins, outs, n_sems, start, finish, alias=None, middle=None):
        self.ins, self.outs, self.n_sems, self.alias = list(ins), list(outs), n_sems, alias or {}
        self.start, self.middle, self.finish = start, middle, finish


class _Sems:
    def __init__(self, ref, first):
        self.ref, self.first = ref, first

    @property
    def at(self):
        return self

    def __getitem__(self, k):
        return self.ref.at[self.first + k]


def _join(riders):
    riders = [r for r in riders if r is not None]
    if not riders:
        return None
    spans, i0, o0, s0 = [], 0, 0, 0
    for r in riders:
        spans.append((r, i0, o0, s0))
        i0, o0, s0 = i0 + len(r.ins), o0 + len(r.outs), s0 + r.n_sems

    def phase(which):
        def run(ins, outs, send, recv):
            for r, i, o, s in spans:
                fn = getattr(r, which)
                if fn is not None:
                    fn(ins[i:i + len(r.ins)], outs[o:o + len(r.outs)], _Sems(send, s), _Sems(recv, s))
        return run

    joined = _Rider([a for r in riders for a in r.ins], [a for r in riders for a in r.outs], s0, phase("start"), phase("finish"),
                    alias={o + k: i + v for r, i, o, s in spans for k, v in r.alias.items()}, middle=phase("middle"))
    joined.split = lambda landed: [list(landed[o:o + len(r.outs)]) for r, i, o, s in spans]
    return joined


def _call_with_rider(rider, body, *, name, grid, in_specs, args, out_specs, out_shape, scratch_shapes, semantics):
    if rider is None:
        return pl.pallas_call(body, name=name, grid=grid, in_specs=in_specs, out_specs=out_specs, out_shape=out_shape,
                              scratch_shapes=scratch_shapes, compiler_params=_cp(*semantics))(*args)
    n_in, n_out, r_in, r_out = len(args), len(out_shape), len(rider.ins), len(rider.outs)

    def riding(*refs):
        ins, rins = refs[:n_in], refs[n_in:n_in + r_in]
        outs, routs = refs[n_in + r_in:n_in + r_in + n_out], refs[n_in + r_in + n_out:n_in + r_in + n_out + r_out]
        rest = refs[n_in + r_in + n_out + r_out:]
        scratch, send, recv = rest[:-2], rest[-2], rest[-1]
        step = pl.program_id(0)
        for ax in range(1, len(grid)):
            step = step * grid[ax] + pl.program_id(ax)
        n_steps = math.prod(grid)

        @pl.when(step == 0)
        def _():
            rider.start(rins, routs, send, recv)

        body(*ins, *outs, *scratch)

        if rider.middle is not None:
            @pl.when(step == (3 * n_steps) // 5)
            def _():
                rider.middle(rins, routs, send, recv)

        @pl.when(step == n_steps - 1)
        def _():
            rider.finish(rins, routs, send, recv)

    return pl.pallas_call(
        riding, name=name, grid=grid, in_specs=list(in_specs) + [ANY] * r_in, out_specs=list(out_specs) + [ANY] * r_out,
        out_shape=list(out_shape) + rider.outs, scratch_shapes=list(scratch_shapes) + [_dma_sems(rider.n_sems), _dma_sems(rider.n_sems)],
        input_output_aliases={n_in + i: n_out + o for o, i in rider.alias.items()},
        compiler_params=_cp(*["arbitrary"] * len(grid)),
    )(*args, *rider.ins)


def _run_rider(rider, *, name):
    def nothing(*refs):
        pass

    return _call_with_rider(rider, nothing, name=name, grid=(1,), in_specs=[], args=[], out_specs=[], out_shape=[], scratch_shapes=[],
                            semantics=("arbitrary",))


def _mm(a, b, *, mode, name, out_dtype=F32, res=None, norm=None, rms_bwd=None, a_kind="2d", b_kind="2d", tm=None, tn=None, tk=None, rider=None):
    a2, b2 = a.shape[-2:], b.shape[-2:]
    if mode == "nn":
        (M, K), N = a2, b2[1]
    elif mode == "nt":
        (M, K), N = a2, b2[0]
    else:
        (K, M), N = a2, b2[1]
    kchunk = a_kind == "kchunk" or b_kind == "kchunk"
    batch = a_kind == "batch" or b_kind == "batch"
    G = (a.shape[0] if a_kind == "batch" else b.shape[0]) if batch else 1
    tm = tm or _tile(M, (1024, 512, 256, 128))
    tn = tn or _tile(N, (1024, 512, 256, 128))
    if kchunk:
        tk, nk = K, (a.shape[0] if a_kind == "kchunk" else b.shape[0])
    else:
        tk = tk or _tile(K, (1024, 512, 256, 128))
        nk = K // tk

    def spec(kind, blk, idx):
        if kind == "2d":
            return pl.BlockSpec(blk, lambda g, i, j, k: idx(g, i, j, k))
        if kind == "batch":
            return pl.BlockSpec((None,) + blk, lambda g, i, j, k: (g,) + idx(g, i, j, k))
        return pl.BlockSpec((None,) + blk, lambda g, i, j, k: (k,) + idx(g, i, j, 0))

    if mode == "nn":
        a_spec = spec(a_kind, (tm, tk), lambda g, i, j, k: (i, k))
        b_spec = spec(b_kind, (tk, tn), lambda g, i, j, k: (k, j))
    elif mode == "nt":
        a_spec = spec(a_kind, (tm, tk), lambda g, i, j, k: (i, k))
        b_spec = spec(b_kind, (tn, tk), lambda g, i, j, k: (j, k))
    else:
        a_spec = spec(a_kind, (tk, tm), lambda g, i, j, k: (k, i))
        b_spec = spec(b_kind, (tk, tn), lambda g, i, j, k: (k, j))
    o_kind = "batch" if batch else "2d"
    o_spec = spec(o_kind, (tm, tn), lambda g, i, j, k: (i, j))
    o_shape = ((G,) if batch else ()) + (M, N)
    dn = {"nn": NN, "nt": NT, "tn": TN}[mode]
    has_res, has_rms, has_norm = res is not None, rms_bwd is not None, norm is not None
    assert not (has_res and has_rms) and (not (has_rms or has_norm) or (tn == N and G == 1)) and (has_res or not has_norm)

    def body(*refs):
        if has_norm:
            a_ref, b_ref, r_ref, g_ref, o_ref, h_ref = refs[:6]
        elif has_res:
            a_ref, b_ref, r_ref, o_ref = refs[:4]
        elif has_rms:
            a_ref, b_ref, x_ref, g_ref, dres_ref, o_ref, dg_ref = refs[:7]
        else:
            a_ref, b_ref, o_ref = refs[:3]
        p = lax.dot_general(a_ref[...].astype(BF16), b_ref[...].astype(BF16), dn, preferred_element_type=F32)
        first_rows = pl.program_id(1) == 0

        def finish(r):
            if has_rms:
                xv = x_ref[...]
                rs = lax.rsqrt(jnp.mean(xv * xv, axis=-1, keepdims=True) + 1e-6)
                u = r * g_ref[...]
                s = jnp.sum(u * xv, axis=-1, keepdims=True)
                o_ref[...] = dres_ref[...] + rs * u - xv * ((rs * rs * rs) * (s * (1.0 / N)))
                part = jnp.sum(r * (xv * rs), axis=0, keepdims=True)

                @pl.when(first_rows)
                def _():
                    dg_ref[...] = part

                @pl.when(jnp.logical_not(first_rows))
                def _():
                    dg_ref[...] += part
                return
            if has_res:
                r = r + r_ref[...]
            o_ref[...] = r.astype(out_dtype)
            if has_norm:
                h_ref[...] = (r * lax.rsqrt(jnp.mean(r * r, axis=-1, keepdims=True) + 1e-6) * g_ref[...]).astype(BF16)

        if nk == 1:
            finish(p)
        else:
            acc = refs[-1]
            k = pl.program_id(3)

            @pl.when(k == 0)
            def _():
                acc[...] = p

            @pl.when(k > 0)
            def _():
                acc[...] += p

            @pl.when(k == nk - 1)
            def _():
                finish(acc[...])

    in_specs, args = [a_spec, b_spec], [a, b]
    tile = spec("2d", (tm, tn), lambda g, i, j, k: (i, j))
    out_specs, out_shape = [o_spec], [jax.ShapeDtypeStruct(o_shape, out_dtype)]
    vec = pl.BlockSpec((1, tn), lambda g, i, j, k: (0, 0))
    if has_res:
        in_specs.append(tile)
        args.append(res)
    if has_norm:
        in_specs.append(vec)
        args.append(norm.reshape(1, N))
        out_specs.append(tile)
        out_shape.append(jax.ShapeDtypeStruct((M, N), BF16))
    if has_rms:
        x, gain, dres = rms_bwd
        in_specs += [tile, vec, tile]
        args += [x, gain.reshape(1, N), dres]
        out_specs.append(vec)
        out_shape.append(jax.ShapeDtypeStruct((1, N), F32))
    outs = _call_with_rider(
        rider, body, name=name, grid=(G, M // tm, N // tn, nk), in_specs=in_specs, args=args, out_specs=out_specs,
        out_shape=out_shape, scratch_shapes=[pltpu.VMEM((tm, tn), F32)] if nk > 1 else [],
        semantics=("parallel", "arbitrary" if has_rms else "parallel", "parallel", "arbitrary"))
    main = (outs[0], outs[1].reshape(N)) if has_rms else (outs[0], outs[1]) if has_norm else outs[0]
    return main if rider is None else (main, outs[len(out_shape):])


def _rms_fwd(x, g, *, name):
    S, Dm = x.shape
    tm = _tile(S, (512, 256))

    def body(x_ref, g_ref, o_ref):
        xv = x_ref[...]
        r = lax.rsqrt(jnp.mean(xv * xv, axis=-1, keepdims=True) + 1e-6)
        o_ref[...] = (xv * r * g_ref[...]).astype(BF16)

    return pl.pallas_call(
        body, name=name, grid=(S // tm,),
        in_specs=[pl.BlockSpec((tm, Dm), lambda i: (i, 0)), pl.BlockSpec((1, Dm), lambda i: (0, 0))],
        out_specs=pl.BlockSpec((tm, Dm), lambda i: (i, 0)), out_shape=jax.ShapeDtypeStruct((S, Dm), BF16),
        compiler_params=_cp("parallel"),
    )(x, g.reshape(1, Dm))


def _rms_bwd(x, g, dh, dres, *, name):
    S, Dm = x.shape
    tm = _tile(S, (512, 256))

    def body(x_ref, g_ref, dh_ref, dr_ref, dx_ref, dg_ref):
        xv, dhv = x_ref[...], dh_ref[...].astype(F32)
        r = lax.rsqrt(jnp.mean(xv * xv, axis=-1, keepdims=True) + 1e-6)
        u = dhv * g_ref[...]
        s = jnp.sum(u * xv, axis=-1, keepdims=True)
        dx_ref[...] = dr_ref[...] + r * u - xv * ((r * r * r) * (s * (1.0 / Dm)))
        part = jnp.sum(dhv * (xv * r), axis=0, keepdims=True)

        @pl.when(pl.program_id(0) == 0)
        def _():
            dg_ref[...] = part

        @pl.when(pl.program_id(0) > 0)
        def _():
            dg_ref[...] += part

    row = pl.BlockSpec((tm, Dm), lambda i: (i, 0))
    vec = pl.BlockSpec((1, Dm), lambda i: (0, 0))
    dx, dg = pl.pallas_call(
        body, name=name, grid=(S // tm,), in_specs=[row, vec, row, row], out_specs=[row, vec],
        out_shape=[jax.ShapeDtypeStruct((S, Dm), F32), jax.ShapeDtypeStruct((1, Dm), F32)],
        compiler_params=_cp("arbitrary"),
    )(x, g.reshape(1, Dm), dh, dres)
    return dx, dg.reshape(Dm)


def _loss_head(x, g, target):
    S, Dm = x.shape
    tm = _tile(S, (512, 256))

    def body(x_ref, g_ref, t_ref, dx_ref, dg_ref, loss_ref):
        xv, gv = x_ref[...], g_ref[...]
        r = lax.rsqrt(jnp.mean(xv * xv, axis=-1, keepdims=True) + 1e-6)
        xn = xv * r
        err = xn * gv - t_ref[...]
        lpart = 0.5 * jnp.sum(jnp.mean(err * err, axis=-1, keepdims=True), axis=0, keepdims=True)
        dy = err * (1.0 / Dm)
        u = dy * gv
        s = jnp.sum(u * xv, axis=-1, keepdims=True)
        dx_ref[...] = r * u - xv * ((r * r * r) * (s * (1.0 / Dm)))
        part = jnp.sum(dy * xn, axis=0, keepdims=True)
        lslab = jnp.broadcast_to(lpart, (8, 128))

        @pl.when(pl.program_id(0) == 0)
        def _():
            dg_ref[...] = part
            loss_ref[...] = lslab

        @pl.when(pl.program_id(0) > 0)
        def _():
            dg_ref[...] += part
            loss_ref[...] += lslab

    row = pl.BlockSpec((tm, Dm), lambda i: (i, 0))
    vec = pl.BlockSpec((1, Dm), lambda i: (0, 0))
    dx, dg, loss = pl.pallas_call(
        body, name="loss_head", grid=(S // tm,), in_specs=[row, vec, row],
        out_specs=[row, vec, pl.BlockSpec((8, 128), lambda i: (0, 0))],
        out_shape=[jax.ShapeDtypeStruct((S, Dm), F32), jax.ShapeDtypeStruct((1, Dm), F32), jax.ShapeDtypeStruct((8, 128), F32)],
        compiler_params=_cp("arbitrary"),
    )(x, g.reshape(1, Dm), target)
    return loss[0, 0], dx, dg.reshape(Dm)


SB_TQ, SB_TK = 256, 256
SB_EXP_FLOOR = -104.0


def _split2(v):
    hi = v.astype(BF16)
    return jnp.concatenate([hi, (v - hi.astype(F32)).astype(BF16)], axis=1)


def _tri2(cmp):
    j = lax.broadcasted_iota(jnp.int32, (2 * SB_TK, SB_TK), 0) % SB_TK
    s = lax.broadcasted_iota(jnp.int32, (2 * SB_TK, SB_TK), 1)
    return cmp(j, s).astype(BF16)


def _sb_scores(qv, kb, k0, q0, tq):
    rows = qv.shape[0]
    z = lax.dot_general(qv, kb, NT, preferred_element_type=F32) * (SB_DH ** -0.5)
    t_pos = q0 + lax.broadcasted_iota(jnp.int32, (rows, SB_TK), 0) % tq
    s_pos = k0 + lax.broadcasted_iota(jnp.int32, (rows, SB_TK), 1)
    valid = s_pos < t_pos
    ls = jnp.minimum(z, 0.0) - jnp.log(1.0 + jnp.exp(-jnp.abs(z)))
    l1m = jnp.where(valid, ls - z, 0.0)
    return z, valid, ls, l1m


SB_PAIRS = SB_HEADS // 2
_Q_BLK, _K_BLK, _V_BLK = 0, SB_PAIRS, 2 * SB_PAIRS


def _wide(x):
    return x if SB_TK == 128 else jnp.concatenate([x] * (SB_TK // 128), axis=1)


def _lanes_of(h, shape):
    lane = lax.broadcasted_iota(jnp.int32, shape, len(shape) - 1)
    return (lane < SB_DH) if h == 0 else (lane >= SB_DH)


def _sb2_fwd(p, rider=None):
    S = p.shape[0]
    tq = min(SB_TQ, S)
    kb_per_q = tq // SB_TK

    def body(q_ref, k_ref, v_ref, o_ref, tot_ref, cnt_ref, qm, acc, c):
        i = pl.program_id(1)
        q0 = i * tq
        later = _tri2(lambda j, s: j > s)
        q2 = q_ref[...]
        for h in range(2):
            qm[h * tq:(h + 1) * tq, :] = jnp.where(_lanes_of(h, q2.shape), q2, 0.0).astype(BF16)
        acc[...] = jnp.zeros_like(acc)
        c[...] = jnp.zeros_like(c)
        nkb = (i + 1) * kb_per_q

        def more(st):
            n, highest = st
            return (n < nkb) & (highest > SB_EXP_FLOOR)

        def step(st):
            n, _ = st
            k0 = pl.multiple_of((nkb - 1 - n) * SB_TK, SB_TK)
            kb, vb = k_ref[pl.ds(k0, SB_TK), :].astype(BF16), v_ref[pl.ds(k0, SB_TK), :].astype(BF16)
            c_old = c[...]
            z, valid, ls, l1m = _sb_scores(qm[...], kb, k0, q0, tq)
            c_new = c_old + jnp.sum(l1m, axis=1, keepdims=True)
            after = jnp.dot(_split2(l1m), later, preferred_element_type=F32)
            a = jnp.where(valid, jnp.exp(ls + after + _wide(c_old)), 0.0)
            av = jnp.dot(a.astype(BF16), vb, preferred_element_type=F32)
            acc[...] += jnp.where(_lanes_of(0, (tq, 128)), av[:tq], av[tq:])
            c[...] = c_new
            return n + 1, jnp.max(c_new)

        n_done, _ = lax.while_loop(more, step, (jnp.int32(0), jnp.float32(0.0)))
        o_ref[...] = acc[...].astype(o_ref.dtype)
        for h in range(2):
            tot_ref[h] = c[h * tq:(h + 1) * tq, :]
        cnt_ref[...] = jnp.full(cnt_ref.shape, n_done.astype(F32))

    col = lambda first: pl.BlockSpec((S, 128), lambda g, i: (0, first + g))
    outs = _call_with_rider(
        rider, body, name="sb_fwd", grid=(SB_PAIRS, S // tq), args=[p, p, p],
        in_specs=[pl.BlockSpec((tq, 128), lambda g, i: (i, _Q_BLK + g)), col(_K_BLK), col(_V_BLK)],
        out_specs=[pl.BlockSpec((tq, 128), lambda g, i: (i, g)), pl.BlockSpec((2, tq, 128), lambda g, i: (g, i, 0)),
                   pl.BlockSpec((None, None, 8, 128), lambda g, i: (g, i, 0, 0))],
        out_shape=[jax.ShapeDtypeStruct((S, BW), BF16), jax.ShapeDtypeStruct((SB_HEADS, S, 128), F32),
                   jax.ShapeDtypeStruct((SB_PAIRS, S // tq, 8, 128), F32)],
        scratch_shapes=[pltpu.VMEM((2 * tq, 128), BF16), pltpu.VMEM((tq, 128), F32), pltpu.VMEM((2 * tq, 128), F32)],
        semantics=("parallel", "parallel"))
    return outs[:3], outs[3:]


def _sb2_bwd(p, dbr, tot, cnt, rider=None):
    S = p.shape[0]
    tq = min(SB_TQ, S)
    kb_per_q = tq // SB_TK
    scale = SB_DH ** -0.5

    def body(q_ref, k_ref, v_ref, do_ref, tot_ref, cnt_ref, dq_ref, dk_ref, dv_ref, qm, dom, tot, dq_acc, pre, gpre):
        i = pl.program_id(1)
        q0 = i * tq
        upto = _tri2(lambda j, s: j <= s)
        before = _tri2(lambda j, s: j < s)

        @pl.when(i == 0)
        def _():
            dk_ref[...] = jnp.zeros_like(dk_ref)
            dv_ref[...] = jnp.zeros_like(dv_ref)

        q2, do2 = q_ref[...], do_ref[...]
        for h in range(2):
            rows = slice(h * tq, (h + 1) * tq)
            qm[rows, :] = jnp.where(_lanes_of(h, q2.shape), q2, 0.0).astype(BF16)
            dom[rows, :] = jnp.where(_lanes_of(h, do2.shape), do2, 0.0).astype(BF16)
            tot[rows, :] = tot_ref[h]
        dq_acc[...] = jnp.zeros_like(dq_acc)
        pre[...] = jnp.zeros_like(pre)
        gpre[...] = jnp.zeros_like(gpre)

        n_done = jnp.max(cnt_ref[...]).astype(jnp.int32)
        first = (i + 1) * kb_per_q - n_done

        def step(n, carry):
            k0 = pl.multiple_of((first + n) * SB_TK, SB_TK)
            kb, vb = k_ref[pl.ds(k0, SB_TK), :].astype(BF16), v_ref[pl.ds(k0, SB_TK), :].astype(BF16)
            pre_o, gpre_o = pre[...], gpre[...]
            z, valid, ls, l1m = _sb_scores(qm[...], kb, k0, q0, tq)
            incl = jnp.dot(_split2(l1m), upto, preferred_element_type=F32)
            rest = _wide(tot[...] - pre_o) - incl
            a = jnp.where(valid, jnp.exp(ls + rest), 0.0)
            da = lax.dot_general(dom[...], vb, NT, preferred_element_type=F32)
            g = a * da
            gbefore = jnp.dot(_split2(g), before, preferred_element_type=F32) + _wide(gpre_o)
            dz = jnp.where(valid, g * jnp.exp(ls - z) - jnp.exp(ls) * gbefore, 0.0) * scale
            dzb = dz.astype(BF16)
            dq_p = jnp.dot(dzb, kb, preferred_element_type=F32)
            dq_acc[...] += jnp.where(_lanes_of(0, (tq, 128)), dq_p[:tq], dq_p[tq:])
            dk_ref[pl.ds(k0, SB_TK), :] += lax.dot_general(dzb, qm[...], TN, preferred_element_type=F32)
            dv_ref[pl.ds(k0, SB_TK), :] += lax.dot_general(a.astype(BF16), dom[...], TN, preferred_element_type=F32)
            pre[...] = pre_o + jnp.sum(l1m, axis=1, keepdims=True)
            gpre[...] = gpre_o + jnp.sum(g, axis=1, keepdims=True)
            return carry

        lax.fori_loop(0, n_done, step, 0)
        dq_ref[...] = dq_acc[...].astype(dq_ref.dtype)

    col = lambda first: pl.BlockSpec((S, 128), lambda g, i: (0, first + g))
    tile = pl.BlockSpec((tq, 128), lambda g, i: (i, g))
    whole = pl.BlockSpec((S, 128), lambda g, i: (0, g))
    outs = _call_with_rider(
        rider, body, name="sb_bwd", grid=(SB_PAIRS, S // tq), args=[p, p, p, dbr, tot, cnt],
        in_specs=[pl.BlockSpec((tq, 128), lambda g, i: (i, _Q_BLK + g)), col(_K_BLK), col(_V_BLK),
                  pl.BlockSpec((None, tq, 128), lambda g, i: (0, i, g)), pl.BlockSpec((2, tq, 128), lambda g, i: (g, i, 0)),
                  pl.BlockSpec((None, None, 8, 128), lambda g, i: (g, i, 0, 0))],
        out_specs=[tile, whole, whole],
        out_shape=[jax.ShapeDtypeStruct((S, BW), BF16), jax.ShapeDtypeStruct((S, BW), F32), jax.ShapeDtypeStruct((S, BW), F32)],
        scratch_shapes=[pltpu.VMEM((2 * tq, 128), BF16), pltpu.VMEM((2 * tq, 128), BF16), pltpu.VMEM((2 * tq, 128), F32),
                        pltpu.VMEM((tq, 128), F32), pltpu.VMEM((2 * tq, 128), F32), pltpu.VMEM((2 * tq, 128), F32)],
        semantics=("parallel", "arbitrary"))
    return outs[:3], outs[3:]


_INV_SQRT2 = 0.7071067811865476
_INV_SQRT2PI = 0.3989422804014327


def _gelu(x):
    return 0.5 * x * (1.0 + lax.erf(x * _INV_SQRT2))


def _gelu_grad(x):
    return 0.5 * (1.0 + lax.erf(x * _INV_SQRT2)) + x * (_INV_SQRT2PI * jnp.exp(-0.5 * x * x))


def _sgu_mask():
    t = lax.broadcasted_iota(jnp.int32, (SGU_LEN, SGU_LEN), 0) // SGU_CHUNK
    s = lax.broadcasted_iota(jnp.int32, (SGU_LEN, SGU_LEN), 1) // SGU_CHUNK
    return t >= s


def _sgu_mask_t():
    t = lax.broadcasted_iota(jnp.int32, (SGU_LEN, SGU_LEN), 0) // SGU_CHUNK
    s = lax.broadcasted_iota(jnp.int32, (SGU_LEN, SGU_LEN), 1) // SGU_CHUNK
    return s >= t


def _sgu_norm(zv, g, b):
    vv = _gelu(zv)
    xc = vv - jnp.mean(vv, axis=-1, keepdims=True)
    rstd = lax.rsqrt(jnp.mean(xc * xc, axis=-1, keepdims=True) + 1e-5)
    xhat = xc * rstd
    return xhat, rstd, xhat * g + b


SGU_TM = 256


def _sgu_fwd(p, ln_g, ln_b, w_s, b_st):
    S = p.shape[0]
    tm = min(SGU_TM, S)

    def body(zu_ref, zv_ref, g_ref, b_ref, w_ref, bs_ref, o_ref):
        u = _gelu(zu_ref[...].astype(F32))
        _, _, vn = _sgu_norm(zv_ref[...].astype(F32), g_ref[...], b_ref[...])
        vnb = vn.astype(BF16)
        mask = _sgu_mask()
        for gi in range(SGU_GROUPS):
            wg = jnp.where(mask, w_ref[gi], 0.0).astype(BF16)
            cols = slice(gi * SGU_GD, (gi + 1) * SGU_GD)
            for ci in range(tm // SGU_LEN):
                rows = slice(ci * SGU_LEN, (ci + 1) * SGU_LEN)
                vm = jnp.dot(wg, vnb[rows, cols], preferred_element_type=F32) + bs_ref[:, gi:gi + 1]
                o_ref[rows, cols] = (u[rows, cols] * vm).astype(BF16)

    vec = pl.BlockSpec((1, BW), lambda i: (0, 0))
    return pl.pallas_call(
        body, name="sgu_fwd", grid=(S // tm,),
        in_specs=[pl.BlockSpec((tm, BW), lambda i: (i, C_Z // BW)), pl.BlockSpec((tm, BW), lambda i: (i, C_Z // BW + 1)), vec, vec,
                  pl.BlockSpec((SGU_GROUPS, SGU_LEN, SGU_LEN), lambda i: (0, 0, 0)), pl.BlockSpec((SGU_LEN, SGU_GROUPS), lambda i: (0, 0))],
        out_specs=pl.BlockSpec((tm, BW), lambda i: (i, 0)), out_shape=jax.ShapeDtypeStruct((S, BW), BF16),
        compiler_params=_cp("parallel"),
    )(p, p, ln_g.reshape(1, BW), ln_b.reshape(1, BW), w_s, b_st)


def _sgu_bwd(p, dyb, ln_g, ln_b, w_s, w_st, b_st):
    S = p.shape[0]
    tm = min(SGU_TM, S)

    def body(zu_ref, zv_ref, dy_ref, g_ref, b_ref, w_ref, wt_ref, bs_ref, dz_ref, dg_ref, db_ref, dw_ref, dbs_ref, dvn):
        first = pl.program_id(0) == 0

        @pl.when(first)
        def _():
            dg_ref[...] = jnp.zeros_like(dg_ref)
            db_ref[...] = jnp.zeros_like(db_ref)
            dw_ref[...] = jnp.zeros_like(dw_ref)
            dbs_ref[...] = jnp.zeros_like(dbs_ref)

        zu, zv, dy = zu_ref[...].astype(F32), zv_ref[...].astype(F32), dy_ref[...].astype(F32)
        u = _gelu(zu)
        xhat, rstd, vn = _sgu_norm(zv, g_ref[...], b_ref[...])
        vnb = vn.astype(BF16)
        mask = _sgu_mask()
        mask_t = _sgu_mask_t()
        for gi in range(SGU_GROUPS):
            wg = jnp.where(mask, w_ref[gi], 0.0).astype(BF16)
            wgt = jnp.where(mask_t, wt_ref[gi], 0.0).astype(BF16)
            cols = slice(gi * SGU_GD, (gi + 1) * SGU_GD)
            for ci in range(tm // SGU_LEN):
                rows = slice(ci * SGU_LEN, (ci + 1) * SGU_LEN)
                vm = jnp.dot(wg, vnb[rows, cols], preferred_element_type=F32) + bs_ref[:, gi:gi + 1]
                dyc = dy[rows, cols]
                dz_ref[rows, cols] = (dyc * vm * _gelu_grad(zu[rows, cols])).astype(BF16)
                dvm = dyc * u[rows, cols]
                dvmb = dvm.astype(BF16)
                dbs_ref[gi] += jnp.broadcast_to(jnp.sum(dvm, axis=1, keepdims=True), (SGU_LEN, SGU_GD))
                dw_ref[gi] += lax.dot_general(dvmb, vnb[rows, cols], NT, preferred_element_type=F32)
                dvn[rows, cols] = jnp.dot(wgt, dvmb, preferred_element_type=F32)
        dvnv = dvn[...]
        dg_ref[...] += jnp.sum(dvnv * xhat, axis=0, keepdims=True)
        db_ref[...] += jnp.sum(dvnv, axis=0, keepdims=True)
        dxh = dvnv * g_ref[...]
        dvv = rstd * (dxh - jnp.mean(dxh, axis=-1, keepdims=True) - xhat * jnp.mean(dxh * xhat, axis=-1, keepdims=True))
        dz_ref[:, BW:] = (dvv * _gelu_grad(zv)).astype(BF16)

        @pl.when(pl.program_id(0) == n_steps - 1)
        def _():
            for gi in range(SGU_GROUPS):
                dw_ref[gi] = jnp.where(mask, dw_ref[gi], 0.0)

    n_steps = S // tm
    vec = pl.BlockSpec((1, BW), lambda i: (0, 0))
    half = lambda c: pl.BlockSpec((tm, BW), lambda i: (i, c))
    wspec = pl.BlockSpec((SGU_GROUPS, SGU_LEN, SGU_LEN), lambda i: (0, 0, 0))
    dz, dg, db, dw, dbs = pl.pallas_call(
        body, name="sgu_bwd", grid=(n_steps,),
        in_specs=[half(C_Z // BW), half(C_Z // BW + 1), half(0), vec, vec, wspec, wspec,
                  pl.BlockSpec((SGU_LEN, SGU_GROUPS), lambda i: (0, 0))],
        out_specs=[pl.BlockSpec((tm, 2 * BW), lambda i: (i, 0)), vec, vec, wspec, wspec],
        out_shape=[jax.ShapeDtypeStruct((S, 2 * BW), BF16), jax.ShapeDtypeStruct((1, BW), F32), jax.ShapeDtypeStruct((1, BW), F32),
                   jax.ShapeDtypeStruct((SGU_GROUPS, SGU_LEN, SGU_LEN), F32), jax.ShapeDtypeStruct((SGU_GROUPS, SGU_LEN, SGU_GD), F32)],
        scratch_shapes=[pltpu.VMEM((tm, BW), F32)],
        compiler_params=_cp("arbitrary"),
    )(p, p, dyb, ln_g.reshape(1, BW), ln_b.reshape(1, BW), w_s, w_st, b_st)
    return dz, dg.reshape(BW), db.reshape(BW), dw, dbs[:, :, 0]


CONV_TC = 128


def _shift_down(y, n):
    rows = lax.broadcasted_iota(jnp.int32, y.shape, 0)
    return jnp.where(rows < n, 0.0, pltpu.roll(y, n, 0))


def _shift_up(y, n):
    rows = lax.broadcasted_iota(jnp.int32, y.shape, 0)
    return jnp.where(rows >= y.shape[0] - n, 0.0, pltpu.roll(y, y.shape[0] - n, 0))


def _conv_specs(S):
    col = lambda c0: pl.BlockSpec((S, CONV_TC), lambda j: (0, c0 // CONV_TC + j))
    return col(C_CB), col(C_CB + BW), col(C_CB + 2 * BW), pl.BlockSpec((3, CONV_TC), lambda j: (0, j)), pl.BlockSpec((S, CONV_TC), lambda j: (0, j))


def _conv_fwd(p, conv_w):
    S = p.shape[0]

    def body(cb_ref, cc_ref, cx_ref, w_ref, o_ref):
        y = cc_ref[...].astype(F32) * cx_ref[...].astype(F32)
        conv = w_ref[0:1, :] * _shift_down(y, 2) + w_ref[1:2, :] * _shift_down(y, 1) + w_ref[2:3, :] * y
        o_ref[...] = (cb_ref[...].astype(F32) * conv).astype(BF16)

    cb, cc, cx, wspec, out = _conv_specs(S)
    return pl.pallas_call(
        body, name="conv_fwd", grid=(BW // CONV_TC,), in_specs=[cb, cc, cx, wspec], out_specs=out,
        out_shape=jax.ShapeDtypeStruct((S, BW), BF16), compiler_params=_cp("parallel"),
    )(p, p, p, conv_w)


def _conv_bwd(p, conv_w, dyc):
    S = p.shape[0]

    def body(cb_ref, cc_ref, cx_ref, w_ref, dy_ref, db_ref, dc_ref, dx_ref, dw_ref):
        cc, cx, dy = cc_ref[...].astype(F32), cx_ref[...].astype(F32), dy_ref[...].astype(F32)
        y = cc * cx
        w0, w1, w2 = w_ref[0:1, :], w_ref[1:2, :], w_ref[2:3, :]
        y1, y2 = _shift_down(y, 1), _shift_down(y, 2)
        conv = w0 * y2 + w1 * y1 + w2 * y
        db_ref[...] = (dy * conv).astype(BF16)
        dconv = dy * cb_ref[...].astype(F32)
        dyy = w2 * dconv + w1 * _shift_up(dconv, 1) + w0 * _shift_up(dconv, 2)
        dc_ref[...] = (dyy * cx).astype(BF16)
        dx_ref[...] = (dyy * cc).astype(BF16)
        dw_ref[0:1, :] = jnp.sum(dconv * y2, axis=0, keepdims=True)
        dw_ref[1:2, :] = jnp.sum(dconv * y1, axis=0, keepdims=True)
        dw_ref[2:3, :] = jnp.sum(dconv * y, axis=0, keepdims=True)

    cb, cc, cx, wspec, out = _conv_specs(S)
    db, dc, dx, dw = pl.pallas_call(
        body, name="conv_bwd", grid=(BW // CONV_TC,), in_specs=[cb, cc, cx, wspec, out],
        out_specs=[out, out, out, wspec],
        out_shape=[jax.ShapeDtypeStruct((S, BW), BF16)] * 3 + [jax.ShapeDtypeStruct((3, BW), F32)],
        compiler_params=_cp("parallel"),
    )(p, p, p, conv_w, dyc)
    return db, dc, dx, dw


def _merge_specs(S, tm):
    gate = lambda n: pl.BlockSpec((tm, D_MODEL), lambda i: (i, C_GATES // D_MODEL + n))
    return [gate(0), gate(1), gate(2)], pl.BlockSpec((3, tm, D_MODEL), lambda i: (0, i, 0)), pl.BlockSpec((tm, D_MODEL), lambda i: (i, 0))


def _branch_merge(br, w_br, p):
    S = p.shape[0]
    tm = _tile(S, (512, 256))

    def body(br_ref, w_ref, g0, g1, g2, bd_ref, o_ref):
        acc = None
        for n, g_ref in enumerate((g0, g1, g2)):
            bdn = jnp.dot(br_ref[n], w_ref[n], preferred_element_type=F32)
            bd_ref[n] = bdn.astype(BF16)
            term = jax.nn.sigmoid(g_ref[...].astype(F32)) * bdn
            acc = term if acc is None else acc + term
        o_ref[...] = acc.astype(BF16)

    gates, bspec, row = _merge_specs(S, tm)
    return pl.pallas_call(
        body, name="mm_branch", grid=(S // tm,),
        in_specs=[pl.BlockSpec((3, tm, BW), lambda i: (0, i, 0)), pl.BlockSpec((3, BW, D_MODEL), lambda i: (0, 0, 0))] + gates,
        out_specs=[bspec, row], out_shape=[jax.ShapeDtypeStruct((3, S, D_MODEL), BF16), jax.ShapeDtypeStruct((S, D_MODEL), BF16)],
        compiler_params=_cp("parallel"),
    )(br, w_br, p, p, p)


def _merge_bwd(p, bd, dm):
    S = p.shape[0]
    tm = _tile(S, (256,))

    def body(g0, g1, g2, b_ref, dm_ref, db_ref, dg_ref):
        dmv = dm_ref[...]
        for n, g_ref in enumerate((g0, g1, g2)):
            sg = jax.nn.sigmoid(g_ref[...].astype(F32))
            db_ref[n] = (dmv * sg).astype(BF16)
            dg_ref[:, n * D_MODEL:(n + 1) * D_MODEL] = (dmv * b_ref[n].astype(F32) * (sg * (1.0 - sg))).astype(BF16)

    gates, bspec, row = _merge_specs(S, tm)
    return pl.pallas_call(
        body, name="merge_bwd", grid=(S // tm,), in_specs=gates + [bspec, row],
        out_specs=[bspec, pl.BlockSpec((tm, 3 * D_MODEL), lambda i: (i, 0))],
        out_shape=[jax.ShapeDtypeStruct((3, S, D_MODEL), BF16), jax.ShapeDtypeStruct((S, 3 * D_MODEL), BF16)],
        compiler_params=_cp("parallel"),
    )(p, p, p, bd, dm)


XA_TM = 512


def _xa_probs(qh, kh):
    s = lax.dot_general(qh, kh, NT, preferred_element_type=F32) * (XA_DH ** -0.5)
    e = jnp.exp(s - jnp.max(s, axis=-1, keepdims=True))
    return e / jnp.sum(e, axis=-1, keepdims=True)


def _xa_fwd(q, kv):
    S = q.shape[0]
    tm = min(XA_TM, S)
    M = kv.shape[1]

    def body(q_ref, kv_ref, o_ref):
        for h in range(XA_HEADS):
            cols = slice(h * XA_DH, (h + 1) * XA_DH)
            pr = _xa_probs(q_ref[:, cols], kv_ref[0, :, cols])
            o_ref[:, cols] = jnp.dot(pr.astype(BF16), kv_ref[1, :, cols], preferred_element_type=F32).astype(BF16)

    row = pl.BlockSpec((tm, D_MODEL), lambda i: (i, 0))
    return pl.pallas_call(
        body, name="xa_fwd", grid=(S // tm,), in_specs=[row, pl.BlockSpec((2, M, D_MODEL), lambda i: (0, 0, 0))], out_specs=row,
        out_shape=jax.ShapeDtypeStruct((S, D_MODEL), BF16), compiler_params=_cp("parallel"),
    )(q, kv)


def _xa_bwd(q, kv, do):
    S = q.shape[0]
    tm = min(XA_TM, S)
    M = kv.shape[1]

    def body(q_ref, kv_ref, do_ref, dq_ref, dkv_ref):
        @pl.when(pl.program_id(0) == 0)
        def _():
            dkv_ref[...] = jnp.zeros_like(dkv_ref)

        for h in range(XA_HEADS):
            cols = slice(h * XA_DH, (h + 1) * XA_DH)
            qh, kh, vh, doh = q_ref[:, cols], kv_ref[0, :, cols], kv_ref[1, :, cols], do_ref[:, cols]
            pr = _xa_probs(qh, kh)
            dkv_ref[1, :, cols] += lax.dot_general(pr.astype(BF16), doh, TN, preferred_element_type=F32)
            dp = lax.dot_general(doh, vh, NT, preferred_element_type=F32)
            ds = (pr * (dp - jnp.sum(dp * pr, axis=-1, keepdims=True)) * (XA_DH ** -0.5)).astype(BF16)
            dq_ref[:, cols] = jnp.dot(ds, kh, preferred_element_type=F32).astype(BF16)
            dkv_ref[0, :, cols] += lax.dot_general(ds, qh, TN, preferred_element_type=F32)

    row = pl.BlockSpec((tm, D_MODEL), lambda i: (i, 0))
    kvs = pl.BlockSpec((2, M, D_MODEL), lambda i: (0, 0, 0))
    return pl.pallas_call(
        body, name="xa_bwd", grid=(S // tm,), in_specs=[row, kvs, row], out_specs=[row, kvs],
        out_shape=[jax.ShapeDtypeStruct((S, D_MODEL), BF16), jax.ShapeDtypeStruct((2, M, D_MODEL), F32)],
        compiler_params=_cp("arbitrary"),
    )(q, kv, do)


def _ffn_up(h3, w_gu, rider=None):
    S = h3.shape[0]
    tm = _tile(S, (1024, 512, 256))

    def body(h_ref, w_ref, ab_ref, hh_ref):
        h = h_ref[...]
        a = lax.dot_general(h, w_ref[0], NT, preferred_element_type=F32)
        b = lax.dot_general(h, w_ref[1], NT, preferred_element_type=F32)
        ab_ref[0] = a.astype(BF16)
        ab_ref[1] = b.astype(BF16)
        hh_ref[...] = (a * jax.nn.sigmoid(a) * b).astype(BF16)

    pair = pl.BlockSpec((None, 2, tm, FFN_SH), lambda j, i: (j, 0, i, 0))
    outs = _call_with_rider(
        rider, body, name="mm_gu", grid=(N_CHIPS, S // tm), args=[h3, w_gu],
        in_specs=[pl.BlockSpec((tm, D_MODEL), lambda j, i: (i, 0)), pl.BlockSpec((None, 2, FFN_SH, D_MODEL), lambda j, i: (j, 0, 0, 0))],
        out_specs=[pair, pl.BlockSpec((None, tm, FFN_SH), lambda j, i: (j, i, 0))],
        out_shape=[jax.ShapeDtypeStruct((N_CHIPS, 2, S, FFN_SH), BF16), jax.ShapeDtypeStruct((N_CHIPS, S, FFN_SH), BF16)],
        scratch_shapes=[], semantics=("parallel", "parallel"))
    return outs[0], outs[1], outs[2:]


def _ffn_down_bwd(dx3, w_dn, ab):
    S = dx3.shape[0]
    tm = _tile(S, (1024, 512, 256))

    def body(dx_ref, w_ref, ab_ref, o_ref):
        d = lax.dot_general(dx_ref[...].astype(BF16), w_ref[...], NT, preferred_element_type=F32)
        a, b = ab_ref[0].astype(F32), ab_ref[1].astype(F32)
        sg = jax.nn.sigmoid(a)
        o_ref[0] = (d * b * (sg * (1.0 + a * (1.0 - sg)))).astype(BF16)
        o_ref[1] = (d * (a * sg)).astype(BF16)

    pair = pl.BlockSpec((None, 2, tm, FFN_SH), lambda j, i: (j, 0, i, 0))
    return pl.pallas_call(
        body, name="mm_down_dx", grid=(N_CHIPS, S // tm),
        in_specs=[pl.BlockSpec((tm, D_MODEL), lambda j, i: (i, 0)), pl.BlockSpec((None, FFN_SH, D_MODEL), lambda j, i: (j, 0, 0)), pair],
        out_specs=pair, out_shape=jax.ShapeDtypeStruct(ab.shape, BF16), compiler_params=_cp("parallel", "parallel"),
    )(dx3, w_dn, ab)


def _reduce_adam(parts, w, m, v, *, name):
    shape = w.shape
    C = shape[-1]
    R = math.prod(shape[:-1])
    tm = _rows(R, 4 * C)
    n = len(parts)
    c1, c2 = 1.0 - ADAM_B1 ** ADAM_STEP, 1.0 - ADAM_B2 ** ADAM_STEP

    def body(*refs):
        g = refs[0][...]
        for r in refs[1:n]:
            g = g + r[...]
        w_ref, m_ref, v_ref, go, do, mo, vo = refs[n:]
        mn = ADAM_B1 * m_ref[...] + (1.0 - ADAM_B1) * g
        vn = ADAM_B2 * v_ref[...] + (1.0 - ADAM_B2) * (g * g)
        go[...] = g
        do[...] = -ADAM_LR * ((mn / c1) / (jnp.sqrt(vn / c2) + ADAM_EPS) + ADAM_WD * w_ref[...])
        mo[...] = mn
        vo[...] = vn

    row = pl.BlockSpec((tm, C), lambda i: (i, 0))
    outs = pl.pallas_call(
        body, name=name, grid=(R // tm,), in_specs=[row] * (n + 3), out_specs=[row] * 4,
        out_shape=[jax.ShapeDtypeStruct((R, C), F32)] * 4, compiler_params=_cp("parallel"),
    )(*[a.reshape(R, C) for a in (*parts, w, m, v)])
    return tuple(o.reshape(shape) for o in outs)


_VIEW = {
    "w_in": ((1024, 7168), (1024, 1792), 256, lambda i, b: (i, b)),
    "w_br": ((1536, 1024), (1536, 256), 512, lambda i, b: (i, b)),
    "w_sq": ((5120, 1024), (1280, 1024), 256, lambda i, b: (4 * i + b, 0)),
    "w_gu": ((5632, 1024), (1408, 1024), 352, lambda i, b: (4 * b + i, 0)),
    "w_dn": ((2816, 1024), (704, 1024), 352, lambda i, b: (2 * b + i, 0)),
    "conv_w": ((8, 512), (8, 128), 8, lambda i, b: (0, b)),
}


def _scalar(v):
    return jnp.asarray(v, jnp.int32).reshape(1)


def _place(name, local, b):
    full2, sh2, tm, idx = _VIEW[name]
    C = sh2[1]
    dt = local.dtype if name == "conv_w" else BF16

    def body(b_ref, x_ref, o0_ref, o1_ref):
        o0_ref[...] = x_ref[0].astype(dt)
        o1_ref[...] = x_ref[1].astype(dt)

    place = pl.BlockSpec((tm, C), lambda i, bs: idx(i, bs[0]))
    outs = pl.pallas_call(
        body, name="place_" + name,
        grid_spec=pltpu.PrefetchScalarGridSpec(num_scalar_prefetch=1, grid=(sh2[0] // tm,),
                                               in_specs=[pl.BlockSpec((DEPTH, tm, C), lambda i, bs: (0, i, 0))], out_specs=[place, place]),
        out_shape=[jax.ShapeDtypeStruct(full2, dt)] * 2, compiler_params=_cp("arbitrary"),
    )(_scalar(b), local.reshape((DEPTH,) + sh2))
    return [o.reshape(_FULL_SHAPE[name]) for o in outs]


def _add_owner(name, g, land, own):
    shape = g.shape
    C = shape[-1]
    R = math.prod(shape[:-1])
    tm = _rows(R, 4 * C)

    def body(s_ref, g_ref, l_ref, o_ref):
        @pl.when(s_ref[0] != 0)
        def _():
            o_ref[...] = (g_ref[...].astype(F32) + l_ref[...].astype(F32)).astype(BF16)

        @pl.when(s_ref[0] == 0)
        def _():
            o_ref[...] = jnp.zeros_like(o_ref)

    pick = pl.BlockSpec((tm, C), lambda i, s: (jnp.where(s[0] != 0, i, 0), 0))
    return pl.pallas_call(
        body, name="presum_" + name,
        grid_spec=pltpu.PrefetchScalarGridSpec(num_scalar_prefetch=1, grid=(R // tm,), in_specs=[pick, pick],
                                               out_specs=pl.BlockSpec((tm, C), lambda i, s: (i, 0))),
        out_shape=jax.ShapeDtypeStruct((R, C), BF16), compiler_params=_cp("arbitrary"),
    )(_scalar(own), g.reshape(R, C), land.reshape(R, C)).reshape(shape)


def _sum_chips(name, slots, part, b, own):
    full2, sh2, tm, idx = _VIEW[name]
    C = sh2[1]

    def body(s_ref, slot_ref, own_ref, o_ref):
        @pl.when(s_ref[1] != 0)
        def _():
            o_ref[...] = ((slot_ref[0].astype(F32) + slot_ref[1].astype(F32)) + slot_ref[2].astype(F32)) + own_ref[...].astype(F32)

        @pl.when(s_ref[1] == 0)
        def _():
            o_ref[...] = jnp.zeros_like(o_ref)

    return pl.pallas_call(
        body, name="sum_chips_" + name,
        grid_spec=pltpu.PrefetchScalarGridSpec(
            num_scalar_prefetch=1, grid=(sh2[0] // tm,),
            in_specs=[pl.BlockSpec((3, tm, C), lambda i, s: (0, jnp.where(s[1] != 0, i, 0), 0)),
                      pl.BlockSpec((tm, C), lambda i, s: idx(jnp.where(s[1] != 0, i, 0), s[0]))],
            out_specs=pl.BlockSpec((tm, C), lambda i, s: (i, 0))),
        out_shape=jax.ShapeDtypeStruct(sh2, F32), compiler_params=_cp("arbitrary"),
    )(jnp.stack([jnp.asarray(b, jnp.int32), jnp.asarray(own, jnp.int32)]), slots.reshape((3,) + sh2),
      part.reshape(full2)).reshape(_SHARD_SHAPE[name])


def _adam_layers(mine, theirs, c, w, m, v, *, name):
    shape = w.shape
    C = shape[-1]
    R = math.prod(shape[1:-1])
    tm = _rows(R, 4 * C)
    c1, c2 = 1.0 - ADAM_B1 ** ADAM_STEP, 1.0 - ADAM_B2 ** ADAM_STEP

    def body(c_ref, m0_ref, m1_ref, t_ref, w_ref, m_ref, v_ref, go, do, mo, vo):
        layer = pl.program_id(0)
        g = jnp.where(layer == c_ref[0], jnp.where(layer == 0, m0_ref[...], m1_ref[...]), t_ref[...])
        mn = ADAM_B1 * m_ref[...] + (1.0 - ADAM_B1) * g
        vn = ADAM_B2 * v_ref[...] + (1.0 - ADAM_B2) * (g * g)
        go[...] = g
        do[...] = -ADAM_LR * ((mn / c1) / (jnp.sqrt(vn / c2) + ADAM_EPS) + ADAM_WD * w_ref[...])
        mo[...] = mn
        vo[...] = vn

    def own(layer):
        return pl.BlockSpec((tm, C), lambda l, i, cs: (jnp.where((l == layer) & (cs[0] == layer), i, 0), 0))

    recv = pl.BlockSpec((tm, C), lambda l, i, cs: (jnp.where(l == cs[0], 0, i), 0))
    row = pl.BlockSpec((None, tm, C), lambda l, i, cs: (l, i, 0))
    outs = pl.pallas_call(
        body, name=name,
        grid_spec=pltpu.PrefetchScalarGridSpec(num_scalar_prefetch=1, grid=(DEPTH, R // tm),
                                               in_specs=[own(0), own(1), recv, row, row, row], out_specs=[row] * 4),
        out_shape=[jax.ShapeDtypeStruct((DEPTH, R, C), F32)] * 4, compiler_params=_cp("arbitrary", "arbitrary"),
    )(_scalar(c), mine[0].reshape(R, C), mine[1].reshape(R, C), theirs.reshape(R, C), *[t.reshape(DEPTH, R, C) for t in (w, m, v)])
    return tuple(o.reshape(shape) for o in outs)


def _take_weights(wl, names, landed):
    for n, t in zip(names, landed):
        wl[n] = t[:3] if n == "conv_w" else t


_GATHER_LATE = ["w_br", "conv_w", "w_sq"]
_GATHER_LAST = ["w_gu", "w_dn"]


def _layer_fwd(x, mem, wl, ride=None, h1=None, next_gain=None):
    S = x.shape[0]
    sv = {"x": x}
    wl = dict(wl)
    if h1 is None:
        h1 = _rms_fwd(x, wl["norm_mix_g"], name="rms_mix")
    if ride is None:
        ride = {"mm_in": None, "sb": None, "mm_gu": None}
        p = _mm(h1, wl["w_in"], mode="nn", out_dtype=BF16, name="mm_in")
    else:
        p, landed = _mm(h1, wl["w_in"], mode="nn", out_dtype=BF16, name="mm_in", rider=ride["mm_in"])
        _take_weights(wl, _GATHER_LATE, landed)
    (ya, tot, cnt), landed = _sb2_fwd(p, ride["sb"])
    _take_weights(wl, _GATHER_LAST, landed)
    b_st = wl["b_spatial"].T
    yb = _sgu_fwd(p, wl["sgu_ln_g"], wl["sgu_ln_b"], wl["w_spatial"], b_st)
    yc = _conv_fwd(p, wl["conv_w"])
    br = jnp.stack([ya, yb, yc])
    bd, merged = _branch_merge(br, wl["w_br"], p)
    x1, h2 = _mm(merged, wl["w_sq"][0], mode="nn", res=x, norm=wl["norm_xa_g"], name="mm_out")
    qx = _mm(h2, wl["w_sq"][1], mode="nn", out_dtype=BF16, name="mm_q")
    mn = _rms_fwd(mem, wl["mem_norm_g"], name="rms_mem")
    kv = _mm(mn, wl["w_sq"][3:5], mode="nn", b_kind="batch", out_dtype=BF16, name="mm_kv")
    o = _xa_fwd(qx, kv)
    x2, h3 = _mm(o, wl["w_sq"][2], mode="nn", res=x1, norm=wl["norm_ffn_g"], name="mm_o")
    ab, hh, rode = _ffn_up(h3, wl["w_gu"], ride["mm_gu"])
    x3 = _mm(hh, wl["w_dn"], mode="nn", a_kind="kchunk", b_kind="kchunk", res=x2, norm=next_gain, name="mm_down")
    x3, h_next = x3 if next_gain is not None else (x3, None)
    sv.update(h1=h1, p=p, tot=tot, cnt=cnt, br=br, bd=bd, merged=merged, x1=x1, h2=h2, qx=qx, mn=mn, kv=kv, o=o,
              x2=x2, h3=h3, ab=ab, hh=hh, b_st=b_st, wl=wl)
    return x3, sv, rode, h_next


class _GradPipe:
    def __init__(self, ci, bi):
        self.ci, self.bi, self.queue = ci, bi, []
        self.part, self.slots = [dict(), dict()], [dict(), dict()]

    def to_owner(self, layer, names, g):
        def arrived(land):
            own = (self.ci == layer).astype(jnp.int32)
            part = {n: _add_owner(n, g[n], t, own) for n, t in zip(names, land)}
            self.part[layer].update(part)
            self.queue.append((layer, names, part))

        return _presum_rider(layer, {n: g[n] for n in names}), arrived

    def exchange(self):
        if not self.queue:
            return None
        layer, names, part = self.queue.pop(0)
        return _shard_rider(layer, part), lambda slots: self.slots[layer].update(zip(names, slots))

    def drain(self, also):
        landed = None
        while self.queue or landed is None:
            job = self.exchange()
            both = _join([job[0] if job else None, also if landed is None else None])
            parts = both.split(_run_rider(both, name="grad_exchange_last"))
            if job:
                job[1](parts[0])
            if landed is None:
                landed = parts[-1]
        return landed

    def reduced(self, layer):
        own = (self.ci == layer).astype(jnp.int32)
        return {n: _sum_chips(n, self.slots[layer][n], self.part[layer][n], self.bi, own) for n in _BIG}


def _layer_bwd(dx3, mem, sv, layer=None, pipe=None):
    S = dx3.shape[0]
    p, wl = sv["p"], sv["wl"]
    g = {}

    def carrying(jobs, fn):
        jobs = [j for j in jobs if j]
        joined = _join([j[0] for j in jobs])
        out, landed = fn(joined)
        if joined is not None:
            for j, part in zip(jobs, joined.split(landed)):
                j[1](part)
        return out

    def mm(*args, job=None, jobs=(), **kw):
        return carrying([job, *jobs], lambda r: (_mm(*args, **kw), None) if r is None else _mm(*args, **kw, rider=r))

    to_owner = (lambda names: pipe.to_owner(layer, names, g)) if pipe else (lambda names: None)
    exchange = pipe.exchange if pipe else (lambda: None)

    g["w_dn"] = _mm(sv["hh"], dx3, mode="tn", a_kind="batch", out_dtype=BF16, name="mm_down_dw")
    dab = _ffn_down_bwd(dx3, wl["w_dn"], sv["ab"]).reshape(2 * N_CHIPS, S, FFN_SH)
    g["w_gu"] = mm(dab, sv["h3"], mode="tn", a_kind="batch", out_dtype=BF16, name="mm_gu_dw", job=exchange()).reshape(_FULL_SHAPE["w_gu"])
    dx2, g["norm_ffn_g"] = mm(dab, wl["w_gu"].reshape(2 * N_CHIPS, FFN_SH, D_MODEL), mode="nn", a_kind="kchunk", b_kind="kchunk", name="mm_gu_dx",
                               rms_bwd=(sv["x2"], wl["norm_ffn_g"], dx3), job=to_owner(["w_dn", "w_gu"]))
    do = _mm(dx2, wl["w_sq"][2], mode="nt", out_dtype=BF16, name="mm_o_dx")
    dw_o = _mm(sv["o"], dx2, mode="tn", out_dtype=BF16, name="mm_o_dw")
    dq, dkv = _xa_bwd(sv["qx"], sv["kv"], do)
    dw_q = _mm(sv["h2"], dq, mode="tn", out_dtype=BF16, name="mm_q_dw")
    dx1, g["norm_xa_g"] = _mm(dq, wl["w_sq"][1], mode="nt", name="mm_q_dx", rms_bwd=(sv["x1"], wl["norm_xa_g"], dx2))
    dw_kv = _mm(sv["mn"], dkv, mode="tn", b_kind="batch", out_dtype=BF16, name="mm_kv_dw")
    dmn = _mm(dkv, wl["w_sq"][3:5], mode="nt", a_kind="kchunk", b_kind="kchunk", name="mm_kv_dx")
    _, g["mem_norm_g"] = _rms_bwd(mem, wl["mem_norm_g"], dmn, jnp.zeros_like(mem), name="rms_mem_bwd")
    dm = _mm(dx1, wl["w_sq"][0], mode="nt", name="mm_out_dx")
    dw_out = _mm(sv["merged"], dx1, mode="tn", out_dtype=BF16, name="mm_out_dw")
    g["w_sq"] = jnp.concatenate([jnp.stack([dw_out, dw_q, dw_o]), dw_kv])
    dbd, dgates = _merge_bwd(p, sv["bd"], dm)
    dbr = mm(dbd, wl["w_br"], mode="nt", a_kind="batch", b_kind="batch", name="mm_branch_dx", job=to_owner(["w_sq"]))
    g["w_br"] = _mm(sv["br"], dbd, mode="tn", a_kind="batch", b_kind="batch", out_dtype=BF16, name="mm_branch_dw")
    dq, dk, dv = carrying([exchange(), to_owner(["w_br"])], lambda r: _sb2_bwd(p, dbr, sv["tot"], sv["cnt"], r))
    dz, g["sgu_ln_g"], g["sgu_ln_b"], g["w_spatial"], g["b_spatial"] = _sgu_bwd(
        p, dbr[1], wl["sgu_ln_g"], wl["sgu_ln_b"], wl["w_spatial"], wl["w_spatial"].transpose(0, 2, 1), sv["b_st"])
    dcb, dcc, dcx, g["conv_w"] = _conv_bwd(p, wl["conv_w"], dbr[2])
    dp = jnp.concatenate([dq, dk.astype(BF16), dv.astype(BF16), dz, dcb, dcc, dcx, dgates], axis=1)
    g["w_in"] = mm(sv["h1"], dp, mode="tn", out_dtype=BF16, name="mm_in_dw", jobs=[exchange(), exchange()])
    dx, g["norm_mix_g"] = mm(dp, wl["w_in"], mode="nt", name="mm_in_dx", rms_bwd=(sv["x"], wl["norm_mix_g"], dx1),
                             job=to_owner(["w_in"]))
    return dx, g


def _local_step(x, mem, target, layers, final_g):
    h, saved = x, []
    for wl in layers:
        h, sv, _, _ = _layer_fwd(h, mem, wl)
        saved.append(sv)
    loss, dx, d_final = _loss_head(h, final_g, target)
    grads = [None] * len(layers)
    for l in reversed(range(len(layers))):
        dx, grads[l] = _layer_bwd(dx, mem, saved[l])
    return loss, dx, grads, d_final


_ALL = slice(None)
CONV_ROWS = 8
_SHARD = {
    "w_in": lambda b: (_ALL, pl.ds(1792 * b, 1792)),
    "w_br": lambda b: (_ALL, _ALL, pl.ds(256 * b, 256)),
    "w_sq": lambda b: (_ALL, pl.ds(256 * b, 256), _ALL),
    "w_gu": lambda b: (b,),
    "w_dn": lambda b: (b,),
    "conv_w": lambda b: (_ALL, pl.ds(128 * b, 128)),
}
_FULL_SHAPE = {"w_in": (1024, 7168), "w_br": (3, 512, 1024), "w_sq": (5, 1024, 1024), "w_gu": (4, 2, 704, 1024),
               "w_dn": (4, 704, 1024), "conv_w": (CONV_ROWS, 512)}
_SHARD_SHAPE = {"w_in": (1024, 1792), "w_br": (3, 512, 256), "w_sq": (5, 256, 1024), "w_gu": (2, 704, 1024),
                "w_dn": (704, 1024), "conv_w": (CONV_ROWS, 128)}


def _pos():
    return lax.axis_index("x"), lax.axis_index("y"), lax.axis_index("c")


def _per_chip(fn):
    x, y, _ = _pos()
    for x0 in (0, 1):
        for y0 in (0, 1):
            @pl.when((x == x0) & (y == y0))
            def _():
                fn(x0, y0)


def _other_chips(x0, y0):
    return [(1 - x0, y0), (x0, 1 - y0), (1 - x0, 1 - y0)]


def _rcopy(src, dst, ssem, rsem, dev):
    return pltpu.make_async_remote_copy(src_ref=src, dst_ref=dst, send_sem=ssem, recv_sem=rsem, device_id=dev, device_id_type=MESH)


def _dma_sems(n):
    return pltpu.SemaphoreType.DMA((n,))


def _gather_rider(layer, placed):
    names = list(placed)
    n = len(names)
    shard = lambda refs, a, b: refs[a].at[_SHARD[names[a]](b)]

    def start(ins, outs, send, recv):
        @pl.when(lax.axis_index("c") == layer)
        def _():
            def run(x0, y0):
                for kk, (px, py) in enumerate(_other_chips(x0, y0)):
                    for a in range(n):
                        own = shard(outs, a, 2 * x0 + y0)
                        _rcopy(own, own, send.at[6 * a + kk], recv.at[6 * a + kk], (px, py, layer)).start()

            _per_chip(run)

    def passing(outs, send, recv, a, kk, bp, x0, y0):
        landed = shard(outs, a, bp)
        return _rcopy(landed, landed, send.at[6 * a + 3 + kk], recv.at[6 * a + 3 + kk], (x0, y0, 1 - layer))

    def middle(ins, outs, send, recv):
        @pl.when(lax.axis_index("c") == layer)
        def _():
            def run(x0, y0):
                for kk, (px, py) in enumerate(_other_chips(x0, y0)):
                    for a in range(n):
                        landed = shard(outs, a, 2 * px + py)
                        _rcopy(landed, landed, send.at[6 * a + kk], recv.at[6 * a + kk], (px, py, layer)).wait_recv()
                        passing(outs, send, recv, a, kk, 2 * px + py, x0, y0).start()

            _per_chip(run)

    def finish(ins, outs, send, recv):
        c = lax.axis_index("c")

        def run(x0, y0):
            chips = _other_chips(x0, y0)

            @pl.when(c == layer)
            def _():
                for kk, (px, py) in enumerate(chips):
                    for a in range(n):
                        own = shard(outs, a, 2 * x0 + y0)
                        _rcopy(own, own, send.at[6 * a + kk], recv.at[6 * a + kk], (px, py, layer)).wait_send()
                        passing(outs, send, recv, a, kk, 2 * px + py, x0, y0).wait_send()

            @pl.when(c != layer)
            def _():
                for kk, (px, py) in enumerate(chips):
                    for a in range(n):
                        got = shard(outs, a, 2 * px + py)
                        _rcopy(got, got, send.at[6 * a + 3 + kk], recv.at[6 * a + 3 + kk], (x0, y0, layer)).wait_recv()

        _per_chip(run)

    arrs = [placed[nm] for nm in names]
    return _Rider(arrs, [jax.ShapeDtypeStruct(t.shape, t.dtype) for t in arrs], 6 * n, start, finish, alias={a: a for a in range(n)}, middle=middle)


def _presum_rider(layer, grads):
    names = list(grads)
    n = len(names)

    def start(ins, outs, send, recv):
        x, y, c = _pos()

        @pl.when(c != layer)
        def _():
            for a in range(n):
                _rcopy(ins[a], outs[a], send.at[a], recv.at[a], (x, y, layer)).start()

    def finish(ins, outs, send, recv):
        x, y, c = _pos()

        @pl.when(c != layer)
        def _():
            for a in range(n):
                _rcopy(ins[a], outs[a], send.at[a], recv.at[a], (x, y, layer)).wait_send()

        @pl.when(c == layer)
        def _():
            for a in range(n):
                _rcopy(outs[a], outs[a], send.at[a], recv.at[a], (x, y, 1 - layer)).wait_recv()

    arrs = [grads[nm] for nm in names]
    return _Rider(arrs, [jax.ShapeDtypeStruct(t.shape, t.dtype) for t in arrs], n, start, finish)


def _shard_rider(layer, part):
    names = list(part)
    n = len(names)

    def each(fn):
        @pl.when(lax.axis_index("c") == layer)
        def _():
            def run(x0, y0):
                for kk, (px, py) in enumerate(_other_chips(x0, y0)):
                    for a in range(n):
                        fn(a, kk, 2 * px + py, (px, py, layer))

            _per_chip(run)

    def start(ins, outs, send, recv):
        each(lambda a, kk, bp, peer: _rcopy(ins[a].at[_SHARD[names[a]](bp)], outs[a].at[kk], send.at[3 * a + kk], recv.at[3 * a + kk], peer).start())

    def finish(ins, outs, send, recv):
        each(lambda a, kk, bp, peer: _rcopy(outs[a].at[kk], outs[a].at[kk], send.at[3 * a + kk], recv.at[3 * a + kk], peer).wait_recv())
        each(lambda a, kk, bp, peer: _rcopy(ins[a].at[_SHARD[names[a]](bp)], outs[a].at[kk], send.at[3 * a + kk], recv.at[3 * a + kk], peer).wait_send())

    return _Rider([part[nm] for nm in names], [jax.ShapeDtypeStruct((N_CHIPS - 1,) + _SHARD_SHAPE[nm], part[nm].dtype) for nm in names],
                  3 * n, start, finish)


def _sibling_exchange(mine0, mine1):
    names = list(mine0)
    n = len(names)

    def body(*refs):
        l0, l1, outs = refs[:n], refs[n:2 * n], refs[2 * n:3 * n]
        send, recv = refs[3 * n:]
        x, y, c = _pos()
        for c0 in (0, 1):
            @pl.when(c == c0)
            def _():
                srcs = l0 if c0 == 0 else l1
                cps = [_rcopy(srcs[a], outs[a], send.at[a], recv.at[a], (x, y, 1 - c0)) for a in range(n)]
                for cp in cps:
                    cp.start()
                for cp in cps:
                    cp.wait()

    outs = pl.pallas_call(
        body, name="grad_sibling_exchange", in_specs=[ANY] * (2 * n), out_specs=[ANY] * n,
        out_shape=[jax.ShapeDtypeStruct(mine0[nm].shape, mine0[nm].dtype) for nm in names],
        scratch_shapes=[_dma_sems(n), _dma_sems(n)],
    )(*[mine0[nm] for nm in names], *[mine1[nm] for nm in names])
    return dict(zip(names, outs))


def _small_rider(pack):
    flips = [(fx, fy, fc) for fx in (0, 1) for fy in (0, 1) for fc in (0, 1) if fx or fy or fc]

    def peers():
        x, y, c = _pos()
        return 4 * x + 2 * y + c, [(x ^ fx, y ^ fy, c ^ fc) for fx, fy, fc in flips]

    def start(ins, outs, send, recv):
        me, to = peers()
        for k, peer in enumerate(to):
            _rcopy(ins[0], outs[0].at[me], send.at[k], recv.at[k], peer).start()

    def finish(ins, outs, send, recv):
        me, to = peers()
        for k, (px, py, pc) in enumerate(to):
            slot = outs[0].at[4 * px + 2 * py + pc]
            _rcopy(slot, slot, send.at[k], recv.at[k], (px, py, pc)).wait_recv()
        for k, peer in enumerate(to):
            _rcopy(ins[0], outs[0].at[me], send.at[k], recv.at[k], peer).wait_send()

    return _Rider([pack], [jax.ShapeDtypeStruct((8,) + pack.shape, pack.dtype)], len(flips), start, finish)


def _sum_devices(gathered, pack, me):
    n, R, C = gathered.shape

    def body(me_ref, r_ref, own_ref, o_ref):
        acc = jnp.where(me_ref[0] == 0, own_ref[...], r_ref[0])
        for s in range(1, n):
            acc = acc + jnp.where(me_ref[0] == s, own_ref[...], r_ref[s])
        o_ref[...] = acc

    return pl.pallas_call(
        body, name="sum_devices_small",
        grid_spec=pltpu.PrefetchScalarGridSpec(num_scalar_prefetch=1, grid=(1,),
                                               in_specs=[pl.BlockSpec((n, R, C), lambda i, m: (0, 0, 0)), pl.BlockSpec((R, C), lambda i, m: (0, 0))],
                                               out_specs=pl.BlockSpec((R, C), lambda i, m: (0, 0))),
        out_shape=jax.ShapeDtypeStruct((R, C), F32), compiler_params=_cp("arbitrary"),
    )(_scalar(me), gathered, pack)


_WEIGHTS = ["norm_mix_g", "w_in", "sgu_ln_g", "sgu_ln_b", "w_spatial", "b_spatial", "conv_w", "w_branch", "w_out", "norm_xa_g",
            "mem_norm_g", "w_q_xa", "w_k_xa", "w_v_xa", "w_o_xa", "norm_ffn_g", "w_gate_ffn", "w_up_ffn", "w_down_ffn", "final_g"]
_REPLICATED = ["norm_mix_g", "sgu_ln_g", "sgu_ln_b", "w_spatial", "b_spatial", "norm_xa_g", "mem_norm_g", "norm_ffn_g", "final_g"]
_SQUARE = ["w_out", "w_q_xa", "w_o_xa", "w_k_xa", "w_v_xa"]
_BIG = ["w_in", "w_br", "w_sq", "w_gu", "w_dn"]


def _pack(arrs):
    return jnp.concatenate([a.reshape(-1) for a in arrs]).reshape(-1, 128)


def _step(a):
    w = {n: a[n] for n in _WEIGHTS}
    x, mem, target = a["x"][0], a["mem"][0], a["loss_target"][0]
    xi, yi, ci = _pos()
    bi = 2 * xi + yi
    groups = list(_FULL_SHAPE)

    tr = lambda t: jnp.swapaxes(t, 1, 2)
    local = {"w_in": w["w_in"], "w_br": w["w_branch"], "w_sq": jnp.stack([w[n] for n in _SQUARE], axis=1),
             "w_gu": jnp.stack([tr(w["w_gate_ffn"]), tr(w["w_up_ffn"])], axis=1), "w_dn": w["w_down_ffn"],
             "conv_w": jnp.pad(w["conv_w"], ((0, 0), (0, CONV_ROWS - 3), (0, 0)))}
    placed = [dict(), dict()]
    for n in groups:
        placed[0][n], placed[1][n] = _place(n, local[n], bi)

    def gather(l, names):
        return _gather_rider(l, {n: placed[l][n] for n in names})

    def start_of(l, w_in):
        return {"w_in": w_in, **{n: w[n][l] for n in _REPLICATED if n != "final_g"}}

    w_in0, = _run_rider(gather(0, ["w_in"]), name="gather_first")
    h, sv0, (w_in1,), h1 = _layer_fwd(x, mem, start_of(0, w_in0), {"mm_in": gather(0, _GATHER_LATE), "sb": gather(0, _GATHER_LAST),
                                                                    "mm_gu": gather(1, ["w_in"])}, next_gain=w["norm_mix_g"][1])
    h, sv1, _, _ = _layer_fwd(h, mem, start_of(1, w_in1), {"mm_in": gather(1, _GATHER_LATE), "sb": gather(1, _GATHER_LAST), "mm_gu": None}, h1=h1)
    loss, dx, d_final = _loss_head(h, w["final_g"], target)
    loss = lax.psum(loss, ("x", "y", "c"))

    pipe = _GradPipe(ci, bi)
    dx, g1 = _layer_bwd(dx, mem, sv1, 1, pipe)
    dx, g0 = _layer_bwd(dx, mem, sv0, 0, pipe)
    grads = [g0, g1]
    small = {n: jnp.stack([g[n] for g in grads]) for n in _REPLICATED if n != "final_g"}
    small["final_g"] = d_final
    conv_g = jnp.stack([g["conv_w"] for g in grads])
    small_pack = _pack([small[n] for n in _REPLICATED] + [conv_g])
    gathered, = pipe.drain(_small_rider(small_pack))
    mine = [pipe.reduced(0), pipe.reduced(1)]
    theirs = _sibling_exchange(mine[0], mine[1])

    out = {}

    def adam_layers(name, group, pick=None, view=lambda t: t):
        sel = (lambda t: t[group]) if pick is None else (lambda t: t[group][pick])
        res = _adam_layers([sel(mine[0]), sel(mine[1])], sel(theirs), ci, view(w[name]), view(a["m_" + name]), view(a["v_" + name]), name="adam_" + name)
        out[name] = tuple(view(r) for r in res)

    adam_layers("w_in", "w_in")
    adam_layers("w_branch", "w_br")
    for t, n in enumerate(_SQUARE):
        adam_layers(n, "w_sq", t)
    adam_layers("w_gate_ffn", "w_gu", 0, tr)
    adam_layers("w_up_ffn", "w_gu", 1, tr)
    adam_layers("w_down_ffn", "w_dn")

    def adam(name, g):
        out[name] = _reduce_adam([g], w[name], a["m_" + name], a["v_" + name], name="adam_" + name)

    n_rep = sum(w[n].size for n in _REPLICATED) // 128
    summed = _sum_devices(gathered, small_pack, 4 * xi + 2 * yi + ci)
    res = _reduce_adam([summed[:n_rep]], _pack([w[n] for n in _REPLICATED]), _pack([a["m_" + n] for n in _REPLICATED]),
                       _pack([a["v_" + n] for n in _REPLICATED]), name="adam_replicated")
    off = 0
    for n in _REPLICATED:
        out[n] = tuple(r.reshape(-1)[off:off + w[n].size].reshape(w[n].shape) for r in res)
        off += w[n].size
    conv_full = summed[n_rep:].reshape(conv_g.shape)
    adam("conv_w", lax.dynamic_slice_in_dim(conv_full, (2 * xi + yi) * 128, 128, axis=2))

    return (loss, dx[None], *[out[n][k] for k in range(4) for n in _WEIGHTS])


def kernel(x, mem, norm_mix_g, w_in, sgu_ln_g, sgu_ln_b, w_spatial, b_spatial, conv_w, w_branch, w_out, norm_xa_g, mem_norm_g, w_q_xa, w_k_xa, w_v_xa, w_o_xa, norm_ffn_g, w_gate_ffn, w_up_ffn, w_down_ffn, final_g, loss_target, m_norm_mix_g, m_w_in, m_sgu_ln_g, m_sgu_ln_b, m_w_spatial, m_b_spatial, m_conv_w, m_w_branch, m_w_out, m_norm_xa_g, m_mem_norm_g, m_w_q_xa, m_w_k_xa, m_w_v_xa, m_w_o_xa, m_norm_ffn_g, m_w_gate_ffn, m_w_up_ffn, m_w_down_ffn, m_final_g, v_norm_mix_g, v_w_in, v_sgu_ln_g, v_sgu_ln_b, v_w_spatial, v_b_spatial, v_conv_w, v_w_branch, v_w_out, v_norm_xa_g, v_mem_norm_g, v_w_q_xa, v_w_k_xa, v_w_v_xa, v_w_o_xa, v_norm_ffn_g, v_w_gate_ffn, v_w_up_ffn, v_w_down_ffn, v_final_g):
    return _step(dict(locals()))
```

```python
import math

import jax
import jax.numpy as jnp
from jax import lax
from jax.experimental import pallas as pl
from jax.experimental.pallas import tpu as pltpu

F32, BF16 = jnp.float32, jnp.bfloat16
MESH = pl.DeviceIdType.MESH
ANY = pl.BlockSpec(memory_space=pl.ANY)

D_MODEL = 1024
DEPTH = 2
BW = 512
SB_HEADS, SB_DH = 8, 64
SGU_LEN, SGU_GROUPS, SGU_GD, SGU_CHUNK = 128, 4, 128, 64
XA_HEADS, XA_DH = 4, 256
FFN_SH = 704
N_CHIPS = 4
C_Z, C_CB, C_GATES = 1536, 2560, 4096

ADAM_LR, ADAM_B1, ADAM_B2, ADAM_EPS, ADAM_WD, ADAM_STEP = 0.001, 0.9, 0.999, 1e-08, 0.01, 10

VMEM_LIMIT_V7X = 56 * 1024 * 1024

NN = (((1,), (0,)), ((), ()))
NT = (((1,), (1,)), ((), ()))
TN = (((0,), (0,)), ((), ()))


def _cp(*sem):
    return pltpu.CompilerParams(dimension_semantics=sem, vmem_limit_bytes=VMEM_LIMIT_V7X)


def _tile(n, pref):
    for t in pref:
        if n % t == 0:
            return t
    return n


def _rows(r, row_bytes, block_bytes=1 << 20):
    fits = [t for t in range(8, r + 1, 8) if r % t == 0 and t * row_bytes <= block_bytes]
    return max(fits) if fits else r


class _Rider:
    def __init__(self, ins, outs, n_sems, start, finish, alias=None, middle=None):
        self.ins, self.outs, self.n_sems, self.alias = list(ins), list(outs), n_sems, alias or {}
        self.start, self.middle, self.finish = start, middle, finish


class _Sems:
    def __init__(self, ref, first):
        self.ref, self.first = ref, first

    @property
    def at(self):
        return self

    def __getitem__(self, k):
        return self.ref.at[self.first + k]


def _join(riders):
    riders = [r for r in riders if r is not None]
    if not riders:
        return None
    spans, i0, o0, s0 = [], 0, 0, 0
    for r in riders:
        spans.append((r, i0, o0, s0))
        i0, o0, s0 = i0 + len(r.ins), o0 + len(r.outs), s0 + r.n_sems

    def phase(which):
        def run(ins, outs, send, recv):
            for r, i, o, s in spans:
                fn = getattr(r, which)
                if fn is not None:
                    fn(ins[i:i + len(r.ins)], outs[o:o + len(r.outs)], _Sems(send, s), _Sems(recv, s))
        return run

    joined = _Rider([a for r in riders for a in r.ins], [a for r in riders for a in r.outs], s0, phase("start"), phase("finish"),
                    alias={o + k: i + v for r, i, o, s in spans for k, v in r.alias.items()}, middle=phase("middle"))
    joined.split = lambda landed: [list(landed[o:o + len(r.outs)]) for r, i, o, s in spans]
    return joined


def _call_with_rider(rider, body, *, name, grid, in_specs, args, out_specs, out_shape, scratch_shapes, semantics):
    if rider is None:
        return pl.pallas_call(body, name=name, grid=grid, in_specs=in_specs, out_specs=out_specs, out_shape=out_shape,
                              scratch_shapes=scratch_shapes, compiler_params=_cp(*semantics))(*args)
    n_in, n_out, r_in, r_out = len(args), len(out_shape), len(rider.ins), len(rider.outs)

    def riding(*refs):
        ins, rins = refs[:n_in], refs[n_in:n_in + r_in]
        outs, routs = refs[n_in + r_in:n_in + r_in + n_out], refs[n_in + r_in + n_out:n_in + r_in + n_out + r_out]
        rest = refs[n_in + r_in + n_out + r_out:]
        scratch, send, recv = rest[:-2], rest[-2], rest[-1]
        step = pl.program_id(0)
        for ax in range(1, len(grid)):
            step = step * grid[ax] + pl.program_id(ax)
        n_steps = math.prod(grid)

        @pl.when(step == 0)
        def _():
            rider.start(rins, routs, send, recv)

        body(*ins, *outs, *scratch)

        if rider.middle is not None:
            @pl.when(step == (17 * n_steps) // 20)
            def _():
                rider.middle(rins, routs, send, recv)

        @pl.when(step == n_steps - 1)
        def _():
            rider.finish(rins, routs, send, recv)

    return pl.pallas_call(
        riding, name=name, grid=grid, in_specs=list(in_specs) + [ANY] * r_in, out_specs=list(out_specs) + [ANY] * r_out,
        out_shape=list(out_shape) + rider.outs, scratch_shapes=list(scratch_shapes) + [_dma_sems(rider.n_sems), _dma_sems(rider.n_sems)],
        input_output_aliases={n_in + i: n_out + o for o, i in rider.alias.items()},
        compiler_params=_cp(*["arbitrary"] * len(grid)),
    )(*args, *rider.ins)


def _run_rider(rider, *, name):
    def nothing(*refs):
        pass

    return _call_with_rider(rider, nothing, name=name, grid=(1,), in_specs=[], args=[], out_specs=[], out_shape=[], scratch_shapes=[],
                            semantics=("arbitrary",))


def _mm(a, b, *, mode, name, out_dtype=F32, res=None, norm=None, rms_bwd=None, a_kind="2d", b_kind="2d", tm=None, tn=None, tk=None, rider=None):
    a2, b2 = a.shape[-2:], b.shape[-2:]
    if mode == "nn":
        (M, K), N = a2, b2[1]
    elif mode == "nt":
        (M, K), N = a2, b2[0]
    else:
        (K, M), N = a2, b2[1]
    kchunk = a_kind == "kchunk" or b_kind == "kchunk"
    batch = a_kind == "batch" or b_kind == "batch"
    G = (a.shape[0] if a_kind == "batch" else b.shape[0]) if batch else 1
    tm = tm or _tile(M, (1024, 512, 256, 128))
    tn = tn or _tile(N, (1024, 512, 256, 128))
    if kchunk:
        tk, nk = K, (a.shape[0] if a_kind == "kchunk" else b.shape[0])
    else:
        tk = tk or _tile(K, (1024, 512, 256, 128))
        nk = K // tk

    def spec(kind, blk, idx):
        if kind == "2d":
            return pl.BlockSpec(blk, lambda g, i, j, k: idx(g, i, j, k))
        if kind == "batch":
            return pl.BlockSpec((None,) + blk, lambda g, i, j, k: (g,) + idx(g, i, j, k))
        return pl.BlockSpec((None,) + blk, lambda g, i, j, k: (k,) + idx(g, i, j, 0))

    if mode == "nn":
        a_spec = spec(a_kind, (tm, tk), lambda g, i, j, k: (i, k))
        b_spec = spec(b_kind, (tk, tn), lambda g, i, j, k: (k, j))
    elif mode == "nt":
        a_spec = spec(a_kind, (tm, tk), lambda g, i, j, k: (i, k))
        b_spec = spec(b_kind, (tn, tk), lambda g, i, j, k: (j, k))
    else:
        a_spec = spec(a_kind, (tk, tm), lambda g, i, j, k: (k, i))
        b_spec = spec(b_kind, (tk, tn), lambda g, i, j, k: (k, j))
    o_kind = "batch" if batch else "2d"
    o_spec = spec(o_kind, (tm, tn), lambda g, i, j, k: (i, j))
    o_shape = ((G,) if batch else ()) + (M, N)
    dn = {"nn": NN, "nt": NT, "tn": TN}[mode]
    has_res, has_rms, has_norm = res is not None, rms_bwd is not None, norm is not None
    assert not (has_res and has_rms) and (not (has_rms or has_norm) or (tn == N and G == 1)) and (has_res or not has_norm)

    def body(*refs):
        if has_norm:
            a_ref, b_ref, r_ref, g_ref, o_ref, h_ref = refs[:6]
        elif has_res:
            a_ref, b_ref, r_ref, o_ref = refs[:4]
        elif has_rms:
            a_ref, b_ref, x_ref, g_ref, dres_ref, o_ref, dg_ref = refs[:7]
        else:
            a_ref, b_ref, o_ref = refs[:3]
        p = lax.dot_general(a_ref[...].astype(BF16), b_ref[...].astype(BF16), dn, preferred_element_type=F32)
        first_rows = pl.program_id(1) == 0

        def finish(r):
            if has_rms:
                xv = x_ref[...]
                rs = lax.rsqrt(jnp.mean(xv * xv, axis=-1, keepdims=True) + 1e-6)
                u = r * g_ref[...]
                s = jnp.sum(u * xv, axis=-1, keepdims=True)
                o_ref[...] = dres_ref[...] + rs * u - xv * ((rs * rs * rs) * (s * (1.0 / N)))
                part = jnp.sum(r * (xv * rs), axis=0, keepdims=True)

                @pl.when(first_rows)
                def _():
                    dg_ref[...] = part

                @pl.when(jnp.logical_not(first_rows))
                def _():
                    dg_ref[...] += part
                return
            if has_res:
                r = r + r_ref[...]
            o_ref[...] = r.astype(out_dtype)
            if has_norm:
                h_ref[...] = (r * lax.rsqrt(jnp.mean(r * r, axis=-1, keepdims=True) + 1e-6) * g_ref[...]).astype(BF16)

        if nk == 1:
            finish(p)
        else:
            acc = refs[-1]
            k = pl.program_id(3)

            @pl.when(k == 0)
            def _():
                acc[...] = p

            @pl.when(k > 0)
            def _():
                acc[...] += p

            @pl.when(k == nk - 1)
            def _():
                finish(acc[...])

    in_specs, args = [a_spec, b_spec], [a, b]
    tile = spec("2d", (tm, tn), lambda g, i, j, k: (i, j))
    out_specs, out_shape = [o_spec], [jax.ShapeDtypeStruct(o_shape, out_dtype)]
    vec = pl.BlockSpec((1, tn), lambda g, i, j, k: (0, 0))
    if has_res:
        in_specs.append(tile)
        args.append(res)
    if has_norm:
        in_specs.append(vec)
        args.append(norm.reshape(1, N))
        out_specs.append(tile)
        out_shape.append(jax.ShapeDtypeStruct((M, N), BF16))
    if has_rms:
        x, gain, dres = rms_bwd
        in_specs += [tile, vec, tile]
        args += [x, gain.reshape(1, N), dres]
        out_specs.append(vec)
        out_shape.append(jax.ShapeDtypeStruct((1, N), F32))
    outs = _call_with_rider(
        rider, body, name=name, grid=(G, M // tm, N // tn, nk), in_specs=in_specs, args=args, out_specs=out_specs,
        out_shape=out_shape, scratch_shapes=[pltpu.VMEM((tm, tn), F32)] if nk > 1 else [],
        semantics=("parallel", "arbitrary" if has_rms else "parallel", "parallel", "arbitrary"))
    main = (outs[0], outs[1].reshape(N)) if has_rms else (outs[0], outs[1]) if has_norm else outs[0]
    return main if rider is None else (main, outs[len(out_shape):])


def _rms_fwd(x, g, *, name):
    S, Dm = x.shape
    tm = _tile(S, (512, 256))

    def body(x_ref, g_ref, o_ref):
        xv = x_ref[...]
        r = lax.rsqrt(jnp.mean(xv * xv, axis=-1, keepdims=True) + 1e-6)
        o_ref[...] = (xv * r * g_ref[...]).astype(BF16)

    return pl.pallas_call(
        body, name=name, grid=(S // tm,),
        in_specs=[pl.BlockSpec((tm, Dm), lambda i: (i, 0)), pl.BlockSpec((1, Dm), lambda i: (0, 0))],
        out_specs=pl.BlockSpec((tm, Dm), lambda i: (i, 0)), out_shape=jax.ShapeDtypeStruct((S, Dm), BF16),
        compiler_params=_cp("parallel"),
    )(x, g.reshape(1, Dm))


def _rms_bwd(x, g, dh, dres, *, name):
    S, Dm = x.shape
    tm = _tile(S, (512, 256))

    def body(x_ref, g_ref, dh_ref, dr_ref, dx_ref, dg_ref):
        xv, dhv = x_ref[...], dh_ref[...].astype(F32)
        r = lax.rsqrt(jnp.mean(xv * xv, axis=-1, keepdims=True) + 1e-6)
        u = dhv * g_ref[...]
        s = jnp.sum(u * xv, axis=-1, keepdims=True)
        dx_ref[...] = dr_ref[...] + r * u - xv * ((r * r * r) * (s * (1.0 / Dm)))
        part = jnp.sum(dhv * (xv * r), axis=0, keepdims=True)

        @pl.when(pl.program_id(0) == 0)
        def _():
            dg_ref[...] = part

        @pl.when(pl.program_id(0) > 0)
        def _():
            dg_ref[...] += part

    row = pl.BlockSpec((tm, Dm), lambda i: (i, 0))
    vec = pl.BlockSpec((1, Dm), lambda i: (0, 0))
    dx, dg = pl.pallas_call(
        body, name=name, grid=(S // tm,), in_specs=[row, vec, row, row], out_specs=[row, vec],
        out_shape=[jax.ShapeDtypeStruct((S, Dm), F32), jax.ShapeDtypeStruct((1, Dm), F32)],
        compiler_params=_cp("arbitrary"),
    )(x, g.reshape(1, Dm), dh, dres)
    return dx, dg.reshape(Dm)


def _loss_head(x, g, target):
    S, Dm = x.shape
    tm = _tile(S, (512, 256))

    def body(x_ref, g_ref, t_ref, dx_ref, dg_ref, loss_ref):
        xv, gv = x_ref[...], g_ref[...]
        r = lax.rsqrt(jnp.mean(xv * xv, axis=-1, keepdims=True) + 1e-6)
        xn = xv * r
        err = xn * gv - t_ref[...]
        lpart = 0.5 * jnp.sum(jnp.mean(err * err, axis=-1, keepdims=True), axis=0, keepdims=True)
        dy = err * (1.0 / Dm)
        u = dy * gv
        s = jnp.sum(u * xv, axis=-1, keepdims=True)
        dx_ref[...] = r * u - xv * ((r * r * r) * (s * (1.0 / Dm)))
        part = jnp.sum(dy * xn, axis=0, keepdims=True)
        lslab = jnp.broadcast_to(lpart, (8, 128))

        @pl.when(pl.program_id(0) == 0)
        def _():
            dg_ref[...] = part
            loss_ref[...] = lslab

        @pl.when(pl.program_id(0) > 0)
        def _():
            dg_ref[...] += part
            loss_ref[...] += lslab

    row = pl.BlockSpec((tm, Dm), lambda i: (i, 0))
    vec = pl.BlockSpec((1, Dm), lambda i: (0, 0))
    dx, dg, loss = pl.pallas_call(
        body, name="loss_head", grid=(S // tm,), in_specs=[row, vec, row],
        out_specs=[row, vec, pl.BlockSpec((8, 128), lambda i: (0, 0))],
        out_shape=[jax.ShapeDtypeStruct((S, Dm), F32), jax.ShapeDtypeStruct((1, Dm), F32), jax.ShapeDtypeStruct((8, 128), F32)],
        compiler_params=_cp("arbitrary"),
    )(x, g.reshape(1, Dm), target)
    return loss[0, 0], dx, dg.reshape(Dm)


SB_TQ, SB_TK = 256, 256
SB_EXP_FLOOR = -104.0


def _split2(v):
    hi = v.astype(BF16)
    return jnp.concatenate([hi, (v - hi.astype(F32)).astype(BF16)], axis=1)


def _tri2(cmp):
    j = lax.broadcasted_iota(jnp.int32, (2 * SB_TK, SB_TK), 0) % SB_TK
    s = lax.broadcasted_iota(jnp.int32, (2 * SB_TK, SB_TK), 1)
    return cmp(j, s).astype(BF16)


def _sb_scores(qv, kb, k0, q0, tq):
    rows = qv.shape[0]
    z = lax.dot_general(qv, kb, NT, preferred_element_type=F32) * (SB_DH ** -0.5)
    t_pos = q0 + lax.broadcasted_iota(jnp.int32, (rows, SB_TK), 0) % tq
    s_pos = k0 + lax.broadcasted_iota(jnp.int32, (rows, SB_TK), 1)
    valid = s_pos < t_pos
    ls = jnp.minimum(z, 0.0) - jnp.log(1.0 + jnp.exp(-jnp.abs(z)))
    l1m = jnp.where(valid, ls - z, 0.0)
    return z, valid, ls, l1m


SB_PAIRS = SB_HEADS // 2
_Q_BLK, _K_BLK, _V_BLK = 0, SB_PAIRS, 2 * SB_PAIRS


def _wide(x):
    return x if SB_TK == 128 else jnp.concatenate([x] * (SB_TK // 128), axis=1)


def _lanes_of(h, shape):
    lane = lax.broadcasted_iota(jnp.int32, shape, len(shape) - 1)
    return (lane < SB_DH) if h == 0 else (lane >= SB_DH)


def _sb2_fwd(p, rider=None):
    S = p.shape[0]
    tq = min(SB_TQ, S)
    kb_per_q = tq // SB_TK

    def body(q_ref, k_ref, v_ref, o_ref, tot_ref, cnt_ref, qm, acc, c):
        i = pl.program_id(1)
        q0 = i * tq
        later = _tri2(lambda j, s: j > s)
        q2 = q_ref[...]
        for h in range(2):
            qm[h * tq:(h + 1) * tq, :] = jnp.where(_lanes_of(h, q2.shape), q2, 0.0).astype(BF16)
        acc[...] = jnp.zeros_like(acc)
        c[...] = jnp.zeros_like(c)
        nkb = (i + 1) * kb_per_q

        def more(st):
            n, highest = st
            return (n < nkb) & (highest > SB_EXP_FLOOR)

        def step(st):
            n, _ = st
            k0 = pl.multiple_of((nkb - 1 - n) * SB_TK, SB_TK)
            kb, vb = k_ref[pl.ds(k0, SB_TK), :].astype(BF16), v_ref[pl.ds(k0, SB_TK), :].astype(BF16)
            c_old = c[...]
            z, valid, ls, l1m = _sb_scores(qm[...], kb, k0, q0, tq)
            c_new = c_old + jnp.sum(l1m, axis=1, keepdims=True)
            after = jnp.dot(_split2(l1m), later, preferred_element_type=F32)
            a = jnp.where(valid, jnp.exp(ls + after + _wide(c_old)), 0.0)
            av = jnp.dot(a.astype(BF16), vb, preferred_element_type=F32)
            acc[...] += jnp.where(_lanes_of(0, (tq, 128)), av[:tq], av[tq:])
            c[...] = c_new
            return n + 1, jnp.max(c_new)

        n_done, _ = lax.while_loop(more, step, (jnp.int32(0), jnp.float32(0.0)))
        o_ref[...] = acc[...].astype(o_ref.dtype)
        for h in range(2):
            tot_ref[h] = c[h * tq:(h + 1) * tq, :]
        cnt_ref[...] = jnp.full(cnt_ref.shape, n_done.astype(F32))

    col = lambda first: pl.BlockSpec((S, 128), lambda g, i: (0, first + g))
    outs = _call_with_rider(
        rider, body, name="sb_fwd", grid=(SB_PAIRS, S // tq), args=[p, p, p],
        in_specs=[pl.BlockSpec((tq, 128), lambda g, i: (i, _Q_BLK + g)), col(_K_BLK), col(_V_BLK)],
        out_specs=[pl.BlockSpec((tq, 128), lambda g, i: (i, g)), pl.BlockSpec((2, tq, 128), lambda g, i: (g, i, 0)),
                   pl.BlockSpec((None, None, 8, 128), lambda g, i: (g, i, 0, 0))],
        out_shape=[jax.ShapeDtypeStruct((S, BW), BF16), jax.ShapeDtypeStruct((SB_HEADS, S, 128), F32),
                   jax.ShapeDtypeStruct((SB_PAIRS, S // tq, 8, 128), F32)],
        scratch_shapes=[pltpu.VMEM((2 * tq, 128), BF16), pltpu.VMEM((tq, 128), F32), pltpu.VMEM((2 * tq, 128), F32)],
        semantics=("parallel", "parallel"))
    return outs[:3], outs[3:]


def _sb2_bwd(p, dbr, tot, cnt, rider=None):
    S = p.shape[0]
    tq = min(SB_TQ, S)
    kb_per_q = tq // SB_TK
    scale = SB_DH ** -0.5

    def body(q_ref, k_ref, v_ref, do_ref, tot_ref, cnt_ref, dq_ref, dk_ref, dv_ref, qm, dom, tot, dq_acc, pre, gpre):
        i = pl.program_id(1)
        q0 = i * tq
        upto = _tri2(lambda j, s: j <= s)
        before = _tri2(lambda j, s: j < s)

        @pl.when(i == 0)
        def _():
            dk_ref[...] = jnp.zeros_like(dk_ref)
            dv_ref[...] = jnp.zeros_like(dv_ref)

        q2, do2 = q_ref[...], do_ref[...]
        for h in range(2):
            rows = slice(h * tq, (h + 1) * tq)
            qm[rows, :] = jnp.where(_lanes_of(h, q2.shape), q2, 0.0).astype(BF16)
            dom[rows, :] = jnp.where(_lanes_of(h, do2.shape), do2, 0.0).astype(BF16)
            tot[rows, :] = tot_ref[h]
        dq_acc[...] = jnp.zeros_like(dq_acc)
        pre[...] = jnp.zeros_like(pre)
        gpre[...] = jnp.zeros_like(gpre)

        n_done = jnp.max(cnt_ref[...]).astype(jnp.int32)
        first = (i + 1) * kb_per_q - n_done

        def step(n, carry):
            k0 = pl.multiple_of((first + n) * SB_TK, SB_TK)
            kb, vb = k_ref[pl.ds(k0, SB_TK), :].astype(BF16), v_ref[pl.ds(k0, SB_TK), :].astype(BF16)
            pre_o, gpre_o = pre[...], gpre[...]
            z, valid, ls, l1m = _sb_scores(qm[...], kb, k0, q0, tq)
            incl = jnp.dot(_split2(l1m), upto, preferred_element_type=F32)
            rest = _wide(tot[...] - pre_o) - incl
            a = jnp.where(valid, jnp.exp(ls + rest), 0.0)
            da = lax.dot_general(dom[...], vb, NT, preferred_element_type=F32)
            g = a * da
            gbefore = jnp.dot(_split2(g), before, preferred_element_type=F32) + _wide(gpre_o)
            dz = jnp.where(valid, g * jnp.exp(ls - z) - jnp.exp(ls) * gbefore, 0.0) * scale
            dzb = dz.astype(BF16)
            dq_p = jnp.dot(dzb, kb, preferred_element_type=F32)
            dq_acc[...] += jnp.where(_lanes_of(0, (tq, 128)), dq_p[:tq], dq_p[tq:])
            dk_ref[pl.ds(k0, SB_TK), :] += lax.dot_general(dzb, qm[...], TN, preferred_element_type=F32)
            dv_ref[pl.ds(k0, SB_TK), :] += lax.dot_general(a.astype(BF16), dom[...], TN, preferred_element_type=F32)
            pre[...] = pre_o + jnp.sum(l1m, axis=1, keepdims=True)
            gpre[...] = gpre_o + jnp.sum(g, axis=1, keepdims=True)
            return carry

        lax.fori_loop(0, n_done, step, 0)
        dq_ref[...] = dq_acc[...].astype(dq_ref.dtype)

    col = lambda first: pl.BlockSpec((S, 128), lambda g, i: (0, first + g))
    tile = pl.BlockSpec((tq, 128), lambda g, i: (i, g))
    whole = pl.BlockSpec((S, 128), lambda g, i: (0, g))
    outs = _call_with_rider(
        rider, body, name="sb_bwd", grid=(SB_PAIRS, S // tq), args=[p, p, p, dbr, tot, cnt],
        in_specs=[pl.BlockSpec((tq, 128), lambda g, i: (i, _Q_BLK + g)), col(_K_BLK), col(_V_BLK),
                  pl.BlockSpec((None, tq, 128), lambda g, i: (0, i, g)), pl.BlockSpec((2, tq, 128), lambda g, i: (g, i, 0)),
                  pl.BlockSpec((None, None, 8, 128), lambda g, i: (g, i, 0, 0))],
        out_specs=[tile, whole, whole],
        out_shape=[jax.ShapeDtypeStruct((S, BW), BF16), jax.ShapeDtypeStruct((S, BW), F32), jax.ShapeDtypeStruct((S, BW), F32)],
        scratch_shapes=[pltpu.VMEM((2 * tq, 128), BF16), pltpu.VMEM((2 * tq, 128), BF16), pltpu.VMEM((2 * tq, 128), F32),
                        pltpu.VMEM((tq, 128), F32), pltpu.VMEM((2 * tq, 128), F32), pltpu.VMEM((2 * tq, 128), F32)],
        semantics=("parallel", "arbitrary"))
    return outs[:3], outs[3:]


_INV_SQRT2 = 0.7071067811865476
_INV_SQRT2PI = 0.3989422804014327


def _gelu(x):
    return 0.5 * x * (1.0 + lax.erf(x * _INV_SQRT2))


def _gelu_grad(x):
    return 0.5 * (1.0 + lax.erf(x * _INV_SQRT2)) + x * (_INV_SQRT2PI * jnp.exp(-0.5 * x * x))


def _sgu_mask():
    t = lax.broadcasted_iota(jnp.int32, (SGU_LEN, SGU_LEN), 0) // SGU_CHUNK
    s = lax.broadcasted_iota(jnp.int32, (SGU_LEN, SGU_LEN), 1) // SGU_CHUNK
    return t >= s


def _sgu_mask_t():
    t = lax.broadcasted_iota(jnp.int32, (SGU_LEN, SGU_LEN), 0) // SGU_CHUNK
    s = lax.broadcasted_iota(jnp.int32, (SGU_LEN, SGU_LEN), 1) // SGU_CHUNK
    return s >= t


def _sgu_norm(zv, g, b):
    vv = _gelu(zv)
    xc = vv - jnp.mean(vv, axis=-1, keepdims=True)
    rstd = lax.rsqrt(jnp.mean(xc * xc, axis=-1, keepdims=True) + 1e-5)
    xhat = xc * rstd
    return xhat, rstd, xhat * g + b


SGU_TM = 256


def _sgu_fwd(p, ln_g, ln_b, w_s, b_st):
    S = p.shape[0]
    tm = min(SGU_TM, S)

    def body(zu_ref, zv_ref, g_ref, b_ref, w_ref, bs_ref, o_ref):
        u = _gelu(zu_ref[...].astype(F32))
        _, _, vn = _sgu_norm(zv_ref[...].astype(F32), g_ref[...], b_ref[...])
        vnb = vn.astype(BF16)
        mask = _sgu_mask()
        for gi in range(SGU_GROUPS):
            wg = jnp.where(mask, w_ref[gi], 0.0).astype(BF16)
            cols = slice(gi * SGU_GD, (gi + 1) * SGU_GD)
            for ci in range(tm // SGU_LEN):
                rows = slice(ci * SGU_LEN, (ci + 1) * SGU_LEN)
                vm = jnp.dot(wg, vnb[rows, cols], preferred_element_type=F32) + bs_ref[:, gi:gi + 1]
                o_ref[rows, cols] = (u[rows, cols] * vm).astype(BF16)

    vec = pl.BlockSpec((1, BW), lambda i: (0, 0))
    return pl.pallas_call(
        body, name="sgu_fwd", grid=(S // tm,),
        in_specs=[pl.BlockSpec((tm, BW), lambda i: (i, C_Z // BW)), pl.BlockSpec((tm, BW), lambda i: (i, C_Z // BW + 1)), vec, vec,
                  pl.BlockSpec((SGU_GROUPS, SGU_LEN, SGU_LEN), lambda i: (0, 0, 0)), pl.BlockSpec((SGU_LEN, SGU_GROUPS), lambda i: (0, 0))],
        out_specs=pl.BlockSpec((tm, BW), lambda i: (i, 0)), out_shape=jax.ShapeDtypeStruct((S, BW), BF16),
        compiler_params=_cp("parallel"),
    )(p, p, ln_g.reshape(1, BW), ln_b.reshape(1, BW), w_s, b_st)


def _sgu_bwd(p, dyb, ln_g, ln_b, w_s, w_st, b_st):
    S = p.shape[0]
    tm = min(SGU_TM, S)

    def body(zu_ref, zv_ref, dy_ref, g_ref, b_ref, w_ref, wt_ref, bs_ref, dz_ref, dg_ref, db_ref, dw_ref, dbs_ref, dvn):
        first = pl.program_id(0) == 0

        @pl.when(first)
        def _():
            dg_ref[...] = jnp.zeros_like(dg_ref)
            db_ref[...] = jnp.zeros_like(db_ref)
            dw_ref[...] = jnp.zeros_like(dw_ref)
            dbs_ref[...] = jnp.zeros_like(dbs_ref)

        zu, zv, dy = zu_ref[...].astype(F32), zv_ref[...].astype(F32), dy_ref[...].astype(F32)
        u = _gelu(zu)
        xhat, rstd, vn = _sgu_norm(zv, g_ref[...], b_ref[...])
        vnb = vn.astype(BF16)
        mask = _sgu_mask()
        mask_t = _sgu_mask_t()
        for gi in range(SGU_GROUPS):
            wg = jnp.where(mask, w_ref[gi], 0.0).astype(BF16)
            wgt = jnp.where(mask_t, wt_ref[gi], 0.0).astype(BF16)
            cols = slice(gi * SGU_GD, (gi + 1) * SGU_GD)
            for ci in range(tm // SGU_LEN):
                rows = slice(ci * SGU_LEN, (ci + 1) * SGU_LEN)
                vm = jnp.dot(wg, vnb[rows, cols], preferred_element_type=F32) + bs_ref[:, gi:gi + 1]
                dyc = dy[rows, cols]
                dz_ref[rows, cols] = (dyc * vm * _gelu_grad(zu[rows, cols])).astype(BF16)
                dvm = dyc * u[rows, cols]
                dvmb = dvm.astype(BF16)
                dbs_ref[gi] += jnp.broadcast_to(jnp.sum(dvm, axis=1, keepdims=True), (SGU_LEN, SGU_GD))
                dw_ref[gi] += lax.dot_general(dvmb, vnb[rows, cols], NT, preferred_element_type=F32)
                dvn[rows, cols] = jnp.dot(wgt, dvmb, preferred_element_type=F32)
        dvnv = dvn[...]
        dg_ref[...] += jnp.sum(dvnv * xhat, axis=0, keepdims=True)
        db_ref[...] += jnp.sum(dvnv, axis=0, keepdims=True)
        dxh = dvnv * g_ref[...]
        dvv = rstd * (dxh - jnp.mean(dxh, axis=-1, keepdims=True) - xhat * jnp.mean(dxh * xhat, axis=-1, keepdims=True))
        dz_ref[:, BW:] = (dvv * _gelu_grad(zv)).astype(BF16)

        @pl.when(pl.program_id(0) == n_steps - 1)
        def _():
            for gi in range(SGU_GROUPS):
                dw_ref[gi] = jnp.where(mask, dw_ref[gi], 0.0)

    n_steps = S // tm
    vec = pl.BlockSpec((1, BW), lambda i: (0, 0))
    half = lambda c: pl.BlockSpec((tm, BW), lambda i: (i, c))
    wspec = pl.BlockSpec((SGU_GROUPS, SGU_LEN, SGU_LEN), lambda i: (0, 0, 0))
    dz, dg, db, dw, dbs = pl.pallas_call(
        body, name="sgu_bwd", grid=(n_steps,),
        in_specs=[half(C_Z // BW), half(C_Z // BW + 1), half(0), vec, vec, wspec, wspec,
                  pl.BlockSpec((SGU_LEN, SGU_GROUPS), lambda i: (0, 0))],
        out_specs=[pl.BlockSpec((tm, 2 * BW), lambda i: (i, 0)), vec, vec, wspec, wspec],
        out_shape=[jax.ShapeDtypeStruct((S, 2 * BW), BF16), jax.ShapeDtypeStruct((1, BW), F32), jax.ShapeDtypeStruct((1, BW), F32),
                   jax.ShapeDtypeStruct((SGU_GROUPS, SGU_LEN, SGU_LEN), F32), jax.ShapeDtypeStruct((SGU_GROUPS, SGU_LEN, SGU_GD), F32)],
        scratch_shapes=[pltpu.VMEM((tm, BW), F32)],
        compiler_params=_cp("arbitrary"),
    )(p, p, dyb, ln_g.reshape(1, BW), ln_b.reshape(1, BW), w_s, w_st, b_st)
    return dz, dg.reshape(BW), db.reshape(BW), dw, dbs[:, :, 0]


CONV_TC = 128


def _shift_down(y, n):
    rows = lax.broadcasted_iota(jnp.int32, y.shape, 0)
    return jnp.where(rows < n, 0.0, pltpu.roll(y, n, 0))


def _shift_up(y, n):
    rows = lax.broadcasted_iota(jnp.int32, y.shape, 0)
    return jnp.where(rows >= y.shape[0] - n, 0.0, pltpu.roll(y, y.shape[0] - n, 0))


def _conv_specs(S):
    col = lambda c0: pl.BlockSpec((S, CONV_TC), lambda j: (0, c0 // CONV_TC + j))
    return col(C_CB), col(C_CB + BW), col(C_CB + 2 * BW), pl.BlockSpec((3, CONV_TC), lambda j: (0, j)), pl.BlockSpec((S, CONV_TC), lambda j: (0, j))


def _conv_fwd(p, conv_w):
    S = p.shape[0]

    def body(cb_ref, cc_ref, cx_ref, w_ref, o_ref):
        y = cc_ref[...].astype(F32) * cx_ref[...].astype(F32)
        conv = w_ref[0:1, :] * _shift_down(y, 2) + w_ref[1:2, :] * _shift_down(y, 1) + w_ref[2:3, :] * y
        o_ref[...] = (cb_ref[...].astype(F32) * conv).astype(BF16)

    cb, cc, cx, wspec, out = _conv_specs(S)
    return pl.pallas_call(
        body, name="conv_fwd", grid=(BW // CONV_TC,), in_specs=[cb, cc, cx, wspec], out_specs=out,
        out_shape=jax.ShapeDtypeStruct((S, BW), BF16), compiler_params=_cp("parallel"),
    )(p, p, p, conv_w)


def _conv_bwd(p, conv_w, dyc):
    S = p.shape[0]

    def body(cb_ref, cc_ref, cx_ref, w_ref, dy_ref, db_ref, dc_ref, dx_ref, dw_ref):
        cc, cx, dy = cc_ref[...].astype(F32), cx_ref[...].astype(F32), dy_ref[...].astype(F32)
        y = cc * cx
        w0, w1, w2 = w_ref[0:1, :], w_ref[1:2, :], w_ref[2:3, :]
        y1, y2 = _shift_down(y, 1), _shift_down(y, 2)
        conv = w0 * y2 + w1 * y1 + w2 * y
        db_ref[...] = (dy * conv).astype(BF16)
        dconv = dy * cb_ref[...].astype(F32)
        dyy = w2 * dconv + w1 * _shift_up(dconv, 1) + w0 * _shift_up(dconv, 2)
        dc_ref[...] = (dyy * cx).astype(BF16)
        dx_ref[...] = (dyy * cc).astype(BF16)
        dw_ref[0:1, :] = jnp.sum(dconv * y2, axis=0, keepdims=True)
        dw_ref[1:2, :] = jnp.sum(dconv * y1, axis=0, keepdims=True)
        dw_ref[2:3, :] = jnp.sum(dconv * y, axis=0, keepdims=True)

    cb, cc, cx, wspec, out = _conv_specs(S)
    db, dc, dx, dw = pl.pallas_call(
        body, name="conv_bwd", grid=(BW // CONV_TC,), in_specs=[cb, cc, cx, wspec, out],
        out_specs=[out, out, out, wspec],
        out_shape=[jax.ShapeDtypeStruct((S, BW), BF16)] * 3 + [jax.ShapeDtypeStruct((3, BW), F32)],
        compiler_params=_cp("parallel"),
    )(p, p, p, conv_w, dyc)
    return db, dc, dx, dw


def _merge_specs(S, tm):
    gate = lambda n: pl.BlockSpec((tm, D_MODEL), lambda i: (i, C_GATES // D_MODEL + n))
    return [gate(0), gate(1), gate(2)], pl.BlockSpec((3, tm, D_MODEL), lambda i: (0, i, 0)), pl.BlockSpec((tm, D_MODEL), lambda i: (i, 0))


def _branch_merge(br, w_br, p):
    S = p.shape[0]
    tm = _tile(S, (512, 256))

    def body(br_ref, w_ref, g0, g1, g2, bd_ref, o_ref):
        acc = None
        for n, g_ref in enumerate((g0, g1, g2)):
            bdn = jnp.dot(br_ref[n], w_ref[n], preferred_element_type=F32)
            bd_ref[n] = bdn.astype(BF16)
            term = jax.nn.sigmoid(g_ref[...].astype(F32)) * bdn
            acc = term if acc is None else acc + term
        o_ref[...] = acc.astype(BF16)

    gates, bspec, row = _merge_specs(S, tm)
    return pl.pallas_call(
        body, name="mm_branch", grid=(S // tm,),
        in_specs=[pl.BlockSpec((3, tm, BW), lambda i: (0, i, 0)), pl.BlockSpec((3, BW, D_MODEL), lambda i: (0, 0, 0))] + gates,
        out_specs=[bspec, row], out_shape=[jax.ShapeDtypeStruct((3, S, D_MODEL), BF16), jax.ShapeDtypeStruct((S, D_MODEL), BF16)],
        compiler_params=_cp("parallel"),
    )(br, w_br, p, p, p)


def _merge_bwd(p, bd, dm):
    S = p.shape[0]
    tm = _tile(S, (256,))

    def body(g0, g1, g2, b_ref, dm_ref, db_ref, dg_ref):
        dmv = dm_ref[...]
        for n, g_ref in enumerate((g0, g1, g2)):
            sg = jax.nn.sigmoid(g_ref[...].astype(F32))
            db_ref[n] = (dmv * sg).astype(BF16)
            dg_ref[:, n * D_MODEL:(n + 1) * D_MODEL] = (dmv * b_ref[n].astype(F32) * (sg * (1.0 - sg))).astype(BF16)

    gates, bspec, row = _merge_specs(S, tm)
    return pl.pallas_call(
        body, name="merge_bwd", grid=(S // tm,), in_specs=gates + [bspec, row],
        out_specs=[bspec, pl.BlockSpec((tm, 3 * D_MODEL), lambda i: (i, 0))],
        out_shape=[jax.ShapeDtypeStruct((3, S, D_MODEL), BF16), jax.ShapeDtypeStruct((S, 3 * D_MODEL), BF16)],
        compiler_params=_cp("parallel"),
    )(p, p, p, bd, dm)


XA_TM = 512


def _xa_probs(qh, kh):
    s = lax.dot_general(qh, kh, NT, preferred_element_type=F32) * (XA_DH ** -0.5)
    e = jnp.exp(s - jnp.max(s, axis=-1, keepdims=True))
    return e / jnp.sum(e, axis=-1, keepdims=True)


def _xa_fwd(q, kv):
    S = q.shape[0]
    tm = min(XA_TM, S)
    M = kv.shape[1]

    def body(q_ref, kv_ref, o_ref):
        for h in range(XA_HEADS):
            cols = slice(h * XA_DH, (h + 1) * XA_DH)
            pr = _xa_probs(q_ref[:, cols], kv_ref[0, :, cols])
            o_ref[:, cols] = jnp.dot(pr.astype(BF16), kv_ref[1, :, cols], preferred_element_type=F32).astype(BF16)

    row = pl.BlockSpec((tm, D_MODEL), lambda i: (i, 0))
    return pl.pallas_call(
        body, name="xa_fwd", grid=(S // tm,), in_specs=[row, pl.BlockSpec((2, M, D_MODEL), lambda i: (0, 0, 0))], out_specs=row,
        out_shape=jax.ShapeDtypeStruct((S, D_MODEL), BF16), compiler_params=_cp("parallel"),
    )(q, kv)


def _xa_bwd(q, kv, do):
    S = q.shape[0]
    tm = min(XA_TM, S)
    M = kv.shape[1]

    def body(q_ref, kv_ref, do_ref, dq_ref, dkv_ref):
        @pl.when(pl.program_id(0) == 0)
        def _():
            dkv_ref[...] = jnp.zeros_like(dkv_ref)

        for h in range(XA_HEADS):
            cols = slice(h * XA_DH, (h + 1) * XA_DH)
            qh, kh, vh, doh = q_ref[:, cols], kv_ref[0, :, cols], kv_ref[1, :, cols], do_ref[:, cols]
            pr = _xa_probs(qh, kh)
            dkv_ref[1, :, cols] += lax.dot_general(pr.astype(BF16), doh, TN, preferred_element_type=F32)
            dp = lax.dot_general(doh, vh, NT, preferred_element_type=F32)
            ds = (pr * (dp - jnp.sum(dp * pr, axis=-1, keepdims=True)) * (XA_DH ** -0.5)).astype(BF16)
            dq_ref[:, cols] = jnp.dot(ds, kh, preferred_element_type=F32).astype(BF16)
            dkv_ref[0, :, cols] += lax.dot_general(ds, qh, TN, preferred_element_type=F32)

    row = pl.BlockSpec((tm, D_MODEL), lambda i: (i, 0))
    kvs = pl.BlockSpec((2, M, D_MODEL), lambda i: (0, 0, 0))
    return pl.pallas_call(
        body, name="xa_bwd", grid=(S // tm,), in_specs=[row, kvs, row], out_specs=[row, kvs],
        out_shape=[jax.ShapeDtypeStruct((S, D_MODEL), BF16), jax.ShapeDtypeStruct((2, M, D_MODEL), F32)],
        compiler_params=_cp("arbitrary"),
    )(q, kv, do)


def _ffn_up(h3, w_gu, rider=None):
    S = h3.shape[0]
    tm = _tile(S, (1024, 512, 256))

    def body(h_ref, w_ref, ab_ref, hh_ref):
        h = h_ref[...]
        a = lax.dot_general(h, w_ref[0], NT, preferred_element_type=F32)
        b = lax.dot_general(h, w_ref[1], NT, preferred_element_type=F32)
        ab_ref[0] = a.astype(BF16)
        ab_ref[1] = b.astype(BF16)
        hh_ref[...] = (a * jax.nn.sigmoid(a) * b).astype(BF16)

    pair = pl.BlockSpec((None, 2, tm, FFN_SH), lambda j, i: (j, 0, i, 0))
    outs = _call_with_rider(
        rider, body, name="mm_gu", grid=(N_CHIPS, S // tm), args=[h3, w_gu],
        in_specs=[pl.BlockSpec((tm, D_MODEL), lambda j, i: (i, 0)), pl.BlockSpec((None, 2, FFN_SH, D_MODEL), lambda j, i: (j, 0, 0, 0))],
        out_specs=[pair, pl.BlockSpec((None, tm, FFN_SH), lambda j, i: (j, i, 0))],
        out_shape=[jax.ShapeDtypeStruct((N_CHIPS, 2, S, FFN_SH), BF16), jax.ShapeDtypeStruct((N_CHIPS, S, FFN_SH), BF16)],
        scratch_shapes=[], semantics=("parallel", "parallel"))
    return outs[0], outs[1], outs[2:]


def _ffn_down_bwd(dx3, w_dn, ab):
    S = dx3.shape[0]
    tm = _tile(S, (1024, 512, 256))

    def body(dx_ref, w_ref, ab_ref, o_ref):
        d = lax.dot_general(dx_ref[...].astype(BF16), w_ref[...], NT, preferred_element_type=F32)
        a, b = ab_ref[0].astype(F32), ab_ref[1].astype(F32)
        sg = jax.nn.sigmoid(a)
        o_ref[0] = (d * b * (sg * (1.0 + a * (1.0 - sg)))).astype(BF16)
        o_ref[1] = (d * (a * sg)).astype(BF16)

    pair = pl.BlockSpec((None, 2, tm, FFN_SH), lambda j, i: (j, 0, i, 0))
    return pl.pallas_call(
        body, name="mm_down_dx", grid=(N_CHIPS, S // tm),
        in_specs=[pl.BlockSpec((tm, D_MODEL), lambda j, i: (i, 0)), pl.BlockSpec((None, FFN_SH, D_MODEL), lambda j, i: (j, 0, 0)), pair],
        out_specs=pair, out_shape=jax.ShapeDtypeStruct(ab.shape, BF16), compiler_params=_cp("parallel", "parallel"),
    )(dx3, w_dn, ab)


def _reduce_adam(parts, w, m, v, *, name):
    shape = w.shape
    C = shape[-1]
    R = math.prod(shape[:-1])
    tm = _rows(R, 4 * C)
    n = len(parts)
    c1, c2 = 1.0 - ADAM_B1 ** ADAM_STEP, 1.0 - ADAM_B2 ** ADAM_STEP

    def body(*refs):
        g = refs[0][...]
        for r in refs[1:n]:
            g = g + r[...]
        w_ref, m_ref, v_ref, go, do, mo, vo = refs[n:]
        mn = ADAM_B1 * m_ref[...] + (1.0 - ADAM_B1) * g
        vn = ADAM_B2 * v_ref[...] + (1.0 - ADAM_B2) * (g * g)
        go[...] = g
        do[...] = -ADAM_LR * ((mn / c1) / (jnp.sqrt(vn / c2) + ADAM_EPS) + ADAM_WD * w_ref[...])
        mo[...] = mn
        vo[...] = vn

    row = pl.BlockSpec((tm, C), lambda i: (i, 0))
    outs = pl.pallas_call(
        body, name=name, grid=(R // tm,), in_specs=[row] * (n + 3), out_specs=[row] * 4,
        out_shape=[jax.ShapeDtypeStruct((R, C), F32)] * 4, compiler_params=_cp("parallel"),
    )(*[a.reshape(R, C) for a in (*parts, w, m, v)])
    return tuple(o.reshape(shape) for o in outs)


_VIEW = {
    "w_in": ((1024, 7168), (1024, 1792), 256, lambda i, b: (i, b)),
    "w_br": ((1536, 1024), (1536, 256), 512, lambda i, b: (i, b)),
    "w_sq": ((5120, 1024), (1280, 1024), 256, lambda i, b: (4 * i + b, 0)),
    "w_gu": ((5632, 1024), (1408, 1024), 352, lambda i, b: (4 * b + i, 0)),
    "w_dn": ((2816, 1024), (704, 1024), 352, lambda i, b: (2 * b + i, 0)),
    "conv_w": ((8, 512), (8, 128), 8, lambda i, b: (0, b)),
}


def _scalar(v):
    return jnp.asarray(v, jnp.int32).reshape(1)


def _place(name, local, b):
    full2, sh2, tm, idx = _VIEW[name]
    C = sh2[1]
    dt = local.dtype if name == "conv_w" else BF16

    def body(b_ref, x_ref, o0_ref, o1_ref):
        o0_ref[...] = x_ref[0].astype(dt)
        o1_ref[...] = x_ref[1].astype(dt)

    place = pl.BlockSpec((tm, C), lambda i, bs: idx(i, bs[0]))
    outs = pl.pallas_call(
        body, name="place_" + name,
        grid_spec=pltpu.PrefetchScalarGridSpec(num_scalar_prefetch=1, grid=(sh2[0] // tm,),
                                               in_specs=[pl.BlockSpec((DEPTH, tm, C), lambda i, bs: (0, i, 0))], out_specs=[place, place]),
        out_shape=[jax.ShapeDtypeStruct(full2, dt)] * 2, compiler_params=_cp("arbitrary"),
    )(_scalar(b), local.reshape((DEPTH,) + sh2))
    return [o.reshape(_FULL_SHAPE[name]) for o in outs]


def _add_owner(name, g, land, own):
    shape = g.shape
    C = shape[-1]
    R = math.prod(shape[:-1])
    tm = _rows(R, 4 * C)

    def body(s_ref, g_ref, l_ref, o_ref):
        @pl.when(s_ref[0] != 0)
        def _():
            o_ref[...] = (g_ref[...].astype(F32) + l_ref[...].astype(F32)).astype(BF16)

        @pl.when(s_ref[0] == 0)
        def _():
            o_ref[...] = jnp.zeros_like(o_ref)

    pick = pl.BlockSpec((tm, C), lambda i, s: (jnp.where(s[0] != 0, i, 0), 0))
    return pl.pallas_call(
        body, name="presum_" + name,
        grid_spec=pltpu.PrefetchScalarGridSpec(num_scalar_prefetch=1, grid=(R // tm,), in_specs=[pick, pick],
                                               out_specs=pl.BlockSpec((tm, C), lambda i, s: (i, 0))),
        out_shape=jax.ShapeDtypeStruct((R, C), BF16), compiler_params=_cp("arbitrary"),
    )(_scalar(own), g.reshape(R, C), land.reshape(R, C)).reshape(shape)


def _sum_chips(name, slots, part, b, own):
    full2, sh2, tm, idx = _VIEW[name]
    C = sh2[1]

    def body(s_ref, slot_ref, own_ref, o_ref):
        @pl.when(s_ref[1] != 0)
        def _():
            o_ref[...] = ((slot_ref[0].astype(F32) + slot_ref[1].astype(F32)) + slot_ref[2].astype(F32)) + own_ref[...].astype(F32)

        @pl.when(s_ref[1] == 0)
        def _():
            o_ref[...] = jnp.zeros_like(o_ref)

    return pl.pallas_call(
        body, name="sum_chips_" + name,
        grid_spec=pltpu.PrefetchScalarGridSpec(
            num_scalar_prefetch=1, grid=(sh2[0] // tm,),
            in_specs=[pl.BlockSpec((3, tm, C), lambda i, s: (0, jnp.where(s[1] != 0, i, 0), 0)),
                      pl.BlockSpec((tm, C), lambda i, s: idx(jnp.where(s[1] != 0, i, 0), s[0]))],
            out_specs=pl.BlockSpec((tm, C), lambda i, s: (i, 0))),
        out_shape=jax.ShapeDtypeStruct(sh2, F32), compiler_params=_cp("arbitrary"),
    )(jnp.stack([jnp.asarray(b, jnp.int32), jnp.asarray(own, jnp.int32)]), slots.reshape((3,) + sh2),
      part.reshape(full2)).reshape(_SHARD_SHAPE[name])


def _adam_layers(mine, theirs, c, w, m, v, *, name):
    shape = w.shape
    C = shape[-1]
    R = math.prod(shape[1:-1])
    tm = _rows(R, 4 * C)
    c1, c2 = 1.0 - ADAM_B1 ** ADAM_STEP, 1.0 - ADAM_B2 ** ADAM_STEP

    def body(c_ref, m0_ref, m1_ref, t_ref, w_ref, m_ref, v_ref, go, do, mo, vo):
        layer = pl.program_id(0)
        g = jnp.where(layer == c_ref[0], jnp.where(layer == 0, m0_ref[...], m1_ref[...]), t_ref[...])
        mn = ADAM_B1 * m_ref[...] + (1.0 - ADAM_B1) * g
        vn = ADAM_B2 * v_ref[...] + (1.0 - ADAM_B2) * (g * g)
        go[...] = g
        do[...] = -ADAM_LR * ((mn / c1) / (jnp.sqrt(vn / c2) + ADAM_EPS) + ADAM_WD * w_ref[...])
        mo[...] = mn
        vo[...] = vn

    def own(layer):
        return pl.BlockSpec((tm, C), lambda l, i, cs: (jnp.where((l == layer) & (cs[0] == layer), i, 0), 0))

    recv = pl.BlockSpec((tm, C), lambda l, i, cs: (jnp.where(l == cs[0], 0, i), 0))
    row = pl.BlockSpec((None, tm, C), lambda l, i, cs: (l, i, 0))
    outs = pl.pallas_call(
        body, name=name,
        grid_spec=pltpu.PrefetchScalarGridSpec(num_scalar_prefetch=1, grid=(DEPTH, R // tm),
                                               in_specs=[own(0), own(1), recv, row, row, row], out_specs=[row] * 4),
        out_shape=[jax.ShapeDtypeStruct((DEPTH, R, C), F32)] * 4, compiler_params=_cp("arbitrary", "arbitrary"),
    )(_scalar(c), mine[0].reshape(R, C), mine[1].reshape(R, C), theirs.reshape(R, C), *[t.reshape(DEPTH, R, C) for t in (w, m, v)])
    return tuple(o.reshape(shape) for o in outs)


def _take_weights(wl, names, landed):
    for n, t in zip(names, landed):
        wl[n] = t[:3] if n == "conv_w" else t


_GATHER_LATE = ["w_br", "conv_w", "w_sq"]
_GATHER_LAST = ["w_gu", "w_dn"]


def _layer_fwd(x, mem, wl, ride=None, h1=None, next_gain=None):
    S = x.shape[0]
    sv = {"x": x}
    wl = dict(wl)
    if h1 is None:
        h1 = _rms_fwd(x, wl["norm_mix_g"], name="rms_mix")
    if ride is None:
        ride = {"mm_in": None, "sb": None, "mm_gu": None}
        p = _mm(h1, wl["w_in"], mode="nn", out_dtype=BF16, name="mm_in")
    else:
        p, landed = _mm(h1, wl["w_in"], mode="nn", out_dtype=BF16, name="mm_in", rider=ride["mm_in"])
        _take_weights(wl, _GATHER_LATE, landed)
    (ya, tot, cnt), landed = _sb2_fwd(p, ride["sb"])
    _take_weights(wl, _GATHER_LAST, landed)
    b_st = wl["b_spatial"].T
    yb = _sgu_fwd(p, wl["sgu_ln_g"], wl["sgu_ln_b"], wl["w_spatial"], b_st)
    yc = _conv_fwd(p, wl["conv_w"])
    br = jnp.stack([ya, yb, yc])
    bd, merged = _branch_merge(br, wl["w_br"], p)
    x1, h2 = _mm(merged, wl["w_sq"][0], mode="nn", res=x, norm=wl["norm_xa_g"], name="mm_out")
    qx = _mm(h2, wl["w_sq"][1], mode="nn", out_dtype=BF16, name="mm_q")
    mn = _rms_fwd(mem, wl["mem_norm_g"], name="rms_mem")
    kv = _mm(mn, wl["w_sq"][3:5], mode="nn", b_kind="batch", out_dtype=BF16, name="mm_kv")
    o = _xa_fwd(qx, kv)
    x2, h3 = _mm(o, wl["w_sq"][2], mode="nn", res=x1, norm=wl["norm_ffn_g"], name="mm_o")
    ab, hh, rode = _ffn_up(h3, wl["w_gu"], ride["mm_gu"])
    x3 = _mm(hh, wl["w_dn"], mode="nn", a_kind="kchunk", b_kind="kchunk", res=x2, norm=next_gain, name="mm_down")
    x3, h_next = x3 if next_gain is not None else (x3, None)
    sv.update(h1=h1, p=p, tot=tot, cnt=cnt, br=br, bd=bd, merged=merged, x1=x1, h2=h2, qx=qx, mn=mn, kv=kv, o=o,
              x2=x2, h3=h3, ab=ab, hh=hh, b_st=b_st, wl=wl)
    return x3, sv, rode, h_next


class _GradPipe:
    def __init__(self, ci, bi):
        self.ci, self.bi, self.queue = ci, bi, []
        self.part, self.slots = [dict(), dict()], [dict(), dict()]

    def to_owner(self, layer, names, g):
        def arrived(land):
            own = (self.ci == layer).astype(jnp.int32)
            part = {n: _add_owner(n, g[n], t, own) for n, t in zip(names, land)}
            self.part[layer].update(part)
            self.queue.append((layer, names, part))

        return _presum_rider(layer, {n: g[n] for n in names}), arrived

    def exchange(self):
        if not self.queue:
            return None
        layer, names, part = self.queue.pop(0)
        return _shard_rider(layer, part), lambda slots: self.slots[layer].update(zip(names, slots))

    def drain(self, also):
        landed = None
        while self.queue or landed is None:
            job = self.exchange()
            both = _join([job[0] if job else None, also if landed is None else None])
            parts = both.split(_run_rider(both, name="grad_exchange_last"))
            if job:
                job[1](parts[0])
            if landed is None:
                landed = parts[-1]
        return landed

    def reduced(self, layer):
        own = (self.ci == layer).astype(jnp.int32)
        return {n: _sum_chips(n, self.slots[layer][n], self.part[layer][n], self.bi, own) for n in _BIG}


def _layer_bwd(dx3, mem, sv, layer=None, pipe=None):
    S = dx3.shape[0]
    p, wl = sv["p"], sv["wl"]
    g = {}

    def carrying(jobs, fn):
        jobs = [j for j in jobs if j]
        joined = _join([j[0] for j in jobs])
        out, landed = fn(joined)
        if joined is not None:
            for j, part in zip(jobs, joined.split(landed)):
                j[1](part)
        return out

    def mm(*args, job=None, jobs=(), **kw):
        return carrying([job, *jobs], lambda r: (_mm(*args, **kw), None) if r is None else _mm(*args, **kw, rider=r))

    to_owner = (lambda names: pipe.to_owner(layer, names, g)) if pipe else (lambda names: None)
    exchange = pipe.exchange if pipe else (lambda: None)

    g["w_dn"] = _mm(sv["hh"], dx3, mode="tn", a_kind="batch", out_dtype=BF16, name="mm_down_dw")
    dab = _ffn_down_bwd(dx3, wl["w_dn"], sv["ab"]).reshape(2 * N_CHIPS, S, FFN_SH)
    g["w_gu"] = mm(dab, sv["h3"], mode="tn", a_kind="batch", out_dtype=BF16, name="mm_gu_dw", job=exchange()).reshape(_FULL_SHAPE["w_gu"])
    dx2, g["norm_ffn_g"] = mm(dab, wl["w_gu"].reshape(2 * N_CHIPS, FFN_SH, D_MODEL), mode="nn", a_kind="kchunk", b_kind="kchunk", name="mm_gu_dx",
                               rms_bwd=(sv["x2"], wl["norm_ffn_g"], dx3), job=to_owner(["w_dn", "w_gu"]))
    do = _mm(dx2, wl["w_sq"][2], mode="nt", out_dtype=BF16, name="mm_o_dx")
    dw_o = _mm(sv["o"], dx2, mode="tn", out_dtype=BF16, name="mm_o_dw")
    dq, dkv = _xa_bwd(sv["qx"], sv["kv"], do)
    dw_q = _mm(sv["h2"], dq, mode="tn", out_dtype=BF16, name="mm_q_dw")
    dx1, g["norm_xa_g"] = _mm(dq, wl["w_sq"][1], mode="nt", name="mm_q_dx", rms_bwd=(sv["x1"], wl["norm_xa_g"], dx2))
    dw_kv = _mm(sv["mn"], dkv, mode="tn", b_kind="batch", out_dtype=BF16, name="mm_kv_dw")
    dmn = _mm(dkv, wl["w_sq"][3:5], mode="nt", a_kind="kchunk", b_kind="kchunk", name="mm_kv_dx")
    _, g["mem_norm_g"] = _rms_bwd(mem, wl["mem_norm_g"], dmn, jnp.zeros_like(mem), name="rms_mem_bwd")
    dm = _mm(dx1, wl["w_sq"][0], mode="nt", name="mm_out_dx")
    dw_out = _mm(sv["merged"], dx1, mode="tn", out_dtype=BF16, name="mm_out_dw")
    g["w_sq"] = jnp.concatenate([jnp.stack([dw_out, dw_q, dw_o]), dw_kv])
    dbd, dgates = _merge_bwd(p, sv["bd"], dm)
    dbr = mm(dbd, wl["w_br"], mode="nt", a_kind="batch", b_kind="batch", name="mm_branch_dx", job=to_owner(["w_sq"]))
    g["w_br"] = _mm(sv["br"], dbd, mode="tn", a_kind="batch", b_kind="batch", out_dtype=BF16, name="mm_branch_dw")
    dq, dk, dv = carrying([exchange(), to_owner(["w_br"])], lambda r: _sb2_bwd(p, dbr, sv["tot"], sv["cnt"], r))
    dz, g["sgu_ln_g"], g["sgu_ln_b"], g["w_spatial"], g["b_spatial"] = _sgu_bwd(
        p, dbr[1], wl["sgu_ln_g"], wl["sgu_ln_b"], wl["w_spatial"], wl["w_spatial"].transpose(0, 2, 1), sv["b_st"])
    dcb, dcc, dcx, g["conv_w"] = _conv_bwd(p, wl["conv_w"], dbr[2])
    dp = jnp.concatenate([dq, dk.astype(BF16), dv.astype(BF16), dz, dcb, dcc, dcx, dgates], axis=1)
    g["w_in"] = mm(sv["h1"], dp, mode="tn", out_dtype=BF16, name="mm_in_dw", jobs=[exchange(), exchange()])
    dx, g["norm_mix_g"] = mm(dp, wl["w_in"], mode="nt", name="mm_in_dx", rms_bwd=(sv["x"], wl["norm_mix_g"], dx1),
                             job=to_owner(["w_in"]))
    return dx, g


def _local_step(x, mem, target, layers, final_g):
    h, saved = x, []
    for wl in layers:
        h, sv, _, _ = _layer_fwd(h, mem, wl)
        saved.append(sv)
    loss, dx, d_final = _loss_head(h, final_g, target)
    grads = [None] * len(layers)
    for l in reversed(range(len(layers))):
        dx, grads[l] = _layer_bwd(dx, mem, saved[l])
    return loss, dx, grads, d_final


_ALL = slice(None)
CONV_ROWS = 8
_SHARD = {
    "w_in": lambda b: (_ALL, pl.ds(1792 * b, 1792)),
    "w_br": lambda b: (_ALL, _ALL, pl.ds(256 * b, 256)),
    "w_sq": lambda b: (_ALL, pl.ds(256 * b, 256), _ALL),
    "w_gu": lambda b: (b,),
    "w_dn": lambda b: (b,),
    "conv_w": lambda b: (_ALL, pl.ds(128 * b, 128)),
}
_FULL_SHAPE = {"w_in": (1024, 7168), "w_br": (3, 512, 1024), "w_sq": (5, 1024, 1024), "w_gu": (4, 2, 704, 1024),
               "w_dn": (4, 704, 1024), "conv_w": (CONV_ROWS, 512)}
_SHARD_SHAPE = {"w_in": (1024, 1792), "w_br": (3, 512, 256), "w_sq": (5, 256, 1024), "w_gu": (2, 704, 1024),
                "w_dn": (704, 1024), "conv_w": (CONV_ROWS, 128)}


def _pos():
    return lax.axis_index("x"), lax.axis_index("y"), lax.axis_index("c")


def _per_chip(fn):
    x, y, _ = _pos()
    for x0 in (0, 1):
        for y0 in (0, 1):
            @pl.when((x == x0) & (y == y0))
            def _():
                fn(x0, y0)


def _other_chips(x0, y0):
    return [(1 - x0, y0), (x0, 1 - y0), (1 - x0, 1 - y0)]


def _rcopy(src, dst, ssem, rsem, dev):
    return pltpu.make_async_remote_copy(src_ref=src, dst_ref=dst, send_sem=ssem, recv_sem=rsem, device_id=dev, device_id_type=MESH)


def _dma_sems(n):
    return pltpu.SemaphoreType.DMA((n,))


def _gather_rider(layer, placed):
    names = list(placed)
    n = len(names)
    shard = lambda refs, a, b: refs[a].at[_SHARD[names[a]](b)]

    def start(ins, outs, send, recv):
        @pl.when(lax.axis_index("c") == layer)
        def _():
            def run(x0, y0):
                for kk, (px, py) in enumerate(_other_chips(x0, y0)):
                    for a in range(n):
                        own = shard(outs, a, 2 * x0 + y0)
                        _rcopy(own, own, send.at[6 * a + kk], recv.at[6 * a + kk], (px, py, layer)).start()

            _per_chip(run)

    def passing(outs, send, recv, a, kk, bp, x0, y0):
        landed = shard(outs, a, bp)
        return _rcopy(landed, landed, send.at[6 * a + 3 + kk], recv.at[6 * a + 3 + kk], (x0, y0, 1 - layer))

    def middle(ins, outs, send, recv):
        @pl.when(lax.axis_index("c") == layer)
        def _():
            def run(x0, y0):
                for kk, (px, py) in enumerate(_other_chips(x0, y0)):
                    for a in range(n):
                        landed = shard(outs, a, 2 * px + py)
                        _rcopy(landed, landed, send.at[6 * a + kk], recv.at[6 * a + kk], (px, py, layer)).wait_recv()
                        passing(outs, send, recv, a, kk, 2 * px + py, x0, y0).start()

            _per_chip(run)

    def finish(ins, outs, send, recv):
        c = lax.axis_index("c")

        def run(x0, y0):
            chips = _other_chips(x0, y0)

            @pl.when(c == layer)
            def _():
                for kk, (px, py) in enumerate(chips):
                    for a in range(n):
                        own = shard(outs, a, 2 * x0 + y0)
                        _rcopy(own, own, send.at[6 * a + kk], recv.at[6 * a + kk], (px, py, layer)).wait_send()
                        passing(outs, send, recv, a, kk, 2 * px + py, x0, y0).wait_send()

            @pl.when(c != layer)
            def _():
                for kk, (px, py) in enumerate(chips):
                    for a in range(n):
                        got = shard(outs, a, 2 * px + py)
                        _rcopy(got, got, send.at[6 * a + 3 + kk], recv.at[6 * a + 3 + kk], (x0, y0, layer)).wait_recv()

        _per_chip(run)

    arrs = [placed[nm] for nm in names]
    return _Rider(arrs, [jax.ShapeDtypeStruct(t.shape, t.dtype) for t in arrs], 6 * n, start, finish, alias={a: a for a in range(n)}, middle=middle)


def _presum_rider(layer, grads):
    names = list(grads)
    n = len(names)

    def start(ins, outs, send, recv):
        x, y, c = _pos()

        @pl.when(c != layer)
        def _():
            for a in range(n):
                _rcopy(ins[a], outs[a], send.at[a], recv.at[a], (x, y, layer)).start()

    def finish(ins, outs, send, recv):
        x, y, c = _pos()

        @pl.when(c != layer)
        def _():
            for a in range(n):
                _rcopy(ins[a], outs[a], send.at[a], recv.at[a], (x, y, layer)).wait_send()

        @pl.when(c == layer)
        def _():
            for a in range(n):
                _rcopy(outs[a], outs[a], send.at[a], recv.at[a], (x, y, 1 - layer)).wait_recv()

    arrs = [grads[nm] for nm in names]
    return _Rider(arrs, [jax.ShapeDtypeStruct(t.shape, t.dtype) for t in arrs], n, start, finish)


def _shard_rider(layer, part):
    names = list(part)
    n = len(names)

    def each(fn):
        @pl.when(lax.axis_index("c") == layer)
        def _():
            def run(x0, y0):
                for kk, (px, py) in enumerate(_other_chips(x0, y0)):
                    for a in range(n):
                        fn(a, kk, 2 * px + py, (px, py, layer))

            _per_chip(run)

    def start(ins, outs, send, recv):
        each(lambda a, kk, bp, peer: _rcopy(ins[a].at[_SHARD[names[a]](bp)], outs[a].at[kk], send.at[3 * a + kk], recv.at[3 * a + kk], peer).start())

    def finish(ins, outs, send, recv):
        each(lambda a, kk, bp, peer: _rcopy(outs[a].at[kk], outs[a].at[kk], send.at[3 * a + kk], recv.at[3 * a + kk], peer).wait_recv())
        each(lambda a, kk, bp, peer: _rcopy(ins[a].at[_SHARD[names[a]](bp)], outs[a].at[kk], send.at[3 * a + kk], recv.at[3 * a + kk], peer).wait_send())

    return _Rider([part[nm] for nm in names], [jax.ShapeDtypeStruct((N_CHIPS - 1,) + _SHARD_SHAPE[nm], part[nm].dtype) for nm in names],
                  3 * n, start, finish)


def _sibling_exchange(mine0, mine1):
    names = list(mine0)
    n = len(names)

    def body(*refs):
        l0, l1, outs = refs[:n], refs[n:2 * n], refs[2 * n:3 * n]
        send, recv = refs[3 * n:]
        x, y, c = _pos()
        for c0 in (0, 1):
            @pl.when(c == c0)
            def _():
                srcs = l0 if c0 == 0 else l1
                cps = [_rcopy(srcs[a], outs[a], send.at[a], recv.at[a], (x, y, 1 - c0)) for a in range(n)]
                for cp in cps:
                    cp.start()
                for cp in cps:
                    cp.wait()

    outs = pl.pallas_call(
        body, name="grad_sibling_exchange", in_specs=[ANY] * (2 * n), out_specs=[ANY] * n,
        out_shape=[jax.ShapeDtypeStruct(mine0[nm].shape, mine0[nm].dtype) for nm in names],
        scratch_shapes=[_dma_sems(n), _dma_sems(n)],
    )(*[mine0[nm] for nm in names], *[mine1[nm] for nm in names])
    return dict(zip(names, outs))


def _small_rider(pack):
    flips = [(fx, fy, fc) for fx in (0, 1) for fy in (0, 1) for fc in (0, 1) if fx or fy or fc]

    def peers():
        x, y, c = _pos()
        return 4 * x + 2 * y + c, [(x ^ fx, y ^ fy, c ^ fc) for fx, fy, fc in flips]

    def start(ins, outs, send, recv):
        me, to = peers()
        for k, peer in enumerate(to):
            _rcopy(ins[0], outs[0].at[me], send.at[k], recv.at[k], peer).start()

    def finish(ins, outs, send, recv):
        me, to = peers()
        for k, (px, py, pc) in enumerate(to):
            slot = outs[0].at[4 * px + 2 * py + pc]
            _rcopy(slot, slot, send.at[k], recv.at[k], (px, py, pc)).wait_recv()
        for k, peer in enumerate(to):
            _rcopy(ins[0], outs[0].at[me], send.at[k], recv.at[k], peer).wait_send()

    return _Rider([pack], [jax.ShapeDtypeStruct((8,) + pack.shape, pack.dtype)], len(flips), start, finish)


def _sum_devices(gathered, pack, me):
    n, R, C = gathered.shape

    def body(me_ref, r_ref, own_ref, o_ref):
        acc = jnp.where(me_ref[0] == 0, own_ref[...], r_ref[0])
        for s in range(1, n):
            acc = acc + jnp.where(me_ref[0] == s, own_ref[...], r_ref[s])
        o_ref[...] = acc

    return pl.pallas_call(
        body, name="sum_devices_small",
        grid_spec=pltpu.PrefetchScalarGridSpec(num_scalar_prefetch=1, grid=(1,),
                                               in_specs=[pl.BlockSpec((n, R, C), lambda i, m: (0, 0, 0)), pl.BlockSpec((R, C), lambda i, m: (0, 0))],
                                               out_specs=pl.BlockSpec((R, C), lambda i, m: (0, 0))),
        out_shape=jax.ShapeDtypeStruct((R, C), F32), compiler_params=_cp("arbitrary"),
    )(_scalar(me), gathered, pack)


_WEIGHTS = ["norm_mix_g", "w_in", "sgu_ln_g", "sgu_ln_b", "w_spatial", "b_spatial", "conv_w", "w_branch", "w_out", "norm_xa_g",
            "mem_norm_g", "w_q_xa", "w_k_xa", "w_v_xa", "w_o_xa", "norm_ffn_g", "w_gate_ffn", "w_up_ffn", "w_down_ffn", "final_g"]
_REPLICATED = ["norm_mix_g", "sgu_ln_g", "sgu_ln_b", "w_spatial", "b_spatial", "norm_xa_g", "mem_norm_g", "norm_ffn_g", "final_g"]
_SQUARE = ["w_out", "w_q_xa", "w_o_xa", "w_k_xa", "w_v_xa"]
_BIG = ["w_in", "w_br", "w_sq", "w_gu", "w_dn"]


def _pack(arrs):
    return jnp.concatenate([a.reshape(-1) for a in arrs]).reshape(-1, 128)


def _step(a):
    w = {n: a[n] for n in _WEIGHTS}
    x, mem, target = a["x"][0], a["mem"][0], a["loss_target"][0]
    xi, yi, ci = _pos()
    bi = 2 * xi + yi
    groups = list(_FULL_SHAPE)

    tr = lambda t: jnp.swapaxes(t, 1, 2)
    local = {"w_in": w["w_in"], "w_br": w["w_branch"], "w_sq": jnp.stack([w[n] for n in _SQUARE], axis=1),
             "w_gu": jnp.stack([tr(w["w_gate_ffn"]), tr(w["w_up_ffn"])], axis=1), "w_dn": w["w_down_ffn"],
             "conv_w": jnp.pad(w["conv_w"], ((0, 0), (0, CONV_ROWS - 3), (0, 0)))}
    placed = [dict(), dict()]
    for n in groups:
        placed[0][n], placed[1][n] = _place(n, local[n], bi)

    def gather(l, names):
        return _gather_rider(l, {n: placed[l][n] for n in names})

    def start_of(l, w_in):
        return {"w_in": w_in, **{n: w[n][l] for n in _REPLICATED if n != "final_g"}}

    w_in0, = _run_rider(gather(0, ["w_in"]), name="gather_first")
    h, sv0, (w_in1,), h1 = _layer_fwd(x, mem, start_of(0, w_in0), {"mm_in": gather(0, _GATHER_LATE), "sb": gather(0, _GATHER_LAST),
                                                                    "mm_gu": gather(1, ["w_in"])}, next_gain=w["norm_mix_g"][1])
    h, sv1, _, _ = _layer_fwd(h, mem, start_of(1, w_in1), {"mm_in": gather(1, _GATHER_LATE), "sb": gather(1, _GATHER_LAST), "mm_gu": None}, h1=h1)
    loss, dx, d_final = _loss_head(h, w["final_g"], target)
    loss = lax.psum(loss, ("x", "y", "c"))

    pipe = _GradPipe(ci, bi)
    dx, g1 = _layer_bwd(dx, mem, sv1, 1, pipe)
    dx, g0 = _layer_bwd(dx, mem, sv0, 0, pipe)
    grads = [g0, g1]
    small = {n: jnp.stack([g[n] for g in grads]) for n in _REPLICATED if n != "final_g"}
    small["final_g"] = d_final
    conv_g = jnp.stack([g["conv_w"] for g in grads])
    small_pack = _pack([small[n] for n in _REPLICATED] + [conv_g])
    gathered, = pipe.drain(_small_rider(small_pack))
    mine = [pipe.reduced(0), pipe.reduced(1)]
    theirs = _sibling_exchange(mine[0], mine[1])

    out = {}

    def adam_layers(name, group, pick=None, view=lambda t: t):
        sel = (lambda t: t[group]) if pick is None else (lambda t: t[group][pick])
        res = _adam_layers([sel(mine[0]), sel(mine[1])], sel(theirs), ci, view(w[name]), view(a["m_" + name]), view(a["v_" + name]), name="adam_" + name)
        out[name] = tuple(view(r) for r in res)

    adam_layers("w_in", "w_in")
    adam_layers("w_branch", "w_br")
    for t, n in enumerate(_SQUARE):
        adam_layers(n, "w_sq", t)
    adam_layers("w_gate_ffn", "w_gu", 0, tr)
    adam_layers("w_up_ffn", "w_gu", 1, tr)
    adam_layers("w_down_ffn", "w_dn")

    def adam(name, g):
        out[name] = _reduce_adam([g], w[name], a["m_" + name], a["v_" + name], name="adam_" + name)

    n_rep = sum(w[n].size for n in _REPLICATED) // 128
    summed = _sum_devices(gathered, small_pack, 4 * xi + 2 * yi + ci)
    res = _reduce_adam([summed[:n_rep]], _pack([w[n] for n in _REPLICATED]), _pack([a["m_" + n] for n in _REPLICATED]),
                       _pack([a["v_" + n] for n in _REPLICATED]), name="adam_replicated")
    off = 0
    for n in _REPLICATED:
        out[n] = tuple(r.reshape(-1)[off:off + w[n].size].reshape(w[n].shape) for r in res)
        off += w[n].size
    conv_full = summed[n_rep:].reshape(conv_g.shape)
    adam("conv_w", lax.dynamic_slice_in_dim(conv_full, (2 * xi + yi) * 128, 128, axis=2))

    return (loss, dx[None], *[out[n][k] for k in range(4) for n in _WEIGHTS])


def kernel(x, mem, norm_mix_g, w_in, sgu_ln_g, sgu_ln_b, w_spatial, b_spatial, conv_w, w_branch, w_out, norm_xa_g, mem_norm_g, w_q_xa, w_k_xa, w_v_xa, w_o_xa, norm_ffn_g, w_gate_ffn, w_up_ffn, w_down_ffn, final_g, loss_target, m_norm_mix_g, m_w_in, m_sgu_ln_g, m_sgu_ln_b, m_w_spatial, m_b_spatial, m_conv_w, m_w_branch, m_w_out, m_norm_xa_g, m_mem_norm_g, m_w_q_xa, m_w_k_xa, m_w_v_xa, m_w_o_xa, m_norm_ffn_g, m_w_gate_ffn, m_w_up_ffn, m_w_down_ffn, m_final_g, v_norm_mix_g, v_w_in, v_sgu_ln_g, v_sgu_ln_b, v_w_spatial, v_b_spatial, v_conv_w, v_w_branch, v_w_out, v_norm_xa_g, v_mem_norm_g, v_w_q_xa, v_w_k_xa, v_w_v_xa, v_w_o_xa, v_norm_ffn_g, v_w_gate_ffn, v_w_up_ffn, v_w_down_ffn, v_final_g):
    return _step(dict(locals()))
```

```python
import math

import jax
import jax.numpy as jnp
from jax import lax
from jax.experimental import pallas as pl
from jax.experimental.pallas import tpu as pltpu

F32, BF16 = jnp.float32, jnp.bfloat16
MESH = pl.DeviceIdType.MESH
ANY = pl.BlockSpec(memory_space=pl.ANY)

D_MODEL = 1024
DEPTH = 2
BW = 512
SB_HEADS, SB_DH = 8, 64
SGU_LEN, SGU_GROUPS, SGU_GD, SGU_CHUNK = 128, 4, 128, 64
XA_HEADS, XA_DH = 4, 256
FFN_SH = 704
N_CHIPS = 4
C_Z, C_CB, C_GATES = 1536, 2560, 4096

ADAM_LR, ADAM_B1, ADAM_B2, ADAM_EPS, ADAM_WD, ADAM_STEP = 0.001, 0.9, 0.999, 1e-08, 0.01, 10

VMEM_LIMIT_V7X = 56 * 1024 * 1024

NN = (((1,), (0,)), ((), ()))
NT = (((1,), (1,)), ((), ()))
TN = (((0,), (0,)), ((), ()))


def _cp(*sem):
    return pltpu.CompilerParams(dimension_semantics=sem, vmem_limit_bytes=VMEM_LIMIT_V7X)


def _tile(n, pref):
    for t in pref:
        if n % t == 0:
            return t
    return n


def _rows(r, row_bytes, block_bytes=1 << 20):
    fits = [t for t in range(8, r + 1, 8) if r % t == 0 and t * row_bytes <= block_bytes]
    return max(fits) if fits else r


class _Rider:
    def __init__(self, ins, outs, n_sems, start, finish, alias=None, middle=None):
        self.ins, self.outs, self.n_sems, self.alias = list(ins), list(outs), n_sems, alias or {}
        self.start, self.middle, self.finish = start, middle, finish


class _Sems:
    def __init__(self, ref, first):
        self.ref, self.first = ref, first

    @property
    def at(self):
        return self

    def __getitem__(self, k):
        return self.ref.at[self.first + k]


def _join(riders):
    riders = [r for r in riders if r is not None]
    if not riders:
        return None
    spans, i0, o0, s0 = [], 0, 0, 0
    for r in riders:
        spans.append((r, i0, o0, s0))
        i0, o0, s0 = i0 + len(r.ins), o0 + len(r.outs), s0 + r.n_sems

    def phase(which):
        def run(ins, outs, send, recv):
            for r, i, o, s in spans:
                fn = getattr(r, which)
                if fn is not None:
                    fn(ins[i:i + len(r.ins)], outs[o:o + len(r.outs)], _Sems(send, s), _Sems(recv, s))
        return run

    joined = _Rider([a for r in riders for a in r.ins], [a for r in riders for a in r.outs], s0, phase("start"), phase("finish"),
                    alias={o + k: i + v for r, i, o, s in spans for k, v in r.alias.items()}, middle=phase("middle"))
    joined.split = lambda landed: [list(landed[o:o + len(r.outs)]) for r, i, o, s in spans]
    return joined


def _call_with_rider(rider, body, *, name, grid, in_specs, args, out_specs, out_shape, scratch_shapes, semantics):
    if rider is None:
        return pl.pallas_call(body, name=name, grid=grid, in_specs=in_specs, out_specs=out_specs, out_shape=out_shape,
                              scratch_shapes=scratch_shapes, compiler_params=_cp(*semantics))(*args)
    n_in, n_out, r_in, r_out = len(args), len(out_shape), len(rider.ins), len(rider.outs)

    def riding(*refs):
        ins, rins = refs[:n_in], refs[n_in:n_in + r_in]
        outs, routs = refs[n_in + r_in:n_in + r_in + n_out], refs[n_in + r_in + n_out:n_in + r_in + n_out + r_out]
        rest = refs[n_in + r_in + n_out + r_out:]
        scratch, send, recv = rest[:-2], rest[-2], rest[-1]
        step = pl.program_id(0)
        for ax in range(1, len(grid)):
            step = step * grid[ax] + pl.program_id(ax)
        n_steps = math.prod(grid)

        @pl.when(step == 0)
        def _():
            rider.start(rins, routs, send, recv)

        body(*ins, *outs, *scratch)

        if rider.middle is not None:
            @pl.when(step == (9 * n_steps) // 10)
            def _():
                rider.middle(rins, routs, send, recv)

        @pl.when(step == n_steps - 1)
        def _():
            rider.finish(rins, routs, send, recv)

    return pl.pallas_call(
        riding, name=name, grid=grid, in_specs=list(in_specs) + [ANY] * r_in, out_specs=list(out_specs) + [ANY] * r_out,
        out_shape=list(out_shape) + rider.outs, scratch_shapes=list(scratch_shapes) + [_dma_sems(rider.n_sems), _dma_sems(rider.n_sems)],
        input_output_aliases={n_in + i: n_out + o for o, i in rider.alias.items()},
        compiler_params=_cp(*["arbitrary"] * len(grid)),
    )(*args, *rider.ins)


def _run_rider(rider, *, name):
    def nothing(*refs):
        pass

    return _call_with_rider(rider, nothing, name=name, grid=(1,), in_specs=[], args=[], out_specs=[], out_shape=[], scratch_shapes=[],
                            semantics=("arbitrary",))


def _mm(a, b, *, mode, name, out_dtype=F32, res=None, norm=None, rms_bwd=None, a_kind="2d", b_kind="2d", tm=None, tn=None, tk=None, rider=None):
    a2, b2 = a.shape[-2:], b.shape[-2:]
    if mode == "nn":
        (M, K), N = a2, b2[1]
    elif mode == "nt":
        (M, K), N = a2, b2[0]
    else:
        (K, M), N = a2, b2[1]
    kchunk = a_kind == "kchunk" or b_kind == "kchunk"
    batch = a_kind == "batch" or b_kind == "batch"
    G = (a.shape[0] if a_kind == "batch" else b.shape[0]) if batch else 1
    tm = tm or _tile(M, (1024, 512, 256, 128))
    tn = tn or _tile(N, (1024, 512, 256, 128))
    if kchunk:
        tk, nk = K, (a.shape[0] if a_kind == "kchunk" else b.shape[0])
    else:
        tk = tk or _tile(K, (1024, 512, 256, 128))
        nk = K // tk

    def spec(kind, blk, idx):
        if kind == "2d":
            return pl.BlockSpec(blk, lambda g, i, j, k: idx(g, i, j, k))
        if kind == "batch":
            return pl.BlockSpec((None,) + blk, lambda g, i, j, k: (g,) + idx(g, i, j, k))
        return pl.BlockSpec((None,) + blk, lambda g, i, j, k: (k,) + idx(g, i, j, 0))

    if mode == "nn":
        a_spec = spec(a_kind, (tm, tk), lambda g, i, j, k: (i, k))
        b_spec = spec(b_kind, (tk, tn), lambda g, i, j, k: (k, j))
    elif mode == "nt":
        a_spec = spec(a_kind, (tm, tk), lambda g, i, j, k: (i, k))
        b_spec = spec(b_kind, (tn, tk), lambda g, i, j, k: (j, k))
    else:
        a_spec = spec(a_kind, (tk, tm), lambda g, i, j, k: (k, i))
        b_spec = spec(b_kind, (tk, tn), lambda g, i, j, k: (k, j))
    o_kind = "batch" if batch else "2d"
    o_spec = spec(o_kind, (tm, tn), lambda g, i, j, k: (i, j))
    o_shape = ((G,) if batch else ()) + (M, N)
    dn = {"nn": NN, "nt": NT, "tn": TN}[mode]
    has_res, has_rms, has_norm = res is not None, rms_bwd is not None, norm is not None
    assert not (has_res and has_rms) and (not (has_rms or has_norm) or (tn == N and G == 1)) and (has_res or not has_norm)

    def body(*refs):
        if has_norm:
            a_ref, b_ref, r_ref, g_ref, o_ref, h_ref = refs[:6]
        elif has_res:
            a_ref, b_ref, r_ref, o_ref = refs[:4]
        elif has_rms:
            a_ref, b_ref, x_ref, g_ref, dres_ref, o_ref, dg_ref = refs[:7]
        else:
            a_ref, b_ref, o_ref = refs[:3]
        p = lax.dot_general(a_ref[...].astype(BF16), b_ref[...].astype(BF16), dn, preferred_element_type=F32)
        first_rows = pl.program_id(1) == 0

        def finish(r):
            if has_rms:
                xv = x_ref[...]
                rs = lax.rsqrt(jnp.mean(xv * xv, axis=-1, keepdims=True) + 1e-6)
                u = r * g_ref[...]
                s = jnp.sum(u * xv, axis=-1, keepdims=True)
                o_ref[...] = dres_ref[...] + rs * u - xv * ((rs * rs * rs) * (s * (1.0 / N)))
                part = jnp.sum(r * (xv * rs), axis=0, keepdims=True)

                @pl.when(first_rows)
                def _():
                    dg_ref[...] = part

                @pl.when(jnp.logical_not(first_rows))
                def _():
                    dg_ref[...] += part
                return
            if has_res:
                r = r + r_ref[...]
            o_ref[...] = r.astype(out_dtype)
            if has_norm:
                h_ref[...] = (r * lax.rsqrt(jnp.mean(r * r, axis=-1, keepdims=True) + 1e-6) * g_ref[...]).astype(BF16)

        if nk == 1:
            finish(p)
        else:
            acc = refs[-1]
            k = pl.program_id(3)

            @pl.when(k == 0)
            def _():
                acc[...] = p

            @pl.when(k > 0)
            def _():
                acc[...] += p

            @pl.when(k == nk - 1)
            def _():
                finish(acc[...])

    in_specs, args = [a_spec, b_spec], [a, b]
    tile = spec("2d", (tm, tn), lambda g, i, j, k: (i, j))
    out_specs, out_shape = [o_spec], [jax.ShapeDtypeStruct(o_shape, out_dtype)]
    vec = pl.BlockSpec((1, tn), lambda g, i, j, k: (0, 0))
    if has_res:
        in_specs.append(tile)
        args.append(res)
    if has_norm:
        in_specs.append(vec)
        args.append(norm.reshape(1, N))
        out_specs.append(tile)
        out_shape.append(jax.ShapeDtypeStruct((M, N), BF16))
    if has_rms:
        x, gain, dres = rms_bwd
        in_specs += [tile, vec, tile]
        args += [x, gain.reshape(1, N), dres]
        out_specs.append(vec)
        out_shape.append(jax.ShapeDtypeStruct((1, N), F32))
    outs = _call_with_rider(
        rider, body, name=name, grid=(G, M // tm, N // tn, nk), in_specs=in_specs, args=args, out_specs=out_specs,
        out_shape=out_shape, scratch_shapes=[pltpu.VMEM((tm, tn), F32)] if nk > 1 else [],
        semantics=("parallel", "arbitrary" if has_rms else "parallel", "parallel", "arbitrary"))
    main = (outs[0], outs[1].reshape(N)) if has_rms else (outs[0], outs[1]) if has_norm else outs[0]
    return main if rider is None else (main, outs[len(out_shape):])


def _rms_fwd(x, g, *, name):
    S, Dm = x.shape
    tm = _tile(S, (512, 256))

    def body(x_ref, g_ref, o_ref):
        xv = x_ref[...]
        r = lax.rsqrt(jnp.mean(xv * xv, axis=-1, keepdims=True) + 1e-6)
        o_ref[...] = (xv * r * g_ref[...]).astype(BF16)

    return pl.pallas_call(
        body, name=name, grid=(S // tm,),
        in_specs=[pl.BlockSpec((tm, Dm), lambda i: (i, 0)), pl.BlockSpec((1, Dm), lambda i: (0, 0))],
        out_specs=pl.BlockSpec((tm, Dm), lambda i: (i, 0)), out_shape=jax.ShapeDtypeStruct((S, Dm), BF16),
        compiler_params=_cp("parallel"),
    )(x, g.reshape(1, Dm))


def _rms_bwd(x, g, dh, dres, *, name):
    S, Dm = x.shape
    tm = _tile(S, (512, 256))

    def body(x_ref, g_ref, dh_ref, dr_ref, dx_ref, dg_ref):
        xv, dhv = x_ref[...], dh_ref[...].astype(F32)
        r = lax.rsqrt(jnp.mean(xv * xv, axis=-1, keepdims=True) + 1e-6)
        u = dhv * g_ref[...]
        s = jnp.sum(u * xv, axis=-1, keepdims=True)
        dx_ref[...] = dr_ref[...] + r * u - xv * ((r * r * r) * (s * (1.0 / Dm)))
        part = jnp.sum(dhv * (xv * r), axis=0, keepdims=True)

        @pl.when(pl.program_id(0) == 0)
        def _():
            dg_ref[...] = part

        @pl.when(pl.program_id(0) > 0)
        def _():
            dg_ref[...] += part

    row = pl.BlockSpec((tm, Dm), lambda i: (i, 0))
    vec = pl.BlockSpec((1, Dm), lambda i: (0, 0))
    dx, dg = pl.pallas_call(
        body, name=name, grid=(S // tm,), in_specs=[row, vec, row, row], out_specs=[row, vec],
        out_shape=[jax.ShapeDtypeStruct((S, Dm), F32), jax.ShapeDtypeStruct((1, Dm), F32)],
        compiler_params=_cp("arbitrary"),
    )(x, g.reshape(1, Dm), dh, dres)
    return dx, dg.reshape(Dm)


def _loss_head(x, g, target):
    S, Dm = x.shape
    tm = _tile(S, (512, 256))

    def body(x_ref, g_ref, t_ref, dx_ref, dg_ref, loss_ref):
        xv, gv = x_ref[...], g_ref[...]
        r = lax.rsqrt(jnp.mean(xv * xv, axis=-1, keepdims=True) + 1e-6)
        xn = xv * r
        err = xn * gv - t_ref[...]
        lpart = 0.5 * jnp.sum(jnp.mean(err * err, axis=-1, keepdims=True), axis=0, keepdims=True)
        dy = err * (1.0 / Dm)
        u = dy * gv
        s = jnp.sum(u * xv, axis=-1, keepdims=True)
        dx_ref[...] = r * u - xv * ((r * r * r) * (s * (1.0 / Dm)))
        part = jnp.sum(dy * xn, axis=0, keepdims=True)
        lslab = jnp.broadcast_to(lpart, (8, 128))

        @pl.when(pl.program_id(0) == 0)
        def _():
            dg_ref[...] = part
            loss_ref[...] = lslab

        @pl.when(pl.program_id(0) > 0)
        def _():
            dg_ref[...] += part
            loss_ref[...] += lslab

    row = pl.BlockSpec((tm, Dm), lambda i: (i, 0))
    vec = pl.BlockSpec((1, Dm), lambda i: (0, 0))
    dx, dg, loss = pl.pallas_call(
        body, name="loss_head", grid=(S // tm,), in_specs=[row, vec, row],
        out_specs=[row, vec, pl.BlockSpec((8, 128), lambda i: (0, 0))],
        out_shape=[jax.ShapeDtypeStruct((S, Dm), F32), jax.ShapeDtypeStruct((1, Dm), F32), jax.ShapeDtypeStruct((8, 128), F32)],
        compiler_params=_cp("arbitrary"),
    )(x, g.reshape(1, Dm), target)
    return loss[0, 0], dx, dg.reshape(Dm)


SB_TQ, SB_TK = 256, 256
SB_EXP_FLOOR = -104.0


def _split2(v):
    hi = v.astype(BF16)
    return jnp.concatenate([hi, (v - hi.astype(F32)).astype(BF16)], axis=1)


def _tri2(cmp):
    j = lax.broadcasted_iota(jnp.int32, (2 * SB_TK, SB_TK), 0) % SB_TK
    s = lax.broadcasted_iota(jnp.int32, (2 * SB_TK, SB_TK), 1)
    return cmp(j, s).astype(BF16)


def _sb_scores(qv, kb, k0, q0, tq):
    rows = qv.shape[0]
    z = lax.dot_general(qv, kb, NT, preferred_element_type=F32) * (SB_DH ** -0.5)
    t_pos = q0 + lax.broadcasted_iota(jnp.int32, (rows, SB_TK), 0) % tq
    s_pos = k0 + lax.broadcasted_iota(jnp.int32, (rows, SB_TK), 1)
    valid = s_pos < t_pos
    ls = jnp.minimum(z, 0.0) - jnp.log(1.0 + jnp.exp(-jnp.abs(z)))
    l1m = jnp.where(valid, ls - z, 0.0)
    return z, valid, ls, l1m


SB_PAIRS = SB_HEADS // 2
_Q_BLK, _K_BLK, _V_BLK = 0, SB_PAIRS, 2 * SB_PAIRS


def _wide(x):
    return x if SB_TK == 128 else jnp.concatenate([x] * (SB_TK // 128), axis=1)


def _lanes_of(h, shape):
    lane = lax.broadcasted_iota(jnp.int32, shape, len(shape) - 1)
    return (lane < SB_DH) if h == 0 else (lane >= SB_DH)


def _sb2_fwd(p, rider=None):
    S = p.shape[0]
    tq = min(SB_TQ, S)
    kb_per_q = tq // SB_TK

    def body(q_ref, k_ref, v_ref, o_ref, tot_ref, cnt_ref, qm, acc, c):
        i = pl.program_id(1)
        q0 = i * tq
        later = _tri2(lambda j, s: j > s)
        q2 = q_ref[...]
        for h in range(2):
            qm[h * tq:(h + 1) * tq, :] = jnp.where(_lanes_of(h, q2.shape), q2, 0.0).astype(BF16)
        acc[...] = jnp.zeros_like(acc)
        c[...] = jnp.zeros_like(c)
        nkb = (i + 1) * kb_per_q

        def more(st):
            n, highest = st
            return (n < nkb) & (highest > SB_EXP_FLOOR)

        def step(st):
            n, _ = st
            k0 = pl.multiple_of((nkb - 1 - n) * SB_TK, SB_TK)
            kb, vb = k_ref[pl.ds(k0, SB_TK), :].astype(BF16), v_ref[pl.ds(k0, SB_TK), :].astype(BF16)
            c_old = c[...]
            z, valid, ls, l1m = _sb_scores(qm[...], kb, k0, q0, tq)
            c_new = c_old + jnp.sum(l1m, axis=1, keepdims=True)
            after = jnp.dot(_split2(l1m), later, preferred_element_type=F32)
            a = jnp.where(valid, jnp.exp(ls + after + _wide(c_old)), 0.0)
            av = jnp.dot(a.astype(BF16), vb, preferred_element_type=F32)
            acc[...] += jnp.where(_lanes_of(0, (tq, 128)), av[:tq], av[tq:])
            c[...] = c_new
            return n + 1, jnp.max(c_new)

        n_done, _ = lax.while_loop(more, step, (jnp.int32(0), jnp.float32(0.0)))
        o_ref[...] = acc[...].astype(o_ref.dtype)
        for h in range(2):
            tot_ref[h] = c[h * tq:(h + 1) * tq, :]
        cnt_ref[...] = jnp.full(cnt_ref.shape, n_done.astype(F32))

    col = lambda first: pl.BlockSpec((S, 128), lambda g, i: (0, first + g))
    outs = _call_with_rider(
        rider, body, name="sb_fwd", grid=(SB_PAIRS, S // tq), args=[p, p, p],
        in_specs=[pl.BlockSpec((tq, 128), lambda g, i: (i, _Q_BLK + g)), col(_K_BLK), col(_V_BLK)],
        out_specs=[pl.BlockSpec((tq, 128), lambda g, i: (i, g)), pl.BlockSpec((2, tq, 128), lambda g, i: (g, i, 0)),
                   pl.BlockSpec((None, None, 8, 128), lambda g, i: (g, i, 0, 0))],
        out_shape=[jax.ShapeDtypeStruct((S, BW), BF16), jax.ShapeDtypeStruct((SB_HEADS, S, 128), F32),
                   jax.ShapeDtypeStruct((SB_PAIRS, S // tq, 8, 128), F32)],
        scratch_shapes=[pltpu.VMEM((2 * tq, 128), BF16), pltpu.VMEM((tq, 128), F32), pltpu.VMEM((2 * tq, 128), F32)],
        semantics=("parallel", "parallel"))
    return outs[:3], outs[3:]


def _sb2_bwd(p, dbr, tot, cnt, rider=None):
    S = p.shape[0]
    tq = min(SB_TQ, S)
    kb_per_q = tq // SB_TK
    scale = SB_DH ** -0.5

    def body(q_ref, k_ref, v_ref, do_ref, tot_ref, cnt_ref, dq_ref, dk_ref, dv_ref, qm, dom, tot, dq_acc, pre, gpre):
        i = pl.program_id(1)
        q0 = i * tq
        upto = _tri2(lambda j, s: j <= s)
        before = _tri2(lambda j, s: j < s)

        @pl.when(i == 0)
        def _():
            dk_ref[...] = jnp.zeros_like(dk_ref)
            dv_ref[...] = jnp.zeros_like(dv_ref)

        q2, do2 = q_ref[...], do_ref[...]
        for h in range(2):
            rows = slice(h * tq, (h + 1) * tq)
            qm[rows, :] = jnp.where(_lanes_of(h, q2.shape), q2, 0.0).astype(BF16)
            dom[rows, :] = jnp.where(_lanes_of(h, do2.shape), do2, 0.0).astype(BF16)
            tot[rows, :] = tot_ref[h]
        dq_acc[...] = jnp.zeros_like(dq_acc)
        pre[...] = jnp.zeros_like(pre)
        gpre[...] = jnp.zeros_like(gpre)

        n_done = jnp.max(cnt_ref[...]).astype(jnp.int32)
        first = (i + 1) * kb_per_q - n_done

        def step(n, carry):
            k0 = pl.multiple_of((first + n) * SB_TK, SB_TK)
            kb, vb = k_ref[pl.ds(k0, SB_TK), :].astype(BF16), v_ref[pl.ds(k0, SB_TK), :].astype(BF16)
            pre_o, gpre_o = pre[...], gpre[...]
            z, valid, ls, l1m = _sb_scores(qm[...], kb, k0, q0, tq)
            incl = jnp.dot(_split2(l1m), upto, preferred_element_type=F32)
            rest = _wide(tot[...] - pre_o) - incl
            a = jnp.where(valid, jnp.exp(ls + rest), 0.0)
            da = lax.dot_general(dom[...], vb, NT, preferred_element_type=F32)
            g = a * da
            gbefore = jnp.dot(_split2(g), before, preferred_element_type=F32) + _wide(gpre_o)
            dz = jnp.where(valid, g * jnp.exp(ls - z) - jnp.exp(ls) * gbefore, 0.0) * scale
            dzb = dz.astype(BF16)
            dq_p = jnp.dot(dzb, kb, preferred_element_type=F32)
            dq_acc[...] += jnp.where(_lanes_of(0, (tq, 128)), dq_p[:tq], dq_p[tq:])
            dk_ref[pl.ds(k0, SB_TK), :] += lax.dot_general(dzb, qm[...], TN, preferred_element_type=F32)
            dv_ref[pl.ds(k0, SB_TK), :] += lax.dot_general(a.astype(BF16), dom[...], TN, preferred_element_type=F32)
            pre[...] = pre_o + jnp.sum(l1m, axis=1, keepdims=True)
            gpre[...] = gpre_o + jnp.sum(g, axis=1, keepdims=True)
            return carry

        lax.fori_loop(0, n_done, step, 0)
        dq_ref[...] = dq_acc[...].astype(dq_ref.dtype)

    col = lambda first: pl.BlockSpec((S, 128), lambda g, i: (0, first + g))
    tile = pl.BlockSpec((tq, 128), lambda g, i: (i, g))
    whole = pl.BlockSpec((S, 128), lambda g, i: (0, g))
    outs = _call_with_rider(
        rider, body, name="sb_bwd", grid=(SB_PAIRS, S // tq), args=[p, p, p, dbr, tot, cnt],
        in_specs=[pl.BlockSpec((tq, 128), lambda g, i: (i, _Q_BLK + g)), col(_K_BLK), col(_V_BLK),
                  pl.BlockSpec((None, tq, 128), lambda g, i: (0, i, g)), pl.BlockSpec((2, tq, 128), lambda g, i: (g, i, 0)),
                  pl.BlockSpec((None, None, 8, 128), lambda g, i: (g, i, 0, 0))],
        out_specs=[tile, whole, whole],
        out_shape=[jax.ShapeDtypeStruct((S, BW), BF16), jax.ShapeDtypeStruct((S, BW), F32), jax.ShapeDtypeStruct((S, BW), F32)],
        scratch_shapes=[pltpu.VMEM((2 * tq, 128), BF16), pltpu.VMEM((2 * tq, 128), BF16), pltpu.VMEM((2 * tq, 128), F32),
                        pltpu.VMEM((tq, 128), F32), pltpu.VMEM((2 * tq, 128), F32), pltpu.VMEM((2 * tq, 128), F32)],
        semantics=("parallel", "arbitrary"))
    return outs[:3], outs[3:]


_INV_SQRT2 = 0.7071067811865476
_INV_SQRT2PI = 0.3989422804014327


def _gelu(x):
    return 0.5 * x * (1.0 + lax.erf(x * _INV_SQRT2))


def _gelu_grad(x):
    return 0.5 * (1.0 + lax.erf(x * _INV_SQRT2)) + x * (_INV_SQRT2PI * jnp.exp(-0.5 * x * x))


def _sgu_mask():
    t = lax.broadcasted_iota(jnp.int32, (SGU_LEN, SGU_LEN), 0) // SGU_CHUNK
    s = lax.broadcasted_iota(jnp.int32, (SGU_LEN, SGU_LEN), 1) // SGU_CHUNK
    return t >= s


def _sgu_mask_t():
    t = lax.broadcasted_iota(jnp.int32, (SGU_LEN, SGU_LEN), 0) // SGU_CHUNK
    s = lax.broadcasted_iota(jnp.int32, (SGU_LEN, SGU_LEN), 1) // SGU_CHUNK
    return s >= t


def _sgu_norm(zv, g, b):
    vv = _gelu(zv)
    xc = vv - jnp.mean(vv, axis=-1, keepdims=True)
    rstd = lax.rsqrt(jnp.mean(xc * xc, axis=-1, keepdims=True) + 1e-5)
    xhat = xc * rstd
    return xhat, rstd, xhat * g + b


SGU_TM = 256


def _sgu_fwd(p, ln_g, ln_b, w_s, b_st):
    S = p.shape[0]
    tm = min(SGU_TM, S)

    def body(zu_ref, zv_ref, g_ref, b_ref, w_ref, bs_ref, o_ref):
        u = _gelu(zu_ref[...].astype(F32))
        _, _, vn = _sgu_norm(zv_ref[...].astype(F32), g_ref[...], b_ref[...])
        vnb = vn.astype(BF16)
        mask = _sgu_mask()
        for gi in range(SGU_GROUPS):
            wg = jnp.where(mask, w_ref[gi], 0.0).astype(BF16)
            cols = slice(gi * SGU_GD, (gi + 1) * SGU_GD)
            for ci in range(tm // SGU_LEN):
                rows = slice(ci * SGU_LEN, (ci + 1) * SGU_LEN)
                vm = jnp.dot(wg, vnb[rows, cols], preferred_element_type=F32) + bs_ref[:, gi:gi + 1]
                o_ref[rows, cols] = (u[rows, cols] * vm).astype(BF16)

    vec = pl.BlockSpec((1, BW), lambda i: (0, 0))
    return pl.pallas_call(
        body, name="sgu_fwd", grid=(S // tm,),
        in_specs=[pl.BlockSpec((tm, BW), lambda i: (i, C_Z // BW)), pl.BlockSpec((tm, BW), lambda i: (i, C_Z // BW + 1)), vec, vec,
                  pl.BlockSpec((SGU_GROUPS, SGU_LEN, SGU_LEN), lambda i: (0, 0, 0)), pl.BlockSpec((SGU_LEN, SGU_GROUPS), lambda i: (0, 0))],
        out_specs=pl.BlockSpec((tm, BW), lambda i: (i, 0)), out_shape=jax.ShapeDtypeStruct((S, BW), BF16),
        compiler_params=_cp("parallel"),
    )(p, p, ln_g.reshape(1, BW), ln_b.reshape(1, BW), w_s, b_st)


def _sgu_bwd(p, dyb, ln_g, ln_b, w_s, w_st, b_st):
    S = p.shape[0]
    tm = min(SGU_TM, S)

    def body(zu_ref, zv_ref, dy_ref, g_ref, b_ref, w_ref, wt_ref, bs_ref, dz_ref, dg_ref, db_ref, dw_ref, dbs_ref, dvn):
        first = pl.program_id(0) == 0

        @pl.when(first)
        def _():
            dg_ref[...] = jnp.zeros_like(dg_ref)
            db_ref[...] = jnp.zeros_like(db_ref)
            dw_ref[...] = jnp.zeros_like(dw_ref)
            dbs_ref[...] = jnp.zeros_like(dbs_ref)

        zu, zv, dy = zu_ref[...].astype(F32), zv_ref[...].astype(F32), dy_ref[...].astype(F32)
        u = _gelu(zu)
        xhat, rstd, vn = _sgu_norm(zv, g_ref[...], b_ref[...])
        vnb = vn.astype(BF16)
        mask = _sgu_mask()
        mask_t = _sgu_mask_t()
        for gi in range(SGU_GROUPS):
            wg = jnp.where(mask, w_ref[gi], 0.0).astype(BF16)
            wgt = jnp.where(mask_t, wt_ref[gi], 0.0).astype(BF16)
            cols = slice(gi * SGU_GD, (gi + 1) * SGU_GD)
            for ci in range(tm // SGU_LEN):
                rows = slice(ci * SGU_LEN, (ci + 1) * SGU_LEN)
                vm = jnp.dot(wg, vnb[rows, cols], preferred_element_type=F32) + bs_ref[:, gi:gi + 1]
                dyc = dy[rows, cols]
                dz_ref[rows, cols] = (dyc * vm * _gelu_grad(zu[rows, cols])).astype(BF16)
                dvm = dyc * u[rows, cols]
                dvmb = dvm.astype(BF16)
                dbs_ref[gi] += jnp.broadcast_to(jnp.sum(dvm, axis=1, keepdims=True), (SGU_LEN, SGU_GD))
                dw_ref[gi] += lax.dot_general(dvmb, vnb[rows, cols], NT, preferred_element_type=F32)
                dvn[rows, cols] = jnp.dot(wgt, dvmb, preferred_element_type=F32)
        dvnv = dvn[...]
        dg_ref[...] += jnp.sum(dvnv * xhat, axis=0, keepdims=True)
        db_ref[...] += jnp.sum(dvnv, axis=0, keepdims=True)
        dxh = dvnv * g_ref[...]
        dvv = rstd * (dxh - jnp.mean(dxh, axis=-1, keepdims=True) - xhat * jnp.mean(dxh * xhat, axis=-1, keepdims=True))
        dz_ref[:, BW:] = (dvv * _gelu_grad(zv)).astype(BF16)

        @pl.when(pl.program_id(0) == n_steps - 1)
        def _():
            for gi in range(SGU_GROUPS):
                dw_ref[gi] = jnp.where(mask, dw_ref[gi], 0.0)

    n_steps = S // tm
    vec = pl.BlockSpec((1, BW), lambda i: (0, 0))
    half = lambda c: pl.BlockSpec((tm, BW), lambda i: (i, c))
    wspec = pl.BlockSpec((SGU_GROUPS, SGU_LEN, SGU_LEN), lambda i: (0, 0, 0))
    dz, dg, db, dw, dbs = pl.pallas_call(
        body, name="sgu_bwd", grid=(n_steps,),
        in_specs=[half(C_Z // BW), half(C_Z // BW + 1), half(0), vec, vec, wspec, wspec,
                  pl.BlockSpec((SGU_LEN, SGU_GROUPS), lambda i: (0, 0))],
        out_specs=[pl.BlockSpec((tm, 2 * BW), lambda i: (i, 0)), vec, vec, wspec, wspec],
        out_shape=[jax.ShapeDtypeStruct((S, 2 * BW), BF16), jax.ShapeDtypeStruct((1, BW), F32), jax.ShapeDtypeStruct((1, BW), F32),
                   jax.ShapeDtypeStruct((SGU_GROUPS, SGU_LEN, SGU_LEN), F32), jax.ShapeDtypeStruct((SGU_GROUPS, SGU_LEN, SGU_GD), F32)],
        scratch_shapes=[pltpu.VMEM((tm, BW), F32)],
        compiler_params=_cp("arbitrary"),
    )(p, p, dyb, ln_g.reshape(1, BW), ln_b.reshape(1, BW), w_s, w_st, b_st)
    return dz, dg.reshape(BW), db.reshape(BW), dw, dbs[:, :, 0]


CONV_TC = 128


def _shift_down(y, n):
    rows = lax.broadcasted_iota(jnp.int32, y.shape, 0)
    return jnp.where(rows < n, 0.0, pltpu.roll(y, n, 0))


def _shift_up(y, n):
    rows = lax.broadcasted_iota(jnp.int32, y.shape, 0)
    return jnp.where(rows >= y.shape[0] - n, 0.0, pltpu.roll(y, y.shape[0] - n, 0))


def _conv_specs(S):
    col = lambda c0: pl.BlockSpec((S, CONV_TC), lambda j: (0, c0 // CONV_TC + j))
    return col(C_CB), col(C_CB + BW), col(C_CB + 2 * BW), pl.BlockSpec((3, CONV_TC), lambda j: (0, j)), pl.BlockSpec((S, CONV_TC), lambda j: (0, j))


def _conv_fwd(p, conv_w):
    S = p.shape[0]

    def body(cb_ref, cc_ref, cx_ref, w_ref, o_ref):
        y = cc_ref[...].astype(F32) * cx_ref[...].astype(F32)
        conv = w_ref[0:1, :] * _shift_down(y, 2) + w_ref[1:2, :] * _shift_down(y, 1) + w_ref[2:3, :] * y
        o_ref[...] = (cb_ref[...].astype(F32) * conv).astype(BF16)

    cb, cc, cx, wspec, out = _conv_specs(S)
    return pl.pallas_call(
        body, name="conv_fwd", grid=(BW // CONV_TC,), in_specs=[cb, cc, cx, wspec], out_specs=out,
        out_shape=jax.ShapeDtypeStruct((S, BW), BF16), compiler_params=_cp("parallel"),
    )(p, p, p, conv_w)


def _conv_bwd(p, conv_w, dyc):
    S = p.shape[0]

    def body(cb_ref, cc_ref, cx_ref, w_ref, dy_ref, db_ref, dc_ref, dx_ref, dw_ref):
        cc, cx, dy = cc_ref[...].astype(F32), cx_ref[...].astype(F32), dy_ref[...].astype(F32)
        y = cc * cx
        w0, w1, w2 = w_ref[0:1, :], w_ref[1:2, :], w_ref[2:3, :]
        y1, y2 = _shift_down(y, 1), _shift_down(y, 2)
        conv = w0 * y2 + w1 * y1 + w2 * y
        db_ref[...] = (dy * conv).astype(BF16)
        dconv = dy * cb_ref[...].astype(F32)
        dyy = w2 * dconv + w1 * _shift_up(dconv, 1) + w0 * _shift_up(dconv, 2)
        dc_ref[...] = (dyy * cx).astype(BF16)
        dx_ref[...] = (dyy * cc).astype(BF16)
        dw_ref[0:1, :] = jnp.sum(dconv * y2, axis=0, keepdims=True)
        dw_ref[1:2, :] = jnp.sum(dconv * y1, axis=0, keepdims=True)
        dw_ref[2:3, :] = jnp.sum(dconv * y, axis=0, keepdims=True)

    cb, cc, cx, wspec, out = _conv_specs(S)
    db, dc, dx, dw = pl.pallas_call(
        body, name="conv_bwd", grid=(BW // CONV_TC,), in_specs=[cb, cc, cx, wspec, out],
        out_specs=[out, out, out, wspec],
        out_shape=[jax.ShapeDtypeStruct((S, BW), BF16)] * 3 + [jax.ShapeDtypeStruct((3, BW), F32)],
        compiler_params=_cp("parallel"),
    )(p, p, p, conv_w, dyc)
    return db, dc, dx, dw


def _merge_specs(S, tm):
    gate = lambda n: pl.BlockSpec((tm, D_MODEL), lambda i: (i, C_GATES // D_MODEL + n))
    return [gate(0), gate(1), gate(2)], pl.BlockSpec((3, tm, D_MODEL), lambda i: (0, i, 0)), pl.BlockSpec((tm, D_MODEL), lambda i: (i, 0))


def _branch_merge(br, w_br, p):
    S = p.shape[0]
    tm = _tile(S, (512, 256))

    def body(br_ref, w_ref, g0, g1, g2, bd_ref, o_ref):
        acc = None
        for n, g_ref in enumerate((g0, g1, g2)):
            bdn = jnp.dot(br_ref[n], w_ref[n], preferred_element_type=F32)
            bd_ref[n] = bdn.astype(BF16)
            term = jax.nn.sigmoid(g_ref[...].astype(F32)) * bdn
            acc = term if acc is None else acc + term
        o_ref[...] = acc.astype(BF16)

    gates, bspec, row = _merge_specs(S, tm)
    return pl.pallas_call(
        body, name="mm_branch", grid=(S // tm,),
        in_specs=[pl.BlockSpec((3, tm, BW), lambda i: (0, i, 0)), pl.BlockSpec((3, BW, D_MODEL), lambda i: (0, 0, 0))] + gates,
        out_specs=[bspec, row], out_shape=[jax.ShapeDtypeStruct((3, S, D_MODEL), BF16), jax.ShapeDtypeStruct((S, D_MODEL), BF16)],
        compiler_params=_cp("parallel"),
    )(br, w_br, p, p, p)


def _merge_bwd(p, bd, dm):
    S = p.shape[0]
    tm = _tile(S, (256,))

    def body(g0, g1, g2, b_ref, dm_ref, db_ref, dg_ref):
        dmv = dm_ref[...]
        for n, g_ref in enumerate((g0, g1, g2)):
            sg = jax.nn.sigmoid(g_ref[...].astype(F32))
            db_ref[n] = (dmv * sg).astype(BF16)
            dg_ref[:, n * D_MODEL:(n + 1) * D_MODEL] = (dmv * b_ref[n].astype(F32) * (sg * (1.0 - sg))).astype(BF16)

    gates, bspec, row = _merge_specs(S, tm)
    return pl.pallas_call(
        body, name="merge_bwd", grid=(S // tm,), in_specs=gates + [bspec, row],
        out_specs=[bspec, pl.BlockSpec((tm, 3 * D_MODEL), lambda i: (i, 0))],
        out_shape=[jax.ShapeDtypeStruct((3, S, D_MODEL), BF16), jax.ShapeDtypeStruct((S, 3 * D_MODEL), BF16)],
        compiler_params=_cp("parallel"),
    )(p, p, p, bd, dm)


XA_TM = 512


def _xa_probs(qh, kh):
    s = lax.dot_general(qh, kh, NT, preferred_element_type=F32) * (XA_DH ** -0.5)
    e = jnp.exp(s - jnp.max(s, axis=-1, keepdims=True))
    return e / jnp.sum(e, axis=-1, keepdims=True)


def _xa_fwd(q, kv):
    S = q.shape[0]
    tm = min(XA_TM, S)
    M = kv.shape[1]

    def body(q_ref, kv_ref, o_ref):
        for h in range(XA_HEADS):
            cols = slice(h * XA_DH, (h + 1) * XA_DH)
            pr = _xa_probs(q_ref[:, cols], kv_ref[0, :, cols])
            o_ref[:, cols] = jnp.dot(pr.astype(BF16), kv_ref[1, :, cols], preferred_element_type=F32).astype(BF16)

    row = pl.BlockSpec((tm, D_MODEL), lambda i: (i, 0))
    return pl.pallas_call(
        body, name="xa_fwd", grid=(S // tm,), in_specs=[row, pl.BlockSpec((2, M, D_MODEL), lambda i: (0, 0, 0))], out_specs=row,
        out_shape=jax.ShapeDtypeStruct((S, D_MODEL), BF16), compiler_params=_cp("parallel"),
    )(q, kv)


def _xa_bwd(q, kv, do):
    S = q.shape[0]
    tm = min(XA_TM, S)
    M = kv.shape[1]

    def body(q_ref, kv_ref, do_ref, dq_ref, dkv_ref):
        @pl.when(pl.program_id(0) == 0)
        def _():
            dkv_ref[...] = jnp.zeros_like(dkv_ref)

        for h in range(XA_HEADS):
            cols = slice(h * XA_DH, (h + 1) * XA_DH)
            qh, kh, vh, doh = q_ref[:, cols], kv_ref[0, :, cols], kv_ref[1, :, cols], do_ref[:, cols]
            pr = _xa_probs(qh, kh)
            dkv_ref[1, :, cols] += lax.dot_general(pr.astype(BF16), doh, TN, preferred_element_type=F32)
            dp = lax.dot_general(doh, vh, NT, preferred_element_type=F32)
            ds = (pr * (dp - jnp.sum(dp * pr, axis=-1, keepdims=True)) * (XA_DH ** -0.5)).astype(BF16)
            dq_ref[:, cols] = jnp.dot(ds, kh, preferred_element_type=F32).astype(BF16)
            dkv_ref[0, :, cols] += lax.dot_general(ds, qh, TN, preferred_element_type=F32)

    row = pl.BlockSpec((tm, D_MODEL), lambda i: (i, 0))
    kvs = pl.BlockSpec((2, M, D_MODEL), lambda i: (0, 0, 0))
    return pl.pallas_call(
        body, name="xa_bwd", grid=(S // tm,), in_specs=[row, kvs, row], out_specs=[row, kvs],
        out_shape=[jax.ShapeDtypeStruct((S, D_MODEL), BF16), jax.ShapeDtypeStruct((2, M, D_MODEL), F32)],
        compiler_params=_cp("arbitrary"),
    )(q, kv, do)


def _ffn_up(h3, w_gu, rider=None):
    S = h3.shape[0]
    tm = _tile(S, (1024, 512, 256))

    def body(h_ref, w_ref, ab_ref, hh_ref):
        h = h_ref[...]
        a = lax.dot_general(h, w_ref[0], NT, preferred_element_type=F32)
        b = lax.dot_general(h, w_ref[1], NT, preferred_element_type=F32)
        ab_ref[0] = a.astype(BF16)
        ab_ref[1] = b.astype(BF16)
        hh_ref[...] = (a * jax.nn.sigmoid(a) * b).astype(BF16)

    pair = pl.BlockSpec((None, 2, tm, FFN_SH), lambda j, i: (j, 0, i, 0))
    outs = _call_with_rider(
        rider, body, name="mm_gu", grid=(N_CHIPS, S // tm), args=[h3, w_gu],
        in_specs=[pl.BlockSpec((tm, D_MODEL), lambda j, i: (i, 0)), pl.BlockSpec((None, 2, FFN_SH, D_MODEL), lambda j, i: (j, 0, 0, 0))],
        out_specs=[pair, pl.BlockSpec((None, tm, FFN_SH), lambda j, i: (j, i, 0))],
        out_shape=[jax.ShapeDtypeStruct((N_CHIPS, 2, S, FFN_SH), BF16), jax.ShapeDtypeStruct((N_CHIPS, S, FFN_SH), BF16)],
        scratch_shapes=[], semantics=("parallel", "parallel"))
    return outs[0], outs[1], outs[2:]


def _ffn_down_bwd(dx3, w_dn, ab):
    S = dx3.shape[0]
    tm = _tile(S, (1024, 512, 256))

    def body(dx_ref, w_ref, ab_ref, o_ref):
        d = lax.dot_general(dx_ref[...].astype(BF16), w_ref[...], NT, preferred_element_type=F32)
        a, b = ab_ref[0].astype(F32), ab_ref[1].astype(F32)
        sg = jax.nn.sigmoid(a)
        o_ref[0] = (d * b * (sg * (1.0 + a * (1.0 - sg)))).astype(BF16)
        o_ref[1] = (d * (a * sg)).astype(BF16)

    pair = pl.BlockSpec((None, 2, tm, FFN_SH), lambda j, i: (j, 0, i, 0))
    return pl.pallas_call(
        body, name="mm_down_dx", grid=(N_CHIPS, S // tm),
        in_specs=[pl.BlockSpec((tm, D_MODEL), lambda j, i: (i, 0)), pl.BlockSpec((None, FFN_SH, D_MODEL), lambda j, i: (j, 0, 0)), pair],
        out_specs=pair, out_shape=jax.ShapeDtypeStruct(ab.shape, BF16), compiler_params=_cp("parallel", "parallel"),
    )(dx3, w_dn, ab)


def _reduce_adam(parts, w, m, v, *, name):
    shape = w.shape
    C = shape[-1]
    R = math.prod(shape[:-1])
    tm = _rows(R, 4 * C)
    n = len(parts)
    c1, c2 = 1.0 - ADAM_B1 ** ADAM_STEP, 1.0 - ADAM_B2 ** ADAM_STEP

    def body(*refs):
        g = refs[0][...]
        for r in refs[1:n]:
            g = g + r[...]
        w_ref, m_ref, v_ref, go, do, mo, vo = refs[n:]
        mn = ADAM_B1 * m_ref[...] + (1.0 - ADAM_B1) * g
        vn = ADAM_B2 * v_ref[...] + (1.0 - ADAM_B2) * (g * g)
        go[...] = g
        do[...] = -ADAM_LR * ((mn / c1) / (jnp.sqrt(vn / c2) + ADAM_EPS) + ADAM_WD * w_ref[...])
        mo[...] = mn
        vo[...] = vn

    row = pl.BlockSpec((tm, C), lambda i: (i, 0))
    outs = pl.pallas_call(
        body, name=name, grid=(R // tm,), in_specs=[row] * (n + 3), out_specs=[row] * 4,
        out_shape=[jax.ShapeDtypeStruct((R, C), F32)] * 4, compiler_params=_cp("parallel"),
    )(*[a.reshape(R, C) for a in (*parts, w, m, v)])
    return tuple(o.reshape(shape) for o in outs)


_VIEW = {
    "w_in": ((1024, 7168), (1024, 1792), 256, lambda i, b: (i, b)),
    "w_br": ((1536, 1024), (1536, 256), 512, lambda i, b: (i, b)),
    "w_sq": ((5120, 1024), (1280, 1024), 256, lambda i, b: (4 * i + b, 0)),
    "w_gu": ((5632, 1024), (1408, 1024), 352, lambda i, b: (4 * b + i, 0)),
    "w_dn": ((2816, 1024), (704, 1024), 352, lambda i, b: (2 * b + i, 0)),
    "conv_w": ((8, 512), (8, 128), 8, lambda i, b: (0, b)),
}


def _scalar(v):
    return jnp.asarray(v, jnp.int32).reshape(1)


def _place(name, local, b):
    full2, sh2, tm, idx = _VIEW[name]
    C = sh2[1]
    dt = local.dtype if name == "conv_w" else BF16

    def body(b_ref, x_ref, o0_ref, o1_ref):
        o0_ref[...] = x_ref[0].astype(dt)
        o1_ref[...] = x_ref[1].astype(dt)

    place = pl.BlockSpec((tm, C), lambda i, bs: idx(i, bs[0]))
    outs = pl.pallas_call(
        body, name="place_" + name,
        grid_spec=pltpu.PrefetchScalarGridSpec(num_scalar_prefetch=1, grid=(sh2[0] // tm,),
                                               in_specs=[pl.BlockSpec((DEPTH, tm, C), lambda i, bs: (0, i, 0))], out_specs=[place, place]),
        out_shape=[jax.ShapeDtypeStruct(full2, dt)] * 2, compiler_params=_cp("arbitrary"),
    )(_scalar(b), local.reshape((DEPTH,) + sh2))
    return [o.reshape(_FULL_SHAPE[name]) for o in outs]


def _add_owner(name, g, land, own):
    shape = g.shape
    C = shape[-1]
    R = math.prod(shape[:-1])
    tm = _rows(R, 4 * C)

    def body(s_ref, g_ref, l_ref, o_ref):
        @pl.when(s_ref[0] != 0)
        def _():
            o_ref[...] = (g_ref[...].astype(F32) + l_ref[...].astype(F32)).astype(BF16)

        @pl.when(s_ref[0] == 0)
        def _():
            o_ref[...] = jnp.zeros_like(o_ref)

    pick = pl.BlockSpec((tm, C), lambda i, s: (jnp.where(s[0] != 0, i, 0), 0))
    return pl.pallas_call(
        body, name="presum_" + name,
        grid_spec=pltpu.PrefetchScalarGridSpec(num_scalar_prefetch=1, grid=(R // tm,), in_specs=[pick, pick],
                                               out_specs=pl.BlockSpec((tm, C), lambda i, s: (i, 0))),
        out_shape=jax.ShapeDtypeStruct((R, C), BF16), compiler_params=_cp("arbitrary"),
    )(_scalar(own), g.reshape(R, C), land.reshape(R, C)).reshape(shape)


def _sum_chips(name, slots, part, b, own):
    full2, sh2, tm, idx = _VIEW[name]
    C = sh2[1]

    def body(s_ref, slot_ref, own_ref, o_ref):
        @pl.when(s_ref[1] != 0)
        def _():
            o_ref[...] = ((slot_ref[0].astype(F32) + slot_ref[1].astype(F32)) + slot_ref[2].astype(F32)) + own_ref[...].astype(F32)

        @pl.when(s_ref[1] == 0)
        def _():
            o_ref[...] = jnp.zeros_like(o_ref)

    return pl.pallas_call(
        body, name="sum_chips_" + name,
        grid_spec=pltpu.PrefetchScalarGridSpec(
            num_scalar_prefetch=1, grid=(sh2[0] // tm,),
            in_specs=[pl.BlockSpec((3, tm, C), lambda i, s: (0, jnp.where(s[1] != 0, i, 0), 0)),
                      pl.BlockSpec((tm, C), lambda i, s: idx(jnp.where(s[1] != 0, i, 0), s[0]))],
            out_specs=pl.BlockSpec((tm, C), lambda i, s: (i, 0))),
        out_shape=jax.ShapeDtypeStruct(sh2, F32), compiler_params=_cp("arbitrary"),
    )(jnp.stack([jnp.asarray(b, jnp.int32), jnp.asarray(own, jnp.int32)]), slots.reshape((3,) + sh2),
      part.reshape(full2)).reshape(_SHARD_SHAPE[name])


def _adam_layers(mine, theirs, c, w, m, v, *, name):
    shape = w.shape
    C = shape[-1]
    R = math.prod(shape[1:-1])
    tm = _rows(R, 4 * C)
    c1, c2 = 1.0 - ADAM_B1 ** ADAM_STEP, 1.0 - ADAM_B2 ** ADAM_STEP

    def body(c_ref, m0_ref, m1_ref, t_ref, w_ref, m_ref, v_ref, go, do, mo, vo):
        layer = pl.program_id(0)
        g = jnp.where(layer == c_ref[0], jnp.where(layer == 0, m0_ref[...], m1_ref[...]), t_ref[...])
        mn = ADAM_B1 * m_ref[...] + (1.0 - ADAM_B1) * g
        vn = ADAM_B2 * v_ref[...] + (1.0 - ADAM_B2) * (g * g)
        go[...] = g
        do[...] = -ADAM_LR * ((mn / c1) / (jnp.sqrt(vn / c2) + ADAM_EPS) + ADAM_WD * w_ref[...])
        mo[...] = mn
        vo[...] = vn

    def own(layer):
        return pl.BlockSpec((tm, C), lambda l, i, cs: (jnp.where((l == layer) & (cs[0] == layer), i, 0), 0))

    recv = pl.BlockSpec((tm, C), lambda l, i, cs: (jnp.where(l == cs[0], 0, i), 0))
    row = pl.BlockSpec((None, tm, C), lambda l, i, cs: (l, i, 0))
    outs = pl.pallas_call(
        body, name=name,
        grid_spec=pltpu.PrefetchScalarGridSpec(num_scalar_prefetch=1, grid=(DEPTH, R // tm),
                                               in_specs=[own(0), own(1), recv, row, row, row], out_specs=[row] * 4),
        out_shape=[jax.ShapeDtypeStruct((DEPTH, R, C), F32)] * 4, compiler_params=_cp("arbitrary", "arbitrary"),
    )(_scalar(c), mine[0].reshape(R, C), mine[1].reshape(R, C), theirs.reshape(R, C), *[t.reshape(DEPTH, R, C) for t in (w, m, v)])
    return tuple(o.reshape(shape) for o in outs)


def _take_weights(wl, names, landed):
    for n, t in zip(names, landed):
        wl[n] = t[:3] if n == "conv_w" else t


_GATHER_LATE = ["w_br", "conv_w", "w_sq"]
_GATHER_LAST = ["w_gu", "w_dn"]


def _layer_fwd(x, mem, wl, ride=None, h1=None, next_gain=None):
    S = x.shape[0]
    sv = {"x": x}
    wl = dict(wl)
    if h1 is None:
        h1 = _rms_fwd(x, wl["norm_mix_g"], name="rms_mix")
    if ride is None:
        ride = {"mm_in": None, "sb": None, "mm_gu": None}
        p = _mm(h1, wl["w_in"], mode="nn", out_dtype=BF16, name="mm_in")
    else:
        p, landed = _mm(h1, wl["w_in"], mode="nn", out_dtype=BF16, name="mm_in", rider=ride["mm_in"])
        _take_weights(wl, _GATHER_LATE, landed)
    (ya, tot, cnt), landed = _sb2_fwd(p, ride["sb"])
    _take_weights(wl, _GATHER_LAST, landed)
    b_st = wl["b_spatial"].T
    yb = _sgu_fwd(p, wl["sgu_ln_g"], wl["sgu_ln_b"], wl["w_spatial"], b_st)
    yc = _conv_fwd(p, wl["conv_w"])
    br = jnp.stack([ya, yb, yc])
    bd, merged = _branch_merge(br, wl["w_br"], p)
    x1, h2 = _mm(merged, wl["w_sq"][0], mode="nn", res=x, norm=wl["norm_xa_g"], name="mm_out")
    qx = _mm(h2, wl["w_sq"][1], mode="nn", out_dtype=BF16, name="mm_q")
    mn = _rms_fwd(mem, wl["mem_norm_g"], name="rms_mem")
    kv = _mm(mn, wl["w_sq"][3:5], mode="nn", b_kind="batch", out_dtype=BF16, name="mm_kv")
    o = _xa_fwd(qx, kv)
    x2, h3 = _mm(o, wl["w_sq"][2], mode="nn", res=x1, norm=wl["norm_ffn_g"], name="mm_o")
    ab, hh, rode = _ffn_up(h3, wl["w_gu"], ride["mm_gu"])
    x3 = _mm(hh, wl["w_dn"], mode="nn", a_kind="kchunk", b_kind="kchunk", res=x2, norm=next_gain, name="mm_down")
    x3, h_next = x3 if next_gain is not None else (x3, None)
    sv.update(h1=h1, p=p, tot=tot, cnt=cnt, br=br, bd=bd, merged=merged, x1=x1, h2=h2, qx=qx, mn=mn, kv=kv, o=o,
              x2=x2, h3=h3, ab=ab, hh=hh, b_st=b_st, wl=wl)
    return x3, sv, rode, h_next


class _GradPipe:
    def __init__(self, ci, bi):
        self.ci, self.bi, self.queue = ci, bi, []
        self.part, self.slots = [dict(), dict()], [dict(), dict()]

    def to_owner(self, layer, names, g):
        def arrived(land):
            own = (self.ci == layer).astype(jnp.int32)
            part = {n: _add_owner(n, g[n], t, own) for n, t in zip(names, land)}
            self.part[layer].update(part)
            self.queue.append((layer, names, part))

        return _presum_rider(layer, {n: g[n] for n in names}), arrived

    def exchange(self):
        if not self.queue:
            return None
        layer, names, part = self.queue.pop(0)
        return _shard_rider(layer, part), lambda slots: self.slots[layer].update(zip(names, slots))

    def drain(self, also):
        landed = None
        while self.queue or landed is None:
            job = self.exchange()
            both = _join([job[0] if job else None, also if landed is None else None])
            parts = both.split(_run_rider(both, name="grad_exchange_last"))
            if job:
                job[1](parts[0])
            if landed is None:
                landed = parts[-1]
        return landed

    def reduced(self, layer):
        own = (self.ci == layer).astype(jnp.int32)
        return {n: _sum_chips(n, self.slots[layer][n], self.part[layer][n], self.bi, own) for n in _BIG}


def _layer_bwd(dx3, mem, sv, layer=None, pipe=None):
    S = dx3.shape[0]
    p, wl = sv["p"], sv["wl"]
    g = {}

    def carrying(jobs, fn):
        jobs = [j for j in jobs if j]
        joined = _join([j[0] for j in jobs])
        out, landed = fn(joined)
        if joined is not None:
            for j, part in zip(jobs, joined.split(landed)):
                j[1](part)
        return out

    def mm(*args, job=None, jobs=(), **kw):
        return carrying([job, *jobs], lambda r: (_mm(*args, **kw), None) if r is None else _mm(*args, **kw, rider=r))

    to_owner = (lambda names: pipe.to_owner(layer, names, g)) if pipe else (lambda names: None)
    exchange = pipe.exchange if pipe else (lambda: None)

    g["w_dn"] = _mm(sv["hh"], dx3, mode="tn", a_kind="batch", out_dtype=BF16, name="mm_down_dw")
    dab = _ffn_down_bwd(dx3, wl["w_dn"], sv["ab"]).reshape(2 * N_CHIPS, S, FFN_SH)
    g["w_gu"] = mm(dab, sv["h3"], mode="tn", a_kind="batch", out_dtype=BF16, name="mm_gu_dw", job=exchange()).reshape(_FULL_SHAPE["w_gu"])
    dx2, g["norm_ffn_g"] = mm(dab, wl["w_gu"].reshape(2 * N_CHIPS, FFN_SH, D_MODEL), mode="nn", a_kind="kchunk", b_kind="kchunk", name="mm_gu_dx",
                               rms_bwd=(sv["x2"], wl["norm_ffn_g"], dx3), job=to_owner(["w_dn", "w_gu"]))
    do = _mm(dx2, wl["w_sq"][2], mode="nt", out_dtype=BF16, name="mm_o_dx")
    dw_o = _mm(sv["o"], dx2, mode="tn", out_dtype=BF16, name="mm_o_dw")
    dq, dkv = _xa_bwd(sv["qx"], sv["kv"], do)
    dw_q = _mm(sv["h2"], dq, mode="tn", out_dtype=BF16, name="mm_q_dw")
    dx1, g["norm_xa_g"] = _mm(dq, wl["w_sq"][1], mode="nt", name="mm_q_dx", rms_bwd=(sv["x1"], wl["norm_xa_g"], dx2))
    dw_kv = _mm(sv["mn"], dkv, mode="tn", b_kind="batch", out_dtype=BF16, name="mm_kv_dw")
    dmn = _mm(dkv, wl["w_sq"][3:5], mode="nt", a_kind="kchunk", b_kind="kchunk", name="mm_kv_dx")
    _, g["mem_norm_g"] = _rms_bwd(mem, wl["mem_norm_g"], dmn, jnp.zeros_like(mem), name="rms_mem_bwd")
    dm = _mm(dx1, wl["w_sq"][0], mode="nt", name="mm_out_dx")
    dw_out = _mm(sv["merged"], dx1, mode="tn", out_dtype=BF16, name="mm_out_dw")
    g["w_sq"] = jnp.concatenate([jnp.stack([dw_out, dw_q, dw_o]), dw_kv])
    dbd, dgates = _merge_bwd(p, sv["bd"], dm)
    dbr = mm(dbd, wl["w_br"], mode="nt", a_kind="batch", b_kind="batch", name="mm_branch_dx", job=to_owner(["w_sq"]))
    g["w_br"] = _mm(sv["br"], dbd, mode="tn", a_kind="batch", b_kind="batch", out_dtype=BF16, name="mm_branch_dw")
    dq, dk, dv = carrying([exchange(), to_owner(["w_br"])], lambda r: _sb2_bwd(p, dbr, sv["tot"], sv["cnt"], r))
    dz, g["sgu_ln_g"], g["sgu_ln_b"], g["w_spatial"], g["b_spatial"] = _sgu_bwd(
        p, dbr[1], wl["sgu_ln_g"], wl["sgu_ln_b"], wl["w_spatial"], wl["w_spatial"].transpose(0, 2, 1), sv["b_st"])
    dcb, dcc, dcx, g["conv_w"] = _conv_bwd(p, wl["conv_w"], dbr[2])
    dp = jnp.concatenate([dq, dk.astype(BF16), dv.astype(BF16), dz, dcb, dcc, dcx, dgates], axis=1)
    g["w_in"] = mm(sv["h1"], dp, mode="tn", out_dtype=BF16, name="mm_in_dw", jobs=[exchange(), exchange()])
    dx, g["norm_mix_g"] = mm(dp, wl["w_in"], mode="nt", name="mm_in_dx", rms_bwd=(sv["x"], wl["norm_mix_g"], dx1),
                             job=to_owner(["w_in"]))
    return dx, g


def _local_step(x, mem, target, layers, final_g):
    h, saved = x, []
    for wl in layers:
        h, sv, _, _ = _layer_fwd(h, mem, wl)
        saved.append(sv)
    loss, dx, d_final = _loss_head(h, final_g, target)
    grads = [None] * len(layers)
    for l in reversed(range(len(layers))):
        dx, grads[l] = _layer_bwd(dx, mem, saved[l])
    return loss, dx, grads, d_final


_ALL = slice(None)
CONV_ROWS = 8
_SHARD = {
    "w_in": lambda b: (_ALL, pl.ds(1792 * b, 1792)),
    "w_br": lambda b: (_ALL, _ALL, pl.ds(256 * b, 256)),
    "w_sq": lambda b: (_ALL, pl.ds(256 * b, 256), _ALL),
    "w_gu": lambda b: (b,),
    "w_dn": lambda b: (b,),
    "conv_w": lambda b: (_ALL, pl.ds(128 * b, 128)),
}
_FULL_SHAPE = {"w_in": (1024, 7168), "w_br": (3, 512, 1024), "w_sq": (5, 1024, 1024), "w_gu": (4, 2, 704, 1024),
               "w_dn": (4, 704, 1024), "conv_w": (CONV_ROWS, 512)}
_SHARD_SHAPE = {"w_in": (1024, 1792), "w_br": (3, 512, 256), "w_sq": (5, 256, 1024), "w_gu": (2, 704, 1024),
                "w_dn": (704, 1024), "conv_w": (CONV_ROWS, 128)}


def _pos():
    return lax.axis_index("x"), lax.axis_index("y"), lax.axis_index("c")


def _per_chip(fn):
    x, y, _ = _pos()
    for x0 in (0, 1):
        for y0 in (0, 1):
            @pl.when((x == x0) & (y == y0))
            def _():
                fn(x0, y0)


def _other_chips(x0, y0):
    return [(1 - x0, y0), (x0, 1 - y0), (1 - x0, 1 - y0)]


def _rcopy(src, dst, ssem, rsem, dev):
    return pltpu.make_async_remote_copy(src_ref=src, dst_ref=dst, send_sem=ssem, recv_sem=rsem, device_id=dev, device_id_type=MESH)


def _dma_sems(n):
    return pltpu.SemaphoreType.DMA((n,))


def _gather_rider(layer, placed):
    names = list(placed)
    n = len(names)
    shard = lambda refs, a, b: refs[a].at[_SHARD[names[a]](b)]

    def start(ins, outs, send, recv):
        @pl.when(lax.axis_index("c") == layer)
        def _():
            def run(x0, y0):
                for kk, (px, py) in enumerate(_other_chips(x0, y0)):
                    for a in range(n):
                        own = shard(outs, a, 2 * x0 + y0)
                        _rcopy(own, own, send.at[6 * a + kk], recv.at[6 * a + kk], (px, py, layer)).start()

            _per_chip(run)

    def passing(outs, send, recv, a, kk, bp, x0, y0):
        landed = shard(outs, a, bp)
        return _rcopy(landed, landed, send.at[6 * a + 3 + kk], recv.at[6 * a + 3 + kk], (x0, y0, 1 - layer))

    def middle(ins, outs, send, recv):
        @pl.when(lax.axis_index("c") == layer)
        def _():
            def run(x0, y0):
                for kk, (px, py) in enumerate(_other_chips(x0, y0)):
                    for a in range(n):
                        landed = shard(outs, a, 2 * px + py)
                        _rcopy(landed, landed, send.at[6 * a + kk], recv.at[6 * a + kk], (px, py, layer)).wait_recv()
                        passing(outs, send, recv, a, kk, 2 * px + py, x0, y0).start()

            _per_chip(run)

    def finish(ins, outs, send, recv):
        c = lax.axis_index("c")

        def run(x0, y0):
            chips = _other_chips(x0, y0)

            @pl.when(c == layer)
            def _():
                for kk, (px, py) in enumerate(chips):
                    for a in range(n):
                        own = shard(outs, a, 2 * x0 + y0)
                        _rcopy(own, own, send.at[6 * a + kk], recv.at[6 * a + kk], (px, py, layer)).wait_send()
                        passing(outs, send, recv, a, kk, 2 * px + py, x0, y0).wait_send()

            @pl.when(c != layer)
            def _():
                for kk, (px, py) in enumerate(chips):
                    for a in range(n):
                        got = shard(outs, a, 2 * px + py)
                        _rcopy(got, got, send.at[6 * a + 3 + kk], recv.at[6 * a + 3 + kk], (x0, y0, layer)).wait_recv()

        _per_chip(run)

    arrs = [placed[nm] for nm in names]
    return _Rider(arrs, [jax.ShapeDtypeStruct(t.shape, t.dtype) for t in arrs], 6 * n, start, finish, alias={a: a for a in range(n)}, middle=middle)


def _presum_rider(layer, grads):
    names = list(grads)
    n = len(names)

    def start(ins, outs, send, recv):
        x, y, c = _pos()

        @pl.when(c != layer)
        def _():
            for a in range(n):
                _rcopy(ins[a], outs[a], send.at[a], recv.at[a], (x, y, layer)).start()

    def finish(ins, outs, send, recv):
        x, y, c = _pos()

        @pl.when(c != layer)
        def _():
            for a in range(n):
                _rcopy(ins[a], outs[a], send.at[a], recv.at[a], (x, y, layer)).wait_send()

        @pl.when(c == layer)
        def _():
            for a in range(n):
                _rcopy(outs[a], outs[a], send.at[a], recv.at[a], (x, y, 1 - layer)).wait_recv()

    arrs = [grads[nm] for nm in names]
    return _Rider(arrs, [jax.ShapeDtypeStruct(t.shape, t.dtype) for t in arrs], n, start, finish)


def _shard_rider(layer, part):
    names = list(part)
    n = len(names)

    def each(fn):
        @pl.when(lax.axis_index("c") == layer)
        def _():
            def run(x0, y0):
                for kk, (px, py) in enumerate(_other_chips(x0, y0)):
                    for a in range(n):
                        fn(a, kk, 2 * px + py, (px, py, layer))

            _per_chip(run)

    def start(ins, outs, send, recv):
        each(lambda a, kk, bp, peer: _rcopy(ins[a].at[_SHARD[names[a]](bp)], outs[a].at[kk], send.at[3 * a + kk], recv.at[3 * a + kk], peer).start())

    def finish(ins, outs, send, recv):
        each(lambda a, kk, bp, peer: _rcopy(outs[a].at[kk], outs[a].at[kk], send.at[3 * a + kk], recv.at[3 * a + kk], peer).wait_recv())
        each(lambda a, kk, bp, peer: _rcopy(ins[a].at[_SHARD[names[a]](bp)], outs[a].at[kk], send.at[3 * a + kk], recv.at[3 * a + kk], peer).wait_send())

    return _Rider([part[nm] for nm in names], [jax.ShapeDtypeStruct((N_CHIPS - 1,) + _SHARD_SHAPE[nm], part[nm].dtype) for nm in names],
                  3 * n, start, finish)


def _sibling_exchange(mine0, mine1):
    names = list(mine0)
    n = len(names)

    def body(*refs):
        l0, l1, outs = refs[:n], refs[n:2 * n], refs[2 * n:3 * n]
        send, recv = refs[3 * n:]
        x, y, c = _pos()
        for c0 in (0, 1):
            @pl.when(c == c0)
            def _():
                srcs = l0 if c0 == 0 else l1
                cps = [_rcopy(srcs[a], outs[a], send.at[a], recv.at[a], (x, y, 1 - c0)) for a in range(n)]
                for cp in cps:
                    cp.start()
                for cp in cps:
                    cp.wait()

    outs = pl.pallas_call(
        body, name="grad_sibling_exchange", in_specs=[ANY] * (2 * n), out_specs=[ANY] * n,
        out_shape=[jax.ShapeDtypeStruct(mine0[nm].shape, mine0[nm].dtype) for nm in names],
        scratch_shapes=[_dma_sems(n), _dma_sems(n)],
    )(*[mine0[nm] for nm in names], *[mine1[nm] for nm in names])
    return dict(zip(names, outs))


def _small_rider(pack):
    flips = [(fx, fy, fc) for fx in (0, 1) for fy in (0, 1) for fc in (0, 1) if fx or fy or fc]

    def peers():
        x, y, c = _pos()
        return 4 * x + 2 * y + c, [(x ^ fx, y ^ fy, c ^ fc) for fx, fy, fc in flips]

    def start(ins, outs, send, recv):
        me, to = peers()
        for k, peer in enumerate(to):
            _rcopy(ins[0], outs[0].at[me], send.at[k], recv.at[k], peer).start()

    def finish(ins, outs, send, recv):
        me, to = peers()
        for k, (px, py, pc) in enumerate(to):
            slot = outs[0].at[4 * px + 2 * py + pc]
            _rcopy(slot, slot, send.at[k], recv.at[k], (px, py, pc)).wait_recv()
        for k, peer in enumerate(to):
            _rcopy(ins[0], outs[0].at[me], send.at[k], recv.at[k], peer).wait_send()

    return _Rider([pack], [jax.ShapeDtypeStruct((8,) + pack.shape, pack.dtype)], len(flips), start, finish)


def _sum_devices(gathered, pack, me):
    n, R, C = gathered.shape

    def body(me_ref, r_ref, own_ref, o_ref):
        acc = jnp.where(me_ref[0] == 0, own_ref[...], r_ref[0])
        for s in range(1, n):
            acc = acc + jnp.where(me_ref[0] == s, own_ref[...], r_ref[s])
        o_ref[...] = acc

    return pl.pallas_call(
        body, name="sum_devices_small",
        grid_spec=pltpu.PrefetchScalarGridSpec(num_scalar_prefetch=1, grid=(1,),
                                               in_specs=[pl.BlockSpec((n, R, C), lambda i, m: (0, 0, 0)), pl.BlockSpec((R, C), lambda i, m: (0, 0))],
                                               out_specs=pl.BlockSpec((R, C), lambda i, m: (0, 0))),
        out_shape=jax.ShapeDtypeStruct((R, C), F32), compiler_params=_cp("arbitrary"),
    )(_scalar(me), gathered, pack)


_WEIGHTS = ["norm_mix_g", "w_in", "sgu_ln_g", "sgu_ln_b", "w_spatial", "b_spatial", "conv_w", "w_branch", "w_out", "norm_xa_g",
            "mem_norm_g", "w_q_xa", "w_k_xa", "w_v_xa", "w_o_xa", "norm_ffn_g", "w_gate_ffn", "w_up_ffn", "w_down_ffn", "final_g"]
_REPLICATED = ["norm_mix_g", "sgu_ln_g", "sgu_ln_b", "w_spatial", "b_spatial", "norm_xa_g", "mem_norm_g", "norm_ffn_g", "final_g"]
_SQUARE = ["w_out", "w_q_xa", "w_o_xa", "w_k_xa", "w_v_xa"]
_BIG = ["w_in", "w_br", "w_sq", "w_gu", "w_dn"]


def _pack(arrs):
    return jnp.concatenate([a.reshape(-1) for a in arrs]).reshape(-1, 128)


def _step(a):
    w = {n: a[n] for n in _WEIGHTS}
    x, mem, target = a["x"][0], a["mem"][0], a["loss_target"][0]
    xi, yi, ci = _pos()
    bi = 2 * xi + yi
    groups = list(_FULL_SHAPE)

    tr = lambda t: jnp.swapaxes(t, 1, 2)
    local = {"w_in": w["w_in"], "w_br": w["w_branch"], "w_sq": jnp.stack([w[n] for n in _SQUARE], axis=1),
             "w_gu": jnp.stack([tr(w["w_gate_ffn"]), tr(w["w_up_ffn"])], axis=1), "w_dn": w["w_down_ffn"],
             "conv_w": jnp.pad(w["conv_w"], ((0, 0), (0, CONV_ROWS - 3), (0, 0)))}
    placed = [dict(), dict()]
    for n in groups:
        placed[0][n], placed[1][n] = _place(n, local[n], bi)

    def gather(l, names):
        return _gather_rider(l, {n: placed[l][n] for n in names})

    def start_of(l, w_in):
        return {"w_in": w_in, **{n: w[n][l] for n in _REPLICATED if n != "final_g"}}

    w_in0, = _run_rider(gather(0, ["w_in"]), name="gather_first")
    h, sv0, (w_in1,), h1 = _layer_fwd(x, mem, start_of(0, w_in0), {"mm_in": gather(0, _GATHER_LATE), "sb": gather(0, _GATHER_LAST),
                                                                    "mm_gu": gather(1, ["w_in"])}, next_gain=w["norm_mix_g"][1])
    h, sv1, _, _ = _layer_fwd(h, mem, start_of(1, w_in1), {"mm_in": gather(1, _GATHER_LATE), "sb": gather(1, _GATHER_LAST), "mm_gu": None}, h1=h1)
    loss, dx, d_final = _loss_head(h, w["final_g"], target)
    loss = lax.psum(loss, ("x", "y", "c"))

    pipe = _GradPipe(ci, bi)
    dx, g1 = _layer_bwd(dx, mem, sv1, 1, pipe)
    dx, g0 = _layer_bwd(dx, mem, sv0, 0, pipe)
    grads = [g0, g1]
    small = {n: jnp.stack([g[n] for g in grads]) for n in _REPLICATED if n != "final_g"}
    small["final_g"] = d_final
    conv_g = jnp.stack([g["conv_w"] for g in grads])
    small_pack = _pack([small[n] for n in _REPLICATED] + [conv_g])
    gathered, = pipe.drain(_small_rider(small_pack))
    mine = [pipe.reduced(0), pipe.reduced(1)]
    theirs = _sibling_exchange(mine[0], mine[1])

    out = {}

    def adam_layers(name, group, pick=None, view=lambda t: t):
        sel = (lambda t: t[group]) if pick is None else (lambda t: t[group][pick])
        res = _adam_layers([sel(mine[0]), sel(mine[1])], sel(theirs), ci, view(w[name]), view(a["m_" + name]), view(a["v_" + name]), name="adam_" + name)
        out[name] = tuple(view(r) for r in res)

    adam_layers("w_in", "w_in")
    adam_layers("w_branch", "w_br")
    for t, n in enumerate(_SQUARE):
        adam_layers(n, "w_sq", t)
    adam_layers("w_gate_ffn", "w_gu", 0, tr)
    adam_layers("w_up_ffn", "w_gu", 1, tr)
    adam_layers("w_down_ffn", "w_dn")

    def adam(name, g):
        out[name] = _reduce_adam([g], w[name], a["m_" + name], a["v_" + name], name="adam_" + name)

    n_rep = sum(w[n].size for n in _REPLICATED) // 128
    summed = _sum_devices(gathered, small_pack, 4 * xi + 2 * yi + ci)
    res = _reduce_adam([summed[:n_rep]], _pack([w[n] for n in _REPLICATED]), _pack([a["m_" + n] for n in _REPLICATED]),
                       _pack([a["v_" + n] for n in _REPLICATED]), name="adam_replicated")
    off = 0
    for n in _REPLICATED:
        out[n] = tuple(r.reshape(-1)[off:off + w[n].size].reshape(w[n].shape) for r in res)
        off += w[n].size
    conv_full = summed[n_rep:].reshape(conv_g.shape)
    adam("conv_w", lax.dynamic_slice_in_dim(conv_full, (2 * xi + yi) * 128, 128, axis=2))

    return (loss, dx[None], *[out[n][k] for k in range(4) for n in _WEIGHTS])


def kernel(x, mem, norm_mix_g, w_in, sgu_ln_g, sgu_ln_b, w_spatial, b_spatial, conv_w, w_branch, w_out, norm_xa_g, mem_norm_g, w_q_xa, w_k_xa, w_v_xa, w_o_xa, norm_ffn_g, w_gate_ffn, w_up_ffn, w_down_ffn, final_g, loss_target, m_norm_mix_g, m_w_in, m_sgu_ln_g, m_sgu_ln_b, m_w_spatial, m_b_spatial, m_conv_w, m_w_branch, m_w_out, m_norm_xa_g, m_mem_norm_g, m_w_q_xa, m_w_k_xa, m_w_v_xa, m_w_o_xa, m_norm_ffn_g, m_w_gate_ffn, m_w_up_ffn, m_w_down_ffn, m_final_g, v_norm_mix_g, v_w_in, v_sgu_ln_g, v_sgu_ln_b, v_w_spatial, v_b_spatial, v_conv_w, v_w_branch, v_w_out, v_norm_xa_g, v_mem_norm_g, v_w_q_xa, v_w_k_xa, v_w_v_xa, v_w_o_xa, v_norm_ffn_g, v_w_gate_ffn, v_w_up_ffn, v_w_down_ffn, v_final_g):
    return _step(dict(locals()))
```

```python
import math

import jax
import jax.numpy as jnp
from jax import lax
from jax.experimental import pallas as pl
from jax.experimental.pallas import tpu as pltpu

F32, BF16 = jnp.float32, jnp.bfloat16
MESH = pl.DeviceIdType.MESH
ANY = pl.BlockSpec(memory_space=pl.ANY)

D_MODEL = 1024
DEPTH = 2
BW = 512
SB_HEADS, SB_DH = 8, 64
SGU_LEN, SGU_GROUPS, SGU_GD, SGU_CHUNK = 128, 4, 128, 64
XA_HEADS, XA_DH = 4, 256
FFN_SH = 704
N_CHIPS = 4
C_Z, C_CB, C_GATES = 1536, 2560, 4096

ADAM_LR, ADAM_B1, ADAM_B2, ADAM_EPS, ADAM_WD, ADAM_STEP = 0.001, 0.9, 0.999, 1e-08, 0.01, 10

VMEM_LIMIT_V7X = 56 * 1024 * 1024

NN = (((1,), (0,)), ((), ()))
NT = (((1,), (1,)), ((), ()))
TN = (((0,), (0,)), ((), ()))


def _cp(*sem):
    return pltpu.CompilerParams(dimension_semantics=sem, vmem_limit_bytes=VMEM_LIMIT_V7X)


def _tile(n, pref):
    for t in pref:
        if n % t == 0:
            return t
    return n


def _rows(r, row_bytes, block_bytes=1 << 20):
    fits = [t for t in range(8, r + 1, 8) if r % t == 0 and t * row_bytes <= block_bytes]
    return max(fits) if fits else r


class _Rider:
    def __init__(self, ins, outs, n_sems, start, finish, alias=None, middle=None):
        self.ins, self.outs, self.n_sems, self.alias = list(ins), list(outs), n_sems, alias or {}
        self.start, self.middle, self.finish = start, middle, finish


class _Sems:
    def __init__(self, ref, first):
        self.ref, self.first = ref, first

    @property
    def at(self):
        return self

    def __getitem__(self, k):
        return self.ref.at[self.first + k]


def _join(riders):
    riders = [r for r in riders if r is not None]
    if not riders:
        return None
    spans, i0, o0, s0 = [], 0, 0, 0
    for r in riders:
        spans.append((r, i0, o0, s0))
        i0, o0, s0 = i0 + len(r.ins), o0 + len(r.outs), s0 + r.n_sems

    def phase(which):
        def run(ins, outs, send, recv):
            for r, i, o, s in spans:
                fn = getattr(r, which)
                if fn is not None:
                    fn(ins[i:i + len(r.ins)], outs[o:o + len(r.outs)], _Sems(send, s), _Sems(recv, s))
        return run

    joined = _Rider([a for r in riders for a in r.ins], [a for r in riders for a in r.outs], s0, phase("start"), phase("finish"),
                    alias={o + k: i + v for r, i, o, s in spans for k, v in r.alias.items()}, middle=phase("middle"))
    joined.split = lambda landed: [list(landed[o:o + len(r.outs)]) for r, i, o, s in spans]
    return joined


def _call_with_rider(rider, body, *, name, grid, in_specs, args, out_specs, out_shape, scratch_shapes, semantics):
    if rider is None:
        return pl.pallas_call(body, name=name, grid=grid, in_specs=in_specs, out_specs=out_specs, out_shape=out_shape,
                              scratch_shapes=scratch_shapes, compiler_params=_cp(*semantics))(*args)
    n_in, n_out, r_in, r_out = len(args), len(out_shape), len(rider.ins), len(rider.outs)

    def riding(*refs):
        ins, rins = refs[:n_in], refs[n_in:n_in + r_in]
        outs, routs = refs[n_in + r_in:n_in + r_in + n_out], refs[n_in + r_in + n_out:n_in + r_in + n_out + r_out]
        rest = refs[n_in + r_in + n_out + r_out:]
        scratch, send, recv = rest[:-2], rest[-2], rest[-1]
        step = pl.program_id(0)
        for ax in range(1, len(grid)):
            step = step * grid[ax] + pl.program_id(ax)
        n_steps = math.prod(grid)

        @pl.when(step == 0)
        def _():
            rider.start(rins, routs, send, recv)

        body(*ins, *outs, *scratch)

        if rider.middle is not None:
            @pl.when(step == (17 * n_steps) // 20)
            def _():
                rider.middle(rins, routs, send, recv)

        @pl.when(step == n_steps - 1)
        def _():
            rider.finish(rins, routs, send, recv)

    return pl.pallas_call(
        riding, name=name, grid=grid, in_specs=list(in_specs) + [ANY] * r_in, out_specs=list(out_specs) + [ANY] * r_out,
        out_shape=list(out_shape) + rider.outs, scratch_shapes=list(scratch_shapes) + [_dma_sems(rider.n_sems), _dma_sems(rider.n_sems)],
        input_output_aliases={n_in + i: n_out + o for o, i in rider.alias.items()},
        compiler_params=_cp(*["arbitrary"] * len(grid)),
    )(*args, *rider.ins)


def _run_rider(rider, *, name):
    def nothing(*refs):
        pass

    return _call_with_rider(rider, nothing, name=name, grid=(1,), in_specs=[], args=[], out_specs=[], out_shape=[], scratch_shapes=[],
                            semantics=("arbitrary",))


def _mm(a, b, *, mode, name, out_dtype=F32, res=None, norm=None, rms_bwd=None, a_kind="2d", b_kind="2d", tm=None, tn=None, tk=None, rider=None):
    a2, b2 = a.shape[-2:], b.shape[-2:]
    if mode == "nn":
        (M, K), N = a2, b2[1]
    elif mode == "nt":
        (M, K), N = a2, b2[0]
    else:
        (K, M), N = a2, b2[1]
    kchunk = a_kind == "kchunk" or b_kind == "kchunk"
    batch = a_kind == "batch" or b_kind == "batch"
    G = (a.shape[0] if a_kind == "batch" else b.shape[0]) if batch else 1
    tm = tm or _tile(M, (1024, 512, 256, 128))
    tn = tn or _tile(N, (1024, 512, 256, 128))
    if kchunk:
        tk, nk = K, (a.shape[0] if a_kind == "kchunk" else b.shape[0])
    else:
        tk = tk or _tile(K, (1024, 512, 256, 128))
        nk = K // tk

    def spec(kind, blk, idx):
        if kind == "2d":
            return pl.BlockSpec(blk, lambda g, i, j, k: idx(g, i, j, k))
        if kind == "batch":
            return pl.BlockSpec((None,) + blk, lambda g, i, j, k: (g,) + idx(g, i, j, k))
        return pl.BlockSpec((None,) + blk, lambda g, i, j, k: (k,) + idx(g, i, j, 0))

    if mode == "nn":
        a_spec = spec(a_kind, (tm, tk), lambda g, i, j, k: (i, k))
        b_spec = spec(b_kind, (tk, tn), lambda g, i, j, k: (k, j))
    elif mode == "nt":
        a_spec = spec(a_kind, (tm, tk), lambda g, i, j, k: (i, k))
        b_spec = spec(b_kind, (tn, tk), lambda g, i, j, k: (j, k))
    else:
        a_spec = spec(a_kind, (tk, tm), lambda g, i, j, k: (k, i))
        b_spec = spec(b_kind, (tk, tn), lambda g, i, j, k: (k, j))
    o_kind = "batch" if batch else "2d"
    o_spec = spec(o_kind, (tm, tn), lambda g, i, j, k: (i, j))
    o_shape = ((G,) if batch else ()) + (M, N)
    dn = {"nn": NN, "nt": NT, "tn": TN}[mode]
    has_res, has_rms, has_norm = res is not None, rms_bwd is not None, norm is not None
    assert not (has_res and has_rms) and (not (has_rms or has_norm) or (tn == N and G == 1)) and (has_res or not has_norm)

    def body(*refs):
        if has_norm:
            a_ref, b_ref, r_ref, g_ref, o_ref, h_ref = refs[:6]
        elif has_res:
            a_ref, b_ref, r_ref, o_ref = refs[:4]
        elif has_rms:
            a_ref, b_ref, x_ref, g_ref, dres_ref, o_ref, dg_ref = refs[:7]
        else:
            a_ref, b_ref, o_ref = refs[:3]
        p = lax.dot_general(a_ref[...].astype(BF16), b_ref[...].astype(BF16), dn, preferred_element_type=F32)
        first_rows = pl.program_id(1) == 0

        def finish(r):
            if has_rms:
                xv = x_ref[...]
                rs = lax.rsqrt(jnp.mean(xv * xv, axis=-1, keepdims=True) + 1e-6)
                u = r * g_ref[...]
                s = jnp.sum(u * xv, axis=-1, keepdims=True)
                o_ref[...] = dres_ref[...] + rs * u - xv * ((rs * rs * rs) * (s * (1.0 / N)))
                part = jnp.sum(r * (xv * rs), axis=0, keepdims=True)

                @pl.when(first_rows)
                def _():
                    dg_ref[...] = part

                @pl.when(jnp.logical_not(first_rows))
                def _():
                    dg_ref[...] += part
                return
            if has_res:
                r = r + r_ref[...]
            o_ref[...] = r.astype(out_dtype)
            if has_norm:
                h_ref[...] = (r * lax.rsqrt(jnp.mean(r * r, axis=-1, keepdims=True) + 1e-6) * g_ref[...]).astype(BF16)

        if nk == 1:
            finish(p)
        else:
            acc = refs[-1]
            k = pl.program_id(3)

            @pl.when(k == 0)
            def _():
                acc[...] = p

            @pl.when(k > 0)
            def _():
                acc[...] += p

            @pl.when(k == nk - 1)
            def _():
                finish(acc[...])

    in_specs, args = [a_spec, b_spec], [a, b]
    tile = spec("2d", (tm, tn), lambda g, i, j, k: (i, j))
    out_specs, out_shape = [o_spec], [jax.ShapeDtypeStruct(o_shape, out_dtype)]
    vec = pl.BlockSpec((1, tn), lambda g, i, j, k: (0, 0))
    if has_res:
        in_specs.append(tile)
        args.append(res)
    if has_norm:
        in_specs.append(vec)
        args.append(norm.reshape(1, N))
        out_specs.append(tile)
        out_shape.append(jax.ShapeDtypeStruct((M, N), BF16))
    if has_rms:
        x, gain, dres = rms_bwd
        in_specs += [tile, vec, tile]
        args += [x, gain.reshape(1, N), dres]
        out_specs.append(vec)
        out_shape.append(jax.ShapeDtypeStruct((1, N), F32))
    outs = _call_with_rider(
        rider, body, name=name, grid=(G, M // tm, N // tn, nk), in_specs=in_specs, args=args, out_specs=out_specs,
        out_shape=out_shape, scratch_shapes=[pltpu.VMEM((tm, tn), F32)] if nk > 1 else [],
        semantics=("parallel", "arbitrary" if has_rms else "parallel", "parallel", "arbitrary"))
    main = (outs[0], outs[1].reshape(N)) if has_rms else (outs[0], outs[1]) if has_norm else outs[0]
    return main if rider is None else (main, outs[len(out_shape):])


def _rms_fwd(x, g, *, name):
    S, Dm = x.shape
    tm = _tile(S, (512, 256))

    def body(x_ref, g_ref, o_ref):
        xv = x_ref[...]
        r = lax.rsqrt(jnp.mean(xv * xv, axis=-1, keepdims=True) + 1e-6)
        o_ref[...] = (xv * r * g_ref[...]).astype(BF16)

    return pl.pallas_call(
        body, name=name, grid=(S // tm,),
        in_specs=[pl.BlockSpec((tm, Dm), lambda i: (i, 0)), pl.BlockSpec((1, Dm), lambda i: (0, 0))],
        out_specs=pl.BlockSpec((tm, Dm), lambda i: (i, 0)), out_shape=jax.ShapeDtypeStruct((S, Dm), BF16),
        compiler_params=_cp("parallel"),
    )(x, g.reshape(1, Dm))


def _rms_bwd(x, g, dh, dres, *, name):
    S, Dm = x.shape
    tm = _tile(S, (512, 256))

    def body(x_ref, g_ref, dh_ref, dr_ref, dx_ref, dg_ref):
        xv, dhv = x_ref[...], dh_ref[...].astype(F32)
        r = lax.rsqrt(jnp.mean(xv * xv, axis=-1, keepdims=True) + 1e-6)
        u = dhv * g_ref[...]
        s = jnp.sum(u * xv, axis=-1, keepdims=True)
        dx_ref[...] = dr_ref[...] + r * u - xv * ((r * r * r) * (s * (1.0 / Dm)))
        part = jnp.sum(dhv * (xv * r), axis=0, keepdims=True)

        @pl.when(pl.program_id(0) == 0)
        def _():
            dg_ref[...] = part

        @pl.when(pl.program_id(0) > 0)
        def _():
            dg_ref[...] += part

    row = pl.BlockSpec((tm, Dm), lambda i: (i, 0))
    vec = pl.BlockSpec((1, Dm), lambda i: (0, 0))
    dx, dg = pl.pallas_call(
        body, name=name, grid=(S // tm,), in_specs=[row, vec, row, row], out_specs=[row, vec],
        out_shape=[jax.ShapeDtypeStruct((S, Dm), F32), jax.ShapeDtypeStruct((1, Dm), F32)],
        compiler_params=_cp("arbitrary"),
    )(x, g.reshape(1, Dm), dh, dres)
    return dx, dg.reshape(Dm)


def _loss_head(x, g, target):
    S, Dm = x.shape
    tm = _tile(S, (512, 256))

    def body(x_ref, g_ref, t_ref, dx_ref, dg_ref, loss_ref):
        xv, gv = x_ref[...], g_ref[...]
        r = lax.rsqrt(jnp.mean(xv * xv, axis=-1, keepdims=True) + 1e-6)
        xn = xv * r
        err = xn * gv - t_ref[...]
        lpart = 0.5 * jnp.sum(jnp.mean(err * err, axis=-1, keepdims=True), axis=0, keepdims=True)
        dy = err * (1.0 / Dm)
        u = dy * gv
        s = jnp.sum(u * xv, axis=-1, keepdims=True)
        dx_ref[...] = r * u - xv * ((r * r * r) * (s * (1.0 / Dm)))
        part = jnp.sum(dy * xn, axis=0, keepdims=True)
        lslab = jnp.broadcast_to(lpart, (8, 128))

        @pl.when(pl.program_id(0) == 0)
        def _():
            dg_ref[...] = part
            loss_ref[...] = lslab

        @pl.when(pl.program_id(0) > 0)
        def _():
            dg_ref[...] += part
            loss_ref[...] += lslab

    row = pl.BlockSpec((tm, Dm), lambda i: (i, 0))
    vec = pl.BlockSpec((1, Dm), lambda i: (0, 0))
    dx, dg, loss = pl.pallas_call(
        body, name="loss_head", grid=(S // tm,), in_specs=[row, vec, row],
        out_specs=[row, vec, pl.BlockSpec((8, 128), lambda i: (0, 0))],
        out_shape=[jax.ShapeDtypeStruct((S, Dm), F32), jax.ShapeDtypeStruct((1, Dm), F32), jax.ShapeDtypeStruct((8, 128), F32)],
        compiler_params=_cp("arbitrary"),
    )(x, g.reshape(1, Dm), target)
    return loss[0, 0], dx, dg.reshape(Dm)


SB_TQ, SB_TK = 256, 256
SB_EXP_FLOOR = -104.0


def _split2(v):
    hi = v.astype(BF16)
    return jnp.concatenate([hi, (v - hi.astype(F32)).astype(BF16)], axis=1)


def _tri2(cmp):
    j = lax.broadcasted_iota(jnp.int32, (2 * SB_TK, SB_TK), 0) % SB_TK
    s = lax.broadcasted_iota(jnp.int32, (2 * SB_TK, SB_TK), 1)
    return cmp(j, s).astype(BF16)


_TRI_SPEC = pl.BlockSpec((2 * SB_TK, SB_TK), lambda g, i: (0, 0))


def _sb_scores(qv, kb, k0, q0, tq):
    rows = qv.shape[0]
    z = lax.dot_general(qv, kb, NT, preferred_element_type=F32) * (SB_DH ** -0.5)
    t_pos = q0 + lax.broadcasted_iota(jnp.int32, (rows, SB_TK), 0) % tq
    s_pos = k0 + lax.broadcasted_iota(jnp.int32, (rows, SB_TK), 1)
    valid = s_pos < t_pos
    ls = jnp.minimum(z, 0.0) - jnp.log(1.0 + jnp.exp(-jnp.abs(z)))
    l1m = jnp.where(valid, ls - z, 0.0)
    return z, valid, ls, l1m


SB_PAIRS = SB_HEADS // 2
_Q_BLK, _K_BLK, _V_BLK = 0, SB_PAIRS, 2 * SB_PAIRS


def _wide(x):
    return x if SB_TK == 128 else jnp.concatenate([x] * (SB_TK // 128), axis=1)


def _lanes_of(h, shape):
    lane = lax.broadcasted_iota(jnp.int32, shape, len(shape) - 1)
    return (lane < SB_DH) if h == 0 else (lane >= SB_DH)


def _sb2_fwd(p, rider=None):
    S = p.shape[0]
    tq = min(SB_TQ, S)
    kb_per_q = tq // SB_TK

    def body(q_ref, k_ref, v_ref, later_ref, o_ref, tot_ref, cnt_ref, qm, acc, c):
        i = pl.program_id(1)
        q0 = i * tq
        q2 = q_ref[...]
        for h in range(2):
            qm[h * tq:(h + 1) * tq, :] = jnp.where(_lanes_of(h, q2.shape), q2, 0.0).astype(BF16)
        acc[...] = jnp.zeros_like(acc)
        c[...] = jnp.zeros_like(c)
        nkb = (i + 1) * kb_per_q

        def more(st):
            n, highest = st
            return (n < nkb) & (highest > SB_EXP_FLOOR)

        def step(st):
            n, _ = st
            k0 = pl.multiple_of((nkb - 1 - n) * SB_TK, SB_TK)
            kb, vb = k_ref[pl.ds(k0, SB_TK), :].astype(BF16), v_ref[pl.ds(k0, SB_TK), :].astype(BF16)
            c_old = c[...]
            z, valid, ls, l1m = _sb_scores(qm[...], kb, k0, q0, tq)
            c_new = c_old + jnp.sum(l1m, axis=1, keepdims=True)
            after = jnp.dot(_split2(l1m), later_ref[...], preferred_element_type=F32)
            a = jnp.where(valid, jnp.exp(ls + after + _wide(c_old)), 0.0)
            av = jnp.dot(a.astype(BF16), vb, preferred_element_type=F32)
            acc[...] += jnp.where(_lanes_of(0, (tq, 128)), av[:tq], av[tq:])
            c[...] = c_new
            return n + 1, jnp.max(c_new)

        n_done, _ = lax.while_loop(more, step, (jnp.int32(0), jnp.float32(0.0)))
        o_ref[...] = acc[...].astype(o_ref.dtype)
        for h in range(2):
            tot_ref[h] = c[h * tq:(h + 1) * tq, :]
        cnt_ref[...] = jnp.full(cnt_ref.shape, n_done.astype(F32))

    col = lambda first: pl.BlockSpec((S, 128), lambda g, i: (0, first + g))
    outs = _call_with_rider(
        rider, body, name="sb_fwd", grid=(SB_PAIRS, S // tq), args=[p, p, p, _tri2(lambda j, s: j > s)],
        in_specs=[pl.BlockSpec((tq, 128), lambda g, i: (i, _Q_BLK + g)), col(_K_BLK), col(_V_BLK), _TRI_SPEC],
        out_specs=[pl.BlockSpec((tq, 128), lambda g, i: (i, g)), pl.BlockSpec((2, tq, 128), lambda g, i: (g, i, 0)),
                   pl.BlockSpec((None, None, 8, 128), lambda g, i: (g, i, 0, 0))],
        out_shape=[jax.ShapeDtypeStruct((S, BW), BF16), jax.ShapeDtypeStruct((SB_HEADS, S, 128), F32),
                   jax.ShapeDtypeStruct((SB_PAIRS, S // tq, 8, 128), F32)],
        scratch_shapes=[pltpu.VMEM((2 * tq, 128), BF16), pltpu.VMEM((tq, 128), F32), pltpu.VMEM((2 * tq, 128), F32)],
        semantics=("parallel", "parallel"))
    return outs[:3], outs[3:]


def _sb2_bwd(p, dbr, tot, cnt, rider=None):
    S = p.shape[0]
    tq = min(SB_TQ, S)
    kb_per_q = tq // SB_TK
    scale = SB_DH ** -0.5

    def body(q_ref, k_ref, v_ref, do_ref, tot_ref, cnt_ref, upto_ref, before_ref, dq_ref, dk_ref, dv_ref, qm, dom, tot, dq_acc, pre, gpre):
        i = pl.program_id(1)
        q0 = i * tq

        @pl.when(i == 0)
        def _():
            dk_ref[...] = jnp.zeros_like(dk_ref)
            dv_ref[...] = jnp.zeros_like(dv_ref)

        q2, do2 = q_ref[...], do_ref[...]
        for h in range(2):
            rows = slice(h * tq, (h + 1) * tq)
            qm[rows, :] = jnp.where(_lanes_of(h, q2.shape), q2, 0.0).astype(BF16)
            dom[rows, :] = jnp.where(_lanes_of(h, do2.shape), do2, 0.0).astype(BF16)
            tot[rows, :] = tot_ref[h]
        dq_acc[...] = jnp.zeros_like(dq_acc)
        pre[...] = jnp.zeros_like(pre)
        gpre[...] = jnp.zeros_like(gpre)

        n_done = jnp.max(cnt_ref[...]).astype(jnp.int32)
        first = (i + 1) * kb_per_q - n_done

        def step(n, carry):
            k0 = pl.multiple_of((first + n) * SB_TK, SB_TK)
            kb, vb = k_ref[pl.ds(k0, SB_TK), :].astype(BF16), v_ref[pl.ds(k0, SB_TK), :].astype(BF16)
            pre_o, gpre_o = pre[...], gpre[...]
            z, valid, ls, l1m = _sb_scores(qm[...], kb, k0, q0, tq)
            incl = jnp.dot(_split2(l1m), upto_ref[...], preferred_element_type=F32)
            rest = _wide(tot[...] - pre_o) - incl
            a = jnp.where(valid, jnp.exp(ls + rest), 0.0)
            da = lax.dot_general(dom[...], vb, NT, preferred_element_type=F32)
            g = a * da
            gbefore = jnp.dot(_split2(g), before_ref[...], preferred_element_type=F32) + _wide(gpre_o)
            dz = jnp.where(valid, g * jnp.exp(ls - z) - jnp.exp(ls) * gbefore, 0.0) * scale
            dzb = dz.astype(BF16)
            dq_p = jnp.dot(dzb, kb, preferred_element_type=F32)
            dq_acc[...] += jnp.where(_lanes_of(0, (tq, 128)), dq_p[:tq], dq_p[tq:])
            dk_ref[pl.ds(k0, SB_TK), :] += lax.dot_general(dzb, qm[...], TN, preferred_element_type=F32)
            dv_ref[pl.ds(k0, SB_TK), :] += lax.dot_general(a.astype(BF16), dom[...], TN, preferred_element_type=F32)
            pre[...] = pre_o + jnp.sum(l1m, axis=1, keepdims=True)
            gpre[...] = gpre_o + jnp.sum(g, axis=1, keepdims=True)
            return carry

        lax.fori_loop(0, n_done, step, 0)
        dq_ref[...] = dq_acc[...].astype(dq_ref.dtype)

    col = lambda first: pl.BlockSpec((S, 128), lambda g, i: (0, first + g))
    tile = pl.BlockSpec((tq, 128), lambda g, i: (i, g))
    whole = pl.BlockSpec((S, 128), lambda g, i: (0, g))
    outs = _call_with_rider(
        rider, body, name="sb_bwd", grid=(SB_PAIRS, S // tq), args=[p, p, p, dbr, tot, cnt, _tri2(lambda j, s: j <= s), _tri2(lambda j, s: j < s)],
        in_specs=[pl.BlockSpec((tq, 128), lambda g, i: (i, _Q_BLK + g)), col(_K_BLK), col(_V_BLK),
                  pl.BlockSpec((None, tq, 128), lambda g, i: (0, i, g)), pl.BlockSpec((2, tq, 128), lambda g, i: (g, i, 0)),
                  pl.BlockSpec((None, None, 8, 128), lambda g, i: (g, i, 0, 0)), _TRI_SPEC, _TRI_SPEC],
        out_specs=[tile, whole, whole],
        out_shape=[jax.ShapeDtypeStruct((S, BW), BF16), jax.ShapeDtypeStruct((S, BW), F32), jax.ShapeDtypeStruct((S, BW), F32)],
        scratch_shapes=[pltpu.VMEM((2 * tq, 128), BF16), pltpu.VMEM((2 * tq, 128), BF16), pltpu.VMEM((2 * tq, 128), F32),
                        pltpu.VMEM((tq, 128), F32), pltpu.VMEM((2 * tq, 128), F32), pltpu.VMEM((2 * tq, 128), F32)],
        semantics=("parallel", "arbitrary"))
    return outs[:3], outs[3:]


_INV_SQRT2 = 0.7071067811865476
_INV_SQRT2PI = 0.3989422804014327


def _gelu(x):
    return 0.5 * x * (1.0 + lax.erf(x * _INV_SQRT2))


def _gelu_grad(x):
    return 0.5 * (1.0 + lax.erf(x * _INV_SQRT2)) + x * (_INV_SQRT2PI * jnp.exp(-0.5 * x * x))


def _sgu_mask():
    t = lax.broadcasted_iota(jnp.int32, (SGU_LEN, SGU_LEN), 0) // SGU_CHUNK
    s = lax.broadcasted_iota(jnp.int32, (SGU_LEN, SGU_LEN), 1) // SGU_CHUNK
    return t >= s


def _sgu_mask_t():
    t = lax.broadcasted_iota(jnp.int32, (SGU_LEN, SGU_LEN), 0) // SGU_CHUNK
    s = lax.broadcasted_iota(jnp.int32, (SGU_LEN, SGU_LEN), 1) // SGU_CHUNK
    return s >= t


def _sgu_norm(zv, g, b):
    vv = _gelu(zv)
    xc = vv - jnp.mean(vv, axis=-1, keepdims=True)
    rstd = lax.rsqrt(jnp.mean(xc * xc, axis=-1, keepdims=True) + 1e-5)
    xhat = xc * rstd
    return xhat, rstd, xhat * g + b


SGU_TM = 256


def _sgu_fwd(p, ln_g, ln_b, w_s, b_st):
    S = p.shape[0]
    tm = min(SGU_TM, S)

    def body(zu_ref, zv_ref, g_ref, b_ref, w_ref, bs_ref, o_ref):
        u = _gelu(zu_ref[...].astype(F32))
        _, _, vn = _sgu_norm(zv_ref[...].astype(F32), g_ref[...], b_ref[...])
        vnb = vn.astype(BF16)
        mask = _sgu_mask()
        for gi in range(SGU_GROUPS):
            wg = jnp.where(mask, w_ref[gi], 0.0).astype(BF16)
            cols = slice(gi * SGU_GD, (gi + 1) * SGU_GD)
            for ci in range(tm // SGU_LEN):
                rows = slice(ci * SGU_LEN, (ci + 1) * SGU_LEN)
                vm = jnp.dot(wg, vnb[rows, cols], preferred_element_type=F32) + bs_ref[:, gi:gi + 1]
                o_ref[rows, cols] = (u[rows, cols] * vm).astype(BF16)

    vec = pl.BlockSpec((1, BW), lambda i: (0, 0))
    return pl.pallas_call(
        body, name="sgu_fwd", grid=(S // tm,),
        in_specs=[pl.BlockSpec((tm, BW), lambda i: (i, C_Z // BW)), pl.BlockSpec((tm, BW), lambda i: (i, C_Z // BW + 1)), vec, vec,
                  pl.BlockSpec((SGU_GROUPS, SGU_LEN, SGU_LEN), lambda i: (0, 0, 0)), pl.BlockSpec((SGU_LEN, SGU_GROUPS), lambda i: (0, 0))],
        out_specs=pl.BlockSpec((tm, BW), lambda i: (i, 0)), out_shape=jax.ShapeDtypeStruct((S, BW), BF16),
        compiler_params=_cp("parallel"),
    )(p, p, ln_g.reshape(1, BW), ln_b.reshape(1, BW), w_s, b_st)


def _sgu_bwd(p, dyb, ln_g, ln_b, w_s, w_st, b_st):
    S = p.shape[0]
    tm = min(SGU_TM, S)

    def body(zu_ref, zv_ref, dy_ref, g_ref, b_ref, w_ref, wt_ref, bs_ref, dz_ref, dg_ref, db_ref, dw_ref, dbs_ref, dvn):
        first = pl.program_id(0) == 0

        @pl.when(first)
        def _():
            dg_ref[...] = jnp.zeros_like(dg_ref)
            db_ref[...] = jnp.zeros_like(db_ref)
            dw_ref[...] = jnp.zeros_like(dw_ref)
            dbs_ref[...] = jnp.zeros_like(dbs_ref)

        zu, zv, dy = zu_ref[...].astype(F32), zv_ref[...].astype(F32), dy_ref[...].astype(F32)
        u = _gelu(zu)
        xhat, rstd, vn = _sgu_norm(zv, g_ref[...], b_ref[...])
        vnb = vn.astype(BF16)
        mask = _sgu_mask()
        mask_t = _sgu_mask_t()
        for gi in range(SGU_GROUPS):
            wg = jnp.where(mask, w_ref[gi], 0.0).astype(BF16)
            wgt = jnp.where(mask_t, wt_ref[gi], 0.0).astype(BF16)
            cols = slice(gi * SGU_GD, (gi + 1) * SGU_GD)
            for ci in range(tm // SGU_LEN):
                rows = slice(ci * SGU_LEN, (ci + 1) * SGU_LEN)
                vm = jnp.dot(wg, vnb[rows, cols], preferred_element_type=F32) + bs_ref[:, gi:gi + 1]
                dyc = dy[rows, cols]
                dz_ref[rows, cols] = (dyc * vm * _gelu_grad(zu[rows, cols])).astype(BF16)
                dvm = dyc * u[rows, cols]
                dvmb = dvm.astype(BF16)
                dbs_ref[gi] += jnp.broadcast_to(jnp.sum(dvm, axis=1, keepdims=True), (SGU_LEN, SGU_GD))
                dw_ref[gi] += lax.dot_general(dvmb, vnb[rows, cols], NT, preferred_element_type=F32)
                dvn[rows, cols] = jnp.dot(wgt, dvmb, preferred_element_type=F32)
        dvnv = dvn[...]
        dg_ref[...] += jnp.sum(dvnv * xhat, axis=0, keepdims=True)
        db_ref[...] += jnp.sum(dvnv, axis=0, keepdims=True)
        dxh = dvnv * g_ref[...]
        dvv = rstd * (dxh - jnp.mean(dxh, axis=-1, keepdims=True) - xhat * jnp.mean(dxh * xhat, axis=-1, keepdims=True))
        dz_ref[:, BW:] = (dvv * _gelu_grad(zv)).astype(BF16)

        @pl.when(pl.program_id(0) == n_steps - 1)
        def _():
            for gi in range(SGU_GROUPS):
                dw_ref[gi] = jnp.where(mask, dw_ref[gi], 0.0)

    n_steps = S // tm
    vec = pl.BlockSpec((1, BW), lambda i: (0, 0))
    half = lambda c: pl.BlockSpec((tm, BW), lambda i: (i, c))
    wspec = pl.BlockSpec((SGU_GROUPS, SGU_LEN, SGU_LEN), lambda i: (0, 0, 0))
    dz, dg, db, dw, dbs = pl.pallas_call(
        body, name="sgu_bwd", grid=(n_steps,),
        in_specs=[half(C_Z // BW), half(C_Z // BW + 1), half(0), vec, vec, wspec, wspec,
                  pl.BlockSpec((SGU_LEN, SGU_GROUPS), lambda i: (0, 0))],
        out_specs=[pl.BlockSpec((tm, 2 * BW), lambda i: (i, 0)), vec, vec, wspec, wspec],
        out_shape=[jax.ShapeDtypeStruct((S, 2 * BW), BF16), jax.ShapeDtypeStruct((1, BW), F32), jax.ShapeDtypeStruct((1, BW), F32),
                   jax.ShapeDtypeStruct((SGU_GROUPS, SGU_LEN, SGU_LEN), F32), jax.ShapeDtypeStruct((SGU_GROUPS, SGU_LEN, SGU_GD), F32)],
        scratch_shapes=[pltpu.VMEM((tm, BW), F32)],
        compiler_params=_cp("arbitrary"),
    )(p, p, dyb, ln_g.reshape(1, BW), ln_b.reshape(1, BW), w_s, w_st, b_st)
    return dz, dg.reshape(BW), db.reshape(BW), dw, dbs[:, :, 0]


CONV_TC = 128


def _shift_down(y, n):
    rows = lax.broadcasted_iota(jnp.int32, y.shape, 0)
    return jnp.where(rows < n, 0.0, pltpu.roll(y, n, 0))


def _shift_up(y, n):
    rows = lax.broadcasted_iota(jnp.int32, y.shape, 0)
    return jnp.where(rows >= y.shape[0] - n, 0.0, pltpu.roll(y, y.shape[0] - n, 0))


def _conv_specs(S):
    col = lambda c0: pl.BlockSpec((S, CONV_TC), lambda j: (0, c0 // CONV_TC + j))
    return col(C_CB), col(C_CB + BW), col(C_CB + 2 * BW), pl.BlockSpec((3, CONV_TC), lambda j: (0, j)), pl.BlockSpec((S, CONV_TC), lambda j: (0, j))


def _conv_fwd(p, conv_w):
    S = p.shape[0]

    def body(cb_ref, cc_ref, cx_ref, w_ref, o_ref):
        y = cc_ref[...].astype(F32) * cx_ref[...].astype(F32)
        conv = w_ref[0:1, :] * _shift_down(y, 2) + w_ref[1:2, :] * _shift_down(y, 1) + w_ref[2:3, :] * y
        o_ref[...] = (cb_ref[...].astype(F32) * conv).astype(BF16)

    cb, cc, cx, wspec, out = _conv_specs(S)
    return pl.pallas_call(
        body, name="conv_fwd", grid=(BW // CONV_TC,), in_specs=[cb, cc, cx, wspec], out_specs=out,
        out_shape=jax.ShapeDtypeStruct((S, BW), BF16), compiler_params=_cp("parallel"),
    )(p, p, p, conv_w)


def _conv_bwd(p, conv_w, dyc):
    S = p.shape[0]

    def body(cb_ref, cc_ref, cx_ref, w_ref, dy_ref, db_ref, dc_ref, dx_ref, dw_ref):
        cc, cx, dy = cc_ref[...].astype(F32), cx_ref[...].astype(F32), dy_ref[...].astype(F32)
        y = cc * cx
        w0, w1, w2 = w_ref[0:1, :], w_ref[1:2, :], w_ref[2:3, :]
        y1, y2 = _shift_down(y, 1), _shift_down(y, 2)
        conv = w0 * y2 + w1 * y1 + w2 * y
        db_ref[...] = (dy * conv).astype(BF16)
        dconv = dy * cb_ref[...].astype(F32)
        dyy = w2 * dconv + w1 * _shift_up(dconv, 1) + w0 * _shift_up(dconv, 2)
        dc_ref[...] = (dyy * cx).astype(BF16)
        dx_ref[...] = (dyy * cc).astype(BF16)
        dw_ref[0:1, :] = jnp.sum(dconv * y2, axis=0, keepdims=True)
        dw_ref[1:2, :] = jnp.sum(dconv * y1, axis=0, keepdims=True)
        dw_ref[2:3, :] = jnp.sum(dconv * y, axis=0, keepdims=True)

    cb, cc, cx, wspec, out = _conv_specs(S)
    db, dc, dx, dw = pl.pallas_call(
        body, name="conv_bwd", grid=(BW // CONV_TC,), in_specs=[cb, cc, cx, wspec, out],
        out_specs=[out, out, out, wspec],
        out_shape=[jax.ShapeDtypeStruct((S, BW), BF16)] * 3 + [jax.ShapeDtypeStruct((3, BW), F32)],
        compiler_params=_cp("parallel"),
    )(p, p, p, conv_w, dyc)
    return db, dc, dx, dw


def _merge_specs(S, tm):
    gate = lambda n: pl.BlockSpec((tm, D_MODEL), lambda i: (i, C_GATES // D_MODEL + n))
    return [gate(0), gate(1), gate(2)], pl.BlockSpec((3, tm, D_MODEL), lambda i: (0, i, 0)), pl.BlockSpec((tm, D_MODEL), lambda i: (i, 0))


def _branch_merge(br, w_br, p):
    S = p.shape[0]
    tm = _tile(S, (512, 256))

    def body(br_ref, w_ref, g0, g1, g2, bd_ref, o_ref):
        acc = None
        for n, g_ref in enumerate((g0, g1, g2)):
            bdn = jnp.dot(br_ref[n], w_ref[n], preferred_element_type=F32)
            bd_ref[n] = bdn.astype(BF16)
            term = jax.nn.sigmoid(g_ref[...].astype(F32)) * bdn
            acc = term if acc is None else acc + term
        o_ref[...] = acc.astype(BF16)

    gates, bspec, row = _merge_specs(S, tm)
    return pl.pallas_call(
        body, name="mm_branch", grid=(S // tm,),
        in_specs=[pl.BlockSpec((3, tm, BW), lambda i: (0, i, 0)), pl.BlockSpec((3, BW, D_MODEL), lambda i: (0, 0, 0))] + gates,
        out_specs=[bspec, row], out_shape=[jax.ShapeDtypeStruct((3, S, D_MODEL), BF16), jax.ShapeDtypeStruct((S, D_MODEL), BF16)],
        compiler_params=_cp("parallel"),
    )(br, w_br, p, p, p)


def _merge_bwd(p, bd, dm):
    S = p.shape[0]
    tm = _tile(S, (256,))

    def body(g0, g1, g2, b_ref, dm_ref, db_ref, dg_ref):
        dmv = dm_ref[...]
        for n, g_ref in enumerate((g0, g1, g2)):
            sg = jax.nn.sigmoid(g_ref[...].astype(F32))
            db_ref[n] = (dmv * sg).astype(BF16)
            dg_ref[:, n * D_MODEL:(n + 1) * D_MODEL] = (dmv * b_ref[n].astype(F32) * (sg * (1.0 - sg))).astype(BF16)

    gates, bspec, row = _merge_specs(S, tm)
    return pl.pallas_call(
        body, name="merge_bwd", grid=(S // tm,), in_specs=gates + [bspec, row],
        out_specs=[bspec, pl.BlockSpec((tm, 3 * D_MODEL), lambda i: (i, 0))],
        out_shape=[jax.ShapeDtypeStruct((3, S, D_MODEL), BF16), jax.ShapeDtypeStruct((S, 3 * D_MODEL), BF16)],
        compiler_params=_cp("parallel"),
    )(p, p, p, bd, dm)


XA_TM = 512


def _xa_probs(qh, kh):
    s = lax.dot_general(qh, kh, NT, preferred_element_type=F32) * (XA_DH ** -0.5)
    e = jnp.exp(s - jnp.max(s, axis=-1, keepdims=True))
    return e / jnp.sum(e, axis=-1, keepdims=True)


def _xa_fwd(q, kv):
    S = q.shape[0]
    tm = min(XA_TM, S)
    M = kv.shape[1]

    def body(q_ref, kv_ref, o_ref):
        for h in range(XA_HEADS):
            cols = slice(h * XA_DH, (h + 1) * XA_DH)
            pr = _xa_probs(q_ref[:, cols], kv_ref[0, :, cols])
            o_ref[:, cols] = jnp.dot(pr.astype(BF16), kv_ref[1, :, cols], preferred_element_type=F32).astype(BF16)

    row = pl.BlockSpec((tm, D_MODEL), lambda i: (i, 0))
    return pl.pallas_call(
        body, name="xa_fwd", grid=(S // tm,), in_specs=[row, pl.BlockSpec((2, M, D_MODEL), lambda i: (0, 0, 0))], out_specs=row,
        out_shape=jax.ShapeDtypeStruct((S, D_MODEL), BF16), compiler_params=_cp("parallel"),
    )(q, kv)


def _xa_bwd(q, kv, do):
    S = q.shape[0]
    tm = min(XA_TM, S)
    M = kv.shape[1]

    def body(q_ref, kv_ref, do_ref, dq_ref, dkv_ref):
        @pl.when(pl.program_id(0) == 0)
        def _():
            dkv_ref[...] = jnp.zeros_like(dkv_ref)

        for h in range(XA_HEADS):
            cols = slice(h * XA_DH, (h + 1) * XA_DH)
            qh, kh, vh, doh = q_ref[:, cols], kv_ref[0, :, cols], kv_ref[1, :, cols], do_ref[:, cols]
            pr = _xa_probs(qh, kh)
            dkv_ref[1, :, cols] += lax.dot_general(pr.astype(BF16), doh, TN, preferred_element_type=F32)
            dp = lax.dot_general(doh, vh, NT, preferred_element_type=F32)
            ds = (pr * (dp - jnp.sum(dp * pr, axis=-1, keepdims=True)) * (XA_DH ** -0.5)).astype(BF16)
            dq_ref[:, cols] = jnp.dot(ds, kh, preferred_element_type=F32).astype(BF16)
            dkv_ref[0, :, cols] += lax.dot_general(ds, qh, TN, preferred_element_type=F32)

    row = pl.BlockSpec((tm, D_MODEL), lambda i: (i, 0))
    kvs = pl.BlockSpec((2, M, D_MODEL), lambda i: (0, 0, 0))
    return pl.pallas_call(
        body, name="xa_bwd", grid=(S // tm,), in_specs=[row, kvs, row], out_specs=[row, kvs],
        out_shape=[jax.ShapeDtypeStruct((S, D_MODEL), BF16), jax.ShapeDtypeStruct((2, M, D_MODEL), F32)],
        compiler_params=_cp("arbitrary"),
    )(q, kv, do)


def _ffn_up(h3, w_gu, rider=None):
    S = h3.shape[0]
    tm = _tile(S, (1024, 512, 256))

    def body(h_ref, w_ref, ab_ref, hh_ref):
        h = h_ref[...]
        a = lax.dot_general(h, w_ref[0], NT, preferred_element_type=F32)
        b = lax.dot_general(h, w_ref[1], NT, preferred_element_type=F32)
        ab_ref[0] = a.astype(BF16)
        ab_ref[1] = b.astype(BF16)
        hh_ref[...] = (a * jax.nn.sigmoid(a) * b).astype(BF16)

    pair = pl.BlockSpec((None, 2, tm, FFN_SH), lambda j, i: (j, 0, i, 0))
    outs = _call_with_rider(
        rider, body, name="mm_gu", grid=(N_CHIPS, S // tm), args=[h3, w_gu],
        in_specs=[pl.BlockSpec((tm, D_MODEL), lambda j, i: (i, 0)), pl.BlockSpec((None, 2, FFN_SH, D_MODEL), lambda j, i: (j, 0, 0, 0))],
        out_specs=[pair, pl.BlockSpec((None, tm, FFN_SH), lambda j, i: (j, i, 0))],
        out_shape=[jax.ShapeDtypeStruct((N_CHIPS, 2, S, FFN_SH), BF16), jax.ShapeDtypeStruct((N_CHIPS, S, FFN_SH), BF16)],
        scratch_shapes=[], semantics=("parallel", "parallel"))
    return outs[0], outs[1], outs[2:]


def _ffn_down_bwd(dx3, w_dn, ab):
    S = dx3.shape[0]
    tm = _tile(S, (1024, 512, 256))

    def body(dx_ref, w_ref, ab_ref, o_ref):
        d = lax.dot_general(dx_ref[...].astype(BF16), w_ref[...], NT, preferred_element_type=F32)
        a, b = ab_ref[0].astype(F32), ab_ref[1].astype(F32)
        sg = jax.nn.sigmoid(a)
        o_ref[0] = (d * b * (sg * (1.0 + a * (1.0 - sg)))).astype(BF16)
        o_ref[1] = (d * (a * sg)).astype(BF16)

    pair = pl.BlockSpec((None, 2, tm, FFN_SH), lambda j, i: (j, 0, i, 0))
    return pl.pallas_call(
        body, name="mm_down_dx", grid=(N_CHIPS, S // tm),
        in_specs=[pl.BlockSpec((tm, D_MODEL), lambda j, i: (i, 0)), pl.BlockSpec((None, FFN_SH, D_MODEL), lambda j, i: (j, 0, 0)), pair],
        out_specs=pair, out_shape=jax.ShapeDtypeStruct(ab.shape, BF16), compiler_params=_cp("parallel", "parallel"),
    )(dx3, w_dn, ab)


def _reduce_adam(parts, w, m, v, *, name):
    shape = w.shape
    C = shape[-1]
    R = math.prod(shape[:-1])
    tm = _rows(R, 4 * C)
    n = len(parts)
    c1, c2 = 1.0 - ADAM_B1 ** ADAM_STEP, 1.0 - ADAM_B2 ** ADAM_STEP

    def body(*refs):
        g = refs[0][...]
        for r in refs[1:n]:
            g = g + r[...]
        w_ref, m_ref, v_ref, go, do, mo, vo = refs[n:]
        mn = ADAM_B1 * m_ref[...] + (1.0 - ADAM_B1) * g
        vn = ADAM_B2 * v_ref[...] + (1.0 - ADAM_B2) * (g * g)
        go[...] = g
        do[...] = -ADAM_LR * ((mn / c1) / (jnp.sqrt(vn / c2) + ADAM_EPS) + ADAM_WD * w_ref[...])
        mo[...] = mn
        vo[...] = vn

    row = pl.BlockSpec((tm, C), lambda i: (i, 0))
    outs = pl.pallas_call(
        body, name=name, grid=(R // tm,), in_specs=[row] * (n + 3), out_specs=[row] * 4,
        out_shape=[jax.ShapeDtypeStruct((R, C), F32)] * 4, compiler_params=_cp("parallel"),
    )(*[a.reshape(R, C) for a in (*parts, w, m, v)])
    return tuple(o.reshape(shape) for o in outs)


_VIEW = {
    "w_in": ((1024, 7168), (1024, 1792), 256, lambda i, b: (i, b)),
    "w_br": ((1536, 1024), (1536, 256), 512, lambda i, b: (i, b)),
    "w_sq": ((5120, 1024), (1280, 1024), 256, lambda i, b: (4 * i + b, 0)),
    "w_gu": ((5632, 1024), (1408, 1024), 352, lambda i, b: (4 * b + i, 0)),
    "w_dn": ((2816, 1024), (704, 1024), 352, lambda i, b: (2 * b + i, 0)),
    "conv_w": ((8, 512), (8, 128), 8, lambda i, b: (0, b)),
}


def _scalar(v):
    return jnp.asarray(v, jnp.int32).reshape(1)


def _place(name, local, b):
    full2, sh2, tm, idx = _VIEW[name]
    C = sh2[1]
    dt = local.dtype if name == "conv_w" else BF16

    def body(b_ref, x_ref, o0_ref, o1_ref):
        o0_ref[...] = x_ref[0].astype(dt)
        o1_ref[...] = x_ref[1].astype(dt)

    place = pl.BlockSpec((tm, C), lambda i, bs: idx(i, bs[0]))
    outs = pl.pallas_call(
        body, name="place_" + name,
        grid_spec=pltpu.PrefetchScalarGridSpec(num_scalar_prefetch=1, grid=(sh2[0] // tm,),
                                               in_specs=[pl.BlockSpec((DEPTH, tm, C), lambda i, bs: (0, i, 0))], out_specs=[place, place]),
        out_shape=[jax.ShapeDtypeStruct(full2, dt)] * 2, compiler_params=_cp("arbitrary"),
    )(_scalar(b), local.reshape((DEPTH,) + sh2))
    return [o.reshape(_FULL_SHAPE[name]) for o in outs]


def _add_owner(name, g, land, own):
    shape = g.shape
    C = shape[-1]
    R = math.prod(shape[:-1])
    tm = _rows(R, 4 * C)

    def body(s_ref, g_ref, l_ref, o_ref):
        @pl.when(s_ref[0] != 0)
        def _():
            o_ref[...] = (g_ref[...].astype(F32) + l_ref[...].astype(F32)).astype(BF16)

        @pl.when(s_ref[0] == 0)
        def _():
            o_ref[...] = jnp.zeros_like(o_ref)

    pick = pl.BlockSpec((tm, C), lambda i, s: (jnp.where(s[0] != 0, i, 0), 0))
    return pl.pallas_call(
        body, name="presum_" + name,
        grid_spec=pltpu.PrefetchScalarGridSpec(num_scalar_prefetch=1, grid=(R // tm,), in_specs=[pick, pick],
                                               out_specs=pl.BlockSpec((tm, C), lambda i, s: (i, 0))),
        out_shape=jax.ShapeDtypeStruct((R, C), BF16), compiler_params=_cp("arbitrary"),
    )(_scalar(own), g.reshape(R, C), land.reshape(R, C)).reshape(shape)


def _sum_chips(name, slots, part, b, own):
    full2, sh2, tm, idx = _VIEW[name]
    C = sh2[1]

    def body(s_ref, slot_ref, own_ref, o_ref):
        @pl.when(s_ref[1] != 0)
        def _():
            o_ref[...] = ((slot_ref[0].astype(F32) + slot_ref[1].astype(F32)) + slot_ref[2].astype(F32)) + own_ref[...].astype(F32)

        @pl.when(s_ref[1] == 0)
        def _():
            o_ref[...] = jnp.zeros_like(o_ref)

    return pl.pallas_call(
        body, name="sum_chips_" + name,
        grid_spec=pltpu.PrefetchScalarGridSpec(
            num_scalar_prefetch=1, grid=(sh2[0] // tm,),
            in_specs=[pl.BlockSpec((3, tm, C), lambda i, s: (0, jnp.where(s[1] != 0, i, 0), 0)),
                      pl.BlockSpec((tm, C), lambda i, s: idx(jnp.where(s[1] != 0, i, 0), s[0]))],
            out_specs=pl.BlockSpec((tm, C), lambda i, s: (i, 0))),
        out_shape=jax.ShapeDtypeStruct(sh2, F32), compiler_params=_cp("arbitrary"),
    )(jnp.stack([jnp.asarray(b, jnp.int32), jnp.asarray(own, jnp.int32)]), slots.reshape((3,) + sh2),
      part.reshape(full2)).reshape(_SHARD_SHAPE[name])


def _adam_layers(mine, theirs, c, w, m, v, *, name):
    shape = w.shape
    C = shape[-1]
    R = math.prod(shape[1:-1])
    tm = _rows(R, 4 * C)
    c1, c2 = 1.0 - ADAM_B1 ** ADAM_STEP, 1.0 - ADAM_B2 ** ADAM_STEP

    def body(c_ref, m0_ref, m1_ref, t_ref, w_ref, m_ref, v_ref, go, do, mo, vo):
        layer = pl.program_id(0)
        g = jnp.where(layer == c_ref[0], jnp.where(layer == 0, m0_ref[...], m1_ref[...]), t_ref[...])
        mn = ADAM_B1 * m_ref[...] + (1.0 - ADAM_B1) * g
        vn = ADAM_B2 * v_ref[...] + (1.0 - ADAM_B2) * (g * g)
        go[...] = g
        do[...] = -ADAM_LR * ((mn / c1) / (jnp.sqrt(vn / c2) + ADAM_EPS) + ADAM_WD * w_ref[...])
        mo[...] = mn
        vo[...] = vn

    def own(layer):
        return pl.BlockSpec((tm, C), lambda l, i, cs: (jnp.where((l == layer) & (cs[0] == layer), i, 0), 0))

    recv = pl.BlockSpec((tm, C), lambda l, i, cs: (jnp.where(l == cs[0], 0, i), 0))
    row = pl.BlockSpec((None, tm, C), lambda l, i, cs: (l, i, 0))
    outs = pl.pallas_call(
        body, name=name,
        grid_spec=pltpu.PrefetchScalarGridSpec(num_scalar_prefetch=1, grid=(DEPTH, R // tm),
                                               in_specs=[own(0), own(1), recv, row, row, row], out_specs=[row] * 4),
        out_shape=[jax.ShapeDtypeStruct((DEPTH, R, C), F32)] * 4, compiler_params=_cp("arbitrary", "arbitrary"),
    )(_scalar(c), mine[0].reshape(R, C), mine[1].reshape(R, C), theirs.reshape(R, C), *[t.reshape(DEPTH, R, C) for t in (w, m, v)])
    return tuple(o.reshape(shape) for o in outs)


def _take_weights(wl, names, landed):
    for n, t in zip(names, landed):
        wl[n] = t[:3] if n == "conv_w" else t


_GATHER_LATE = ["w_br", "conv_w", "w_sq"]
_GATHER_LAST = ["w_gu", "w_dn"]


def _layer_fwd(x, mem, wl, ride=None, h1=None, next_gain=None):
    S = x.shape[0]
    sv = {"x": x}
    wl = dict(wl)
    if h1 is None:
        h1 = _rms_fwd(x, wl["norm_mix_g"], name="rms_mix")
    if ride is None:
        ride = {"mm_in": None, "sb": None, "mm_gu": None}
        p = _mm(h1, wl["w_in"], mode="nn", out_dtype=BF16, name="mm_in")
    else:
        p, landed = _mm(h1, wl["w_in"], mode="nn", out_dtype=BF16, name="mm_in", rider=ride["mm_in"])
        _take_weights(wl, _GATHER_LATE, landed)
    (ya, tot, cnt), landed = _sb2_fwd(p, ride["sb"])
    _take_weights(wl, _GATHER_LAST, landed)
    b_st = wl["b_spatial"].T
    yb = _sgu_fwd(p, wl["sgu_ln_g"], wl["sgu_ln_b"], wl["w_spatial"], b_st)
    yc = _conv_fwd(p, wl["conv_w"])
    br = jnp.stack([ya, yb, yc])
    bd, merged = _branch_merge(br, wl["w_br"], p)
    x1, h2 = _mm(merged, wl["w_sq"][0], mode="nn", res=x, norm=wl["norm_xa_g"], name="mm_out")
    qx = _mm(h2, wl["w_sq"][1], mode="nn", out_dtype=BF16, name="mm_q")
    mn = _rms_fwd(mem, wl["mem_norm_g"], name="rms_mem")
    kv = _mm(mn, wl["w_sq"][3:5], mode="nn", b_kind="batch", out_dtype=BF16, name="mm_kv")
    o = _xa_fwd(qx, kv)
    x2, h3 = _mm(o, wl["w_sq"][2], mode="nn", res=x1, norm=wl["norm_ffn_g"], name="mm_o")
    ab, hh, rode = _ffn_up(h3, wl["w_gu"], ride["mm_gu"])
    x3 = _mm(hh, wl["w_dn"], mode="nn", a_kind="kchunk", b_kind="kchunk", res=x2, norm=next_gain, name="mm_down")
    x3, h_next = x3 if next_gain is not None else (x3, None)
    sv.update(h1=h1, p=p, tot=tot, cnt=cnt, br=br, bd=bd, merged=merged, x1=x1, h2=h2, qx=qx, mn=mn, kv=kv, o=o,
              x2=x2, h3=h3, ab=ab, hh=hh, b_st=b_st, wl=wl)
    return x3, sv, rode, h_next


class _GradPipe:
    def __init__(self, ci, bi):
        self.ci, self.bi, self.queue = ci, bi, []
        self.part, self.slots = [dict(), dict()], [dict(), dict()]

    def to_owner(self, layer, names, g):
        def arrived(land):
            own = (self.ci == layer).astype(jnp.int32)
            part = {n: _add_owner(n, g[n], t, own) for n, t in zip(names, land)}
            self.part[layer].update(part)
            self.queue.append((layer, names, part))

        return _presum_rider(layer, {n: g[n] for n in names}), arrived

    def exchange(self):
        if not self.queue:
            return None
        layer, names, part = self.queue.pop(0)
        return _shard_rider(layer, part), lambda slots: self.slots[layer].update(zip(names, slots))

    def drain(self, also):
        landed = None
        while self.queue or landed is None:
            job = self.exchange()
            both = _join([job[0] if job else None, also if landed is None else None])
            parts = both.split(_run_rider(both, name="grad_exchange_last"))
            if job:
                job[1](parts[0])
            if landed is None:
                landed = parts[-1]
        return landed

    def reduced(self, layer):
        own = (self.ci == layer).astype(jnp.int32)
        return {n: _sum_chips(n, self.slots[layer][n], self.part[layer][n], self.bi, own) for n in _BIG}


def _layer_bwd(dx3, mem, sv, layer=None, pipe=None):
    S = dx3.shape[0]
    p, wl = sv["p"], sv["wl"]
    g = {}

    def carrying(jobs, fn):
        jobs = [j for j in jobs if j]
        joined = _join([j[0] for j in jobs])
        out, landed = fn(joined)
        if joined is not None:
            for j, part in zip(jobs, joined.split(landed)):
                j[1](part)
        return out

    def mm(*args, job=None, jobs=(), **kw):
        return carrying([job, *jobs], lambda r: (_mm(*args, **kw), None) if r is None else _mm(*args, **kw, rider=r))

    to_owner = (lambda names: pipe.to_owner(layer, names, g)) if pipe else (lambda names: None)
    exchange = pipe.exchange if pipe else (lambda: None)

    g["w_dn"] = _mm(sv["hh"], dx3, mode="tn", a_kind="batch", out_dtype=BF16, name="mm_down_dw")
    dab = _ffn_down_bwd(dx3, wl["w_dn"], sv["ab"]).reshape(2 * N_CHIPS, S, FFN_SH)
    g["w_gu"] = mm(dab, sv["h3"], mode="tn", a_kind="batch", out_dtype=BF16, name="mm_gu_dw", job=exchange()).reshape(_FULL_SHAPE["w_gu"])
    dx2, g["norm_ffn_g"] = mm(dab, wl["w_gu"].reshape(2 * N_CHIPS, FFN_SH, D_MODEL), mode="nn", a_kind="kchunk", b_kind="kchunk", name="mm_gu_dx",
                               rms_bwd=(sv["x2"], wl["norm_ffn_g"], dx3), job=to_owner(["w_dn", "w_gu"]))
    do = _mm(dx2, wl["w_sq"][2], mode="nt", out_dtype=BF16, name="mm_o_dx")
    dw_o = _mm(sv["o"], dx2, mode="tn", out_dtype=BF16, name="mm_o_dw")
    dq, dkv = _xa_bwd(sv["qx"], sv["kv"], do)
    dw_q = _mm(sv["h2"], dq, mode="tn", out_dtype=BF16, name="mm_q_dw")
    dx1, g["norm_xa_g"] = _mm(dq, wl["w_sq"][1], mode="nt", name="mm_q_dx", rms_bwd=(sv["x1"], wl["norm_xa_g"], dx2))
    dw_kv = _mm(sv["mn"], dkv, mode="tn", b_kind="batch", out_dtype=BF16, name="mm_kv_dw")
    dmn = _mm(dkv, wl["w_sq"][3:5], mode="nt", a_kind="kchunk", b_kind="kchunk", name="mm_kv_dx")
    _, g["mem_norm_g"] = _rms_bwd(mem, wl["mem_norm_g"], dmn, jnp.zeros_like(mem), name="rms_mem_bwd")
    dm = _mm(dx1, wl["w_sq"][0], mode="nt", name="mm_out_dx")
    dw_out = _mm(sv["merged"], dx1, mode="tn", out_dtype=BF16, name="mm_out_dw")
    g["w_sq"] = jnp.concatenate([jnp.stack([dw_out, dw_q, dw_o]), dw_kv])
    dbd, dgates = _merge_bwd(p, sv["bd"], dm)
    dbr = mm(dbd, wl["w_br"], mode="nt", a_kind="batch", b_kind="batch", name="mm_branch_dx", job=to_owner(["w_sq"]))
    g["w_br"] = _mm(sv["br"], dbd, mode="tn", a_kind="batch", b_kind="batch", out_dtype=BF16, name="mm_branch_dw")
    dq, dk, dv = carrying([exchange(), to_owner(["w_br"])], lambda r: _sb2_bwd(p, dbr, sv["tot"], sv["cnt"], r))
    dz, g["sgu_ln_g"], g["sgu_ln_b"], g["w_spatial"], g["b_spatial"] = _sgu_bwd(
        p, dbr[1], wl["sgu_ln_g"], wl["sgu_ln_b"], wl["w_spatial"], wl["w_spatial"].transpose(0, 2, 1), sv["b_st"])
    dcb, dcc, dcx, g["conv_w"] = _conv_bwd(p, wl["conv_w"], dbr[2])
    dp = jnp.concatenate([dq, dk.astype(BF16), dv.astype(BF16), dz, dcb, dcc, dcx, dgates], axis=1)
    g["w_in"] = mm(sv["h1"], dp, mode="tn", out_dtype=BF16, name="mm_in_dw", jobs=[exchange(), exchange()])
    dx, g["norm_mix_g"] = mm(dp, wl["w_in"], mode="nt", name="mm_in_dx", rms_bwd=(sv["x"], wl["norm_mix_g"], dx1),
                             job=to_owner(["w_in"]))
    return dx, g


def _local_step(x, mem, target, layers, final_g):
    h, saved = x, []
    for wl in layers:
        h, sv, _, _ = _layer_fwd(h, mem, wl)
        saved.append(sv)
    loss, dx, d_final = _loss_head(h, final_g, target)
    grads = [None] * len(layers)
    for l in reversed(range(len(layers))):
        dx, grads[l] = _layer_bwd(dx, mem, saved[l])
    return loss, dx, grads, d_final


_ALL = slice(None)
CONV_ROWS = 8
_SHARD = {
    "w_in": lambda b: (_ALL, pl.ds(1792 * b, 1792)),
    "w_br": lambda b: (_ALL, _ALL, pl.ds(256 * b, 256)),
    "w_sq": lambda b: (_ALL, pl.ds(256 * b, 256), _ALL),
    "w_gu": lambda b: (b,),
    "w_dn": lambda b: (b,),
    "conv_w": lambda b: (_ALL, pl.ds(128 * b, 128)),
}
_FULL_SHAPE = {"w_in": (1024, 7168), "w_br": (3, 512, 1024), "w_sq": (5, 1024, 1024), "w_gu": (4, 2, 704, 1024),
               "w_dn": (4, 704, 1024), "conv_w": (CONV_ROWS, 512)}
_SHARD_SHAPE = {"w_in": (1024, 1792), "w_br": (3, 512, 256), "w_sq": (5, 256, 1024), "w_gu": (2, 704, 1024),
                "w_dn": (704, 1024), "conv_w": (CONV_ROWS, 128)}


def _pos():
    return lax.axis_index("x"), lax.axis_index("y"), lax.axis_index("c")


def _per_chip(fn):
    x, y, _ = _pos()
    for x0 in (0, 1):
        for y0 in (0, 1):
            @pl.when((x == x0) & (y == y0))
            def _():
                fn(x0, y0)


def _other_chips(x0, y0):
    return [(1 - x0, y0), (x0, 1 - y0), (1 - x0, 1 - y0)]


def _rcopy(src, dst, ssem, rsem, dev):
    return pltpu.make_async_remote_copy(src_ref=src, dst_ref=dst, send_sem=ssem, recv_sem=rsem, device_id=dev, device_id_type=MESH)


def _dma_sems(n):
    return pltpu.SemaphoreType.DMA((n,))


def _gather_rider(layer, placed):
    names = list(placed)
    n = len(names)
    shard = lambda refs, a, b: refs[a].at[_SHARD[names[a]](b)]

    def start(ins, outs, send, recv):
        @pl.when(lax.axis_index("c") == layer)
        def _():
            def run(x0, y0):
                for kk, (px, py) in enumerate(_other_chips(x0, y0)):
                    for a in range(n):
                        own = shard(outs, a, 2 * x0 + y0)
                        _rcopy(own, own, send.at[6 * a + kk], recv.at[6 * a + kk], (px, py, layer)).start()

            _per_chip(run)

    def passing(outs, send, recv, a, kk, bp, x0, y0):
        landed = shard(outs, a, bp)
        return _rcopy(landed, landed, send.at[6 * a + 3 + kk], recv.at[6 * a + 3 + kk], (x0, y0, 1 - layer))

    def middle(ins, outs, send, recv):
        @pl.when(lax.axis_index("c") == layer)
        def _():
            def run(x0, y0):
                for kk, (px, py) in enumerate(_other_chips(x0, y0)):
                    for a in range(n):
                        landed = shard(outs, a, 2 * px + py)
                        _rcopy(landed, landed, send.at[6 * a + kk], recv.at[6 * a + kk], (px, py, layer)).wait_recv()
                        passing(outs, send, recv, a, kk, 2 * px + py, x0, y0).start()

            _per_chip(run)

    def finish(ins, outs, send, recv):
        c = lax.axis_index("c")

        def run(x0, y0):
            chips = _other_chips(x0, y0)

            @pl.when(c == layer)
            def _():
                for kk, (px, py) in enumerate(chips):
                    for a in range(n):
                        own = shard(outs, a, 2 * x0 + y0)
                        _rcopy(own, own, send.at[6 * a + kk], recv.at[6 * a + kk], (px, py, layer)).wait_send()
                        passing(outs, send, recv, a, kk, 2 * px + py, x0, y0).wait_send()

            @pl.when(c != layer)
            def _():
                for kk, (px, py) in enumerate(chips):
                    for a in range(n):
                        got = shard(outs, a, 2 * px + py)
                        _rcopy(got, got, send.at[6 * a + 3 + kk], recv.at[6 * a + 3 + kk], (x0, y0, layer)).wait_recv()

        _per_chip(run)

    arrs = [placed[nm] for nm in names]
    return _Rider(arrs, [jax.ShapeDtypeStruct(t.shape, t.dtype) for t in arrs], 6 * n, start, finish, alias={a: a for a in range(n)}, middle=middle)


def _presum_rider(layer, grads):
    names = list(grads)
    n = len(names)

    def start(ins, outs, send, recv):
        x, y, c = _pos()

        @pl.when(c != layer)
        def _():
            for a in range(n):
                _rcopy(ins[a], outs[a], send.at[a], recv.at[a], (x, y, layer)).start()

    def finish(ins, outs, send, recv):
        x, y, c = _pos()

        @pl.when(c != layer)
        def _():
            for a in range(n):
                _rcopy(ins[a], outs[a], send.at[a], recv.at[a], (x, y, layer)).wait_send()

        @pl.when(c == layer)
        def _():
            for a in range(n):
                _rcopy(outs[a], outs[a], send.at[a], recv.at[a], (x, y, 1 - layer)).wait_recv()

    arrs = [grads[nm] for nm in names]
    return _Rider(arrs, [jax.ShapeDtypeStruct(t.shape, t.dtype) for t in arrs], n, start, finish)


def _shard_rider(layer, part):
    names = list(part)
    n = len(names)

    def each(fn):
        @pl.when(lax.axis_index("c") == layer)
        def _():
            def run(x0, y0):
                for kk, (px, py) in enumerate(_other_chips(x0, y0)):
                    for a in range(n):
                        fn(a, kk, 2 * px + py, (px, py, layer))

            _per_chip(run)

    def start(ins, outs, send, recv):
        each(lambda a, kk, bp, peer: _rcopy(ins[a].at[_SHARD[names[a]](bp)], outs[a].at[kk], send.at[3 * a + kk], recv.at[3 * a + kk], peer).start())

    def finish(ins, outs, send, recv):
        each(lambda a, kk, bp, peer: _rcopy(outs[a].at[kk], outs[a].at[kk], send.at[3 * a + kk], recv.at[3 * a + kk], peer).wait_recv())
        each(lambda a, kk, bp, peer: _rcopy(ins[a].at[_SHARD[names[a]](bp)], outs[a].at[kk], send.at[3 * a + kk], recv.at[3 * a + kk], peer).wait_send())

    return _Rider([part[nm] for nm in names], [jax.ShapeDtypeStruct((N_CHIPS - 1,) + _SHARD_SHAPE[nm], part[nm].dtype) for nm in names],
                  3 * n, start, finish)


def _sibling_exchange(mine0, mine1):
    names = list(mine0)
    n = len(names)

    def body(*refs):
        l0, l1, outs = refs[:n], refs[n:2 * n], refs[2 * n:3 * n]
        send, recv = refs[3 * n:]
        x, y, c = _pos()
        for c0 in (0, 1):
            @pl.when(c == c0)
            def _():
                srcs = l0 if c0 == 0 else l1
                cps = [_rcopy(srcs[a], outs[a], send.at[a], recv.at[a], (x, y, 1 - c0)) for a in range(n)]
                for cp in cps:
                    cp.start()
                for cp in cps:
                    cp.wait()

    outs = pl.pallas_call(
        body, name="grad_sibling_exchange", in_specs=[ANY] * (2 * n), out_specs=[ANY] * n,
        out_shape=[jax.ShapeDtypeStruct(mine0[nm].shape, mine0[nm].dtype) for nm in names],
        scratch_shapes=[_dma_sems(n), _dma_sems(n)],
    )(*[mine0[nm] for nm in names], *[mine1[nm] for nm in names])
    return dict(zip(names, outs))


def _small_rider(pack):
    flips = [(fx, fy, fc) for fx in (0, 1) for fy in (0, 1) for fc in (0, 1) if fx or fy or fc]

    def peers():
        x, y, c = _pos()
        return 4 * x + 2 * y + c, [(x ^ fx, y ^ fy, c ^ fc) for fx, fy, fc in flips]

    def start(ins, outs, send, recv):
        me, to = peers()
        for k, peer in enumerate(to):
            _rcopy(ins[0], outs[0].at[me], send.at[k], recv.at[k], peer).start()

    def finish(ins, outs, send, recv):
        me, to = peers()
        for k, (px, py, pc) in enumerate(to):
            slot = outs[0].at[4 * px + 2 * py + pc]
            _rcopy(slot, slot, send.at[k], recv.at[k], (px, py, pc)).wait_recv()
        for k, peer in enumerate(to):
            _rcopy(ins[0], outs[0].at[me], send.at[k], recv.at[k], peer).wait_send()

    return _Rider([pack], [jax.ShapeDtypeStruct((8,) + pack.shape, pack.dtype)], len(flips), start, finish)


def _sum_devices(gathered, pack, me):
    n, R, C = gathered.shape

    def body(me_ref, r_ref, own_ref, o_ref):
        acc = jnp.where(me_ref[0] == 0, own_ref[...], r_ref[0])
        for s in range(1, n):
            acc = acc + jnp.where(me_ref[0] == s, own_ref[...], r_ref[s])
        o_ref[...] = acc

    return pl.pallas_call(
        body, name="sum_devices_small",
        grid_spec=pltpu.PrefetchScalarGridSpec(num_scalar_prefetch=1, grid=(1,),
                                               in_specs=[pl.BlockSpec((n, R, C), lambda i, m: (0, 0, 0)), pl.BlockSpec((R, C), lambda i, m: (0, 0))],
                                               out_specs=pl.BlockSpec((R, C), lambda i, m: (0, 0))),
        out_shape=jax.ShapeDtypeStruct((R, C), F32), compiler_params=_cp("arbitrary"),
    )(_scalar(me), gathered, pack)


_WEIGHTS = ["norm_mix_g", "w_in", "sgu_ln_g", "sgu_ln_b", "w_spatial", "b_spatial", "conv_w", "w_branch", "w_out", "norm_xa_g",
            "mem_norm_g", "w_q_xa", "w_k_xa", "w_v_xa", "w_o_xa", "norm_ffn_g", "w_gate_ffn", "w_up_ffn", "w_down_ffn", "final_g"]
_REPLICATED = ["norm_mix_g", "sgu_ln_g", "sgu_ln_b", "w_spatial", "b_spatial", "norm_xa_g", "mem_norm_g", "norm_ffn_g", "final_g"]
_SQUARE = ["w_out", "w_q_xa", "w_o_xa", "w_k_xa", "w_v_xa"]
_BIG = ["w_in", "w_br", "w_sq", "w_gu", "w_dn"]


def _pack(arrs):
    return jnp.concatenate([a.reshape(-1) for a in arrs]).reshape(-1, 128)


def _step(a):
    w = {n: a[n] for n in _WEIGHTS}
    x, mem, target = a["x"][0], a["mem"][0], a["loss_target"][0]
    xi, yi, ci = _pos()
    bi = 2 * xi + yi
    groups = list(_FULL_SHAPE)

    tr = lambda t: jnp.swapaxes(t, 1, 2)
    local = {"w_in": w["w_in"], "w_br": w["w_branch"], "w_sq": jnp.stack([w[n] for n in _SQUARE], axis=1),
             "w_gu": jnp.stack([tr(w["w_gate_ffn"]), tr(w["w_up_ffn"])], axis=1), "w_dn": w["w_down_ffn"],
             "conv_w": jnp.pad(w["conv_w"], ((0, 0), (0, CONV_ROWS - 3), (0, 0)))}
    placed = [dict(), dict()]
    for n in groups:
        placed[0][n], placed[1][n] = _place(n, local[n], bi)

    def gather(l, names):
        return _gather_rider(l, {n: placed[l][n] for n in names})

    def start_of(l, w_in):
        return {"w_in": w_in, **{n: w[n][l] for n in _REPLICATED if n != "final_g"}}

    w_in0, = _run_rider(gather(0, ["w_in"]), name="gather_first")
    h, sv0, (w_in1,), h1 = _layer_fwd(x, mem, start_of(0, w_in0), {"mm_in": gather(0, _GATHER_LATE), "sb": gather(0, _GATHER_LAST),
                                                                    "mm_gu": gather(1, ["w_in"])}, next_gain=w["norm_mix_g"][1])
    h, sv1, _, _ = _layer_fwd(h, mem, start_of(1, w_in1), {"mm_in": gather(1, _GATHER_LATE), "sb": gather(1, _GATHER_LAST), "mm_gu": None}, h1=h1)
    loss, dx, d_final = _loss_head(h, w["final_g"], target)
    loss = lax.psum(loss, ("x", "y", "c"))

    pipe = _GradPipe(ci, bi)
    dx, g1 = _layer_bwd(dx, mem, sv1, 1, pipe)
    dx, g0 = _layer_bwd(dx, mem, sv0, 0, pipe)
    grads = [g0, g1]
    small = {n: jnp.stack([g[n] for g in grads]) for n in _REPLICATED if n != "final_g"}
    small["final_g"] = d_final
    conv_g = jnp.stack([g["conv_w"] for g in grads])
    small_pack = _pack([small[n] for n in _REPLICATED] + [conv_g])
    gathered, = pipe.drain(_small_rider(small_pack))
    mine = [pipe.reduced(0), pipe.reduced(1)]
    theirs = _sibling_exchange(mine[0], mine[1])

    out = {}

    def adam_layers(name, group, pick=None, view=lambda t: t):
        sel = (lambda t: t[group]) if pick is None else (lambda t: t[group][pick])
        res = _adam_layers([sel(mine[0]), sel(mine[1])], sel(theirs), ci, view(w[name]), view(a["m_" + name]), view(a["v_" + name]), name="adam_" + name)
        out[name] = tuple(view(r) for r in res)

    adam_layers("w_in", "w_in")
    adam_layers("w_branch", "w_br")
    for t, n in enumerate(_SQUARE):
        adam_layers(n, "w_sq", t)
    adam_layers("w_gate_ffn", "w_gu", 0, tr)
    adam_layers("w_up_ffn", "w_gu", 1, tr)
    adam_layers("w_down_ffn", "w_dn")

    def adam(name, g):
        out[name] = _reduce_adam([g], w[name], a["m_" + name], a["v_" + name], name="adam_" + name)

    n_rep = sum(w[n].size for n in _REPLICATED) // 128
    summed = _sum_devices(gathered, small_pack, 4 * xi + 2 * yi + ci)
    res = _reduce_adam([summed[:n_rep]], _pack([w[n] for n in _REPLICATED]), _pack([a["m_" + n] for n in _REPLICATED]),
                       _pack([a["v_" + n] for n in _REPLICATED]), name="adam_replicated")
    off = 0
    for n in _REPLICATED:
        out[n] = tuple(r.reshape(-1)[off:off + w[n].size].reshape(w[n].shape) for r in res)
        off += w[n].size
    conv_full = summed[n_rep:].reshape(conv_g.shape)
    adam("conv_w", lax.dynamic_slice_in_dim(conv_full, (2 * xi + yi) * 128, 128, axis=2))

    return (loss, dx[None], *[out[n][k] for k in range(4) for n in _WEIGHTS])


def kernel(x, mem, norm_mix_g, w_in, sgu_ln_g, sgu_ln_b, w_spatial, b_spatial, conv_w, w_branch, w_out, norm_xa_g, mem_norm_g, w_q_xa, w_k_xa, w_v_xa, w_o_xa, norm_ffn_g, w_gate_ffn, w_up_ffn, w_down_ffn, final_g, loss_target, m_norm_mix_g, m_w_in, m_sgu_ln_g, m_sgu_ln_b, m_w_spatial, m_b_spatial, m_conv_w, m_w_branch, m_w_out, m_norm_xa_g, m_mem_norm_g, m_w_q_xa, m_w_k_xa, m_w_v_xa, m_w_o_xa, m_norm_ffn_g, m_w_gate_ffn, m_w_up_ffn, m_w_down_ffn, m_final_g, v_norm_mix_g, v_w_in, v_sgu_ln_g, v_sgu_ln_b, v_w_spatial, v_b_spatial, v_conv_w, v_w_branch, v_w_out, v_norm_xa_g, v_mem_norm_g, v_w_q_xa, v_w_k_xa, v_w_v_xa, v_w_o_xa, v_norm_ffn_g, v_w_gate_ffn, v_w_up_ffn, v_w_down_ffn, v_final_g):
    return _step(dict(locals()))
```
